```python
import math
import jax, jax.numpy as jnp
from jax import lax
import numpy as np

D_MODEL = 1024
BATCH = 8
SEQ = 4096
DEPTH = 2

POOL_WINDOWS = (2, 4, 8, 16)
N_POOL_GROUPS = len(POOL_WINDOWS)
POOL_GROUP_DIM = D_MODEL // 8
POOL_DIM = N_POOL_GROUPS * POOL_GROUP_DIM

N_HEADS = D_MODEL // 128
QK_NOPE = 64
QK_ROPE = 32
V_DIM = 64
Q_LORA = 384
KV_LORA = 256
ROPE_THETA = 10000.0
ATTN_DIM = N_HEADS * V_DIM
Q_BLOCK = 128

D_FF = 4 * D_MODEL

N_MOD = 6
EPS = 1e-6

IN_SPLITS = (POOL_DIM, Q_LORA, KV_LORA, QK_ROPE, D_MODEL, D_MODEL)
D_IN = sum(IN_SPLITS)

kernel_name = "hybrid_pool_mla_gated_adaln"


def rms_norm(x, g):
    xf = x.astype(jnp.float32)
    y = xf * lax.rsqrt(jnp.mean(xf * xf, axis=-1, keepdims=True) + EPS)
    return y.astype(x.dtype) * g


def apply_rope(x, cos, sin):
    half = x.shape[-1] // 2
    x1, x2 = x[..., :half], x[..., half:]
    return jnp.concatenate([x1 * cos - x2 * sin, x2 * cos + x1 * sin], axis=-1)


def pool_mixer(u, w_pool, pool_scale):
    B, S, _ = u.shape
    t = jnp.arange(S)
    outs = []
    for g, w in enumerate(POOL_WINDOWS):
        ug = u[..., g * POOL_GROUP_DIM:(g + 1) * POOL_GROUP_DIM].astype(jnp.float32)
        cs = jnp.cumsum(ug, axis=1)
        lag = jnp.pad(cs, ((0, 0), (w, 0), (0, 0)))[:, :S]
        cnt = jnp.minimum(t + 1, w).astype(jnp.float32)[None, :, None]
        outs.append(((cs - lag) / cnt - ug).astype(u.dtype))
    p = jnp.stack(outs, axis=2)
    y = jnp.einsum('bsgc,gcd->bsgd', p, w_pool).reshape(B, S, POOL_DIM)
    return y * pool_scale


def mla(c_q_raw, c_kv_raw, k_rope_raw, q_norm_g, w_uq, kv_norm_g, w_uk, w_uv, cos, sin):
    B, S, _ = c_q_raw.shape
    c_q = rms_norm(c_q_raw, q_norm_g)
    q = jnp.einsum('bsr,rhd->bshd', c_q, w_uq)
    q_nope = q[..., :QK_NOPE]
    q_rope = apply_rope(q[..., QK_NOPE:], cos[:, :, None, :], sin[:, :, None, :])
    c_kv = rms_norm(c_kv_raw, kv_norm_g)
    k_nope = jnp.einsum('bsr,rhd->bhsd', c_kv, w_uk)
    v = jnp.einsum('bsr,rhd->bhsd', c_kv, w_uv)
    k_rope = apply_rope(k_rope_raw, cos, sin)

    nb = S // Q_BLOCK
    def to_blocks(a):
        d = a.shape[-1]
        return a.reshape(B, nb, Q_BLOCK, N_HEADS, d).transpose(1, 0, 3, 2, 4)
    qn_b = to_blocks(q_nope)
    qr_b = to_blocks(q_rope)
    starts = jnp.arange(nb, dtype=jnp.int32) * Q_BLOCK
    key_pos = jnp.arange(S, dtype=jnp.int32)
    scale = 1.0 / math.sqrt(QK_NOPE + QK_ROPE)
    neg = jnp.finfo(jnp.float32).min

    def attend(args):
        qn, qr, start = args
        s = (jnp.einsum('bhqd,bhkd->bhqk', qn, k_nope)
             + jnp.einsum('bhqd,bkd->bhqk', qr, k_rope)).astype(jnp.float32) * scale
        q_pos = start + jnp.arange(Q_BLOCK, dtype=jnp.int32)
        mask = q_pos[:, None] >= key_pos[None, :]
        p = jax.nn.softmax(jnp.where(mask, s, neg), axis=-1)
        return jnp.einsum('bhqk,bhkd->bhqd', p.astype(v.dtype), v)

    o = lax.map(attend, (qn_b, qr_b, starts))
    return o.transpose(1, 0, 3, 2, 4).reshape(B, S, ATTN_DIM)


def _fwd_setup_inputs(seed: int = 0) -> dict:
    key = jax.random.key(seed)
    ks = jax.random.split(key, 24)
    f32 = jnp.float32

    def dense(k, shape, fan_in, mult=1.0):
        return jax.random.normal(k, shape, f32) * (mult * fan_in ** -0.5)

    def gain(k, shape):
        return 1.0 + 0.02 * jax.random.normal(k, shape, f32)

    x = jax.random.normal(ks[0], (BATCH, SEQ, D_MODEL), f32)
    c = jax.random.normal(ks[1], (BATCH, D_MODEL), f32)
    offsets = jax.random.randint(ks[2], (BATCH, 1), 0, 1024, dtype=jnp.int32)
    positions = offsets + jnp.arange(SEQ, dtype=jnp.int32)[None, :]
    return {
        "x": x,
        "c": c,
        "positions": positions,
        "ln1_g": gain(ks[3], (DEPTH, D_MODEL)),
        "ln2_g": gain(ks[4], (DEPTH, D_MODEL)),
        "w_ada": dense(ks[5], (DEPTH, D_MODEL, N_MOD * D_MODEL), D_MODEL, 0.5),
        "b_ada": 0.01 * jax.random.normal(ks[6], (DEPTH, N_MOD * D_MODEL), f32),
        "w_in": dense(ks[7], (DEPTH, D_MODEL, D_IN), D_MODEL),
        "q_norm_g": gain(ks[8], (DEPTH, Q_LORA)),
        "w_uq": dense(ks[9], (DEPTH, Q_LORA, N_HEADS, QK_NOPE + QK_ROPE), Q_LORA),
        "kv_norm_g": gain(ks[10], (DEPTH, KV_LORA)),
        "w_uk": dense(ks[11], (DEPTH, KV_LORA, N_HEADS, QK_NOPE), KV_LORA),
        "w_uv": dense(ks[12], (DEPTH, KV_LORA, N_HEADS, V_DIM), KV_LORA),
        "w_pool": dense(ks[13], (DEPTH, N_POOL_GROUPS, POOL_GROUP_DIM, POOL_GROUP_DIM), POOL_GROUP_DIM),
        "pool_scale": gain(ks[14], (DEPTH, POOL_DIM)),
        "p_pool": dense(ks[15], (DEPTH, POOL_DIM, D_MODEL), POOL_DIM),
        "p_attn": dense(ks[16], (DEPTH, ATTN_DIM, D_MODEL), ATTN_DIM),
        "w_out": dense(ks[17], (DEPTH, D_MODEL, D_MODEL), D_MODEL),
        "w_ff1": dense(ks[18], (DEPTH, D_MODEL, D_FF), D_MODEL),
        "w_ff2": dense(ks[19], (DEPTH, D_FF, D_MODEL), D_FF),
        "final_g": gain(ks[20], (D_MODEL,)),
    }


def _fwd_reference(x, c, positions, ln1_g, ln2_g, w_ada, b_ada, w_in, q_norm_g, w_uq,
              kv_norm_g, w_uk, w_uv, w_pool, pool_scale, p_pool, p_attn, w_out,
              w_ff1, w_ff2, final_g):
    inv_freq = ROPE_THETA ** (-jnp.arange(0, QK_ROPE, 2, dtype=jnp.float32) / QK_ROPE)
    ang = positions.astype(jnp.float32)[..., None] * inv_freq
    cos = jnp.cos(ang).astype(x.dtype)
    sin = jnp.sin(ang).astype(x.dtype)
    c_act = jax.nn.silu(c)
    cuts = np.cumsum(IN_SPLITS)[:-1].tolist()

    for l in range(DEPTH):
        mod = c_act @ w_ada[l] + b_ada[l]
        sh1, sc1, g1, sh2, sc2, g2 = [m[:, None, :] for m in jnp.split(mod, N_MOD, axis=-1)]

        h = rms_norm(x, ln1_g[l]) * (1.0 + sc1) + sh1
        z = h @ w_in[l]
        u_pool, c_q_raw, c_kv_raw, k_rope_raw, gz_a, gz_b = jnp.split(z, cuts, axis=-1)
        y_a = pool_mixer(u_pool, w_pool[l], pool_scale[l]) @ p_pool[l]
        y_b = mla(c_q_raw, c_kv_raw, k_rope_raw, q_norm_g[l], w_uq[l], kv_norm_g[l],
                  w_uk[l], w_uv[l], cos, sin) @ p_attn[l]
        merged = jax.nn.sigmoid(gz_a) * y_a + jax.nn.sigmoid(gz_b) * y_b
        x = x + g1 * (merged @ w_out[l])

        h2 = rms_norm(x, ln2_g[l]) * (1.0 + sc2) + sh2
        x = x + g2 * (jnp.square(jax.nn.relu(h2 @ w_ff1[l])) @ w_ff2[l])

    return rms_norm(x, final_g)


import jax as _jax
import jax.numpy as _jnp

TWIN_FORMAT = 'train_step'
FWD_PARAMS = ['x', 'c', 'positions', 'ln1_g', 'ln2_g', 'w_ada', 'b_ada', 'w_in', 'q_norm_g', 'w_uq', 'kv_norm_g', 'w_uk', 'w_uv', 'w_pool', 'pool_scale', 'p_pool', 'p_attn', 'w_out', 'w_ff1', 'w_ff2', 'final_g']
TWIN_WEIGHTS = ['ln1_g', 'ln2_g', 'w_ada', 'b_ada', 'w_in', 'q_norm_g', 'w_uq', 'kv_norm_g', 'w_uk', 'w_uv', 'w_pool', 'pool_scale', 'p_pool', 'p_attn', 'w_out', 'w_ff1', 'w_ff2', 'final_g']
TWIN_DIFF_INPUT = 'x'
TWIN_INPUTS = ['x', 'c', 'positions', 'ln1_g', 'ln2_g', 'w_ada', 'b_ada', 'w_in', 'q_norm_g', 'w_uq', 'kv_norm_g', 'w_uk', 'w_uv', 'w_pool', 'pool_scale', 'p_pool', 'p_attn', 'w_out', 'w_ff1', 'w_ff2', 'final_g', 'loss_target', 'm_ln1_g', 'm_ln2_g', 'm_w_ada', 'm_b_ada', 'm_w_in', 'm_q_norm_g', 'm_w_uq', 'm_kv_norm_g', 'm_w_uk', 'm_w_uv', 'm_w_pool', 'm_pool_scale', 'm_p_pool', 'm_p_attn', 'm_w_out', 'm_w_ff1', 'm_w_ff2', 'm_final_g', 'v_ln1_g', 'v_ln2_g', 'v_w_ada', 'v_b_ada', 'v_w_in', 'v_q_norm_g', 'v_w_uq', 'v_kv_norm_g', 'v_w_uk', 'v_w_uv', 'v_w_pool', 'v_pool_scale', 'v_p_pool', 'v_p_attn', 'v_w_out', 'v_w_ff1', 'v_w_ff2', 'v_final_g']
TWIN_OUTPUTS = ['loss', 'grad_x', 'grad_ln1_g', 'grad_ln2_g', 'grad_w_ada', 'grad_b_ada', 'grad_w_in', 'grad_q_norm_g', 'grad_w_uq', 'grad_kv_norm_g', 'grad_w_uk', 'grad_w_uv', 'grad_w_pool', 'grad_pool_scale', 'grad_p_pool', 'grad_p_attn', 'grad_w_out', 'grad_w_ff1', 'grad_w_ff2', 'grad_final_g', 'delta_ln1_g', 'delta_ln2_g', 'delta_w_ada', 'delta_b_ada', 'delta_w_in', 'delta_q_norm_g', 'delta_w_uq', 'delta_kv_norm_g', 'delta_w_uk', 'delta_w_uv', 'delta_w_pool', 'delta_pool_scale', 'delta_p_pool', 'delta_p_attn', 'delta_w_out', 'delta_w_ff1', 'delta_w_ff2', 'delta_final_g', 'new_m_ln1_g', 'new_m_ln2_g', 'new_m_w_ada', 'new_m_b_ada', 'new_m_w_in', 'new_m_q_norm_g', 'new_m_w_uq', 'new_m_kv_norm_g', 'new_m_w_uk', 'new_m_w_uv', 'new_m_w_pool', 'new_m_pool_scale', 'new_m_p_pool', 'new_m_p_attn', 'new_m_w_out', 'new_m_w_ff1', 'new_m_w_ff2', 'new_m_final_g', 'new_v_ln1_g', 'new_v_ln2_g', 'new_v_w_ada', 'new_v_b_ada', 'new_v_w_in', 'new_v_q_norm_g', 'new_v_w_uq', 'new_v_kv_norm_g', 'new_v_w_uk', 'new_v_w_uv', 'new_v_w_pool', 'new_v_pool_scale', 'new_v_p_pool', 'new_v_p_attn', 'new_v_w_out', 'new_v_w_ff1', 'new_v_w_ff2', 'new_v_final_g']
TWIN_LEAF_KINDS = {'loss': 'loss', 'grad_x': 'grad_x', 'grad_ln1_g': 'grad_w', 'grad_ln2_g': 'grad_w', 'grad_w_ada': 'grad_w', 'grad_b_ada': 'grad_w', 'grad_w_in': 'grad_w', 'grad_q_norm_g': 'grad_w', 'grad_w_uq': 'grad_w', 'grad_kv_norm_g': 'grad_w', 'grad_w_uk': 'grad_w', 'grad_w_uv': 'grad_w', 'grad_w_pool': 'grad_w', 'grad_pool_scale': 'grad_w', 'grad_p_pool': 'grad_w', 'grad_p_attn': 'grad_w', 'grad_w_out': 'grad_w', 'grad_w_ff1': 'grad_w', 'grad_w_ff2': 'grad_w', 'grad_final_g': 'grad_w', 'delta_ln1_g': 'delta_w', 'delta_ln2_g': 'delta_w', 'delta_w_ada': 'delta_w', 'delta_b_ada': 'delta_w', 'delta_w_in': 'delta_w', 'delta_q_norm_g': 'delta_w', 'delta_w_uq': 'delta_w', 'delta_kv_norm_g': 'delta_w', 'delta_w_uk': 'delta_w', 'delta_w_uv': 'delta_w', 'delta_w_pool': 'delta_w', 'delta_pool_scale': 'delta_w', 'delta_p_pool': 'delta_w', 'delta_p_attn': 'delta_w', 'delta_w_out': 'delta_w', 'delta_w_ff1': 'delta_w', 'delta_w_ff2': 'delta_w', 'delta_final_g': 'delta_w', 'new_m_ln1_g': 'new_m', 'new_m_ln2_g': 'new_m', 'new_m_w_ada': 'new_m', 'new_m_b_ada': 'new_m', 'new_m_w_in': 'new_m', 'new_m_q_norm_g': 'new_m', 'new_m_w_uq': 'new_m', 'new_m_kv_norm_g': 'new_m', 'new_m_w_uk': 'new_m', 'new_m_w_uv': 'new_m', 'new_m_w_pool': 'new_m', 'new_m_pool_scale': 'new_m', 'new_m_p_pool': 'new_m', 'new_m_p_attn': 'new_m', 'new_m_w_out': 'new_m', 'new_m_w_ff1': 'new_m', 'new_m_w_ff2': 'new_m', 'new_m_final_g': 'new_m', 'new_v_ln1_g': 'new_v', 'new_v_ln2_g': 'new_v', 'new_v_w_ada': 'new_v', 'new_v_b_ada': 'new_v', 'new_v_w_in': 'new_v', 'new_v_q_norm_g': 'new_v', 'new_v_w_uq': 'new_v', 'new_v_kv_norm_g': 'new_v', 'new_v_w_uk': 'new_v', 'new_v_w_uv': 'new_v', 'new_v_w_pool': 'new_v', 'new_v_pool_scale': 'new_v', 'new_v_p_pool': 'new_v', 'new_v_p_attn': 'new_v', 'new_v_w_out': 'new_v', 'new_v_w_ff1': 'new_v', 'new_v_w_ff2': 'new_v', 'new_v_final_g': 'new_v'}


def _forward(args):
    return _fwd_reference(*[args[k] for k in FWD_PARAMS])


def _output_shape():
    def fwd():
        inp = _fwd_setup_inputs(0)
        return _fwd_reference(*[inp[k] for k in FWD_PARAMS])
    out = _jax.eval_shape(fwd)
    return out.shape, out.dtype

N_MICROBATCH = 1
ADAM_LR = 0.001
ADAM_B1 = 0.9
ADAM_B2 = 0.999
ADAM_EPS = 1e-08
ADAM_WD = 0.01
ADAM_STEP = 10
PER_EXAMPLE_BATCH_AXIS = {'x': 0, 'c': 0, 'positions': 0, 'loss_target': 0}
SHARED_INPUTS = []
_WEIGHT_DTYPES = {'ln1_g': _jnp.float32, 'ln2_g': _jnp.float32, 'w_ada': _jnp.float32, 'b_ada': _jnp.float32, 'w_in': _jnp.float32, 'q_norm_g': _jnp.float32, 'w_uq': _jnp.float32, 'kv_norm_g': _jnp.float32, 'w_uk': _jnp.float32, 'w_uv': _jnp.float32, 'w_pool': _jnp.float32, 'pool_scale': _jnp.float32, 'p_pool': _jnp.float32, 'p_attn': _jnp.float32, 'w_out': _jnp.float32, 'w_ff1': _jnp.float32, 'w_ff2': _jnp.float32, 'final_g': _jnp.float32}
MOMENT_SCALE = {'ln1_g': 2.996603e-02, 'ln2_g': 7.523252e-02, 'w_ada': 7.503903e-02, 'b_ada': 1.333274e-01, 'w_in': 1.720909e-02, 'q_norm_g': 7.569125e-03, 'w_uq': 5.545456e-03, 'kv_norm_g': 2.037695e-02, 'w_uk': 5.490229e-03, 'w_uv': 1.452512e-02, 'w_pool': 3.706708e-02, 'pool_scale': 3.650011e-02, 'p_pool': 2.603344e-02, 'p_attn': 1.016286e-02, 'w_out': 2.765888e-02, 'w_ff1': 3.925997e-02, 'w_ff2': 7.101848e-02, 'final_g': 3.224670e+01}


def _to_microbatches(a, axis):
    t = _jnp.moveaxis(a, axis, 0)
    t = t.reshape((N_MICROBATCH, t.shape[0] // N_MICROBATCH) + t.shape[1:])
    return _jnp.moveaxis(t, 1, axis + 1)


def setup_inputs(seed: int = 0) -> dict:
    inp = _fwd_setup_inputs(seed)
    key = _jax.random.fold_in(_jax.random.key(seed), 7919)
    shape, _ = _output_shape()
    out = dict(inp)
    out["loss_target"] = _jax.random.normal(_jax.random.fold_in(key, 0), shape, _jnp.float32)
    for i, name in enumerate(TWIN_WEIGHTS):
        w = inp[name].astype(_jnp.float32)
        if MOMENT_SCALE is None:
            s = _jnp.sqrt(_jnp.mean(_jnp.square(w)) + 1e-30)
        else:
            s = MOMENT_SCALE[name]
        km, kv = _jax.random.split(_jax.random.fold_in(key, i + 1))
        out[name] = w
        out["m_" + name] = s * _jax.random.normal(km, w.shape, _jnp.float32)
        out["v_" + name] = (s * s) * _jax.random.uniform(kv, w.shape, _jnp.float32, 0.5, 1.5)
    if N_MICROBATCH > 1:
        for name, axis in PER_EXAMPLE_BATCH_AXIS.items():
            out[name] = _to_microbatches(out[name], axis)
    return {'x': out['x'], 'c': out['c'], 'positions': out['positions'], 'ln1_g': out['ln1_g'], 'ln2_g': out['ln2_g'], 'w_ada': out['w_ada'], 'b_ada': out['b_ada'], 'w_in': out['w_in'], 'q_norm_g': out['q_norm_g'], 'w_uq': out['w_uq'], 'kv_norm_g': out['kv_norm_g'], 'w_uk': out['w_uk'], 'w_uv': out['w_uv'], 'w_pool': out['w_pool'], 'pool_scale': out['pool_scale'], 'p_pool': out['p_pool'], 'p_attn': out['p_attn'], 'w_out': out['w_out'], 'w_ff1': out['w_ff1'], 'w_ff2': out['w_ff2'], 'final_g': out['final_g'], 'loss_target': out['loss_target'], 'm_ln1_g': out['m_ln1_g'], 'm_ln2_g': out['m_ln2_g'], 'm_w_ada': out['m_w_ada'], 'm_b_ada': out['m_b_ada'], 'm_w_in': out['m_w_in'], 'm_q_norm_g': out['m_q_norm_g'], 'm_w_uq': out['m_w_uq'], 'm_kv_norm_g': out['m_kv_norm_g'], 'm_w_uk': out['m_w_uk'], 'm_w_uv': out['m_w_uv'], 'm_w_pool': out['m_w_pool'], 'm_pool_scale': out['m_pool_scale'], 'm_p_pool': out['m_p_pool'], 'm_p_attn': out['m_p_attn'], 'm_w_out': out['m_w_out'], 'm_w_ff1': out['m_w_ff1'], 'm_w_ff2': out['m_w_ff2'], 'm_final_g': out['m_final_g'], 'v_ln1_g': out['v_ln1_g'], 'v_ln2_g': out['v_ln2_g'], 'v_w_ada': out['v_w_ada'], 'v_b_ada': out['v_b_ada'], 'v_w_in': out['v_w_in'], 'v_q_norm_g': out['v_q_norm_g'], 'v_w_uq': out['v_w_uq'], 'v_kv_norm_g': out['v_kv_norm_g'], 'v_w_uk': out['v_w_uk'], 'v_w_uv': out['v_w_uv'], 'v_w_pool': out['v_w_pool'], 'v_pool_scale': out['v_pool_scale'], 'v_p_pool': out['v_p_pool'], 'v_p_attn': out['v_p_attn'], 'v_w_out': out['v_w_out'], 'v_w_ff1': out['v_w_ff1'], 'v_w_ff2': out['v_w_ff2'], 'v_final_g': out['v_final_g']}


def _loss(weights, diff, rest, loss_target):
    with _jax.named_scope("forward"):
        args = {**rest, TWIN_DIFF_INPUT: diff, **{k: w.astype(_WEIGHT_DTYPES[k]) for k, w in weights.items()}}
        y = _forward(args)
    with _jax.named_scope("loss_head"):
        err = _jnp.square(y.astype(_jnp.float32) - loss_target)
        return 0.5 * _jnp.sum(_jnp.mean(err, axis=-1)) if err.ndim else 0.5 * err


def _adamw(w, g, m, v):
    m = ADAM_B1 * m + (1.0 - ADAM_B1) * g
    v = ADAM_B2 * v + (1.0 - ADAM_B2) * _jnp.square(g)
    m_hat = m / (1.0 - ADAM_B1 ** ADAM_STEP)
    v_hat = v / (1.0 - ADAM_B2 ** ADAM_STEP)
    delta = -ADAM_LR * (m_hat / (_jnp.sqrt(v_hat) + ADAM_EPS) + ADAM_WD * w)
    return delta, m, v


def reference(x, c, positions, ln1_g, ln2_g, w_ada, b_ada, w_in, q_norm_g, w_uq, kv_norm_g, w_uk, w_uv, w_pool, pool_scale, p_pool, p_attn, w_out, w_ff1, w_ff2, final_g, loss_target, m_ln1_g, m_ln2_g, m_w_ada, m_b_ada, m_w_in, m_q_norm_g, m_w_uq, m_kv_norm_g, m_w_uk, m_w_uv, m_w_pool, m_pool_scale, m_p_pool, m_p_attn, m_w_out, m_w_ff1, m_w_ff2, m_final_g, v_ln1_g, v_ln2_g, v_w_ada, v_b_ada, v_w_in, v_q_norm_g, v_w_uq, v_kv_norm_g, v_w_uk, v_w_uv, v_w_pool, v_pool_scale, v_p_pool, v_p_attn, v_w_out, v_w_ff1, v_w_ff2, v_final_g):
    given = dict(x=x, c=c, positions=positions, ln1_g=ln1_g, ln2_g=ln2_g, w_ada=w_ada, b_ada=b_ada, w_in=w_in, q_norm_g=q_norm_g, w_uq=w_uq, kv_norm_g=kv_norm_g, w_uk=w_uk, w_uv=w_uv, w_pool=w_pool, pool_scale=pool_scale, p_pool=p_pool, p_attn=p_attn, w_out=w_out, w_ff1=w_ff1, w_ff2=w_ff2, final_g=final_g, loss_target=loss_target, m_ln1_g=m_ln1_g, m_ln2_g=m_ln2_g, m_w_ada=m_w_ada, m_b_ada=m_b_ada, m_w_in=m_w_in, m_q_norm_g=m_q_norm_g, m_w_uq=m_w_uq, m_kv_norm_g=m_kv_norm_g, m_w_uk=m_w_uk, m_w_uv=m_w_uv, m_w_pool=m_w_pool, m_pool_scale=m_pool_scale, m_p_pool=m_p_pool, m_p_attn=m_p_attn, m_w_out=m_w_out, m_w_ff1=m_w_ff1, m_w_ff2=m_w_ff2, m_final_g=m_final_g, v_ln1_g=v_ln1_g, v_ln2_g=v_ln2_g, v_w_ada=v_w_ada, v_b_ada=v_b_ada, v_w_in=v_w_in, v_q_norm_g=v_q_norm_g, v_w_uq=v_w_uq, v_kv_norm_g=v_kv_norm_g, v_w_uk=v_w_uk, v_w_uv=v_w_uv, v_w_pool=v_w_pool, v_pool_scale=v_pool_scale, v_p_pool=v_p_pool, v_p_attn=v_p_attn, v_w_out=v_w_out, v_w_ff1=v_w_ff1, v_w_ff2=v_w_ff2, v_final_g=v_final_g)
    weights = {n: given[n] for n in TWIN_WEIGHTS}
    shared = {n: given[n] for n in SHARED_INPUTS}
    per_example = {n: given[n] for n in ['x', 'c', 'positions']}
    grad_fn = _jax.value_and_grad(_loss, argnums=(0, 1))

    def one_microbatch(ex, loss_target):
        ex = dict(ex)
        diff = ex.pop(TWIN_DIFF_INPUT)
        return grad_fn(weights, diff, {**shared, **ex}, loss_target)

    if N_MICROBATCH == 1:
        loss, (grad_w, grad_x) = one_microbatch(per_example, given["loss_target"])
    else:
        def body(carry, xs):
            loss_sum, grad_sum = carry
            l_k, (gw_k, gx_k) = one_microbatch(xs[0], xs[1])
            with _jax.named_scope("update"):
                return (loss_sum + l_k, _jax.tree.map(_jnp.add, grad_sum, gw_k)), gx_k

        init = (_jnp.zeros((), _jnp.float32), _jax.tree.map(_jnp.zeros_like, weights))
        (loss, grad_w), grad_x = _jax.lax.scan(body, init, (per_example, given["loss_target"]))
    with _jax.named_scope("update"):
        delta_w, new_m, new_v = {}, {}, {}
        for n in TWIN_WEIGHTS:
            delta_w[n], new_m[n], new_v[n] = _adamw(weights[n], grad_w[n], given["m_" + n], given["v_" + n])
    return (loss, grad_x, *[grad_w[n] for n in TWIN_WEIGHTS], *[delta_w[n] for n in TWIN_WEIGHTS],
            *[new_m[n] for n in TWIN_WEIGHTS], *[new_v[n] for n in TWIN_WEIGHTS])
```

```python
import functools
import math

import jax
import jax.numpy as jnp
from jax import lax
from jax.experimental import pallas as pl
from jax.experimental.pallas import tpu as pltpu

F32 = jnp.float32
BF16 = jnp.bfloat16
MESH = pl.DeviceIdType.MESH

D_MODEL = 1024
DEPTH = 2
POOL_WINDOWS = (2, 4, 8, 16)
POOL_GROUP = 128
POOL_DIM = 512
N_HEADS = 8
QK_NOPE = 64
QK_ROPE = 32
QK_DIM = QK_NOPE + QK_ROPE
V_DIM = 64
HEAD_PAD = 128
Q_LORA = 384
KV_LORA = 256
ROPE_THETA = 10000.0
ATTN_DIM = N_HEADS * V_DIM
D_FF = 4 * D_MODEL
N_MOD = 6
EPS = 1e-6
N_CHIPS = 4
N_DEV = 8

ADAM_LR = 0.001
ADAM_B1 = 0.9
ADAM_B2 = 0.999
ADAM_EPS = 1e-08
ADAM_WD = 0.01
ADAM_STEP = 10

VMEM_LIMIT_BYTES = 56 * 1024 * 1024
LANES = 128
HALO = 16

ZC_CQ = 0
ZC_KR = 384
ZC_U = 512
ZC_GA = 1024
ZC_GB = 2048
ZC_CKV = 3072
Z_DIM = 3328

PACK_ROWS = (("w_in", 808), ("w_uq", 72), ("w_uk", 32), ("w_uv", 32), ("p_pool", 128), ("p_attn", 128),
             ("w_out", 256), ("w_ff1", 1024), ("w_ff2", 1024))
PACK_LAYER_ROWS = sum(r for _, r in PACK_ROWS)


def _params(sem=None, **kw):
    return pltpu.CompilerParams(dimension_semantics=sem, vmem_limit_bytes=VMEM_LIMIT_BYTES, **kw)


def _tile(n, target, unit=LANES):
    best = None
    for t in range(unit, min(n, target) + 1, unit):
        if n % t == 0:
            best = t
    return best if best is not None else n


def _mm(a, b, *, name, ta=False, tb=False, out_dtypes=(F32,), epilogue=None, extras=(), tm=1024, tn=1024, tk=512):
    (k_dim, m_dim) = a.shape if ta else a.shape[::-1]
    (n_dim, k_b) = b.shape if tb else b.shape[::-1]
    assert k_dim == k_b, (a.shape, b.shape)
    tm, tn, tk = _tile(m_dim, tm), _tile(n_dim, tn), _tile(k_dim, tk)
    nk = k_dim // tk
    n_extra, n_out = len(extras), len(out_dtypes)
    dims = (((0 if ta else 1,), (1 if tb else 0,)), ((), ()))
    if epilogue is None:
        epilogue = lambda acc: (acc,) * n_out

    def body(a_ref, b_ref, *rest):
        extra_refs, out_refs, acc_ref = rest[:n_extra], rest[n_extra:n_extra + n_out], rest[-1]
        k = pl.program_id(2)

        @pl.when(k == 0)
        def _():
            acc_ref[...] = jnp.zeros_like(acc_ref)

        acc_ref[...] += lax.dot_general(a_ref[...].astype(BF16), b_ref[...].astype(BF16), dims,
                                        preferred_element_type=F32)

        @pl.when(k == nk - 1)
        def _():
            outs = epilogue(acc_ref[...], *[r[...] for r in extra_refs])
            for o_ref, o in zip(out_refs, outs):
                o_ref[...] = o.astype(o_ref.dtype)

    a_spec = pl.BlockSpec((tk, tm), lambda i, j, k: (k, i)) if ta else pl.BlockSpec((tm, tk), lambda i, j, k: (i, k))
    b_spec = pl.BlockSpec((tn, tk), lambda i, j, k: (j, k)) if tb else pl.BlockSpec((tk, tn), lambda i, j, k: (k, j))
    extra_specs = []
    for arr, kind in extras:
        if kind == "tile":
            extra_specs.append(pl.BlockSpec((tm, tn), lambda i, j, k: (i, j)))
        elif kind == "row":
            extra_specs.append(pl.BlockSpec((1, tn), lambda i, j, k: (0, j)))
        elif kind == "col":
            extra_specs.append(pl.BlockSpec((tm, 1), lambda i, j, k: (i, 0)))
        else:
            extra_specs.append(pl.BlockSpec((tm, tn), lambda i, j, k: (i, 0)))
    return pl.pallas_call(
        body,
        name=name,
        grid=(m_dim // tm, n_dim // tn, nk),
        in_specs=[a_spec, b_spec] + extra_specs,
        out_specs=[pl.BlockSpec((tm, tn), lambda i, j, k: (i, j)) for _ in out_dtypes],
        out_shape=[jax.ShapeDtypeStruct((m_dim, n_dim), dt) for dt in out_dtypes],
        scratch_shapes=[pltpu.VMEM((tm, tn), F32)],
        compiler_params=_params(("parallel", "parallel", "arbitrary")),
    )(a, b, *[arr for arr, _ in extras])


def _rows(s):
    return min(512, s)


def _rope_tables(pos_col, inv_freq_lanes, *, name):
    s = pos_col.shape[0]
    tb = _rows(s)

    def body(pos_ref, f_ref, cos_ref, sin_ref):
        ang = pos_ref[...].astype(F32) * f_ref[...]
        lane = lax.broadcasted_iota(jnp.int32, ang.shape, 1)
        on = (lane >= QK_NOPE) & (lane < QK_DIM)
        cos_ref[...] = jnp.where(on, jnp.cos(ang), 0.0)
        sin_ref[...] = jnp.where(on, jnp.sin(ang), 0.0)

    return pl.pallas_call(
        body, name=name, grid=(s // tb,),
        in_specs=[pl.BlockSpec((tb, 1), lambda i: (i, 0)), pl.BlockSpec((1, LANES), lambda i: (0, 0))],
        out_specs=[pl.BlockSpec((tb, LANES), lambda i: (i, 0))] * 2,
        out_shape=[jax.ShapeDtypeStruct((s, LANES), F32)] * 2,
        compiler_params=_params(("parallel",)),
    )(pos_col, inv_freq_lanes)


def _rotate_half(x):
    lane = lax.broadcasted_iota(jnp.int32, x.shape, 1)
    half = QK_ROPE // 2
    first = (lane >= QK_NOPE) & (lane < QK_NOPE + half)
    second = (lane >= QK_NOPE + half) & (lane < QK_DIM)
    return jnp.where(first, -pltpu.roll(x, LANES - half, 1), jnp.where(second, pltpu.roll(x, half, 1), 0.0))


def _norm_mod(x, g, sc, sh, *, name):
    s, d = x.shape
    tb = _rows(s)

    def body(x_ref, g_ref, sc_ref, sh_ref, h_ref, r_ref):
        xv = x_ref[...]
        r = lax.rsqrt(jnp.mean(xv * xv, axis=-1, keepdims=True) + EPS)
        r_ref[...] = r
        h_ref[...] = (((xv * r) * g_ref[...]) * (1.0 + sc_ref[...]) + sh_ref[...]).astype(BF16)

    vec = pl.BlockSpec((1, d), lambda i: (0, 0))
    return pl.pallas_call(
        body, name=name, grid=(s // tb,),
        in_specs=[pl.BlockSpec((tb, d), lambda i: (i, 0)), vec, vec, vec],
        out_specs=[pl.BlockSpec((tb, d), lambda i: (i, 0)), pl.BlockSpec((tb, 1), lambda i: (i, 0))],
        out_shape=[jax.ShapeDtypeStruct((s, d), BF16), jax.ShapeDtypeStruct((s, 1), F32)],
        compiler_params=_params(("parallel",)),
    )(x, g, sc, sh)


def _window_sums(ext, sign):
    n = ext.shape[0]
    sums, cur, k = [], ext, 1
    for _ in POOL_WINDOWS:
        cur = cur + pltpu.roll(cur, k if sign > 0 else n - k, 0)
        sums.append(cur)
        k *= 2
    return sums


def _mixer_pre(z, cos_t, sin_t, w_pool, pool_scale, gq, gkv, *, name):
    s = z.shape[0]
    tb = _rows(s)
    hb = tb // HALO

    def body(zcq_ref, zkr_ref, zu_ref, zuh_ref, zckv_ref, cos_ref, sin_ref, wp_ref, ps_ref, gq_ref, gkv_ref,
             p_ref, yp_ref, cq_ref, ckv_ref, kr_ref, rq_ref, rkv_ref):
        i = pl.program_id(0)
        u = zu_ref[...]
        halo = jnp.where(i > 0, zuh_ref[...], 0.0)
        ext = jnp.concatenate([halo, u], axis=0)
        t = i * tb + lax.broadcasted_iota(jnp.int32, (tb, 1), 0)
        for g, (w, sw) in enumerate(zip(POOL_WINDOWS, _window_sums(ext, +1))):
            cols = slice(g * POOL_GROUP, (g + 1) * POOL_GROUP)
            cnt = jnp.minimum(t + 1, w).astype(F32)
            pg = (sw[HALO:, cols] / cnt - u[:, cols]).astype(BF16)
            p_ref[:, cols] = pg
            yg = jnp.dot(pg, wp_ref[g].astype(BF16), preferred_element_type=F32)
            yp_ref[:, cols] = (yg * ps_ref[:, cols]).astype(BF16)

        def rms(x_ref, g_ref, out_ref, r_ref):
            xv = x_ref[...]
            r = lax.rsqrt(jnp.mean(xv * xv, axis=-1, keepdims=True) + EPS)
            r_ref[...] = r
            out_ref[...] = ((xv * r) * g_ref[...]).astype(BF16)

        rms(zcq_ref, gq_ref, cq_ref, rq_ref)
        rms(zckv_ref, gkv_ref, ckv_ref, rkv_ref)
        kr = zkr_ref[...]
        kr_ref[...] = (kr * cos_ref[...] + _rotate_half(kr) * sin_ref[...]).astype(BF16)

    def zcol(width, off):
        return pl.BlockSpec((tb, width), lambda i: (i, off // width))

    def full(a):
        return pl.BlockSpec(a.shape, lambda i: (0,) * a.ndim)

    def out(width, dt):
        return pl.BlockSpec((tb, width), lambda i: (i, 0)), jax.ShapeDtypeStruct((s, width), dt)

    outs = [out(POOL_DIM, BF16), out(POOL_DIM, BF16), out(Q_LORA, BF16), out(KV_LORA, BF16), out(LANES, BF16),
            out(1, F32), out(1, F32)]
    return pl.pallas_call(
        body, name=name, grid=(s // tb,),
        in_specs=[zcol(Q_LORA, ZC_CQ), zcol(LANES, ZC_KR), zcol(POOL_DIM, ZC_U),
                  pl.BlockSpec((HALO, POOL_DIM), lambda i: (jnp.maximum(i * hb - 1, 0), ZC_U // POOL_DIM)),
                  zcol(KV_LORA, ZC_CKV),
                  pl.BlockSpec((tb, LANES), lambda i: (i, 0)), pl.BlockSpec((tb, LANES), lambda i: (i, 0)),
                  full(w_pool), full(pool_scale), full(gq), full(gkv)],
        out_specs=[o[0] for o in outs], out_shape=[o[1] for o in outs],
        compiler_params=_params(("parallel",)),
    )(z, z, z, z, z, cos_t, sin_t, w_pool, pool_scale, gq, gkv)


def _sigmoid(x):
    return 1.0 / (1.0 + jnp.exp(-x))


def _merge(z, ya, yb, *, name):
    s, d = ya.shape
    tb = _rows(s)

    def body(ga_ref, gb_ref, ya_ref, yb_ref, out_ref):
        out_ref[...] = (_sigmoid(ga_ref[...]) * ya_ref[...] + _sigmoid(gb_ref[...]) * yb_ref[...]).astype(BF16)

    blk = pl.BlockSpec((tb, d), lambda i: (i, 0))
    return pl.pallas_call(
        body, name=name, grid=(s // tb,),
        in_specs=[pl.BlockSpec((tb, d), lambda i: (i, ZC_GA // d)), pl.BlockSpec((tb, d), lambda i: (i, ZC_GB // d)),
                  blk, blk],
        out_specs=blk, out_shape=jax.ShapeDtypeStruct((s, d), BF16),
        compiler_params=_params(("parallel",)),
    )(z, z, ya, yb)


ATTN_SCALE = 1.0 / math.sqrt(QK_DIM)
NEG_BIG = -1e30


def _causal(qi, kj, t):
    rows = qi * t + lax.broadcasted_iota(jnp.int32, (t, t), 0)
    cols = kj * t + lax.broadcasted_iota(jnp.int32, (t, t), 1)
    return rows >= cols


def _attn_fwd(q, k, v, *, name):
    s = q.shape[0]
    t = _rows(s)
    nt = s // t
    nt_dims = (((1,), (1,)), ((), ()))

    def body(q_ref, k_ref, v_ref, o_ref, lse_ref, m_sc, l_sc, acc_sc):
        qi, kj = pl.program_id(1), pl.program_id(2)

        @pl.when(kj == 0)
        def _():
            m_sc[...] = jnp.full_like(m_sc, -jnp.inf)
            l_sc[...] = jnp.zeros_like(l_sc)
            acc_sc[...] = jnp.zeros_like(acc_sc)

        @pl.when(kj <= qi)
        def _():
            sc = lax.dot_general(q_ref[...], k_ref[...], nt_dims, preferred_element_type=F32) * ATTN_SCALE
            sc = jnp.where(_causal(qi, kj, t), sc, NEG_BIG)
            m_prev = m_sc[...]
            m_new = jnp.maximum(m_prev, jnp.max(sc, axis=-1, keepdims=True))
            p = jnp.exp(sc - m_new)
            alpha = jnp.exp(m_prev - m_new)
            l_sc[...] = alpha * l_sc[...] + jnp.sum(p, axis=-1, keepdims=True)
            acc_sc[...] = alpha * acc_sc[...] + jnp.dot(p.astype(BF16), v_ref[...], preferred_element_type=F32)
            m_sc[...] = m_new

        @pl.when(kj == nt - 1)
        def _():
            l = l_sc[...]
            o_ref[...] = (acc_sc[...] / l).astype(BF16)
            lse_ref[0] = m_sc[...] + jnp.log(l)

    kv_spec = pl.BlockSpec((t, HEAD_PAD), lambda h, i, j: (jnp.minimum(j, i), h))
    return pl.pallas_call(
        body, name=name, grid=(N_HEADS, nt, nt),
        in_specs=[pl.BlockSpec((t, HEAD_PAD), lambda h, i, j: (i, h)), kv_spec, kv_spec],
        out_specs=[pl.BlockSpec((t, HEAD_PAD), lambda h, i, j: (i, h)), pl.BlockSpec((1, t, 1), lambda h, i, j: (h, i, 0))],
        out_shape=[jax.ShapeDtypeStruct((s, N_HEADS * HEAD_PAD), BF16), jax.ShapeDtypeStruct((N_HEADS, s, 1), F32)],
        scratch_shapes=[pltpu.VMEM((t, 1), F32), pltpu.VMEM((t, 1), F32), pltpu.VMEM((t, HEAD_PAD), F32)],
        compiler_params=_params(("parallel", "parallel", "arbitrary")),
    )(q, k, v)


def _attn_delta(do, o, *, name):
    s = o.shape[0]
    t = _rows(s)

    def body(do_ref, o_ref, out_ref):
        out_ref[0] = jnp.sum(do_ref[...].astype(F32) * o_ref[...].astype(F32), axis=-1, keepdims=True)

    blk = pl.BlockSpec((t, HEAD_PAD), lambda h, i: (i, h))
    return pl.pallas_call(
        body, name=name, grid=(N_HEADS, s // t), in_specs=[blk, blk],
        out_specs=pl.BlockSpec((1, t, 1), lambda h, i: (h, i, 0)),
        out_shape=jax.ShapeDtypeStruct((N_HEADS, s, 1), F32),
        compiler_params=_params(("parallel", "parallel")),
    )(do, o)


def _attn_probs(q_ref, k_ref, v_ref, do_ref, lse_ref, dl_ref, qi, kj, t):
    nt_dims = (((1,), (1,)), ((), ()))
    sc = lax.dot_general(q_ref[...], k_ref[...], nt_dims, preferred_element_type=F32) * ATTN_SCALE
    p = jnp.where(_causal(qi, kj, t), jnp.exp(sc - lse_ref[0]), 0.0)
    dp = lax.dot_general(do_ref[...], v_ref[...], nt_dims, preferred_element_type=F32)
    ds = p * (dp - dl_ref[0]) * ATTN_SCALE
    return p, ds


def _attn_bwd_kv(q, k, v, do, lse, delta, *, name):
    s = q.shape[0]
    t = _rows(s)
    nt = s // t
    tn_dims = (((0,), (0,)), ((), ()))

    def body(q_ref, k_ref, v_ref, do_ref, lse_ref, dl_ref, dk_ref, dv_ref, dk_sc, dv_sc):
        kj, qi = pl.program_id(1), pl.program_id(2)

        @pl.when(qi == 0)
        def _():
            dk_sc[...] = jnp.zeros_like(dk_sc)
            dv_sc[...] = jnp.zeros_like(dv_sc)

        @pl.when(qi >= kj)
        def _():
            p, ds = _attn_probs(q_ref, k_ref, v_ref, do_ref, lse_ref, dl_ref, qi, kj, t)
            dv_sc[...] += lax.dot_general(p.astype(BF16), do_ref[...], tn_dims, preferred_element_type=F32)
            dk_sc[...] += lax.dot_general(ds.astype(BF16), q_ref[...], tn_dims, preferred_element_type=F32)

        @pl.when(qi == nt - 1)
        def _():
            dk_ref[...] = dk_sc[...]
            dv_ref[...] = dv_sc[...].astype(BF16)

    q_spec = pl.BlockSpec((t, HEAD_PAD), lambda h, j, i: (jnp.maximum(i, j), h))
    kv_spec = pl.BlockSpec((t, HEAD_PAD), lambda h, j, i: (j, h))
    vec_spec = pl.BlockSpec((1, t, 1), lambda h, j, i: (h, jnp.maximum(i, j), 0))
    return pl.pallas_call(
        body, name=name, grid=(N_HEADS, nt, nt),
        in_specs=[q_spec, kv_spec, kv_spec, q_spec, vec_spec, vec_spec],
        out_specs=[kv_spec, kv_spec],
        out_shape=[jax.ShapeDtypeStruct((s, N_HEADS * HEAD_PAD), F32), jax.ShapeDtypeStruct((s, N_HEADS * HEAD_PAD), BF16)],
        scratch_shapes=[pltpu.VMEM((t, HEAD_PAD), F32), pltpu.VMEM((t, HEAD_PAD), F32)],
        compiler_params=_params(("parallel", "parallel", "arbitrary")),
    )(q, k, v, do, lse, delta)


def _attn_bwd_q(q, k, v, do, lse, delta, *, name):
    s = q.shape[0]
    t = _rows(s)
    nt = s // t

    def body(q_ref, k_ref, v_ref, do_ref, lse_ref, dl_ref, dq_ref, dq_sc):
        qi, kj = pl.program_id(1), pl.program_id(2)

        @pl.when(kj == 0)
        def _():
            dq_sc[...] = jnp.zeros_like(dq_sc)

        @pl.when(kj <= qi)
        def _():
            _, ds = _attn_probs(q_ref, k_ref, v_ref, do_ref, lse_ref, dl_ref, qi, kj, t)
            dq_sc[...] += jnp.dot(ds.astype(BF16), k_ref[...], preferred_element_type=F32)

        @pl.when(kj == nt - 1)
        def _():
            dq_ref[...] = dq_sc[...]

    q_spec = pl.BlockSpec((t, HEAD_PAD), lambda h, i, j: (i, h))
    kv_spec = pl.BlockSpec((t, HEAD_PAD), lambda h, i, j: (jnp.minimum(j, i), h))
    vec_spec = pl.BlockSpec((1, t, 1), lambda h, i, j: (h, i, 0))
    return pl.pallas_call(
        body, name=name, grid=(N_HEADS, nt, nt),
        in_specs=[q_spec, kv_spec, kv_spec, q_spec, vec_spec, vec_spec],
        out_specs=q_spec,
        out_shape=jax.ShapeDtypeStruct((s, N_HEADS * HEAD_PAD), F32),
        scratch_shapes=[pltpu.VMEM((t, HEAD_PAD), F32)],
        compiler_params=_params(("parallel", "parallel", "arbitrary")),
    )(q, k, v, do, lse, delta)


def _acc_specs(widths):
    return ([pl.BlockSpec((1, w), lambda i: (0, 0)) for w in widths],
            [jax.ShapeDtypeStruct((1, w), F32) for w in widths])


def _final_loss(x, g, target, *, name):
    s, d = x.shape
    tb = _rows(s)

    def body(x_ref, g_ref, t_ref, dx_ref, loss_ref, dg_ref):
        @pl.when(pl.program_id(0) == 0)
        def _():
            loss_ref[...] = jnp.zeros_like(loss_ref)
            dg_ref[...] = jnp.zeros_like(dg_ref)

        xv = x_ref[...]
        r = lax.rsqrt(jnp.mean(xv * xv, axis=-1, keepdims=True) + EPS)
        xn = xv * r
        err = xn * g_ref[...] - t_ref[...]
        loss_ref[...] += 0.5 * jnp.sum(jnp.mean(err * err, axis=-1, keepdims=True), axis=0, keepdims=True)
        dy = err / d
        dg_ref[...] += jnp.sum(dy * xn, axis=0, keepdims=True)
        dxn = dy * g_ref[...]
        dx_ref[...] = r * (dxn - xn * jnp.mean(dxn * xn, axis=-1, keepdims=True))

    blk = pl.BlockSpec((tb, d), lambda i: (i, 0))
    acc_specs, acc_shapes = _acc_specs((LANES, d))
    return pl.pallas_call(
        body, name=name, grid=(s // tb,),
        in_specs=[blk, pl.BlockSpec((1, d), lambda i: (0, 0)), blk],
        out_specs=[blk] + acc_specs, out_shape=[jax.ShapeDtypeStruct((s, d), F32)] + acc_shapes,
        compiler_params=_params(("arbitrary",)),
    )(x, g, target)


def _gate_bwd(dx, m, g, *, name):
    s, d = dx.shape
    tb = _rows(s)

    def body(dx_ref, m_ref, g_ref, dm_ref, dg_ref):
        @pl.when(pl.program_id(0) == 0)
        def _():
            dg_ref[...] = jnp.zeros_like(dg_ref)

        dxv = dx_ref[...]
        dm_ref[...] = (dxv * g_ref[...]).astype(BF16)
        dg_ref[...] += jnp.sum(dxv * m_ref[...], axis=0, keepdims=True)

    blk = pl.BlockSpec((tb, d), lambda i: (i, 0))
    acc_specs, acc_shapes = _acc_specs((d,))
    return pl.pallas_call(
        body, name=name, grid=(s // tb,),
        in_specs=[blk, blk, pl.BlockSpec((1, d), lambda i: (0, 0))],
        out_specs=[blk] + acc_specs, out_shape=[jax.ShapeDtypeStruct((s, d), BF16)] + acc_shapes,
        compiler_params=_params(("arbitrary",)),
    )(dx, m, g)


def _norm_mod_bwd(dh, x, r, g, sc, dx_skip, *, name):
    s, d = x.shape
    tb = _rows(s)
    nb = s // tb

    def body(dh_ref, x_ref, r_ref, g_ref, sc_ref, skip_ref, dx_ref, dg_ref, dsc_ref, dsh_ref, da_sc):
        i = pl.program_id(0)

        @pl.when(i == 0)
        def _():
            da_sc[...] = jnp.zeros_like(da_sc)
            dsh_ref[...] = jnp.zeros_like(dsh_ref)

        dhv, rv = dh_ref[...], r_ref[...]
        xn = x_ref[...] * rv
        dsh_ref[...] += jnp.sum(dhv, axis=0, keepdims=True)
        da_sc[...] += jnp.sum(dhv * xn, axis=0, keepdims=True)
        dxn = dhv * (g_ref[...] * (1.0 + sc_ref[...]))
        dx_ref[...] = skip_ref[...] + rv * (dxn - xn * jnp.mean(dxn * xn, axis=-1, keepdims=True))

        @pl.when(i == nb - 1)
        def _():
            dg_ref[...] = da_sc[...] * (1.0 + sc_ref[...])
            dsc_ref[...] = da_sc[...] * g_ref[...]

    blk = pl.BlockSpec((tb, d), lambda i: (i, 0))
    vec = pl.BlockSpec((1, d), lambda i: (0, 0))
    acc_specs, acc_shapes = _acc_specs((d, d, d))
    return pl.pallas_call(
        body, name=name, grid=(nb,),
        in_specs=[blk, blk, pl.BlockSpec((tb, 1), lambda i: (i, 0)), vec, vec, blk],
        out_specs=[blk] + acc_specs, out_shape=[jax.ShapeDtypeStruct((s, d), F32)] + acc_shapes,
        scratch_shapes=[pltpu.VMEM((1, d), F32)],
        compiler_params=_params(("arbitrary",)),
    )(dh, x, r, g, sc, dx_skip)


def _merge_bwd(dmerged, z, ya, yb, *, name):
    s, d = ya.shape
    tb = _rows(s)

    def body(dm_ref, ga_ref, gb_ref, ya_ref, yb_ref, dya_ref, dyb_ref, dga_ref, dgb_ref):
        dm = dm_ref[...]
        for g_ref, y_ref, dy_ref, dg_ref in ((ga_ref, ya_ref, dya_ref, dga_ref), (gb_ref, yb_ref, dyb_ref, dgb_ref)):
            sg = _sigmoid(g_ref[...])
            dy_ref[...] = (dm * sg).astype(BF16)
            dg_ref[...] = (dm * y_ref[...] * (sg * (1.0 - sg))).astype(BF16)

    blk = pl.BlockSpec((tb, d), lambda i: (i, 0))
    return pl.pallas_call(
        body, name=name, grid=(s // tb,),
        in_specs=[blk, pl.BlockSpec((tb, d), lambda i: (i, ZC_GA // d)), pl.BlockSpec((tb, d), lambda i: (i, ZC_GB // d)),
                  blk, blk],
        out_specs=[blk] * 4, out_shape=[jax.ShapeDtypeStruct((s, d), BF16)] * 4,
        compiler_params=_params(("parallel",)),
    )(dmerged, z, z, ya, yb)


def _pool_bwd(dyp, p, w_pool, pool_scale, *, name):
    s = dyp.shape[0]
    tb = _rows(s)
    nb = s // tb
    hb = tb // HALO
    nt_dims = (((1,), (1,)), ((), ()))
    tn_dims = (((0,), (0,)), ((), ()))

    def body(dy_ref, dyn_ref, p_ref, wp_ref, ps_ref, du_ref, gwp_ref, gps_ref):
        i = pl.program_id(0)

        @pl.when(i == 0)
        def _():
            gwp_ref[...] = jnp.zeros_like(gwp_ref)
            gps_ref[...] = jnp.zeros_like(gps_ref)

        cur = dy_ref[...]
        nxt = jnp.where(i < nb - 1, dyn_ref[...], 0.0)
        dpw = (jnp.concatenate([cur, nxt], axis=0) * ps_ref[...]).astype(BF16)
        t = i * tb + lax.broadcasted_iota(jnp.int32, (tb + HALO, 1), 0)
        for g, w in enumerate(POOL_WINDOWS):
            cols = slice(g * POOL_GROUP, (g + 1) * POOL_GROUP)
            wg = wp_ref[g].astype(BF16)
            dp = lax.dot_general(dpw[:, cols], wg, nt_dims, preferred_element_type=F32)
            e = dp / jnp.minimum(t + 1, w).astype(F32)
            lead = _window_sums(e, -1)[g]
            du_ref[:, cols] = (lead[:tb] - dp[:tb]).astype(BF16)
            pg = p_ref[:, cols]
            pw = jnp.dot(pg, wg, preferred_element_type=F32)
            gps_ref[:, cols] += jnp.sum(cur[:, cols] * pw, axis=0, keepdims=True)
            gwp_ref[g] += lax.dot_general(pg, dpw[:tb, cols], tn_dims, preferred_element_type=F32)

    blk = pl.BlockSpec((tb, POOL_DIM), lambda i: (i, 0))
    return pl.pallas_call(
        body, name=name, grid=(nb,),
        in_specs=[blk, pl.BlockSpec((HALO, POOL_DIM), lambda i: (jnp.minimum((i + 1) * hb, s // HALO - 1), 0)), blk,
                  pl.BlockSpec(w_pool.shape, lambda i: (0, 0, 0)), pl.BlockSpec((1, POOL_DIM), lambda i: (0, 0))],
        out_specs=[blk, pl.BlockSpec(w_pool.shape, lambda i: (0, 0, 0)), pl.BlockSpec((1, POOL_DIM), lambda i: (0, 0))],
        out_shape=[jax.ShapeDtypeStruct((s, POOL_DIM), BF16), jax.ShapeDtypeStruct(w_pool.shape, F32),
                   jax.ShapeDtypeStruct((1, POOL_DIM), F32)],
        compiler_params=_params(("arbitrary",)),
    )(dyp, dyp, p, w_pool, pool_scale)


def _rope_bwd_q(dq, cos_t, sin_t, *, name):
    s = dq.shape[0]
    tb = _rows(s)

    def body(dq_ref, cos_ref, sin_ref, out_ref):
        dqv = dq_ref[...]
        lane = lax.broadcasted_iota(jnp.int32, dqv.shape, 1)
        cos_q = cos_ref[...] + jnp.where(lane < QK_NOPE, 1.0, 0.0)
        out_ref[...] = (dqv * cos_q - _rotate_half(dqv * sin_ref[...])).astype(BF16)

    blk = pl.BlockSpec((tb, HEAD_PAD), lambda i, h: (i, h))
    tab = pl.BlockSpec((tb, LANES), lambda i, h: (i, 0))
    return pl.pallas_call(
        body, name=name, grid=(s // tb, N_HEADS), in_specs=[blk, tab, tab], out_specs=blk,
        out_shape=jax.ShapeDtypeStruct(dq.shape, BF16),
        compiler_params=_params(("parallel", "parallel")),
    )(dq, cos_t, sin_t)


def _key_bwd(dk, cos_t, sin_t, *, name):
    s = dk.shape[0]
    tb = _rows(s)

    def body(dk_ref, cos_ref, sin_ref, dkb_ref, dkr_ref):
        dkv = dk_ref[...]
        dkb_ref[...] = dkv.astype(BF16)
        tot = dkv[:, :HEAD_PAD]
        for h in range(1, N_HEADS):
            tot = tot + dkv[:, h * HEAD_PAD:(h + 1) * HEAD_PAD]
        dkr_ref[...] = (tot * cos_ref[...] - _rotate_half(tot * sin_ref[...])).astype(BF16)

    blk = pl.BlockSpec((tb, N_HEADS * HEAD_PAD), lambda i: (i, 0))
    tab = pl.BlockSpec((tb, LANES), lambda i: (i, 0))
    return pl.pallas_call(
        body, name=name, grid=(s // tb,), in_specs=[blk, tab, tab], out_specs=[blk, tab],
        out_shape=[jax.ShapeDtypeStruct(dk.shape, BF16), jax.ShapeDtypeStruct((s, LANES), BF16)],
        compiler_params=_params(("parallel",)),
    )(dk, cos_t, sin_t)


def _rms_bwd(dy, z, z_off, r, g, *, name):
    s, n = dy.shape
    tb = _rows(s)

    def body(dy_ref, x_ref, r_ref, g_ref, dx_ref, dg_ref):
        @pl.when(pl.program_id(0) == 0)
        def _():
            dg_ref[...] = jnp.zeros_like(dg_ref)

        dyv, rv = dy_ref[...], r_ref[...]
        xn = x_ref[...] * rv
        dg_ref[...] += jnp.sum(dyv * xn, axis=0, keepdims=True)
        dxn = dyv * g_ref[...]
        dx_ref[...] = (rv * (dxn - xn * jnp.mean(dxn * xn, axis=-1, keepdims=True))).astype(BF16)

    blk = pl.BlockSpec((tb, n), lambda i: (i, 0))
    acc_specs, acc_shapes = _acc_specs((n,))
    return pl.pallas_call(
        body, name=name, grid=(s // tb,),
        in_specs=[blk, pl.BlockSpec((tb, n), lambda i: (i, z_off // n)), pl.BlockSpec((tb, 1), lambda i: (i, 0)),
                  pl.BlockSpec((1, n), lambda i: (0, 0))],
        out_specs=[blk] + acc_specs, out_shape=[jax.ShapeDtypeStruct((s, n), BF16)] + acc_shapes,
        compiler_params=_params(("arbitrary",)),
    )(dy, z, r, g)


def _silu(c, *, name):
    def body(c_ref, out_ref):
        cv = c_ref[...]
        out_ref[...] = (cv * _sigmoid(cv)).astype(BF16)

    return pl.pallas_call(body, name=name, out_shape=jax.ShapeDtypeStruct(c.shape, BF16),
                          compiler_params=_params())(c)


def _sum_slots(a, n, *, name, out_dtype=F32):
    _, rows, cols = a.shape
    tr = _tile(rows, 256, 8)

    def body(a_ref, out_ref):
        tot = a_ref[0].astype(F32)
        for j in range(1, n):
            tot = tot + a_ref[j].astype(F32)
        out_ref[...] = tot.astype(out_dtype)

    return pl.pallas_call(
        body, name=name, grid=(rows // tr,),
        in_specs=[pl.BlockSpec((n, tr, cols), lambda i: (0, i, 0))],
        out_specs=pl.BlockSpec((tr, cols), lambda i: (i, 0)),
        out_shape=jax.ShapeDtypeStruct((rows, cols), out_dtype),
        compiler_params=_params(("parallel",)),
    )(a)


def _add2(a, b, *, name):
    rows, cols = a.shape
    tr = _tile(rows, 256, 8)

    def body(a_ref, b_ref, out_ref):
        out_ref[...] = a_ref[...] + b_ref[...]

    blk = pl.BlockSpec((tr, cols), lambda i: (i, 0))
    return pl.pallas_call(
        body, name=name, grid=(rows // tr,), in_specs=[blk, blk], out_specs=blk,
        out_shape=jax.ShapeDtypeStruct((rows, cols), F32),
        compiler_params=_params(("parallel",)),
    )(a, b)


def _adamw(w, g, m, v, *, name):
    rows, cols = w.shape
    tr = _tile(rows, max(8, (1 << 18) // cols), 8)
    c1 = 1.0 - ADAM_B1 ** ADAM_STEP
    c2 = 1.0 - ADAM_B2 ** ADAM_STEP

    def body(w_ref, g_ref, m_ref, v_ref, d_ref, nm_ref, nv_ref):
        gv = g_ref[...]
        nm = ADAM_B1 * m_ref[...] + (1.0 - ADAM_B1) * gv
        nv = ADAM_B2 * v_ref[...] + (1.0 - ADAM_B2) * (gv * gv)
        nm_ref[...] = nm
        nv_ref[...] = nv
        d_ref[...] = -ADAM_LR * ((nm / c1) / (jnp.sqrt(nv / c2) + ADAM_EPS) + ADAM_WD * w_ref[...])

    blk = pl.BlockSpec((tr, cols), lambda i: (i, 0))
    return pl.pallas_call(
        body, name=name, grid=(rows // tr,), in_specs=[blk] * 4, out_specs=[blk] * 3,
        out_shape=[jax.ShapeDtypeStruct((rows, cols), F32)] * 3,
        compiler_params=_params(("parallel",)),
    )(w, g, m, v)


def _coords():
    return lax.axis_index("x"), lax.axis_index("y"), lax.axis_index("c")


def _other_chips(x, y):
    return [(1 - x, y), (x, 1 - y), (1 - x, 1 - y)]


def _all_gather_small(blk, *, name):
    m_per, n = blk.shape

    def body(x_ref, out_ref, send_sems, recv_sems, local_sem):
        x, y, c = _coords()
        me, sibling = (x, y, c), (x, y, 1 - c)
        chips = _other_chips(x, y)

        def rows(px, py, pc):
            return out_ref.at[pl.ds((4 * px + 2 * py + pc) * m_per, m_per), :]

        def copy(k, block, to, src=None):
            return pltpu.make_async_remote_copy(
                src_ref=rows(*block) if src is None else src, dst_ref=rows(*block),
                send_sem=send_sems.at[k], recv_sem=recv_sems.at[k], device_id=to, device_id_type=MESH)

        mine = pltpu.make_async_copy(x_ref, rows(*me), local_sem)
        mine.start()
        first = [copy(0, me, sibling, src=x_ref)]
        first += [copy(1 + j, me, (*chip, c), src=x_ref) for j, chip in enumerate(chips)]
        for cp in first:
            cp.start()
        passed = [copy(4 + j, (*chip, c), sibling) for j, chip in enumerate(chips)]
        for j, chip in enumerate(chips):
            copy(1 + j, (*chip, c), me).wait_recv()
            passed[j].start()
        copy(0, sibling, me).wait_recv()
        for j, chip in enumerate(chips):
            copy(4 + j, (*chip, 1 - c), me).wait_recv()
        for cp in first + passed:
            cp.wait_send()
        mine.wait()

    return pl.pallas_call(
        body, name=name,
        out_shape=jax.ShapeDtypeStruct((N_DEV * m_per, n), blk.dtype),
        in_specs=[pl.BlockSpec(memory_space=pltpu.VMEM)],
        out_specs=pl.BlockSpec(memory_space=pltpu.VMEM),
        scratch_shapes=[pltpu.SemaphoreType.DMA((7,)), pltpu.SemaphoreType.DMA((7,)), pltpu.SemaphoreType.DMA],
        compiler_params=_params(),
    )(blk)


def _exchange_chips(src, *, name, scatter):
    out_shape = src.shape if scatter else (N_CHIPS,) + src.shape

    def body(src_ref, out_ref, send_sems, recv_sems, local_sem):
        x, y, c = _coords()
        my = 2 * x + y
        chips = _other_chips(x, y)

        def copy(k, px, py, sending):
            peer = 2 * px + py
            src_blk = src_ref.at[peer if sending else my] if scatter else src_ref
            return pltpu.make_async_remote_copy(
                src_ref=src_blk, dst_ref=out_ref.at[my if sending else peer],
                send_sem=send_sems.at[k], recv_sem=recv_sems.at[k], device_id=(px, py, c), device_id_type=MESH)

        mine = pltpu.make_async_copy(src_ref.at[my] if scatter else src_ref, out_ref.at[my], local_sem)
        mine.start()
        sends = [copy(k, px, py, True) for k, (px, py) in enumerate(chips)]
        for cp in sends:
            cp.start()
        for k, (px, py) in enumerate(chips):
            copy(k, px, py, False).wait_recv()
        for cp in sends:
            cp.wait_send()
        mine.wait()

    return pl.pallas_call(
        body, name=name,
        out_shape=jax.ShapeDtypeStruct(out_shape, src.dtype),
        in_specs=[pl.BlockSpec(memory_space=pl.ANY)],
        out_specs=pl.BlockSpec(memory_space=pl.ANY),
        scratch_shapes=[pltpu.SemaphoreType.DMA((3,)), pltpu.SemaphoreType.DMA((3,)), pltpu.SemaphoreType.DMA],
        compiler_params=_params(),
    )(src)


def _exchange_sibling(src, *, name):
    def body(src_ref, out_ref, send_sem, recv_sem):
        x, y, c = _coords()
        cp = pltpu.make_async_remote_copy(src_ref=src_ref, dst_ref=out_ref, send_sem=send_sem, recv_sem=recv_sem,
                                          device_id=(x, y, 1 - c), device_id_type=MESH)
        cp.start()
        cp.wait()

    return pl.pallas_call(
        body, name=name,
        out_shape=jax.ShapeDtypeStruct(src.shape, src.dtype),
        in_specs=[pl.BlockSpec(memory_space=pl.ANY)],
        out_specs=pl.BlockSpec(memory_space=pl.ANY),
        scratch_shapes=[pltpu.SemaphoreType.DMA, pltpu.SemaphoreType.DMA],
        compiler_params=_params(),
    )(src)


def _pack_rows(a):
    return a.reshape(-1, D_MODEL)


def _pad_heads(w, width):
    r = w.shape[0]
    return jnp.pad(w, ((0, 0), (0, 0), (0, HEAD_PAD - width))).reshape(r, N_HEADS * HEAD_PAD)


def _unpack_layer(gathered, l):
    w, off = {}, l * PACK_LAYER_ROWS
    for name, rows in PACK_ROWS:
        w[name] = gathered[:, off:off + rows]
        off += rows

    def cols(a, k):
        return a.reshape(N_CHIPS, k, -1).transpose(1, 0, 2).reshape(k, -1)

    w_in = cols(w["w_in"], D_MODEL)
    kr = jnp.pad(w_in[:, 1152:1184], ((0, 0), (QK_NOPE, HEAD_PAD - QK_DIM)))
    return dict(
        w_in=jnp.concatenate([w_in[:, 512:896], kr, w_in[:, 0:512], w_in[:, 1184:3232], w_in[:, 896:1152]], axis=1),
        w_uq=_pad_heads(w["w_uq"].reshape(Q_LORA, N_HEADS, QK_DIM), QK_DIM),
        w_uk=_pad_heads(w["w_uk"].reshape(KV_LORA, N_HEADS, QK_NOPE), QK_NOPE),
        w_uv=_pad_heads(w["w_uv"].reshape(KV_LORA, N_HEADS, V_DIM), V_DIM),
        p_pool=cols(w["p_pool"], POOL_DIM),
        p_attn=jnp.pad(cols(w["p_attn"], ATTN_DIM).reshape(N_HEADS, V_DIM, D_MODEL),
                       ((0, 0), (0, HEAD_PAD - V_DIM), (0, 0))).reshape(N_HEADS * HEAD_PAD, D_MODEL),
        w_out=w["w_out"].reshape(D_MODEL, D_MODEL),
        w_ff1=cols(w["w_ff1"], D_MODEL),
        w_ff2=w["w_ff2"].reshape(D_FF, D_MODEL),
    )


def _pack_layer_grads(g):
    def cols(a):
        k = a.shape[0]
        return a.reshape(k, N_CHIPS, -1).transpose(1, 0, 2).reshape(N_CHIPS, -1, D_MODEL)

    def rows(a):
        return a.reshape(N_CHIPS, -1, D_MODEL)

    def heads(a, width):
        return a.reshape(a.shape[0], N_HEADS, HEAD_PAD)[:, :, :width]

    gi = g["w_in"]
    w_in = jnp.concatenate([gi[:, ZC_U:ZC_U + 512], gi[:, ZC_CQ:ZC_CQ + 384], gi[:, ZC_CKV:ZC_CKV + 256],
                            gi[:, ZC_KR + QK_NOPE:ZC_KR + QK_DIM], gi[:, ZC_GA:ZC_GA + 2048]], axis=1)
    p_attn = g["p_attn"].reshape(N_HEADS, HEAD_PAD, D_MODEL)[:, :V_DIM].reshape(ATTN_DIM, D_MODEL)
    parts = dict(w_in=cols(w_in), w_uq=rows(heads(g["w_uq"], QK_DIM)), w_uk=rows(heads(g["w_uk"], QK_NOPE)),
                 w_uv=rows(heads(g["w_uv"], V_DIM)), p_pool=cols(g["p_pool"]), p_attn=cols(p_attn),
                 w_out=rows(g["w_out"]), w_ff1=cols(g["w_ff1"]), w_ff2=rows(g["w_ff2"]))
    return jnp.concatenate([parts[name] for name, _ in PACK_ROWS], axis=1)


def _layer_fwd(l, x, mod, w, small, cos_t, sin_t):
    sh1, sc1, g1, sh2, sc2, g2 = mod
    tag = f"_l{l}"
    h, r1 = _norm_mod(x, small["ln1_g"], sc1, sh1, name="norm1" + tag)
    (z,) = _mm(h, w["w_in"], name="in_proj" + tag, tn=_tile(Z_DIM, 1024))
    p, yp, cq, ckv, kr, rq, rkv = _mixer_pre(z, cos_t, sin_t, small["w_pool"], small["pool_scale"],
                                              small["q_norm_g"], small["kv_norm_g"], name="mixer_pre" + tag)
    (ya,) = _mm(yp, w["p_pool"], name="pool_out" + tag)

    def rope_q(acc, cos, sin):
        lane = lax.broadcasted_iota(jnp.int32, acc.shape, 1)
        return (acc * (cos + jnp.where(lane < QK_NOPE, 1.0, 0.0)) + _rotate_half(acc) * sin,)

    (q,) = _mm(cq, w["w_uq"], name="q_proj" + tag, out_dtypes=(BF16,), epilogue=rope_q,
               extras=((cos_t, "head"), (sin_t, "head")), tn=HEAD_PAD)
    (k,) = _mm(ckv, w["w_uk"], name="k_proj" + tag, out_dtypes=(BF16,), epilogue=lambda acc, krv: (acc + krv,),
               extras=((kr, "head"),), tn=HEAD_PAD)
    (v,) = _mm(ckv, w["w_uv"], name="v_proj" + tag, out_dtypes=(BF16,))
    o, lse = _attn_fwd(q, k, v, name="attn_fwd" + tag)
    (yb,) = _mm(o, w["p_attn"], name="attn_out" + tag)
    merged = _merge(z, ya, yb, name="merge" + tag)
    mo, x1 = _mm(merged, w["w_out"], name="mix_out" + tag, out_dtypes=(F32, F32),
                 epilogue=lambda acc, xr, g: (acc, xr + g * acc), extras=((x, "tile"), (g1, "row")))
    h2, r2 = _norm_mod(x1, small["ln2_g"], sc2, sh2, name="norm2" + tag)
    f, act = _mm(h2, w["w_ff1"], name="ff1" + tag, out_dtypes=(F32, BF16),
                 epilogue=lambda acc: (acc, jnp.square(jnp.maximum(acc, 0.0))))
    m2, x2 = _mm(act, w["w_ff2"], name="ff2" + tag, out_dtypes=(F32, F32),
                 epilogue=lambda acc, xr, g: (acc, xr + g * acc), extras=((x1, "tile"), (g2, "row")))
    saved = dict(x=x, h=h, r1=r1, z=z, p=p, yp=yp, cq=cq, ckv=ckv, rq=rq, rkv=rkv, ya=ya, q=q, k=k, v=v, o=o, lse=lse,
                 yb=yb, merged=merged, mo=mo, x1=x1, h2=h2, r2=r2, f=f, act=act, m2=m2)
    return x2, saved


def _layer_bwd(l, dx2, sv, mod, w, small, cos_t, sin_t):
    sh1, sc1, g1, sh2, sc2, g2 = mod
    tag = f"_l{l}"
    gw = {}
    dm2, dg2 = _gate_bwd(dx2, sv["m2"], g2, name="gate2_bwd" + tag)
    (df,) = _mm(dm2, w["w_ff2"], tb=True, name="ff2_dx" + tag, out_dtypes=(BF16,),
                epilogue=lambda acc, f: (acc * (2.0 * jnp.maximum(f, 0.0)),), extras=((sv["f"], "tile"),))
    (gw["w_ff2"],) = _mm(sv["act"], dm2, ta=True, name="ff2_dw" + tag, out_dtypes=(BF16,))
    (gw["w_ff1"],) = _mm(sv["h2"], df, ta=True, name="ff1_dw" + tag, out_dtypes=(BF16,))
    (dh2,) = _mm(df, w["w_ff1"], tb=True, name="ff1_dx" + tag)
    dx1, dln2, dsc2, dsh2 = _norm_mod_bwd(dh2, sv["x1"], sv["r2"], small["ln2_g"], sc2, dx2, name="norm2_bwd" + tag)
    dmo, dg1 = _gate_bwd(dx1, sv["mo"], g1, name="gate1_bwd" + tag)
    (dmerged,) = _mm(dmo, w["w_out"], tb=True, name="mix_out_dx" + tag)
    (gw["w_out"],) = _mm(sv["merged"], dmo, ta=True, name="mix_out_dw" + tag, out_dtypes=(BF16,))
    dya, dyb, dga, dgb = _merge_bwd(dmerged, sv["z"], sv["ya"], sv["yb"], name="merge_bwd" + tag)
    (gw["p_pool"],) = _mm(sv["yp"], dya, ta=True, name="pool_out_dw" + tag, out_dtypes=(BF16,))
    (dyp,) = _mm(dya, w["p_pool"], tb=True, name="pool_out_dx" + tag)
    du, g_w_pool, g_pool_scale = _pool_bwd(dyp, sv["p"], small["w_pool"], small["pool_scale"], name="pool_bwd" + tag)
    (gw["p_attn"],) = _mm(sv["o"], dyb, ta=True, name="attn_out_dw" + tag, out_dtypes=(BF16,))
    (do,) = _mm(dyb, w["p_attn"], tb=True, name="attn_out_dx" + tag, out_dtypes=(BF16,))
    delta = _attn_delta(do, sv["o"], name="attn_delta" + tag)
    dk, dv = _attn_bwd_kv(sv["q"], sv["k"], sv["v"], do, sv["lse"], delta, name="attn_bwd_kv" + tag)
    dq = _attn_bwd_q(sv["q"], sv["k"], sv["v"], do, sv["lse"], delta, name="attn_bwd_q" + tag)
    dql = _rope_bwd_q(dq, cos_t, sin_t, name="rope_bwd_q" + tag)
    dkb, dkr = _key_bwd(dk, cos_t, sin_t, name="key_bwd" + tag)
    (gw["w_uq"],) = _mm(sv["cq"], dql, ta=True, name="q_proj_dw" + tag, out_dtypes=(BF16,))
    (gw["w_uk"],) = _mm(sv["ckv"], dkb, ta=True, name="k_proj_dw" + tag, out_dtypes=(BF16,))
    (gw["w_uv"],) = _mm(sv["ckv"], dv, ta=True, name="v_proj_dw" + tag, out_dtypes=(BF16,))
    (dcq,) = _mm(dql, w["w_uq"], tb=True, name="q_proj_dx" + tag)
    (dckv,) = _mm(jnp.concatenate([dkb, dv], axis=1), jnp.concatenate([w["w_uk"], w["w_uv"]], axis=1), tb=True,
                  name="kv_proj_dx" + tag)
    dcq_raw, g_qn = _rms_bwd(dcq, sv["z"], ZC_CQ, sv["rq"], small["q_norm_g"], name="q_norm_bwd" + tag)
    dckv_raw, g_kvn = _rms_bwd(dckv, sv["z"], ZC_CKV, sv["rkv"], small["kv_norm_g"], name="kv_norm_bwd" + tag)
    dz = jnp.concatenate([dcq_raw, dkr, du, dga, dgb, dckv_raw], axis=1)
    (gw["w_in"],) = _mm(sv["h"], dz, ta=True, name="in_proj_dw" + tag, out_dtypes=(BF16,), tn=_tile(Z_DIM, 1024))
    (dh,) = _mm(dz, w["w_in"], tb=True, name="in_proj_dx" + tag)
    dx, dln1, dsc1, dsh1 = _norm_mod_bwd(dh, sv["x"], sv["r1"], small["ln1_g"], sc1, dx1, name="norm1_bwd" + tag)
    dmod = jnp.concatenate([dsh1, dsc1, dg1, dsh2, dsc2, dg2], axis=0)
    gsmall = dict(ln1_g=dln1, ln2_g=dln2, q_norm_g=g_qn, kv_norm_g=g_kvn, w_pool=g_w_pool, pool_scale=g_pool_scale)
    return dx, gw, dmod, gsmall


SMALL_ROWS = 152


def kernel(x, c, positions, ln1_g, ln2_g, w_ada, b_ada, w_in, q_norm_g, w_uq, kv_norm_g, w_uk, w_uv, w_pool, pool_scale, p_pool, p_attn, w_out, w_ff1, w_ff2, final_g, loss_target, m_ln1_g, m_ln2_g, m_w_ada, m_b_ada, m_w_in, m_q_norm_g, m_w_uq, m_kv_norm_g, m_w_uk, m_w_uv, m_w_pool, m_pool_scale, m_p_pool, m_p_attn, m_w_out, m_w_ff1, m_w_ff2, m_final_g, v_ln1_g, v_ln2_g, v_w_ada, v_b_ada, v_w_in, v_q_norm_g, v_w_uq, v_kv_norm_g, v_w_uk, v_w_uv, v_w_pool, v_pool_scale, v_p_pool, v_p_attn, v_w_out, v_w_ff1, v_w_ff2, v_final_g):
    weights = dict(ln1_g=ln1_g, ln2_g=ln2_g, w_ada=w_ada, b_ada=b_ada, w_in=w_in, q_norm_g=q_norm_g, w_uq=w_uq,
                   kv_norm_g=kv_norm_g, w_uk=w_uk, w_uv=w_uv, w_pool=w_pool, pool_scale=pool_scale, p_pool=p_pool,
                   p_attn=p_attn, w_out=w_out, w_ff1=w_ff1, w_ff2=w_ff2, final_g=final_g)
    moms = dict(ln1_g=m_ln1_g, ln2_g=m_ln2_g, w_ada=m_w_ada, b_ada=m_b_ada, w_in=m_w_in, q_norm_g=m_q_norm_g,
                w_uq=m_w_uq, kv_norm_g=m_kv_norm_g, w_uk=m_w_uk, w_uv=m_w_uv, w_pool=m_w_pool,
                pool_scale=m_pool_scale, p_pool=m_p_pool, p_attn=m_p_attn, w_out=m_w_out, w_ff1=m_w_ff1,
                w_ff2=m_w_ff2, final_g=m_final_g)
    vels = dict(ln1_g=v_ln1_g, ln2_g=v_ln2_g, w_ada=v_w_ada, b_ada=v_b_ada, w_in=v_w_in, q_norm_g=v_q_norm_g,
                w_uq=v_w_uq, kv_norm_g=v_kv_norm_g, w_uk=v_w_uk, w_uv=v_w_uv, w_pool=v_w_pool,
                pool_scale=v_pool_scale, p_pool=v_p_pool, p_attn=v_p_attn, w_out=v_w_out, w_ff1=v_w_ff1,
                w_ff2=v_w_ff2, final_g=v_final_g)
    order = list(weights)
    seq = x.shape[1]
    my_chip = 2 * lax.axis_index("x") + lax.axis_index("y")
    my_dev = 2 * my_chip + lax.axis_index("c")
    ada_cols = w_ada.shape[2]

    packed = jnp.concatenate([_pack_rows(weights[name][l]) for l in range(DEPTH) for name, _ in PACK_ROWS], axis=0)
    gathered = _exchange_chips(packed.astype(BF16), name="weights_all_gather", scatter=False)
    wl = [_unpack_layer(gathered, l) for l in range(DEPTH)]
    small = [dict(ln1_g=ln1_g[l:l + 1], ln2_g=ln2_g[l:l + 1], q_norm_g=q_norm_g[l:l + 1], kv_norm_g=kv_norm_g[l:l + 1],
                  w_pool=w_pool[l], pool_scale=pool_scale[l:l + 1]) for l in range(DEPTH)]

    c_all = _all_gather_small(jnp.pad(c, ((0, 7), (0, 0))), name="cond_all_gather")
    c_act = _silu(c_all, name="cond_silu")
    b_mine = lax.dynamic_slice_in_dim(b_ada, my_chip * ada_cols, ada_cols, axis=1)
    mod_parts = [_mm(c_act, w_ada[l], name=f"ada_fwd_l{l}", epilogue=lambda acc, b: (acc + b,),
                     extras=((b_mine[l:l + 1], "row"),), tn=_tile(ada_cols, 1024))[0] for l in range(DEPTH)]
    mod_mine = jnp.concatenate([mp[::8] for mp in mod_parts], axis=0)
    mod_all = _all_gather_small(mod_mine, name="mod_all_gather").reshape(N_DEV, DEPTH, N_DEV, ada_cols)
    mods = []
    for l in range(DEPTH):
        row = jnp.concatenate([lax.dynamic_index_in_dim(mod_all[2 * j, l], my_dev, axis=0, keepdims=True)
                               for j in range(N_CHIPS)], axis=1)
        mods.append([row[:, i * D_MODEL:(i + 1) * D_MODEL] for i in range(N_MOD)])

    inv_freq = ROPE_THETA ** (-jnp.arange(0, QK_ROPE, 2, dtype=F32) / QK_ROPE)
    freq_lanes = jnp.concatenate([jnp.zeros((QK_NOPE,), F32), inv_freq, inv_freq,
                                  jnp.zeros((HEAD_PAD - QK_DIM,), F32)]).reshape(1, LANES)
    cos_t, sin_t = _rope_tables(positions.reshape(seq, 1), freq_lanes, name="rope_tables")

    xs, saved = x.reshape(seq, D_MODEL), []
    for l in range(DEPTH):
        xs, sv = _layer_fwd(l, xs, mods[l], wl[l], small[l], cos_t, sin_t)
        saved.append(sv)
    dx, loss_part, g_final = _final_loss(xs, final_g.reshape(1, D_MODEL), loss_target.reshape(seq, D_MODEL),
                                         name="final_loss")
    loss = lax.psum(loss_part[0, 0], ("x", "y", "c"))
    gw, dmod, gsmall = [None] * DEPTH, [None] * DEPTH, [None] * DEPTH
    for l in reversed(range(DEPTH)):
        dx, gw[l], dmod[l], gsmall[l] = _layer_bwd(l, dx, saved[l], mods[l], wl[l], small[l], cos_t, sin_t)
    grads = dict(x=dx.reshape(1, seq, D_MODEL))

    gpack = jnp.concatenate([_pack_layer_grads(gw[l]) for l in range(DEPTH)], axis=1)
    received = _exchange_chips(gpack, name="grads_exchange", scatter=True)
    part = _sum_slots(received, N_CHIPS, name="grads_sum_chips")
    gsum = _add2(part, _exchange_sibling(part, name="grads_swap_cores"), name="grads_sum_cores")
    for name, rows in PACK_ROWS:
        grads[name] = []
    off = 0
    for l in range(DEPTH):
        for name, rows in PACK_ROWS:
            grads[name].append(gsum[off:off + rows].reshape(weights[name].shape[1:]))
            off += rows
    for name, _ in PACK_ROWS:
        grads[name] = jnp.stack(grads[name])

    def lanes(a):
        flat = a.reshape(1, -1)
        return jnp.pad(flat, ((0, 0), (0, D_MODEL - flat.shape[1])))

    contrib = [dmod[0], dmod[1], gsmall[0]["ln1_g"], gsmall[1]["ln1_g"], gsmall[0]["ln2_g"], gsmall[1]["ln2_g"], g_final,
               lanes(jnp.concatenate([gsmall[l]["pool_scale"] for l in range(DEPTH)], axis=1)),
               lanes(jnp.concatenate([gsmall[l]["q_norm_g"] for l in range(DEPTH)], axis=1)),
               lanes(jnp.concatenate([gsmall[l]["kv_norm_g"] for l in range(DEPTH)], axis=1)),
               gsmall[0]["w_pool"].reshape(-1, D_MODEL), gsmall[1]["w_pool"].reshape(-1, D_MODEL)]
    used = sum(a.shape[0] for a in contrib)
    contrib.append(jnp.zeros((SMALL_ROWS - used, D_MODEL), F32))
    small_all = _all_gather_small(jnp.concatenate(contrib, axis=0), name="small_grads_all_gather")
    small_all = small_all.reshape(N_DEV, SMALL_ROWS, D_MODEL)
    ssum = _sum_slots(small_all, N_DEV, name="small_grads_sum")
    n_dmod = DEPTH * N_MOD
    grads["b_ada"] = ssum[:n_dmod].reshape(DEPTH, N_MOD * D_MODEL)
    grads["ln1_g"] = ssum[n_dmod:n_dmod + 2]
    grads["ln2_g"] = ssum[n_dmod + 2:n_dmod + 4]
    grads["final_g"] = ssum[n_dmod + 4]
    grads["pool_scale"] = ssum[n_dmod + 5].reshape(DEPTH, POOL_DIM)
    grads["q_norm_g"] = ssum[n_dmod + 6, :DEPTH * Q_LORA].reshape(DEPTH, Q_LORA)
    grads["kv_norm_g"] = ssum[n_dmod + 7, :DEPTH * KV_LORA].reshape(DEPTH, KV_LORA)
    grads["w_pool"] = ssum[n_dmod + 8:used].reshape(w_pool.shape)
    c_act_t = jnp.pad(c_act[::8].T, ((0, 0), (0, LANES - N_DEV)))
    g_ada = []
    for l in range(DEPTH):
        d_all = small_all[:, l * N_MOD:(l + 1) * N_MOD].reshape(N_DEV, N_MOD * D_MODEL)
        d_mine = lax.dynamic_slice_in_dim(d_all, my_chip * ada_cols, ada_cols, axis=1)
        g_ada.append(_mm(c_act_t, jnp.pad(d_mine, ((0, LANES - N_DEV), (0, 0))), name=f"ada_dw_l{l}",
                         tn=_tile(ada_cols, 1024))[0])
    grads["w_ada"] = jnp.stack(g_ada)

    def view(a):
        return a.reshape(1, -1) if a.ndim == 1 else a.reshape(-1, a.shape[-1])

    delta, new_m, new_v = {}, {}, {}
    for name in order:
        shape = weights[name].shape
        d, nm, nv = _adamw(view(weights[name]), view(grads[name]), view(moms[name]), view(vels[name]),
                           name="adamw_" + name)
        delta[name], new_m[name], new_v[name] = d.reshape(shape), nm.reshape(shape), nv.reshape(shape)
    return (loss, grads["x"], *[grads[n] for n in order], *[delta[n] for n in order],
            *[new_m[n] for n in order], *[new_v[n] for n in order])
```

```python
import functools
import math

import jax
import jax.numpy as jnp
from jax import lax
from jax.experimental import pallas as pl
from jax.experimental.pallas import tpu as pltpu

F32 = jnp.float32
BF16 = jnp.bfloat16
MESH = pl.DeviceIdType.MESH

D_MODEL = 1024
DEPTH = 2
POOL_WINDOWS = (2, 4, 8, 16)
POOL_GROUP = 128
POOL_DIM = 512
N_HEADS = 8
QK_NOPE = 64
QK_ROPE = 32
QK_DIM = QK_NOPE + QK_ROPE
V_DIM = 64
HEAD_PAD = 128
Q_LORA = 384
KV_LORA = 256
ROPE_THETA = 10000.0
ATTN_DIM = N_HEADS * V_DIM
D_FF = 4 * D_MODEL
N_MOD = 6
EPS = 1e-6
N_CHIPS = 4
N_DEV = 8

ADAM_LR = 0.001
ADAM_B1 = 0.9
ADAM_B2 = 0.999
ADAM_EPS = 1e-08
ADAM_WD = 0.01
ADAM_STEP = 10

VMEM_LIMIT_BYTES = 56 * 1024 * 1024
LANES = 128
HALO = 16

ZC_CQ = 0
ZC_KR = 384
ZC_U = 512
ZC_GA = 1024
ZC_GB = 2048
ZC_CKV = 3072
Z_DIM = 3328

PACK_ROWS = (("w_in", 808), ("w_uq", 72), ("w_uk", 32), ("w_uv", 32), ("p_pool", 128), ("p_attn", 128),
             ("w_out", 256), ("w_ff1", 1024), ("w_ff2", 1024))
PACK_LAYER_ROWS = sum(r for _, r in PACK_ROWS)


def _params(sem=None, **kw):
    return pltpu.CompilerParams(dimension_semantics=sem, vmem_limit_bytes=VMEM_LIMIT_BYTES, **kw)


def _tile(n, target, unit=LANES):
    best = None
    for t in range(unit, min(n, target) + 1, unit):
        if n % t == 0:
            best = t
    return best if best is not None else n


def _near_tile(n, target):
    cands = [t for t in range(LANES, n + 1, LANES) if n % t == 0]
    return min(cands, key=lambda t: abs(math.log(t / target))) if cands else n


def _mm(a, b, *, name, ta=False, tb=False, out_dtypes=(F32,), epilogue=None, extras=(), tm=1024, tn=1024, tk=1024):
    (k_dim, m_dim) = a.shape if ta else a.shape[::-1]
    (n_dim, k_b) = b.shape if tb else b.shape[::-1]
    assert k_dim == k_b, (a.shape, b.shape)
    tm, tn, tk = _near_tile(m_dim, tm), _near_tile(n_dim, tn), _near_tile(k_dim, tk)
    nk = k_dim // tk
    n_extra, n_out = len(extras), len(out_dtypes)
    dims = (((0 if ta else 1,), (1 if tb else 0,)), ((), ()))
    if epilogue is None:
        epilogue = lambda acc: (acc,) * n_out

    def body(a_ref, b_ref, *rest):
        extra_refs, out_refs = rest[:n_extra], rest[n_extra:n_extra + n_out]

        def product():
            return lax.dot_general(a_ref[...].astype(BF16), b_ref[...].astype(BF16), dims, preferred_element_type=F32)

        def finish(acc):
            outs = epilogue(acc, *[r[...] for r in extra_refs])
            for o_ref, o in zip(out_refs, outs):
                o_ref[...] = o.astype(o_ref.dtype)

        if nk == 1:
            finish(product())
            return
        acc_ref = rest[-1]
        k = pl.program_id(2)

        @pl.when(k == 0)
        def _():
            acc_ref[...] = product()

        @pl.when((k > 0) & (k < nk - 1))
        def _():
            acc_ref[...] += product()

        @pl.when(k == nk - 1)
        def _():
            finish(acc_ref[...] + product())

    a_spec = pl.BlockSpec((tk, tm), lambda i, j, k: (k, i)) if ta else pl.BlockSpec((tm, tk), lambda i, j, k: (i, k))
    b_spec = pl.BlockSpec((tn, tk), lambda i, j, k: (j, k)) if tb else pl.BlockSpec((tk, tn), lambda i, j, k: (k, j))
    extra_specs = []
    for arr, kind in extras:
        if kind == "tile":
            extra_specs.append(pl.BlockSpec((tm, tn), lambda i, j, k: (i, j)))
        elif kind == "row":
            extra_specs.append(pl.BlockSpec((1, tn), lambda i, j, k: (0, j)))
        elif kind == "col":
            extra_specs.append(pl.BlockSpec((tm, 1), lambda i, j, k: (i, 0)))
        else:
            extra_specs.append(pl.BlockSpec((tm, tn), lambda i, j, k: (i, 0)))
    return pl.pallas_call(
        body,
        name=name,
        grid=(m_dim // tm, n_dim // tn, nk),
        in_specs=[a_spec, b_spec] + extra_specs,
        out_specs=[pl.BlockSpec((tm, tn), lambda i, j, k: (i, j)) for _ in out_dtypes],
        out_shape=[jax.ShapeDtypeStruct((m_dim, n_dim), dt) for dt in out_dtypes],
        scratch_shapes=[pltpu.VMEM((tm, tn), F32)] if nk > 1 else [],
        compiler_params=_params(("parallel", "parallel", "arbitrary")),
    )(a, b, *[arr for arr, _ in extras])


def _rows(s):
    return min(512, s)


def _rope_tables(pos_col, inv_freq_lanes, *, name):
    s = pos_col.shape[0]
    tb = _rows(s)

    def body(pos_ref, f_ref, cos_ref, sin_ref):
        ang = pos_ref[...].astype(F32) * f_ref[...]
        lane = lax.broadcasted_iota(jnp.int32, ang.shape, 1)
        on = (lane >= QK_NOPE) & (lane < QK_DIM)
        cos_ref[...] = jnp.where(on, jnp.cos(ang), 0.0)
        sin_ref[...] = jnp.where(on, jnp.sin(ang), 0.0)

    return pl.pallas_call(
        body, name=name, grid=(s // tb,),
        in_specs=[pl.BlockSpec((tb, 1), lambda i: (i, 0)), pl.BlockSpec((1, LANES), lambda i: (0, 0))],
        out_specs=[pl.BlockSpec((tb, LANES), lambda i: (i, 0))] * 2,
        out_shape=[jax.ShapeDtypeStruct((s, LANES), F32)] * 2,
        compiler_params=_params(("parallel",)),
    )(pos_col, inv_freq_lanes)


def _rotate_half(x):
    lane = lax.broadcasted_iota(jnp.int32, x.shape, 1)
    half = QK_ROPE // 2
    first = (lane >= QK_NOPE) & (lane < QK_NOPE + half)
    second = (lane >= QK_NOPE + half) & (lane < QK_DIM)
    return jnp.where(first, -pltpu.roll(x, LANES - half, 1), jnp.where(second, pltpu.roll(x, half, 1), 0.0))


def _norm_mod(x, g, sc, sh, *, name):
    s, d = x.shape
    tb = _rows(s)

    def body(x_ref, g_ref, sc_ref, sh_ref, h_ref, r_ref):
        xv = x_ref[...]
        r = lax.rsqrt(jnp.mean(xv * xv, axis=-1, keepdims=True) + EPS)
        r_ref[...] = r
        h_ref[...] = (((xv * r) * g_ref[...]) * (1.0 + sc_ref[...]) + sh_ref[...]).astype(BF16)

    vec = pl.BlockSpec((1, d), lambda i: (0, 0))
    return pl.pallas_call(
        body, name=name, grid=(s // tb,),
        in_specs=[pl.BlockSpec((tb, d), lambda i: (i, 0)), vec, vec, vec],
        out_specs=[pl.BlockSpec((tb, d), lambda i: (i, 0)), pl.BlockSpec((tb, 1), lambda i: (i, 0))],
        out_shape=[jax.ShapeDtypeStruct((s, d), BF16), jax.ShapeDtypeStruct((s, 1), F32)],
        compiler_params=_params(("parallel",)),
    )(x, g, sc, sh)


def _window_sums(ext, sign):
    n = ext.shape[0]
    sums, cur, k = [], ext, 1
    for _ in POOL_WINDOWS:
        cur = cur + pltpu.roll(cur, k if sign > 0 else n - k, 0)
        sums.append(cur)
        k *= 2
    return sums


def _mixer_pre(z, cos_t, sin_t, w_pool, pool_scale, gq, gkv, *, name):
    s = z.shape[0]
    tb = _rows(s)
    hb = tb // HALO

    def body(zcq_ref, zkr_ref, zu_ref, zuh_ref, zckv_ref, cos_ref, sin_ref, wp_ref, ps_ref, gq_ref, gkv_ref,
             p_ref, yp_ref, cq_ref, ckv_ref, kr_ref, rq_ref, rkv_ref):
        i = pl.program_id(0)
        u = zu_ref[...]
        halo = jnp.where(i > 0, zuh_ref[...], 0.0)
        ext = jnp.concatenate([halo, u], axis=0)
        t = i * tb + lax.broadcasted_iota(jnp.int32, (tb, 1), 0)
        for g, (w, sw) in enumerate(zip(POOL_WINDOWS, _window_sums(ext, +1))):
            cols = slice(g * POOL_GROUP, (g + 1) * POOL_GROUP)
            cnt = jnp.minimum(t + 1, w).astype(F32)
            pg = (sw[HALO:, cols] / cnt - u[:, cols]).astype(BF16)
            p_ref[:, cols] = pg
            yg = jnp.dot(pg, wp_ref[g].astype(BF16), preferred_element_type=F32)
            yp_ref[:, cols] = (yg * ps_ref[:, cols]).astype(BF16)

        def rms(x_ref, g_ref, out_ref, r_ref):
            xv = x_ref[...]
            r = lax.rsqrt(jnp.mean(xv * xv, axis=-1, keepdims=True) + EPS)
            r_ref[...] = r
            out_ref[...] = ((xv * r) * g_ref[...]).astype(BF16)

        rms(zcq_ref, gq_ref, cq_ref, rq_ref)
        rms(zckv_ref, gkv_ref, ckv_ref, rkv_ref)
        kr = zkr_ref[...]
        kr_ref[...] = (kr * cos_ref[...] + _rotate_half(kr) * sin_ref[...]).astype(BF16)

    def zcol(width, off):
        return pl.BlockSpec((tb, width), lambda i: (i, off // width))

    def full(a):
        return pl.BlockSpec(a.shape, lambda i: (0,) * a.ndim)

    def out(width, dt):
        return pl.BlockSpec((tb, width), lambda i: (i, 0)), jax.ShapeDtypeStruct((s, width), dt)

    outs = [out(POOL_DIM, BF16), out(POOL_DIM, BF16), out(Q_LORA, BF16), out(KV_LORA, BF16), out(LANES, BF16),
            out(1, F32), out(1, F32)]
    return pl.pallas_call(
        body, name=name, grid=(s // tb,),
        in_specs=[zcol(Q_LORA, ZC_CQ), zcol(LANES, ZC_KR), zcol(POOL_DIM, ZC_U),
                  pl.BlockSpec((HALO, POOL_DIM), lambda i: (jnp.maximum(i * hb - 1, 0), ZC_U // POOL_DIM)),
                  zcol(KV_LORA, ZC_CKV),
                  pl.BlockSpec((tb, LANES), lambda i: (i, 0)), pl.BlockSpec((tb, LANES), lambda i: (i, 0)),
                  full(w_pool), full(pool_scale), full(gq), full(gkv)],
        out_specs=[o[0] for o in outs], out_shape=[o[1] for o in outs],
        compiler_params=_params(("parallel",)),
    )(z, z, z, z, z, cos_t, sin_t, w_pool, pool_scale, gq, gkv)


def _sigmoid(x):
    return 1.0 / (1.0 + jnp.exp(-x))


def _merge(z, ya, yb, *, name):
    s, d = ya.shape
    tb = _rows(s)

    def body(ga_ref, gb_ref, ya_ref, yb_ref, out_ref):
        out_ref[...] = (_sigmoid(ga_ref[...]) * ya_ref[...] + _sigmoid(gb_ref[...]) * yb_ref[...]).astype(BF16)

    blk = pl.BlockSpec((tb, d), lambda i: (i, 0))
    return pl.pallas_call(
        body, name=name, grid=(s // tb,),
        in_specs=[pl.BlockSpec((tb, d), lambda i: (i, ZC_GA // d)), pl.BlockSpec((tb, d), lambda i: (i, ZC_GB // d)),
                  blk, blk],
        out_specs=blk, out_shape=jax.ShapeDtypeStruct((s, d), BF16),
        compiler_params=_params(("parallel",)),
    )(z, z, ya, yb)


ATTN_SCALE = 1.0 / math.sqrt(QK_DIM)
NEG_BIG = -1e30


LOG2_E = math.log2(math.e)
EXP2_SCALE = ATTN_SCALE * LOG2_E
NT_DIMS = (((1,), (1,)), ((), ()))
TN_DIMS = (((0,), (0,)), ((), ()))


def _on_or_below_diagonal(t):
    return lax.broadcasted_iota(jnp.int32, (t, t), 0) >= lax.broadcasted_iota(jnp.int32, (t, t), 1)


def _attn_fwd(q, k, v, *, name):
    s = q.shape[0]
    t = _rows(s)

    def body(q_ref, k_ref, v_ref, o_ref, lse_ref):
        qi = pl.program_id(1)
        qv = q_ref[...]

        def block(j, carry, diagonal):
            m, l, acc = carry
            rows = pl.ds(pl.multiple_of(j * t, t), t)
            sc = lax.dot_general(qv, k_ref[rows, :], NT_DIMS, preferred_element_type=F32)
            if diagonal:
                sc = jnp.where(_on_or_below_diagonal(t), sc, NEG_BIG)
            m_new = jnp.maximum(m, jnp.max(sc, axis=-1, keepdims=True))
            p = jnp.exp2((sc - m_new) * EXP2_SCALE)
            alpha = jnp.exp2((m - m_new) * EXP2_SCALE)
            l = alpha * l + jnp.sum(p, axis=-1, keepdims=True)
            acc = alpha * acc + jnp.dot(p.astype(BF16), v_ref[rows, :], preferred_element_type=F32)
            return m_new, l, acc

        init = (jnp.full((t, 1), -jnp.inf, F32), jnp.zeros((t, 1), F32), jnp.zeros((t, HEAD_PAD), F32))
        carry = lax.fori_loop(0, qi, lambda j, c: block(j, c, False), init)
        m, l, acc = block(qi, carry, True)
        o_ref[...] = (acc / l).astype(BF16)
        lse_ref[0] = m * ATTN_SCALE + jnp.log(l)

    q_spec = pl.BlockSpec((t, HEAD_PAD), lambda h, i: (i, h))
    kv_spec = pl.BlockSpec((s, HEAD_PAD), lambda h, i: (0, h))
    return pl.pallas_call(
        body, name=name, grid=(N_HEADS, s // t),
        in_specs=[q_spec, kv_spec, kv_spec],
        out_specs=[q_spec, pl.BlockSpec((1, t, 1), lambda h, i: (h, i, 0))],
        out_shape=[jax.ShapeDtypeStruct((s, N_HEADS * HEAD_PAD), BF16), jax.ShapeDtypeStruct((N_HEADS, s, 1), F32)],
        compiler_params=_params(("parallel", "parallel")),
    )(q, k, v)


def _attn_delta(do, o, *, name):
    s = o.shape[0]
    t = _rows(s)

    def body(do_ref, o_ref, out_ref):
        out_ref[0] = jnp.sum(do_ref[...].astype(F32) * o_ref[...].astype(F32), axis=-1, keepdims=True)

    blk = pl.BlockSpec((t, HEAD_PAD), lambda h, i: (i, h))
    return pl.pallas_call(
        body, name=name, grid=(N_HEADS, s // t), in_specs=[blk, blk],
        out_specs=pl.BlockSpec((1, t, 1), lambda h, i: (h, i, 0)),
        out_shape=jax.ShapeDtypeStruct((N_HEADS, s, 1), F32),
        compiler_params=_params(("parallel", "parallel")),
    )(do, o)


def _attn_bwd(q, k, v, do, lse, delta, *, name):
    s = q.shape[0]
    t = _rows(s)
    nt = s // t

    def body(q_ref, k_ref, v_ref, do_ref, lse_ref, dl_ref, dq_ref, dk_ref, dv_ref):
        kj = pl.program_id(1)

        @pl.when(kj == 0)
        def _():
            dq_ref[...] = jnp.zeros_like(dq_ref)

        kv, vv = k_ref[...], v_ref[...]

        def block(i, carry, diagonal):
            dk, dv = carry
            rows = pl.ds(pl.multiple_of(i * t, t), t)
            qv, dov = q_ref[rows, :], do_ref[rows, :]
            sc = lax.dot_general(qv, kv, NT_DIMS, preferred_element_type=F32)
            p = jnp.exp2(sc * EXP2_SCALE - lse_ref[0, rows, :] * LOG2_E)
            if diagonal:
                p = jnp.where(_on_or_below_diagonal(t), p, 0.0)
            dp = lax.dot_general(dov, vv, NT_DIMS, preferred_element_type=F32)
            ds = (p * (dp - dl_ref[0, rows, :])).astype(BF16)
            dv = dv + lax.dot_general(p.astype(BF16), dov, TN_DIMS, preferred_element_type=F32)
            dk = dk + lax.dot_general(ds, qv, TN_DIMS, preferred_element_type=F32)
            dq_ref[rows, :] += jnp.dot(ds, kv, preferred_element_type=F32) * ATTN_SCALE
            return dk, dv

        zero = jnp.zeros((t, HEAD_PAD), F32)
        carry = block(kj, (zero, zero), True)
        dk, dv = lax.fori_loop(kj + 1, nt, lambda i, c: block(i, c, False), carry)
        dk_ref[...] = dk * ATTN_SCALE
        dv_ref[...] = dv.astype(BF16)

    full_spec = pl.BlockSpec((s, HEAD_PAD), lambda h, j: (0, h))
    kv_spec = pl.BlockSpec((t, HEAD_PAD), lambda h, j: (j, h))
    vec_spec = pl.BlockSpec((1, s, 1), lambda h, j: (h, 0, 0))
    wide = (s, N_HEADS * HEAD_PAD)
    return pl.pallas_call(
        body, name=name, grid=(N_HEADS, nt),
        in_specs=[full_spec, kv_spec, kv_spec, full_spec, vec_spec, vec_spec],
        out_specs=[full_spec, kv_spec, kv_spec],
        out_shape=[jax.ShapeDtypeStruct(wide, F32), jax.ShapeDtypeStruct(wide, F32), jax.ShapeDtypeStruct(wide, BF16)],
        compiler_params=_params(("parallel", "arbitrary")),
    )(q, k, v, do, lse, delta)


def _acc_specs(widths):
    return ([pl.BlockSpec((1, w), lambda i: (0, 0)) for w in widths],
            [jax.ShapeDtypeStruct((1, w), F32) for w in widths])


def _final_loss(x, g, target, *, name):
    s, d = x.shape
    tb = _rows(s)

    def body(x_ref, g_ref, t_ref, dx_ref, loss_ref, dg_ref):
        @pl.when(pl.program_id(0) == 0)
        def _():
            loss_ref[...] = jnp.zeros_like(loss_ref)
            dg_ref[...] = jnp.zeros_like(dg_ref)

        xv = x_ref[...]
        r = lax.rsqrt(jnp.mean(xv * xv, axis=-1, keepdims=True) + EPS)
        xn = xv * r
        err = xn * g_ref[...] - t_ref[...]
        loss_ref[...] += 0.5 * jnp.sum(jnp.mean(err * err, axis=-1, keepdims=True), axis=0, keepdims=True)
        dy = err / d
        dg_ref[...] += jnp.sum(dy * xn, axis=0, keepdims=True)
        dxn = dy * g_ref[...]
        dx_ref[...] = r * (dxn - xn * jnp.mean(dxn * xn, axis=-1, keepdims=True))

    blk = pl.BlockSpec((tb, d), lambda i: (i, 0))
    acc_specs, acc_shapes = _acc_specs((LANES, d))
    return pl.pallas_call(
        body, name=name, grid=(s // tb,),
        in_specs=[blk, pl.BlockSpec((1, d), lambda i: (0, 0)), blk],
        out_specs=[blk] + acc_specs, out_shape=[jax.ShapeDtypeStruct((s, d), F32)] + acc_shapes,
        compiler_params=_params(("arbitrary",)),
    )(x, g, target)


def _gate_bwd(dx, m, g, *, name):
    s, d = dx.shape
    tb = _rows(s)

    def body(dx_ref, m_ref, g_ref, dm_ref, dg_ref):
        @pl.when(pl.program_id(0) == 0)
        def _():
            dg_ref[...] = jnp.zeros_like(dg_ref)

        dxv = dx_ref[...]
        dm_ref[...] = (dxv * g_ref[...]).astype(BF16)
        dg_ref[...] += jnp.sum(dxv * m_ref[...], axis=0, keepdims=True)

    blk = pl.BlockSpec((tb, d), lambda i: (i, 0))
    acc_specs, acc_shapes = _acc_specs((d,))
    return pl.pallas_call(
        body, name=name, grid=(s // tb,),
        in_specs=[blk, blk, pl.BlockSpec((1, d), lambda i: (0, 0))],
        out_specs=[blk] + acc_specs, out_shape=[jax.ShapeDtypeStruct((s, d), BF16)] + acc_shapes,
        compiler_params=_params(("arbitrary",)),
    )(dx, m, g)


def _norm_mod_bwd(dh, x, r, g, sc, dx_skip, *, name):
    s, d = x.shape
    tb = _rows(s)
    nb = s // tb

    def body(dh_ref, x_ref, r_ref, g_ref, sc_ref, skip_ref, dx_ref, dg_ref, dsc_ref, dsh_ref, da_sc):
        i = pl.program_id(0)

        @pl.when(i == 0)
        def _():
            da_sc[...] = jnp.zeros_like(da_sc)
            dsh_ref[...] = jnp.zeros_like(dsh_ref)

        dhv, rv = dh_ref[...], r_ref[...]
        xn = x_ref[...] * rv
        dsh_ref[...] += jnp.sum(dhv, axis=0, keepdims=True)
        da_sc[...] += jnp.sum(dhv * xn, axis=0, keepdims=True)
        dxn = dhv * (g_ref[...] * (1.0 + sc_ref[...]))
        dx_ref[...] = skip_ref[...] + rv * (dxn - xn * jnp.mean(dxn * xn, axis=-1, keepdims=True))

        @pl.when(i == nb - 1)
        def _():
            dg_ref[...] = da_sc[...] * (1.0 + sc_ref[...])
            dsc_ref[...] = da_sc[...] * g_ref[...]

    blk = pl.BlockSpec((tb, d), lambda i: (i, 0))
    vec = pl.BlockSpec((1, d), lambda i: (0, 0))
    acc_specs, acc_shapes = _acc_specs((d, d, d))
    return pl.pallas_call(
        body, name=name, grid=(nb,),
        in_specs=[blk, blk, pl.BlockSpec((tb, 1), lambda i: (i, 0)), vec, vec, blk],
        out_specs=[blk] + acc_specs, out_shape=[jax.ShapeDtypeStruct((s, d), F32)] + acc_shapes,
        scratch_shapes=[pltpu.VMEM((1, d), F32)],
        compiler_params=_params(("arbitrary",)),
    )(dh, x, r, g, sc, dx_skip)


def _merge_bwd(dmerged, z, ya, yb, *, name):
    s, d = ya.shape
    tb = _rows(s)

    def body(dm_ref, ga_ref, gb_ref, ya_ref, yb_ref, dya_ref, dyb_ref, dga_ref, dgb_ref):
        dm = dm_ref[...]
        for g_ref, y_ref, dy_ref, dg_ref in ((ga_ref, ya_ref, dya_ref, dga_ref), (gb_ref, yb_ref, dyb_ref, dgb_ref)):
            sg = _sigmoid(g_ref[...])
            dy_ref[...] = (dm * sg).astype(BF16)
            dg_ref[...] = (dm * y_ref[...] * (sg * (1.0 - sg))).astype(BF16)

    blk = pl.BlockSpec((tb, d), lambda i: (i, 0))
    return pl.pallas_call(
        body, name=name, grid=(s // tb,),
        in_specs=[blk, pl.BlockSpec((tb, d), lambda i: (i, ZC_GA // d)), pl.BlockSpec((tb, d), lambda i: (i, ZC_GB // d)),
                  blk, blk],
        out_specs=[blk] * 4, out_shape=[jax.ShapeDtypeStruct((s, d), BF16)] * 4,
        compiler_params=_params(("parallel",)),
    )(dmerged, z, z, ya, yb)


def _pool_bwd(dyp, p, w_pool, pool_scale, *, name):
    s = dyp.shape[0]
    tb = _rows(s)
    nb = s // tb
    hb = tb // HALO
    nt_dims = (((1,), (1,)), ((), ()))
    tn_dims = (((0,), (0,)), ((), ()))

    def body(dy_ref, dyn_ref, p_ref, wp_ref, ps_ref, du_ref, gwp_ref, gps_ref):
        i = pl.program_id(0)

        @pl.when(i == 0)
        def _():
            gwp_ref[...] = jnp.zeros_like(gwp_ref)
            gps_ref[...] = jnp.zeros_like(gps_ref)

        cur = dy_ref[...]
        nxt = jnp.where(i < nb - 1, dyn_ref[...], 0.0)
        dpw = (jnp.concatenate([cur, nxt], axis=0) * ps_ref[...]).astype(BF16)
        t = i * tb + lax.broadcasted_iota(jnp.int32, (tb + HALO, 1), 0)
        for g, w in enumerate(POOL_WINDOWS):
            cols = slice(g * POOL_GROUP, (g + 1) * POOL_GROUP)
            wg = wp_ref[g].astype(BF16)
            dp = lax.dot_general(dpw[:, cols], wg, nt_dims, preferred_element_type=F32)
            e = dp / jnp.minimum(t + 1, w).astype(F32)
            lead = _window_sums(e, -1)[g]
            du_ref[:, cols] = (lead[:tb] - dp[:tb]).astype(BF16)
            pg = p_ref[:, cols]
            pw = jnp.dot(pg, wg, preferred_element_type=F32)
            gps_ref[:, cols] += jnp.sum(cur[:, cols] * pw, axis=0, keepdims=True)
            gwp_ref[g] += lax.dot_general(pg, dpw[:tb, cols], tn_dims, preferred_element_type=F32)

    blk = pl.BlockSpec((tb, POOL_DIM), lambda i: (i, 0))
    return pl.pallas_call(
        body, name=name, grid=(nb,),
        in_specs=[blk, pl.BlockSpec((HALO, POOL_DIM), lambda i: (jnp.minimum((i + 1) * hb, s // HALO - 1), 0)), blk,
                  pl.BlockSpec(w_pool.shape, lambda i: (0, 0, 0)), pl.BlockSpec((1, POOL_DIM), lambda i: (0, 0))],
        out_specs=[blk, pl.BlockSpec(w_pool.shape, lambda i: (0, 0, 0)), pl.BlockSpec((1, POOL_DIM), lambda i: (0, 0))],
        out_shape=[jax.ShapeDtypeStruct((s, POOL_DIM), BF16), jax.ShapeDtypeStruct(w_pool.shape, F32),
                   jax.ShapeDtypeStruct((1, POOL_DIM), F32)],
        compiler_params=_params(("arbitrary",)),
    )(dyp, dyp, p, w_pool, pool_scale)


def _rope_bwd_q(dq, cos_t, sin_t, *, name):
    s = dq.shape[0]
    tb = _rows(s)

    def body(dq_ref, cos_ref, sin_ref, out_ref):
        dqv = dq_ref[...]
        lane = lax.broadcasted_iota(jnp.int32, dqv.shape, 1)
        cos_q = cos_ref[...] + jnp.where(lane < QK_NOPE, 1.0, 0.0)
        out_ref[...] = (dqv * cos_q - _rotate_half(dqv * sin_ref[...])).astype(BF16)

    blk = pl.BlockSpec((tb, HEAD_PAD), lambda i, h: (i, h))
    tab = pl.BlockSpec((tb, LANES), lambda i, h: (i, 0))
    return pl.pallas_call(
        body, name=name, grid=(s // tb, N_HEADS), in_specs=[blk, tab, tab], out_specs=blk,
        out_shape=jax.ShapeDtypeStruct(dq.shape, BF16),
        compiler_params=_params(("parallel", "parallel")),
    )(dq, cos_t, sin_t)


def _key_bwd(dk, cos_t, sin_t, *, name):
    s = dk.shape[0]
    tb = _rows(s)

    def body(dk_ref, cos_ref, sin_ref, dkb_ref, dkr_ref):
        dkv = dk_ref[...]
        dkb_ref[...] = dkv.astype(BF16)
        tot = dkv[:, :HEAD_PAD]
        for h in range(1, N_HEADS):
            tot = tot + dkv[:, h * HEAD_PAD:(h + 1) * HEAD_PAD]
        dkr_ref[...] = (tot * cos_ref[...] - _rotate_half(tot * sin_ref[...])).astype(BF16)

    blk = pl.BlockSpec((tb, N_HEADS * HEAD_PAD), lambda i: (i, 0))
    tab = pl.BlockSpec((tb, LANES), lambda i: (i, 0))
    return pl.pallas_call(
        body, name=name, grid=(s // tb,), in_specs=[blk, tab, tab], out_specs=[blk, tab],
        out_shape=[jax.ShapeDtypeStruct(dk.shape, BF16), jax.ShapeDtypeStruct((s, LANES), BF16)],
        compiler_params=_params(("parallel",)),
    )(dk, cos_t, sin_t)


def _rms_bwd(dy, z, z_off, r, g, *, name):
    s, n = dy.shape
    tb = _rows(s)

    def body(dy_ref, x_ref, r_ref, g_ref, dx_ref, dg_ref):
        @pl.when(pl.program_id(0) == 0)
        def _():
            dg_ref[...] = jnp.zeros_like(dg_ref)

        dyv, rv = dy_ref[...], r_ref[...]
        xn = x_ref[...] * rv
        dg_ref[...] += jnp.sum(dyv * xn, axis=0, keepdims=True)
        dxn = dyv * g_ref[...]
        dx_ref[...] = (rv * (dxn - xn * jnp.mean(dxn * xn, axis=-1, keepdims=True))).astype(BF16)

    blk = pl.BlockSpec((tb, n), lambda i: (i, 0))
    acc_specs, acc_shapes = _acc_specs((n,))
    return pl.pallas_call(
        body, name=name, grid=(s // tb,),
        in_specs=[blk, pl.BlockSpec((tb, n), lambda i: (i, z_off // n)), pl.BlockSpec((tb, 1), lambda i: (i, 0)),
                  pl.BlockSpec((1, n), lambda i: (0, 0))],
        out_specs=[blk] + acc_specs, out_shape=[jax.ShapeDtypeStruct((s, n), BF16)] + acc_shapes,
        compiler_params=_params(("arbitrary",)),
    )(dy, z, r, g)


def _silu(c, *, name):
    def body(c_ref, out_ref):
        cv = c_ref[...]
        out_ref[...] = (cv * _sigmoid(cv)).astype(BF16)

    return pl.pallas_call(body, name=name, out_shape=jax.ShapeDtypeStruct(c.shape, BF16),
                          compiler_params=_params())(c)


def _sum_slots(a, n, *, name, out_dtype=F32):
    _, rows, cols = a.shape
    tr = _tile(rows, 256, 8)

    def body(a_ref, out_ref):
        tot = a_ref[0].astype(F32)
        for j in range(1, n):
            tot = tot + a_ref[j].astype(F32)
        out_ref[...] = tot.astype(out_dtype)

    return pl.pallas_call(
        body, name=name, grid=(rows // tr,),
        in_specs=[pl.BlockSpec((n, tr, cols), lambda i: (0, i, 0))],
        out_specs=pl.BlockSpec((tr, cols), lambda i: (i, 0)),
        out_shape=jax.ShapeDtypeStruct((rows, cols), out_dtype),
        compiler_params=_params(("parallel",)),
    )(a)


def _add2(a, b, *, name):
    rows, cols = a.shape
    tr = _tile(rows, 256, 8)

    def body(a_ref, b_ref, out_ref):
        out_ref[...] = a_ref[...] + b_ref[...]

    blk = pl.BlockSpec((tr, cols), lambda i: (i, 0))
    return pl.pallas_call(
        body, name=name, grid=(rows // tr,), in_specs=[blk, blk], out_specs=blk,
        out_shape=jax.ShapeDtypeStruct((rows, cols), F32),
        compiler_params=_params(("parallel",)),
    )(a, b)


def _adamw(w, g, m, v, *, name):
    rows, cols = w.shape
    tr = _tile(rows, max(8, (1 << 18) // cols), 8)
    c1 = 1.0 - ADAM_B1 ** ADAM_STEP
    c2 = 1.0 - ADAM_B2 ** ADAM_STEP

    def body(w_ref, g_ref, m_ref, v_ref, d_ref, nm_ref, nv_ref):
        gv = g_ref[...]
        nm = ADAM_B1 * m_ref[...] + (1.0 - ADAM_B1) * gv
        nv = ADAM_B2 * v_ref[...] + (1.0 - ADAM_B2) * (gv * gv)
        nm_ref[...] = nm
        nv_ref[...] = nv
        d_ref[...] = -ADAM_LR * ((nm / c1) / (jnp.sqrt(nv / c2) + ADAM_EPS) + ADAM_WD * w_ref[...])

    blk = pl.BlockSpec((tr, cols), lambda i: (i, 0))
    return pl.pallas_call(
        body, name=name, grid=(rows // tr,), in_specs=[blk] * 4, out_specs=[blk] * 3,
        out_shape=[jax.ShapeDtypeStruct((rows, cols), F32)] * 3,
        compiler_params=_params(("parallel",)),
    )(w, g, m, v)


def _coords():
    return lax.axis_index("x"), lax.axis_index("y"), lax.axis_index("c")


def _other_chips(x, y):
    return [(1 - x, y), (x, 1 - y), (1 - x, 1 - y)]


def _all_gather_small(blk, *, name):
    m_per, n = blk.shape

    def body(x_ref, out_ref, send_sems, recv_sems, local_sem):
        x, y, c = _coords()
        me, sibling = (x, y, c), (x, y, 1 - c)
        chips = _other_chips(x, y)

        def rows(px, py, pc):
            return out_ref.at[pl.ds((4 * px + 2 * py + pc) * m_per, m_per), :]

        def copy(k, block, to, src=None):
            return pltpu.make_async_remote_copy(
                src_ref=rows(*block) if src is None else src, dst_ref=rows(*block),
                send_sem=send_sems.at[k], recv_sem=recv_sems.at[k], device_id=to, device_id_type=MESH)

        mine = pltpu.make_async_copy(x_ref, rows(*me), local_sem)
        mine.start()
        first = [copy(0, me, sibling, src=x_ref)]
        first += [copy(1 + j, me, (*chip, c), src=x_ref) for j, chip in enumerate(chips)]
        for cp in first:
            cp.start()
        passed = [copy(4 + j, (*chip, c), sibling) for j, chip in enumerate(chips)]
        for j, chip in enumerate(chips):
            copy(1 + j, (*chip, c), me).wait_recv()
            passed[j].start()
        copy(0, sibling, me).wait_recv()
        for j, chip in enumerate(chips):
            copy(4 + j, (*chip, 1 - c), me).wait_recv()
        for cp in first + passed:
            cp.wait_send()
        mine.wait()

    return pl.pallas_call(
        body, name=name,
        out_shape=jax.ShapeDtypeStruct((N_DEV * m_per, n), blk.dtype),
        in_specs=[pl.BlockSpec(memory_space=pltpu.VMEM)],
        out_specs=pl.BlockSpec(memory_space=pltpu.VMEM),
        scratch_shapes=[pltpu.SemaphoreType.DMA((7,)), pltpu.SemaphoreType.DMA((7,)), pltpu.SemaphoreType.DMA],
        compiler_params=_params(),
    )(blk)


def _exchange_chips(src, *, name, scatter):
    out_shape = src.shape if scatter else (N_CHIPS,) + src.shape

    def body(src_ref, out_ref, send_sems, recv_sems, local_sem):
        x, y, c = _coords()
        my = 2 * x + y
        chips = _other_chips(x, y)

        def copy(k, px, py, sending):
            peer = 2 * px + py
            src_blk = src_ref.at[peer if sending else my] if scatter else src_ref
            return pltpu.make_async_remote_copy(
                src_ref=src_blk, dst_ref=out_ref.at[my if sending else peer],
                send_sem=send_sems.at[k], recv_sem=recv_sems.at[k], device_id=(px, py, c), device_id_type=MESH)

        mine = pltpu.make_async_copy(src_ref.at[my] if scatter else src_ref, out_ref.at[my], local_sem)
        mine.start()
        sends = [copy(k, px, py, True) for k, (px, py) in enumerate(chips)]
        for cp in sends:
            cp.start()
        for k, (px, py) in enumerate(chips):
            copy(k, px, py, False).wait_recv()
        for cp in sends:
            cp.wait_send()
        mine.wait()

    return pl.pallas_call(
        body, name=name,
        out_shape=jax.ShapeDtypeStruct(out_shape, src.dtype),
        in_specs=[pl.BlockSpec(memory_space=pl.ANY)],
        out_specs=pl.BlockSpec(memory_space=pl.ANY),
        scratch_shapes=[pltpu.SemaphoreType.DMA((3,)), pltpu.SemaphoreType.DMA((3,)), pltpu.SemaphoreType.DMA],
        compiler_params=_params(),
    )(src)


def _exchange_sibling(src, *, name):
    def body(src_ref, out_ref, send_sem, recv_sem):
        x, y, c = _coords()
        cp = pltpu.make_async_remote_copy(src_ref=src_ref, dst_ref=out_ref, send_sem=send_sem, recv_sem=recv_sem,
                                          device_id=(x, y, 1 - c), device_id_type=MESH)
        cp.start()
        cp.wait()

    return pl.pallas_call(
        body, name=name,
        out_shape=jax.ShapeDtypeStruct(src.shape, src.dtype),
        in_specs=[pl.BlockSpec(memory_space=pl.ANY)],
        out_specs=pl.BlockSpec(memory_space=pl.ANY),
        scratch_shapes=[pltpu.SemaphoreType.DMA, pltpu.SemaphoreType.DMA],
        compiler_params=_params(),
    )(src)


def _pack_rows(a):
    return a.reshape(-1, D_MODEL)


def _pad_heads(w, width):
    r = w.shape[0]
    return jnp.pad(w, ((0, 0), (0, 0), (0, HEAD_PAD - width))).reshape(r, N_HEADS * HEAD_PAD)


def _unpack_layer(gathered, l):
    w, off = {}, l * PACK_LAYER_ROWS
    for name, rows in PACK_ROWS:
        w[name] = gathered[:, off:off + rows]
        off += rows

    def cols(a, k):
        return a.reshape(N_CHIPS, k, -1).transpose(1, 0, 2).reshape(k, -1)

    w_in = cols(w["w_in"], D_MODEL)
    kr = jnp.pad(w_in[:, 1152:1184], ((0, 0), (QK_NOPE, HEAD_PAD - QK_DIM)))
    return dict(
        w_in=jnp.concatenate([w_in[:, 512:896], kr, w_in[:, 0:512], w_in[:, 1184:3232], w_in[:, 896:1152]], axis=1),
        w_uq=_pad_heads(w["w_uq"].reshape(Q_LORA, N_HEADS, QK_DIM), QK_DIM),
        w_uk=_pad_heads(w["w_uk"].reshape(KV_LORA, N_HEADS, QK_NOPE), QK_NOPE),
        w_uv=_pad_heads(w["w_uv"].reshape(KV_LORA, N_HEADS, V_DIM), V_DIM),
        p_pool=cols(w["p_pool"], POOL_DIM),
        p_attn=jnp.pad(cols(w["p_attn"], ATTN_DIM).reshape(N_HEADS, V_DIM, D_MODEL),
                       ((0, 0), (0, HEAD_PAD - V_DIM), (0, 0))).reshape(N_HEADS * HEAD_PAD, D_MODEL),
        w_out=w["w_out"].reshape(D_MODEL, D_MODEL),
        w_ff1=cols(w["w_ff1"], D_MODEL),
        w_ff2=w["w_ff2"].reshape(D_FF, D_MODEL),
    )


def _pack_layer_grads(g):
    def cols(a):
        k = a.shape[0]
        return a.reshape(k, N_CHIPS, -1).transpose(1, 0, 2).reshape(N_CHIPS, -1, D_MODEL)

    def rows(a):
        return a.reshape(N_CHIPS, -1, D_MODEL)

    def heads(a, width):
        return a.reshape(a.shape[0], N_HEADS, HEAD_PAD)[:, :, :width]

    gi = g["w_in"]
    w_in = jnp.concatenate([gi[:, ZC_U:ZC_U + 512], gi[:, ZC_CQ:ZC_CQ + 384], gi[:, ZC_CKV:ZC_CKV + 256],
                            gi[:, ZC_KR + QK_NOPE:ZC_KR + QK_DIM], gi[:, ZC_GA:ZC_GA + 2048]], axis=1)
    p_attn = g["p_attn"].reshape(N_HEADS, HEAD_PAD, D_MODEL)[:, :V_DIM].reshape(ATTN_DIM, D_MODEL)
    parts = dict(w_in=cols(w_in), w_uq=rows(heads(g["w_uq"], QK_DIM)), w_uk=rows(heads(g["w_uk"], QK_NOPE)),
                 w_uv=rows(heads(g["w_uv"], V_DIM)), p_pool=cols(g["p_pool"]), p_attn=cols(p_attn),
                 w_out=rows(g["w_out"]), w_ff1=cols(g["w_ff1"]), w_ff2=rows(g["w_ff2"]))
    return jnp.concatenate([parts[name] for name, _ in PACK_ROWS], axis=1)


def _layer_fwd(l, x, mod, w, small, cos_t, sin_t):
    sh1, sc1, g1, sh2, sc2, g2 = mod
    tag = f"_l{l}"
    h, r1 = _norm_mod(x, small["ln1_g"], sc1, sh1, name="norm1" + tag)
    (z,) = _mm(h, w["w_in"], name="in_proj" + tag)
    p, yp, cq, ckv, kr, rq, rkv = _mixer_pre(z, cos_t, sin_t, small["w_pool"], small["pool_scale"],
                                              small["q_norm_g"], small["kv_norm_g"], name="mixer_pre" + tag)
    (ya,) = _mm(yp, w["p_pool"], name="pool_out" + tag)

    def rope_q(acc, cos, sin):
        lane = lax.broadcasted_iota(jnp.int32, acc.shape, 1)
        return (acc * (cos + jnp.where(lane < QK_NOPE, 1.0, 0.0)) + _rotate_half(acc) * sin,)

    (q,) = _mm(cq, w["w_uq"], name="q_proj" + tag, out_dtypes=(BF16,), epilogue=rope_q,
               extras=((cos_t, "head"), (sin_t, "head")), tn=HEAD_PAD)
    (k,) = _mm(ckv, w["w_uk"], name="k_proj" + tag, out_dtypes=(BF16,), epilogue=lambda acc, krv: (acc + krv,),
               extras=((kr, "head"),), tn=HEAD_PAD)
    (v,) = _mm(ckv, w["w_uv"], name="v_proj" + tag, out_dtypes=(BF16,))
    o, lse = _attn_fwd(q, k, v, name="attn_fwd" + tag)
    (yb,) = _mm(o, w["p_attn"], name="attn_out" + tag)
    merged = _merge(z, ya, yb, name="merge" + tag)
    mo, x1 = _mm(merged, w["w_out"], name="mix_out" + tag, out_dtypes=(F32, F32),
                 epilogue=lambda acc, xr, g: (acc, xr + g * acc), extras=((x, "tile"), (g1, "row")), tm=512)
    h2, r2 = _norm_mod(x1, small["ln2_g"], sc2, sh2, name="norm2" + tag)
    f, act = _mm(h2, w["w_ff1"], name="ff1" + tag, out_dtypes=(F32, BF16),
                 epilogue=lambda acc: (acc, jnp.square(jnp.maximum(acc, 0.0))))
    m2, x2 = _mm(act, w["w_ff2"], name="ff2" + tag, out_dtypes=(F32, F32),
                 epilogue=lambda acc, xr, g: (acc, xr + g * acc), extras=((x1, "tile"), (g2, "row")), tm=512)
    saved = dict(x=x, h=h, r1=r1, z=z, p=p, yp=yp, cq=cq, ckv=ckv, rq=rq, rkv=rkv, ya=ya, q=q, k=k, v=v, o=o, lse=lse,
                 yb=yb, merged=merged, mo=mo, x1=x1, h2=h2, r2=r2, f=f, act=act, m2=m2)
    return x2, saved


def _layer_bwd(l, dx2, sv, mod, w, small, cos_t, sin_t):
    sh1, sc1, g1, sh2, sc2, g2 = mod
    tag = f"_l{l}"
    gw = {}
    dm2, dg2 = _gate_bwd(dx2, sv["m2"], g2, name="gate2_bwd" + tag)
    (df,) = _mm(dm2, w["w_ff2"], tb=True, name="ff2_dx" + tag, out_dtypes=(BF16,),
                epilogue=lambda acc, f: (acc * (2.0 * jnp.maximum(f, 0.0)),), extras=((sv["f"], "tile"),))
    (gw["w_ff2"],) = _mm(sv["act"], dm2, ta=True, name="ff2_dw" + tag, out_dtypes=(BF16,))
    (gw["w_ff1"],) = _mm(sv["h2"], df, ta=True, name="ff1_dw" + tag, out_dtypes=(BF16,))
    (dh2,) = _mm(df, w["w_ff1"], tb=True, name="ff1_dx" + tag)
    dx1, dln2, dsc2, dsh2 = _norm_mod_bwd(dh2, sv["x1"], sv["r2"], small["ln2_g"], sc2, dx2, name="norm2_bwd" + tag)
    dmo, dg1 = _gate_bwd(dx1, sv["mo"], g1, name="gate1_bwd" + tag)
    (dmerged,) = _mm(dmo, w["w_out"], tb=True, name="mix_out_dx" + tag)
    (gw["w_out"],) = _mm(sv["merged"], dmo, ta=True, name="mix_out_dw" + tag, out_dtypes=(BF16,))
    dya, dyb, dga, dgb = _merge_bwd(dmerged, sv["z"], sv["ya"], sv["yb"], name="merge_bwd" + tag)
    (gw["p_pool"],) = _mm(sv["yp"], dya, ta=True, name="pool_out_dw" + tag, out_dtypes=(BF16,))
    (dyp,) = _mm(dya, w["p_pool"], tb=True, name="pool_out_dx" + tag)
    du, g_w_pool, g_pool_scale = _pool_bwd(dyp, sv["p"], small["w_pool"], small["pool_scale"], name="pool_bwd" + tag)
    (gw["p_attn"],) = _mm(sv["o"], dyb, ta=True, name="attn_out_dw" + tag, out_dtypes=(BF16,))
    (do,) = _mm(dyb, w["p_attn"], tb=True, name="attn_out_dx" + tag, out_dtypes=(BF16,))
    delta = _attn_delta(do, sv["o"], name="attn_delta" + tag)
    dq, dk, dv = _attn_bwd(sv["q"], sv["k"], sv["v"], do, sv["lse"], delta, name="attn_bwd" + tag)
    dql = _rope_bwd_q(dq, cos_t, sin_t, name="rope_bwd_q" + tag)
    dkb, dkr = _key_bwd(dk, cos_t, sin_t, name="key_bwd" + tag)
    (gw["w_uq"],) = _mm(sv["cq"], dql, ta=True, name="q_proj_dw" + tag, out_dtypes=(BF16,))
    (gw["w_uk"],) = _mm(sv["ckv"], dkb, ta=True, name="k_proj_dw" + tag, out_dtypes=(BF16,))
    (gw["w_uv"],) = _mm(sv["ckv"], dv, ta=True, name="v_proj_dw" + tag, out_dtypes=(BF16,))
    (dcq,) = _mm(dql, w["w_uq"], tb=True, name="q_proj_dx" + tag)
    (dckv,) = _mm(jnp.concatenate([dkb, dv], axis=1), jnp.concatenate([w["w_uk"], w["w_uv"]], axis=1), tb=True,
                  name="kv_proj_dx" + tag)
    dcq_raw, g_qn = _rms_bwd(dcq, sv["z"], ZC_CQ, sv["rq"], small["q_norm_g"], name="q_norm_bwd" + tag)
    dckv_raw, g_kvn = _rms_bwd(dckv, sv["z"], ZC_CKV, sv["rkv"], small["kv_norm_g"], name="kv_norm_bwd" + tag)
    dz = jnp.concatenate([dcq_raw, dkr, du, dga, dgb, dckv_raw], axis=1)
    (gw["w_in"],) = _mm(sv["h"], dz, ta=True, name="in_proj_dw" + tag, out_dtypes=(BF16,))
    (dh,) = _mm(dz, w["w_in"], tb=True, name="in_proj_dx" + tag)
    dx, dln1, dsc1, dsh1 = _norm_mod_bwd(dh, sv["x"], sv["r1"], small["ln1_g"], sc1, dx1, name="norm1_bwd" + tag)
    dmod = jnp.concatenate([dsh1, dsc1, dg1, dsh2, dsc2, dg2], axis=0)
    gsmall = dict(ln1_g=dln1, ln2_g=dln2, q_norm_g=g_qn, kv_norm_g=g_kvn, w_pool=g_w_pool, pool_scale=g_pool_scale)
    return dx, gw, dmod, gsmall


SMALL_ROWS = 152


def kernel(x, c, positions, ln1_g, ln2_g, w_ada, b_ada, w_in, q_norm_g, w_uq, kv_norm_g, w_uk, w_uv, w_pool, pool_scale, p_pool, p_attn, w_out, w_ff1, w_ff2, final_g, loss_target, m_ln1_g, m_ln2_g, m_w_ada, m_b_ada, m_w_in, m_q_norm_g, m_w_uq, m_kv_norm_g, m_w_uk, m_w_uv, m_w_pool, m_pool_scale, m_p_pool, m_p_attn, m_w_out, m_w_ff1, m_w_ff2, m_final_g, v_ln1_g, v_ln2_g, v_w_ada, v_b_ada, v_w_in, v_q_norm_g, v_w_uq, v_kv_norm_g, v_w_uk, v_w_uv, v_w_pool, v_pool_scale, v_p_pool, v_p_attn, v_w_out, v_w_ff1, v_w_ff2, v_final_g):
    weights = dict(ln1_g=ln1_g, ln2_g=ln2_g, w_ada=w_ada, b_ada=b_ada, w_in=w_in, q_norm_g=q_norm_g, w_uq=w_uq,
                   kv_norm_g=kv_norm_g, w_uk=w_uk, w_uv=w_uv, w_pool=w_pool, pool_scale=pool_scale, p_pool=p_pool,
                   p_attn=p_attn, w_out=w_out, w_ff1=w_ff1, w_ff2=w_ff2, final_g=final_g)
    moms = dict(ln1_g=m_ln1_g, ln2_g=m_ln2_g, w_ada=m_w_ada, b_ada=m_b_ada, w_in=m_w_in, q_norm_g=m_q_norm_g,
                w_uq=m_w_uq, kv_norm_g=m_kv_norm_g, w_uk=m_w_uk, w_uv=m_w_uv, w_pool=m_w_pool,
                pool_scale=m_pool_scale, p_pool=m_p_pool, p_attn=m_p_attn, w_out=m_w_out, w_ff1=m_w_ff1,
                w_ff2=m_w_ff2, final_g=m_final_g)
    vels = dict(ln1_g=v_ln1_g, ln2_g=v_ln2_g, w_ada=v_w_ada, b_ada=v_b_ada, w_in=v_w_in, q_norm_g=v_q_norm_g,
                w_uq=v_w_uq, kv_norm_g=v_kv_norm_g, w_uk=v_w_uk, w_uv=v_w_uv, w_pool=v_w_pool,
                pool_scale=v_pool_scale, p_pool=v_p_pool, p_attn=v_p_attn, w_out=v_w_out, w_ff1=v_w_ff1,
                w_ff2=v_w_ff2, final_g=v_final_g)
    order = list(weights)
    seq = x.shape[1]
    my_chip = 2 * lax.axis_index("x") + lax.axis_index("y")
    my_dev = 2 * my_chip + lax.axis_index("c")
    ada_cols = w_ada.shape[2]

    packed = jnp.concatenate([_pack_rows(weights[name][l]) for l in range(DEPTH) for name, _ in PACK_ROWS], axis=0)
    gathered = _exchange_chips(packed.astype(BF16), name="weights_all_gather", scatter=False)
    wl = [_unpack_layer(gathered, l) for l in range(DEPTH)]
    small = [dict(ln1_g=ln1_g[l:l + 1], ln2_g=ln2_g[l:l + 1], q_norm_g=q_norm_g[l:l + 1], kv_norm_g=kv_norm_g[l:l + 1],
                  w_pool=w_pool[l], pool_scale=pool_scale[l:l + 1]) for l in range(DEPTH)]

    c_all = _all_gather_small(jnp.pad(c, ((0, 7), (0, 0))), name="cond_all_gather")
    c_act = _silu(c_all, name="cond_silu")
    b_mine = lax.dynamic_slice_in_dim(b_ada, my_chip * ada_cols, ada_cols, axis=1)
    mod_parts = [_mm(c_act, w_ada[l], name=f"ada_fwd_l{l}", epilogue=lambda acc, b: (acc + b,),
                     extras=((b_mine[l:l + 1], "row"),))[0] for l in range(DEPTH)]
    mod_mine = jnp.concatenate([mp[::8] for mp in mod_parts], axis=0)
    mod_all = _all_gather_small(mod_mine, name="mod_all_gather").reshape(N_DEV, DEPTH, N_DEV, ada_cols)
    mods = []
    for l in range(DEPTH):
        row = jnp.concatenate([lax.dynamic_index_in_dim(mod_all[2 * j, l], my_dev, axis=0, keepdims=True)
                               for j in range(N_CHIPS)], axis=1)
        mods.append([row[:, i * D_MODEL:(i + 1) * D_MODEL] for i in range(N_MOD)])

    inv_freq = ROPE_THETA ** (-jnp.arange(0, QK_ROPE, 2, dtype=F32) / QK_ROPE)
    freq_lanes = jnp.concatenate([jnp.zeros((QK_NOPE,), F32), inv_freq, inv_freq,
                                  jnp.zeros((HEAD_PAD - QK_DIM,), F32)]).reshape(1, LANES)
    cos_t, sin_t = _rope_tables(positions.reshape(seq, 1), freq_lanes, name="rope_tables")

    xs, saved = x.reshape(seq, D_MODEL), []
    for l in range(DEPTH):
        xs, sv = _layer_fwd(l, xs, mods[l], wl[l], small[l], cos_t, sin_t)
        saved.append(sv)
    dx, loss_part, g_final = _final_loss(xs, final_g.reshape(1, D_MODEL), loss_target.reshape(seq, D_MODEL),
                                         name="final_loss")
    loss = lax.psum(loss_part[0, 0], ("x", "y", "c"))
    gw, dmod, gsmall = [None] * DEPTH, [None] * DEPTH, [None] * DEPTH
    for l in reversed(range(DEPTH)):
        dx, gw[l], dmod[l], gsmall[l] = _layer_bwd(l, dx, saved[l], mods[l], wl[l], small[l], cos_t, sin_t)
    grads = dict(x=dx.reshape(1, seq, D_MODEL))

    gpack = jnp.concatenate([_pack_layer_grads(gw[l]) for l in range(DEPTH)], axis=1)
    received = _exchange_chips(gpack, name="grads_exchange", scatter=True)
    part = _sum_slots(received, N_CHIPS, name="grads_sum_chips")
    gsum = _add2(part, _exchange_sibling(part, name="grads_swap_cores"), name="grads_sum_cores")
    for name, rows in PACK_ROWS:
        grads[name] = []
    off = 0
    for l in range(DEPTH):
        for name, rows in PACK_ROWS:
            grads[name].append(gsum[off:off + rows].reshape(weights[name].shape[1:]))
            off += rows
    for name, _ in PACK_ROWS:
        grads[name] = jnp.stack(grads[name])

    def lanes(a):
        flat = a.reshape(1, -1)
        return jnp.pad(flat, ((0, 0), (0, D_MODEL - flat.shape[1])))

    contrib = [dmod[0], dmod[1], gsmall[0]["ln1_g"], gsmall[1]["ln1_g"], gsmall[0]["ln2_g"], gsmall[1]["ln2_g"], g_final,
               lanes(jnp.concatenate([gsmall[l]["pool_scale"] for l in range(DEPTH)], axis=1)),
               lanes(jnp.concatenate([gsmall[l]["q_norm_g"] for l in range(DEPTH)], axis=1)),
               lanes(jnp.concatenate([gsmall[l]["kv_norm_g"] for l in range(DEPTH)], axis=1)),
               gsmall[0]["w_pool"].reshape(-1, D_MODEL), gsmall[1]["w_pool"].reshape(-1, D_MODEL)]
    used = sum(a.shape[0] for a in contrib)
    contrib.append(jnp.zeros((SMALL_ROWS - used, D_MODEL), F32))
    small_all = _all_gather_small(jnp.concatenate(contrib, axis=0), name="small_grads_all_gather")
    small_all = small_all.reshape(N_DEV, SMALL_ROWS, D_MODEL)
    ssum = _sum_slots(small_all, N_DEV, name="small_grads_sum")
    n_dmod = DEPTH * N_MOD
    grads["b_ada"] = ssum[:n_dmod].reshape(DEPTH, N_MOD * D_MODEL)
    grads["ln1_g"] = ssum[n_dmod:n_dmod + 2]
    grads["ln2_g"] = ssum[n_dmod + 2:n_dmod + 4]
    grads["final_g"] = ssum[n_dmod + 4]
    grads["pool_scale"] = ssum[n_dmod + 5].reshape(DEPTH, POOL_DIM)
    grads["q_norm_g"] = ssum[n_dmod + 6, :DEPTH * Q_LORA].reshape(DEPTH, Q_LORA)
    grads["kv_norm_g"] = ssum[n_dmod + 7, :DEPTH * KV_LORA].reshape(DEPTH, KV_LORA)
    grads["w_pool"] = ssum[n_dmod + 8:used].reshape(w_pool.shape)
    c_act_t = jnp.pad(c_act[::8].T, ((0, 0), (0, LANES - N_DEV)))
    g_ada = []
    for l in range(DEPTH):
        d_all = small_all[:, l * N_MOD:(l + 1) * N_MOD].reshape(N_DEV, N_MOD * D_MODEL)
        d_mine = lax.dynamic_slice_in_dim(d_all, my_chip * ada_cols, ada_cols, axis=1)
        g_ada.append(_mm(c_act_t, jnp.pad(d_mine, ((0, LANES - N_DEV), (0, 0))), name=f"ada_dw_l{l}")[0])
    grads["w_ada"] = jnp.stack(g_ada)

    def view(a):
        return a.reshape(1, -1) if a.ndim == 1 else a.reshape(-1, a.shape[-1])

    delta, new_m, new_v = {}, {}, {}
    for name in order:
        shape = weights[name].shape
        d, nm, nv = _adamw(view(weights[name]), view(grads[name]), view(moms[name]), view(vels[name]),
                           name="adamw_" + name)
        delta[name], new_m[name], new_v[name] = d.reshape(shape), nm.reshape(shape), nv.reshape(shape)
    return (loss, grads["x"], *[grads[n] for n in order], *[delta[n] for n in order],
            *[new_m[n] for n in order], *[new_v[n] for n in order])
```

```python
import functools
import math

import jax
import jax.numpy as jnp
from jax import lax
from jax.experimental import pallas as pl
from jax.experimental.pallas import tpu as pltpu

F32 = jnp.float32
BF16 = jnp.bfloat16
MESH = pl.DeviceIdType.MESH

D_MODEL = 1024
DEPTH = 2
POOL_WINDOWS = (2, 4, 8, 16)
POOL_GROUP = 128
POOL_DIM = 512
N_HEADS = 8
QK_NOPE = 64
QK_ROPE = 32
QK_DIM = QK_NOPE + QK_ROPE
V_DIM = 64
HEAD_PAD = 128
Q_LORA = 384
KV_LORA = 256
ROPE_THETA = 10000.0
ATTN_DIM = N_HEADS * V_DIM
D_FF = 4 * D_MODEL
N_MOD = 6
EPS = 1e-6
N_CHIPS = 4
N_DEV = 8

ADAM_LR = 0.001
ADAM_B1 = 0.9
ADAM_B2 = 0.999
ADAM_EPS = 1e-08
ADAM_WD = 0.01
ADAM_STEP = 10

VMEM_LIMIT_BYTES = 56 * 1024 * 1024
LANES = 128
HALO = 16

ZC_CQ = 0
ZC_KR = 384
ZC_U = 512
ZC_GA = 1024
ZC_GB = 2048
ZC_CKV = 3072
Z_DIM = 3328

PACK_ROWS = (("w_in", 808), ("w_uq", 72), ("w_uk", 32), ("w_uv", 32), ("p_pool", 128), ("p_attn", 128),
             ("w_out", 256), ("w_ff1", 1024), ("w_ff2", 1024))
PACK_LAYER_ROWS = sum(r for _, r in PACK_ROWS)


def _params(sem=None, **kw):
    return pltpu.CompilerParams(dimension_semantics=sem, vmem_limit_bytes=VMEM_LIMIT_BYTES, **kw)


def _tile(n, target, unit=LANES):
    best = None
    for t in range(unit, min(n, target) + 1, unit):
        if n % t == 0:
            best = t
    return best if best is not None else n


def _near_tile(n, target):
    cands = [t for t in range(LANES, n + 1, LANES) if n % t == 0]
    return min(cands, key=lambda t: abs(math.log(t / target))) if cands else n


def _mm(a, b, *, name, ta=False, tb=False, out_dtypes=(F32,), epilogue=None, extras=(), tm=1024, tn=1024, tk=1024):
    (k_dim, m_dim) = a.shape if ta else a.shape[::-1]
    (n_dim, k_b) = b.shape if tb else b.shape[::-1]
    assert k_dim == k_b, (a.shape, b.shape)
    tm, tn, tk = _near_tile(m_dim, tm), _near_tile(n_dim, tn), _near_tile(k_dim, tk)
    nk = k_dim // tk
    n_extra, n_out = len(extras), len(out_dtypes)
    dims = (((0 if ta else 1,), (1 if tb else 0,)), ((), ()))
    if epilogue is None:
        epilogue = lambda acc: (acc,) * n_out

    def body(a_ref, b_ref, *rest):
        extra_refs, out_refs = rest[:n_extra], rest[n_extra:n_extra + n_out]

        def product():
            return lax.dot_general(a_ref[...].astype(BF16), b_ref[...].astype(BF16), dims, preferred_element_type=F32)

        def finish(acc):
            outs = epilogue(acc, *[r[...] for r in extra_refs])
            for o_ref, o in zip(out_refs, outs):
                o_ref[...] = o.astype(o_ref.dtype)

        if nk == 1:
            finish(product())
            return
        acc_ref = rest[-1]
        k = pl.program_id(2)

        @pl.when(k == 0)
        def _():
            acc_ref[...] = product()

        @pl.when((k > 0) & (k < nk - 1))
        def _():
            acc_ref[...] += product()

        @pl.when(k == nk - 1)
        def _():
            finish(acc_ref[...] + product())

    a_spec = pl.BlockSpec((tk, tm), lambda i, j, k: (k, i)) if ta else pl.BlockSpec((tm, tk), lambda i, j, k: (i, k))
    b_spec = pl.BlockSpec((tn, tk), lambda i, j, k: (j, k)) if tb else pl.BlockSpec((tk, tn), lambda i, j, k: (k, j))
    extra_specs = []
    for arr, kind in extras:
        if kind == "tile":
            extra_specs.append(pl.BlockSpec((tm, tn), lambda i, j, k: (i, j)))
        elif kind == "row":
            extra_specs.append(pl.BlockSpec((1, tn), lambda i, j, k: (0, j)))
        elif kind == "col":
            extra_specs.append(pl.BlockSpec((tm, 1), lambda i, j, k: (i, 0)))
        else:
            extra_specs.append(pl.BlockSpec((tm, tn), lambda i, j, k: (i, 0)))
    return pl.pallas_call(
        body,
        name=name,
        grid=(m_dim // tm, n_dim // tn, nk),
        in_specs=[a_spec, b_spec] + extra_specs,
        out_specs=[pl.BlockSpec((tm, tn), lambda i, j, k: (i, j)) for _ in out_dtypes],
        out_shape=[jax.ShapeDtypeStruct((m_dim, n_dim), dt) for dt in out_dtypes],
        scratch_shapes=[pltpu.VMEM((tm, tn), F32)] if nk > 1 else [],
        compiler_params=_params(("parallel", "parallel", "arbitrary")),
    )(a, b, *[arr for arr, _ in extras])


def _rows(s):
    return min(512, s)


def _rope_tables(pos_col, inv_freq_lanes, *, name):
    s = pos_col.shape[0]
    tb = _rows(s)

    def body(pos_ref, f_ref, cos_ref, sin_ref):
        ang = pos_ref[...].astype(F32) * f_ref[...]
        lane = lax.broadcasted_iota(jnp.int32, ang.shape, 1)
        on = (lane >= QK_NOPE) & (lane < QK_DIM)
        cos_ref[...] = jnp.where(on, jnp.cos(ang), 0.0)
        sin_ref[...] = jnp.where(on, jnp.sin(ang), 0.0)

    return pl.pallas_call(
        body, name=name, grid=(s // tb,),
        in_specs=[pl.BlockSpec((tb, 1), lambda i: (i, 0)), pl.BlockSpec((1, LANES), lambda i: (0, 0))],
        out_specs=[pl.BlockSpec((tb, LANES), lambda i: (i, 0))] * 2,
        out_shape=[jax.ShapeDtypeStruct((s, LANES), F32)] * 2,
        compiler_params=_params(("parallel",)),
    )(pos_col, inv_freq_lanes)


def _rotate_half(x):
    lane = lax.broadcasted_iota(jnp.int32, x.shape, 1)
    half = QK_ROPE // 2
    first = (lane >= QK_NOPE) & (lane < QK_NOPE + half)
    second = (lane >= QK_NOPE + half) & (lane < QK_DIM)
    return jnp.where(first, -pltpu.roll(x, LANES - half, 1), jnp.where(second, pltpu.roll(x, half, 1), 0.0))


def _norm_mod(x, g, sc, sh, *, name):
    s, d = x.shape
    tb = _rows(s)

    def body(x_ref, g_ref, sc_ref, sh_ref, h_ref, r_ref):
        xv = x_ref[...]
        r = lax.rsqrt(jnp.mean(xv * xv, axis=-1, keepdims=True) + EPS)
        r_ref[...] = r
        h_ref[...] = (((xv * r) * g_ref[...]) * (1.0 + sc_ref[...]) + sh_ref[...]).astype(BF16)

    vec = pl.BlockSpec((1, d), lambda i: (0, 0))
    return pl.pallas_call(
        body, name=name, grid=(s // tb,),
        in_specs=[pl.BlockSpec((tb, d), lambda i: (i, 0)), vec, vec, vec],
        out_specs=[pl.BlockSpec((tb, d), lambda i: (i, 0)), pl.BlockSpec((tb, 1), lambda i: (i, 0))],
        out_shape=[jax.ShapeDtypeStruct((s, d), BF16), jax.ShapeDtypeStruct((s, 1), F32)],
        compiler_params=_params(("parallel",)),
    )(x, g, sc, sh)


def _window_sums(ext, sign):
    n = ext.shape[0]
    sums, cur, k = [], ext, 1
    for _ in POOL_WINDOWS:
        cur = cur + pltpu.roll(cur, k if sign > 0 else n - k, 0)
        sums.append(cur)
        k *= 2
    return sums


def _mixer_pre(z, cos_t, sin_t, w_pool, pool_scale, gq, gkv, *, name):
    s = z.shape[0]
    tb = _rows(s)
    hb = tb // HALO

    def body(zcq_ref, zkr_ref, zu_ref, zuh_ref, zckv_ref, cos_ref, sin_ref, wp_ref, ps_ref, gq_ref, gkv_ref,
             p_ref, yp_ref, cq_ref, ckv_ref, kr_ref, rq_ref, rkv_ref):
        i = pl.program_id(0)
        u = zu_ref[...]
        halo = jnp.where(i > 0, zuh_ref[...], 0.0)
        ext = jnp.concatenate([halo, u], axis=0)
        t = i * tb + lax.broadcasted_iota(jnp.int32, (tb, 1), 0)
        for g, (w, sw) in enumerate(zip(POOL_WINDOWS, _window_sums(ext, +1))):
            cols = slice(g * POOL_GROUP, (g + 1) * POOL_GROUP)
            cnt = jnp.minimum(t + 1, w).astype(F32)
            pg = (sw[HALO:, cols] / cnt - u[:, cols]).astype(BF16)
            p_ref[:, cols] = pg
            yg = jnp.dot(pg, wp_ref[g].astype(BF16), preferred_element_type=F32)
            yp_ref[:, cols] = (yg * ps_ref[:, cols]).astype(BF16)

        def rms(x_ref, g_ref, out_ref, r_ref):
            xv = x_ref[...]
            r = lax.rsqrt(jnp.mean(xv * xv, axis=-1, keepdims=True) + EPS)
            r_ref[...] = r
            out_ref[...] = ((xv * r) * g_ref[...]).astype(BF16)

        rms(zcq_ref, gq_ref, cq_ref, rq_ref)
        rms(zckv_ref, gkv_ref, ckv_ref, rkv_ref)
        kr = zkr_ref[...]
        kr_ref[...] = (kr * cos_ref[...] + _rotate_half(kr) * sin_ref[...]).astype(BF16)

    def zcol(width, off):
        return pl.BlockSpec((tb, width), lambda i: (i, off // width))

    def full(a):
        return pl.BlockSpec(a.shape, lambda i: (0,) * a.ndim)

    def out(width, dt):
        return pl.BlockSpec((tb, width), lambda i: (i, 0)), jax.ShapeDtypeStruct((s, width), dt)

    outs = [out(POOL_DIM, BF16), out(POOL_DIM, BF16), out(Q_LORA, BF16), out(KV_LORA, BF16), out(LANES, BF16),
            out(1, F32), out(1, F32)]
    return pl.pallas_call(
        body, name=name, grid=(s // tb,),
        in_specs=[zcol(Q_LORA, ZC_CQ), zcol(LANES, ZC_KR), zcol(POOL_DIM, ZC_U),
                  pl.BlockSpec((HALO, POOL_DIM), lambda i: (jnp.maximum(i * hb - 1, 0), ZC_U // POOL_DIM)),
                  zcol(KV_LORA, ZC_CKV),
                  pl.BlockSpec((tb, LANES), lambda i: (i, 0)), pl.BlockSpec((tb, LANES), lambda i: (i, 0)),
                  full(w_pool), full(pool_scale), full(gq), full(gkv)],
        out_specs=[o[0] for o in outs], out_shape=[o[1] for o in outs],
        compiler_params=_params(("parallel",)),
    )(z, z, z, z, z, cos_t, sin_t, w_pool, pool_scale, gq, gkv)


def _sigmoid(x):
    return 1.0 / (1.0 + jnp.exp(-x))


def _merge(z, ya, yb, *, name):
    s, d = ya.shape
    tb = _rows(s)

    def body(ga_ref, gb_ref, ya_ref, yb_ref, out_ref):
        out_ref[...] = (_sigmoid(ga_ref[...]) * ya_ref[...] + _sigmoid(gb_ref[...]) * yb_ref[...]).astype(BF16)

    blk = pl.BlockSpec((tb, d), lambda i: (i, 0))
    return pl.pallas_call(
        body, name=name, grid=(s // tb,),
        in_specs=[pl.BlockSpec((tb, d), lambda i: (i, ZC_GA // d)), pl.BlockSpec((tb, d), lambda i: (i, ZC_GB // d)),
                  blk, blk],
        out_specs=blk, out_shape=jax.ShapeDtypeStruct((s, d), BF16),
        compiler_params=_params(("parallel",)),
    )(z, z, ya, yb)


ATTN_SCALE = 1.0 / math.sqrt(QK_DIM)
NEG_BIG = -1e30


LOG2_E = math.log2(math.e)
EXP2_SCALE = ATTN_SCALE * LOG2_E
NT_DIMS = (((1,), (1,)), ((), ()))
TN_DIMS = (((0,), (0,)), ((), ()))


def _on_or_below_diagonal(t):
    return lax.broadcasted_iota(jnp.int32, (t, t), 0) >= lax.broadcasted_iota(jnp.int32, (t, t), 1)


def _attn_fwd(q, k, v, *, name):
    s = q.shape[0]
    t = _rows(s)

    def body(q_ref, k_ref, v_ref, o_ref, lse_ref):
        qi = pl.program_id(1)
        qv = q_ref[...]

        def block(j, carry, diagonal):
            m, l, acc = carry
            rows = pl.ds(pl.multiple_of(j * t, t), t)
            sc = lax.dot_general(qv, k_ref[rows, :], NT_DIMS, preferred_element_type=F32)
            if diagonal:
                sc = jnp.where(_on_or_below_diagonal(t), sc, NEG_BIG)
            m_new = jnp.maximum(m, jnp.max(sc, axis=-1, keepdims=True))
            p = jnp.exp2((sc - m_new) * EXP2_SCALE)
            alpha = jnp.exp2((m - m_new) * EXP2_SCALE)
            l = alpha * l + jnp.sum(p, axis=-1, keepdims=True)
            acc = alpha * acc + jnp.dot(p.astype(BF16), v_ref[rows, :], preferred_element_type=F32)
            return m_new, l, acc

        init = (jnp.full((t, 1), -jnp.inf, F32), jnp.zeros((t, 1), F32), jnp.zeros((t, HEAD_PAD), F32))
        carry = lax.fori_loop(0, qi, lambda j, c: block(j, c, False), init)
        m, l, acc = block(qi, carry, True)
        o_ref[...] = (acc / l).astype(BF16)
        lse_ref[0] = m * ATTN_SCALE + jnp.log(l)

    q_spec = pl.BlockSpec((t, HEAD_PAD), lambda h, i: (i, h))
    kv_spec = pl.BlockSpec((s, HEAD_PAD), lambda h, i: (0, h))
    return pl.pallas_call(
        body, name=name, grid=(N_HEADS, s // t),
        in_specs=[q_spec, kv_spec, kv_spec],
        out_specs=[q_spec, pl.BlockSpec((1, t, 1), lambda h, i: (h, i, 0))],
        out_shape=[jax.ShapeDtypeStruct((s, N_HEADS * HEAD_PAD), BF16), jax.ShapeDtypeStruct((N_HEADS, s, 1), F32)],
        compiler_params=_params(("parallel", "parallel")),
    )(q, k, v)


def _attn_delta(do, o, *, name):
    s = o.shape[0]
    t = _rows(s)

    def body(do_ref, o_ref, out_ref):
        out_ref[0] = jnp.sum(do_ref[...].astype(F32) * o_ref[...].astype(F32), axis=-1, keepdims=True)

    blk = pl.BlockSpec((t, HEAD_PAD), lambda h, i: (i, h))
    return pl.pallas_call(
        body, name=name, grid=(N_HEADS, s // t), in_specs=[blk, blk],
        out_specs=pl.BlockSpec((1, t, 1), lambda h, i: (h, i, 0)),
        out_shape=jax.ShapeDtypeStruct((N_HEADS, s, 1), F32),
        compiler_params=_params(("parallel", "parallel")),
    )(do, o)


def _attn_bwd(q, k, v, do, lse, delta, *, name):
    s = q.shape[0]
    t = _rows(s)
    nt = s // t

    def body(q_ref, k_ref, v_ref, do_ref, lse_ref, dl_ref, dq_ref, dk_ref, dv_ref):
        kj = pl.program_id(1)

        @pl.when(kj == 0)
        def _():
            dq_ref[...] = jnp.zeros_like(dq_ref)

        kv, vv = k_ref[...], v_ref[...]

        def block(i, carry, diagonal):
            dk, dv = carry
            rows = pl.ds(pl.multiple_of(i * t, t), t)
            qv, dov = q_ref[rows, :], do_ref[rows, :]
            sc = lax.dot_general(qv, kv, NT_DIMS, preferred_element_type=F32)
            p = jnp.exp2(sc * EXP2_SCALE - lse_ref[0, rows, :] * LOG2_E)
            if diagonal:
                p = jnp.where(_on_or_below_diagonal(t), p, 0.0)
            dp = lax.dot_general(dov, vv, NT_DIMS, preferred_element_type=F32)
            ds = (p * (dp - dl_ref[0, rows, :])).astype(BF16)
            dv = dv + lax.dot_general(p.astype(BF16), dov, TN_DIMS, preferred_element_type=F32)
            dk = dk + lax.dot_general(ds, qv, TN_DIMS, preferred_element_type=F32)
            dq_ref[rows, :] += jnp.dot(ds, kv, preferred_element_type=F32) * ATTN_SCALE
            return dk, dv

        zero = jnp.zeros((t, HEAD_PAD), F32)
        carry = block(kj, (zero, zero), True)
        dk, dv = lax.fori_loop(kj + 1, nt, lambda i, c: block(i, c, False), carry)
        dk_ref[...] = dk * ATTN_SCALE
        dv_ref[...] = dv.astype(BF16)

    full_spec = pl.BlockSpec((s, HEAD_PAD), lambda h, j: (0, h))
    kv_spec = pl.BlockSpec((t, HEAD_PAD), lambda h, j: (j, h))
    vec_spec = pl.BlockSpec((1, s, 1), lambda h, j: (h, 0, 0))
    wide = (s, N_HEADS * HEAD_PAD)
    return pl.pallas_call(
        body, name=name, grid=(N_HEADS, nt),
        in_specs=[full_spec, kv_spec, kv_spec, full_spec, vec_spec, vec_spec],
        out_specs=[full_spec, kv_spec, kv_spec],
        out_shape=[jax.ShapeDtypeStruct(wide, F32), jax.ShapeDtypeStruct(wide, F32), jax.ShapeDtypeStruct(wide, BF16)],
        compiler_params=_params(("parallel", "arbitrary")),
    )(q, k, v, do, lse, delta)


def _acc_specs(widths):
    return ([pl.BlockSpec((1, w), lambda i: (0, 0)) for w in widths],
            [jax.ShapeDtypeStruct((1, w), F32) for w in widths])


def _final_loss(x, g, target, *, name):
    s, d = x.shape
    tb = _rows(s)

    def body(x_ref, g_ref, t_ref, dx_ref, loss_ref, dg_ref):
        @pl.when(pl.program_id(0) == 0)
        def _():
            loss_ref[...] = jnp.zeros_like(loss_ref)
            dg_ref[...] = jnp.zeros_like(dg_ref)

        xv = x_ref[...]
        r = lax.rsqrt(jnp.mean(xv * xv, axis=-1, keepdims=True) + EPS)
        xn = xv * r
        err = xn * g_ref[...] - t_ref[...]
        loss_ref[...] += 0.5 * jnp.sum(jnp.mean(err * err, axis=-1, keepdims=True), axis=0, keepdims=True)
        dy = err / d
        dg_ref[...] += jnp.sum(dy * xn, axis=0, keepdims=True)
        dxn = dy * g_ref[...]
        dx_ref[...] = r * (dxn - xn * jnp.mean(dxn * xn, axis=-1, keepdims=True))

    blk = pl.BlockSpec((tb, d), lambda i: (i, 0))
    acc_specs, acc_shapes = _acc_specs((LANES, d))
    return pl.pallas_call(
        body, name=name, grid=(s // tb,),
        in_specs=[blk, pl.BlockSpec((1, d), lambda i: (0, 0)), blk],
        out_specs=[blk] + acc_specs, out_shape=[jax.ShapeDtypeStruct((s, d), F32)] + acc_shapes,
        compiler_params=_params(("arbitrary",)),
    )(x, g, target)


def _gate_bwd(dx, m, g, *, name):
    s, d = dx.shape
    tb = _rows(s)

    def body(dx_ref, m_ref, g_ref, dm_ref, dg_ref):
        @pl.when(pl.program_id(0) == 0)
        def _():
            dg_ref[...] = jnp.zeros_like(dg_ref)

        dxv = dx_ref[...]
        dm_ref[...] = (dxv * g_ref[...]).astype(BF16)
        dg_ref[...] += jnp.sum(dxv * m_ref[...], axis=0, keepdims=True)

    blk = pl.BlockSpec((tb, d), lambda i: (i, 0))
    acc_specs, acc_shapes = _acc_specs((d,))
    return pl.pallas_call(
        body, name=name, grid=(s // tb,),
        in_specs=[blk, blk, pl.BlockSpec((1, d), lambda i: (0, 0))],
        out_specs=[blk] + acc_specs, out_shape=[jax.ShapeDtypeStruct((s, d), BF16)] + acc_shapes,
        compiler_params=_params(("arbitrary",)),
    )(dx, m, g)


def _norm_mod_bwd(dh, x, r, g, sc, dx_skip, *, name):
    s, d = x.shape
    tb = _rows(s)
    nb = s // tb

    def body(dh_ref, x_ref, r_ref, g_ref, sc_ref, skip_ref, dx_ref, dg_ref, dsc_ref, dsh_ref, da_sc):
        i = pl.program_id(0)

        @pl.when(i == 0)
        def _():
            da_sc[...] = jnp.zeros_like(da_sc)
            dsh_ref[...] = jnp.zeros_like(dsh_ref)

        dhv, rv = dh_ref[...], r_ref[...]
        xn = x_ref[...] * rv
        dsh_ref[...] += jnp.sum(dhv, axis=0, keepdims=True)
        da_sc[...] += jnp.sum(dhv * xn, axis=0, keepdims=True)
        dxn = dhv * (g_ref[...] * (1.0 + sc_ref[...]))
        dx_ref[...] = skip_ref[...] + rv * (dxn - xn * jnp.mean(dxn * xn, axis=-1, keepdims=True))

        @pl.when(i == nb - 1)
        def _():
            dg_ref[...] = da_sc[...] * (1.0 + sc_ref[...])
            dsc_ref[...] = da_sc[...] * g_ref[...]

    blk = pl.BlockSpec((tb, d), lambda i: (i, 0))
    vec = pl.BlockSpec((1, d), lambda i: (0, 0))
    acc_specs, acc_shapes = _acc_specs((d, d, d))
    return pl.pallas_call(
        body, name=name, grid=(nb,),
        in_specs=[blk, blk, pl.BlockSpec((tb, 1), lambda i: (i, 0)), vec, vec, blk],
        out_specs=[blk] + acc_specs, out_shape=[jax.ShapeDtypeStruct((s, d), F32)] + acc_shapes,
        scratch_shapes=[pltpu.VMEM((1, d), F32)],
        compiler_params=_params(("arbitrary",)),
    )(dh, x, r, g, sc, dx_skip)


def _merge_bwd(dmerged, z, ya, yb, *, name):
    s, d = ya.shape
    tb = _rows(s)

    def body(dm_ref, ga_ref, gb_ref, ya_ref, yb_ref, dya_ref, dyb_ref, dga_ref, dgb_ref):
        dm = dm_ref[...]
        for g_ref, y_ref, dy_ref, dg_ref in ((ga_ref, ya_ref, dya_ref, dga_ref), (gb_ref, yb_ref, dyb_ref, dgb_ref)):
            sg = _sigmoid(g_ref[...])
            dy_ref[...] = (dm * sg).astype(BF16)
            dg_ref[...] = (dm * y_ref[...] * (sg * (1.0 - sg))).astype(BF16)

    blk = pl.BlockSpec((tb, d), lambda i: (i, 0))
    return pl.pallas_call(
        body, name=name, grid=(s // tb,),
        in_specs=[blk, pl.BlockSpec((tb, d), lambda i: (i, ZC_GA // d)), pl.BlockSpec((tb, d), lambda i: (i, ZC_GB // d)),
                  blk, blk],
        out_specs=[blk] * 4, out_shape=[jax.ShapeDtypeStruct((s, d), BF16)] * 4,
        compiler_params=_params(("parallel",)),
    )(dmerged, z, z, ya, yb)


def _pool_bwd(dyp, p, w_pool, pool_scale, *, name):
    s = dyp.shape[0]
    tb = _rows(s)
    nb = s // tb
    hb = tb // HALO
    nt_dims = (((1,), (1,)), ((), ()))
    tn_dims = (((0,), (0,)), ((), ()))

    def body(dy_ref, dyn_ref, p_ref, wp_ref, ps_ref, du_ref, gwp_ref, gps_ref):
        i = pl.program_id(0)

        @pl.when(i == 0)
        def _():
            gwp_ref[...] = jnp.zeros_like(gwp_ref)
            gps_ref[...] = jnp.zeros_like(gps_ref)

        cur = dy_ref[...]
        nxt = jnp.where(i < nb - 1, dyn_ref[...], 0.0)
        dpw = (jnp.concatenate([cur, nxt], axis=0) * ps_ref[...]).astype(BF16)
        t = i * tb + lax.broadcasted_iota(jnp.int32, (tb + HALO, 1), 0)
        for g, w in enumerate(POOL_WINDOWS):
            cols = slice(g * POOL_GROUP, (g + 1) * POOL_GROUP)
            wg = wp_ref[g].astype(BF16)
            dp = lax.dot_general(dpw[:, cols], wg, nt_dims, preferred_element_type=F32)
            e = dp / jnp.minimum(t + 1, w).astype(F32)
            lead = _window_sums(e, -1)[g]
            du_ref[:, cols] = (lead[:tb] - dp[:tb]).astype(BF16)
            pg = p_ref[:, cols]
            pw = jnp.dot(pg, wg, preferred_element_type=F32)
            gps_ref[:, cols] += jnp.sum(cur[:, cols] * pw, axis=0, keepdims=True)
            gwp_ref[g] += lax.dot_general(pg, dpw[:tb, cols], tn_dims, preferred_element_type=F32)

    blk = pl.BlockSpec((tb, POOL_DIM), lambda i: (i, 0))
    return pl.pallas_call(
        body, name=name, grid=(nb,),
        in_specs=[blk, pl.BlockSpec((HALO, POOL_DIM), lambda i: (jnp.minimum((i + 1) * hb, s // HALO - 1), 0)), blk,
                  pl.BlockSpec(w_pool.shape, lambda i: (0, 0, 0)), pl.BlockSpec((1, POOL_DIM), lambda i: (0, 0))],
        out_specs=[blk, pl.BlockSpec(w_pool.shape, lambda i: (0, 0, 0)), pl.BlockSpec((1, POOL_DIM), lambda i: (0, 0))],
        out_shape=[jax.ShapeDtypeStruct((s, POOL_DIM), BF16), jax.ShapeDtypeStruct(w_pool.shape, F32),
                   jax.ShapeDtypeStruct((1, POOL_DIM), F32)],
        compiler_params=_params(("arbitrary",)),
    )(dyp, dyp, p, w_pool, pool_scale)


def _rope_bwd_q(dq, cos_t, sin_t, *, name):
    s = dq.shape[0]
    tb = _rows(s)

    def body(dq_ref, cos_ref, sin_ref, out_ref):
        dqv = dq_ref[...]
        lane = lax.broadcasted_iota(jnp.int32, dqv.shape, 1)
        cos_q = cos_ref[...] + jnp.where(lane < QK_NOPE, 1.0, 0.0)
        out_ref[...] = (dqv * cos_q - _rotate_half(dqv * sin_ref[...])).astype(BF16)

    blk = pl.BlockSpec((tb, HEAD_PAD), lambda i, h: (i, h))
    tab = pl.BlockSpec((tb, LANES), lambda i, h: (i, 0))
    return pl.pallas_call(
        body, name=name, grid=(s // tb, N_HEADS), in_specs=[blk, tab, tab], out_specs=blk,
        out_shape=jax.ShapeDtypeStruct(dq.shape, BF16),
        compiler_params=_params(("parallel", "parallel")),
    )(dq, cos_t, sin_t)


def _key_bwd(dk, cos_t, sin_t, *, name):
    s = dk.shape[0]
    tb = _rows(s)

    def body(dk_ref, cos_ref, sin_ref, dkb_ref, dkr_ref):
        dkv = dk_ref[...]
        dkb_ref[...] = dkv.astype(BF16)
        tot = dkv[:, :HEAD_PAD]
        for h in range(1, N_HEADS):
            tot = tot + dkv[:, h * HEAD_PAD:(h + 1) * HEAD_PAD]
        dkr_ref[...] = (tot * cos_ref[...] - _rotate_half(tot * sin_ref[...])).astype(BF16)

    blk = pl.BlockSpec((tb, N_HEADS * HEAD_PAD), lambda i: (i, 0))
    tab = pl.BlockSpec((tb, LANES), lambda i: (i, 0))
    return pl.pallas_call(
        body, name=name, grid=(s // tb,), in_specs=[blk, tab, tab], out_specs=[blk, tab],
        out_shape=[jax.ShapeDtypeStruct(dk.shape, BF16), jax.ShapeDtypeStruct((s, LANES), BF16)],
        compiler_params=_params(("parallel",)),
    )(dk, cos_t, sin_t)


def _rms_bwd(dy, z, z_off, r, g, *, name):
    s, n = dy.shape
    tb = _rows(s)

    def body(dy_ref, x_ref, r_ref, g_ref, dx_ref, dg_ref):
        @pl.when(pl.program_id(0) == 0)
        def _():
            dg_ref[...] = jnp.zeros_like(dg_ref)

        dyv, rv = dy_ref[...], r_ref[...]
        xn = x_ref[...] * rv
        dg_ref[...] += jnp.sum(dyv * xn, axis=0, keepdims=True)
        dxn = dyv * g_ref[...]
        dx_ref[...] = (rv * (dxn - xn * jnp.mean(dxn * xn, axis=-1, keepdims=True))).astype(BF16)

    blk = pl.BlockSpec((tb, n), lambda i: (i, 0))
    acc_specs, acc_shapes = _acc_specs((n,))
    return pl.pallas_call(
        body, name=name, grid=(s // tb,),
        in_specs=[blk, pl.BlockSpec((tb, n), lambda i: (i, z_off // n)), pl.BlockSpec((tb, 1), lambda i: (i, 0)),
                  pl.BlockSpec((1, n), lambda i: (0, 0))],
        out_specs=[blk] + acc_specs, out_shape=[jax.ShapeDtypeStruct((s, n), BF16)] + acc_shapes,
        compiler_params=_params(("arbitrary",)),
    )(dy, z, r, g)


def _silu(c, *, name):
    def body(c_ref, out_ref):
        cv = c_ref[...]
        out_ref[...] = (cv * _sigmoid(cv)).astype(BF16)

    return pl.pallas_call(body, name=name, out_shape=jax.ShapeDtypeStruct(c.shape, BF16),
                          compiler_params=_params())(c)


def _sum_slots(a, n, *, name, out_dtype=F32):
    _, rows, cols = a.shape
    tr = _tile(rows, 256, 8)

    def body(a_ref, out_ref):
        tot = a_ref[0].astype(F32)
        for j in range(1, n):
            tot = tot + a_ref[j].astype(F32)
        out_ref[...] = tot.astype(out_dtype)

    return pl.pallas_call(
        body, name=name, grid=(rows // tr,),
        in_specs=[pl.BlockSpec((n, tr, cols), lambda i: (0, i, 0))],
        out_specs=pl.BlockSpec((tr, cols), lambda i: (i, 0)),
        out_shape=jax.ShapeDtypeStruct((rows, cols), out_dtype),
        compiler_params=_params(("parallel",)),
    )(a)


def _add2(a, b, *, name):
    rows, cols = a.shape
    tr = _tile(rows, 256, 8)

    def body(a_ref, b_ref, out_ref):
        out_ref[...] = a_ref[...] + b_ref[...]

    blk = pl.BlockSpec((tr, cols), lambda i: (i, 0))
    return pl.pallas_call(
        body, name=name, grid=(rows // tr,), in_specs=[blk, blk], out_specs=blk,
        out_shape=jax.ShapeDtypeStruct((rows, cols), F32),
        compiler_params=_params(("parallel",)),
    )(a, b)


def _adamw(w, g, m, v, *, name):
    rows, cols = w.shape
    tr = _tile(rows, max(8, (1 << 18) // cols), 8)
    c1 = 1.0 - ADAM_B1 ** ADAM_STEP
    c2 = 1.0 - ADAM_B2 ** ADAM_STEP

    def body(w_ref, g_ref, m_ref, v_ref, d_ref, nm_ref, nv_ref):
        gv = g_ref[...]
        nm = ADAM_B1 * m_ref[...] + (1.0 - ADAM_B1) * gv
        nv = ADAM_B2 * v_ref[...] + (1.0 - ADAM_B2) * (gv * gv)
        nm_ref[...] = nm
        nv_ref[...] = nv
        d_ref[...] = -ADAM_LR * ((nm / c1) / (jnp.sqrt(nv / c2) + ADAM_EPS) + ADAM_WD * w_ref[...])

    blk = pl.BlockSpec((tr, cols), lambda i: (i, 0))
    return pl.pallas_call(
        body, name=name, grid=(rows // tr,), in_specs=[blk] * 4, out_specs=[blk] * 3,
        out_shape=[jax.ShapeDtypeStruct((rows, cols), F32)] * 3,
        compiler_params=_params(("parallel",)),
    )(w, g, m, v)


def _coords():
    return lax.axis_index("x"), lax.axis_index("y"), lax.axis_index("c")


def _other_chips(x, y):
    return [(1 - x, y), (x, 1 - y), (1 - x, 1 - y)]


def _all_gather_small(blk, *, name):
    m_per, n = blk.shape

    def body(x_ref, out_ref, send_sems, recv_sems, local_sem):
        x, y, c = _coords()
        me, sibling = (x, y, c), (x, y, 1 - c)
        chips = _other_chips(x, y)

        def rows(px, py, pc):
            return out_ref.at[pl.ds((4 * px + 2 * py + pc) * m_per, m_per), :]

        def copy(k, block, to, src=None):
            return pltpu.make_async_remote_copy(
                src_ref=rows(*block) if src is None else src, dst_ref=rows(*block),
                send_sem=send_sems.at[k], recv_sem=recv_sems.at[k], device_id=to, device_id_type=MESH)

        mine = pltpu.make_async_copy(x_ref, rows(*me), local_sem)
        mine.start()
        first = [copy(0, me, sibling, src=x_ref)]
        first += [copy(1 + j, me, (*chip, c), src=x_ref) for j, chip in enumerate(chips)]
        for cp in first:
            cp.start()
        passed = [copy(4 + j, (*chip, c), sibling) for j, chip in enumerate(chips)]
        for j, chip in enumerate(chips):
            copy(1 + j, (*chip, c), me).wait_recv()
            passed[j].start()
        copy(0, sibling, me).wait_recv()
        for j, chip in enumerate(chips):
            copy(4 + j, (*chip, 1 - c), me).wait_recv()
        for cp in first + passed:
            cp.wait_send()
        mine.wait()

    return pl.pallas_call(
        body, name=name,
        out_shape=jax.ShapeDtypeStruct((N_DEV * m_per, n), blk.dtype),
        in_specs=[pl.BlockSpec(memory_space=pltpu.VMEM)],
        out_specs=pl.BlockSpec(memory_space=pltpu.VMEM),
        scratch_shapes=[pltpu.SemaphoreType.DMA((7,)), pltpu.SemaphoreType.DMA((7,)), pltpu.SemaphoreType.DMA],
        compiler_params=_params(),
    )(blk)


HBM_SPEC = pl.BlockSpec(memory_space=pltpu.HBM)
SEM_SPEC = pl.BlockSpec(memory_space=pltpu.SEMAPHORE)
DATAFLOW = pltpu.SideEffectType.DATAFLOW_SIDE_EFFECTING


def _chip_copies(src_ref, land_ref, send_sems, recv_sems, scatter):
    x, y, c = _coords()
    my = 2 * x + y
    outgoing, incoming = [], []
    for k, (px, py) in enumerate(_other_chips(x, y)):
        peer = 2 * px + py

        def copy(src_slot, dst_slot):
            return pltpu.make_async_remote_copy(
                src_ref=src_ref.at[src_slot] if scatter else src_ref, dst_ref=land_ref.at[dst_slot],
                send_sem=send_sems.at[k], recv_sem=recv_sems.at[k], device_id=(px, py, c), device_id_type=MESH)

        outgoing.append(copy(peer, my))
        incoming.append(copy(my, peer))
    return outgoing, incoming


def _exchange_start(src, *, name, scatter):
    land_shape = src.shape if scatter else (N_CHIPS,) + src.shape

    def body(src_ref, land_ref, send_sems, recv_sems, src_thru, land_thru, token):
        outgoing, _ = _chip_copies(src_ref, land_ref, send_sems, recv_sems, scatter)
        for cp in outgoing:
            cp.start()
        token[...] = jnp.zeros_like(token)

    return pl.pallas_call(
        body, name=name,
        out_shape=(pltpu.SemaphoreType.DMA((N_CHIPS - 1,)), pltpu.SemaphoreType.DMA((N_CHIPS - 1,)),
                   pltpu.HBM(src.shape, src.dtype), pltpu.HBM(land_shape, src.dtype), jax.ShapeDtypeStruct((8, LANES), F32)),
        in_specs=(HBM_SPEC, HBM_SPEC),
        out_specs=(SEM_SPEC, SEM_SPEC, HBM_SPEC, HBM_SPEC, pl.BlockSpec(memory_space=pltpu.VMEM)),
        input_output_aliases={0: 2, 1: 3},
        compiler_params=pltpu.CompilerParams(has_side_effects=DATAFLOW),
    )(pltpu.with_memory_space_constraint(src, pltpu.HBM),
      pltpu.with_memory_space_constraint(lax.empty(land_shape, src.dtype), pltpu.HBM))


def _exchange_wait(started, after, *, name, scatter):
    send_sems, recv_sems, src_thru, land_thru, _ = started

    def body(src_ref, land_ref, send_sems, recv_sems, after_ref, src_dead, got_ref):
        outgoing, incoming = _chip_copies(src_ref, land_ref, send_sems, recv_sems, scatter)
        for cp in outgoing:
            cp.wait_send()
        for cp in incoming:
            cp.wait_recv()

    return pl.pallas_call(
        body, name=name,
        out_shape=(pltpu.HBM(src_thru.shape, src_thru.dtype), pltpu.HBM(land_thru.shape, land_thru.dtype)),
        in_specs=(HBM_SPEC, HBM_SPEC, SEM_SPEC, SEM_SPEC, pl.BlockSpec(memory_space=pl.ANY)),
        out_specs=(HBM_SPEC, HBM_SPEC),
        input_output_aliases={0: 0, 1: 1},
        compiler_params=pltpu.CompilerParams(has_side_effects=DATAFLOW),
    )(src_thru, land_thru, send_sems, recv_sems, after)[1]


def _exchange_sibling(src, *, name):
    def body(src_ref, out_ref, send_sem, recv_sem):
        x, y, c = _coords()
        cp = pltpu.make_async_remote_copy(src_ref=src_ref, dst_ref=out_ref, send_sem=send_sem, recv_sem=recv_sem,
                                          device_id=(x, y, 1 - c), device_id_type=MESH)
        cp.start()
        cp.wait()

    return pl.pallas_call(
        body, name=name,
        out_shape=jax.ShapeDtypeStruct(src.shape, src.dtype),
        in_specs=[pl.BlockSpec(memory_space=pl.ANY)],
        out_specs=pl.BlockSpec(memory_space=pl.ANY),
        scratch_shapes=[pltpu.SemaphoreType.DMA, pltpu.SemaphoreType.DMA],
        compiler_params=_params(),
    )(src)


def _pack_rows(a):
    return a.reshape(-1, D_MODEL)


def _pad_heads(w, width):
    r = w.shape[0]
    return jnp.pad(w, ((0, 0), (0, 0), (0, HEAD_PAD - width))).reshape(r, N_HEADS * HEAD_PAD)


MIXER_NAMES = ("w_in", "w_uq", "w_uk", "w_uv", "p_pool", "p_attn", "w_out")
FFN_NAMES = ("w_ff1", "w_ff2")
ROWS_OF = dict(PACK_ROWS)


def _unpack_weights(gathered, names):
    w, off = {}, 0
    for name in names:
        w[name] = gathered[:, off:off + ROWS_OF[name]]
        off += ROWS_OF[name]

    def cols(a, k):
        return a.reshape(N_CHIPS, k, -1).transpose(1, 0, 2).reshape(k, -1)

    if names == FFN_NAMES:
        return dict(w_ff1=cols(w["w_ff1"], D_MODEL), w_ff2=w["w_ff2"].reshape(D_FF, D_MODEL))
    w_in = cols(w["w_in"], D_MODEL)
    kr = jnp.pad(w_in[:, 1152:1184], ((0, 0), (QK_NOPE, HEAD_PAD - QK_DIM)))
    return dict(
        w_in=jnp.concatenate([w_in[:, 512:896], kr, w_in[:, 0:512], w_in[:, 1184:3232], w_in[:, 896:1152]], axis=1),
        w_uq=_pad_heads(w["w_uq"].reshape(Q_LORA, N_HEADS, QK_DIM), QK_DIM),
        w_uk=_pad_heads(w["w_uk"].reshape(KV_LORA, N_HEADS, QK_NOPE), QK_NOPE),
        w_uv=_pad_heads(w["w_uv"].reshape(KV_LORA, N_HEADS, V_DIM), V_DIM),
        p_pool=cols(w["p_pool"], POOL_DIM),
        p_attn=jnp.pad(cols(w["p_attn"], ATTN_DIM).reshape(N_HEADS, V_DIM, D_MODEL),
                       ((0, 0), (0, HEAD_PAD - V_DIM), (0, 0))).reshape(N_HEADS * HEAD_PAD, D_MODEL),
        w_out=w["w_out"].reshape(D_MODEL, D_MODEL),
    )


def _pack_grads(g, names):
    def cols(a):
        k = a.shape[0]
        return a.reshape(k, N_CHIPS, -1).transpose(1, 0, 2).reshape(N_CHIPS, -1, D_MODEL)

    def rows(a):
        return a.reshape(N_CHIPS, -1, D_MODEL)

    def heads(a, width):
        return a.reshape(a.shape[0], N_HEADS, HEAD_PAD)[:, :, :width]

    if names == FFN_NAMES:
        return jnp.concatenate([cols(g["w_ff1"]), rows(g["w_ff2"])], axis=1)
    gi = g["w_in"]
    w_in = jnp.concatenate([gi[:, ZC_U:ZC_U + 512], gi[:, ZC_CQ:ZC_CQ + 384], gi[:, ZC_CKV:ZC_CKV + 256],
                            gi[:, ZC_KR + QK_NOPE:ZC_KR + QK_DIM], gi[:, ZC_GA:ZC_GA + 2048]], axis=1)
    p_attn = g["p_attn"].reshape(N_HEADS, HEAD_PAD, D_MODEL)[:, :V_DIM].reshape(ATTN_DIM, D_MODEL)
    parts = dict(w_in=cols(w_in), w_uq=rows(heads(g["w_uq"], QK_DIM)), w_uk=rows(heads(g["w_uk"], QK_NOPE)),
                 w_uv=rows(heads(g["w_uv"], V_DIM)), p_pool=cols(g["p_pool"]), p_attn=cols(p_attn),
                 w_out=rows(g["w_out"]))
    return jnp.concatenate([parts[name] for name in names], axis=1)


def _layer_fwd(l, x, mod, w, get_ffn_weights, small, cos_t, sin_t):
    sh1, sc1, g1, sh2, sc2, g2 = mod
    tag = f"_l{l}"
    h, r1 = _norm_mod(x, small["ln1_g"], sc1, sh1, name="norm1" + tag)
    (z,) = _mm(h, w["w_in"], name="in_proj" + tag)
    p, yp, cq, ckv, kr, rq, rkv = _mixer_pre(z, cos_t, sin_t, small["w_pool"], small["pool_scale"],
                                              small["q_norm_g"], small["kv_norm_g"], name="mixer_pre" + tag)
    (ya,) = _mm(yp, w["p_pool"], name="pool_out" + tag)

    def rope_q(acc, cos, sin):
        lane = lax.broadcasted_iota(jnp.int32, acc.shape, 1)
        return (acc * (cos + jnp.where(lane < QK_NOPE, 1.0, 0.0)) + _rotate_half(acc) * sin,)

    (q,) = _mm(cq, w["w_uq"], name="q_proj" + tag, out_dtypes=(BF16,), epilogue=rope_q,
               extras=((cos_t, "head"), (sin_t, "head")), tn=HEAD_PAD)
    (k,) = _mm(ckv, w["w_uk"], name="k_proj" + tag, out_dtypes=(BF16,), epilogue=lambda acc, krv: (acc + krv,),
               extras=((kr, "head"),), tn=HEAD_PAD)
    (v,) = _mm(ckv, w["w_uv"], name="v_proj" + tag, out_dtypes=(BF16,))
    o, lse = _attn_fwd(q, k, v, name="attn_fwd" + tag)
    (yb,) = _mm(o, w["p_attn"], name="attn_out" + tag)
    merged = _merge(z, ya, yb, name="merge" + tag)
    mo, x1 = _mm(merged, w["w_out"], name="mix_out" + tag, out_dtypes=(F32, F32),
                 epilogue=lambda acc, xr, g: (acc, xr + g * acc), extras=((x, "tile"), (g1, "row")), tm=512)
    h2, r2 = _norm_mod(x1, small["ln2_g"], sc2, sh2, name="norm2" + tag)
    w.update(get_ffn_weights(merged))
    f, act = _mm(h2, w["w_ff1"], name="ff1" + tag, out_dtypes=(F32, BF16),
                 epilogue=lambda acc: (acc, jnp.square(jnp.maximum(acc, 0.0))))
    m2, x2 = _mm(act, w["w_ff2"], name="ff2" + tag, out_dtypes=(F32, F32),
                 epilogue=lambda acc, xr, g: (acc, xr + g * acc), extras=((x1, "tile"), (g2, "row")), tm=512)
    saved = dict(x=x, h=h, r1=r1, z=z, p=p, yp=yp, cq=cq, ckv=ckv, rq=rq, rkv=rkv, ya=ya, q=q, k=k, v=v, o=o, lse=lse,
                 yb=yb, merged=merged, mo=mo, x1=x1, h2=h2, r2=r2, f=f, act=act, m2=m2)
    return x2, saved


def _layer_bwd(l, dx2, sv, mod, w, small, cos_t, sin_t, send_ffn_grads):
    sh1, sc1, g1, sh2, sc2, g2 = mod
    tag = f"_l{l}"
    gw = {}
    dm2, dg2 = _gate_bwd(dx2, sv["m2"], g2, name="gate2_bwd" + tag)
    (df,) = _mm(dm2, w["w_ff2"], tb=True, name="ff2_dx" + tag, out_dtypes=(BF16,),
                epilogue=lambda acc, f: (acc * (2.0 * jnp.maximum(f, 0.0)),), extras=((sv["f"], "tile"),))
    (g_ff2,) = _mm(sv["act"], dm2, ta=True, name="ff2_dw" + tag, out_dtypes=(BF16,))
    (g_ff1,) = _mm(sv["h2"], df, ta=True, name="ff1_dw" + tag, out_dtypes=(BF16,))
    sc2 = sc2 + send_ffn_grads(dict(w_ff1=g_ff1, w_ff2=g_ff2))
    (dh2,) = _mm(df, w["w_ff1"], tb=True, name="ff1_dx" + tag)
    dx1, dln2, dsc2, dsh2 = _norm_mod_bwd(dh2, sv["x1"], sv["r2"], small["ln2_g"], sc2, dx2, name="norm2_bwd" + tag)
    dmo, dg1 = _gate_bwd(dx1, sv["mo"], g1, name="gate1_bwd" + tag)
    (dmerged,) = _mm(dmo, w["w_out"], tb=True, name="mix_out_dx" + tag)
    (gw["w_out"],) = _mm(sv["merged"], dmo, ta=True, name="mix_out_dw" + tag, out_dtypes=(BF16,))
    dya, dyb, dga, dgb = _merge_bwd(dmerged, sv["z"], sv["ya"], sv["yb"], name="merge_bwd" + tag)
    (gw["p_pool"],) = _mm(sv["yp"], dya, ta=True, name="pool_out_dw" + tag, out_dtypes=(BF16,))
    (dyp,) = _mm(dya, w["p_pool"], tb=True, name="pool_out_dx" + tag)
    du, g_w_pool, g_pool_scale = _pool_bwd(dyp, sv["p"], small["w_pool"], small["pool_scale"], name="pool_bwd" + tag)
    (gw["p_attn"],) = _mm(sv["o"], dyb, ta=True, name="attn_out_dw" + tag, out_dtypes=(BF16,))
    (do,) = _mm(dyb, w["p_attn"], tb=True, name="attn_out_dx" + tag, out_dtypes=(BF16,))
    delta = _attn_delta(do, sv["o"], name="attn_delta" + tag)
    dq, dk, dv = _attn_bwd(sv["q"], sv["k"], sv["v"], do, sv["lse"], delta, name="attn_bwd" + tag)
    dql = _rope_bwd_q(dq, cos_t, sin_t, name="rope_bwd_q" + tag)
    dkb, dkr = _key_bwd(dk, cos_t, sin_t, name="key_bwd" + tag)
    (gw["w_uq"],) = _mm(sv["cq"], dql, ta=True, name="q_proj_dw" + tag, out_dtypes=(BF16,))
    (gw["w_uk"],) = _mm(sv["ckv"], dkb, ta=True, name="k_proj_dw" + tag, out_dtypes=(BF16,))
    (gw["w_uv"],) = _mm(sv["ckv"], dv, ta=True, name="v_proj_dw" + tag, out_dtypes=(BF16,))
    (dcq,) = _mm(dql, w["w_uq"], tb=True, name="q_proj_dx" + tag)
    (dckv,) = _mm(jnp.concatenate([dkb, dv], axis=1), jnp.concatenate([w["w_uk"], w["w_uv"]], axis=1), tb=True,
                  name="kv_proj_dx" + tag)
    dcq_raw, g_qn = _rms_bwd(dcq, sv["z"], ZC_CQ, sv["rq"], small["q_norm_g"], name="q_norm_bwd" + tag)
    dckv_raw, g_kvn = _rms_bwd(dckv, sv["z"], ZC_CKV, sv["rkv"], small["kv_norm_g"], name="kv_norm_bwd" + tag)
    dz = jnp.concatenate([dcq_raw, dkr, du, dga, dgb, dckv_raw], axis=1)
    (gw["w_in"],) = _mm(sv["h"], dz, ta=True, name="in_proj_dw" + tag, out_dtypes=(BF16,))
    (dh,) = _mm(dz, w["w_in"], tb=True, name="in_proj_dx" + tag)
    dx, dln1, dsc1, dsh1 = _norm_mod_bwd(dh, sv["x"], sv["r1"], small["ln1_g"], sc1, dx1, name="norm1_bwd" + tag)
    dmod = jnp.concatenate([dsh1, dsc1, dg1, dsh2, dsc2, dg2], axis=0)
    gsmall = dict(ln1_g=dln1, ln2_g=dln2, q_norm_g=g_qn, kv_norm_g=g_kvn, w_pool=g_w_pool, pool_scale=g_pool_scale)
    return dx, gw, dmod, gsmall


SMALL_SINGLES = 16
SMALL_POOL = 24
SMALL_POOL_ROWS = len(POOL_WINDOWS) * POOL_GROUP * POOL_GROUP // D_MODEL
SMALL_ROWS = SMALL_POOL + DEPTH * SMALL_POOL_ROWS


def _pack_small(parts, *, name):
    def body(*refs):
        out_ref = refs[-1]
        out_ref[...] = jnp.zeros_like(out_ref)
        for ref, (_, row) in zip(refs[:-1], parts):
            out_ref[row:row + ref.shape[0], :] = ref[...]

    return pl.pallas_call(body, name=name, out_shape=jax.ShapeDtypeStruct((SMALL_ROWS, D_MODEL), F32),
                          compiler_params=_params())(*[a for a, _ in parts])


def kernel(x, c, positions, ln1_g, ln2_g, w_ada, b_ada, w_in, q_norm_g, w_uq, kv_norm_g, w_uk, w_uv, w_pool, pool_scale, p_pool, p_attn, w_out, w_ff1, w_ff2, final_g, loss_target, m_ln1_g, m_ln2_g, m_w_ada, m_b_ada, m_w_in, m_q_norm_g, m_w_uq, m_kv_norm_g, m_w_uk, m_w_uv, m_w_pool, m_pool_scale, m_p_pool, m_p_attn, m_w_out, m_w_ff1, m_w_ff2, m_final_g, v_ln1_g, v_ln2_g, v_w_ada, v_b_ada, v_w_in, v_q_norm_g, v_w_uq, v_kv_norm_g, v_w_uk, v_w_uv, v_w_pool, v_pool_scale, v_p_pool, v_p_attn, v_w_out, v_w_ff1, v_w_ff2, v_final_g):
    weights = dict(ln1_g=ln1_g, ln2_g=ln2_g, w_ada=w_ada, b_ada=b_ada, w_in=w_in, q_norm_g=q_norm_g, w_uq=w_uq,
                   kv_norm_g=kv_norm_g, w_uk=w_uk, w_uv=w_uv, w_pool=w_pool, pool_scale=pool_scale, p_pool=p_pool,
                   p_attn=p_attn, w_out=w_out, w_ff1=w_ff1, w_ff2=w_ff2, final_g=final_g)
    moms = dict(ln1_g=m_ln1_g, ln2_g=m_ln2_g, w_ada=m_w_ada, b_ada=m_b_ada, w_in=m_w_in, q_norm_g=m_q_norm_g,
                w_uq=m_w_uq, kv_norm_g=m_kv_norm_g, w_uk=m_w_uk, w_uv=m_w_uv, w_pool=m_w_pool,
                pool_scale=m_pool_scale, p_pool=m_p_pool, p_attn=m_p_attn, w_out=m_w_out, w_ff1=m_w_ff1,
                w_ff2=m_w_ff2, final_g=m_final_g)
    vels = dict(ln1_g=v_ln1_g, ln2_g=v_ln2_g, w_ada=v_w_ada, b_ada=v_b_ada, w_in=v_w_in, q_norm_g=v_q_norm_g,
                w_uq=v_w_uq, kv_norm_g=v_kv_norm_g, w_uk=v_w_uk, w_uv=v_w_uv, w_pool=v_w_pool,
                pool_scale=v_pool_scale, p_pool=v_p_pool, p_attn=v_p_attn, w_out=v_w_out, w_ff1=v_w_ff1,
                w_ff2=v_w_ff2, final_g=v_final_g)
    order = list(weights)
    seq = x.shape[1]
    my_chip = 2 * lax.axis_index("x") + lax.axis_index("y")
    my_dev = 2 * my_chip + lax.axis_index("c")
    ada_cols = w_ada.shape[2]

    groups = [(l, names) for l in range(DEPTH) for names in (MIXER_NAMES, FFN_NAMES)]

    def tag_of(l, names):
        return f"_l{l}_" + ("mixer" if names is MIXER_NAMES else "ffn")

    local, started = {}, {}
    for l, names in groups:
        local[l, names] = jnp.concatenate([_pack_rows(weights[n][l]) for n in names], axis=0).astype(BF16)
        started[l, names] = _exchange_start(local[l, names], name="weights_send" + tag_of(l, names), scatter=False)
    pin = sum(st[4][0:1, 0:1] for st in started.values())

    def gathered_weights(l, names, after):
        land = _exchange_wait(started[l, names], after, name="weights_wait" + tag_of(l, names), scatter=False)
        land = lax.dynamic_update_slice_in_dim(land, local[l, names][None], my_chip, axis=0)
        return _unpack_weights(land, names)

    small = [dict(ln1_g=ln1_g[l:l + 1], ln2_g=ln2_g[l:l + 1], q_norm_g=q_norm_g[l:l + 1], kv_norm_g=kv_norm_g[l:l + 1],
                  w_pool=w_pool[l], pool_scale=pool_scale[l:l + 1]) for l in range(DEPTH)]

    c_all = _all_gather_small(jnp.pad(c + pin, ((0, 7), (0, 0))), name="cond_all_gather")
    c_act = _silu(c_all, name="cond_silu")
    b_mine = lax.dynamic_slice_in_dim(b_ada, my_chip * ada_cols, ada_cols, axis=1)
    mod_parts = [_mm(c_act, w_ada[l], name=f"ada_fwd_l{l}", epilogue=lambda acc, b: (acc + b,),
                     extras=((b_mine[l:l + 1], "row"),))[0] for l in range(DEPTH)]
    mod_mine = jnp.concatenate([mp[::8] for mp in mod_parts], axis=0)
    mod_all = _all_gather_small(mod_mine, name="mod_all_gather").reshape(N_DEV, DEPTH, N_DEV, ada_cols)
    mods = []
    for l in range(DEPTH):
        row = jnp.concatenate([lax.dynamic_index_in_dim(mod_all[2 * j, l], my_dev, axis=0, keepdims=True)
                               for j in range(N_CHIPS)], axis=1)
        mods.append([row[:, i * D_MODEL:(i + 1) * D_MODEL] for i in range(N_MOD)])

    inv_freq = ROPE_THETA ** (-jnp.arange(0, QK_ROPE, 2, dtype=F32) / QK_ROPE)
    freq_lanes = jnp.concatenate([jnp.zeros((QK_NOPE,), F32), inv_freq, inv_freq,
                                  jnp.zeros((HEAD_PAD - QK_DIM,), F32)]).reshape(1, LANES)
    cos_t, sin_t = _rope_tables(positions.reshape(seq, 1), freq_lanes, name="rope_tables")

    xs, saved, wl = x.reshape(seq, D_MODEL), [], []
    for l in range(DEPTH):
        w_l = gathered_weights(l, MIXER_NAMES, cos_t if l == 0 else xs)
        xs, sv = _layer_fwd(l, xs, mods[l], w_l, functools.partial(gathered_weights, l, FFN_NAMES), small[l], cos_t, sin_t)
        saved.append(sv)
        wl.append(w_l)
    dx, loss_part, g_final = _final_loss(xs, final_g.reshape(1, D_MODEL), loss_target.reshape(seq, D_MODEL),
                                         name="final_loss")
    loss = lax.psum(loss_part[0, 0], ("x", "y", "c"))

    sent = {}

    def send_grads(l, names, g):
        gpack = _pack_grads(g, names)
        sent[l, names] = (_exchange_start(gpack, name="grads_send" + tag_of(l, names), scatter=True), gpack)
        return sent[l, names][0][4][0:1, 0:1]

    dmod, gsmall, token = [None] * DEPTH, [None] * DEPTH, jnp.zeros((1, 1), F32)
    for l in reversed(range(DEPTH)):
        mod_l = list(mods[l])
        mod_l[5] = mod_l[5] + token
        dx, g_mixer, dmod[l], gsmall[l] = _layer_bwd(l, dx, saved[l], mod_l, wl[l], small[l], cos_t, sin_t,
                                                     functools.partial(send_grads, l, FFN_NAMES))
        token = send_grads(l, MIXER_NAMES, g_mixer)
    grads = dict(x=dx.reshape(1, seq, D_MODEL))

    def lanes(a):
        flat = a.reshape(1, -1)
        return jnp.pad(flat, ((0, 0), (0, D_MODEL - flat.shape[1])))

    singles = [gsmall[0]["ln1_g"], gsmall[1]["ln1_g"], gsmall[0]["ln2_g"], gsmall[1]["ln2_g"], g_final + token,
               lanes(jnp.concatenate([gsmall[l]["pool_scale"] for l in range(DEPTH)], axis=1)),
               lanes(jnp.concatenate([gsmall[l]["q_norm_g"] for l in range(DEPTH)], axis=1)),
               lanes(jnp.concatenate([gsmall[l]["kv_norm_g"] for l in range(DEPTH)], axis=1))]
    parts = [(dmod[0], 0), (dmod[1], 8)] + [(a, SMALL_SINGLES + i) for i, a in enumerate(singles)]
    parts += [(gsmall[l]["w_pool"].reshape(-1, D_MODEL), SMALL_POOL + l * SMALL_POOL_ROWS) for l in range(DEPTH)]
    small_all = _all_gather_small(_pack_small(parts, name="small_grads_pack"), name="small_grads_all_gather")
    small_all = small_all.reshape(N_DEV, SMALL_ROWS, D_MODEL)
    ssum = _sum_slots(small_all, N_DEV, name="small_grads_sum")
    grads["b_ada"] = jnp.stack([ssum[8 * l:8 * l + N_MOD] for l in range(DEPTH)]).reshape(DEPTH, N_MOD * D_MODEL)
    grads["ln1_g"] = ssum[SMALL_SINGLES:SMALL_SINGLES + 2]
    grads["ln2_g"] = ssum[SMALL_SINGLES + 2:SMALL_SINGLES + 4]
    grads["final_g"] = ssum[SMALL_SINGLES + 4]
    grads["pool_scale"] = ssum[SMALL_SINGLES + 5].reshape(DEPTH, POOL_DIM)
    grads["q_norm_g"] = ssum[SMALL_SINGLES + 6, :DEPTH * Q_LORA].reshape(DEPTH, Q_LORA)
    grads["kv_norm_g"] = ssum[SMALL_SINGLES + 7, :DEPTH * KV_LORA].reshape(DEPTH, KV_LORA)
    grads["w_pool"] = ssum[SMALL_POOL:SMALL_ROWS].reshape(w_pool.shape)

    gsum, after = {}, ssum
    for l, names in reversed(groups):
        tg = tag_of(l, names)
        started_g, gpack = sent[l, names]
        land = _exchange_wait(started_g, after, name="grads_wait" + tg, scatter=True)
        own = lax.dynamic_index_in_dim(gpack, my_chip, axis=0, keepdims=True)
        land = lax.dynamic_update_slice_in_dim(land, own, my_chip, axis=0)
        part = _sum_slots(land, N_CHIPS, name="grads_sum_chips" + tg)
        gsum[l, names] = _add2(part, _exchange_sibling(part, name="grads_swap_cores" + tg), name="grads_sum_cores" + tg)
        after = gsum[l, names]
    for names in (MIXER_NAMES, FFN_NAMES):
        off = 0
        for name in names:
            rows = ROWS_OF[name]
            grads[name] = jnp.stack([gsum[l, names][off:off + rows].reshape(weights[name].shape[1:])
                                     for l in range(DEPTH)])
            off += rows

    c_act_t = jnp.pad(c_act[::8].T, ((0, 0), (0, LANES - N_DEV)))
    g_ada = []
    for l in range(DEPTH):
        d_all = small_all[:, 8 * l:8 * l + N_MOD].reshape(N_DEV, N_MOD * D_MODEL)
        d_mine = lax.dynamic_slice_in_dim(d_all, my_chip * ada_cols, ada_cols, axis=1)
        g_ada.append(_mm(c_act_t, jnp.pad(d_mine, ((0, LANES - N_DEV), (0, 0))), name=f"ada_dw_l{l}")[0])
    grads["w_ada"] = jnp.stack(g_ada)

    def view(a):
        return a.reshape(1, -1) if a.ndim == 1 else a.reshape(-1, a.shape[-1])

    delta, new_m, new_v = {}, {}, {}
    for name in order:
        shape = weights[name].shape
        d, nm, nv = _adamw(view(weights[name]), view(grads[name]), view(moms[name]), view(vels[name]),
                           name="adamw_" + name)
        delta[name], new_m[name], new_v[name] = d.reshape(shape), nm.reshape(shape), nv.reshape(shape)
    return (loss, grads["x"], *[grads[n] for n in order], *[delta[n] for n in order],
            *[new_m[n] for n in order], *[new_v[n] for n in order])
```

```python
import functools
import math

import jax
import jax.numpy as jnp
from jax import lax
from jax.experimental import pallas as pl
from jax.experimental.pallas import tpu as pltpu

F32 = jnp.float32
BF16 = jnp.bfloat16
MESH = pl.DeviceIdType.MESH

D_MODEL = 1024
DEPTH = 2
POOL_WINDOWS = (2, 4, 8, 16)
POOL_GROUP = 128
POOL_DIM = 512
N_HEADS = 8
QK_NOPE = 64
QK_ROPE = 32
QK_DIM = QK_NOPE + QK_ROPE
V_DIM = 64
HEAD_PAD = 128
Q_LORA = 384
KV_LORA = 256
ROPE_THETA = 10000.0
ATTN_DIM = N_HEADS * V_DIM
D_FF = 4 * D_MODEL
N_MOD = 6
EPS = 1e-6
N_CHIPS = 4
N_DEV = 8

ADAM_LR = 0.001
ADAM_B1 = 0.9
ADAM_B2 = 0.999
ADAM_EPS = 1e-08
ADAM_WD = 0.01
ADAM_STEP = 10

VMEM_LIMIT_BYTES = 56 * 1024 * 1024
LANES = 128
HALO = 16

ZC_CQ = 0
ZC_KR = 384
ZC_U = 512
ZC_GA = 1024
ZC_GB = 2048
ZC_CKV = 3072
Z_DIM = 3328

PACK_ROWS = (("w_in", 808), ("w_uq", 72), ("w_uk", 32), ("w_uv", 32), ("p_pool", 128), ("p_attn", 128),
             ("w_out", 256), ("w_ff1", 1024), ("w_ff2", 1024))
PACK_LAYER_ROWS = sum(r for _, r in PACK_ROWS)


def _params(sem=None, **kw):
    return pltpu.CompilerParams(dimension_semantics=sem, vmem_limit_bytes=VMEM_LIMIT_BYTES, **kw)


def _tile(n, target, unit=LANES):
    best = None
    for t in range(unit, min(n, target) + 1, unit):
        if n % t == 0:
            best = t
    return best if best is not None else n


def _near_tile(n, target):
    cands = [t for t in range(LANES, n + 1, LANES) if n % t == 0]
    return min(cands, key=lambda t: abs(math.log(t / target))) if cands else n


def _mm(a, b, *, name, ta=False, tb=False, out_dtypes=(F32,), epilogue=None, extras=(), tm=1024, tn=1024, tk=1024):
    (k_dim, m_dim) = a.shape if ta else a.shape[::-1]
    (n_dim, k_b) = b.shape if tb else b.shape[::-1]
    assert k_dim == k_b, (a.shape, b.shape)
    tm, tn, tk = _near_tile(m_dim, tm), _near_tile(n_dim, tn), _near_tile(k_dim, tk)
    nk = k_dim // tk
    n_extra, n_out = len(extras), len(out_dtypes)
    dims = (((0 if ta else 1,), (1 if tb else 0,)), ((), ()))
    if epilogue is None:
        epilogue = lambda acc: (acc,) * n_out

    def body(a_ref, b_ref, *rest):
        extra_refs, out_refs = rest[:n_extra], rest[n_extra:n_extra + n_out]

        def product():
            return lax.dot_general(a_ref[...].astype(BF16), b_ref[...].astype(BF16), dims, preferred_element_type=F32)

        def finish(acc):
            outs = epilogue(acc, *[r[...] for r in extra_refs])
            for o_ref, o in zip(out_refs, outs):
                o_ref[...] = o.astype(o_ref.dtype)

        if nk == 1:
            finish(product())
            return
        acc_ref = rest[-1]
        k = pl.program_id(2)

        @pl.when(k == 0)
        def _():
            acc_ref[...] = product()

        @pl.when((k > 0) & (k < nk - 1))
        def _():
            acc_ref[...] += product()

        @pl.when(k == nk - 1)
        def _():
            finish(acc_ref[...] + product())

    a_spec = pl.BlockSpec((tk, tm), lambda i, j, k: (k, i)) if ta else pl.BlockSpec((tm, tk), lambda i, j, k: (i, k))
    b_spec = pl.BlockSpec((tn, tk), lambda i, j, k: (j, k)) if tb else pl.BlockSpec((tk, tn), lambda i, j, k: (k, j))
    extra_specs = []
    for arr, kind in extras:
        if kind == "tile":
            extra_specs.append(pl.BlockSpec((tm, tn), lambda i, j, k: (i, j)))
        elif kind == "row":
            extra_specs.append(pl.BlockSpec((1, tn), lambda i, j, k: (0, j)))
        elif kind == "col":
            extra_specs.append(pl.BlockSpec((tm, 1), lambda i, j, k: (i, 0)))
        else:
            assert kind == "table", kind
            extra_specs.append(pl.BlockSpec((tm, LANES), lambda i, j, k: (i, 0)))
    return pl.pallas_call(
        body,
        name=name,
        grid=(m_dim // tm, n_dim // tn, nk),
        in_specs=[a_spec, b_spec] + extra_specs,
        out_specs=[pl.BlockSpec((tm, tn), lambda i, j, k: (i, j)) for _ in out_dtypes],
        out_shape=[jax.ShapeDtypeStruct((m_dim, n_dim), dt) for dt in out_dtypes],
        scratch_shapes=[pltpu.VMEM((tm, tn), F32)] if nk > 1 else [],
        compiler_params=_params(("parallel", "parallel", "arbitrary")),
    )(a, b, *[arr for arr, _ in extras])


def _rows(s):
    return min(512, s)


def _rope_tables(pos_col, inv_freq_lanes, *, name):
    s = pos_col.shape[0]
    tb = _rows(s)

    def body(pos_ref, f_ref, cos_ref, sin_ref):
        ang = pos_ref[...].astype(F32) * f_ref[...]
        lane = lax.broadcasted_iota(jnp.int32, ang.shape, 1)
        on = (lane >= QK_NOPE) & (lane < QK_DIM)
        cos_ref[...] = jnp.where(on, jnp.cos(ang), 0.0)
        sin_ref[...] = jnp.where(on, jnp.sin(ang), 0.0)

    return pl.pallas_call(
        body, name=name, grid=(s // tb,),
        in_specs=[pl.BlockSpec((tb, 1), lambda i: (i, 0)), pl.BlockSpec((1, LANES), lambda i: (0, 0))],
        out_specs=[pl.BlockSpec((tb, LANES), lambda i: (i, 0))] * 2,
        out_shape=[jax.ShapeDtypeStruct((s, LANES), F32)] * 2,
        compiler_params=_params(("parallel",)),
    )(pos_col, inv_freq_lanes)


def _rotate_half(x):
    lane = lax.broadcasted_iota(jnp.int32, x.shape, 1)
    half = QK_ROPE // 2
    first = (lane >= QK_NOPE) & (lane < QK_NOPE + half)
    second = (lane >= QK_NOPE + half) & (lane < QK_DIM)
    return jnp.where(first, -pltpu.roll(x, LANES - half, 1), jnp.where(second, pltpu.roll(x, half, 1), 0.0))


def _norm_mod(x, g, sc, sh, *, name):
    s, d = x.shape
    tb = _rows(s)

    def body(x_ref, g_ref, sc_ref, sh_ref, h_ref, r_ref):
        xv = x_ref[...]
        r = lax.rsqrt(jnp.mean(xv * xv, axis=-1, keepdims=True) + EPS)
        r_ref[...] = r
        h_ref[...] = (((xv * r) * g_ref[...]) * (1.0 + sc_ref[...]) + sh_ref[...]).astype(BF16)

    vec = pl.BlockSpec((1, d), lambda i: (0, 0))
    return pl.pallas_call(
        body, name=name, grid=(s // tb,),
        in_specs=[pl.BlockSpec((tb, d), lambda i: (i, 0)), vec, vec, vec],
        out_specs=[pl.BlockSpec((tb, d), lambda i: (i, 0)), pl.BlockSpec((tb, 1), lambda i: (i, 0))],
        out_shape=[jax.ShapeDtypeStruct((s, d), BF16), jax.ShapeDtypeStruct((s, 1), F32)],
        compiler_params=_params(("parallel",)),
    )(x, g, sc, sh)


def _window_sums(ext, sign):
    n = ext.shape[0]
    sums, cur, k = [], ext, 1
    for _ in POOL_WINDOWS:
        cur = cur + pltpu.roll(cur, k if sign > 0 else n - k, 0)
        sums.append(cur)
        k *= 2
    return sums


def _mixer_pre(z, cos_t, sin_t, w_pool, pool_scale, gq, gkv, *, name):
    s = z.shape[0]
    tb = _rows(s)
    hb = tb // HALO

    def body(zcq_ref, zkr_ref, zu_ref, zuh_ref, zckv_ref, cos_ref, sin_ref, wp_ref, ps_ref, gq_ref, gkv_ref,
             p_ref, yp_ref, cq_ref, ckv_ref, kr_ref, rq_ref, rkv_ref):
        i = pl.program_id(0)
        u = zu_ref[...]
        halo = jnp.where(i > 0, zuh_ref[...], 0.0)
        ext = jnp.concatenate([halo, u], axis=0)
        t = i * tb + lax.broadcasted_iota(jnp.int32, (tb, 1), 0)
        for g, (w, sw) in enumerate(zip(POOL_WINDOWS, _window_sums(ext, +1))):
            cols = slice(g * POOL_GROUP, (g + 1) * POOL_GROUP)
            cnt = jnp.minimum(t + 1, w).astype(F32)
            pg = (sw[HALO:, cols] / cnt - u[:, cols]).astype(BF16)
            p_ref[:, cols] = pg
            yg = jnp.dot(pg, wp_ref[g].astype(BF16), preferred_element_type=F32)
            yp_ref[:, cols] = (yg * ps_ref[:, cols]).astype(BF16)

        def rms(x_ref, g_ref, out_ref, r_ref):
            xv = x_ref[...]
            r = lax.rsqrt(jnp.mean(xv * xv, axis=-1, keepdims=True) + EPS)
            r_ref[...] = r
            out_ref[...] = ((xv * r) * g_ref[...]).astype(BF16)

        rms(zcq_ref, gq_ref, cq_ref, rq_ref)
        rms(zckv_ref, gkv_ref, ckv_ref, rkv_ref)
        kr = zkr_ref[...]
        kr_ref[...] = (kr * cos_ref[...] + _rotate_half(kr) * sin_ref[...]).astype(BF16)

    def zcol(width, off):
        return pl.BlockSpec((tb, width), lambda i: (i, off // width))

    def full(a):
        return pl.BlockSpec(a.shape, lambda i: (0,) * a.ndim)

    def out(width, dt):
        return pl.BlockSpec((tb, width), lambda i: (i, 0)), jax.ShapeDtypeStruct((s, width), dt)

    outs = [out(POOL_DIM, BF16), out(POOL_DIM, BF16), out(Q_LORA, BF16), out(KV_LORA, BF16), out(LANES, BF16),
            out(1, F32), out(1, F32)]
    return pl.pallas_call(
        body, name=name, grid=(s // tb,),
        in_specs=[zcol(Q_LORA, ZC_CQ), zcol(LANES, ZC_KR), zcol(POOL_DIM, ZC_U),
                  pl.BlockSpec((HALO, POOL_DIM), lambda i: (jnp.maximum(i * hb - 1, 0), ZC_U // POOL_DIM)),
                  zcol(KV_LORA, ZC_CKV),
                  pl.BlockSpec((tb, LANES), lambda i: (i, 0)), pl.BlockSpec((tb, LANES), lambda i: (i, 0)),
                  full(w_pool), full(pool_scale), full(gq), full(gkv)],
        out_specs=[o[0] for o in outs], out_shape=[o[1] for o in outs],
        compiler_params=_params(("parallel",)),
    )(z, z, z, z, z, cos_t, sin_t, w_pool, pool_scale, gq, gkv)


def _sigmoid(x):
    return 1.0 / (1.0 + jnp.exp(-x))


def _merge(z, ya, yb, *, name):
    s, d = ya.shape
    tb = _rows(s)

    def body(ga_ref, gb_ref, ya_ref, yb_ref, out_ref):
        out_ref[...] = (_sigmoid(ga_ref[...]) * ya_ref[...] + _sigmoid(gb_ref[...]) * yb_ref[...]).astype(BF16)

    blk = pl.BlockSpec((tb, d), lambda i: (i, 0))
    return pl.pallas_call(
        body, name=name, grid=(s // tb,),
        in_specs=[pl.BlockSpec((tb, d), lambda i: (i, ZC_GA // d)), pl.BlockSpec((tb, d), lambda i: (i, ZC_GB // d)),
                  blk, blk],
        out_specs=blk, out_shape=jax.ShapeDtypeStruct((s, d), BF16),
        compiler_params=_params(("parallel",)),
    )(z, z, ya, yb)


ATTN_SCALE = 1.0 / math.sqrt(QK_DIM)
NEG_BIG = -1e30


LOG2_E = math.log2(math.e)
EXP2_SCALE = ATTN_SCALE * LOG2_E
NT_DIMS = (((1,), (1,)), ((), ()))
TN_DIMS = (((0,), (0,)), ((), ()))


def _on_or_below_diagonal(t):
    return lax.broadcasted_iota(jnp.int32, (t, t), 0) >= lax.broadcasted_iota(jnp.int32, (t, t), 1)


def _attn_fwd(q, k, v, *, name):
    s = q.shape[0]
    t = _rows(s)

    def body(q_ref, k_ref, v_ref, o_ref, lse_ref):
        qi = pl.program_id(1)
        qv = q_ref[...]

        def block(j, carry, diagonal):
            m, l, acc = carry
            rows = pl.ds(pl.multiple_of(j * t, t), t)
            sc = lax.dot_general(qv, k_ref[rows, :], NT_DIMS, preferred_element_type=F32)
            if diagonal:
                sc = jnp.where(_on_or_below_diagonal(t), sc, NEG_BIG)
            m_new = jnp.maximum(m, jnp.max(sc, axis=-1, keepdims=True))
            p = jnp.exp2((sc - m_new) * EXP2_SCALE)
            alpha = jnp.exp2((m - m_new) * EXP2_SCALE)
            l = alpha * l + jnp.sum(p, axis=-1, keepdims=True)
            acc = alpha * acc + jnp.dot(p.astype(BF16), v_ref[rows, :], preferred_element_type=F32)
            return m_new, l, acc

        init = (jnp.full((t, 1), -jnp.inf, F32), jnp.zeros((t, 1), F32), jnp.zeros((t, HEAD_PAD), F32))
        carry = lax.fori_loop(0, qi, lambda j, c: block(j, c, False), init)
        m, l, acc = block(qi, carry, True)
        o_ref[...] = (acc / l).astype(BF16)
        lse_ref[0] = m * ATTN_SCALE + jnp.log(l)

    q_spec = pl.BlockSpec((t, HEAD_PAD), lambda h, i: (i, h))
    kv_spec = pl.BlockSpec((s, HEAD_PAD), lambda h, i: (0, h))
    return pl.pallas_call(
        body, name=name, grid=(N_HEADS, s // t),
        in_specs=[q_spec, kv_spec, kv_spec],
        out_specs=[q_spec, pl.BlockSpec((1, t, 1), lambda h, i: (h, i, 0))],
        out_shape=[jax.ShapeDtypeStruct((s, N_HEADS * HEAD_PAD), BF16), jax.ShapeDtypeStruct((N_HEADS, s, 1), F32)],
        compiler_params=_params(("parallel", "parallel")),
    )(q, k, v)


def _attn_delta(do, o, *, name):
    s = o.shape[0]
    t = _rows(s)

    def body(do_ref, o_ref, out_ref):
        for h in range(N_HEADS):
            cols = slice(h * HEAD_PAD, (h + 1) * HEAD_PAD)
            out_ref[h] = jnp.sum(do_ref[:, cols].astype(F32) * o_ref[:, cols].astype(F32), axis=-1, keepdims=True)

    blk = pl.BlockSpec((t, N_HEADS * HEAD_PAD), lambda i: (i, 0))
    return pl.pallas_call(
        body, name=name, grid=(s // t,), in_specs=[blk, blk],
        out_specs=pl.BlockSpec((N_HEADS, t, 1), lambda i: (0, i, 0)),
        out_shape=jax.ShapeDtypeStruct((N_HEADS, s, 1), F32),
        compiler_params=_params(("parallel",)),
    )(do, o)


def _attn_bwd(q, k, v, do, lse, delta, *, name):
    s = q.shape[0]
    t = _rows(s)
    nt = s // t

    def body(q_ref, k_ref, v_ref, do_ref, lse_ref, dl_ref, dq_ref, dk_ref, dv_ref):
        kj = pl.program_id(1)

        @pl.when(kj == 0)
        def _():
            dq_ref[...] = jnp.zeros_like(dq_ref)

        kv, vv = k_ref[...], v_ref[...]

        def block(i, carry, diagonal):
            dk, dv = carry
            rows = pl.ds(pl.multiple_of(i * t, t), t)
            qv, dov = q_ref[rows, :], do_ref[rows, :]
            sc = lax.dot_general(qv, kv, NT_DIMS, preferred_element_type=F32)
            p = jnp.exp2(sc * EXP2_SCALE - lse_ref[0, rows, :] * LOG2_E)
            if diagonal:
                p = jnp.where(_on_or_below_diagonal(t), p, 0.0)
            dp = lax.dot_general(dov, vv, NT_DIMS, preferred_element_type=F32)
            ds = (p * (dp - dl_ref[0, rows, :])).astype(BF16)
            dv = dv + lax.dot_general(p.astype(BF16), dov, TN_DIMS, preferred_element_type=F32)
            dk = dk + lax.dot_general(ds, qv, TN_DIMS, preferred_element_type=F32)
            dq_ref[rows, :] += jnp.dot(ds, kv, preferred_element_type=F32) * ATTN_SCALE
            return dk, dv

        zero = jnp.zeros((t, HEAD_PAD), F32)
        carry = block(kj, (zero, zero), True)
        dk, dv = lax.fori_loop(kj + 1, nt, lambda i, c: block(i, c, False), carry)
        dk_ref[...] = dk * ATTN_SCALE
        dv_ref[...] = dv.astype(BF16)

    full_spec = pl.BlockSpec((s, HEAD_PAD), lambda h, j: (0, h))
    kv_spec = pl.BlockSpec((t, HEAD_PAD), lambda h, j: (j, h))
    vec_spec = pl.BlockSpec((1, s, 1), lambda h, j: (h, 0, 0))
    wide = (s, N_HEADS * HEAD_PAD)
    return pl.pallas_call(
        body, name=name, grid=(N_HEADS, nt),
        in_specs=[full_spec, kv_spec, kv_spec, full_spec, vec_spec, vec_spec],
        out_specs=[full_spec, kv_spec, kv_spec],
        out_shape=[jax.ShapeDtypeStruct(wide, F32), jax.ShapeDtypeStruct(wide, F32), jax.ShapeDtypeStruct(wide, BF16)],
        compiler_params=_params(("parallel", "arbitrary")),
    )(q, k, v, do, lse, delta)


def _acc_specs(widths):
    return ([pl.BlockSpec((1, w), lambda i: (0, 0)) for w in widths],
            [jax.ShapeDtypeStruct((1, w), F32) for w in widths])


def _final_loss(x, g, target, *, name):
    s, d = x.shape
    tb = _rows(s)

    def body(x_ref, g_ref, t_ref, dx_ref, loss_ref, dg_ref):
        @pl.when(pl.program_id(0) == 0)
        def _():
            loss_ref[...] = jnp.zeros_like(loss_ref)
            dg_ref[...] = jnp.zeros_like(dg_ref)

        xv = x_ref[...]
        r = lax.rsqrt(jnp.mean(xv * xv, axis=-1, keepdims=True) + EPS)
        xn = xv * r
        err = xn * g_ref[...] - t_ref[...]
        loss_ref[...] += 0.5 * jnp.sum(jnp.mean(err * err, axis=-1, keepdims=True), axis=0, keepdims=True)
        dy = err / d
        dg_ref[...] += jnp.sum(dy * xn, axis=0, keepdims=True)
        dxn = dy * g_ref[...]
        dx_ref[...] = r * (dxn - xn * jnp.mean(dxn * xn, axis=-1, keepdims=True))

    blk = pl.BlockSpec((tb, d), lambda i: (i, 0))
    acc_specs, acc_shapes = _acc_specs((LANES, d))
    return pl.pallas_call(
        body, name=name, grid=(s // tb,),
        in_specs=[blk, pl.BlockSpec((1, d), lambda i: (0, 0)), blk],
        out_specs=[blk] + acc_specs, out_shape=[jax.ShapeDtypeStruct((s, d), F32)] + acc_shapes,
        compiler_params=_params(("arbitrary",)),
    )(x, g, target)


def _gate_bwd(dx, m, g, *, name):
    s, d = dx.shape
    tb = _rows(s)

    def body(dx_ref, m_ref, g_ref, dm_ref, dg_ref):
        @pl.when(pl.program_id(0) == 0)
        def _():
            dg_ref[...] = jnp.zeros_like(dg_ref)

        dxv = dx_ref[...]
        dm_ref[...] = (dxv * g_ref[...]).astype(BF16)
        dg_ref[...] += jnp.sum(dxv * m_ref[...], axis=0, keepdims=True)

    blk = pl.BlockSpec((tb, d), lambda i: (i, 0))
    acc_specs, acc_shapes = _acc_specs((d,))
    return pl.pallas_call(
        body, name=name, grid=(s // tb,),
        in_specs=[blk, blk, pl.BlockSpec((1, d), lambda i: (0, 0))],
        out_specs=[blk] + acc_specs, out_shape=[jax.ShapeDtypeStruct((s, d), BF16)] + acc_shapes,
        compiler_params=_params(("arbitrary",)),
    )(dx, m, g)


def _norm_mod_bwd(dh, x, r, g, sc, dx_skip, *, name):
    s, d = x.shape
    tb = _rows(s)
    nb = s // tb

    def body(dh_ref, x_ref, r_ref, g_ref, sc_ref, skip_ref, dx_ref, dg_ref, dsc_ref, dsh_ref, da_sc):
        i = pl.program_id(0)

        @pl.when(i == 0)
        def _():
            da_sc[...] = jnp.zeros_like(da_sc)
            dsh_ref[...] = jnp.zeros_like(dsh_ref)

        dhv, rv = dh_ref[...], r_ref[...]
        xn = x_ref[...] * rv
        dsh_ref[...] += jnp.sum(dhv, axis=0, keepdims=True)
        da_sc[...] += jnp.sum(dhv * xn, axis=0, keepdims=True)
        dxn = dhv * (g_ref[...] * (1.0 + sc_ref[...]))
        dx_ref[...] = skip_ref[...] + rv * (dxn - xn * jnp.mean(dxn * xn, axis=-1, keepdims=True))

        @pl.when(i == nb - 1)
        def _():
            dg_ref[...] = da_sc[...] * (1.0 + sc_ref[...])
            dsc_ref[...] = da_sc[...] * g_ref[...]

    blk = pl.BlockSpec((tb, d), lambda i: (i, 0))
    vec = pl.BlockSpec((1, d), lambda i: (0, 0))
    acc_specs, acc_shapes = _acc_specs((d, d, d))
    return pl.pallas_call(
        body, name=name, grid=(nb,),
        in_specs=[blk, blk, pl.BlockSpec((tb, 1), lambda i: (i, 0)), vec, vec, blk],
        out_specs=[blk] + acc_specs, out_shape=[jax.ShapeDtypeStruct((s, d), F32)] + acc_shapes,
        scratch_shapes=[pltpu.VMEM((1, d), F32)],
        compiler_params=_params(("arbitrary",)),
    )(dh, x, r, g, sc, dx_skip)


def _merge_bwd(dmerged, z, ya, yb, *, name):
    s, d = ya.shape
    tb = _rows(s)

    def body(dm_ref, ga_ref, gb_ref, ya_ref, yb_ref, dya_ref, dyb_ref, dga_ref, dgb_ref):
        dm = dm_ref[...]
        for g_ref, y_ref, dy_ref, dg_ref in ((ga_ref, ya_ref, dya_ref, dga_ref), (gb_ref, yb_ref, dyb_ref, dgb_ref)):
            sg = _sigmoid(g_ref[...])
            dy_ref[...] = (dm * sg).astype(BF16)
            dg_ref[...] = (dm * y_ref[...] * (sg * (1.0 - sg))).astype(BF16)

    blk = pl.BlockSpec((tb, d), lambda i: (i, 0))
    return pl.pallas_call(
        body, name=name, grid=(s // tb,),
        in_specs=[blk, pl.BlockSpec((tb, d), lambda i: (i, ZC_GA // d)), pl.BlockSpec((tb, d), lambda i: (i, ZC_GB // d)),
                  blk, blk],
        out_specs=[blk] * 4, out_shape=[jax.ShapeDtypeStruct((s, d), BF16)] * 4,
        compiler_params=_params(("parallel",)),
    )(dmerged, z, z, ya, yb)


def _pool_bwd(dyp, p, w_pool, pool_scale, *, name):
    s = dyp.shape[0]
    tb = _rows(s)
    nb = s // tb
    hb = tb // HALO
    nt_dims = (((1,), (1,)), ((), ()))
    tn_dims = (((0,), (0,)), ((), ()))

    def body(dy_ref, dyn_ref, p_ref, wp_ref, ps_ref, du_ref, gwp_ref, gps_ref):
        i = pl.program_id(0)

        @pl.when(i == 0)
        def _():
            gwp_ref[...] = jnp.zeros_like(gwp_ref)
            gps_ref[...] = jnp.zeros_like(gps_ref)

        cur = dy_ref[...]
        nxt = jnp.where(i < nb - 1, dyn_ref[...], 0.0)
        dpw = (jnp.concatenate([cur, nxt], axis=0) * ps_ref[...]).astype(BF16)
        t = i * tb + lax.broadcasted_iota(jnp.int32, (tb + HALO, 1), 0)
        for g, w in enumerate(POOL_WINDOWS):
            cols = slice(g * POOL_GROUP, (g + 1) * POOL_GROUP)
            wg = wp_ref[g].astype(BF16)
            dp = lax.dot_general(dpw[:, cols], wg, nt_dims, preferred_element_type=F32)
            e = dp / jnp.minimum(t + 1, w).astype(F32)
            lead = _window_sums(e, -1)[g]
            du_ref[:, cols] = (lead[:tb] - dp[:tb]).astype(BF16)
            pg = p_ref[:, cols]
            pw = jnp.dot(pg, wg, preferred_element_type=F32)
            gps_ref[:, cols] += jnp.sum(cur[:, cols] * pw, axis=0, keepdims=True)
            gwp_ref[g] += lax.dot_general(pg, dpw[:tb, cols], tn_dims, preferred_element_type=F32)

    blk = pl.BlockSpec((tb, POOL_DIM), lambda i: (i, 0))
    return pl.pallas_call(
        body, name=name, grid=(nb,),
        in_specs=[blk, pl.BlockSpec((HALO, POOL_DIM), lambda i: (jnp.minimum((i + 1) * hb, s // HALO - 1), 0)), blk,
                  pl.BlockSpec(w_pool.shape, lambda i: (0, 0, 0)), pl.BlockSpec((1, POOL_DIM), lambda i: (0, 0))],
        out_specs=[blk, pl.BlockSpec(w_pool.shape, lambda i: (0, 0, 0)), pl.BlockSpec((1, POOL_DIM), lambda i: (0, 0))],
        out_shape=[jax.ShapeDtypeStruct((s, POOL_DIM), BF16), jax.ShapeDtypeStruct(w_pool.shape, F32),
                   jax.ShapeDtypeStruct((1, POOL_DIM), F32)],
        compiler_params=_params(("arbitrary",)),
    )(dyp, dyp, p, w_pool, pool_scale)


def _rope_bwd_q(dq, cos_t, sin_t, *, name):
    s = dq.shape[0]
    tb = _rows(s)

    def body(dq_ref, cos_ref, sin_ref, out_ref):
        sin = sin_ref[...]
        lane = lax.broadcasted_iota(jnp.int32, sin.shape, 1)
        cos_q = cos_ref[...] + jnp.where(lane < QK_NOPE, 1.0, 0.0)
        for h in range(N_HEADS):
            cols = slice(h * HEAD_PAD, (h + 1) * HEAD_PAD)
            dqv = dq_ref[:, cols]
            out_ref[:, cols] = (dqv * cos_q - _rotate_half(dqv * sin)).astype(BF16)

    blk = pl.BlockSpec((tb, N_HEADS * HEAD_PAD), lambda i: (i, 0))
    tab = pl.BlockSpec((tb, LANES), lambda i: (i, 0))
    return pl.pallas_call(
        body, name=name, grid=(s // tb,), in_specs=[blk, tab, tab], out_specs=blk,
        out_shape=jax.ShapeDtypeStruct(dq.shape, BF16),
        compiler_params=_params(("parallel",)),
    )(dq, cos_t, sin_t)


def _key_bwd(dk, cos_t, sin_t, *, name):
    s = dk.shape[0]
    tb = _rows(s)

    def body(dk_ref, cos_ref, sin_ref, dkb_ref, dkr_ref):
        dkv = dk_ref[...]
        dkb_ref[...] = dkv.astype(BF16)
        tot = dkv[:, :HEAD_PAD]
        for h in range(1, N_HEADS):
            tot = tot + dkv[:, h * HEAD_PAD:(h + 1) * HEAD_PAD]
        dkr_ref[...] = (tot * cos_ref[...] - _rotate_half(tot * sin_ref[...])).astype(BF16)

    blk = pl.BlockSpec((tb, N_HEADS * HEAD_PAD), lambda i: (i, 0))
    tab = pl.BlockSpec((tb, LANES), lambda i: (i, 0))
    return pl.pallas_call(
        body, name=name, grid=(s // tb,), in_specs=[blk, tab, tab], out_specs=[blk, tab],
        out_shape=[jax.ShapeDtypeStruct(dk.shape, BF16), jax.ShapeDtypeStruct((s, LANES), BF16)],
        compiler_params=_params(("parallel",)),
    )(dk, cos_t, sin_t)


def _rms_bwd(dy, z, z_off, r, g, *, name):
    s, n = dy.shape
    tb = _rows(s)

    def body(dy_ref, x_ref, r_ref, g_ref, dx_ref, dg_ref):
        @pl.when(pl.program_id(0) == 0)
        def _():
            dg_ref[...] = jnp.zeros_like(dg_ref)

        dyv, rv = dy_ref[...], r_ref[...]
        xn = x_ref[...] * rv
        dg_ref[...] += jnp.sum(dyv * xn, axis=0, keepdims=True)
        dxn = dyv * g_ref[...]
        dx_ref[...] = (rv * (dxn - xn * jnp.mean(dxn * xn, axis=-1, keepdims=True))).astype(BF16)

    blk = pl.BlockSpec((tb, n), lambda i: (i, 0))
    acc_specs, acc_shapes = _acc_specs((n,))
    return pl.pallas_call(
        body, name=name, grid=(s // tb,),
        in_specs=[blk, pl.BlockSpec((tb, n), lambda i: (i, z_off // n)), pl.BlockSpec((tb, 1), lambda i: (i, 0)),
                  pl.BlockSpec((1, n), lambda i: (0, 0))],
        out_specs=[blk] + acc_specs, out_shape=[jax.ShapeDtypeStruct((s, n), BF16)] + acc_shapes,
        compiler_params=_params(("arbitrary",)),
    )(dy, z, r, g)


def _silu(c, *, name):
    def body(c_ref, out_ref):
        cv = c_ref[...]
        out_ref[...] = (cv * _sigmoid(cv)).astype(BF16)

    return pl.pallas_call(body, name=name, out_shape=jax.ShapeDtypeStruct(c.shape, BF16),
                          compiler_params=_params())(c)


def _sum_slots(a, n, *, name, out_dtype=F32):
    _, rows, cols = a.shape
    tr = _tile(rows, 256, 8)

    def body(a_ref, out_ref):
        tot = a_ref[0].astype(F32)
        for j in range(1, n):
            tot = tot + a_ref[j].astype(F32)
        out_ref[...] = tot.astype(out_dtype)

    return pl.pallas_call(
        body, name=name, grid=(rows // tr,),
        in_specs=[pl.BlockSpec((n, tr, cols), lambda i: (0, i, 0))],
        out_specs=pl.BlockSpec((tr, cols), lambda i: (i, 0)),
        out_shape=jax.ShapeDtypeStruct((rows, cols), out_dtype),
        compiler_params=_params(("parallel",)),
    )(a)


def _add2(a, b, *, name):
    rows, cols = a.shape
    tr = _tile(rows, 256, 8)

    def body(a_ref, b_ref, out_ref):
        out_ref[...] = a_ref[...] + b_ref[...]

    blk = pl.BlockSpec((tr, cols), lambda i: (i, 0))
    return pl.pallas_call(
        body, name=name, grid=(rows // tr,), in_specs=[blk, blk], out_specs=blk,
        out_shape=jax.ShapeDtypeStruct((rows, cols), F32),
        compiler_params=_params(("parallel",)),
    )(a, b)


def _adamw(w, g, m, v, *, name):
    rows, cols = w.shape
    tr = _tile(rows, max(8, (1 << 18) // cols), 8)
    c1 = 1.0 - ADAM_B1 ** ADAM_STEP
    c2 = 1.0 - ADAM_B2 ** ADAM_STEP

    def body(w_ref, g_ref, m_ref, v_ref, d_ref, nm_ref, nv_ref):
        gv = g_ref[...]
        nm = ADAM_B1 * m_ref[...] + (1.0 - ADAM_B1) * gv
        nv = ADAM_B2 * v_ref[...] + (1.0 - ADAM_B2) * (gv * gv)
        nm_ref[...] = nm
        nv_ref[...] = nv
        d_ref[...] = -ADAM_LR * ((nm / c1) / (jnp.sqrt(nv / c2) + ADAM_EPS) + ADAM_WD * w_ref[...])

    blk = pl.BlockSpec((tr, cols), lambda i: (i, 0))
    return pl.pallas_call(
        body, name=name, grid=(rows // tr,), in_specs=[blk] * 4, out_specs=[blk] * 3,
        out_shape=[jax.ShapeDtypeStruct((rows, cols), F32)] * 3,
        compiler_params=_params(("parallel",)),
    )(w, g, m, v)


def _coords():
    return lax.axis_index("x"), lax.axis_index("y"), lax.axis_index("c")


def _other_chips(x, y):
    return [(1 - x, y), (x, 1 - y), (1 - x, 1 - y)]


def _all_gather_small(blk, *, name):
    m_per, n = blk.shape

    def body(x_ref, out_ref, send_sems, recv_sems, local_sem):
        x, y, c = _coords()
        me, sibling = (x, y, c), (x, y, 1 - c)
        chips = _other_chips(x, y)

        def rows(px, py, pc):
            return out_ref.at[pl.ds((4 * px + 2 * py + pc) * m_per, m_per), :]

        def copy(k, block, to, src=None):
            return pltpu.make_async_remote_copy(
                src_ref=rows(*block) if src is None else src, dst_ref=rows(*block),
                send_sem=send_sems.at[k], recv_sem=recv_sems.at[k], device_id=to, device_id_type=MESH)

        mine = pltpu.make_async_copy(x_ref, rows(*me), local_sem)
        mine.start()
        first = [copy(0, me, sibling, src=x_ref)]
        first += [copy(1 + j, me, (*chip, c), src=x_ref) for j, chip in enumerate(chips)]
        for cp in first:
            cp.start()
        passed = [copy(4 + j, (*chip, c), sibling) for j, chip in enumerate(chips)]
        for j, chip in enumerate(chips):
            copy(1 + j, (*chip, c), me).wait_recv()
            passed[j].start()
        copy(0, sibling, me).wait_recv()
        for j, chip in enumerate(chips):
            copy(4 + j, (*chip, 1 - c), me).wait_recv()
        for cp in first + passed:
            cp.wait_send()
        mine.wait()

    return pl.pallas_call(
        body, name=name,
        out_shape=jax.ShapeDtypeStruct((N_DEV * m_per, n), blk.dtype),
        in_specs=[pl.BlockSpec(memory_space=pltpu.VMEM)],
        out_specs=pl.BlockSpec(memory_space=pltpu.VMEM),
        scratch_shapes=[pltpu.SemaphoreType.DMA((7,)), pltpu.SemaphoreType.DMA((7,)), pltpu.SemaphoreType.DMA],
        compiler_params=_params(),
    )(blk)


HBM_SPEC = pl.BlockSpec(memory_space=pltpu.HBM)
SEM_SPEC = pl.BlockSpec(memory_space=pltpu.SEMAPHORE)
DATAFLOW = pltpu.SideEffectType.DATAFLOW_SIDE_EFFECTING


def _chip_copies(src_ref, land_ref, send_sems, recv_sems, scatter):
    x, y, c = _coords()
    my = 2 * x + y
    outgoing, incoming = [], []
    for k, (px, py) in enumerate(_other_chips(x, y)):
        peer = 2 * px + py

        def copy(src_slot, dst_slot):
            return pltpu.make_async_remote_copy(
                src_ref=src_ref.at[src_slot] if scatter else src_ref, dst_ref=land_ref.at[dst_slot],
                send_sem=send_sems.at[k], recv_sem=recv_sems.at[k], device_id=(px, py, c), device_id_type=MESH)

        outgoing.append(copy(peer, my))
        incoming.append(copy(my, peer))
    return outgoing, incoming


def _exchange_start(src, *, name, scatter):
    land_shape = src.shape if scatter else (N_CHIPS,) + src.shape

    def body(src_ref, land_ref, send_sems, recv_sems, src_thru, land_thru, token):
        outgoing, _ = _chip_copies(src_ref, land_ref, send_sems, recv_sems, scatter)
        for cp in outgoing:
            cp.start()
        token[...] = jnp.zeros_like(token)

    return pl.pallas_call(
        body, name=name,
        out_shape=(pltpu.SemaphoreType.DMA((N_CHIPS - 1,)), pltpu.SemaphoreType.DMA((N_CHIPS - 1,)),
                   pltpu.HBM(src.shape, src.dtype), pltpu.HBM(land_shape, src.dtype), jax.ShapeDtypeStruct((8, LANES), F32)),
        in_specs=(HBM_SPEC, HBM_SPEC),
        out_specs=(SEM_SPEC, SEM_SPEC, HBM_SPEC, HBM_SPEC, pl.BlockSpec(memory_space=pltpu.VMEM)),
        input_output_aliases={0: 2, 1: 3},
        compiler_params=pltpu.CompilerParams(has_side_effects=DATAFLOW),
    )(pltpu.with_memory_space_constraint(src, pltpu.HBM),
      pltpu.with_memory_space_constraint(lax.empty(land_shape, src.dtype), pltpu.HBM))


def _exchange_wait(started, after, *, name, scatter):
    send_sems, recv_sems, src_thru, land_thru, _ = started

    def body(src_ref, land_ref, send_sems, recv_sems, after_ref, src_dead, got_ref):
        outgoing, incoming = _chip_copies(src_ref, land_ref, send_sems, recv_sems, scatter)
        for cp in outgoing:
            cp.wait_send()
        for cp in incoming:
            cp.wait_recv()

    return pl.pallas_call(
        body, name=name,
        out_shape=(pltpu.HBM(src_thru.shape, src_thru.dtype), pltpu.HBM(land_thru.shape, land_thru.dtype)),
        in_specs=(HBM_SPEC, HBM_SPEC, SEM_SPEC, SEM_SPEC, pl.BlockSpec(memory_space=pl.ANY)),
        out_specs=(HBM_SPEC, HBM_SPEC),
        input_output_aliases={0: 0, 1: 1},
        compiler_params=pltpu.CompilerParams(has_side_effects=DATAFLOW),
    )(src_thru, land_thru, send_sems, recv_sems, after)[1]


def _exchange_sibling(src, *, name):
    def body(src_ref, out_ref, send_sem, recv_sem):
        x, y, c = _coords()
        cp = pltpu.make_async_remote_copy(src_ref=src_ref, dst_ref=out_ref, send_sem=send_sem, recv_sem=recv_sem,
                                          device_id=(x, y, 1 - c), device_id_type=MESH)
        cp.start()
        cp.wait()

    return pl.pallas_call(
        body, name=name,
        out_shape=jax.ShapeDtypeStruct(src.shape, src.dtype),
        in_specs=[pl.BlockSpec(memory_space=pl.ANY)],
        out_specs=pl.BlockSpec(memory_space=pl.ANY),
        scratch_shapes=[pltpu.SemaphoreType.DMA, pltpu.SemaphoreType.DMA],
        compiler_params=_params(),
    )(src)


def _pack_rows(a):
    return a.reshape(-1, D_MODEL)


def _pad_heads(w, width):
    r = w.shape[0]
    return jnp.pad(w, ((0, 0), (0, 0), (0, HEAD_PAD - width))).reshape(r, N_HEADS * HEAD_PAD)


IN_NAMES = ("w_in",)
MIX_NAMES = ("w_uq", "w_uk", "w_uv", "p_pool", "p_attn", "w_out")
FFN_NAMES = ("w_ff1", "w_ff2")
GROUPS = (("in", IN_NAMES), ("mix", MIX_NAMES), ("ffn", FFN_NAMES))
ROWS_OF = dict(PACK_ROWS)


def _unpack_weights(gathered, names):
    def cols(a, k):
        return a.reshape(N_CHIPS, k, -1).transpose(1, 0, 2).reshape(k, -1)

    def w_in(a):
        full = cols(a, D_MODEL)
        kr = jnp.pad(full[:, 1152:1184], ((0, 0), (QK_NOPE, HEAD_PAD - QK_DIM)))
        return jnp.concatenate([full[:, 512:896], kr, full[:, 0:512], full[:, 1184:3232], full[:, 896:1152]], axis=1)

    def p_attn(a):
        full = cols(a, ATTN_DIM).reshape(N_HEADS, V_DIM, D_MODEL)
        return jnp.pad(full, ((0, 0), (0, HEAD_PAD - V_DIM), (0, 0))).reshape(N_HEADS * HEAD_PAD, D_MODEL)

    build = dict(
        w_in=w_in,
        w_uq=lambda a: _pad_heads(a.reshape(Q_LORA, N_HEADS, QK_DIM), QK_DIM),
        w_uk=lambda a: _pad_heads(a.reshape(KV_LORA, N_HEADS, QK_NOPE), QK_NOPE),
        w_uv=lambda a: _pad_heads(a.reshape(KV_LORA, N_HEADS, V_DIM), V_DIM),
        p_pool=lambda a: cols(a, POOL_DIM),
        p_attn=p_attn,
        w_out=lambda a: a.reshape(D_MODEL, D_MODEL),
        w_ff1=lambda a: cols(a, D_MODEL),
        w_ff2=lambda a: a.reshape(D_FF, D_MODEL),
    )
    w, off = {}, 0
    for name in names:
        w[name] = build[name](gathered[:, off:off + ROWS_OF[name]])
        off += ROWS_OF[name]
    return w


def _pack_grads(g, names):
    def cols(a):
        k = a.shape[0]
        return a.reshape(k, N_CHIPS, -1).transpose(1, 0, 2).reshape(N_CHIPS, -1, D_MODEL)

    def rows(a):
        return a.reshape(N_CHIPS, -1, D_MODEL)

    def heads(width):
        return lambda a: rows(a.reshape(a.shape[0], N_HEADS, HEAD_PAD)[:, :, :width])

    def w_in(gi):
        return cols(jnp.concatenate([gi[:, ZC_U:ZC_U + 512], gi[:, ZC_CQ:ZC_CQ + 384], gi[:, ZC_CKV:ZC_CKV + 256],
                                     gi[:, ZC_KR + QK_NOPE:ZC_KR + QK_DIM], gi[:, ZC_GA:ZC_GA + 2048]], axis=1))

    def p_attn(a):
        return cols(a.reshape(N_HEADS, HEAD_PAD, D_MODEL)[:, :V_DIM].reshape(ATTN_DIM, D_MODEL))

    build = dict(w_in=w_in, w_uq=heads(QK_DIM), w_uk=heads(QK_NOPE), w_uv=heads(V_DIM), p_pool=cols, p_attn=p_attn,
                 w_out=rows, w_ff1=cols, w_ff2=rows)
    return jnp.concatenate([build[name](g[name]) for name in names], axis=1)


def _per_head(fn, acc, *tables):
    return jnp.concatenate([fn(acc[:, h * HEAD_PAD:(h + 1) * HEAD_PAD], *tables) for h in range(N_HEADS)], axis=1)


def _rope_head(a, cos, sin):
    lane = lax.broadcasted_iota(jnp.int32, a.shape, 1)
    return a * (cos + jnp.where(lane < QK_NOPE, 1.0, 0.0)) + _rotate_half(a) * sin


def _layer_fwd(l, x, mod, get_weights, small, cos_t, sin_t):
    sh1, sc1, g1, sh2, sc2, g2 = mod
    tag = f"_l{l}"
    h, r1 = _norm_mod(x, small["ln1_g"], sc1, sh1, name="norm1" + tag)
    w = dict(get_weights("in", h))
    (z,) = _mm(h, w["w_in"], name="in_proj" + tag)
    p, yp, cq, ckv, kr, rq, rkv = _mixer_pre(z, cos_t, sin_t, small["w_pool"], small["pool_scale"],
                                              small["q_norm_g"], small["kv_norm_g"], name="mixer_pre" + tag)
    w.update(get_weights("mix", yp))
    (ya,) = _mm(yp, w["p_pool"], name="pool_out" + tag)
    (q,) = _mm(cq, w["w_uq"], name="q_proj" + tag, out_dtypes=(BF16,),
               epilogue=lambda acc, cos, sin: (_per_head(_rope_head, acc, cos, sin),),
               extras=((cos_t, "table"), (sin_t, "table")))
    (k,) = _mm(ckv, w["w_uk"], name="k_proj" + tag, out_dtypes=(BF16,),
               epilogue=lambda acc, krv: (_per_head(lambda a, b: a + b, acc, krv),), extras=((kr, "table"),))
    (v,) = _mm(ckv, w["w_uv"], name="v_proj" + tag, out_dtypes=(BF16,))
    o, lse = _attn_fwd(q, k, v, name="attn_fwd" + tag)
    (yb,) = _mm(o, w["p_attn"], name="attn_out" + tag)
    merged = _merge(z, ya, yb, name="merge" + tag)
    mo, x1 = _mm(merged, w["w_out"], name="mix_out" + tag, out_dtypes=(F32, F32),
                 epilogue=lambda acc, xr, g: (acc, xr + g * acc), extras=((x, "tile"), (g1, "row")), tm=512)
    h2, r2 = _norm_mod(x1, small["ln2_g"], sc2, sh2, name="norm2" + tag)
    w.update(get_weights("ffn", merged))
    f, act = _mm(h2, w["w_ff1"], name="ff1" + tag, out_dtypes=(F32, BF16),
                 epilogue=lambda acc: (acc, jnp.square(jnp.maximum(acc, 0.0))))
    m2, x2 = _mm(act, w["w_ff2"], name="ff2" + tag, out_dtypes=(F32, F32),
                 epilogue=lambda acc, xr, g: (acc, xr + g * acc), extras=((x1, "tile"), (g2, "row")), tm=512)
    saved = dict(x=x, h=h, r1=r1, z=z, p=p, yp=yp, cq=cq, ckv=ckv, rq=rq, rkv=rkv, ya=ya, q=q, k=k, v=v, o=o, lse=lse,
                 yb=yb, merged=merged, mo=mo, x1=x1, h2=h2, r2=r2, f=f, act=act, m2=m2)
    return x2, saved, w


def _layer_bwd(l, dx2, sv, mod, w, small, cos_t, sin_t, send_grads):
    sh1, sc1, g1, sh2, sc2, g2 = mod
    tag = f"_l{l}"
    gw = {}
    dm2, dg2 = _gate_bwd(dx2, sv["m2"], g2, name="gate2_bwd" + tag)
    (df,) = _mm(dm2, w["w_ff2"], tb=True, name="ff2_dx" + tag, out_dtypes=(BF16,),
                epilogue=lambda acc, f: (acc * (2.0 * jnp.maximum(f, 0.0)),), extras=((sv["f"], "tile"),))
    (g_ff2,) = _mm(sv["act"], dm2, ta=True, name="ff2_dw" + tag, out_dtypes=(BF16,))
    (g_ff1,) = _mm(sv["h2"], df, ta=True, name="ff1_dw" + tag, out_dtypes=(BF16,))
    sc2 = sc2 + send_grads("ffn", dict(w_ff1=g_ff1, w_ff2=g_ff2))
    (dh2,) = _mm(df, w["w_ff1"], tb=True, name="ff1_dx" + tag)
    dx1, dln2, dsc2, dsh2 = _norm_mod_bwd(dh2, sv["x1"], sv["r2"], small["ln2_g"], sc2, dx2, name="norm2_bwd" + tag)
    dmo, dg1 = _gate_bwd(dx1, sv["mo"], g1, name="gate1_bwd" + tag)
    (dmerged,) = _mm(dmo, w["w_out"], tb=True, name="mix_out_dx" + tag)
    (gw["w_out"],) = _mm(sv["merged"], dmo, ta=True, name="mix_out_dw" + tag, out_dtypes=(BF16,))
    dya, dyb, dga, dgb = _merge_bwd(dmerged, sv["z"], sv["ya"], sv["yb"], name="merge_bwd" + tag)
    (gw["p_pool"],) = _mm(sv["yp"], dya, ta=True, name="pool_out_dw" + tag, out_dtypes=(BF16,))
    (dyp,) = _mm(dya, w["p_pool"], tb=True, name="pool_out_dx" + tag)
    du, g_w_pool, g_pool_scale = _pool_bwd(dyp, sv["p"], small["w_pool"], small["pool_scale"], name="pool_bwd" + tag)
    (gw["p_attn"],) = _mm(sv["o"], dyb, ta=True, name="attn_out_dw" + tag, out_dtypes=(BF16,))
    (do,) = _mm(dyb, w["p_attn"], tb=True, name="attn_out_dx" + tag, out_dtypes=(BF16,))
    delta = _attn_delta(do, sv["o"], name="attn_delta" + tag)
    dq, dk, dv = _attn_bwd(sv["q"], sv["k"], sv["v"], do, sv["lse"], delta, name="attn_bwd" + tag)
    dql = _rope_bwd_q(dq, cos_t, sin_t, name="rope_bwd_q" + tag)
    dkb, dkr = _key_bwd(dk, cos_t, sin_t, name="key_bwd" + tag)
    (gw["w_uq"],) = _mm(sv["cq"], dql, ta=True, name="q_proj_dw" + tag, out_dtypes=(BF16,))
    (gw["w_uk"],) = _mm(sv["ckv"], dkb, ta=True, name="k_proj_dw" + tag, out_dtypes=(BF16,))
    (gw["w_uv"],) = _mm(sv["ckv"], dv, ta=True, name="v_proj_dw" + tag, out_dtypes=(BF16,))
    (dcq,) = _mm(dql, w["w_uq"], tb=True, name="q_proj_dx" + tag)
    (dckv,) = _mm(jnp.concatenate([dkb, dv], axis=1), jnp.concatenate([w["w_uk"], w["w_uv"]], axis=1), tb=True,
                  name="kv_proj_dx" + tag)
    q_norm_g = small["q_norm_g"] + send_grads("mix", gw)
    dcq_raw, g_qn = _rms_bwd(dcq, sv["z"], ZC_CQ, sv["rq"], q_norm_g, name="q_norm_bwd" + tag)
    dckv_raw, g_kvn = _rms_bwd(dckv, sv["z"], ZC_CKV, sv["rkv"], small["kv_norm_g"], name="kv_norm_bwd" + tag)
    dz = jnp.concatenate([dcq_raw, dkr, du, dga, dgb, dckv_raw], axis=1)
    (g_in,) = _mm(sv["h"], dz, ta=True, name="in_proj_dw" + tag, out_dtypes=(BF16,))
    sc1 = sc1 + send_grads("in", dict(w_in=g_in))
    (dh,) = _mm(dz, w["w_in"], tb=True, name="in_proj_dx" + tag)
    dx, dln1, dsc1, dsh1 = _norm_mod_bwd(dh, sv["x"], sv["r1"], small["ln1_g"], sc1, dx1, name="norm1_bwd" + tag)
    dmod = jnp.concatenate([dsh1, dsc1, dg1, dsh2, dsc2, dg2], axis=0)
    gsmall = dict(ln1_g=dln1, ln2_g=dln2, q_norm_g=g_qn, kv_norm_g=g_kvn, w_pool=g_w_pool, pool_scale=g_pool_scale)
    return dx, dmod, gsmall


SMALL_LOSS = 6
SMALL_SINGLES = 16
SMALL_POOL = 24
SMALL_POOL_ROWS = len(POOL_WINDOWS) * POOL_GROUP * POOL_GROUP // D_MODEL
SMALL_ROWS = SMALL_POOL + DEPTH * SMALL_POOL_ROWS


def _pack_small(parts, *, name):
    def body(*refs):
        out_ref = refs[-1]
        out_ref[...] = jnp.zeros_like(out_ref)
        for ref, (_, row) in zip(refs[:-1], parts):
            out_ref[row:row + ref.shape[0], :] = ref[...]

    return pl.pallas_call(body, name=name, out_shape=jax.ShapeDtypeStruct((SMALL_ROWS, D_MODEL), F32),
                          compiler_params=_params())(*[a for a, _ in parts])


def kernel(x, c, positions, ln1_g, ln2_g, w_ada, b_ada, w_in, q_norm_g, w_uq, kv_norm_g, w_uk, w_uv, w_pool, pool_scale, p_pool, p_attn, w_out, w_ff1, w_ff2, final_g, loss_target, m_ln1_g, m_ln2_g, m_w_ada, m_b_ada, m_w_in, m_q_norm_g, m_w_uq, m_kv_norm_g, m_w_uk, m_w_uv, m_w_pool, m_pool_scale, m_p_pool, m_p_attn, m_w_out, m_w_ff1, m_w_ff2, m_final_g, v_ln1_g, v_ln2_g, v_w_ada, v_b_ada, v_w_in, v_q_norm_g, v_w_uq, v_kv_norm_g, v_w_uk, v_w_uv, v_w_pool, v_pool_scale, v_p_pool, v_p_attn, v_w_out, v_w_ff1, v_w_ff2, v_final_g):
    weights = dict(ln1_g=ln1_g, ln2_g=ln2_g, w_ada=w_ada, b_ada=b_ada, w_in=w_in, q_norm_g=q_norm_g, w_uq=w_uq,
                   kv_norm_g=kv_norm_g, w_uk=w_uk, w_uv=w_uv, w_pool=w_pool, pool_scale=pool_scale, p_pool=p_pool,
                   p_attn=p_attn, w_out=w_out, w_ff1=w_ff1, w_ff2=w_ff2, final_g=final_g)
    moms = dict(ln1_g=m_ln1_g, ln2_g=m_ln2_g, w_ada=m_w_ada, b_ada=m_b_ada, w_in=m_w_in, q_norm_g=m_q_norm_g,
                w_uq=m_w_uq, kv_norm_g=m_kv_norm_g, w_uk=m_w_uk, w_uv=m_w_uv, w_pool=m_w_pool,
                pool_scale=m_pool_scale, p_pool=m_p_pool, p_attn=m_p_attn, w_out=m_w_out, w_ff1=m_w_ff1,
                w_ff2=m_w_ff2, final_g=m_final_g)
    vels = dict(ln1_g=v_ln1_g, ln2_g=v_ln2_g, w_ada=v_w_ada, b_ada=v_b_ada, w_in=v_w_in, q_norm_g=v_q_norm_g,
                w_uq=v_w_uq, kv_norm_g=v_kv_norm_g, w_uk=v_w_uk, w_uv=v_w_uv, w_pool=v_w_pool,
                pool_scale=v_pool_scale, p_pool=v_p_pool, p_attn=v_p_attn, w_out=v_w_out, w_ff1=v_w_ff1,
                w_ff2=v_w_ff2, final_g=v_final_g)
    order = list(weights)
    seq = x.shape[1]
    my_chip = 2 * lax.axis_index("x") + lax.axis_index("y")
    my_dev = 2 * my_chip + lax.axis_index("c")
    ada_cols = w_ada.shape[2]

    small = [dict(ln1_g=ln1_g[l:l + 1], ln2_g=ln2_g[l:l + 1], q_norm_g=q_norm_g[l:l + 1], kv_norm_g=kv_norm_g[l:l + 1],
                  w_pool=w_pool[l], pool_scale=pool_scale[l:l + 1]) for l in range(DEPTH)]

    c_all = _all_gather_small(jnp.pad(c, ((0, 7), (0, 0))), name="cond_all_gather")
    c_act = _silu(c_all, name="cond_silu")
    b_mine = lax.dynamic_slice_in_dim(b_ada, my_chip * ada_cols, ada_cols, axis=1)
    mod_parts = [_mm(c_act, w_ada[l], name=f"ada_fwd_l{l}", epilogue=lambda acc, b: (acc + b,),
                     extras=((b_mine[l:l + 1], "row"),))[0] for l in range(DEPTH)]
    mod_mine = jnp.concatenate([mp[::8] for mp in mod_parts], axis=0)
    mod_all = _all_gather_small(mod_mine, name="mod_all_gather").reshape(N_DEV, DEPTH, N_DEV, ada_cols)

    zero = mod_all[0, 0, 0, 0] * 0.0
    local, started = {}, {}
    for l in range(DEPTH):
        for group, names in GROUPS:
            packed = jnp.concatenate([_pack_rows(weights[n][l]) for n in names], axis=0)
            local[l, group] = (packed + zero).astype(BF16)
            started[l, group] = _exchange_start(local[l, group], name=f"weights_send_l{l}_{group}", scatter=False)
    pin = sum(st[4][0:1, 0:1] for st in started.values())

    def gathered_weights(l, group, after):
        land = _exchange_wait(started[l, group], after, name=f"weights_wait_l{l}_{group}", scatter=False)
        land = lax.dynamic_update_slice_in_dim(land, local[l, group][None], my_chip, axis=0)
        return _unpack_weights(land, dict(GROUPS)[group])

    mods = []
    for l in range(DEPTH):
        row = jnp.concatenate([lax.dynamic_index_in_dim(mod_all[2 * j, l], my_dev, axis=0, keepdims=True)
                               for j in range(N_CHIPS)], axis=1) + pin
        mods.append([row[:, i * D_MODEL:(i + 1) * D_MODEL] for i in range(N_MOD)])

    inv_freq = ROPE_THETA ** (-jnp.arange(0, QK_ROPE, 2, dtype=F32) / QK_ROPE)
    freq_lanes = jnp.concatenate([jnp.zeros((QK_NOPE,), F32), inv_freq, inv_freq,
                                  jnp.zeros((HEAD_PAD - QK_DIM,), F32)]).reshape(1, LANES)
    cos_t, sin_t = _rope_tables(positions.reshape(seq, 1), freq_lanes, name="rope_tables")

    xs, saved, wl = x.reshape(seq, D_MODEL), [], []
    for l in range(DEPTH):
        xs, sv, w_l = _layer_fwd(l, xs, mods[l], functools.partial(gathered_weights, l), small[l], cos_t, sin_t)
        saved.append(sv)
        wl.append(w_l)
    dx, loss_part, g_final = _final_loss(xs, final_g.reshape(1, D_MODEL), loss_target.reshape(seq, D_MODEL),
                                         name="final_loss")

    sent = []

    def send_grads(l, group, g):
        gpack = _pack_grads(g, dict(GROUPS)[group])
        started_g = _exchange_start(gpack, name=f"grads_send_l{l}_{group}", scatter=True)
        sent.append((l, group, started_g, gpack))
        return started_g[4][0:1, 0:1]

    dmod, gsmall = [None] * DEPTH, [None] * DEPTH
    for l in reversed(range(DEPTH)):
        dx, dmod[l], gsmall[l] = _layer_bwd(l, dx, saved[l], mods[l], wl[l], small[l], cos_t, sin_t,
                                            functools.partial(send_grads, l))
    grads = dict(x=dx.reshape(1, seq, D_MODEL))

    def lanes(a):
        flat = a.reshape(1, -1)
        return jnp.pad(flat, ((0, 0), (0, D_MODEL - flat.shape[1])))

    singles = [gsmall[0]["ln1_g"], gsmall[1]["ln1_g"], gsmall[0]["ln2_g"], gsmall[1]["ln2_g"], g_final,
               lanes(jnp.concatenate([gsmall[l]["pool_scale"] for l in range(DEPTH)], axis=1)),
               lanes(jnp.concatenate([gsmall[l]["q_norm_g"] for l in range(DEPTH)], axis=1)),
               lanes(jnp.concatenate([gsmall[l]["kv_norm_g"] for l in range(DEPTH)], axis=1))]
    parts = [(dmod[0], 0), (lanes(loss_part), SMALL_LOSS), (dmod[1], 8)]
    parts += [(a, SMALL_SINGLES + i) for i, a in enumerate(singles)]
    parts += [(gsmall[l]["w_pool"].reshape(-1, D_MODEL), SMALL_POOL + l * SMALL_POOL_ROWS) for l in range(DEPTH)]
    small_all = _all_gather_small(_pack_small(parts, name="small_grads_pack"), name="small_grads_all_gather")
    small_all = small_all.reshape(N_DEV, SMALL_ROWS, D_MODEL)
    ssum = _sum_slots(small_all, N_DEV, name="small_grads_sum")
    loss = ssum[SMALL_LOSS, 0]
    grads["b_ada"] = jnp.stack([ssum[8 * l:8 * l + N_MOD] for l in range(DEPTH)]).reshape(DEPTH, N_MOD * D_MODEL)
    grads["ln1_g"] = ssum[SMALL_SINGLES:SMALL_SINGLES + 2]
    grads["ln2_g"] = ssum[SMALL_SINGLES + 2:SMALL_SINGLES + 4]
    grads["final_g"] = ssum[SMALL_SINGLES + 4]
    grads["pool_scale"] = ssum[SMALL_SINGLES + 5].reshape(DEPTH, POOL_DIM)
    grads["q_norm_g"] = ssum[SMALL_SINGLES + 6, :DEPTH * Q_LORA].reshape(DEPTH, Q_LORA)
    grads["kv_norm_g"] = ssum[SMALL_SINGLES + 7, :DEPTH * KV_LORA].reshape(DEPTH, KV_LORA)
    grads["w_pool"] = ssum[SMALL_POOL:SMALL_ROWS].reshape(w_pool.shape)

    gsum, after = {}, ssum
    for l, group, started_g, gpack in sent:
        tg = f"_l{l}_{group}"
        land = _exchange_wait(started_g, after, name="grads_wait" + tg, scatter=True)
        own = lax.dynamic_index_in_dim(gpack, my_chip, axis=0, keepdims=True)
        land = lax.dynamic_update_slice_in_dim(land, own, my_chip, axis=0)
        part = _sum_slots(land, N_CHIPS, name="grads_sum_chips" + tg)
        gsum[l, group] = _add2(part, _exchange_sibling(part, name="grads_swap_cores" + tg), name="grads_sum_cores" + tg)
        after = gsum[l, group]
    for group, names in GROUPS:
        off = 0
        for name in names:
            rows = ROWS_OF[name]
            grads[name] = jnp.stack([gsum[l, group][off:off + rows].reshape(weights[name].shape[1:])
                                     for l in range(DEPTH)])
            off += rows

    c_act_t = jnp.pad(c_act[::8].T, ((0, 0), (0, LANES - N_DEV)))
    g_ada = []
    for l in range(DEPTH):
        d_all = small_all[:, 8 * l:8 * l + N_MOD].reshape(N_DEV, N_MOD * D_MODEL)
        d_mine = lax.dynamic_slice_in_dim(d_all, my_chip * ada_cols, ada_cols, axis=1)
        g_ada.append(_mm(c_act_t, jnp.pad(d_mine, ((0, LANES - N_DEV), (0, 0))), name=f"ada_dw_l{l}")[0])
    grads["w_ada"] = jnp.stack(g_ada)

    def view(a):
        return a.reshape(1, -1) if a.ndim == 1 else a.reshape(-1, a.shape[-1])

    delta, new_m, new_v = {}, {}, {}
    for name in order:
        shape = weights[name].shape
        d, nm, nv = _adamw(view(weights[name]), view(grads[name]), view(moms[name]), view(vels[name]),
                           name="adamw_" + name)
        delta[name], new_m[name], new_v[name] = d.reshape(shape), nm.reshape(shape), nv.reshape(shape)
    return (loss, grads["x"], *[grads[n] for n in order], *[delta[n] for n in order],
            *[new_m[n] for n in order], *[new_v[n] for n in order])
```

```python
import functools
import math

import jax
import jax.numpy as jnp
from jax import lax
from jax.experimental import pallas as pl
from jax.experimental.pallas import tpu as pltpu

F32 = jnp.float32
BF16 = jnp.bfloat16
MESH = pl.DeviceIdType.MESH

D_MODEL = 1024
DEPTH = 2
POOL_WINDOWS = (2, 4, 8, 16)
POOL_GROUP = 128
POOL_DIM = 512
N_HEADS = 8
QK_NOPE = 64
QK_ROPE = 32
QK_DIM = QK_NOPE + QK_ROPE
V_DIM = 64
HEAD_PAD = 128
Q_LORA = 384
KV_LORA = 256
ROPE_THETA = 10000.0
ATTN_DIM = N_HEADS * V_DIM
D_FF = 4 * D_MODEL
N_MOD = 6
EPS = 1e-6
N_CHIPS = 4
N_DEV = 8

ADAM_LR = 0.001
ADAM_B1 = 0.9
ADAM_B2 = 0.999
ADAM_EPS = 1e-08
ADAM_WD = 0.01
ADAM_STEP = 10

VMEM_LIMIT_BYTES = 56 * 1024 * 1024
LANES = 128
HALO = 16

ZC_CQ = 0
ZC_KR = 384
ZC_U = 512
ZC_GA = 1024
ZC_GB = 2048
ZC_CKV = 3072
Z_DIM = 3328

PACK_ROWS = (("w_in", 808), ("w_uq", 72), ("w_uk", 32), ("w_uv", 32), ("p_pool", 128), ("p_attn", 128),
             ("w_out", 256), ("w_ff1", 1024), ("w_ff2", 1024))


def _params(sem=None, **kw):
    return pltpu.CompilerParams(dimension_semantics=sem, vmem_limit_bytes=VMEM_LIMIT_BYTES, **kw)


def _tile(n, target, unit=LANES):
    best = None
    for t in range(unit, min(n, target) + 1, unit):
        if n % t == 0:
            best = t
    return best if best is not None and 4 * best >= min(n, target) else n


def _near_tile(n, target):
    cands = [t for t in range(LANES, n + 1, LANES) if n % t == 0]
    return min(cands, key=lambda t: abs(math.log(t / target))) if cands else n


def _mm(a, b, *, name, ta=False, tb=False, out_dtypes=(F32,), epilogue=None, extras=(), tm=1024, tn=1024, tk=1024):
    (k_dim, m_dim) = a.shape if ta else a.shape[::-1]
    (n_dim, k_b) = b.shape if tb else b.shape[::-1]
    assert k_dim == k_b, (a.shape, b.shape)
    tm, tn, tk = _near_tile(m_dim, tm), _near_tile(n_dim, tn), _near_tile(k_dim, tk)
    nk = k_dim // tk
    n_extra, n_out = len(extras), len(out_dtypes)
    dims = (((0 if ta else 1,), (1 if tb else 0,)), ((), ()))
    if epilogue is None:
        epilogue = lambda acc: (acc,) * n_out

    def body(a_ref, b_ref, *rest):
        extra_refs, out_refs = rest[:n_extra], rest[n_extra:n_extra + n_out]

        def product():
            return lax.dot_general(a_ref[...].astype(BF16), b_ref[...].astype(BF16), dims, preferred_element_type=F32)

        def finish(acc):
            outs = epilogue(acc, *[r[...] for r in extra_refs])
            for o_ref, o in zip(out_refs, outs):
                o_ref[...] = o.astype(o_ref.dtype)

        if nk == 1:
            finish(product())
            return
        acc_ref = rest[-1]
        k = pl.program_id(2)

        @pl.when(k == 0)
        def _():
            acc_ref[...] = product()

        @pl.when((k > 0) & (k < nk - 1))
        def _():
            acc_ref[...] += product()

        @pl.when(k == nk - 1)
        def _():
            finish(acc_ref[...] + product())

    a_spec = pl.BlockSpec((tk, tm), lambda i, j, k: (k, i)) if ta else pl.BlockSpec((tm, tk), lambda i, j, k: (i, k))
    b_spec = pl.BlockSpec((tn, tk), lambda i, j, k: (j, k)) if tb else pl.BlockSpec((tk, tn), lambda i, j, k: (k, j))
    extra_specs = []
    for arr, kind in extras:
        if kind == "tile":
            extra_specs.append(pl.BlockSpec((tm, tn), lambda i, j, k: (i, j)))
        elif kind == "row":
            extra_specs.append(pl.BlockSpec((1, tn), lambda i, j, k: (0, j)))
        elif kind == "col":
            extra_specs.append(pl.BlockSpec((tm, 1), lambda i, j, k: (i, 0)))
        else:
            assert kind == "table", kind
            extra_specs.append(pl.BlockSpec((tm, LANES), lambda i, j, k: (i, 0)))
    return pl.pallas_call(
        body,
        name=name,
        grid=(m_dim // tm, n_dim // tn, nk),
        in_specs=[a_spec, b_spec] + extra_specs,
        out_specs=[pl.BlockSpec((tm, tn), lambda i, j, k: (i, j)) for _ in out_dtypes],
        out_shape=[jax.ShapeDtypeStruct((m_dim, n_dim), dt) for dt in out_dtypes],
        scratch_shapes=[pltpu.VMEM((tm, tn), F32)] if nk > 1 else [],
        compiler_params=_params(("parallel", "parallel", "arbitrary")),
    )(a, b, *[arr for arr, _ in extras])


def _rows(s):
    return min(512, s)


def _rope_tables(pos_col, inv_freq_lanes, *, name):
    s = pos_col.shape[0]
    tb = _rows(s)

    def body(pos_ref, f_ref, cos_ref, sin_ref):
        ang = pos_ref[...].astype(F32) * f_ref[...]
        lane = lax.broadcasted_iota(jnp.int32, ang.shape, 1)
        on = (lane >= QK_NOPE) & (lane < QK_DIM)
        cos_ref[...] = jnp.where(on, jnp.cos(ang), 0.0)
        sin_ref[...] = jnp.where(on, jnp.sin(ang), 0.0)

    return pl.pallas_call(
        body, name=name, grid=(s // tb,),
        in_specs=[pl.BlockSpec((tb, 1), lambda i: (i, 0)), pl.BlockSpec((1, LANES), lambda i: (0, 0))],
        out_specs=[pl.BlockSpec((tb, LANES), lambda i: (i, 0))] * 2,
        out_shape=[jax.ShapeDtypeStruct((s, LANES), F32)] * 2,
        compiler_params=_params(("parallel",)),
    )(pos_col, inv_freq_lanes)


def _rotate_half(x):
    lane = lax.broadcasted_iota(jnp.int32, x.shape, 1)
    half = QK_ROPE // 2
    first = (lane >= QK_NOPE) & (lane < QK_NOPE + half)
    second = (lane >= QK_NOPE + half) & (lane < QK_DIM)
    return jnp.where(first, -pltpu.roll(x, LANES - half, 1), jnp.where(second, pltpu.roll(x, half, 1), 0.0))


def _norm_mod(x, g, sc, sh, *, name):
    s, d = x.shape
    tb = _rows(s)

    def body(x_ref, g_ref, sc_ref, sh_ref, h_ref, r_ref):
        xv = x_ref[...]
        r = lax.rsqrt(jnp.mean(xv * xv, axis=-1, keepdims=True) + EPS)
        r_ref[...] = r
        h_ref[...] = (((xv * r) * g_ref[...]) * (1.0 + sc_ref[...]) + sh_ref[...]).astype(BF16)

    vec = pl.BlockSpec((1, d), lambda i: (0, 0))
    return pl.pallas_call(
        body, name=name, grid=(s // tb,),
        in_specs=[pl.BlockSpec((tb, d), lambda i: (i, 0)), vec, vec, vec],
        out_specs=[pl.BlockSpec((tb, d), lambda i: (i, 0)), pl.BlockSpec((tb, 1), lambda i: (i, 0))],
        out_shape=[jax.ShapeDtypeStruct((s, d), BF16), jax.ShapeDtypeStruct((s, 1), F32)],
        compiler_params=_params(("parallel",)),
    )(x, g, sc, sh)


def _window_sums(ext, sign):
    n = ext.shape[0]
    sums, cur, k = [], ext, 1
    for _ in POOL_WINDOWS:
        cur = cur + pltpu.roll(cur, k if sign > 0 else n - k, 0)
        sums.append(cur)
        k *= 2
    return sums


def _mixer_pre(z, cos_t, sin_t, w_pool, pool_scale, gq, gkv, *, name):
    s = z.shape[0]
    tb = _rows(s)
    hb = tb // HALO

    def body(zcq_ref, zkr_ref, zu_ref, zuh_ref, zckv_ref, cos_ref, sin_ref, wp_ref, ps_ref, gq_ref, gkv_ref,
             p_ref, yp_ref, cq_ref, ckv_ref, kr_ref, rq_ref, rkv_ref):
        i = pl.program_id(0)
        u = zu_ref[...]
        halo = jnp.where(i > 0, zuh_ref[...], 0.0)
        ext = jnp.concatenate([halo, u], axis=0)
        t = i * tb + lax.broadcasted_iota(jnp.int32, (tb, 1), 0)
        for g, (w, sw) in enumerate(zip(POOL_WINDOWS, _window_sums(ext, +1))):
            cols = slice(g * POOL_GROUP, (g + 1) * POOL_GROUP)
            cnt = jnp.minimum(t + 1, w).astype(F32)
            pg = (sw[HALO:, cols] / cnt - u[:, cols]).astype(BF16)
            p_ref[:, cols] = pg
            yg = jnp.dot(pg, wp_ref[g].astype(BF16), preferred_element_type=F32)
            yp_ref[:, cols] = (yg * ps_ref[:, cols]).astype(BF16)

        def rms(x_ref, g_ref, out_ref, r_ref):
            xv = x_ref[...]
            r = lax.rsqrt(jnp.mean(xv * xv, axis=-1, keepdims=True) + EPS)
            r_ref[...] = r
            out_ref[...] = ((xv * r) * g_ref[...]).astype(BF16)

        rms(zcq_ref, gq_ref, cq_ref, rq_ref)
        rms(zckv_ref, gkv_ref, ckv_ref, rkv_ref)
        kr = zkr_ref[...]
        kr_ref[...] = (kr * cos_ref[...] + _rotate_half(kr) * sin_ref[...]).astype(BF16)

    def zcol(width, off):
        return pl.BlockSpec((tb, width), lambda i: (i, off // width))

    def full(a):
        return pl.BlockSpec(a.shape, lambda i: (0,) * a.ndim)

    def out(width, dt):
        return pl.BlockSpec((tb, width), lambda i: (i, 0)), jax.ShapeDtypeStruct((s, width), dt)

    outs = [out(POOL_DIM, BF16), out(POOL_DIM, BF16), out(Q_LORA, BF16), out(KV_LORA, BF16), out(LANES, BF16),
            out(1, F32), out(1, F32)]
    return pl.pallas_call(
        body, name=name, grid=(s // tb,),
        in_specs=[zcol(Q_LORA, ZC_CQ), zcol(LANES, ZC_KR), zcol(POOL_DIM, ZC_U),
                  pl.BlockSpec((HALO, POOL_DIM), lambda i: (jnp.maximum(i * hb - 1, 0), ZC_U // POOL_DIM)),
                  zcol(KV_LORA, ZC_CKV),
                  pl.BlockSpec((tb, LANES), lambda i: (i, 0)), pl.BlockSpec((tb, LANES), lambda i: (i, 0)),
                  full(w_pool), full(pool_scale), full(gq), full(gkv)],
        out_specs=[o[0] for o in outs], out_shape=[o[1] for o in outs],
        compiler_params=_params(("parallel",)),
    )(z, z, z, z, z, cos_t, sin_t, w_pool, pool_scale, gq, gkv)


def _sigmoid(x):
    return 1.0 / (1.0 + jnp.exp(-x))


def _merge(z, ya, yb, *, name):
    s, d = ya.shape
    tb = _rows(s)

    def body(ga_ref, gb_ref, ya_ref, yb_ref, out_ref):
        out_ref[...] = (_sigmoid(ga_ref[...]) * ya_ref[...] + _sigmoid(gb_ref[...]) * yb_ref[...]).astype(BF16)

    blk = pl.BlockSpec((tb, d), lambda i: (i, 0))
    return pl.pallas_call(
        body, name=name, grid=(s // tb,),
        in_specs=[pl.BlockSpec((tb, d), lambda i: (i, ZC_GA // d)), pl.BlockSpec((tb, d), lambda i: (i, ZC_GB // d)),
                  blk, blk],
        out_specs=blk, out_shape=jax.ShapeDtypeStruct((s, d), BF16),
        compiler_params=_params(("parallel",)),
    )(z, z, ya, yb)


ATTN_SCALE = 1.0 / math.sqrt(QK_DIM)
NEG_BIG = -1e30


LOG2_E = math.log2(math.e)
EXP2_SCALE = ATTN_SCALE * LOG2_E
NT_DIMS = (((1,), (1,)), ((), ()))
TN_DIMS = (((0,), (0,)), ((), ()))


def _on_or_below_diagonal(t):
    return lax.broadcasted_iota(jnp.int32, (t, t), 0) >= lax.broadcasted_iota(jnp.int32, (t, t), 1)


HEADS_PER_STEP = 2
HEAD_COLS = [slice(g * HEAD_PAD, (g + 1) * HEAD_PAD) for g in range(HEADS_PER_STEP)]


def _attn_fwd(q, k, v, *, name):
    s = q.shape[0]
    t = _rows(s)
    wide = HEADS_PER_STEP * HEAD_PAD

    def body(q_ref, k_ref, v_ref, o_ref, lse_ref):
        qi = pl.program_id(1)
        qs = [q_ref[:, cols] for cols in HEAD_COLS]

        def block(j, carry, diagonal):
            rows = pl.ds(pl.multiple_of(j * t, t), t)
            out = []
            for qv, cols, (m, l, acc) in zip(qs, HEAD_COLS, carry):
                sc = lax.dot_general(qv, k_ref[rows, cols], NT_DIMS, preferred_element_type=F32)
                if diagonal:
                    sc = jnp.where(_on_or_below_diagonal(t), sc, NEG_BIG)
                m_new = jnp.maximum(m, jnp.max(sc, axis=-1, keepdims=True))
                p = jnp.exp2((sc - m_new) * EXP2_SCALE)
                alpha = jnp.exp2((m - m_new) * EXP2_SCALE)
                l = alpha * l + jnp.sum(p, axis=-1, keepdims=True)
                acc = alpha * acc + jnp.dot(p.astype(BF16), v_ref[rows, cols], preferred_element_type=F32)
                out.append((m_new, l, acc))
            return tuple(out)

        init = tuple((jnp.full((t, 1), -jnp.inf, F32), jnp.zeros((t, 1), F32), jnp.zeros((t, HEAD_PAD), F32))
                     for _ in HEAD_COLS)
        carry = lax.fori_loop(0, qi, lambda j, c: block(j, c, False), init)
        for g, (cols, (m, l, acc)) in enumerate(zip(HEAD_COLS, block(qi, carry, True))):
            o_ref[:, cols] = (acc / l).astype(BF16)
            lse_ref[g] = m * ATTN_SCALE + jnp.log(l)

    q_spec = pl.BlockSpec((t, wide), lambda h, i: (i, h))
    kv_spec = pl.BlockSpec((s, wide), lambda h, i: (0, h))
    return pl.pallas_call(
        body, name=name, grid=(N_HEADS // HEADS_PER_STEP, s // t),
        in_specs=[q_spec, kv_spec, kv_spec],
        out_specs=[q_spec, pl.BlockSpec((HEADS_PER_STEP, t, 1), lambda h, i: (h, i, 0))],
        out_shape=[jax.ShapeDtypeStruct((s, N_HEADS * HEAD_PAD), BF16), jax.ShapeDtypeStruct((N_HEADS, s, 1), F32)],
        compiler_params=_params(("parallel", "parallel")),
    )(q, k, v)


def _attn_delta(do, o, *, name):
    s = o.shape[0]
    t = _rows(s)

    def body(do_ref, o_ref, out_ref):
        for h in range(N_HEADS):
            cols = slice(h * HEAD_PAD, (h + 1) * HEAD_PAD)
            out_ref[h] = jnp.sum(do_ref[:, cols].astype(F32) * o_ref[:, cols].astype(F32), axis=-1, keepdims=True)

    blk = pl.BlockSpec((t, N_HEADS * HEAD_PAD), lambda i: (i, 0))
    return pl.pallas_call(
        body, name=name, grid=(s // t,), in_specs=[blk, blk],
        out_specs=pl.BlockSpec((N_HEADS, t, 1), lambda i: (0, i, 0)),
        out_shape=jax.ShapeDtypeStruct((N_HEADS, s, 1), F32),
        compiler_params=_params(("parallel",)),
    )(do, o)


def _attn_bwd(q, k, v, do, lse, delta, *, name):
    s = q.shape[0]
    t = _rows(s)
    nt = s // t

    def body(q_ref, k_ref, v_ref, do_ref, lse_ref, dl_ref, dq_ref, dk_ref, dv_ref):
        kj = pl.program_id(1)

        @pl.when(kj == 0)
        def _():
            dq_ref[...] = jnp.zeros_like(dq_ref)

        kvs = [(k_ref[:, cols], v_ref[:, cols]) for cols in HEAD_COLS]

        def block(i, carry, diagonal):
            rows = pl.ds(pl.multiple_of(i * t, t), t)
            out = []
            for g, (cols, (kv, vv), (dk, dv)) in enumerate(zip(HEAD_COLS, kvs, carry)):
                qv, dov = q_ref[rows, cols], do_ref[rows, cols]
                sc = lax.dot_general(qv, kv, NT_DIMS, preferred_element_type=F32)
                p = jnp.exp2(sc * EXP2_SCALE - lse_ref[g, rows, :] * LOG2_E)
                if diagonal:
                    p = jnp.where(_on_or_below_diagonal(t), p, 0.0)
                dp = lax.dot_general(dov, vv, NT_DIMS, preferred_element_type=F32)
                ds = (p * (dp - dl_ref[g, rows, :])).astype(BF16)
                dv = dv + lax.dot_general(p.astype(BF16), dov, TN_DIMS, preferred_element_type=F32)
                dk = dk + lax.dot_general(ds, qv, TN_DIMS, preferred_element_type=F32)
                dq_ref[rows, cols] += jnp.dot(ds, kv, preferred_element_type=F32) * ATTN_SCALE
                out.append((dk, dv))
            return tuple(out)

        zero = jnp.zeros((t, HEAD_PAD), F32)
        carry = block(kj, tuple((zero, zero) for _ in HEAD_COLS), True)
        for cols, (dk, dv) in zip(HEAD_COLS, lax.fori_loop(kj + 1, nt, lambda i, c: block(i, c, False), carry)):
            dk_ref[:, cols] = dk * ATTN_SCALE
            dv_ref[:, cols] = dv.astype(BF16)

    full_spec = pl.BlockSpec((s, HEADS_PER_STEP * HEAD_PAD), lambda h, j: (0, h))
    kv_spec = pl.BlockSpec((t, HEADS_PER_STEP * HEAD_PAD), lambda h, j: (j, h))
    vec_spec = pl.BlockSpec((HEADS_PER_STEP, s, 1), lambda h, j: (h, 0, 0))
    wide = (s, N_HEADS * HEAD_PAD)
    return pl.pallas_call(
        body, name=name, grid=(N_HEADS // HEADS_PER_STEP, nt),
        in_specs=[full_spec, kv_spec, kv_spec, full_spec, vec_spec, vec_spec],
        out_specs=[full_spec, kv_spec, kv_spec],
        out_shape=[jax.ShapeDtypeStruct(wide, F32), jax.ShapeDtypeStruct(wide, F32), jax.ShapeDtypeStruct(wide, BF16)],
        compiler_params=_params(("parallel", "arbitrary")),
    )(q, k, v, do, lse, delta)


def _acc_specs(widths):
    return ([pl.BlockSpec((1, w), lambda i: (0, 0)) for w in widths],
            [jax.ShapeDtypeStruct((1, w), F32) for w in widths])


def _final_loss(x, g, target, *, name):
    s, d = x.shape
    tb = _rows(s)

    def body(x_ref, g_ref, t_ref, dx_ref, loss_ref, dg_ref):
        @pl.when(pl.program_id(0) == 0)
        def _():
            loss_ref[...] = jnp.zeros_like(loss_ref)
            dg_ref[...] = jnp.zeros_like(dg_ref)

        xv = x_ref[...]
        r = lax.rsqrt(jnp.mean(xv * xv, axis=-1, keepdims=True) + EPS)
        xn = xv * r
        err = xn * g_ref[...] - t_ref[...]
        loss_ref[...] += 0.5 * jnp.sum(jnp.mean(err * err, axis=-1, keepdims=True), axis=0, keepdims=True)
        dy = err / d
        dg_ref[...] += jnp.sum(dy * xn, axis=0, keepdims=True)
        dxn = dy * g_ref[...]
        dx_ref[...] = r * (dxn - xn * jnp.mean(dxn * xn, axis=-1, keepdims=True))

    blk = pl.BlockSpec((tb, d), lambda i: (i, 0))
    acc_specs, acc_shapes = _acc_specs((LANES, d))
    return pl.pallas_call(
        body, name=name, grid=(s // tb,),
        in_specs=[blk, pl.BlockSpec((1, d), lambda i: (0, 0)), blk],
        out_specs=[blk] + acc_specs, out_shape=[jax.ShapeDtypeStruct((s, d), F32)] + acc_shapes,
        compiler_params=_params(("arbitrary",)),
    )(x, g, target)


def _gate_bwd(dx, m, g, *, name):
    s, d = dx.shape
    tb = _rows(s)

    def body(dx_ref, m_ref, g_ref, dm_ref, dg_ref):
        @pl.when(pl.program_id(0) == 0)
        def _():
            dg_ref[...] = jnp.zeros_like(dg_ref)

        dxv = dx_ref[...]
        dm_ref[...] = (dxv * g_ref[...]).astype(BF16)
        dg_ref[...] += jnp.sum(dxv * m_ref[...], axis=0, keepdims=True)

    blk = pl.BlockSpec((tb, d), lambda i: (i, 0))
    acc_specs, acc_shapes = _acc_specs((d,))
    return pl.pallas_call(
        body, name=name, grid=(s // tb,),
        in_specs=[blk, blk, pl.BlockSpec((1, d), lambda i: (0, 0))],
        out_specs=[blk] + acc_specs, out_shape=[jax.ShapeDtypeStruct((s, d), BF16)] + acc_shapes,
        compiler_params=_params(("arbitrary",)),
    )(dx, m, g)


def _norm_mod_bwd(dh, x, r, g, sc, dx_skip, *, name):
    s, d = x.shape
    tb = _rows(s)
    nb = s // tb

    def body(dh_ref, x_ref, r_ref, g_ref, sc_ref, skip_ref, dx_ref, dg_ref, dsc_ref, dsh_ref, da_sc):
        i = pl.program_id(0)

        @pl.when(i == 0)
        def _():
            da_sc[...] = jnp.zeros_like(da_sc)
            dsh_ref[...] = jnp.zeros_like(dsh_ref)

        dhv, rv = dh_ref[...], r_ref[...]
        xn = x_ref[...] * rv
        dsh_ref[...] += jnp.sum(dhv, axis=0, keepdims=True)
        da_sc[...] += jnp.sum(dhv * xn, axis=0, keepdims=True)
        dxn = dhv * (g_ref[...] * (1.0 + sc_ref[...]))
        dx_ref[...] = skip_ref[...] + rv * (dxn - xn * jnp.mean(dxn * xn, axis=-1, keepdims=True))

        @pl.when(i == nb - 1)
        def _():
            dg_ref[...] = da_sc[...] * (1.0 + sc_ref[...])
            dsc_ref[...] = da_sc[...] * g_ref[...]

    blk = pl.BlockSpec((tb, d), lambda i: (i, 0))
    vec = pl.BlockSpec((1, d), lambda i: (0, 0))
    acc_specs, acc_shapes = _acc_specs((d, d, d))
    return pl.pallas_call(
        body, name=name, grid=(nb,),
        in_specs=[blk, blk, pl.BlockSpec((tb, 1), lambda i: (i, 0)), vec, vec, blk],
        out_specs=[blk] + acc_specs, out_shape=[jax.ShapeDtypeStruct((s, d), F32)] + acc_shapes,
        scratch_shapes=[pltpu.VMEM((1, d), F32)],
        compiler_params=_params(("arbitrary",)),
    )(dh, x, r, g, sc, dx_skip)


def _merge_bwd(dmerged, z, ya, yb, *, name):
    s, d = ya.shape
    tb = _rows(s)

    def body(dm_ref, ga_ref, gb_ref, ya_ref, yb_ref, dya_ref, dyb_ref, dga_ref, dgb_ref):
        dm = dm_ref[...]
        for g_ref, y_ref, dy_ref, dg_ref in ((ga_ref, ya_ref, dya_ref, dga_ref), (gb_ref, yb_ref, dyb_ref, dgb_ref)):
            sg = _sigmoid(g_ref[...])
            dy_ref[...] = (dm * sg).astype(BF16)
            dg_ref[...] = (dm * y_ref[...] * (sg * (1.0 - sg))).astype(BF16)

    blk = pl.BlockSpec((tb, d), lambda i: (i, 0))
    return pl.pallas_call(
        body, name=name, grid=(s // tb,),
        in_specs=[blk, pl.BlockSpec((tb, d), lambda i: (i, ZC_GA // d)), pl.BlockSpec((tb, d), lambda i: (i, ZC_GB // d)),
                  blk, blk],
        out_specs=[blk] * 4, out_shape=[jax.ShapeDtypeStruct((s, d), BF16)] * 4,
        compiler_params=_params(("parallel",)),
    )(dmerged, z, z, ya, yb)


def _pool_bwd(dyp, p, w_pool, pool_scale, *, name):
    s = dyp.shape[0]
    tb = _rows(s)
    nb = s // tb
    hb = tb // HALO
    nt_dims = (((1,), (1,)), ((), ()))
    tn_dims = (((0,), (0,)), ((), ()))

    def body(dy_ref, dyn_ref, p_ref, wp_ref, ps_ref, du_ref, gwp_ref, gps_ref):
        i = pl.program_id(0)

        @pl.when(i == 0)
        def _():
            gwp_ref[...] = jnp.zeros_like(gwp_ref)
            gps_ref[...] = jnp.zeros_like(gps_ref)

        cur = dy_ref[...]
        nxt = jnp.where(i < nb - 1, dyn_ref[...], 0.0)
        dpw = (jnp.concatenate([cur, nxt], axis=0) * ps_ref[...]).astype(BF16)
        t = i * tb + lax.broadcasted_iota(jnp.int32, (tb + HALO, 1), 0)
        for g, w in enumerate(POOL_WINDOWS):
            cols = slice(g * POOL_GROUP, (g + 1) * POOL_GROUP)
            wg = wp_ref[g].astype(BF16)
            dp = lax.dot_general(dpw[:, cols], wg, nt_dims, preferred_element_type=F32)
            e = dp / jnp.minimum(t + 1, w).astype(F32)
            lead = _window_sums(e, -1)[g]
            du_ref[:, cols] = (lead[:tb] - dp[:tb]).astype(BF16)
            pg = p_ref[:, cols]
            pw = jnp.dot(pg, wg, preferred_element_type=F32)
            gps_ref[:, cols] += jnp.sum(cur[:, cols] * pw, axis=0, keepdims=True)
            gwp_ref[g] += lax.dot_general(pg, dpw[:tb, cols], tn_dims, preferred_element_type=F32)

    blk = pl.BlockSpec((tb, POOL_DIM), lambda i: (i, 0))
    return pl.pallas_call(
        body, name=name, grid=(nb,),
        in_specs=[blk, pl.BlockSpec((HALO, POOL_DIM), lambda i: (jnp.minimum((i + 1) * hb, s // HALO - 1), 0)), blk,
                  pl.BlockSpec(w_pool.shape, lambda i: (0, 0, 0)), pl.BlockSpec((1, POOL_DIM), lambda i: (0, 0))],
        out_specs=[blk, pl.BlockSpec(w_pool.shape, lambda i: (0, 0, 0)), pl.BlockSpec((1, POOL_DIM), lambda i: (0, 0))],
        out_shape=[jax.ShapeDtypeStruct((s, POOL_DIM), BF16), jax.ShapeDtypeStruct(w_pool.shape, F32),
                   jax.ShapeDtypeStruct((1, POOL_DIM), F32)],
        compiler_params=_params(("arbitrary",)),
    )(dyp, dyp, p, w_pool, pool_scale)


def _rope_bwd_q(dq, cos_t, sin_t, *, name):
    s = dq.shape[0]
    tb = _rows(s)

    def body(dq_ref, cos_ref, sin_ref, out_ref):
        sin = sin_ref[...]
        lane = lax.broadcasted_iota(jnp.int32, sin.shape, 1)
        cos_q = cos_ref[...] + jnp.where(lane < QK_NOPE, 1.0, 0.0)
        for h in range(N_HEADS):
            cols = slice(h * HEAD_PAD, (h + 1) * HEAD_PAD)
            dqv = dq_ref[:, cols]
            out_ref[:, cols] = (dqv * cos_q - _rotate_half(dqv * sin)).astype(BF16)

    blk = pl.BlockSpec((tb, N_HEADS * HEAD_PAD), lambda i: (i, 0))
    tab = pl.BlockSpec((tb, LANES), lambda i: (i, 0))
    return pl.pallas_call(
        body, name=name, grid=(s // tb,), in_specs=[blk, tab, tab], out_specs=blk,
        out_shape=jax.ShapeDtypeStruct(dq.shape, BF16),
        compiler_params=_params(("parallel",)),
    )(dq, cos_t, sin_t)


def _key_bwd(dk, cos_t, sin_t, *, name):
    s = dk.shape[0]
    tb = _rows(s)

    def body(dk_ref, cos_ref, sin_ref, dkb_ref, dkr_ref):
        dkv = dk_ref[...]
        dkb_ref[...] = dkv.astype(BF16)
        tot = dkv[:, :HEAD_PAD]
        for h in range(1, N_HEADS):
            tot = tot + dkv[:, h * HEAD_PAD:(h + 1) * HEAD_PAD]
        dkr_ref[...] = (tot * cos_ref[...] - _rotate_half(tot * sin_ref[...])).astype(BF16)

    blk = pl.BlockSpec((tb, N_HEADS * HEAD_PAD), lambda i: (i, 0))
    tab = pl.BlockSpec((tb, LANES), lambda i: (i, 0))
    return pl.pallas_call(
        body, name=name, grid=(s // tb,), in_specs=[blk, tab, tab], out_specs=[blk, tab],
        out_shape=[jax.ShapeDtypeStruct(dk.shape, BF16), jax.ShapeDtypeStruct((s, LANES), BF16)],
        compiler_params=_params(("parallel",)),
    )(dk, cos_t, sin_t)


def _rms_bwd(dy, z, z_off, r, g, *, name):
    s, n = dy.shape
    tb = _rows(s)

    def body(dy_ref, x_ref, r_ref, g_ref, dx_ref, dg_ref):
        @pl.when(pl.program_id(0) == 0)
        def _():
            dg_ref[...] = jnp.zeros_like(dg_ref)

        dyv, rv = dy_ref[...], r_ref[...]
        xn = x_ref[...] * rv
        dg_ref[...] += jnp.sum(dyv * xn, axis=0, keepdims=True)
        dxn = dyv * g_ref[...]
        dx_ref[...] = (rv * (dxn - xn * jnp.mean(dxn * xn, axis=-1, keepdims=True))).astype(BF16)

    blk = pl.BlockSpec((tb, n), lambda i: (i, 0))
    acc_specs, acc_shapes = _acc_specs((n,))
    return pl.pallas_call(
        body, name=name, grid=(s // tb,),
        in_specs=[blk, pl.BlockSpec((tb, n), lambda i: (i, z_off // n)), pl.BlockSpec((tb, 1), lambda i: (i, 0)),
                  pl.BlockSpec((1, n), lambda i: (0, 0))],
        out_specs=[blk] + acc_specs, out_shape=[jax.ShapeDtypeStruct((s, n), BF16)] + acc_shapes,
        compiler_params=_params(("arbitrary",)),
    )(dy, z, r, g)


def _silu(c, *, name):
    def body(c_ref, out_ref):
        cv = c_ref[...]
        out_ref[...] = (cv * _sigmoid(cv)).astype(BF16)

    return pl.pallas_call(body, name=name, out_shape=jax.ShapeDtypeStruct(c.shape, BF16),
                          compiler_params=_params())(c)


def _sum_slots(a, n, *, name, out_dtype=F32):
    _, rows, cols = a.shape
    tr = _tile(rows, 256, 8)

    def body(a_ref, out_ref):
        tot = a_ref[0].astype(F32)
        for j in range(1, n):
            tot = tot + a_ref[j].astype(F32)
        out_ref[...] = tot.astype(out_dtype)

    return pl.pallas_call(
        body, name=name, grid=(rows // tr,),
        in_specs=[pl.BlockSpec((n, tr, cols), lambda i: (0, i, 0))],
        out_specs=pl.BlockSpec((tr, cols), lambda i: (i, 0)),
        out_shape=jax.ShapeDtypeStruct((rows, cols), out_dtype),
        compiler_params=_params(("parallel",)),
    )(a)


def _add2(a, b, *, name):
    rows, cols = a.shape
    tr = _tile(rows, 256, 8)

    def body(a_ref, b_ref, out_ref):
        out_ref[...] = a_ref[...] + b_ref[...]

    blk = pl.BlockSpec((tr, cols), lambda i: (i, 0))
    return pl.pallas_call(
        body, name=name, grid=(rows // tr,), in_specs=[blk, blk], out_specs=blk,
        out_shape=jax.ShapeDtypeStruct((rows, cols), F32),
        compiler_params=_params(("parallel",)),
    )(a, b)


def _adamw(w, g, m, v, *, name):
    rows, cols = w.shape
    tr = _tile(rows, max(8, (1 << 18) // cols), 8)
    c1 = 1.0 - ADAM_B1 ** ADAM_STEP
    c2 = 1.0 - ADAM_B2 ** ADAM_STEP

    def body(w_ref, g_ref, m_ref, v_ref, d_ref, nm_ref, nv_ref):
        gv = g_ref[...]
        nm = ADAM_B1 * m_ref[...] + (1.0 - ADAM_B1) * gv
        nv = ADAM_B2 * v_ref[...] + (1.0 - ADAM_B2) * (gv * gv)
        nm_ref[...] = nm
        nv_ref[...] = nv
        d_ref[...] = -ADAM_LR * ((nm / c1) / (jnp.sqrt(nv / c2) + ADAM_EPS) + ADAM_WD * w_ref[...])

    blk = pl.BlockSpec((tr, cols), lambda i: (i, 0))
    return pl.pallas_call(
        body, name=name, grid=(rows // tr,), in_specs=[blk] * 4, out_specs=[blk] * 3,
        out_shape=[jax.ShapeDtypeStruct((rows, cols), F32)] * 3,
        compiler_params=_params(("parallel",)),
    )(w, g, m, v)


def _coords():
    return lax.axis_index("x"), lax.axis_index("y"), lax.axis_index("c")


def _other_chips(x, y):
    return [(1 - x, y), (x, 1 - y), (1 - x, 1 - y)]


def _all_gather_small(blk, *, name):
    m_per, n = blk.shape

    def body(x_ref, out_ref, send_sems, recv_sems, local_sem):
        x, y, c = _coords()
        me, sibling = (x, y, c), (x, y, 1 - c)
        chips = _other_chips(x, y)

        def rows(px, py, pc):
            return out_ref.at[pl.ds((4 * px + 2 * py + pc) * m_per, m_per), :]

        def copy(k, block, to, src=None):
            return pltpu.make_async_remote_copy(
                src_ref=rows(*block) if src is None else src, dst_ref=rows(*block),
                send_sem=send_sems.at[k], recv_sem=recv_sems.at[k], device_id=to, device_id_type=MESH)

        mine = pltpu.make_async_copy(x_ref, rows(*me), local_sem)
        mine.start()
        first = [copy(0, me, sibling, src=x_ref)]
        first += [copy(1 + j, me, (*chip, c), src=x_ref) for j, chip in enumerate(chips)]
        for cp in first:
            cp.start()
        passed = [copy(4 + j, (*chip, c), sibling) for j, chip in enumerate(chips)]
        for j, chip in enumerate(chips):
            copy(1 + j, (*chip, c), me).wait_recv()
            passed[j].start()
        copy(0, sibling, me).wait_recv()
        for j, chip in enumerate(chips):
            copy(4 + j, (*chip, 1 - c), me).wait_recv()
        for cp in first + passed:
            cp.wait_send()
        mine.wait()

    return pl.pallas_call(
        body, name=name,
        out_shape=jax.ShapeDtypeStruct((N_DEV * m_per, n), blk.dtype),
        in_specs=[pl.BlockSpec(memory_space=pltpu.VMEM)],
        out_specs=pl.BlockSpec(memory_space=pltpu.VMEM),
        scratch_shapes=[pltpu.SemaphoreType.DMA((7,)), pltpu.SemaphoreType.DMA((7,)), pltpu.SemaphoreType.DMA],
        compiler_params=_params(),
    )(blk)


HBM_SPEC = pl.BlockSpec(memory_space=pltpu.HBM)
SEM_SPEC = pl.BlockSpec(memory_space=pltpu.SEMAPHORE)
DATAFLOW = pltpu.SideEffectType.DATAFLOW_SIDE_EFFECTING


def _chip_copies(src_ref, land_ref, send_sems, recv_sems, scatter):
    x, y, c = _coords()
    my = 2 * x + y
    outgoing, incoming = [], []
    for k, (px, py) in enumerate(_other_chips(x, y)):
        peer = 2 * px + py

        def copy(src_slot, dst_slot):
            return pltpu.make_async_remote_copy(
                src_ref=src_ref.at[src_slot] if scatter else src_ref, dst_ref=land_ref.at[dst_slot],
                send_sem=send_sems.at[k], recv_sem=recv_sems.at[k], device_id=(px, py, c), device_id_type=MESH)

        outgoing.append(copy(peer, my))
        incoming.append(copy(my, peer))
    return outgoing, incoming


def _exchange_start(src, *, name, scatter):
    land_shape = src.shape if scatter else (N_CHIPS,) + src.shape

    def body(src_ref, land_ref, send_sems, recv_sems, src_thru, land_thru, token):
        outgoing, _ = _chip_copies(src_ref, land_ref, send_sems, recv_sems, scatter)
        for cp in outgoing:
            cp.start()
        token[...] = jnp.zeros_like(token)

    return pl.pallas_call(
        body, name=name,
        out_shape=(pltpu.SemaphoreType.DMA((N_CHIPS - 1,)), pltpu.SemaphoreType.DMA((N_CHIPS - 1,)),
                   pltpu.HBM(src.shape, src.dtype), pltpu.HBM(land_shape, src.dtype), jax.ShapeDtypeStruct((8, LANES), F32)),
        in_specs=(HBM_SPEC, HBM_SPEC),
        out_specs=(SEM_SPEC, SEM_SPEC, HBM_SPEC, HBM_SPEC, pl.BlockSpec(memory_space=pltpu.VMEM)),
        input_output_aliases={0: 2, 1: 3},
        compiler_params=pltpu.CompilerParams(has_side_effects=DATAFLOW),
    )(pltpu.with_memory_space_constraint(src, pltpu.HBM),
      pltpu.with_memory_space_constraint(lax.empty(land_shape, src.dtype), pltpu.HBM))


def _exchange_wait(started, after, *, name, scatter):
    send_sems, recv_sems, src_thru, land_thru, _ = started

    def body(src_ref, land_ref, send_sems, recv_sems, after_ref, src_dead, got_ref):
        outgoing, incoming = _chip_copies(src_ref, land_ref, send_sems, recv_sems, scatter)
        for cp in outgoing:
            cp.wait_send()
        for cp in incoming:
            cp.wait_recv()

    return pl.pallas_call(
        body, name=name,
        out_shape=(pltpu.HBM(src_thru.shape, src_thru.dtype), pltpu.HBM(land_thru.shape, land_thru.dtype)),
        in_specs=(HBM_SPEC, HBM_SPEC, SEM_SPEC, SEM_SPEC, pl.BlockSpec(memory_space=pl.ANY)),
        out_specs=(HBM_SPEC, HBM_SPEC),
        input_output_aliases={0: 0, 1: 1},
        compiler_params=pltpu.CompilerParams(has_side_effects=DATAFLOW),
    )(src_thru, land_thru, send_sems, recv_sems, after)[1]


def _exchange_sibling(src, *, name):
    def body(src_ref, out_ref, send_sem, recv_sem):
        x, y, c = _coords()
        cp = pltpu.make_async_remote_copy(src_ref=src_ref, dst_ref=out_ref, send_sem=send_sem, recv_sem=recv_sem,
                                          device_id=(x, y, 1 - c), device_id_type=MESH)
        cp.start()
        cp.wait()

    return pl.pallas_call(
        body, name=name,
        out_shape=jax.ShapeDtypeStruct(src.shape, src.dtype),
        in_specs=[pl.BlockSpec(memory_space=pl.ANY)],
        out_specs=pl.BlockSpec(memory_space=pl.ANY),
        scratch_shapes=[pltpu.SemaphoreType.DMA, pltpu.SemaphoreType.DMA],
        compiler_params=_params(),
    )(src)


def _pack_rows(a):
    return a.reshape(-1, D_MODEL)


def _pad_heads(w, width):
    r = w.shape[0]
    return jnp.pad(w, ((0, 0), (0, 0), (0, HEAD_PAD - width))).reshape(r, N_HEADS * HEAD_PAD)


IN_NAMES = ("w_in",)
MIX_NAMES = ("w_uq", "w_uk", "w_uv", "p_pool", "p_attn", "w_out")
FFN_NAMES = ("w_ff1", "w_ff2")
GROUPS = (("in", IN_NAMES), ("mix", MIX_NAMES), ("ffn", FFN_NAMES))
ROWS_OF = dict(PACK_ROWS)


def _unpack_weights(gathered, names):
    def cols(a, k):
        return a.reshape(N_CHIPS, k, -1).transpose(1, 0, 2).reshape(k, -1)

    def w_in(a):
        full = cols(a, D_MODEL)
        kr = jnp.pad(full[:, 1152:1184], ((0, 0), (QK_NOPE, HEAD_PAD - QK_DIM)))
        return jnp.concatenate([full[:, 512:896], kr, full[:, 0:512], full[:, 1184:3232], full[:, 896:1152]], axis=1)

    def p_attn(a):
        full = cols(a, ATTN_DIM).reshape(N_HEADS, V_DIM, D_MODEL)
        return jnp.pad(full, ((0, 0), (0, HEAD_PAD - V_DIM), (0, 0))).reshape(N_HEADS * HEAD_PAD, D_MODEL)

    build = dict(
        w_in=w_in,
        w_uq=lambda a: _pad_heads(a.reshape(Q_LORA, N_HEADS, QK_DIM), QK_DIM),
        w_uk=lambda a: _pad_heads(a.reshape(KV_LORA, N_HEADS, QK_NOPE), QK_NOPE),
        w_uv=lambda a: _pad_heads(a.reshape(KV_LORA, N_HEADS, V_DIM), V_DIM),
        p_pool=lambda a: cols(a, POOL_DIM),
        p_attn=p_attn,
        w_out=lambda a: a.reshape(D_MODEL, D_MODEL),
        w_ff1=lambda a: cols(a, D_MODEL),
        w_ff2=lambda a: a.reshape(D_FF, D_MODEL),
    )
    w, off = {}, 0
    for name in names:
        w[name] = build[name](gathered[:, off:off + ROWS_OF[name]])
        off += ROWS_OF[name]
    return w


def _pack_grads(g, names):
    def cols(a):
        k = a.shape[0]
        return a.reshape(k, N_CHIPS, -1).transpose(1, 0, 2).reshape(N_CHIPS, -1, D_MODEL)

    def rows(a):
        return a.reshape(N_CHIPS, -1, D_MODEL)

    def heads(width):
        return lambda a: rows(a.reshape(a.shape[0], N_HEADS, HEAD_PAD)[:, :, :width])

    def w_in(gi):
        return cols(jnp.concatenate([gi[:, ZC_U:ZC_U + 512], gi[:, ZC_CQ:ZC_CQ + 384], gi[:, ZC_CKV:ZC_CKV + 256],
                                     gi[:, ZC_KR + QK_NOPE:ZC_KR + QK_DIM], gi[:, ZC_GA:ZC_GA + 2048]], axis=1))

    def p_attn(a):
        return cols(a.reshape(N_HEADS, HEAD_PAD, D_MODEL)[:, :V_DIM].reshape(ATTN_DIM, D_MODEL))

    build = dict(w_in=w_in, w_uq=heads(QK_DIM), w_uk=heads(QK_NOPE), w_uv=heads(V_DIM), p_pool=cols, p_attn=p_attn,
                 w_out=rows, w_ff1=cols, w_ff2=rows)
    return jnp.concatenate([build[name](g[name]) for name in names], axis=1)


def _per_head(fn, acc, *tables):
    return jnp.concatenate([fn(acc[:, h * HEAD_PAD:(h + 1) * HEAD_PAD], *tables) for h in range(N_HEADS)], axis=1)


def _rope_head(a, cos, sin):
    lane = lax.broadcasted_iota(jnp.int32, a.shape, 1)
    return a * (cos + jnp.where(lane < QK_NOPE, 1.0, 0.0)) + _rotate_half(a) * sin


def _layer_fwd(l, x, mod, get_weights, small, cos_t, sin_t):
    sh1, sc1, g1, sh2, sc2, g2 = mod
    tag = f"_l{l}"
    h, r1 = _norm_mod(x, small["ln1_g"], sc1, sh1, name="norm1" + tag)
    w = dict(get_weights("in", h))
    (z,) = _mm(h, w["w_in"], name="in_proj" + tag)
    p, yp, cq, ckv, kr, rq, rkv = _mixer_pre(z, cos_t, sin_t, small["w_pool"], small["pool_scale"],
                                              small["q_norm_g"], small["kv_norm_g"], name="mixer_pre" + tag)
    w.update(get_weights("mix", yp))
    (ya,) = _mm(yp, w["p_pool"], name="pool_out" + tag)
    (q,) = _mm(cq, w["w_uq"], name="q_proj" + tag, out_dtypes=(BF16,),
               epilogue=lambda acc, cos, sin: (_per_head(_rope_head, acc, cos, sin),),
               extras=((cos_t, "table"), (sin_t, "table")))
    (k,) = _mm(ckv, w["w_uk"], name="k_proj" + tag, out_dtypes=(BF16,),
               epilogue=lambda acc, krv: (_per_head(lambda a, b: a + b, acc, krv),), extras=((kr, "table"),))
    (v,) = _mm(ckv, w["w_uv"], name="v_proj" + tag, out_dtypes=(BF16,))
    o, lse = _attn_fwd(q, k, v, name="attn_fwd" + tag)
    (yb,) = _mm(o, w["p_attn"], name="attn_out" + tag)
    merged = _merge(z, ya, yb, name="merge" + tag)
    mo, x1 = _mm(merged, w["w_out"], name="mix_out" + tag, out_dtypes=(F32, F32),
                 epilogue=lambda acc, xr, g: (acc, xr + g * acc), extras=((x, "tile"), (g1, "row")), tm=512)
    h2, r2 = _norm_mod(x1, small["ln2_g"], sc2, sh2, name="norm2" + tag)
    w.update(get_weights("ffn", merged))
    f, act = _mm(h2, w["w_ff1"], name="ff1" + tag, out_dtypes=(F32, BF16),
                 epilogue=lambda acc: (acc, jnp.square(jnp.maximum(acc, 0.0))))
    m2, x2 = _mm(act, w["w_ff2"], name="ff2" + tag, out_dtypes=(F32, F32),
                 epilogue=lambda acc, xr, g: (acc, xr + g * acc), extras=((x1, "tile"), (g2, "row")), tm=512)
    saved = dict(x=x, h=h, r1=r1, z=z, p=p, yp=yp, cq=cq, ckv=ckv, rq=rq, rkv=rkv, ya=ya, q=q, k=k, v=v, o=o, lse=lse,
                 yb=yb, merged=merged, mo=mo, x1=x1, h2=h2, r2=r2, f=f, act=act, m2=m2)
    return x2, saved, w


def _layer_bwd(l, dx2, sv, mod, w, small, cos_t, sin_t, send_grads):
    sh1, sc1, g1, sh2, sc2, g2 = mod
    tag = f"_l{l}"
    gw = {}
    dm2, dg2 = _gate_bwd(dx2, sv["m2"], g2, name="gate2_bwd" + tag)
    (df,) = _mm(dm2, w["w_ff2"], tb=True, name="ff2_dx" + tag, out_dtypes=(BF16,),
                epilogue=lambda acc, f: (acc * (2.0 * jnp.maximum(f, 0.0)),), extras=((sv["f"], "tile"),))
    (g_ff2,) = _mm(sv["act"], dm2, ta=True, name="ff2_dw" + tag, out_dtypes=(BF16,))
    (g_ff1,) = _mm(sv["h2"], df, ta=True, name="ff1_dw" + tag, out_dtypes=(BF16,))
    sc2 = sc2 + send_grads("ffn", dict(w_ff1=g_ff1, w_ff2=g_ff2))
    (dh2,) = _mm(df, w["w_ff1"], tb=True, name="ff1_dx" + tag)
    dx1, dln2, dsc2, dsh2 = _norm_mod_bwd(dh2, sv["x1"], sv["r2"], small["ln2_g"], sc2, dx2, name="norm2_bwd" + tag)
    dmo, dg1 = _gate_bwd(dx1, sv["mo"], g1, name="gate1_bwd" + tag)
    (dmerged,) = _mm(dmo, w["w_out"], tb=True, name="mix_out_dx" + tag)
    (gw["w_out"],) = _mm(sv["merged"], dmo, ta=True, name="mix_out_dw" + tag, out_dtypes=(BF16,))
    dya, dyb, dga, dgb = _merge_bwd(dmerged, sv["z"], sv["ya"], sv["yb"], name="merge_bwd" + tag)
    (gw["p_pool"],) = _mm(sv["yp"], dya, ta=True, name="pool_out_dw" + tag, out_dtypes=(BF16,))
    (dyp,) = _mm(dya, w["p_pool"], tb=True, name="pool_out_dx" + tag)
    du, g_w_pool, g_pool_scale = _pool_bwd(dyp, sv["p"], small["w_pool"], small["pool_scale"], name="pool_bwd" + tag)
    (gw["p_attn"],) = _mm(sv["o"], dyb, ta=True, name="attn_out_dw" + tag, out_dtypes=(BF16,))
    (do,) = _mm(dyb, w["p_attn"], tb=True, name="attn_out_dx" + tag, out_dtypes=(BF16,))
    delta = _attn_delta(do, sv["o"], name="attn_delta" + tag)
    dq, dk, dv = _attn_bwd(sv["q"], sv["k"], sv["v"], do, sv["lse"], delta, name="attn_bwd" + tag)
    dql = _rope_bwd_q(dq, cos_t, sin_t, name="rope_bwd_q" + tag)
    dkb, dkr = _key_bwd(dk, cos_t, sin_t, name="key_bwd" + tag)
    (gw["w_uq"],) = _mm(sv["cq"], dql, ta=True, name="q_proj_dw" + tag, out_dtypes=(BF16,))
    (gw["w_uk"],) = _mm(sv["ckv"], dkb, ta=True, name="k_proj_dw" + tag, out_dtypes=(BF16,))
    (gw["w_uv"],) = _mm(sv["ckv"], dv, ta=True, name="v_proj_dw" + tag, out_dtypes=(BF16,))
    (dcq,) = _mm(dql, w["w_uq"], tb=True, name="q_proj_dx" + tag)
    (dckv,) = _mm(jnp.concatenate([dkb, dv], axis=1), jnp.concatenate([w["w_uk"], w["w_uv"]], axis=1), tb=True,
                  name="kv_proj_dx" + tag)
    q_norm_g = small["q_norm_g"] + send_grads("mix", gw)
    dcq_raw, g_qn = _rms_bwd(dcq, sv["z"], ZC_CQ, sv["rq"], q_norm_g, name="q_norm_bwd" + tag)
    dckv_raw, g_kvn = _rms_bwd(dckv, sv["z"], ZC_CKV, sv["rkv"], small["kv_norm_g"], name="kv_norm_bwd" + tag)
    dz = jnp.concatenate([dcq_raw, dkr, du, dga, dgb, dckv_raw], axis=1)
    (g_in,) = _mm(sv["h"], dz, ta=True, name="in_proj_dw" + tag, out_dtypes=(BF16,))
    sc1 = sc1 + send_grads("in", dict(w_in=g_in))
    (dh,) = _mm(dz, w["w_in"], tb=True, name="in_proj_dx" + tag)
    dx, dln1, dsc1, dsh1 = _norm_mod_bwd(dh, sv["x"], sv["r1"], small["ln1_g"], sc1, dx1, name="norm1_bwd" + tag)
    dmod = jnp.concatenate([dsh1, dsc1, dg1, dsh2, dsc2, dg2], axis=0)
    gsmall = dict(ln1_g=dln1, ln2_g=dln2, q_norm_g=g_qn, kv_norm_g=g_kvn, w_pool=g_w_pool, pool_scale=g_pool_scale)
    return dx, dmod, gsmall


SMALL_LOSS = 6
SMALL_SINGLES = 16
SMALL_POOL = 24
SMALL_POOL_ROWS = len(POOL_WINDOWS) * POOL_GROUP * POOL_GROUP // D_MODEL
SMALL_ROWS = SMALL_POOL + DEPTH * SMALL_POOL_ROWS


def _pack_small(parts, *, name):
    def body(*refs):
        out_ref = refs[-1]
        out_ref[...] = jnp.zeros_like(out_ref)
        for ref, (_, row) in zip(refs[:-1], parts):
            out_ref[row:row + ref.shape[0], :] = ref[...]

    return pl.pallas_call(body, name=name, out_shape=jax.ShapeDtypeStruct((SMALL_ROWS, D_MODEL), F32),
                          compiler_params=_params())(*[a for a, _ in parts])


def kernel(x, c, positions, ln1_g, ln2_g, w_ada, b_ada, w_in, q_norm_g, w_uq, kv_norm_g, w_uk, w_uv, w_pool, pool_scale, p_pool, p_attn, w_out, w_ff1, w_ff2, final_g, loss_target, m_ln1_g, m_ln2_g, m_w_ada, m_b_ada, m_w_in, m_q_norm_g, m_w_uq, m_kv_norm_g, m_w_uk, m_w_uv, m_w_pool, m_pool_scale, m_p_pool, m_p_attn, m_w_out, m_w_ff1, m_w_ff2, m_final_g, v_ln1_g, v_ln2_g, v_w_ada, v_b_ada, v_w_in, v_q_norm_g, v_w_uq, v_kv_norm_g, v_w_uk, v_w_uv, v_w_pool, v_pool_scale, v_p_pool, v_p_attn, v_w_out, v_w_ff1, v_w_ff2, v_final_g):
    weights = dict(ln1_g=ln1_g, ln2_g=ln2_g, w_ada=w_ada, b_ada=b_ada, w_in=w_in, q_norm_g=q_norm_g, w_uq=w_uq,
                   kv_norm_g=kv_norm_g, w_uk=w_uk, w_uv=w_uv, w_pool=w_pool, pool_scale=pool_scale, p_pool=p_pool,
                   p_attn=p_attn, w_out=w_out, w_ff1=w_ff1, w_ff2=w_ff2, final_g=final_g)
    moms = dict(ln1_g=m_ln1_g, ln2_g=m_ln2_g, w_ada=m_w_ada, b_ada=m_b_ada, w_in=m_w_in, q_norm_g=m_q_norm_g,
                w_uq=m_w_uq, kv_norm_g=m_kv_norm_g, w_uk=m_w_uk, w_uv=m_w_uv, w_pool=m_w_pool,
                pool_scale=m_pool_scale, p_pool=m_p_pool, p_attn=m_p_attn, w_out=m_w_out, w_ff1=m_w_ff1,
                w_ff2=m_w_ff2, final_g=m_final_g)
    vels = dict(ln1_g=v_ln1_g, ln2_g=v_ln2_g, w_ada=v_w_ada, b_ada=v_b_ada, w_in=v_w_in, q_norm_g=v_q_norm_g,
                w_uq=v_w_uq, kv_norm_g=v_kv_norm_g, w_uk=v_w_uk, w_uv=v_w_uv, w_pool=v_w_pool,
                pool_scale=v_pool_scale, p_pool=v_p_pool, p_attn=v_p_attn, w_out=v_w_out, w_ff1=v_w_ff1,
                w_ff2=v_w_ff2, final_g=v_final_g)
    order = list(weights)
    seq = x.shape[1]
    my_chip = 2 * lax.axis_index("x") + lax.axis_index("y")
    my_dev = 2 * my_chip + lax.axis_index("c")
    ada_cols = w_ada.shape[2]

    small = [dict(ln1_g=ln1_g[l:l + 1], ln2_g=ln2_g[l:l + 1], q_norm_g=q_norm_g[l:l + 1], kv_norm_g=kv_norm_g[l:l + 1],
                  w_pool=w_pool[l], pool_scale=pool_scale[l:l + 1]) for l in range(DEPTH)]

    c_all = _all_gather_small(jnp.pad(c, ((0, 7), (0, 0))), name="cond_all_gather")
    c_act = _silu(c_all, name="cond_silu")
    b_mine = lax.dynamic_slice_in_dim(b_ada, my_chip * ada_cols, ada_cols, axis=1)
    mod_parts = [_mm(c_act, w_ada[l], name=f"ada_fwd_l{l}", epilogue=lambda acc, b: (acc + b,),
                     extras=((b_mine[l:l + 1], "row"),))[0] for l in range(DEPTH)]
    mod_mine = jnp.concatenate([mp[::8] for mp in mod_parts], axis=0)
    mod_all = _all_gather_small(mod_mine, name="mod_all_gather").reshape(N_DEV, DEPTH, N_DEV, ada_cols)

    zero = mod_all[0, 0, 0, 0] * 0.0
    local, started = {}, {}
    for l in range(DEPTH):
        for group, names in GROUPS:
            packed = jnp.concatenate([_pack_rows(weights[n][l]) for n in names], axis=0)
            local[l, group] = (packed + zero).astype(BF16)
            started[l, group] = _exchange_start(local[l, group], name=f"weights_send_l{l}_{group}", scatter=False)
    pin = sum(st[4][0:1, 0:1] for st in started.values())

    def gathered_weights(l, group, after):
        land = _exchange_wait(started[l, group], after, name=f"weights_wait_l{l}_{group}", scatter=False)
        land = lax.dynamic_update_slice_in_dim(land, local[l, group][None], my_chip, axis=0)
        return _unpack_weights(land, dict(GROUPS)[group])

    mods = []
    for l in range(DEPTH):
        row = jnp.concatenate([lax.dynamic_index_in_dim(mod_all[2 * j, l], my_dev, axis=0, keepdims=True)
                               for j in range(N_CHIPS)], axis=1) + pin
        mods.append([row[:, i * D_MODEL:(i + 1) * D_MODEL] for i in range(N_MOD)])

    inv_freq = ROPE_THETA ** (-jnp.arange(0, QK_ROPE, 2, dtype=F32) / QK_ROPE)
    freq_lanes = jnp.concatenate([jnp.zeros((QK_NOPE,), F32), inv_freq, inv_freq,
                                  jnp.zeros((HEAD_PAD - QK_DIM,), F32)]).reshape(1, LANES)
    cos_t, sin_t = _rope_tables(positions.reshape(seq, 1), freq_lanes, name="rope_tables")

    xs, saved, wl = x.reshape(seq, D_MODEL), [], []
    for l in range(DEPTH):
        xs, sv, w_l = _layer_fwd(l, xs, mods[l], functools.partial(gathered_weights, l), small[l], cos_t, sin_t)
        saved.append(sv)
        wl.append(w_l)
    dx, loss_part, g_final = _final_loss(xs, final_g.reshape(1, D_MODEL), loss_target.reshape(seq, D_MODEL),
                                         name="final_loss")

    sent = []

    def send_grads(l, group, g):
        gpack = _pack_grads(g, dict(GROUPS)[group])
        started_g = _exchange_start(gpack, name=f"grads_send_l{l}_{group}", scatter=True)
        sent.append((l, group, started_g, gpack))
        return started_g[4][0:1, 0:1]

    dmod, gsmall = [None] * DEPTH, [None] * DEPTH
    for l in reversed(range(DEPTH)):
        dx, dmod[l], gsmall[l] = _layer_bwd(l, dx, saved[l], mods[l], wl[l], small[l], cos_t, sin_t,
                                            functools.partial(send_grads, l))
    grads = dict(x=dx.reshape(1, seq, D_MODEL))

    def lanes(a):
        flat = a.reshape(1, -1)
        return jnp.pad(flat, ((0, 0), (0, D_MODEL - flat.shape[1])))

    singles = [gsmall[0]["ln1_g"], gsmall[1]["ln1_g"], gsmall[0]["ln2_g"], gsmall[1]["ln2_g"], g_final,
               lanes(jnp.concatenate([gsmall[l]["pool_scale"] for l in range(DEPTH)], axis=1)),
               lanes(jnp.concatenate([gsmall[l]["q_norm_g"] for l in range(DEPTH)], axis=1)),
               lanes(jnp.concatenate([gsmall[l]["kv_norm_g"] for l in range(DEPTH)], axis=1))]
    parts = [(dmod[0], 0), (lanes(loss_part), SMALL_LOSS), (dmod[1], 8)]
    parts += [(a, SMALL_SINGLES + i) for i, a in enumerate(singles)]
    parts += [(gsmall[l]["w_pool"].reshape(-1, D_MODEL), SMALL_POOL + l * SMALL_POOL_ROWS) for l in range(DEPTH)]
    small_all = _all_gather_small(_pack_small(parts, name="small_grads_pack"), name="small_grads_all_gather")
    small_all = small_all.reshape(N_DEV, SMALL_ROWS, D_MODEL)
    ssum = _sum_slots(small_all, N_DEV, name="small_grads_sum")
    loss = ssum[SMALL_LOSS, 0]
    grads["b_ada"] = jnp.stack([ssum[8 * l:8 * l + N_MOD] for l in range(DEPTH)]).reshape(DEPTH, N_MOD * D_MODEL)
    grads["ln1_g"] = ssum[SMALL_SINGLES:SMALL_SINGLES + 2]
    grads["ln2_g"] = ssum[SMALL_SINGLES + 2:SMALL_SINGLES + 4]
    grads["final_g"] = ssum[SMALL_SINGLES + 4]
    grads["pool_scale"] = ssum[SMALL_SINGLES + 5].reshape(DEPTH, POOL_DIM)
    grads["q_norm_g"] = ssum[SMALL_SINGLES + 6, :DEPTH * Q_LORA].reshape(DEPTH, Q_LORA)
    grads["kv_norm_g"] = ssum[SMALL_SINGLES + 7, :DEPTH * KV_LORA].reshape(DEPTH, KV_LORA)
    grads["w_pool"] = ssum[SMALL_POOL:SMALL_ROWS].reshape(w_pool.shape)

    gsum, after = {}, ssum
    for l, group, started_g, gpack in sent:
        tg = f"_l{l}_{group}"
        land = _exchange_wait(started_g, after, name="grads_wait" + tg, scatter=True)
        own = lax.dynamic_index_in_dim(gpack, my_chip, axis=0, keepdims=True)
        land = lax.dynamic_update_slice_in_dim(land, own, my_chip, axis=0)
        part = _sum_slots(land, N_CHIPS, name="grads_sum_chips" + tg)
        gsum[l, group] = _add2(part, _exchange_sibling(part, name="grads_swap_cores" + tg), name="grads_sum_cores" + tg)
        after = gsum[l, group]
    for group, names in GROUPS:
        off = 0
        for name in names:
            rows = ROWS_OF[name]
            grads[name] = jnp.stack([gsum[l, group][off:off + rows].reshape(weights[name].shape[1:])
                                     for l in range(DEPTH)])
            off += rows

    c_act_t = jnp.pad(c_act[::8].T, ((0, 0), (0, LANES - N_DEV)))
    g_ada = []
    for l in range(DEPTH):
        d_all = small_all[:, 8 * l:8 * l + N_MOD].reshape(N_DEV, N_MOD * D_MODEL)
        d_mine = lax.dynamic_slice_in_dim(d_all, my_chip * ada_cols, ada_cols, axis=1)
        g_ada.append(_mm(c_act_t, jnp.pad(d_mine, ((0, LANES - N_DEV), (0, 0))), name=f"ada_dw_l{l}")[0])
    grads["w_ada"] = jnp.stack(g_ada)

    def view(a):
        return a.reshape(1, -1) if a.ndim == 1 else a.reshape(-1, a.shape[-1])

    delta, new_m, new_v = {}, {}, {}
    for name in order:
        shape = weights[name].shape
        d, nm, nv = _adamw(view(weights[name]), view(grads[name]), view(moms[name]), view(vels[name]),
                           name="adamw_" + name)
        delta[name], new_m[name], new_v[name] = d.reshape(shape), nm.reshape(shape), nv.reshape(shape)
    return (loss, grads["x"], *[grads[n] for n in order], *[delta[n] for n in order],
            *[new_m[n] for n in order], *[new_v[n] for n in order])
```

```python
import functools
import math

import jax
import jax.numpy as jnp
from jax import lax
from jax.experimental import pallas as pl
from jax.experimental.pallas import tpu as pltpu

F32 = jnp.float32
BF16 = jnp.bfloat16
MESH = pl.DeviceIdType.MESH

D_MODEL = 1024
DEPTH = 2
POOL_WINDOWS = (2, 4, 8, 16)
POOL_GROUP = 128
POOL_DIM = 512
N_HEADS = 8
QK_NOPE = 64
QK_ROPE = 32
QK_DIM = QK_NOPE + QK_ROPE
V_DIM = 64
HEAD_PAD = 128
Q_LORA = 384
KV_LORA = 256
ROPE_THETA = 10000.0
ATTN_DIM = N_HEADS * V_DIM
D_FF = 4 * D_MODEL
N_MOD = 6
EPS = 1e-6
N_CHIPS = 4
N_DEV = 8

ADAM_LR = 0.001
ADAM_B1 = 0.9
ADAM_B2 = 0.999
ADAM_EPS = 1e-08
ADAM_WD = 0.01
ADAM_STEP = 10

VMEM_LIMIT_BYTES = 56 * 1024 * 1024
LANES = 128
HALO = 16

ZC_CQ = 0
ZC_KR = 384
ZC_U = 512
ZC_GA = 1024
ZC_GB = 2048
ZC_CKV = 3072
Z_DIM = 3328

PACK_ROWS = (("w_in", 808), ("w_uq", 72), ("w_uk", 32), ("w_uv", 32), ("p_pool", 128), ("p_attn", 128),
             ("w_out", 256), ("w_ff1", 1024), ("w_ff2", 1024))


def _params(sem=None, **kw):
    return pltpu.CompilerParams(dimension_semantics=sem, vmem_limit_bytes=VMEM_LIMIT_BYTES, **kw)


def _tile(n, target, unit=LANES):
    best = None
    for t in range(unit, min(n, target) + 1, unit):
        if n % t == 0:
            best = t
    return best if best is not None and 4 * best >= min(n, target) else n


def _near_tile(n, target):
    cands = [t for t in range(LANES, n + 1, LANES) if n % t == 0]
    return min(cands, key=lambda t: abs(math.log(t / target))) if cands else n


def _mm(a, b, *, name, ta=False, tb=False, out_dtypes=(F32,), epilogue=None, extras=(), tm=1024, tn=1024, tk=1024,
        second=None, b_stack=False, out_stack=None):
    (k_dim, m_dim) = a.shape if ta else a.shape[::-1]
    if b_stack:
        g_b, k_b, n_shard = b.shape
        n_dim, k_b = (k_b, g_b * n_shard) if tb else (g_b * n_shard, k_b)
    else:
        (n_dim, k_b) = b.shape if tb else b.shape[::-1]
    assert k_dim == k_b, (a.shape, b.shape)
    n_unit = n_shard if b_stack and not tb else n_dim // out_stack if out_stack else n_dim
    k_unit = n_shard if b_stack and tb else k_dim
    tm, tn, tk = _near_tile(m_dim, tm), _near_tile(n_unit, tn), _near_tile(k_unit, tk)
    nk = k_dim // tk
    n_extra, n_out = len(extras), len(out_dtypes)
    n_lhs = 4 if second else 2
    dims = (((0 if ta else 1,), (1 if tb else 0,)), ((), ()))
    if epilogue is None:
        epilogue = lambda acc: (acc,) * n_out

    def body(*refs):
        operand_refs, rest = refs[:n_lhs], refs[n_lhs:]
        extra_refs, out_refs = rest[:n_extra], rest[n_extra:n_extra + n_out]

        def product():
            total = None
            for a_ref, b_ref in zip(operand_refs[0::2], operand_refs[1::2]):
                part = lax.dot_general(a_ref[...].astype(BF16), b_ref[...].astype(BF16), dims, preferred_element_type=F32)
                total = part if total is None else total + part
            return total

        def finish(acc):
            outs = epilogue(acc, *[r[...] for r in extra_refs])
            for o_ref, o in zip(out_refs, outs):
                o_ref[...] = o.astype(o_ref.dtype)

        if nk == 1:
            finish(product())
            return
        acc_ref = rest[-1]
        k = pl.program_id(2)

        @pl.when(k == 0)
        def _():
            acc_ref[...] = product()

        @pl.when((k > 0) & (k < nk - 1))
        def _():
            acc_ref[...] += product()

        @pl.when(k == nk - 1)
        def _():
            finish(acc_ref[...] + product())

    a_spec = pl.BlockSpec((tk, tm), lambda i, j, k: (k, i)) if ta else pl.BlockSpec((tm, tk), lambda i, j, k: (i, k))
    if b_stack and tb:
        per = n_shard // tk
        b_spec = pl.BlockSpec((None, tn, tk), lambda i, j, k: (k // per, j, k % per))
    elif b_stack:
        per = n_shard // tn
        b_spec = pl.BlockSpec((None, tk, tn), lambda i, j, k: (j // per, k, j % per))
    elif tb:
        b_spec = pl.BlockSpec((tn, tk), lambda i, j, k: (j, k))
    else:
        b_spec = pl.BlockSpec((tk, tn), lambda i, j, k: (k, j))
    if out_stack:
        per_out = (n_dim // out_stack) // tn
        out_spec = pl.BlockSpec((None, tm, tn), lambda i, j, k: (j // per_out, i, j % per_out))
        out_dims = (out_stack, m_dim, n_dim // out_stack)
    else:
        out_spec = pl.BlockSpec((tm, tn), lambda i, j, k: (i, j))
        out_dims = (m_dim, n_dim)
    extra_specs = []
    for arr, kind in extras:
        if kind == "tile":
            extra_specs.append(pl.BlockSpec((tm, tn), lambda i, j, k: (i, j)))
        elif isinstance(kind, tuple):
            extra_specs.append(pl.BlockSpec((tm, tn), functools.partial(lambda i, j, k, c: (i, j + c), c=kind[1])))
        elif kind == "row":
            extra_specs.append(pl.BlockSpec((1, tn), lambda i, j, k: (0, j)))
        elif kind == "col":
            extra_specs.append(pl.BlockSpec((tm, 1), lambda i, j, k: (i, 0)))
        else:
            assert kind == "table", kind
            extra_specs.append(pl.BlockSpec((tm, LANES), lambda i, j, k: (i, 0)))
    return pl.pallas_call(
        body,
        name=name,
        grid=(m_dim // tm, n_dim // tn, nk),
        in_specs=[a_spec, b_spec] * (n_lhs // 2) + extra_specs,
        out_specs=[out_spec for _ in out_dtypes],
        out_shape=[jax.ShapeDtypeStruct(out_dims, dt) for dt in out_dtypes],
        scratch_shapes=[pltpu.VMEM((tm, tn), F32)] if nk > 1 else [],
        compiler_params=_params(("parallel", "parallel", "arbitrary")),
    )(a, b, *(second or ()), *[arr for arr, _ in extras])


def _rows(s):
    return min(512, s)


def _rope_tables(pos_col, inv_freq_lanes, *, name):
    s = pos_col.shape[0]
    tb = _rows(s)

    def body(pos_ref, f_ref, cos_ref, sin_ref):
        ang = pos_ref[...].astype(F32) * f_ref[...]
        lane = lax.broadcasted_iota(jnp.int32, ang.shape, 1)
        on = (lane >= QK_NOPE) & (lane < QK_DIM)
        cos_ref[...] = jnp.where(on, jnp.cos(ang), 0.0)
        sin_ref[...] = jnp.where(on, jnp.sin(ang), 0.0)

    return pl.pallas_call(
        body, name=name, grid=(s // tb,),
        in_specs=[pl.BlockSpec((tb, 1), lambda i: (i, 0)), pl.BlockSpec((1, LANES), lambda i: (0, 0))],
        out_specs=[pl.BlockSpec((tb, LANES), lambda i: (i, 0))] * 2,
        out_shape=[jax.ShapeDtypeStruct((s, LANES), F32)] * 2,
        compiler_params=_params(("parallel",)),
    )(pos_col, inv_freq_lanes)


def _rotate_half(x):
    lane = lax.broadcasted_iota(jnp.int32, x.shape, 1)
    half = QK_ROPE // 2
    first = (lane >= QK_NOPE) & (lane < QK_NOPE + half)
    second = (lane >= QK_NOPE + half) & (lane < QK_DIM)
    return jnp.where(first, -pltpu.roll(x, LANES - half, 1), jnp.where(second, pltpu.roll(x, half, 1), 0.0))


def _norm_mod(x, g, sc, sh, *, name):
    s, d = x.shape
    tb = _rows(s)

    def body(x_ref, g_ref, sc_ref, sh_ref, h_ref, r_ref):
        xv = x_ref[...]
        r = lax.rsqrt(jnp.mean(xv * xv, axis=-1, keepdims=True) + EPS)
        r_ref[...] = r
        h_ref[...] = (((xv * r) * g_ref[...]) * (1.0 + sc_ref[...]) + sh_ref[...]).astype(BF16)

    vec = pl.BlockSpec((1, d), lambda i: (0, 0))
    return pl.pallas_call(
        body, name=name, grid=(s // tb,),
        in_specs=[pl.BlockSpec((tb, d), lambda i: (i, 0)), vec, vec, vec],
        out_specs=[pl.BlockSpec((tb, d), lambda i: (i, 0)), pl.BlockSpec((tb, 1), lambda i: (i, 0))],
        out_shape=[jax.ShapeDtypeStruct((s, d), BF16), jax.ShapeDtypeStruct((s, 1), F32)],
        compiler_params=_params(("parallel",)),
    )(x, g, sc, sh)


def _window_sums(ext, sign):
    n = ext.shape[0]
    sums, cur, k = [], ext, 1
    for _ in POOL_WINDOWS:
        cur = cur + pltpu.roll(cur, k if sign > 0 else n - k, 0)
        sums.append(cur)
        k *= 2
    return sums


def _mixer_pre(z, cos_t, sin_t, w_pool, pool_scale, gq, gkv, *, name):
    s = z.shape[0]
    tb = _rows(s)
    hb = tb // HALO

    def body(zcq_ref, zkr_ref, zu_ref, zuh_ref, zckv_ref, cos_ref, sin_ref, wp_ref, ps_ref, gq_ref, gkv_ref,
             p_ref, yp_ref, cq_ref, ckv_ref, kr_ref, rq_ref, rkv_ref):
        i = pl.program_id(0)
        u = zu_ref[...]
        halo = jnp.where(i > 0, zuh_ref[...], 0.0)
        ext = jnp.concatenate([halo, u], axis=0)
        t = i * tb + lax.broadcasted_iota(jnp.int32, (tb, 1), 0)
        for g, (w, sw) in enumerate(zip(POOL_WINDOWS, _window_sums(ext, +1))):
            cols = slice(g * POOL_GROUP, (g + 1) * POOL_GROUP)
            cnt = jnp.minimum(t + 1, w).astype(F32)
            pg = (sw[HALO:, cols] / cnt - u[:, cols]).astype(BF16)
            p_ref[:, cols] = pg
            yg = jnp.dot(pg, wp_ref[g].astype(BF16), preferred_element_type=F32)
            yp_ref[:, cols] = (yg * ps_ref[:, cols]).astype(BF16)

        def rms(x_ref, g_ref, out_ref, r_ref):
            xv = x_ref[...]
            r = lax.rsqrt(jnp.mean(xv * xv, axis=-1, keepdims=True) + EPS)
            r_ref[...] = r
            out_ref[...] = ((xv * r) * g_ref[...]).astype(BF16)

        rms(zcq_ref, gq_ref, cq_ref, rq_ref)
        rms(zckv_ref, gkv_ref, ckv_ref, rkv_ref)
        kr = zkr_ref[...]
        kr_ref[...] = (kr * cos_ref[...] + _rotate_half(kr) * sin_ref[...]).astype(BF16)

    def zcol(width, off):
        return pl.BlockSpec((tb, width), lambda i: (i, off // width))

    def full(a):
        return pl.BlockSpec(a.shape, lambda i: (0,) * a.ndim)

    def out(width, dt):
        return pl.BlockSpec((tb, width), lambda i: (i, 0)), jax.ShapeDtypeStruct((s, width), dt)

    outs = [out(POOL_DIM, BF16), out(POOL_DIM, BF16), out(Q_LORA, BF16), out(KV_LORA, BF16), out(LANES, BF16),
            out(1, F32), out(1, F32)]
    return pl.pallas_call(
        body, name=name, grid=(s // tb,),
        in_specs=[zcol(Q_LORA, ZC_CQ), zcol(LANES, ZC_KR), zcol(POOL_DIM, ZC_U),
                  pl.BlockSpec((HALO, POOL_DIM), lambda i: (jnp.maximum(i * hb - 1, 0), ZC_U // POOL_DIM)),
                  zcol(KV_LORA, ZC_CKV),
                  pl.BlockSpec((tb, LANES), lambda i: (i, 0)), pl.BlockSpec((tb, LANES), lambda i: (i, 0)),
                  full(w_pool), full(pool_scale), full(gq), full(gkv)],
        out_specs=[o[0] for o in outs], out_shape=[o[1] for o in outs],
        compiler_params=_params(("parallel",)),
    )(z, z, z, z, z, cos_t, sin_t, w_pool, pool_scale, gq, gkv)


def _sigmoid(x):
    return 1.0 / (1.0 + jnp.exp(-x))


ATTN_SCALE = 1.0 / math.sqrt(QK_DIM)
NEG_BIG = -1e30


LOG2_E = math.log2(math.e)
EXP2_SCALE = ATTN_SCALE * LOG2_E
NT_DIMS = (((1,), (1,)), ((), ()))
TN_DIMS = (((0,), (0,)), ((), ()))


def _on_or_below_diagonal(t):
    return lax.broadcasted_iota(jnp.int32, (t, t), 0) >= lax.broadcasted_iota(jnp.int32, (t, t), 1)


HEADS_PER_STEP = 2
HEAD_COLS = [slice(g * HEAD_PAD, (g + 1) * HEAD_PAD) for g in range(HEADS_PER_STEP)]


def _attn_fwd(q, k, v, *, name):
    s = q.shape[0]
    t = _rows(s)
    wide = HEADS_PER_STEP * HEAD_PAD

    def body(q_ref, k_ref, v_ref, o_ref, lse_ref):
        qi = pl.program_id(1)
        qs = [q_ref[:, cols] for cols in HEAD_COLS]

        def block(j, carry, diagonal):
            rows = pl.ds(pl.multiple_of(j * t, t), t)
            out = []
            for qv, cols, (m, l, acc) in zip(qs, HEAD_COLS, carry):
                sc = lax.dot_general(qv, k_ref[rows, cols], NT_DIMS, preferred_element_type=F32)
                if diagonal:
                    sc = jnp.where(_on_or_below_diagonal(t), sc, NEG_BIG)
                m_new = jnp.maximum(m, jnp.max(sc, axis=-1, keepdims=True))
                p = jnp.exp2((sc - m_new) * EXP2_SCALE)
                alpha = jnp.exp2((m - m_new) * EXP2_SCALE)
                l = alpha * l + jnp.sum(p, axis=-1, keepdims=True)
                acc = alpha * acc + jnp.dot(p.astype(BF16), v_ref[rows, cols], preferred_element_type=F32)
                out.append((m_new, l, acc))
            return tuple(out)

        init = tuple((jnp.full((t, 1), -jnp.inf, F32), jnp.zeros((t, 1), F32), jnp.zeros((t, HEAD_PAD), F32))
                     for _ in HEAD_COLS)
        carry = lax.fori_loop(0, qi, lambda j, c: block(j, c, False), init)
        for g, (cols, (m, l, acc)) in enumerate(zip(HEAD_COLS, block(qi, carry, True))):
            o_ref[:, cols] = (acc / l).astype(BF16)
            lse_ref[g] = m * ATTN_SCALE + jnp.log(l)

    q_spec = pl.BlockSpec((t, wide), lambda h, i: (i, h))
    kv_spec = pl.BlockSpec((s, wide), lambda h, i: (0, h))
    return pl.pallas_call(
        body, name=name, grid=(N_HEADS // HEADS_PER_STEP, s // t),
        in_specs=[q_spec, kv_spec, kv_spec],
        out_specs=[q_spec, pl.BlockSpec((HEADS_PER_STEP, t, 1), lambda h, i: (h, i, 0))],
        out_shape=[jax.ShapeDtypeStruct((s, N_HEADS * HEAD_PAD), BF16), jax.ShapeDtypeStruct((N_HEADS, s, 1), F32)],
        compiler_params=_params(("parallel", "parallel")),
    )(q, k, v)


def _attn_delta(do, o, *, name):
    s = o.shape[0]
    t = _rows(s)

    def body(do_ref, o_ref, out_ref):
        for h in range(N_HEADS):
            cols = slice(h * HEAD_PAD, (h + 1) * HEAD_PAD)
            out_ref[h] = jnp.sum(do_ref[:, cols].astype(F32) * o_ref[:, cols].astype(F32), axis=-1, keepdims=True)

    blk = pl.BlockSpec((t, N_HEADS * HEAD_PAD), lambda i: (i, 0))
    return pl.pallas_call(
        body, name=name, grid=(s // t,), in_specs=[blk, blk],
        out_specs=pl.BlockSpec((N_HEADS, t, 1), lambda i: (0, i, 0)),
        out_shape=jax.ShapeDtypeStruct((N_HEADS, s, 1), F32),
        compiler_params=_params(("parallel",)),
    )(do, o)


def _attn_bwd(q, k, v, do, lse, delta, *, name):
    s = q.shape[0]
    t = _rows(s)
    nt = s // t

    def body(q_ref, k_ref, v_ref, do_ref, lse_ref, dl_ref, dq_ref, dk_ref, dv_ref):
        kj = pl.program_id(1)

        @pl.when(kj == 0)
        def _():
            dq_ref[...] = jnp.zeros_like(dq_ref)

        kvs = [(k_ref[:, cols], v_ref[:, cols]) for cols in HEAD_COLS]

        def block(i, carry, diagonal):
            rows = pl.ds(pl.multiple_of(i * t, t), t)
            out = []
            for g, (cols, (kv, vv), (dk, dv)) in enumerate(zip(HEAD_COLS, kvs, carry)):
                qv, dov = q_ref[rows, cols], do_ref[rows, cols]
                sc = lax.dot_general(qv, kv, NT_DIMS, preferred_element_type=F32)
                p = jnp.exp2(sc * EXP2_SCALE - lse_ref[g, rows, :] * LOG2_E)
                if diagonal:
                    p = jnp.where(_on_or_below_diagonal(t), p, 0.0)
                dp = lax.dot_general(dov, vv, NT_DIMS, preferred_element_type=F32)
                ds = (p * (dp - dl_ref[g, rows, :])).astype(BF16)
                dv = dv + lax.dot_general(p.astype(BF16), dov, TN_DIMS, preferred_element_type=F32)
                dk = dk + lax.dot_general(ds, qv, TN_DIMS, preferred_element_type=F32)
                dq_ref[rows, cols] += jnp.dot(ds, kv, preferred_element_type=F32) * ATTN_SCALE
                out.append((dk, dv))
            return tuple(out)

        zero = jnp.zeros((t, HEAD_PAD), F32)
        carry = block(kj, tuple((zero, zero) for _ in HEAD_COLS), True)
        for cols, (dk, dv) in zip(HEAD_COLS, lax.fori_loop(kj + 1, nt, lambda i, c: block(i, c, False), carry)):
            dk_ref[:, cols] = dk * ATTN_SCALE
            dv_ref[:, cols] = dv.astype(BF16)

    full_spec = pl.BlockSpec((s, HEADS_PER_STEP * HEAD_PAD), lambda h, j: (0, h))
    kv_spec = pl.BlockSpec((t, HEADS_PER_STEP * HEAD_PAD), lambda h, j: (j, h))
    vec_spec = pl.BlockSpec((HEADS_PER_STEP, s, 1), lambda h, j: (h, 0, 0))
    wide = (s, N_HEADS * HEAD_PAD)
    return pl.pallas_call(
        body, name=name, grid=(N_HEADS // HEADS_PER_STEP, nt),
        in_specs=[full_spec, kv_spec, kv_spec, full_spec, vec_spec, vec_spec],
        out_specs=[full_spec, kv_spec, kv_spec],
        out_shape=[jax.ShapeDtypeStruct(wide, F32), jax.ShapeDtypeStruct(wide, F32), jax.ShapeDtypeStruct(wide, BF16)],
        compiler_params=_params(("parallel", "arbitrary")),
    )(q, k, v, do, lse, delta)


def _acc_specs(widths):
    return ([pl.BlockSpec((1, w), lambda i: (0, 0)) for w in widths],
            [jax.ShapeDtypeStruct((1, w), F32) for w in widths])


def _final_loss(x, g, target, *, name):
    s, d = x.shape
    tb = _rows(s)

    def body(x_ref, g_ref, t_ref, dx_ref, loss_ref, dg_ref):
        @pl.when(pl.program_id(0) == 0)
        def _():
            loss_ref[...] = jnp.zeros_like(loss_ref)
            dg_ref[...] = jnp.zeros_like(dg_ref)

        xv = x_ref[...]
        r = lax.rsqrt(jnp.mean(xv * xv, axis=-1, keepdims=True) + EPS)
        xn = xv * r
        err = xn * g_ref[...] - t_ref[...]
        loss_ref[...] += 0.5 * jnp.sum(jnp.mean(err * err, axis=-1, keepdims=True), axis=0, keepdims=True)
        dy = err / d
        dg_ref[...] += jnp.sum(dy * xn, axis=0, keepdims=True)
        dxn = dy * g_ref[...]
        dx_ref[...] = r * (dxn - xn * jnp.mean(dxn * xn, axis=-1, keepdims=True))

    blk = pl.BlockSpec((tb, d), lambda i: (i, 0))
    acc_specs, acc_shapes = _acc_specs((LANES, d))
    return pl.pallas_call(
        body, name=name, grid=(s // tb,),
        in_specs=[blk, pl.BlockSpec((1, d), lambda i: (0, 0)), blk],
        out_specs=[blk] + acc_specs, out_shape=[jax.ShapeDtypeStruct((s, d), F32)] + acc_shapes,
        compiler_params=_params(("arbitrary",)),
    )(x, g, target)


def _gate_bwd(dx, m, g, *, name):
    s, d = dx.shape
    tb = _rows(s)

    def body(dx_ref, m_ref, g_ref, dm_ref, dg_ref):
        @pl.when(pl.program_id(0) == 0)
        def _():
            dg_ref[...] = jnp.zeros_like(dg_ref)

        dxv = dx_ref[...]
        dm_ref[...] = (dxv * g_ref[...]).astype(BF16)
        dg_ref[...] += jnp.sum(dxv * m_ref[...], axis=0, keepdims=True)

    blk = pl.BlockSpec((tb, d), lambda i: (i, 0))
    acc_specs, acc_shapes = _acc_specs((d,))
    return pl.pallas_call(
        body, name=name, grid=(s // tb,),
        in_specs=[blk, blk, pl.BlockSpec((1, d), lambda i: (0, 0))],
        out_specs=[blk] + acc_specs, out_shape=[jax.ShapeDtypeStruct((s, d), BF16)] + acc_shapes,
        compiler_params=_params(("arbitrary",)),
    )(dx, m, g)


def _norm_mod_bwd(dh, x, r, g, sc, dx_skip, *, name):
    s, d = x.shape
    tb = _rows(s)
    nb = s // tb

    def body(dh_ref, x_ref, r_ref, g_ref, sc_ref, skip_ref, dx_ref, dg_ref, dsc_ref, dsh_ref, da_sc):
        i = pl.program_id(0)

        @pl.when(i == 0)
        def _():
            da_sc[...] = jnp.zeros_like(da_sc)
            dsh_ref[...] = jnp.zeros_like(dsh_ref)

        dhv, rv = dh_ref[...], r_ref[...]
        xn = x_ref[...] * rv
        dsh_ref[...] += jnp.sum(dhv, axis=0, keepdims=True)
        da_sc[...] += jnp.sum(dhv * xn, axis=0, keepdims=True)
        dxn = dhv * (g_ref[...] * (1.0 + sc_ref[...]))
        dx_ref[...] = skip_ref[...] + rv * (dxn - xn * jnp.mean(dxn * xn, axis=-1, keepdims=True))

        @pl.when(i == nb - 1)
        def _():
            dg_ref[...] = da_sc[...] * (1.0 + sc_ref[...])
            dsc_ref[...] = da_sc[...] * g_ref[...]

    blk = pl.BlockSpec((tb, d), lambda i: (i, 0))
    vec = pl.BlockSpec((1, d), lambda i: (0, 0))
    acc_specs, acc_shapes = _acc_specs((d, d, d))
    return pl.pallas_call(
        body, name=name, grid=(nb,),
        in_specs=[blk, blk, pl.BlockSpec((tb, 1), lambda i: (i, 0)), vec, vec, blk],
        out_specs=[blk] + acc_specs, out_shape=[jax.ShapeDtypeStruct((s, d), F32)] + acc_shapes,
        scratch_shapes=[pltpu.VMEM((1, d), F32)],
        compiler_params=_params(("arbitrary",)),
    )(dh, x, r, g, sc, dx_skip)


def _merge_bwd(dmerged, z, ya, yb, *, name):
    s, d = ya.shape
    tb = _rows(s)

    def body(dm_ref, ga_ref, gb_ref, ya_ref, yb_ref, dya_ref, dyb_ref, dga_ref, dgb_ref):
        dm = dm_ref[...]
        for g_ref, y_ref, dy_ref, dg_ref in ((ga_ref, ya_ref, dya_ref, dga_ref), (gb_ref, yb_ref, dyb_ref, dgb_ref)):
            sg = _sigmoid(g_ref[...])
            dy_ref[...] = (dm * sg).astype(BF16)
            dg_ref[...] = (dm * y_ref[...] * (sg * (1.0 - sg))).astype(BF16)

    blk = pl.BlockSpec((tb, d), lambda i: (i, 0))
    return pl.pallas_call(
        body, name=name, grid=(s // tb,),
        in_specs=[blk, pl.BlockSpec((tb, d), lambda i: (i, ZC_GA // d)), pl.BlockSpec((tb, d), lambda i: (i, ZC_GB // d)),
                  blk, blk],
        out_specs=[blk] * 4, out_shape=[jax.ShapeDtypeStruct((s, d), BF16)] * 4,
        compiler_params=_params(("parallel",)),
    )(dmerged, z, z, ya, yb)


def _pool_bwd(dyp, p, w_pool, pool_scale, *, name):
    s = dyp.shape[0]
    tb = _rows(s)
    nb = s // tb
    hb = tb // HALO
    nt_dims = (((1,), (1,)), ((), ()))
    tn_dims = (((0,), (0,)), ((), ()))

    def body(dy_ref, dyn_ref, p_ref, wp_ref, ps_ref, du_ref, gwp_ref, gps_ref):
        i = pl.program_id(0)

        @pl.when(i == 0)
        def _():
            gwp_ref[...] = jnp.zeros_like(gwp_ref)
            gps_ref[...] = jnp.zeros_like(gps_ref)

        cur = dy_ref[...]
        nxt = jnp.where(i < nb - 1, dyn_ref[...], 0.0)
        dpw = (jnp.concatenate([cur, nxt], axis=0) * ps_ref[...]).astype(BF16)
        t = i * tb + lax.broadcasted_iota(jnp.int32, (tb + HALO, 1), 0)
        for g, w in enumerate(POOL_WINDOWS):
            cols = slice(g * POOL_GROUP, (g + 1) * POOL_GROUP)
            wg = wp_ref[g].astype(BF16)
            dp = lax.dot_general(dpw[:, cols], wg, nt_dims, preferred_element_type=F32)
            e = dp / jnp.minimum(t + 1, w).astype(F32)
            lead = _window_sums(e, -1)[g]
            du_ref[:, cols] = (lead[:tb] - dp[:tb]).astype(BF16)
            pg = p_ref[:, cols]
            pw = jnp.dot(pg, wg, preferred_element_type=F32)
            gps_ref[:, cols] += jnp.sum(cur[:, cols] * pw, axis=0, keepdims=True)
            gwp_ref[g] += lax.dot_general(pg, dpw[:tb, cols], tn_dims, preferred_element_type=F32)

    blk = pl.BlockSpec((tb, POOL_DIM), lambda i: (i, 0))
    return pl.pallas_call(
        body, name=name, grid=(nb,),
        in_specs=[blk, pl.BlockSpec((HALO, POOL_DIM), lambda i: (jnp.minimum((i + 1) * hb, s // HALO - 1), 0)), blk,
                  pl.BlockSpec(w_pool.shape, lambda i: (0, 0, 0)), pl.BlockSpec((1, POOL_DIM), lambda i: (0, 0))],
        out_specs=[blk, pl.BlockSpec(w_pool.shape, lambda i: (0, 0, 0)), pl.BlockSpec((1, POOL_DIM), lambda i: (0, 0))],
        out_shape=[jax.ShapeDtypeStruct((s, POOL_DIM), BF16), jax.ShapeDtypeStruct(w_pool.shape, F32),
                   jax.ShapeDtypeStruct((1, POOL_DIM), F32)],
        compiler_params=_params(("arbitrary",)),
    )(dyp, dyp, p, w_pool, pool_scale)


def _rope_bwd_q(dq, cos_t, sin_t, *, name):
    s = dq.shape[0]
    tb = _rows(s)

    def body(dq_ref, cos_ref, sin_ref, out_ref):
        sin = sin_ref[...]
        lane = lax.broadcasted_iota(jnp.int32, sin.shape, 1)
        cos_q = cos_ref[...] + jnp.where(lane < QK_NOPE, 1.0, 0.0)
        for h in range(N_HEADS):
            cols = slice(h * HEAD_PAD, (h + 1) * HEAD_PAD)
            dqv = dq_ref[:, cols]
            out_ref[:, cols] = (dqv * cos_q - _rotate_half(dqv * sin)).astype(BF16)

    blk = pl.BlockSpec((tb, N_HEADS * HEAD_PAD), lambda i: (i, 0))
    tab = pl.BlockSpec((tb, LANES), lambda i: (i, 0))
    return pl.pallas_call(
        body, name=name, grid=(s // tb,), in_specs=[blk, tab, tab], out_specs=blk,
        out_shape=jax.ShapeDtypeStruct(dq.shape, BF16),
        compiler_params=_params(("parallel",)),
    )(dq, cos_t, sin_t)


def _key_bwd(dk, cos_t, sin_t, *, name):
    s = dk.shape[0]
    tb = _rows(s)

    def body(dk_ref, cos_ref, sin_ref, dkb_ref, dkr_ref):
        dkv = dk_ref[...]
        dkb_ref[...] = dkv.astype(BF16)
        tot = dkv[:, :HEAD_PAD]
        for h in range(1, N_HEADS):
            tot = tot + dkv[:, h * HEAD_PAD:(h + 1) * HEAD_PAD]
        dkr_ref[...] = (tot * cos_ref[...] - _rotate_half(tot * sin_ref[...])).astype(BF16)

    blk = pl.BlockSpec((tb, N_HEADS * HEAD_PAD), lambda i: (i, 0))
    tab = pl.BlockSpec((tb, LANES), lambda i: (i, 0))
    return pl.pallas_call(
        body, name=name, grid=(s // tb,), in_specs=[blk, tab, tab], out_specs=[blk, tab],
        out_shape=[jax.ShapeDtypeStruct(dk.shape, BF16), jax.ShapeDtypeStruct((s, LANES), BF16)],
        compiler_params=_params(("parallel",)),
    )(dk, cos_t, sin_t)


def _rms_bwd(dy, z, z_off, r, g, *, name):
    s, n = dy.shape
    tb = _rows(s)

    def body(dy_ref, x_ref, r_ref, g_ref, dx_ref, dg_ref):
        @pl.when(pl.program_id(0) == 0)
        def _():
            dg_ref[...] = jnp.zeros_like(dg_ref)

        dyv, rv = dy_ref[...], r_ref[...]
        xn = x_ref[...] * rv
        dg_ref[...] += jnp.sum(dyv * xn, axis=0, keepdims=True)
        dxn = dyv * g_ref[...]
        dx_ref[...] = (rv * (dxn - xn * jnp.mean(dxn * xn, axis=-1, keepdims=True))).astype(BF16)

    blk = pl.BlockSpec((tb, n), lambda i: (i, 0))
    acc_specs, acc_shapes = _acc_specs((n,))
    return pl.pallas_call(
        body, name=name, grid=(s // tb,),
        in_specs=[blk, pl.BlockSpec((tb, n), lambda i: (i, z_off // n)), pl.BlockSpec((tb, 1), lambda i: (i, 0)),
                  pl.BlockSpec((1, n), lambda i: (0, 0))],
        out_specs=[blk] + acc_specs, out_shape=[jax.ShapeDtypeStruct((s, n), BF16)] + acc_shapes,
        compiler_params=_params(("arbitrary",)),
    )(dy, z, r, g)


def _silu(c, *, name):
    def body(c_ref, out_ref):
        cv = c_ref[...]
        out_ref[...] = (cv * _sigmoid(cv)).astype(BF16)

    return pl.pallas_call(body, name=name, out_shape=jax.ShapeDtypeStruct(c.shape, BF16),
                          compiler_params=_params())(c)


def _sum_slots(a, n, *, name, out_dtype=F32):
    _, rows, cols = a.shape
    tr = _tile(rows, 256, 8)

    def body(a_ref, out_ref):
        tot = a_ref[0].astype(F32)
        for j in range(1, n):
            tot = tot + a_ref[j].astype(F32)
        out_ref[...] = tot.astype(out_dtype)

    return pl.pallas_call(
        body, name=name, grid=(rows // tr,),
        in_specs=[pl.BlockSpec((n, tr, cols), lambda i: (0, i, 0))],
        out_specs=pl.BlockSpec((tr, cols), lambda i: (i, 0)),
        out_shape=jax.ShapeDtypeStruct((rows, cols), out_dtype),
        compiler_params=_params(("parallel",)),
    )(a)


def _add2_stacked(a, b, stacked, l, *, name):
    rows, cols = a.shape
    tr = _tile(rows, 256, 8)

    def body(a_ref, b_ref, *rest):
        rest[-1][...] = a_ref[...] + b_ref[...]

    blk = pl.BlockSpec((tr, cols), lambda i: (i, 0))
    carried = [] if stacked is None else [stacked]
    return pl.pallas_call(
        body, name=name, grid=(rows // tr,),
        in_specs=[blk, blk] + [pl.BlockSpec(memory_space=pl.ANY) for _ in carried],
        out_specs=pl.BlockSpec((None, tr, cols), lambda i: (l, i, 0)),
        out_shape=jax.ShapeDtypeStruct((DEPTH, rows, cols), F32),
        input_output_aliases={2: 0} if carried else {},
        compiler_params=_params(("parallel",)),
    )(a, b, *carried)


def _adamw(w, g, m, v, *, name):
    rows, cols = w.shape
    tr = _tile(rows, max(8, (1 << 18) // cols), 8)
    c1 = 1.0 - ADAM_B1 ** ADAM_STEP
    c2 = 1.0 - ADAM_B2 ** ADAM_STEP

    def body(w_ref, g_ref, m_ref, v_ref, d_ref, nm_ref, nv_ref):
        gv = g_ref[...]
        nm = ADAM_B1 * m_ref[...] + (1.0 - ADAM_B1) * gv
        nv = ADAM_B2 * v_ref[...] + (1.0 - ADAM_B2) * (gv * gv)
        nm_ref[...] = nm
        nv_ref[...] = nv
        d_ref[...] = -ADAM_LR * ((nm / c1) / (jnp.sqrt(nv / c2) + ADAM_EPS) + ADAM_WD * w_ref[...])

    blk = pl.BlockSpec((tr, cols), lambda i: (i, 0))
    return pl.pallas_call(
        body, name=name, grid=(rows // tr,), in_specs=[blk] * 4, out_specs=[blk] * 3,
        out_shape=[jax.ShapeDtypeStruct((rows, cols), F32)] * 3,
        compiler_params=_params(("parallel",)),
    )(w, g, m, v)


def _coords():
    return lax.axis_index("x"), lax.axis_index("y"), lax.axis_index("c")


def _other_chips(x, y):
    return [(1 - x, y), (x, 1 - y), (1 - x, 1 - y)]


def _all_gather_small(blk, *, name):
    m_per, n = blk.shape

    def body(x_ref, out_ref, send_sems, recv_sems, local_sem):
        x, y, c = _coords()
        me, sibling = (x, y, c), (x, y, 1 - c)
        chips = _other_chips(x, y)

        def rows(px, py, pc):
            return out_ref.at[pl.ds((4 * px + 2 * py + pc) * m_per, m_per), :]

        def copy(k, block, to, src=None):
            return pltpu.make_async_remote_copy(
                src_ref=rows(*block) if src is None else src, dst_ref=rows(*block),
                send_sem=send_sems.at[k], recv_sem=recv_sems.at[k], device_id=to, device_id_type=MESH)

        mine = pltpu.make_async_copy(x_ref, rows(*me), local_sem)
        mine.start()
        first = [copy(0, me, sibling, src=x_ref)]
        first += [copy(1 + j, me, (*chip, c), src=x_ref) for j, chip in enumerate(chips)]
        for cp in first:
            cp.start()
        passed = [copy(4 + j, (*chip, c), sibling) for j, chip in enumerate(chips)]
        for j, chip in enumerate(chips):
            copy(1 + j, (*chip, c), me).wait_recv()
            passed[j].start()
        copy(0, sibling, me).wait_recv()
        for j, chip in enumerate(chips):
            copy(4 + j, (*chip, 1 - c), me).wait_recv()
        for cp in first + passed:
            cp.wait_send()
        mine.wait()

    return pl.pallas_call(
        body, name=name,
        out_shape=jax.ShapeDtypeStruct((N_DEV * m_per, n), blk.dtype),
        in_specs=[pl.BlockSpec(memory_space=pltpu.VMEM)],
        out_specs=pl.BlockSpec(memory_space=pltpu.VMEM),
        scratch_shapes=[pltpu.SemaphoreType.DMA((7,)), pltpu.SemaphoreType.DMA((7,)), pltpu.SemaphoreType.DMA],
        compiler_params=_params(),
    )(blk)


HBM_SPEC = pl.BlockSpec(memory_space=pltpu.HBM)
SEM_SPEC = pl.BlockSpec(memory_space=pltpu.SEMAPHORE)
DATAFLOW = pltpu.SideEffectType.DATAFLOW_SIDE_EFFECTING


def _chip_copies(src_ref, land_ref, send_sems, recv_sems, scatter):
    x, y, c = _coords()
    my = 2 * x + y
    outgoing, incoming = [], []
    for k, (px, py) in enumerate(_other_chips(x, y)):
        peer = 2 * px + py

        def copy(src_slot, dst_slot):
            return pltpu.make_async_remote_copy(
                src_ref=src_ref.at[src_slot] if scatter else src_ref, dst_ref=land_ref.at[dst_slot],
                send_sem=send_sems.at[k], recv_sem=recv_sems.at[k], device_id=(px, py, c), device_id_type=MESH)

        outgoing.append(copy(peer, my))
        incoming.append(copy(my, peer))
    return outgoing, incoming


def _exchange_start(src, *, name, scatter):
    land_shape = src.shape if scatter else (N_CHIPS,) + src.shape

    def body(src_ref, land_ref, send_sems, recv_sems, src_thru, land_thru, token):
        outgoing, _ = _chip_copies(src_ref, land_ref, send_sems, recv_sems, scatter)
        for cp in outgoing:
            cp.start()
        token[...] = jnp.zeros_like(token)

    return pl.pallas_call(
        body, name=name,
        out_shape=(pltpu.SemaphoreType.DMA((N_CHIPS - 1,)), pltpu.SemaphoreType.DMA((N_CHIPS - 1,)),
                   pltpu.HBM(src.shape, src.dtype), pltpu.HBM(land_shape, src.dtype), jax.ShapeDtypeStruct((8, LANES), F32)),
        in_specs=(HBM_SPEC, HBM_SPEC),
        out_specs=(SEM_SPEC, SEM_SPEC, HBM_SPEC, HBM_SPEC, pl.BlockSpec(memory_space=pltpu.VMEM)),
        input_output_aliases={0: 2, 1: 3},
        compiler_params=pltpu.CompilerParams(has_side_effects=DATAFLOW),
    )(pltpu.with_memory_space_constraint(src, pltpu.HBM),
      pltpu.with_memory_space_constraint(lax.empty(land_shape, src.dtype), pltpu.HBM))


def _exchange_wait(started, after, *, name, scatter):
    send_sems, recv_sems, src_thru, land_thru, _ = started

    def body(src_ref, land_ref, send_sems, recv_sems, after_ref, src_dead, got_ref):
        outgoing, incoming = _chip_copies(src_ref, land_ref, send_sems, recv_sems, scatter)
        for cp in outgoing:
            cp.wait_send()
        for cp in incoming:
            cp.wait_recv()

    return pl.pallas_call(
        body, name=name,
        out_shape=(pltpu.HBM(src_thru.shape, src_thru.dtype), pltpu.HBM(land_thru.shape, land_thru.dtype)),
        in_specs=(HBM_SPEC, HBM_SPEC, SEM_SPEC, SEM_SPEC, pl.BlockSpec(memory_space=pl.ANY)),
        out_specs=(HBM_SPEC, HBM_SPEC),
        input_output_aliases={0: 0, 1: 1},
        compiler_params=pltpu.CompilerParams(has_side_effects=DATAFLOW),
    )(src_thru, land_thru, send_sems, recv_sems, after)[1]


def _exchange_sibling(src, *, name):
    def body(src_ref, out_ref, send_sem, recv_sem):
        x, y, c = _coords()
        cp = pltpu.make_async_remote_copy(src_ref=src_ref, dst_ref=out_ref, send_sem=send_sem, recv_sem=recv_sem,
                                          device_id=(x, y, 1 - c), device_id_type=MESH)
        cp.start()
        cp.wait()

    return pl.pallas_call(
        body, name=name,
        out_shape=jax.ShapeDtypeStruct(src.shape, src.dtype),
        in_specs=[pl.BlockSpec(memory_space=pl.ANY)],
        out_specs=pl.BlockSpec(memory_space=pl.ANY),
        scratch_shapes=[pltpu.SemaphoreType.DMA, pltpu.SemaphoreType.DMA],
        compiler_params=_params(),
    )(src)


def _pack_rows(a):
    return a.reshape(-1, D_MODEL)


def _pad_heads(w, width):
    r = w.shape[0]
    return jnp.pad(w, ((0, 0), (0, 0), (0, HEAD_PAD - width))).reshape(r, N_HEADS * HEAD_PAD)


MIX_NAMES = ("w_uq", "w_uk", "w_uv", "p_pool", "p_attn", "w_out")
GROUPS = ("in", "mix", "ff1", "ff2")
ROWS_OF = dict(PACK_ROWS)
W_IN_COLS = 3232
W_IN_SHARD = W_IN_COLS // N_CHIPS


def _local_shard(weights, l, group, zero):
    if group == "mix":
        shard = jnp.concatenate([_pack_rows(weights[n][l]) for n in MIX_NAMES], axis=0)
    else:
        shard = weights[{"in": "w_in", "ff1": "w_ff1", "ff2": "w_ff2"}[group]][l]
    return (shard + zero).astype(BF16)


def _unpack_weights(gathered, group):
    def cols(a, k):
        return a.reshape(N_CHIPS, k, -1).transpose(1, 0, 2).reshape(k, -1)

    if group == "in":
        full = gathered.transpose(1, 0, 2).reshape(D_MODEL, W_IN_COLS)
        kr = jnp.pad(full[:, 1152:1184], ((0, 0), (QK_NOPE, HEAD_PAD - QK_DIM)))
        return dict(w_in=jnp.concatenate([full[:, 512:896], kr, full[:, 0:512], full[:, 1184:3232], full[:, 896:1152]],
                                         axis=1))
    if group == "ff1":
        return dict(w_ff1=gathered)
    if group == "ff2":
        return dict(w_ff2=gathered.reshape(D_FF, D_MODEL))

    def p_attn(a):
        full = cols(a, ATTN_DIM).reshape(N_HEADS, V_DIM, D_MODEL)
        return jnp.pad(full, ((0, 0), (0, HEAD_PAD - V_DIM), (0, 0))).reshape(N_HEADS * HEAD_PAD, D_MODEL)

    build = dict(
        w_uq=lambda a: _pad_heads(a.reshape(Q_LORA, N_HEADS, QK_DIM), QK_DIM),
        w_uk=lambda a: _pad_heads(a.reshape(KV_LORA, N_HEADS, QK_NOPE), QK_NOPE),
        w_uv=lambda a: _pad_heads(a.reshape(KV_LORA, N_HEADS, V_DIM), V_DIM),
        p_pool=lambda a: cols(a, POOL_DIM),
        p_attn=p_attn,
        w_out=lambda a: a.reshape(D_MODEL, D_MODEL),
    )
    w, off = {}, 0
    for name in MIX_NAMES:
        w[name] = build[name](gathered[:, off:off + ROWS_OF[name]])
        off += ROWS_OF[name]
    return w


def _pack_grads(g, group):
    def cols(a):
        k = a.shape[0]
        return a.reshape(k, N_CHIPS, -1).transpose(1, 0, 2).reshape(N_CHIPS, -1, D_MODEL)

    def rows(a):
        return a.reshape(N_CHIPS, -1, D_MODEL)

    def heads(width):
        return lambda a: rows(a.reshape(a.shape[0], N_HEADS, HEAD_PAD)[:, :, :width])

    if group == "in":
        gi = g["w_in"]
        full = jnp.concatenate([gi[:, ZC_U:ZC_U + 512], gi[:, ZC_CQ:ZC_CQ + 384], gi[:, ZC_CKV:ZC_CKV + 256],
                                gi[:, ZC_KR + QK_NOPE:ZC_KR + QK_DIM], gi[:, ZC_GA:ZC_GA + 2048]], axis=1)
        return full.reshape(D_MODEL, N_CHIPS, W_IN_SHARD).transpose(1, 0, 2)
    if group == "ff1":
        return g["w_ff1"]
    if group == "ff2":
        return g["w_ff2"].reshape(N_CHIPS, D_FF // N_CHIPS, D_MODEL)

    def p_attn(a):
        return cols(a.reshape(N_HEADS, HEAD_PAD, D_MODEL)[:, :V_DIM].reshape(ATTN_DIM, D_MODEL))

    build = dict(w_uq=heads(QK_DIM), w_uk=heads(QK_NOPE), w_uv=heads(V_DIM), p_pool=cols, p_attn=p_attn, w_out=rows)
    return jnp.concatenate([build[name](g[name]) for name in MIX_NAMES], axis=1)


def _per_head(fn, acc, *tables):
    return jnp.concatenate([fn(acc[:, h * HEAD_PAD:(h + 1) * HEAD_PAD], *tables) for h in range(N_HEADS)], axis=1)


def _rope_head(a, cos, sin):
    lane = lax.broadcasted_iota(jnp.int32, a.shape, 1)
    return a * (cos + jnp.where(lane < QK_NOPE, 1.0, 0.0)) + _rotate_half(a) * sin


def _layer_fwd(l, x, mod, get_weights, small, cos_t, sin_t):
    sh1, sc1, g1, sh2, sc2, g2 = mod
    tag = f"_l{l}"
    h, r1 = _norm_mod(x, small["ln1_g"], sc1, sh1, name="norm1" + tag)
    w = dict(get_weights("in", h))
    (z,) = _mm(h, w["w_in"], name="in_proj" + tag)
    p, yp, cq, ckv, kr, rq, rkv = _mixer_pre(z, cos_t, sin_t, small["w_pool"], small["pool_scale"],
                                              small["q_norm_g"], small["kv_norm_g"], name="mixer_pre" + tag)
    w.update(get_weights("mix", yp))
    (ya,) = _mm(yp, w["p_pool"], name="pool_out" + tag, out_dtypes=(BF16,))
    (q,) = _mm(cq, w["w_uq"], name="q_proj" + tag, out_dtypes=(BF16,),
               epilogue=lambda acc, cos, sin: (_per_head(_rope_head, acc, cos, sin),),
               extras=((cos_t, "table"), (sin_t, "table")))
    (k,) = _mm(ckv, w["w_uk"], name="k_proj" + tag, out_dtypes=(BF16,),
               epilogue=lambda acc, krv: (_per_head(lambda a, b: a + b, acc, krv),), extras=((kr, "table"),))
    (v,) = _mm(ckv, w["w_uv"], name="v_proj" + tag, out_dtypes=(BF16,))
    o, lse = _attn_fwd(q, k, v, name="attn_fwd" + tag)
    yb, merged = _mm(o, w["p_attn"], name="attn_out" + tag, out_dtypes=(BF16, BF16), tm=512,
                     epilogue=lambda acc, ga, gb, yav: (acc, _sigmoid(ga) * yav + _sigmoid(gb) * acc),
                     extras=((z, ("tile", ZC_GA // D_MODEL)), (z, ("tile", ZC_GB // D_MODEL)), (ya, "tile")))
    mo, x1 = _mm(merged, w["w_out"], name="mix_out" + tag, out_dtypes=(BF16, F32),
                 epilogue=lambda acc, xr, g: (acc, xr + g * acc), extras=((x, "tile"), (g1, "row")), tm=512)
    h2, r2 = _norm_mod(x1, small["ln2_g"], sc2, sh2, name="norm2" + tag)
    w.update(get_weights("ff1", merged))
    f, act = _mm(h2, w["w_ff1"], b_stack=True, name="ff1" + tag, out_dtypes=(BF16, BF16),
                 epilogue=lambda acc: (acc, jnp.square(jnp.maximum(acc, 0.0))))
    w.update(get_weights("ff2", act))
    m2, x2 = _mm(act, w["w_ff2"], name="ff2" + tag, out_dtypes=(BF16, F32),
                 epilogue=lambda acc, xr, g: (acc, xr + g * acc), extras=((x1, "tile"), (g2, "row")), tm=512)
    saved = dict(x=x, h=h, r1=r1, z=z, p=p, yp=yp, cq=cq, ckv=ckv, rq=rq, rkv=rkv, ya=ya, q=q, k=k, v=v, o=o, lse=lse,
                 yb=yb, merged=merged, mo=mo, x1=x1, h2=h2, r2=r2, f=f, act=act, m2=m2)
    return x2, saved, w


def _layer_bwd(l, dx2, sv, mod, w, small, cos_t, sin_t, send_grads):
    sh1, sc1, g1, sh2, sc2, g2 = mod
    tag = f"_l{l}"
    gw = {}
    dm2, dg2 = _gate_bwd(dx2, sv["m2"], g2, name="gate2_bwd" + tag)
    (df,) = _mm(dm2, w["w_ff2"], tb=True, name="ff2_dx" + tag, out_dtypes=(BF16,),
                epilogue=lambda acc, f: (acc * (2.0 * jnp.maximum(f, 0.0)),), extras=((sv["f"], "tile"),))
    (g_ff2,) = _mm(sv["act"], dm2, ta=True, name="ff2_dw" + tag, out_dtypes=(BF16,))
    (g_ff1,) = _mm(sv["h2"], df, ta=True, out_stack=N_CHIPS, name="ff1_dw" + tag, out_dtypes=(BF16,))
    sc2 = sc2 + send_grads("ff2", dict(w_ff2=g_ff2)) + send_grads("ff1", dict(w_ff1=g_ff1))
    (dh2,) = _mm(df, w["w_ff1"], tb=True, b_stack=True, name="ff1_dx" + tag)
    dx1, dln2, dsc2, dsh2 = _norm_mod_bwd(dh2, sv["x1"], sv["r2"], small["ln2_g"], sc2, dx2, name="norm2_bwd" + tag)
    dmo, dg1 = _gate_bwd(dx1, sv["mo"], g1, name="gate1_bwd" + tag)
    (dmerged,) = _mm(dmo, w["w_out"], tb=True, name="mix_out_dx" + tag)
    (gw["w_out"],) = _mm(sv["merged"], dmo, ta=True, name="mix_out_dw" + tag, out_dtypes=(BF16,))
    dya, dyb, dga, dgb = _merge_bwd(dmerged, sv["z"], sv["ya"], sv["yb"], name="merge_bwd" + tag)
    (gw["p_pool"],) = _mm(sv["yp"], dya, ta=True, name="pool_out_dw" + tag, out_dtypes=(BF16,))
    (dyp,) = _mm(dya, w["p_pool"], tb=True, name="pool_out_dx" + tag)
    du, g_w_pool, g_pool_scale = _pool_bwd(dyp, sv["p"], small["w_pool"], small["pool_scale"], name="pool_bwd" + tag)
    (gw["p_attn"],) = _mm(sv["o"], dyb, ta=True, name="attn_out_dw" + tag, out_dtypes=(BF16,))
    (do,) = _mm(dyb, w["p_attn"], tb=True, name="attn_out_dx" + tag, out_dtypes=(BF16,))
    delta = _attn_delta(do, sv["o"], name="attn_delta" + tag)
    dq, dk, dv = _attn_bwd(sv["q"], sv["k"], sv["v"], do, sv["lse"], delta, name="attn_bwd" + tag)
    dql = _rope_bwd_q(dq, cos_t, sin_t, name="rope_bwd_q" + tag)
    dkb, dkr = _key_bwd(dk, cos_t, sin_t, name="key_bwd" + tag)
    (gw["w_uq"],) = _mm(sv["cq"], dql, ta=True, name="q_proj_dw" + tag, out_dtypes=(BF16,))
    (gw["w_uk"],) = _mm(sv["ckv"], dkb, ta=True, name="k_proj_dw" + tag, out_dtypes=(BF16,))
    (gw["w_uv"],) = _mm(sv["ckv"], dv, ta=True, name="v_proj_dw" + tag, out_dtypes=(BF16,))
    (dcq,) = _mm(dql, w["w_uq"], tb=True, name="q_proj_dx" + tag)
    (dckv,) = _mm(dkb, w["w_uk"], tb=True, second=(dv, w["w_uv"]), name="kv_proj_dx" + tag)
    q_norm_g = small["q_norm_g"] + send_grads("mix", gw)
    dcq_raw, g_qn = _rms_bwd(dcq, sv["z"], ZC_CQ, sv["rq"], q_norm_g, name="q_norm_bwd" + tag)
    dckv_raw, g_kvn = _rms_bwd(dckv, sv["z"], ZC_CKV, sv["rkv"], small["kv_norm_g"], name="kv_norm_bwd" + tag)
    dz = jnp.concatenate([dcq_raw, dkr, du, dga, dgb, dckv_raw], axis=1)
    (g_in,) = _mm(sv["h"], dz, ta=True, name="in_proj_dw" + tag, out_dtypes=(BF16,))
    sc1 = sc1 + send_grads("in", dict(w_in=g_in))
    (dh,) = _mm(dz, w["w_in"], tb=True, name="in_proj_dx" + tag)
    dx, dln1, dsc1, dsh1 = _norm_mod_bwd(dh, sv["x"], sv["r1"], small["ln1_g"], sc1, dx1, name="norm1_bwd" + tag)
    dmod = jnp.concatenate([dsh1, dsc1, dg1, dsh2, dsc2, dg2], axis=0)
    gsmall = dict(ln1_g=dln1, ln2_g=dln2, q_norm_g=g_qn, kv_norm_g=g_kvn, w_pool=g_w_pool, pool_scale=g_pool_scale)
    return dx, dmod, gsmall


SMALL_LOSS = 6
SMALL_SINGLES = 16
SMALL_POOL = 24
SMALL_POOL_ROWS = len(POOL_WINDOWS) * POOL_GROUP * POOL_GROUP // D_MODEL
SMALL_ROWS = SMALL_POOL + DEPTH * SMALL_POOL_ROWS


def _pack_small(parts, *, name):
    def body(*refs):
        out_ref = refs[-1]
        out_ref[...] = jnp.zeros_like(out_ref)
        for ref, (_, row) in zip(refs[:-1], parts):
            out_ref[row:row + ref.shape[0], :] = ref[...]

    return pl.pallas_call(body, name=name, out_shape=jax.ShapeDtypeStruct((SMALL_ROWS, D_MODEL), F32),
                          compiler_params=_params())(*[a for a, _ in parts])


def kernel(x, c, positions, ln1_g, ln2_g, w_ada, b_ada, w_in, q_norm_g, w_uq, kv_norm_g, w_uk, w_uv, w_pool, pool_scale, p_pool, p_attn, w_out, w_ff1, w_ff2, final_g, loss_target, m_ln1_g, m_ln2_g, m_w_ada, m_b_ada, m_w_in, m_q_norm_g, m_w_uq, m_kv_norm_g, m_w_uk, m_w_uv, m_w_pool, m_pool_scale, m_p_pool, m_p_attn, m_w_out, m_w_ff1, m_w_ff2, m_final_g, v_ln1_g, v_ln2_g, v_w_ada, v_b_ada, v_w_in, v_q_norm_g, v_w_uq, v_kv_norm_g, v_w_uk, v_w_uv, v_w_pool, v_pool_scale, v_p_pool, v_p_attn, v_w_out, v_w_ff1, v_w_ff2, v_final_g):
    weights = dict(ln1_g=ln1_g, ln2_g=ln2_g, w_ada=w_ada, b_ada=b_ada, w_in=w_in, q_norm_g=q_norm_g, w_uq=w_uq,
                   kv_norm_g=kv_norm_g, w_uk=w_uk, w_uv=w_uv, w_pool=w_pool, pool_scale=pool_scale, p_pool=p_pool,
                   p_attn=p_attn, w_out=w_out, w_ff1=w_ff1, w_ff2=w_ff2, final_g=final_g)
    moms = dict(ln1_g=m_ln1_g, ln2_g=m_ln2_g, w_ada=m_w_ada, b_ada=m_b_ada, w_in=m_w_in, q_norm_g=m_q_norm_g,
                w_uq=m_w_uq, kv_norm_g=m_kv_norm_g, w_uk=m_w_uk, w_uv=m_w_uv, w_pool=m_w_pool,
                pool_scale=m_pool_scale, p_pool=m_p_pool, p_attn=m_p_attn, w_out=m_w_out, w_ff1=m_w_ff1,
                w_ff2=m_w_ff2, final_g=m_final_g)
    vels = dict(ln1_g=v_ln1_g, ln2_g=v_ln2_g, w_ada=v_w_ada, b_ada=v_b_ada, w_in=v_w_in, q_norm_g=v_q_norm_g,
                w_uq=v_w_uq, kv_norm_g=v_kv_norm_g, w_uk=v_w_uk, w_uv=v_w_uv, w_pool=v_w_pool,
                pool_scale=v_pool_scale, p_pool=v_p_pool, p_attn=v_p_attn, w_out=v_w_out, w_ff1=v_w_ff1,
                w_ff2=v_w_ff2, final_g=v_final_g)
    order = list(weights)
    seq = x.shape[1]
    my_chip = 2 * lax.axis_index("x") + lax.axis_index("y")
    my_dev = 2 * my_chip + lax.axis_index("c")
    ada_cols = w_ada.shape[2]

    small = [dict(ln1_g=ln1_g[l:l + 1], ln2_g=ln2_g[l:l + 1], q_norm_g=q_norm_g[l:l + 1], kv_norm_g=kv_norm_g[l:l + 1],
                  w_pool=w_pool[l], pool_scale=pool_scale[l:l + 1]) for l in range(DEPTH)]

    c_all = _all_gather_small(jnp.pad(c, ((0, 7), (0, 0))), name="cond_all_gather")
    c_act = _silu(c_all, name="cond_silu")
    b_mine = lax.dynamic_slice_in_dim(b_ada, my_chip * ada_cols, ada_cols, axis=1).reshape(1, DEPTH * ada_cols)
    (mod_cat,) = _mm(c_act, w_ada, b_stack=True, name="ada_fwd", epilogue=lambda acc, b: (acc + b,),
                     extras=((b_mine, "row"),))
    mod_mine = jnp.concatenate([mod_cat[::8, l * ada_cols:(l + 1) * ada_cols] for l in range(DEPTH)], axis=0)
    mod_all = _all_gather_small(mod_mine, name="mod_all_gather").reshape(N_DEV, DEPTH, N_DEV, ada_cols)

    zero = mod_all[0, 0, 0, 0] * 0.0
    local, started = {}, {}
    for l in range(DEPTH):
        for group in GROUPS:
            local[l, group] = _local_shard(weights, l, group, zero)
            started[l, group] = _exchange_start(local[l, group], name=f"weights_send_l{l}_{group}", scatter=False)
    pin = sum(st[4][0:1, 0:1] for st in started.values())

    def gathered_weights(l, group, after):
        land = _exchange_wait(started[l, group], after, name=f"weights_wait_l{l}_{group}", scatter=False)
        land = lax.dynamic_update_slice_in_dim(land, local[l, group][None], my_chip, axis=0)
        return _unpack_weights(land, group)

    mods = []
    for l in range(DEPTH):
        row = jnp.concatenate([lax.dynamic_index_in_dim(mod_all[2 * j, l], my_dev, axis=0, keepdims=True)
                               for j in range(N_CHIPS)], axis=1) + pin
        mods.append([row[:, i * D_MODEL:(i + 1) * D_MODEL] for i in range(N_MOD)])

    inv_freq = ROPE_THETA ** (-jnp.arange(0, QK_ROPE, 2, dtype=F32) / QK_ROPE)
    freq_lanes = jnp.concatenate([jnp.zeros((QK_NOPE,), F32), inv_freq, inv_freq,
                                  jnp.zeros((HEAD_PAD - QK_DIM,), F32)]).reshape(1, LANES)
    cos_t, sin_t = _rope_tables(positions.reshape(seq, 1), freq_lanes, name="rope_tables")

    xs, saved, wl = x.reshape(seq, D_MODEL), [], []
    for l in range(DEPTH):
        xs, sv, w_l = _layer_fwd(l, xs, mods[l], functools.partial(gathered_weights, l), small[l], cos_t, sin_t)
        saved.append(sv)
        wl.append(w_l)
    dx, loss_part, g_final = _final_loss(xs, final_g.reshape(1, D_MODEL), loss_target.reshape(seq, D_MODEL),
                                         name="final_loss")

    sent = []

    def send_grads(l, group, g):
        gpack = _pack_grads(g, group)
        started_g = _exchange_start(gpack, name=f"grads_send_l{l}_{group}", scatter=True)
        sent.append((l, group, started_g, gpack))
        return started_g[4][0:1, 0:1]

    dmod, gsmall = [None] * DEPTH, [None] * DEPTH
    for l in reversed(range(DEPTH)):
        dx, dmod[l], gsmall[l] = _layer_bwd(l, dx, saved[l], mods[l], wl[l], small[l], cos_t, sin_t,
                                            functools.partial(send_grads, l))
    grads = dict(x=dx.reshape(1, seq, D_MODEL))

    def lanes(a):
        flat = a.reshape(1, -1)
        return jnp.pad(flat, ((0, 0), (0, D_MODEL - flat.shape[1])))

    singles = [gsmall[0]["ln1_g"], gsmall[1]["ln1_g"], gsmall[0]["ln2_g"], gsmall[1]["ln2_g"], g_final,
               lanes(jnp.concatenate([gsmall[l]["pool_scale"] for l in range(DEPTH)], axis=1)),
               lanes(jnp.concatenate([gsmall[l]["q_norm_g"] for l in range(DEPTH)], axis=1)),
               lanes(jnp.concatenate([gsmall[l]["kv_norm_g"] for l in range(DEPTH)], axis=1))]
    parts = [(dmod[0], 0), (lanes(loss_part), SMALL_LOSS), (dmod[1], 8)]
    parts += [(a, SMALL_SINGLES + i) for i, a in enumerate(singles)]
    parts += [(gsmall[l]["w_pool"].reshape(-1, D_MODEL), SMALL_POOL + l * SMALL_POOL_ROWS) for l in range(DEPTH)]
    small_all = _all_gather_small(_pack_small(parts, name="small_grads_pack"), name="small_grads_all_gather")
    small_all = small_all.reshape(N_DEV, SMALL_ROWS, D_MODEL)
    ssum = _sum_slots(small_all, N_DEV, name="small_grads_sum")
    loss = ssum[SMALL_LOSS, 0]
    grads["b_ada"] = jnp.stack([ssum[8 * l:8 * l + N_MOD] for l in range(DEPTH)]).reshape(DEPTH, N_MOD * D_MODEL)
    grads["ln1_g"] = ssum[SMALL_SINGLES:SMALL_SINGLES + 2]
    grads["ln2_g"] = ssum[SMALL_SINGLES + 2:SMALL_SINGLES + 4]
    grads["final_g"] = ssum[SMALL_SINGLES + 4]
    grads["pool_scale"] = ssum[SMALL_SINGLES + 5].reshape(DEPTH, POOL_DIM)
    grads["q_norm_g"] = ssum[SMALL_SINGLES + 6, :DEPTH * Q_LORA].reshape(DEPTH, Q_LORA)
    grads["kv_norm_g"] = ssum[SMALL_SINGLES + 7, :DEPTH * KV_LORA].reshape(DEPTH, KV_LORA)
    grads["w_pool"] = ssum[SMALL_POOL:SMALL_ROWS].reshape(w_pool.shape)

    gsum, after = {}, ssum
    for l, group, started_g, gpack in sent:
        tg = f"_l{l}_{group}"
        land = _exchange_wait(started_g, after, name="grads_wait" + tg, scatter=True)
        own = lax.dynamic_index_in_dim(gpack, my_chip, axis=0, keepdims=True)
        land = lax.dynamic_update_slice_in_dim(land, own, my_chip, axis=0)
        part = _sum_slots(land, N_CHIPS, name="grads_sum_chips" + tg)
        other = _exchange_sibling(part, name="grads_swap_cores" + tg)
        gsum[group] = _add2_stacked(part, other, gsum.get(group), l, name="grads_sum_cores" + tg)
        after = gsum[group]
    grads.update(w_in=gsum["in"], w_ff1=gsum["ff1"], w_ff2=gsum["ff2"])
    off = 0
    for name in MIX_NAMES:
        grads[name] = gsum["mix"][:, off:off + ROWS_OF[name]].reshape(weights[name].shape)
        off += ROWS_OF[name]

    c_act_t = jnp.pad(c_act[::8].T, ((0, 0), (0, LANES - N_DEV)))
    d_mine = []
    for l in range(DEPTH):
        d_all = small_all[:, 8 * l:8 * l + N_MOD].reshape(N_DEV, N_MOD * D_MODEL)
        d_mine.append(lax.dynamic_slice_in_dim(d_all, my_chip * ada_cols, ada_cols, axis=1))
    d_cat = jnp.pad(jnp.concatenate(d_mine, axis=1), ((0, LANES - N_DEV), (0, 0)))
    (grads["w_ada"],) = _mm(c_act_t, d_cat, out_stack=DEPTH, name="ada_dw")

    def view(a):
        return a.reshape(1, -1) if a.ndim == 1 else a.reshape(-1, a.shape[-1])

    delta, new_m, new_v = {}, {}, {}
    for name in order:
        shape = weights[name].shape
        d, nm, nv = _adamw(view(weights[name]), view(grads[name]), view(moms[name]), view(vels[name]),
                           name="adamw_" + name)
        delta[name], new_m[name], new_v[name] = d.reshape(shape), nm.reshape(shape), nv.reshape(shape)
    return (loss, grads["x"], *[grads[n] for n in order], *[delta[n] for n in order],
            *[new_m[n] for n in order], *[new_v[n] for n in order])
```

```python
import functools
import math

import jax
import jax.numpy as jnp
from jax import lax
from jax.experimental import pallas as pl
from jax.experimental.pallas import tpu as pltpu

F32 = jnp.float32
BF16 = jnp.bfloat16
MESH = pl.DeviceIdType.MESH

D_MODEL = 1024
DEPTH = 2
POOL_WINDOWS = (2, 4, 8, 16)
POOL_GROUP = 128
POOL_DIM = 512
N_HEADS = 8
QK_NOPE = 64
QK_ROPE = 32
QK_DIM = QK_NOPE + QK_ROPE
V_DIM = 64
HEAD_PAD = 128
Q_LORA = 384
KV_LORA = 256
ROPE_THETA = 10000.0
ATTN_DIM = N_HEADS * V_DIM
D_FF = 4 * D_MODEL
N_MOD = 6
EPS = 1e-6
N_CHIPS = 4
N_DEV = 8

ADAM_LR = 0.001
ADAM_B1 = 0.9
ADAM_B2 = 0.999
ADAM_EPS = 1e-08
ADAM_WD = 0.01
ADAM_STEP = 10

VMEM_LIMIT_BYTES = 56 * 1024 * 1024
LANES = 128
HALO = 16

ZC_CQ = 0
ZC_KR = 384
ZC_U = 512
ZC_GA = 1024
ZC_GB = 2048
ZC_CKV = 3072
Z_DIM = 3328

PACK_ROWS = (("w_in", 808), ("w_uq", 72), ("w_uk", 32), ("w_uv", 32), ("p_pool", 128), ("p_attn", 128),
             ("w_out", 256), ("w_ff1", 1024), ("w_ff2", 1024))


def _params(sem=None, **kw):
    return pltpu.CompilerParams(dimension_semantics=sem, vmem_limit_bytes=VMEM_LIMIT_BYTES, **kw)


def _tile(n, target, unit=LANES):
    best = None
    for t in range(unit, min(n, target) + 1, unit):
        if n % t == 0:
            best = t
    return best if best is not None and 4 * best >= min(n, target) else n


def _near_tile(n, target):
    cands = [t for t in range(LANES, n + 1, LANES) if n % t == 0]
    return min(cands, key=lambda t: abs(math.log(t / target))) if cands else n


def _mm(a, b, *, name, ta=False, tb=False, out_dtypes=(F32,), epilogue=None, extras=(), tm=1024, tn=1024, tk=1024,
        second=None, b_stack=False, out_stack=None):
    (k_dim, m_dim) = a.shape if ta else a.shape[::-1]
    if b_stack:
        g_b, k_b, n_shard = b.shape
        n_dim, k_b = (k_b, g_b * n_shard) if tb else (g_b * n_shard, k_b)
    else:
        (n_dim, k_b) = b.shape if tb else b.shape[::-1]
    assert k_dim == k_b, (a.shape, b.shape)
    n_unit = n_shard if b_stack and not tb else n_dim // out_stack if out_stack else n_dim
    k_unit = n_shard if b_stack and tb else k_dim
    tm, tn, tk = _near_tile(m_dim, tm), _near_tile(n_unit, tn), _near_tile(k_unit, tk)
    nk = k_dim // tk
    n_extra, n_out = len(extras), len(out_dtypes)
    n_lhs = 4 if second else 2
    dims = (((0 if ta else 1,), (1 if tb else 0,)), ((), ()))
    if epilogue is None:
        epilogue = lambda acc: (acc,) * n_out

    def body(*refs):
        operand_refs, rest = refs[:n_lhs], refs[n_lhs:]
        extra_refs, out_refs = rest[:n_extra], rest[n_extra:n_extra + n_out]

        def product():
            total = None
            for a_ref, b_ref in zip(operand_refs[0::2], operand_refs[1::2]):
                part = lax.dot_general(a_ref[...].astype(BF16), b_ref[...].astype(BF16), dims, preferred_element_type=F32)
                total = part if total is None else total + part
            return total

        def finish(acc):
            outs = epilogue(acc, *[r[...] for r in extra_refs])
            for o_ref, o in zip(out_refs, outs):
                o_ref[...] = o.astype(o_ref.dtype)

        if nk == 1:
            finish(product())
            return
        acc_ref = rest[-1]
        k = pl.program_id(2)

        @pl.when(k == 0)
        def _():
            acc_ref[...] = product()

        @pl.when((k > 0) & (k < nk - 1))
        def _():
            acc_ref[...] += product()

        @pl.when(k == nk - 1)
        def _():
            finish(acc_ref[...] + product())

    a_spec = pl.BlockSpec((tk, tm), lambda i, j, k: (k, i)) if ta else pl.BlockSpec((tm, tk), lambda i, j, k: (i, k))
    if b_stack and tb:
        per = n_shard // tk
        b_spec = pl.BlockSpec((None, tn, tk), lambda i, j, k: (k // per, j, k % per))
    elif b_stack:
        per = n_shard // tn
        b_spec = pl.BlockSpec((None, tk, tn), lambda i, j, k: (j // per, k, j % per))
    elif tb:
        b_spec = pl.BlockSpec((tn, tk), lambda i, j, k: (j, k))
    else:
        b_spec = pl.BlockSpec((tk, tn), lambda i, j, k: (k, j))
    if out_stack:
        per_out = (n_dim // out_stack) // tn
        out_spec = pl.BlockSpec((None, tm, tn), lambda i, j, k: (j // per_out, i, j % per_out))
        out_dims = (out_stack, m_dim, n_dim // out_stack)
    else:
        out_spec = pl.BlockSpec((tm, tn), lambda i, j, k: (i, j))
        out_dims = (m_dim, n_dim)
    extra_specs = []
    for arr, kind in extras:
        if kind == "tile":
            extra_specs.append(pl.BlockSpec((tm, tn), lambda i, j, k: (i, j)))
        elif isinstance(kind, tuple):
            extra_specs.append(pl.BlockSpec((tm, tn), functools.partial(lambda i, j, k, c: (i, j + c), c=kind[1])))
        elif kind == "row":
            extra_specs.append(pl.BlockSpec((1, tn), lambda i, j, k: (0, j)))
        elif kind == "col":
            extra_specs.append(pl.BlockSpec((tm, 1), lambda i, j, k: (i, 0)))
        else:
            assert kind == "table", kind
            extra_specs.append(pl.BlockSpec((tm, LANES), lambda i, j, k: (i, 0)))
    return pl.pallas_call(
        body,
        name=name,
        grid=(m_dim // tm, n_dim // tn, nk),
        in_specs=[a_spec, b_spec] * (n_lhs // 2) + extra_specs,
        out_specs=[out_spec for _ in out_dtypes],
        out_shape=[jax.ShapeDtypeStruct(out_dims, dt) for dt in out_dtypes],
        scratch_shapes=[pltpu.VMEM((tm, tn), F32)] if nk > 1 else [],
        compiler_params=_params(("parallel", "parallel", "arbitrary")),
    )(a, b, *(second or ()), *[arr for arr, _ in extras])


def _rows(s):
    return min(512, s)


def _rope_tables(pos_col, inv_freq_lanes, *, name):
    s = pos_col.shape[0]
    tb = _rows(s)

    def body(pos_ref, f_ref, cos_ref, sin_ref):
        ang = pos_ref[...].astype(F32) * f_ref[...]
        lane = lax.broadcasted_iota(jnp.int32, ang.shape, 1)
        on = (lane >= QK_NOPE) & (lane < QK_DIM)
        cos_ref[...] = jnp.where(on, jnp.cos(ang), 0.0)
        sin_ref[...] = jnp.where(on, jnp.sin(ang), 0.0)

    return pl.pallas_call(
        body, name=name, grid=(s // tb,),
        in_specs=[pl.BlockSpec((tb, 1), lambda i: (i, 0)), pl.BlockSpec((1, LANES), lambda i: (0, 0))],
        out_specs=[pl.BlockSpec((tb, LANES), lambda i: (i, 0))] * 2,
        out_shape=[jax.ShapeDtypeStruct((s, LANES), F32)] * 2,
        compiler_params=_params(("parallel",)),
    )(pos_col, inv_freq_lanes)


def _rotate_half(x):
    lane = lax.broadcasted_iota(jnp.int32, x.shape, 1)
    half = QK_ROPE // 2
    first = (lane >= QK_NOPE) & (lane < QK_NOPE + half)
    second = (lane >= QK_NOPE + half) & (lane < QK_DIM)
    return jnp.where(first, -pltpu.roll(x, LANES - half, 1), jnp.where(second, pltpu.roll(x, half, 1), 0.0))


def _norm_mod(x, g, sc, sh, *, name):
    s, d = x.shape
    tb = _rows(s)

    def body(x_ref, g_ref, sc_ref, sh_ref, h_ref, r_ref):
        xv = x_ref[...]
        r = lax.rsqrt(jnp.mean(xv * xv, axis=-1, keepdims=True) + EPS)
        r_ref[...] = r
        h_ref[...] = (((xv * r) * g_ref[...]) * (1.0 + sc_ref[...]) + sh_ref[...]).astype(BF16)

    vec = pl.BlockSpec((1, d), lambda i: (0, 0))
    return pl.pallas_call(
        body, name=name, grid=(s // tb,),
        in_specs=[pl.BlockSpec((tb, d), lambda i: (i, 0)), vec, vec, vec],
        out_specs=[pl.BlockSpec((tb, d), lambda i: (i, 0)), pl.BlockSpec((tb, 1), lambda i: (i, 0))],
        out_shape=[jax.ShapeDtypeStruct((s, d), BF16), jax.ShapeDtypeStruct((s, 1), F32)],
        compiler_params=_params(("parallel",)),
    )(x, g, sc, sh)


def _window_sums(ext, sign):
    n = ext.shape[0]
    sums, cur, k = [], ext, 1
    for _ in POOL_WINDOWS:
        cur = cur + pltpu.roll(cur, k if sign > 0 else n - k, 0)
        sums.append(cur)
        k *= 2
    return sums


def _mixer_pre(z, cos_t, sin_t, w_pool, pool_scale, gq, gkv, *, name):
    s = z.shape[0]
    tb = _rows(s)
    hb = tb // HALO

    def body(zcq_ref, zkr_ref, zu_ref, zuh_ref, zckv_ref, cos_ref, sin_ref, wp_ref, ps_ref, gq_ref, gkv_ref,
             p_ref, yp_ref, cq_ref, ckv_ref, kr_ref, rq_ref, rkv_ref):
        i = pl.program_id(0)
        u = zu_ref[...]
        halo = jnp.where(i > 0, zuh_ref[...], 0.0)
        ext = jnp.concatenate([halo, u], axis=0)
        t = i * tb + lax.broadcasted_iota(jnp.int32, (tb, 1), 0)
        for g, (w, sw) in enumerate(zip(POOL_WINDOWS, _window_sums(ext, +1))):
            cols = slice(g * POOL_GROUP, (g + 1) * POOL_GROUP)
            cnt = jnp.minimum(t + 1, w).astype(F32)
            pg = (sw[HALO:, cols] / cnt - u[:, cols]).astype(BF16)
            p_ref[:, cols] = pg
            yg = jnp.dot(pg, wp_ref[g].astype(BF16), preferred_element_type=F32)
            yp_ref[:, cols] = (yg * ps_ref[:, cols]).astype(BF16)

        def rms(x_ref, g_ref, out_ref, r_ref):
            xv = x_ref[...]
            r = lax.rsqrt(jnp.mean(xv * xv, axis=-1, keepdims=True) + EPS)
            r_ref[...] = r
            out_ref[...] = ((xv * r) * g_ref[...]).astype(BF16)

        rms(zcq_ref, gq_ref, cq_ref, rq_ref)
        rms(zckv_ref, gkv_ref, ckv_ref, rkv_ref)
        kr = zkr_ref[...]
        kr_ref[...] = (kr * cos_ref[...] + _rotate_half(kr) * sin_ref[...]).astype(BF16)

    def zcol(width, off):
        return pl.BlockSpec((tb, width), lambda i: (i, off // width))

    def full(a):
        return pl.BlockSpec(a.shape, lambda i: (0,) * a.ndim)

    def out(width, dt):
        return pl.BlockSpec((tb, width), lambda i: (i, 0)), jax.ShapeDtypeStruct((s, width), dt)

    outs = [out(POOL_DIM, BF16), out(POOL_DIM, BF16), out(Q_LORA, BF16), out(KV_LORA, BF16), out(LANES, BF16),
            out(1, F32), out(1, F32)]
    return pl.pallas_call(
        body, name=name, grid=(s // tb,),
        in_specs=[zcol(Q_LORA, ZC_CQ), zcol(LANES, ZC_KR), zcol(POOL_DIM, ZC_U),
                  pl.BlockSpec((HALO, POOL_DIM), lambda i: (jnp.maximum(i * hb - 1, 0), ZC_U // POOL_DIM)),
                  zcol(KV_LORA, ZC_CKV),
                  pl.BlockSpec((tb, LANES), lambda i: (i, 0)), pl.BlockSpec((tb, LANES), lambda i: (i, 0)),
                  full(w_pool), full(pool_scale), full(gq), full(gkv)],
        out_specs=[o[0] for o in outs], out_shape=[o[1] for o in outs],
        compiler_params=_params(("parallel",)),
    )(z, z, z, z, z, cos_t, sin_t, w_pool, pool_scale, gq, gkv)


def _sigmoid(x):
    return 1.0 / (1.0 + jnp.exp(-x))


ATTN_SCALE = 1.0 / math.sqrt(QK_DIM)
NEG_BIG = -1e30


LOG2_E = math.log2(math.e)
EXP2_SCALE = ATTN_SCALE * LOG2_E
NT_DIMS = (((1,), (1,)), ((), ()))
TN_DIMS = (((0,), (0,)), ((), ()))


def _on_or_below_diagonal(t):
    return lax.broadcasted_iota(jnp.int32, (t, t), 0) >= lax.broadcasted_iota(jnp.int32, (t, t), 1)


HEADS_PER_STEP = 2
HEAD_COLS = [slice(g * HEAD_PAD, (g + 1) * HEAD_PAD) for g in range(HEADS_PER_STEP)]


def _attn_fwd(q, k, v, *, name):
    s = q.shape[0]
    t = _rows(s)
    wide = HEADS_PER_STEP * HEAD_PAD

    def body(q_ref, k_ref, v_ref, o_ref, lse_ref):
        qi = pl.program_id(1)
        qs = [q_ref[:, cols] for cols in HEAD_COLS]

        def block(j, carry, diagonal):
            rows = pl.ds(pl.multiple_of(j * t, t), t)
            out = []
            for qv, cols, (m, l, acc) in zip(qs, HEAD_COLS, carry):
                sc = lax.dot_general(qv, k_ref[rows, cols], NT_DIMS, preferred_element_type=F32)
                if diagonal:
                    sc = jnp.where(_on_or_below_diagonal(t), sc, NEG_BIG)
                m_new = jnp.maximum(m, jnp.max(sc, axis=-1, keepdims=True))
                p = jnp.exp2((sc - m_new) * EXP2_SCALE)
                alpha = jnp.exp2((m - m_new) * EXP2_SCALE)
                l = alpha * l + jnp.sum(p, axis=-1, keepdims=True)
                acc = alpha * acc + jnp.dot(p.astype(BF16), v_ref[rows, cols], preferred_element_type=F32)
                out.append((m_new, l, acc))
            return tuple(out)

        init = tuple((jnp.full((t, 1), -jnp.inf, F32), jnp.zeros((t, 1), F32), jnp.zeros((t, HEAD_PAD), F32))
                     for _ in HEAD_COLS)
        carry = lax.fori_loop(0, qi, lambda j, c: block(j, c, False), init)
        for g, (cols, (m, l, acc)) in enumerate(zip(HEAD_COLS, block(qi, carry, True))):
            o_ref[:, cols] = (acc / l).astype(BF16)
            lse_ref[g] = m * ATTN_SCALE + jnp.log(l)

    q_spec = pl.BlockSpec((t, wide), lambda h, i: (i, h))
    kv_spec = pl.BlockSpec((s, wide), lambda h, i: (0, h))
    return pl.pallas_call(
        body, name=name, grid=(N_HEADS // HEADS_PER_STEP, s // t),
        in_specs=[q_spec, kv_spec, kv_spec],
        out_specs=[q_spec, pl.BlockSpec((HEADS_PER_STEP, t, 1), lambda h, i: (h, i, 0))],
        out_shape=[jax.ShapeDtypeStruct((s, N_HEADS * HEAD_PAD), BF16), jax.ShapeDtypeStruct((N_HEADS, s, 1), F32)],
        compiler_params=_params(("parallel", "parallel")),
    )(q, k, v)


def _attn_delta(do, o, *, name):
    s = o.shape[0]
    t = _rows(s)

    def body(do_ref, o_ref, out_ref):
        for h in range(N_HEADS):
            cols = slice(h * HEAD_PAD, (h + 1) * HEAD_PAD)
            out_ref[h] = jnp.sum(do_ref[:, cols].astype(F32) * o_ref[:, cols].astype(F32), axis=-1, keepdims=True)

    blk = pl.BlockSpec((t, N_HEADS * HEAD_PAD), lambda i: (i, 0))
    return pl.pallas_call(
        body, name=name, grid=(s // t,), in_specs=[blk, blk],
        out_specs=pl.BlockSpec((N_HEADS, t, 1), lambda i: (0, i, 0)),
        out_shape=jax.ShapeDtypeStruct((N_HEADS, s, 1), F32),
        compiler_params=_params(("parallel",)),
    )(do, o)


def _attn_bwd(q, k, v, do, lse, delta, *, name):
    s = q.shape[0]
    t = _rows(s)
    nt = s // t

    def body(q_ref, k_ref, v_ref, do_ref, lse_ref, dl_ref, dq_ref, dk_ref, dv_ref):
        kj = pl.program_id(1)

        @pl.when(kj == 0)
        def _():
            dq_ref[...] = jnp.zeros_like(dq_ref)

        kvs = [(k_ref[:, cols], v_ref[:, cols]) for cols in HEAD_COLS]

        def block(i, carry, diagonal):
            rows = pl.ds(pl.multiple_of(i * t, t), t)
            out = []
            for g, (cols, (kv, vv), (dk, dv)) in enumerate(zip(HEAD_COLS, kvs, carry)):
                qv, dov = q_ref[rows, cols], do_ref[rows, cols]
                sc = lax.dot_general(qv, kv, NT_DIMS, preferred_element_type=F32)
                p = jnp.exp2(sc * EXP2_SCALE - lse_ref[g, rows, :] * LOG2_E)
                if diagonal:
                    p = jnp.where(_on_or_below_diagonal(t), p, 0.0)
                dp = lax.dot_general(dov, vv, NT_DIMS, preferred_element_type=F32)
                ds = (p * (dp - dl_ref[g, rows, :])).astype(BF16)
                dv = dv + lax.dot_general(p.astype(BF16), dov, TN_DIMS, preferred_element_type=F32)
                dk = dk + lax.dot_general(ds, qv, TN_DIMS, preferred_element_type=F32)
                dq_ref[rows, cols] += jnp.dot(ds, kv, preferred_element_type=F32) * ATTN_SCALE
                out.append((dk, dv))
            return tuple(out)

        zero = jnp.zeros((t, HEAD_PAD), F32)
        carry = block(kj, tuple((zero, zero) for _ in HEAD_COLS), True)
        for cols, (dk, dv) in zip(HEAD_COLS, lax.fori_loop(kj + 1, nt, lambda i, c: block(i, c, False), carry)):
            dk_ref[:, cols] = dk * ATTN_SCALE
            dv_ref[:, cols] = dv.astype(BF16)

    full_spec = pl.BlockSpec((s, HEADS_PER_STEP * HEAD_PAD), lambda h, j: (0, h))
    kv_spec = pl.BlockSpec((t, HEADS_PER_STEP * HEAD_PAD), lambda h, j: (j, h))
    vec_spec = pl.BlockSpec((HEADS_PER_STEP, s, 1), lambda h, j: (h, 0, 0))
    wide = (s, N_HEADS * HEAD_PAD)
    return pl.pallas_call(
        body, name=name, grid=(N_HEADS // HEADS_PER_STEP, nt),
        in_specs=[full_spec, kv_spec, kv_spec, full_spec, vec_spec, vec_spec],
        out_specs=[full_spec, kv_spec, kv_spec],
        out_shape=[jax.ShapeDtypeStruct(wide, F32), jax.ShapeDtypeStruct(wide, F32), jax.ShapeDtypeStruct(wide, BF16)],
        compiler_params=_params(("parallel", "arbitrary")),
    )(q, k, v, do, lse, delta)


def _acc_specs(widths):
    return ([pl.BlockSpec((1, w), lambda i: (0, 0)) for w in widths],
            [jax.ShapeDtypeStruct((1, w), F32) for w in widths])


def _gate_grads(dxv, m_ref, gate_ref, dm_ref, dgate_ref):
    dm_ref[...] = (dxv * gate_ref[...]).astype(BF16)
    dgate_ref[...] += jnp.sum(dxv * m_ref[...], axis=0, keepdims=True)


def _final_loss(x, g, target, m, gate, *, name):
    s, d = x.shape
    tb = _rows(s)

    def body(x_ref, g_ref, t_ref, m_ref, gate_ref, dx_ref, loss_ref, dg_ref, dm_ref, dgate_ref):
        @pl.when(pl.program_id(0) == 0)
        def _():
            loss_ref[...] = jnp.zeros_like(loss_ref)
            dg_ref[...] = jnp.zeros_like(dg_ref)
            dgate_ref[...] = jnp.zeros_like(dgate_ref)

        xv = x_ref[...]
        r = lax.rsqrt(jnp.mean(xv * xv, axis=-1, keepdims=True) + EPS)
        xn = xv * r
        err = xn * g_ref[...] - t_ref[...]
        loss_ref[...] += 0.5 * jnp.sum(jnp.mean(err * err, axis=-1, keepdims=True), axis=0, keepdims=True)
        dy = err / d
        dg_ref[...] += jnp.sum(dy * xn, axis=0, keepdims=True)
        dxn = dy * g_ref[...]
        dxv = r * (dxn - xn * jnp.mean(dxn * xn, axis=-1, keepdims=True))
        dx_ref[...] = dxv
        _gate_grads(dxv, m_ref, gate_ref, dm_ref, dgate_ref)

    blk = pl.BlockSpec((tb, d), lambda i: (i, 0))
    vec = pl.BlockSpec((1, d), lambda i: (0, 0))
    acc_specs, acc_shapes = _acc_specs((LANES, d))
    return pl.pallas_call(
        body, name=name, grid=(s // tb,),
        in_specs=[blk, vec, blk, blk, vec],
        out_specs=[blk] + acc_specs + [blk, vec],
        out_shape=[jax.ShapeDtypeStruct((s, d), F32)] + acc_shapes + [jax.ShapeDtypeStruct((s, d), BF16),
                                                                     jax.ShapeDtypeStruct((1, d), F32)],
        compiler_params=_params(("arbitrary",)),
    )(x, g, target, m, gate)


def _norm_mod_bwd(dh, x, r, g, sc, dx_skip, *, name, gate=None):
    s, d = x.shape
    tb = _rows(s)
    nb = s // tb
    n_gate = 2 if gate else 0

    def body(dh_ref, x_ref, r_ref, g_ref, sc_ref, skip_ref, *rest):
        gate_refs, (dx_ref, dg_ref, dsc_ref, dsh_ref) = rest[:n_gate], rest[n_gate:n_gate + 4]
        gate_outs, da_sc = rest[n_gate + 4:-1], rest[-1]
        i = pl.program_id(0)

        @pl.when(i == 0)
        def _():
            da_sc[...] = jnp.zeros_like(da_sc)
            dsh_ref[...] = jnp.zeros_like(dsh_ref)
            if gate:
                gate_outs[1][...] = jnp.zeros_like(gate_outs[1])

        dhv, rv = dh_ref[...], r_ref[...]
        xn = x_ref[...] * rv
        dsh_ref[...] += jnp.sum(dhv, axis=0, keepdims=True)
        da_sc[...] += jnp.sum(dhv * xn, axis=0, keepdims=True)
        dxn = dhv * (g_ref[...] * (1.0 + sc_ref[...]))
        dxv = skip_ref[...] + rv * (dxn - xn * jnp.mean(dxn * xn, axis=-1, keepdims=True))
        dx_ref[...] = dxv
        if gate:
            _gate_grads(dxv, *gate_refs, *gate_outs)

        @pl.when(i == nb - 1)
        def _():
            dg_ref[...] = da_sc[...] * (1.0 + sc_ref[...])
            dsc_ref[...] = da_sc[...] * g_ref[...]

    blk = pl.BlockSpec((tb, d), lambda i: (i, 0))
    vec = pl.BlockSpec((1, d), lambda i: (0, 0))
    acc_specs, acc_shapes = _acc_specs((d, d, d))
    gate_specs = [blk, vec] if gate else []
    gate_shapes = [jax.ShapeDtypeStruct((s, d), BF16), jax.ShapeDtypeStruct((1, d), F32)] if gate else []
    return pl.pallas_call(
        body, name=name, grid=(nb,),
        in_specs=[blk, blk, pl.BlockSpec((tb, 1), lambda i: (i, 0)), vec, vec, blk] + gate_specs,
        out_specs=[blk] + acc_specs + gate_specs,
        out_shape=[jax.ShapeDtypeStruct((s, d), F32)] + acc_shapes + gate_shapes,
        scratch_shapes=[pltpu.VMEM((1, d), F32)],
        compiler_params=_params(("arbitrary",)),
    )(dh, x, r, g, sc, dx_skip, *(gate or ()))


def _pool_bwd(dyp, p, w_pool, pool_scale, *, name):
    s = dyp.shape[0]
    tb = _rows(s)
    nb = s // tb
    hb = tb // HALO
    nt_dims = (((1,), (1,)), ((), ()))
    tn_dims = (((0,), (0,)), ((), ()))

    def body(dy_ref, dyn_ref, p_ref, wp_ref, ps_ref, du_ref, gwp_ref, gps_ref):
        i = pl.program_id(0)

        @pl.when(i == 0)
        def _():
            gwp_ref[...] = jnp.zeros_like(gwp_ref)
            gps_ref[...] = jnp.zeros_like(gps_ref)

        cur = dy_ref[...]
        nxt = jnp.where(i < nb - 1, dyn_ref[...], 0.0)
        dpw = (jnp.concatenate([cur, nxt], axis=0) * ps_ref[...]).astype(BF16)
        t = i * tb + lax.broadcasted_iota(jnp.int32, (tb + HALO, 1), 0)
        for g, w in enumerate(POOL_WINDOWS):
            cols = slice(g * POOL_GROUP, (g + 1) * POOL_GROUP)
            wg = wp_ref[g].astype(BF16)
            dp = lax.dot_general(dpw[:, cols], wg, nt_dims, preferred_element_type=F32)
            e = dp / jnp.minimum(t + 1, w).astype(F32)
            lead = _window_sums(e, -1)[g]
            du_ref[:, cols] = (lead[:tb] - dp[:tb]).astype(BF16)
            pg = p_ref[:, cols]
            pw = jnp.dot(pg, wg, preferred_element_type=F32)
            gps_ref[:, cols] += jnp.sum(cur[:, cols] * pw, axis=0, keepdims=True)
            gwp_ref[g] += lax.dot_general(pg, dpw[:tb, cols], tn_dims, preferred_element_type=F32)

    blk = pl.BlockSpec((tb, POOL_DIM), lambda i: (i, 0))
    return pl.pallas_call(
        body, name=name, grid=(nb,),
        in_specs=[blk, pl.BlockSpec((HALO, POOL_DIM), lambda i: (jnp.minimum((i + 1) * hb, s // HALO - 1), 0)), blk,
                  pl.BlockSpec(w_pool.shape, lambda i: (0, 0, 0)), pl.BlockSpec((1, POOL_DIM), lambda i: (0, 0))],
        out_specs=[blk, pl.BlockSpec(w_pool.shape, lambda i: (0, 0, 0)), pl.BlockSpec((1, POOL_DIM), lambda i: (0, 0))],
        out_shape=[jax.ShapeDtypeStruct((s, POOL_DIM), BF16), jax.ShapeDtypeStruct(w_pool.shape, F32),
                   jax.ShapeDtypeStruct((1, POOL_DIM), F32)],
        compiler_params=_params(("arbitrary",)),
    )(dyp, dyp, p, w_pool, pool_scale)


def _rope_bwd_q(dq, cos_t, sin_t, *, name):
    s = dq.shape[0]
    tb = _rows(s)

    def body(dq_ref, cos_ref, sin_ref, out_ref):
        sin = sin_ref[...]
        lane = lax.broadcasted_iota(jnp.int32, sin.shape, 1)
        cos_q = cos_ref[...] + jnp.where(lane < QK_NOPE, 1.0, 0.0)
        for h in range(N_HEADS):
            cols = slice(h * HEAD_PAD, (h + 1) * HEAD_PAD)
            dqv = dq_ref[:, cols]
            out_ref[:, cols] = (dqv * cos_q - _rotate_half(dqv * sin)).astype(BF16)

    blk = pl.BlockSpec((tb, N_HEADS * HEAD_PAD), lambda i: (i, 0))
    tab = pl.BlockSpec((tb, LANES), lambda i: (i, 0))
    return pl.pallas_call(
        body, name=name, grid=(s // tb,), in_specs=[blk, tab, tab], out_specs=blk,
        out_shape=jax.ShapeDtypeStruct(dq.shape, BF16),
        compiler_params=_params(("parallel",)),
    )(dq, cos_t, sin_t)


def _key_bwd(dk, cos_t, sin_t, *, name):
    s = dk.shape[0]
    tb = _rows(s)

    def body(dk_ref, cos_ref, sin_ref, dkb_ref, dkr_ref):
        dkv = dk_ref[...]
        dkb_ref[...] = dkv.astype(BF16)
        tot = dkv[:, :HEAD_PAD]
        for h in range(1, N_HEADS):
            tot = tot + dkv[:, h * HEAD_PAD:(h + 1) * HEAD_PAD]
        dkr_ref[...] = (tot * cos_ref[...] - _rotate_half(tot * sin_ref[...])).astype(BF16)

    blk = pl.BlockSpec((tb, N_HEADS * HEAD_PAD), lambda i: (i, 0))
    tab = pl.BlockSpec((tb, LANES), lambda i: (i, 0))
    return pl.pallas_call(
        body, name=name, grid=(s // tb,), in_specs=[blk, tab, tab], out_specs=[blk, tab],
        out_shape=[jax.ShapeDtypeStruct(dk.shape, BF16), jax.ShapeDtypeStruct((s, LANES), BF16)],
        compiler_params=_params(("parallel",)),
    )(dk, cos_t, sin_t)


def _rms_bwd(dy, z, z_off, r, g, *, name):
    s, n = dy.shape
    tb = _rows(s)

    def body(dy_ref, x_ref, r_ref, g_ref, dx_ref, dg_ref):
        @pl.when(pl.program_id(0) == 0)
        def _():
            dg_ref[...] = jnp.zeros_like(dg_ref)

        dyv, rv = dy_ref[...], r_ref[...]
        xn = x_ref[...] * rv
        dg_ref[...] += jnp.sum(dyv * xn, axis=0, keepdims=True)
        dxn = dyv * g_ref[...]
        dx_ref[...] = (rv * (dxn - xn * jnp.mean(dxn * xn, axis=-1, keepdims=True))).astype(BF16)

    blk = pl.BlockSpec((tb, n), lambda i: (i, 0))
    acc_specs, acc_shapes = _acc_specs((n,))
    return pl.pallas_call(
        body, name=name, grid=(s // tb,),
        in_specs=[blk, pl.BlockSpec((tb, n), lambda i: (i, z_off // n)), pl.BlockSpec((tb, 1), lambda i: (i, 0)),
                  pl.BlockSpec((1, n), lambda i: (0, 0))],
        out_specs=[blk] + acc_specs, out_shape=[jax.ShapeDtypeStruct((s, n), BF16)] + acc_shapes,
        compiler_params=_params(("arbitrary",)),
    )(dy, z, r, g)


def _silu(c, *, name):
    def body(c_ref, out_ref):
        cv = c_ref[...]
        out_ref[...] = (cv * _sigmoid(cv)).astype(BF16)

    return pl.pallas_call(body, name=name, out_shape=jax.ShapeDtypeStruct(c.shape, BF16),
                          compiler_params=_params())(c)


def _sum_slots(a, n, *, name, out_dtype=F32):
    _, rows, cols = a.shape
    tr = _tile(rows, 256, 8)

    def body(a_ref, out_ref):
        tot = a_ref[0].astype(F32)
        for j in range(1, n):
            tot = tot + a_ref[j].astype(F32)
        out_ref[...] = tot.astype(out_dtype)

    return pl.pallas_call(
        body, name=name, grid=(rows // tr,),
        in_specs=[pl.BlockSpec((n, tr, cols), lambda i: (0, i, 0))],
        out_specs=pl.BlockSpec((tr, cols), lambda i: (i, 0)),
        out_shape=jax.ShapeDtypeStruct((rows, cols), out_dtype),
        compiler_params=_params(("parallel",)),
    )(a)


def _add2_stacked(a, b, stacked, l, *, name):
    rows, cols = a.shape
    tr = _tile(rows, 256, 8)

    def body(a_ref, b_ref, *rest):
        rest[-1][...] = a_ref[...] + b_ref[...]

    blk = pl.BlockSpec((tr, cols), lambda i: (i, 0))
    carried = [] if stacked is None else [stacked]
    return pl.pallas_call(
        body, name=name, grid=(rows // tr,),
        in_specs=[blk, blk] + [pl.BlockSpec(memory_space=pl.ANY) for _ in carried],
        out_specs=pl.BlockSpec((None, tr, cols), lambda i: (l, i, 0)),
        out_shape=jax.ShapeDtypeStruct((DEPTH, rows, cols), F32),
        input_output_aliases={2: 0} if carried else {},
        compiler_params=_params(("parallel",)),
    )(a, b, *carried)


def _adamw(w, g, m, v, *, name):
    shape = w.shape
    if w.ndim == 2:
        w, g, m, v = (a.reshape((1,) + shape) for a in (w, g, m, v))
    layers, rows, cols = w.shape
    tr = _tile(rows, max(8, (1 << 18) // cols), 8)
    c1 = 1.0 - ADAM_B1 ** ADAM_STEP
    c2 = 1.0 - ADAM_B2 ** ADAM_STEP

    def body(w_ref, g_ref, m_ref, v_ref, d_ref, nm_ref, nv_ref):
        gv = g_ref[...]
        nm = ADAM_B1 * m_ref[...] + (1.0 - ADAM_B1) * gv
        nv = ADAM_B2 * v_ref[...] + (1.0 - ADAM_B2) * (gv * gv)
        nm_ref[...] = nm
        nv_ref[...] = nv
        d_ref[...] = -ADAM_LR * ((nm / c1) / (jnp.sqrt(nv / c2) + ADAM_EPS) + ADAM_WD * w_ref[...])

    blk = pl.BlockSpec((None, tr, cols), lambda l, i: (l, i, 0))
    outs = pl.pallas_call(
        body, name=name, grid=(layers, rows // tr), in_specs=[blk] * 4, out_specs=[blk] * 3,
        out_shape=[jax.ShapeDtypeStruct((layers, rows, cols), F32)] * 3,
        compiler_params=_params(("parallel", "parallel")),
    )(w, g, m, v)
    return [o.reshape(shape) for o in outs]


def _coords():
    return lax.axis_index("x"), lax.axis_index("y"), lax.axis_index("c")


def _other_chips(x, y):
    return [(1 - x, y), (x, 1 - y), (1 - x, 1 - y)]


def _all_gather_small(blk, *, name):
    m_per, n = blk.shape

    def body(x_ref, out_ref, send_sems, recv_sems, local_sem):
        x, y, c = _coords()
        me, sibling = (x, y, c), (x, y, 1 - c)
        chips = _other_chips(x, y)

        def rows(px, py, pc):
            return out_ref.at[pl.ds((4 * px + 2 * py + pc) * m_per, m_per), :]

        def copy(k, block, to, src=None):
            return pltpu.make_async_remote_copy(
                src_ref=rows(*block) if src is None else src, dst_ref=rows(*block),
                send_sem=send_sems.at[k], recv_sem=recv_sems.at[k], device_id=to, device_id_type=MESH)

        mine = pltpu.make_async_copy(x_ref, rows(*me), local_sem)
        mine.start()
        first = [copy(0, me, sibling, src=x_ref)]
        first += [copy(1 + j, me, (*chip, c), src=x_ref) for j, chip in enumerate(chips)]
        for cp in first:
            cp.start()
        passed = [copy(4 + j, (*chip, c), sibling) for j, chip in enumerate(chips)]
        for j, chip in enumerate(chips):
            copy(1 + j, (*chip, c), me).wait_recv()
            passed[j].start()
        copy(0, sibling, me).wait_recv()
        for j, chip in enumerate(chips):
            copy(4 + j, (*chip, 1 - c), me).wait_recv()
        for cp in first + passed:
            cp.wait_send()
        mine.wait()

    return pl.pallas_call(
        body, name=name,
        out_shape=jax.ShapeDtypeStruct((N_DEV * m_per, n), blk.dtype),
        in_specs=[pl.BlockSpec(memory_space=pltpu.VMEM)],
        out_specs=pl.BlockSpec(memory_space=pltpu.VMEM),
        scratch_shapes=[pltpu.SemaphoreType.DMA((7,)), pltpu.SemaphoreType.DMA((7,)), pltpu.SemaphoreType.DMA],
        compiler_params=_params(),
    )(blk)


HBM_SPEC = pl.BlockSpec(memory_space=pltpu.HBM)
SEM_SPEC = pl.BlockSpec(memory_space=pltpu.SEMAPHORE)
DATAFLOW = pltpu.SideEffectType.DATAFLOW_SIDE_EFFECTING


def _chip_copies(src_ref, land_ref, send_sems, recv_sems, scatter):
    x, y, c = _coords()
    my = 2 * x + y
    outgoing, incoming = [], []
    for k, (px, py) in enumerate(_other_chips(x, y)):
        peer = 2 * px + py

        def copy(src_slot, dst_slot):
            return pltpu.make_async_remote_copy(
                src_ref=src_ref.at[src_slot] if scatter else src_ref, dst_ref=land_ref.at[dst_slot],
                send_sem=send_sems.at[k], recv_sem=recv_sems.at[k], device_id=(px, py, c), device_id_type=MESH)

        outgoing.append(copy(peer, my))
        incoming.append(copy(my, peer))
    return outgoing, incoming


def _exchange_start(src, *, name, scatter):
    land_shape = src.shape if scatter else (N_CHIPS,) + src.shape

    def body(src_ref, land_ref, send_sems, recv_sems, src_thru, land_thru, token):
        outgoing, _ = _chip_copies(src_ref, land_ref, send_sems, recv_sems, scatter)
        for cp in outgoing:
            cp.start()
        token[...] = jnp.zeros_like(token)

    return pl.pallas_call(
        body, name=name,
        out_shape=(pltpu.SemaphoreType.DMA((N_CHIPS - 1,)), pltpu.SemaphoreType.DMA((N_CHIPS - 1,)),
                   pltpu.HBM(src.shape, src.dtype), pltpu.HBM(land_shape, src.dtype), jax.ShapeDtypeStruct((8, LANES), F32)),
        in_specs=(HBM_SPEC, HBM_SPEC),
        out_specs=(SEM_SPEC, SEM_SPEC, HBM_SPEC, HBM_SPEC, pl.BlockSpec(memory_space=pltpu.VMEM)),
        input_output_aliases={0: 2, 1: 3},
        compiler_params=pltpu.CompilerParams(has_side_effects=DATAFLOW),
    )(pltpu.with_memory_space_constraint(src, pltpu.HBM),
      pltpu.with_memory_space_constraint(lax.empty(land_shape, src.dtype), pltpu.HBM))


def _exchange_wait(started, after, *, name, scatter):
    send_sems, recv_sems, src_thru, land_thru, _ = started

    def body(src_ref, land_ref, send_sems, recv_sems, after_ref, src_dead, got_ref):
        outgoing, incoming = _chip_copies(src_ref, land_ref, send_sems, recv_sems, scatter)
        for cp in outgoing:
            cp.wait_send()
        for cp in incoming:
            cp.wait_recv()

    return pl.pallas_call(
        body, name=name,
        out_shape=(pltpu.HBM(src_thru.shape, src_thru.dtype), pltpu.HBM(land_thru.shape, land_thru.dtype)),
        in_specs=(HBM_SPEC, HBM_SPEC, SEM_SPEC, SEM_SPEC, pl.BlockSpec(memory_space=pl.ANY)),
        out_specs=(HBM_SPEC, HBM_SPEC),
        input_output_aliases={0: 0, 1: 1},
        compiler_params=pltpu.CompilerParams(has_side_effects=DATAFLOW),
    )(src_thru, land_thru, send_sems, recv_sems, after)


def _exchange_sibling(src, *, name):
    def body(src_ref, out_ref, send_sem, recv_sem):
        x, y, c = _coords()
        cp = pltpu.make_async_remote_copy(src_ref=src_ref, dst_ref=out_ref, send_sem=send_sem, recv_sem=recv_sem,
                                          device_id=(x, y, 1 - c), device_id_type=MESH)
        cp.start()
        cp.wait()

    return pl.pallas_call(
        body, name=name,
        out_shape=jax.ShapeDtypeStruct(src.shape, src.dtype),
        in_specs=[pl.BlockSpec(memory_space=pl.ANY)],
        out_specs=pl.BlockSpec(memory_space=pl.ANY),
        scratch_shapes=[pltpu.SemaphoreType.DMA, pltpu.SemaphoreType.DMA],
        compiler_params=_params(),
    )(src)


def _pack_rows(a):
    return a.reshape(-1, D_MODEL)


def _pad_heads(w, width):
    r = w.shape[0]
    return jnp.pad(w, ((0, 0), (0, 0), (0, HEAD_PAD - width))).reshape(r, N_HEADS * HEAD_PAD)


MIX_NAMES = ("w_uq", "w_uk", "w_uv", "p_pool", "p_attn", "w_out")
GROUPS = ("in", "mix", "ff1", "ff2")
ROWS_OF = dict(PACK_ROWS)
W_IN_COLS = 3232
W_IN_SHARD = W_IN_COLS // N_CHIPS


def _local_shard(weights, l, group, zero):
    if group == "mix":
        shard = jnp.concatenate([_pack_rows(weights[n][l]) for n in MIX_NAMES], axis=0)
    else:
        shard = weights[{"in": "w_in", "ff1": "w_ff1", "ff2": "w_ff2"}[group]][l]
    return (shard + zero).astype(BF16)


def _unpack_weights(gathered, group):
    def cols(a, k):
        return a.reshape(N_CHIPS, k, -1).transpose(1, 0, 2).reshape(k, -1)

    if group == "in":
        full = gathered.transpose(1, 0, 2).reshape(D_MODEL, W_IN_COLS)
        kr = jnp.pad(full[:, 1152:1184], ((0, 0), (QK_NOPE, HEAD_PAD - QK_DIM)))
        return dict(w_in=jnp.concatenate([full[:, 512:896], kr, full[:, 0:512], full[:, 1184:3232], full[:, 896:1152]],
                                         axis=1))
    if group == "ff1":
        return dict(w_ff1=gathered)
    if group == "ff2":
        return dict(w_ff2=gathered.reshape(D_FF, D_MODEL))

    def p_attn(a):
        full = cols(a, ATTN_DIM).reshape(N_HEADS, V_DIM, D_MODEL)
        return jnp.pad(full, ((0, 0), (0, HEAD_PAD - V_DIM), (0, 0))).reshape(N_HEADS * HEAD_PAD, D_MODEL)

    build = dict(
        w_uq=lambda a: _pad_heads(a.reshape(Q_LORA, N_HEADS, QK_DIM), QK_DIM),
        w_uk=lambda a: _pad_heads(a.reshape(KV_LORA, N_HEADS, QK_NOPE), QK_NOPE),
        w_uv=lambda a: _pad_heads(a.reshape(KV_LORA, N_HEADS, V_DIM), V_DIM),
        p_pool=lambda a: cols(a, POOL_DIM),
        p_attn=p_attn,
        w_out=lambda a: a.reshape(D_MODEL, D_MODEL),
    )
    w, off = {}, 0
    for name in MIX_NAMES:
        w[name] = build[name](gathered[:, off:off + ROWS_OF[name]])
        off += ROWS_OF[name]
    return w


def _pack_grads(g, group):
    def cols(a):
        k = a.shape[0]
        return a.reshape(k, N_CHIPS, -1).transpose(1, 0, 2).reshape(N_CHIPS, -1, D_MODEL)

    def rows(a):
        return a.reshape(N_CHIPS, -1, D_MODEL)

    def heads(width):
        return lambda a: rows(a.reshape(a.shape[0], N_HEADS, HEAD_PAD)[:, :, :width])

    if group == "in":
        gi = g["w_in"]
        full = jnp.concatenate([gi[:, ZC_U:ZC_U + 512], gi[:, ZC_CQ:ZC_CQ + 384], gi[:, ZC_CKV:ZC_CKV + 256],
                                gi[:, ZC_KR + QK_NOPE:ZC_KR + QK_DIM], gi[:, ZC_GA:ZC_GA + 2048]], axis=1)
        return full.reshape(D_MODEL, N_CHIPS, W_IN_SHARD).transpose(1, 0, 2)
    if group == "ff1":
        return g["w_ff1"]
    if group == "ff2":
        return g["w_ff2"].reshape(N_CHIPS, D_FF // N_CHIPS, D_MODEL)

    def p_attn(a):
        return cols(a.reshape(N_HEADS, HEAD_PAD, D_MODEL)[:, :V_DIM].reshape(ATTN_DIM, D_MODEL))

    build = dict(w_uq=heads(QK_DIM), w_uk=heads(QK_NOPE), w_uv=heads(V_DIM), p_pool=cols, p_attn=p_attn, w_out=rows)
    return jnp.concatenate([build[name](g[name]) for name in MIX_NAMES], axis=1)


def _per_head(fn, acc, *tables):
    return jnp.concatenate([fn(acc[:, h * HEAD_PAD:(h + 1) * HEAD_PAD], *tables) for h in range(N_HEADS)], axis=1)


def _rope_head(a, cos, sin):
    lane = lax.broadcasted_iota(jnp.int32, a.shape, 1)
    return a * (cos + jnp.where(lane < QK_NOPE, 1.0, 0.0)) + _rotate_half(a) * sin


def _layer_fwd(l, x, mod, get_weights, small, cos_t, sin_t):
    sh1, sc1, g1, sh2, sc2, g2 = mod
    tag = f"_l{l}"
    h, r1 = _norm_mod(x, small["ln1_g"], sc1, sh1, name="norm1" + tag)
    w = dict(get_weights("in", h))
    (z,) = _mm(h, w["w_in"], name="in_proj" + tag)
    p, yp, cq, ckv, kr, rq, rkv = _mixer_pre(z, cos_t, sin_t, small["w_pool"], small["pool_scale"],
                                              small["q_norm_g"], small["kv_norm_g"], name="mixer_pre" + tag)
    w.update(get_weights("mix", yp))
    (ya,) = _mm(yp, w["p_pool"], name="pool_out" + tag, out_dtypes=(BF16,))
    (q,) = _mm(cq, w["w_uq"], name="q_proj" + tag, out_dtypes=(BF16,),
               epilogue=lambda acc, cos, sin: (_per_head(_rope_head, acc, cos, sin),),
               extras=((cos_t, "table"), (sin_t, "table")))
    (k,) = _mm(ckv, w["w_uk"], name="k_proj" + tag, out_dtypes=(BF16,),
               epilogue=lambda acc, krv: (_per_head(lambda a, b: a + b, acc, krv),), extras=((kr, "table"),))
    (v,) = _mm(ckv, w["w_uv"], name="v_proj" + tag, out_dtypes=(BF16,))
    o, lse = _attn_fwd(q, k, v, name="attn_fwd" + tag)
    yb, merged = _mm(o, w["p_attn"], name="attn_out" + tag, out_dtypes=(BF16, BF16), tm=512,
                     epilogue=lambda acc, ga, gb, yav: (acc, _sigmoid(ga) * yav + _sigmoid(gb) * acc),
                     extras=((z, ("tile", ZC_GA // D_MODEL)), (z, ("tile", ZC_GB // D_MODEL)), (ya, "tile")))
    mo, x1 = _mm(merged, w["w_out"], name="mix_out" + tag, out_dtypes=(BF16, F32),
                 epilogue=lambda acc, xr, g: (acc, xr + g * acc), extras=((x, "tile"), (g1, "row")), tm=512)
    h2, r2 = _norm_mod(x1, small["ln2_g"], sc2, sh2, name="norm2" + tag)
    w.update(get_weights("ff1", merged))
    f, act = _mm(h2, w["w_ff1"], b_stack=True, name="ff1" + tag, out_dtypes=(BF16, BF16),
                 epilogue=lambda acc: (acc, jnp.square(jnp.maximum(acc, 0.0))))
    w.update(get_weights("ff2", act))
    m2, x2 = _mm(act, w["w_ff2"], name="ff2" + tag, out_dtypes=(BF16, F32),
                 epilogue=lambda acc, xr, g: (acc, xr + g * acc), extras=((x1, "tile"), (g2, "row")), tm=512)
    saved = dict(x=x, h=h, r1=r1, z=z, p=p, yp=yp, cq=cq, ckv=ckv, rq=rq, rkv=rkv, ya=ya, q=q, k=k, v=v, o=o, lse=lse,
                 yb=yb, merged=merged, mo=mo, x1=x1, h2=h2, r2=r2, f=f, act=act, m2=m2)
    return x2, saved, w


def _merge_grads(dm, ga, gb, ya, yb):
    sa, sb = _sigmoid(ga), _sigmoid(gb)
    return dm * sa, dm * sb, dm * ya * (sa * (1.0 - sa)), dm * yb * (sb * (1.0 - sb))


def _layer_bwd(l, dx2, dm2, dg2, sv, mod, w, small, cos_t, sin_t, send_grads, gate_below):
    sh1, sc1, g1, sh2, sc2, g2 = mod
    tag = f"_l{l}"
    gw = {}
    (df,) = _mm(dm2, w["w_ff2"], tb=True, name="ff2_dx" + tag, out_dtypes=(BF16,),
                epilogue=lambda acc, f: (acc * (2.0 * jnp.maximum(f, 0.0)),), extras=((sv["f"], "tile"),))
    (g_ff2,) = _mm(sv["act"], dm2, ta=True, name="ff2_dw" + tag, out_dtypes=(BF16,))
    (g_ff1,) = _mm(sv["h2"], df, ta=True, out_stack=N_CHIPS, name="ff1_dw" + tag, out_dtypes=(BF16,))
    sc2 = sc2 + send_grads("ff2", dict(w_ff2=g_ff2)) + send_grads("ff1", dict(w_ff1=g_ff1))
    (dh2,) = _mm(df, w["w_ff1"], tb=True, b_stack=True, name="ff1_dx" + tag)
    dx1, dln2, dsc2, dsh2, dmo, dg1 = _norm_mod_bwd(dh2, sv["x1"], sv["r2"], small["ln2_g"], sc2, dx2,
                                                    gate=(sv["mo"], g1), name="norm2_bwd" + tag)
    dya, dyb, dga, dgb = _mm(dmo, w["w_out"], tb=True, name="mix_out_dx" + tag, out_dtypes=(BF16,) * 4, tm=512,
                             epilogue=_merge_grads,
                             extras=((sv["z"], ("tile", ZC_GA // D_MODEL)), (sv["z"], ("tile", ZC_GB // D_MODEL)),
                                     (sv["ya"], "tile"), (sv["yb"], "tile")))
    (gw["w_out"],) = _mm(sv["merged"], dmo, ta=True, name="mix_out_dw" + tag, out_dtypes=(BF16,))
    (gw["p_pool"],) = _mm(sv["yp"], dya, ta=True, name="pool_out_dw" + tag, out_dtypes=(BF16,))
    (dyp,) = _mm(dya, w["p_pool"], tb=True, name="pool_out_dx" + tag)
    du, g_w_pool, g_pool_scale = _pool_bwd(dyp, sv["p"], small["w_pool"], small["pool_scale"], name="pool_bwd" + tag)
    (gw["p_attn"],) = _mm(sv["o"], dyb, ta=True, name="attn_out_dw" + tag, out_dtypes=(BF16,))
    (do,) = _mm(dyb, w["p_attn"], tb=True, name="attn_out_dx" + tag, out_dtypes=(BF16,))
    delta = _attn_delta(do, sv["o"], name="attn_delta" + tag)
    dq, dk, dv = _attn_bwd(sv["q"], sv["k"], sv["v"], do, sv["lse"], delta, name="attn_bwd" + tag)
    dql = _rope_bwd_q(dq, cos_t, sin_t, name="rope_bwd_q" + tag)
    dkb, dkr = _key_bwd(dk, cos_t, sin_t, name="key_bwd" + tag)
    (gw["w_uq"],) = _mm(sv["cq"], dql, ta=True, name="q_proj_dw" + tag, out_dtypes=(BF16,))
    (gw["w_uk"],) = _mm(sv["ckv"], dkb, ta=True, name="k_proj_dw" + tag, out_dtypes=(BF16,))
    (gw["w_uv"],) = _mm(sv["ckv"], dv, ta=True, name="v_proj_dw" + tag, out_dtypes=(BF16,))
    (dcq,) = _mm(dql, w["w_uq"], tb=True, name="q_proj_dx" + tag)
    (dckv,) = _mm(dkb, w["w_uk"], tb=True, second=(dv, w["w_uv"]), name="kv_proj_dx" + tag)
    q_norm_g = small["q_norm_g"] + send_grads("mix", gw)
    dcq_raw, g_qn = _rms_bwd(dcq, sv["z"], ZC_CQ, sv["rq"], q_norm_g, name="q_norm_bwd" + tag)
    dckv_raw, g_kvn = _rms_bwd(dckv, sv["z"], ZC_CKV, sv["rkv"], small["kv_norm_g"], name="kv_norm_bwd" + tag)
    dz = jnp.concatenate([dcq_raw, dkr, du, dga, dgb, dckv_raw], axis=1)
    (g_in,) = _mm(sv["h"], dz, ta=True, name="in_proj_dw" + tag, out_dtypes=(BF16,))
    sc1 = sc1 + send_grads("in", dict(w_in=g_in))
    (dh,) = _mm(dz, w["w_in"], tb=True, name="in_proj_dx" + tag)
    dx, dln1, dsc1, dsh1, *below = _norm_mod_bwd(dh, sv["x"], sv["r1"], small["ln1_g"], sc1, dx1, gate=gate_below,
                                                 name="norm1_bwd" + tag)
    dmod = jnp.concatenate([dsh1, dsc1, dg1, dsh2, dsc2, dg2], axis=0)
    gsmall = dict(ln1_g=dln1, ln2_g=dln2, q_norm_g=g_qn, kv_norm_g=g_kvn, w_pool=g_w_pool, pool_scale=g_pool_scale)
    return dx, dmod, gsmall, below


SMALL_LOSS = 6
SMALL_SINGLES = 16
SMALL_POOL = 24
SMALL_POOL_ROWS = len(POOL_WINDOWS) * POOL_GROUP * POOL_GROUP // D_MODEL
SMALL_ROWS = SMALL_POOL + DEPTH * SMALL_POOL_ROWS


def _pack_small(parts, *, name):
    def body(*refs):
        out_ref = refs[-1]
        out_ref[...] = jnp.zeros_like(out_ref)
        for ref, (_, row) in zip(refs[:-1], parts):
            out_ref[row:row + ref.shape[0], :] = ref[...]

    return pl.pallas_call(body, name=name, out_shape=jax.ShapeDtypeStruct((SMALL_ROWS, D_MODEL), F32),
                          compiler_params=_params())(*[a for a, _ in parts])


def kernel(x, c, positions, ln1_g, ln2_g, w_ada, b_ada, w_in, q_norm_g, w_uq, kv_norm_g, w_uk, w_uv, w_pool, pool_scale, p_pool, p_attn, w_out, w_ff1, w_ff2, final_g, loss_target, m_ln1_g, m_ln2_g, m_w_ada, m_b_ada, m_w_in, m_q_norm_g, m_w_uq, m_kv_norm_g, m_w_uk, m_w_uv, m_w_pool, m_pool_scale, m_p_pool, m_p_attn, m_w_out, m_w_ff1, m_w_ff2, m_final_g, v_ln1_g, v_ln2_g, v_w_ada, v_b_ada, v_w_in, v_q_norm_g, v_w_uq, v_kv_norm_g, v_w_uk, v_w_uv, v_w_pool, v_pool_scale, v_p_pool, v_p_attn, v_w_out, v_w_ff1, v_w_ff2, v_final_g):
    weights = dict(ln1_g=ln1_g, ln2_g=ln2_g, w_ada=w_ada, b_ada=b_ada, w_in=w_in, q_norm_g=q_norm_g, w_uq=w_uq,
                   kv_norm_g=kv_norm_g, w_uk=w_uk, w_uv=w_uv, w_pool=w_pool, pool_scale=pool_scale, p_pool=p_pool,
                   p_attn=p_attn, w_out=w_out, w_ff1=w_ff1, w_ff2=w_ff2, final_g=final_g)
    moms = dict(ln1_g=m_ln1_g, ln2_g=m_ln2_g, w_ada=m_w_ada, b_ada=m_b_ada, w_in=m_w_in, q_norm_g=m_q_norm_g,
                w_uq=m_w_uq, kv_norm_g=m_kv_norm_g, w_uk=m_w_uk, w_uv=m_w_uv, w_pool=m_w_pool,
                pool_scale=m_pool_scale, p_pool=m_p_pool, p_attn=m_p_attn, w_out=m_w_out, w_ff1=m_w_ff1,
                w_ff2=m_w_ff2, final_g=m_final_g)
    vels = dict(ln1_g=v_ln1_g, ln2_g=v_ln2_g, w_ada=v_w_ada, b_ada=v_b_ada, w_in=v_w_in, q_norm_g=v_q_norm_g,
                w_uq=v_w_uq, kv_norm_g=v_kv_norm_g, w_uk=v_w_uk, w_uv=v_w_uv, w_pool=v_w_pool,
                pool_scale=v_pool_scale, p_pool=v_p_pool, p_attn=v_p_attn, w_out=v_w_out, w_ff1=v_w_ff1,
                w_ff2=v_w_ff2, final_g=v_final_g)
    order = list(weights)
    seq = x.shape[1]
    my_chip = 2 * lax.axis_index("x") + lax.axis_index("y")
    my_dev = 2 * my_chip + lax.axis_index("c")
    ada_cols = w_ada.shape[2]

    small = [dict(ln1_g=ln1_g[l:l + 1], ln2_g=ln2_g[l:l + 1], q_norm_g=q_norm_g[l:l + 1], kv_norm_g=kv_norm_g[l:l + 1],
                  w_pool=w_pool[l], pool_scale=pool_scale[l:l + 1]) for l in range(DEPTH)]

    c_all = _all_gather_small(jnp.pad(c, ((0, 7), (0, 0))), name="cond_all_gather")
    c_act = _silu(c_all, name="cond_silu")
    b_mine = lax.dynamic_slice_in_dim(b_ada, my_chip * ada_cols, ada_cols, axis=1).reshape(1, DEPTH * ada_cols)
    (mod_cat,) = _mm(c_act, w_ada, b_stack=True, name="ada_fwd", epilogue=lambda acc, b: (acc + b,),
                     extras=((b_mine, "row"),))
    mod_mine = jnp.concatenate([mod_cat[::8, l * ada_cols:(l + 1) * ada_cols] for l in range(DEPTH)], axis=0)
    mod_all = _all_gather_small(mod_mine, name="mod_all_gather").reshape(N_DEV, DEPTH, N_DEV, ada_cols)

    zero = mod_all[0, 0, 0, 0] * 0.0
    started = {}
    for l in range(DEPTH):
        for group in GROUPS:
            started[l, group] = _exchange_start(_local_shard(weights, l, group, zero), name=f"weights_send_l{l}_{group}",
                                                scatter=False)
    pin = sum(st[4][0:1, 0:1] for st in started.values())

    def gathered_weights(l, group, after):
        mine, land = _exchange_wait(started[l, group], after, name=f"weights_wait_l{l}_{group}", scatter=False)
        land = lax.dynamic_update_slice_in_dim(land, mine[None], my_chip, axis=0)
        return _unpack_weights(land, group)

    mods = []
    for l in range(DEPTH):
        row = jnp.concatenate([lax.dynamic_index_in_dim(mod_all[2 * j, l], my_dev, axis=0, keepdims=True)
                               for j in range(N_CHIPS)], axis=1) + pin
        mods.append([row[:, i * D_MODEL:(i + 1) * D_MODEL] for i in range(N_MOD)])

    inv_freq = ROPE_THETA ** (-jnp.arange(0, QK_ROPE, 2, dtype=F32) / QK_ROPE)
    freq_lanes = jnp.concatenate([jnp.zeros((QK_NOPE,), F32), inv_freq, inv_freq,
                                  jnp.zeros((HEAD_PAD - QK_DIM,), F32)]).reshape(1, LANES)
    cos_t, sin_t = _rope_tables(positions.reshape(seq, 1), freq_lanes, name="rope_tables")

    xs, saved, wl = x.reshape(seq, D_MODEL), [], []
    for l in range(DEPTH):
        xs, sv, w_l = _layer_fwd(l, xs, mods[l], functools.partial(gathered_weights, l), small[l], cos_t, sin_t)
        saved.append(sv)
        wl.append(w_l)
    dx, loss_part, g_final, dm2, dg2 = _final_loss(xs, final_g.reshape(1, D_MODEL), loss_target.reshape(seq, D_MODEL),
                                                   saved[-1]["m2"], mods[-1][5], name="final_loss")

    sent = []

    def send_grads(l, group, g):
        gpack = _pack_grads(g, group)
        started_g = _exchange_start(gpack, name=f"grads_send_l{l}_{group}", scatter=True)
        sent.append((l, group, started_g))
        return started_g[4][0:1, 0:1]

    dmod, gsmall = [None] * DEPTH, [None] * DEPTH
    for l in reversed(range(DEPTH)):
        gate_below = (saved[l - 1]["m2"], mods[l - 1][5]) if l > 0 else None
        dx, dmod[l], gsmall[l], below = _layer_bwd(l, dx, dm2, dg2, saved[l], mods[l], wl[l], small[l], cos_t, sin_t,
                                                   functools.partial(send_grads, l), gate_below)
        dm2, dg2 = below if below else (None, None)
    grads = dict(x=dx.reshape(1, seq, D_MODEL))

    def lanes(a):
        flat = a.reshape(1, -1)
        return jnp.pad(flat, ((0, 0), (0, D_MODEL - flat.shape[1])))

    singles = [gsmall[0]["ln1_g"], gsmall[1]["ln1_g"], gsmall[0]["ln2_g"], gsmall[1]["ln2_g"], g_final,
               lanes(jnp.concatenate([gsmall[l]["pool_scale"] for l in range(DEPTH)], axis=1)),
               lanes(jnp.concatenate([gsmall[l]["q_norm_g"] for l in range(DEPTH)], axis=1)),
               lanes(jnp.concatenate([gsmall[l]["kv_norm_g"] for l in range(DEPTH)], axis=1))]
    parts = [(dmod[0], 0), (lanes(loss_part), SMALL_LOSS), (dmod[1], 8)]
    parts += [(a, SMALL_SINGLES + i) for i, a in enumerate(singles)]
    parts += [(gsmall[l]["w_pool"].reshape(-1, D_MODEL), SMALL_POOL + l * SMALL_POOL_ROWS) for l in range(DEPTH)]
    small_all = _all_gather_small(_pack_small(parts, name="small_grads_pack"), name="small_grads_all_gather")
    small_all = small_all.reshape(N_DEV, SMALL_ROWS, D_MODEL)
    ssum = _sum_slots(small_all, N_DEV, name="small_grads_sum")
    loss = ssum[SMALL_LOSS, 0]
    grads["b_ada"] = jnp.stack([ssum[8 * l:8 * l + N_MOD] for l in range(DEPTH)]).reshape(DEPTH, N_MOD * D_MODEL)
    grads["ln1_g"] = ssum[SMALL_SINGLES:SMALL_SINGLES + 2]
    grads["ln2_g"] = ssum[SMALL_SINGLES + 2:SMALL_SINGLES + 4]
    grads["final_g"] = ssum[SMALL_SINGLES + 4]
    grads["pool_scale"] = ssum[SMALL_SINGLES + 5].reshape(DEPTH, POOL_DIM)
    grads["q_norm_g"] = ssum[SMALL_SINGLES + 6, :DEPTH * Q_LORA].reshape(DEPTH, Q_LORA)
    grads["kv_norm_g"] = ssum[SMALL_SINGLES + 7, :DEPTH * KV_LORA].reshape(DEPTH, KV_LORA)
    grads["w_pool"] = ssum[SMALL_POOL:SMALL_ROWS].reshape(w_pool.shape)

    gsum, after = {}, ssum
    for l, group, started_g in sent:
        tg = f"_l{l}_{group}"
        gpack, land = _exchange_wait(started_g, after, name="grads_wait" + tg, scatter=True)
        own = lax.dynamic_index_in_dim(gpack, my_chip, axis=0, keepdims=True)
        land = lax.dynamic_update_slice_in_dim(land, own, my_chip, axis=0)
        part = _sum_slots(land, N_CHIPS, name="grads_sum_chips" + tg)
        other = _exchange_sibling(part, name="grads_swap_cores" + tg)
        gsum[group] = _add2_stacked(part, other, gsum.get(group), l, name="grads_sum_cores" + tg)
        after = gsum[group]
    grads.update(w_in=gsum["in"], w_ff1=gsum["ff1"], w_ff2=gsum["ff2"])
    off = 0
    for name in MIX_NAMES:
        grads[name] = gsum["mix"][:, off:off + ROWS_OF[name]].reshape(weights[name].shape)
        off += ROWS_OF[name]

    c_act_t = jnp.pad(c_act[::8].T, ((0, 0), (0, LANES - N_DEV)))
    d_mine = []
    for l in range(DEPTH):
        d_all = small_all[:, 8 * l:8 * l + N_MOD].reshape(N_DEV, N_MOD * D_MODEL)
        d_mine.append(lax.dynamic_slice_in_dim(d_all, my_chip * ada_cols, ada_cols, axis=1))
    d_cat = jnp.pad(jnp.concatenate(d_mine, axis=1), ((0, LANES - N_DEV), (0, 0)))
    (grads["w_ada"],) = _mm(c_act_t, d_cat, out_stack=DEPTH, name="ada_dw")

    def view(a):
        return a.reshape(1, -1) if a.ndim == 1 else a if a.ndim == 3 else a.reshape(-1, a.shape[-1])

    delta, new_m, new_v = {}, {}, {}
    for name in order:
        shape = weights[name].shape
        d, nm, nv = _adamw(view(weights[name]), view(grads[name]), view(moms[name]), view(vels[name]),
                           name="adamw_" + name)
        delta[name], new_m[name], new_v[name] = d.reshape(shape), nm.reshape(shape), nv.reshape(shape)
    return (loss, grads["x"], *[grads[n] for n in order], *[delta[n] for n in order],
            *[new_m[n] for n in order], *[new_v[n] for n in order])
```

```python
import functools
import math

import jax
import jax.numpy as jnp
from jax import lax
from jax.experimental import pallas as pl
from jax.experimental.pallas import tpu as pltpu

F32 = jnp.float32
BF16 = jnp.bfloat16
MESH = pl.DeviceIdType.MESH

D_MODEL = 1024
DEPTH = 2
POOL_WINDOWS = (2, 4, 8, 16)
POOL_GROUP = 128
POOL_DIM = 512
N_HEADS = 8
QK_NOPE = 64
QK_ROPE = 32
QK_DIM = QK_NOPE + QK_ROPE
V_DIM = 64
HEAD_PAD = 128
Q_LORA = 384
KV_LORA = 256
ROPE_THETA = 10000.0
ATTN_DIM = N_HEADS * V_DIM
D_FF = 4 * D_MODEL
N_MOD = 6
EPS = 1e-6
N_CHIPS = 4
N_DEV = 8

ADAM_LR = 0.001
ADAM_B1 = 0.9
ADAM_B2 = 0.999
ADAM_EPS = 1e-08
ADAM_WD = 0.01
ADAM_STEP = 10

VMEM_LIMIT_BYTES = 56 * 1024 * 1024
LANES = 128
HALO = 16

ZC_CQ = 0
ZC_KR = 384
ZC_U = 512
ZC_GA = 1024
ZC_GB = 2048
ZC_CKV = 3072
Z_DIM = 3328

PACK_ROWS = (("w_in", 808), ("w_uq", 72), ("w_uk", 32), ("w_uv", 32), ("p_pool", 128), ("p_attn", 128),
             ("w_out", 256), ("w_ff1", 1024), ("w_ff2", 1024))


def _params(sem=None, **kw):
    return pltpu.CompilerParams(dimension_semantics=sem, vmem_limit_bytes=VMEM_LIMIT_BYTES, **kw)


def _tile(n, target, unit=LANES):
    best = None
    for t in range(unit, min(n, target) + 1, unit):
        if n % t == 0:
            best = t
    return best if best is not None and 4 * best >= min(n, target) else n


def _near_tile(n, target):
    cands = [t for t in range(LANES, n + 1, LANES) if n % t == 0]
    return min(cands, key=lambda t: abs(math.log(t / target))) if cands else n


def _mm(a, b, *, name, ta=False, tb=False, out_dtypes=(F32,), epilogue=None, extras=(), tm=1024, tn=1024, tk=1024,
        second=None, b_stack=False, out_stack=None):
    (k_dim, m_dim) = a.shape if ta else a.shape[::-1]
    if b_stack:
        g_b, k_b, n_shard = b.shape
        n_dim, k_b = (k_b, g_b * n_shard) if tb else (g_b * n_shard, k_b)
    else:
        (n_dim, k_b) = b.shape if tb else b.shape[::-1]
    assert k_dim == k_b, (a.shape, b.shape)
    n_unit = n_shard if b_stack and not tb else n_dim // out_stack if out_stack else n_dim
    k_unit = n_shard if b_stack and tb else k_dim
    tm, tn, tk = _near_tile(m_dim, tm), _near_tile(n_unit, tn), _near_tile(k_unit, tk)
    nk = k_dim // tk
    n_extra, n_out = len(extras), len(out_dtypes)
    n_lhs = 4 if second else 2
    dims = (((0 if ta else 1,), (1 if tb else 0,)), ((), ()))
    if epilogue is None:
        epilogue = lambda acc: (acc,) * n_out

    def body(*refs):
        operand_refs, rest = refs[:n_lhs], refs[n_lhs:]
        extra_refs, out_refs = rest[:n_extra], rest[n_extra:n_extra + n_out]

        def product():
            total = None
            for a_ref, b_ref in zip(operand_refs[0::2], operand_refs[1::2]):
                part = lax.dot_general(a_ref[...].astype(BF16), b_ref[...].astype(BF16), dims, preferred_element_type=F32)
                total = part if total is None else total + part
            return total

        def finish(acc):
            outs = epilogue(acc, *[r[...] for r in extra_refs])
            for o_ref, o in zip(out_refs, outs):
                o_ref[...] = o.astype(o_ref.dtype)

        if nk == 1:
            finish(product())
            return
        acc_ref = rest[-1]
        k = pl.program_id(2)

        @pl.when(k == 0)
        def _():
            acc_ref[...] = product()

        @pl.when((k > 0) & (k < nk - 1))
        def _():
            acc_ref[...] += product()

        @pl.when(k == nk - 1)
        def _():
            finish(acc_ref[...] + product())

    a_spec = pl.BlockSpec((tk, tm), lambda i, j, k: (k, i)) if ta else pl.BlockSpec((tm, tk), lambda i, j, k: (i, k))
    if b_stack and tb:
        per = n_shard // tk
        b_spec = pl.BlockSpec((None, tn, tk), lambda i, j, k: (k // per, j, k % per))
    elif b_stack:
        per = n_shard // tn
        b_spec = pl.BlockSpec((None, tk, tn), lambda i, j, k: (j // per, k, j % per))
    elif tb:
        b_spec = pl.BlockSpec((tn, tk), lambda i, j, k: (j, k))
    else:
        b_spec = pl.BlockSpec((tk, tn), lambda i, j, k: (k, j))
    if out_stack:
        per_out = (n_dim // out_stack) // tn
        out_spec = pl.BlockSpec((None, tm, tn), lambda i, j, k: (j // per_out, i, j % per_out))
        out_dims = (out_stack, m_dim, n_dim // out_stack)
    else:
        out_spec = pl.BlockSpec((tm, tn), lambda i, j, k: (i, j))
        out_dims = (m_dim, n_dim)
    extra_specs = []
    for arr, kind in extras:
        if kind == "tile":
            extra_specs.append(pl.BlockSpec((tm, tn), lambda i, j, k: (i, j)))
        elif isinstance(kind, tuple):
            extra_specs.append(pl.BlockSpec((tm, tn), functools.partial(lambda i, j, k, c: (i, j + c), c=kind[1])))
        elif kind == "row":
            extra_specs.append(pl.BlockSpec((1, tn), lambda i, j, k: (0, j)))
        elif kind == "col":
            extra_specs.append(pl.BlockSpec((tm, 1), lambda i, j, k: (i, 0)))
        else:
            assert kind == "table", kind
            extra_specs.append(pl.BlockSpec((tm, LANES), lambda i, j, k: (i, 0)))
    return pl.pallas_call(
        body,
        name=name,
        grid=(m_dim // tm, n_dim // tn, nk),
        in_specs=[a_spec, b_spec] * (n_lhs // 2) + extra_specs,
        out_specs=[out_spec for _ in out_dtypes],
        out_shape=[jax.ShapeDtypeStruct(out_dims, dt) for dt in out_dtypes],
        scratch_shapes=[pltpu.VMEM((tm, tn), F32)] if nk > 1 else [],
        compiler_params=_params(("parallel", "parallel", "arbitrary")),
    )(a, b, *(second or ()), *[arr for arr, _ in extras])


def _rows(s):
    return min(512, s)


def _rope_tables(pos_col, inv_freq_lanes, *, name):
    s = pos_col.shape[0]
    tb = _rows(s)

    def body(pos_ref, f_ref, cos_ref, sin_ref):
        ang = pos_ref[...].astype(F32) * f_ref[...]
        lane = lax.broadcasted_iota(jnp.int32, ang.shape, 1)
        on = (lane >= QK_NOPE) & (lane < QK_DIM)
        cos_ref[...] = jnp.where(on, jnp.cos(ang), 0.0)
        sin_ref[...] = jnp.where(on, jnp.sin(ang), 0.0)

    return pl.pallas_call(
        body, name=name, grid=(s // tb,),
        in_specs=[pl.BlockSpec((tb, 1), lambda i: (i, 0)), pl.BlockSpec((1, LANES), lambda i: (0, 0))],
        out_specs=[pl.BlockSpec((tb, LANES), lambda i: (i, 0))] * 2,
        out_shape=[jax.ShapeDtypeStruct((s, LANES), F32)] * 2,
        compiler_params=_params(("parallel",)),
    )(pos_col, inv_freq_lanes)


def _rotate_half(x):
    lane = lax.broadcasted_iota(jnp.int32, x.shape, 1)
    half = QK_ROPE // 2
    first = (lane >= QK_NOPE) & (lane < QK_NOPE + half)
    second = (lane >= QK_NOPE + half) & (lane < QK_DIM)
    return jnp.where(first, -pltpu.roll(x, LANES - half, 1), jnp.where(second, pltpu.roll(x, half, 1), 0.0))


def _norm_mod(x, g, sc, sh, *, name):
    s, d = x.shape
    tb = _rows(s)

    def body(x_ref, g_ref, sc_ref, sh_ref, h_ref, r_ref):
        xv = x_ref[...]
        r = lax.rsqrt(jnp.mean(xv * xv, axis=-1, keepdims=True) + EPS)
        r_ref[...] = r
        h_ref[...] = (((xv * r) * g_ref[...]) * (1.0 + sc_ref[...]) + sh_ref[...]).astype(BF16)

    vec = pl.BlockSpec((1, d), lambda i: (0, 0))
    return pl.pallas_call(
        body, name=name, grid=(s // tb,),
        in_specs=[pl.BlockSpec((tb, d), lambda i: (i, 0)), vec, vec, vec],
        out_specs=[pl.BlockSpec((tb, d), lambda i: (i, 0)), pl.BlockSpec((tb, 1), lambda i: (i, 0))],
        out_shape=[jax.ShapeDtypeStruct((s, d), BF16), jax.ShapeDtypeStruct((s, 1), F32)],
        compiler_params=_params(("parallel",)),
    )(x, g, sc, sh)


def _window_sums(ext, sign):
    n = ext.shape[0]
    sums, cur, k = [], ext, 1
    for _ in POOL_WINDOWS:
        cur = cur + pltpu.roll(cur, k if sign > 0 else n - k, 0)
        sums.append(cur)
        k *= 2
    return sums


def _mixer_pre(z, cos_t, sin_t, w_pool, pool_scale, gq, gkv, *, name):
    s = z.shape[0]
    tb = _rows(s)
    hb = tb // HALO

    def body(zcq_ref, zkr_ref, zu_ref, zuh_ref, zckv_ref, cos_ref, sin_ref, wp_ref, ps_ref, gq_ref, gkv_ref,
             p_ref, yp_ref, cq_ref, ckv_ref, kr_ref, rq_ref, rkv_ref):
        i = pl.program_id(0)
        u = zu_ref[...].astype(F32)
        halo = jnp.where(i > 0, zuh_ref[...].astype(F32), 0.0)
        ext = jnp.concatenate([halo, u], axis=0)
        t = i * tb + lax.broadcasted_iota(jnp.int32, (tb, 1), 0)
        for g, (w, sw) in enumerate(zip(POOL_WINDOWS, _window_sums(ext, +1))):
            cols = slice(g * POOL_GROUP, (g + 1) * POOL_GROUP)
            cnt = jnp.minimum(t + 1, w).astype(F32)
            pg = (sw[HALO:, cols] / cnt - u[:, cols]).astype(BF16)
            p_ref[:, cols] = pg
            yg = jnp.dot(pg, wp_ref[g].astype(BF16), preferred_element_type=F32)
            yp_ref[:, cols] = (yg * ps_ref[:, cols]).astype(BF16)

        def rms(x_ref, g_ref, out_ref, r_ref):
            xv = x_ref[...].astype(F32)
            r = lax.rsqrt(jnp.mean(xv * xv, axis=-1, keepdims=True) + EPS)
            r_ref[...] = r
            out_ref[...] = ((xv * r) * g_ref[...]).astype(BF16)

        rms(zcq_ref, gq_ref, cq_ref, rq_ref)
        rms(zckv_ref, gkv_ref, ckv_ref, rkv_ref)
        kr = zkr_ref[...].astype(F32)
        kr_ref[...] = (kr * cos_ref[...] + _rotate_half(kr) * sin_ref[...]).astype(BF16)

    def zcol(width, off):
        return pl.BlockSpec((tb, width), lambda i: (i, off // width))

    def full(a):
        return pl.BlockSpec(a.shape, lambda i: (0,) * a.ndim)

    def out(width, dt):
        return pl.BlockSpec((tb, width), lambda i: (i, 0)), jax.ShapeDtypeStruct((s, width), dt)

    outs = [out(POOL_DIM, BF16), out(POOL_DIM, BF16), out(Q_LORA, BF16), out(KV_LORA, BF16), out(LANES, BF16),
            out(1, F32), out(1, F32)]
    return pl.pallas_call(
        body, name=name, grid=(s // tb,),
        in_specs=[zcol(Q_LORA, ZC_CQ), zcol(LANES, ZC_KR), zcol(POOL_DIM, ZC_U),
                  pl.BlockSpec((HALO, POOL_DIM), lambda i: (jnp.maximum(i * hb - 1, 0), ZC_U // POOL_DIM)),
                  zcol(KV_LORA, ZC_CKV),
                  pl.BlockSpec((tb, LANES), lambda i: (i, 0)), pl.BlockSpec((tb, LANES), lambda i: (i, 0)),
                  full(w_pool), full(pool_scale), full(gq), full(gkv)],
        out_specs=[o[0] for o in outs], out_shape=[o[1] for o in outs],
        compiler_params=_params(("parallel",)),
    )(z, z, z, z, z, cos_t, sin_t, w_pool, pool_scale, gq, gkv)


def _sigmoid(x):
    return 1.0 / (1.0 + jnp.exp(-x.astype(F32)))


ATTN_SCALE = 1.0 / math.sqrt(QK_DIM)
NEG_BIG = -1e30


LOG2_E = math.log2(math.e)
EXP2_SCALE = ATTN_SCALE * LOG2_E
NT_DIMS = (((1,), (1,)), ((), ()))
TN_DIMS = (((0,), (0,)), ((), ()))


def _on_or_below_diagonal(t):
    return lax.broadcasted_iota(jnp.int32, (t, t), 0) >= lax.broadcasted_iota(jnp.int32, (t, t), 1)


HEADS_PER_STEP = 2
HEAD_COLS = [slice(g * HEAD_PAD, (g + 1) * HEAD_PAD) for g in range(HEADS_PER_STEP)]


def _attn_fwd(q, k, v, *, name):
    s = q.shape[0]
    t = _rows(s)
    wide = HEADS_PER_STEP * HEAD_PAD

    def body(q_ref, k_ref, v_ref, o_ref, lse_ref):
        qi = pl.program_id(1)
        qs = [q_ref[:, cols] for cols in HEAD_COLS]

        def block(j, carry, diagonal):
            rows = pl.ds(pl.multiple_of(j * t, t), t)
            out = []
            for qv, cols, (m, l, acc) in zip(qs, HEAD_COLS, carry):
                sc = lax.dot_general(qv, k_ref[rows, cols], NT_DIMS, preferred_element_type=F32)
                if diagonal:
                    sc = jnp.where(_on_or_below_diagonal(t), sc, NEG_BIG)
                m_new = jnp.maximum(m, jnp.max(sc, axis=-1, keepdims=True))
                p = jnp.exp2((sc - m_new) * EXP2_SCALE)
                alpha = jnp.exp2((m - m_new) * EXP2_SCALE)
                l = alpha * l + jnp.sum(p, axis=-1, keepdims=True)
                acc = alpha * acc + jnp.dot(p.astype(BF16), v_ref[rows, cols], preferred_element_type=F32)
                out.append((m_new, l, acc))
            return tuple(out)

        init = tuple((jnp.full((t, 1), -jnp.inf, F32), jnp.zeros((t, 1), F32), jnp.zeros((t, HEAD_PAD), F32))
                     for _ in HEAD_COLS)
        carry = lax.fori_loop(0, qi, lambda j, c: block(j, c, False), init)
        for g, (cols, (m, l, acc)) in enumerate(zip(HEAD_COLS, block(qi, carry, True))):
            o_ref[:, cols] = (acc / l).astype(BF16)
            lse_ref[g] = m * ATTN_SCALE + jnp.log(l)

    q_spec = pl.BlockSpec((t, wide), lambda h, i: (i, h))
    kv_spec = pl.BlockSpec((s, wide), lambda h, i: (0, h))
    return pl.pallas_call(
        body, name=name, grid=(N_HEADS // HEADS_PER_STEP, s // t),
        in_specs=[q_spec, kv_spec, kv_spec],
        out_specs=[q_spec, pl.BlockSpec((HEADS_PER_STEP, t, 1), lambda h, i: (h, i, 0))],
        out_shape=[jax.ShapeDtypeStruct((s, N_HEADS * HEAD_PAD), BF16), jax.ShapeDtypeStruct((N_HEADS, s, 1), F32)],
        compiler_params=_params(("parallel", "parallel")),
    )(q, k, v)


def _attn_delta(do, o, *, name):
    s = o.shape[0]
    t = _rows(s)

    def body(do_ref, o_ref, out_ref):
        for h in range(N_HEADS):
            cols = slice(h * HEAD_PAD, (h + 1) * HEAD_PAD)
            out_ref[h] = jnp.sum(do_ref[:, cols].astype(F32) * o_ref[:, cols].astype(F32), axis=-1, keepdims=True)

    blk = pl.BlockSpec((t, N_HEADS * HEAD_PAD), lambda i: (i, 0))
    return pl.pallas_call(
        body, name=name, grid=(s // t,), in_specs=[blk, blk],
        out_specs=pl.BlockSpec((N_HEADS, t, 1), lambda i: (0, i, 0)),
        out_shape=jax.ShapeDtypeStruct((N_HEADS, s, 1), F32),
        compiler_params=_params(("parallel",)),
    )(do, o)


def _attn_bwd(q, k, v, do, lse, delta, *, name):
    s = q.shape[0]
    t = _rows(s)
    nt = s // t

    def body(q_ref, k_ref, v_ref, do_ref, lse_ref, dl_ref, dq_ref, dk_ref, dv_ref):
        kj = pl.program_id(1)

        @pl.when(kj == 0)
        def _():
            dq_ref[...] = jnp.zeros_like(dq_ref)

        kvs = [(k_ref[:, cols], v_ref[:, cols]) for cols in HEAD_COLS]

        def block(i, carry, diagonal):
            rows = pl.ds(pl.multiple_of(i * t, t), t)
            out = []
            for g, (cols, (kv, vv), (dk, dv)) in enumerate(zip(HEAD_COLS, kvs, carry)):
                qv, dov = q_ref[rows, cols], do_ref[rows, cols]
                sc = lax.dot_general(qv, kv, NT_DIMS, preferred_element_type=F32)
                p = jnp.exp2(sc * EXP2_SCALE - lse_ref[g, rows, :] * LOG2_E)
                if diagonal:
                    p = jnp.where(_on_or_below_diagonal(t), p, 0.0)
                dp = lax.dot_general(dov, vv, NT_DIMS, preferred_element_type=F32)
                ds = (p * (dp - dl_ref[g, rows, :])).astype(BF16)
                dv = dv + lax.dot_general(p.astype(BF16), dov, TN_DIMS, preferred_element_type=F32)
                dk = dk + lax.dot_general(ds, qv, TN_DIMS, preferred_element_type=F32)
                dq_ref[rows, cols] += jnp.dot(ds, kv, preferred_element_type=F32) * ATTN_SCALE
                out.append((dk, dv))
            return tuple(out)

        zero = jnp.zeros((t, HEAD_PAD), F32)
        carry = block(kj, tuple((zero, zero) for _ in HEAD_COLS), True)
        for cols, (dk, dv) in zip(HEAD_COLS, lax.fori_loop(kj + 1, nt, lambda i, c: block(i, c, False), carry)):
            dk_ref[:, cols] = dk * ATTN_SCALE
            dv_ref[:, cols] = dv.astype(BF16)

    full_spec = pl.BlockSpec((s, HEADS_PER_STEP * HEAD_PAD), lambda h, j: (0, h))
    kv_spec = pl.BlockSpec((t, HEADS_PER_STEP * HEAD_PAD), lambda h, j: (j, h))
    vec_spec = pl.BlockSpec((HEADS_PER_STEP, s, 1), lambda h, j: (h, 0, 0))
    wide = (s, N_HEADS * HEAD_PAD)
    return pl.pallas_call(
        body, name=name, grid=(N_HEADS // HEADS_PER_STEP, nt),
        in_specs=[full_spec, kv_spec, kv_spec, full_spec, vec_spec, vec_spec],
        out_specs=[full_spec, kv_spec, kv_spec],
        out_shape=[jax.ShapeDtypeStruct(wide, F32), jax.ShapeDtypeStruct(wide, F32), jax.ShapeDtypeStruct(wide, BF16)],
        compiler_params=_params(("parallel", "arbitrary")),
    )(q, k, v, do, lse, delta)


def _acc_specs(widths):
    return ([pl.BlockSpec((1, w), lambda i: (0, 0)) for w in widths],
            [jax.ShapeDtypeStruct((1, w), F32) for w in widths])


def _gate_grads(dxv, m_ref, gate_ref, dm_ref, dgate_ref):
    dm_ref[...] = (dxv * gate_ref[...]).astype(BF16)
    dgate_ref[...] += jnp.sum(dxv * m_ref[...], axis=0, keepdims=True)


def _final_loss(x, g, target, m, gate, *, name):
    s, d = x.shape
    tb = _rows(s)

    def body(x_ref, g_ref, t_ref, m_ref, gate_ref, dx_ref, loss_ref, dg_ref, dm_ref, dgate_ref):
        @pl.when(pl.program_id(0) == 0)
        def _():
            loss_ref[...] = jnp.zeros_like(loss_ref)
            dg_ref[...] = jnp.zeros_like(dg_ref)
            dgate_ref[...] = jnp.zeros_like(dgate_ref)

        xv = x_ref[...]
        r = lax.rsqrt(jnp.mean(xv * xv, axis=-1, keepdims=True) + EPS)
        xn = xv * r
        err = xn * g_ref[...] - t_ref[...]
        loss_ref[...] += 0.5 * jnp.sum(jnp.mean(err * err, axis=-1, keepdims=True), axis=0, keepdims=True)
        dy = err / d
        dg_ref[...] += jnp.sum(dy * xn, axis=0, keepdims=True)
        dxn = dy * g_ref[...]
        dxv = r * (dxn - xn * jnp.mean(dxn * xn, axis=-1, keepdims=True))
        dx_ref[...] = dxv
        _gate_grads(dxv, m_ref, gate_ref, dm_ref, dgate_ref)

    blk = pl.BlockSpec((tb, d), lambda i: (i, 0))
    vec = pl.BlockSpec((1, d), lambda i: (0, 0))
    acc_specs, acc_shapes = _acc_specs((LANES, d))
    return pl.pallas_call(
        body, name=name, grid=(s // tb,),
        in_specs=[blk, vec, blk, blk, vec],
        out_specs=[blk] + acc_specs + [blk, vec],
        out_shape=[jax.ShapeDtypeStruct((s, d), F32)] + acc_shapes + [jax.ShapeDtypeStruct((s, d), BF16),
                                                                     jax.ShapeDtypeStruct((1, d), F32)],
        compiler_params=_params(("arbitrary",)),
    )(x, g, target, m, gate)


def _norm_mod_bwd(dh, x, r, g, sc, dx_skip, *, name, gate=None):
    s, d = x.shape
    tb = _rows(s)
    nb = s // tb
    n_gate = 2 if gate else 0

    def body(dh_ref, x_ref, r_ref, g_ref, sc_ref, skip_ref, *rest):
        gate_refs, (dx_ref, dg_ref, dsc_ref, dsh_ref) = rest[:n_gate], rest[n_gate:n_gate + 4]
        gate_outs, da_sc = rest[n_gate + 4:-1], rest[-1]
        i = pl.program_id(0)

        @pl.when(i == 0)
        def _():
            da_sc[...] = jnp.zeros_like(da_sc)
            dsh_ref[...] = jnp.zeros_like(dsh_ref)
            if gate:
                gate_outs[1][...] = jnp.zeros_like(gate_outs[1])

        dhv, rv = dh_ref[...], r_ref[...]
        xn = x_ref[...] * rv
        dsh_ref[...] += jnp.sum(dhv, axis=0, keepdims=True)
        da_sc[...] += jnp.sum(dhv * xn, axis=0, keepdims=True)
        dxn = dhv * (g_ref[...] * (1.0 + sc_ref[...]))
        dxv = skip_ref[...] + rv * (dxn - xn * jnp.mean(dxn * xn, axis=-1, keepdims=True))
        dx_ref[...] = dxv
        if gate:
            _gate_grads(dxv, *gate_refs, *gate_outs)

        @pl.when(i == nb - 1)
        def _():
            dg_ref[...] = da_sc[...] * (1.0 + sc_ref[...])
            dsc_ref[...] = da_sc[...] * g_ref[...]

    blk = pl.BlockSpec((tb, d), lambda i: (i, 0))
    vec = pl.BlockSpec((1, d), lambda i: (0, 0))
    acc_specs, acc_shapes = _acc_specs((d, d, d))
    gate_specs = [blk, vec] if gate else []
    gate_shapes = [jax.ShapeDtypeStruct((s, d), BF16), jax.ShapeDtypeStruct((1, d), F32)] if gate else []
    return pl.pallas_call(
        body, name=name, grid=(nb,),
        in_specs=[blk, blk, pl.BlockSpec((tb, 1), lambda i: (i, 0)), vec, vec, blk] + gate_specs,
        out_specs=[blk] + acc_specs + gate_specs,
        out_shape=[jax.ShapeDtypeStruct((s, d), F32)] + acc_shapes + gate_shapes,
        scratch_shapes=[pltpu.VMEM((1, d), F32)],
        compiler_params=_params(("arbitrary",)),
    )(dh, x, r, g, sc, dx_skip, *(gate or ()))


def _pool_bwd(dyp, p, w_pool, pool_scale, *, name):
    s = dyp.shape[0]
    tb = _rows(s)
    nb = s // tb
    hb = tb // HALO
    nt_dims = (((1,), (1,)), ((), ()))
    tn_dims = (((0,), (0,)), ((), ()))

    def body(dy_ref, dyn_ref, p_ref, wp_ref, ps_ref, du_ref, gwp_ref, gps_ref):
        i = pl.program_id(0)

        @pl.when(i == 0)
        def _():
            gwp_ref[...] = jnp.zeros_like(gwp_ref)
            gps_ref[...] = jnp.zeros_like(gps_ref)

        cur = dy_ref[...]
        nxt = jnp.where(i < nb - 1, dyn_ref[...], 0.0)
        dpw = (jnp.concatenate([cur, nxt], axis=0) * ps_ref[...]).astype(BF16)
        t = i * tb + lax.broadcasted_iota(jnp.int32, (tb + HALO, 1), 0)
        for g, w in enumerate(POOL_WINDOWS):
            cols = slice(g * POOL_GROUP, (g + 1) * POOL_GROUP)
            wg = wp_ref[g].astype(BF16)
            dp = lax.dot_general(dpw[:, cols], wg, nt_dims, preferred_element_type=F32)
            e = dp / jnp.minimum(t + 1, w).astype(F32)
            lead = _window_sums(e, -1)[g]
            du_ref[:, cols] = (lead[:tb] - dp[:tb]).astype(BF16)
            pg = p_ref[:, cols]
            pw = jnp.dot(pg, wg, preferred_element_type=F32)
            gps_ref[:, cols] += jnp.sum(cur[:, cols] * pw, axis=0, keepdims=True)
            gwp_ref[g] += lax.dot_general(pg, dpw[:tb, cols], tn_dims, preferred_element_type=F32)

    blk = pl.BlockSpec((tb, POOL_DIM), lambda i: (i, 0))
    return pl.pallas_call(
        body, name=name, grid=(nb,),
        in_specs=[blk, pl.BlockSpec((HALO, POOL_DIM), lambda i: (jnp.minimum((i + 1) * hb, s // HALO - 1), 0)), blk,
                  pl.BlockSpec(w_pool.shape, lambda i: (0, 0, 0)), pl.BlockSpec((1, POOL_DIM), lambda i: (0, 0))],
        out_specs=[blk, pl.BlockSpec(w_pool.shape, lambda i: (0, 0, 0)), pl.BlockSpec((1, POOL_DIM), lambda i: (0, 0))],
        out_shape=[jax.ShapeDtypeStruct((s, POOL_DIM), BF16), jax.ShapeDtypeStruct(w_pool.shape, F32),
                   jax.ShapeDtypeStruct((1, POOL_DIM), F32)],
        compiler_params=_params(("arbitrary",)),
    )(dyp, dyp, p, w_pool, pool_scale)


def _rope_bwd_q(dq, cos_t, sin_t, *, name):
    s = dq.shape[0]
    tb = _rows(s)

    def body(dq_ref, cos_ref, sin_ref, out_ref):
        sin = sin_ref[...]
        lane = lax.broadcasted_iota(jnp.int32, sin.shape, 1)
        cos_q = cos_ref[...] + jnp.where(lane < QK_NOPE, 1.0, 0.0)
        for h in range(N_HEADS):
            cols = slice(h * HEAD_PAD, (h + 1) * HEAD_PAD)
            dqv = dq_ref[:, cols]
            out_ref[:, cols] = (dqv * cos_q - _rotate_half(dqv * sin)).astype(BF16)

    blk = pl.BlockSpec((tb, N_HEADS * HEAD_PAD), lambda i: (i, 0))
    tab = pl.BlockSpec((tb, LANES), lambda i: (i, 0))
    return pl.pallas_call(
        body, name=name, grid=(s // tb,), in_specs=[blk, tab, tab], out_specs=blk,
        out_shape=jax.ShapeDtypeStruct(dq.shape, BF16),
        compiler_params=_params(("parallel",)),
    )(dq, cos_t, sin_t)


def _key_bwd(dk, cos_t, sin_t, *, name):
    s = dk.shape[0]
    tb = _rows(s)

    def body(dk_ref, cos_ref, sin_ref, dkb_ref, dkr_ref):
        dkv = dk_ref[...]
        dkb_ref[...] = dkv.astype(BF16)
        tot = dkv[:, :HEAD_PAD]
        for h in range(1, N_HEADS):
            tot = tot + dkv[:, h * HEAD_PAD:(h + 1) * HEAD_PAD]
        dkr_ref[...] = (tot * cos_ref[...] - _rotate_half(tot * sin_ref[...])).astype(BF16)

    blk = pl.BlockSpec((tb, N_HEADS * HEAD_PAD), lambda i: (i, 0))
    tab = pl.BlockSpec((tb, LANES), lambda i: (i, 0))
    return pl.pallas_call(
        body, name=name, grid=(s // tb,), in_specs=[blk, tab, tab], out_specs=[blk, tab],
        out_shape=[jax.ShapeDtypeStruct(dk.shape, BF16), jax.ShapeDtypeStruct((s, LANES), BF16)],
        compiler_params=_params(("parallel",)),
    )(dk, cos_t, sin_t)


def _rms_bwd(dy, z, z_off, r, g, *, name):
    s, n = dy.shape
    tb = _rows(s)

    def body(dy_ref, x_ref, r_ref, g_ref, dx_ref, dg_ref):
        @pl.when(pl.program_id(0) == 0)
        def _():
            dg_ref[...] = jnp.zeros_like(dg_ref)

        dyv, rv = dy_ref[...], r_ref[...]
        xn = x_ref[...].astype(F32) * rv
        dg_ref[...] += jnp.sum(dyv * xn, axis=0, keepdims=True)
        dxn = dyv * g_ref[...]
        dx_ref[...] = (rv * (dxn - xn * jnp.mean(dxn * xn, axis=-1, keepdims=True))).astype(BF16)

    blk = pl.BlockSpec((tb, n), lambda i: (i, 0))
    acc_specs, acc_shapes = _acc_specs((n,))
    return pl.pallas_call(
        body, name=name, grid=(s // tb,),
        in_specs=[blk, pl.BlockSpec((tb, n), lambda i: (i, z_off // n)), pl.BlockSpec((tb, 1), lambda i: (i, 0)),
                  pl.BlockSpec((1, n), lambda i: (0, 0))],
        out_specs=[blk] + acc_specs, out_shape=[jax.ShapeDtypeStruct((s, n), BF16)] + acc_shapes,
        compiler_params=_params(("arbitrary",)),
    )(dy, z, r, g)


def _silu(c, *, name):
    def body(c_ref, out_ref):
        cv = c_ref[...]
        out_ref[...] = (cv * _sigmoid(cv)).astype(BF16)

    return pl.pallas_call(body, name=name, out_shape=jax.ShapeDtypeStruct(c.shape, BF16),
                          compiler_params=_params())(c)


def _sum_slots(a, n, *, name, out_dtype=F32):
    _, rows, cols = a.shape
    tr = _tile(rows, 256, 8)

    def body(a_ref, out_ref):
        tot = a_ref[0].astype(F32)
        for j in range(1, n):
            tot = tot + a_ref[j].astype(F32)
        out_ref[...] = tot.astype(out_dtype)

    return pl.pallas_call(
        body, name=name, grid=(rows // tr,),
        in_specs=[pl.BlockSpec((n, tr, cols), lambda i: (0, i, 0))],
        out_specs=pl.BlockSpec((tr, cols), lambda i: (i, 0)),
        out_shape=jax.ShapeDtypeStruct((rows, cols), out_dtype),
        compiler_params=_params(("parallel",)),
    )(a)


def _add2_stacked(a, b, stacked, l, *, name):
    rows, cols = a.shape
    tr = _tile(rows, 256, 8)

    def body(a_ref, b_ref, *rest):
        rest[-1][...] = a_ref[...] + b_ref[...]

    blk = pl.BlockSpec((tr, cols), lambda i: (i, 0))
    carried = [] if stacked is None else [stacked]
    return pl.pallas_call(
        body, name=name, grid=(rows // tr,),
        in_specs=[blk, blk] + [pl.BlockSpec(memory_space=pl.ANY) for _ in carried],
        out_specs=pl.BlockSpec((None, tr, cols), lambda i: (l, i, 0)),
        out_shape=jax.ShapeDtypeStruct((DEPTH, rows, cols), F32),
        input_output_aliases={2: 0} if carried else {},
        compiler_params=_params(("parallel",)),
    )(a, b, *carried)


def _adamw(w, g, m, v, *, name):
    shape = w.shape
    if w.ndim == 2:
        w, g, m, v = (a.reshape((1,) + shape) for a in (w, g, m, v))
    layers, rows, cols = w.shape
    tr = _tile(rows, max(8, (1 << 18) // cols), 8)
    c1 = 1.0 - ADAM_B1 ** ADAM_STEP
    c2 = 1.0 - ADAM_B2 ** ADAM_STEP

    def body(w_ref, g_ref, m_ref, v_ref, d_ref, nm_ref, nv_ref):
        gv = g_ref[...]
        nm = ADAM_B1 * m_ref[...] + (1.0 - ADAM_B1) * gv
        nv = ADAM_B2 * v_ref[...] + (1.0 - ADAM_B2) * (gv * gv)
        nm_ref[...] = nm
        nv_ref[...] = nv
        d_ref[...] = -ADAM_LR * ((nm / c1) / (jnp.sqrt(nv / c2) + ADAM_EPS) + ADAM_WD * w_ref[...])

    blk = pl.BlockSpec((None, tr, cols), lambda l, i: (l, i, 0))
    outs = pl.pallas_call(
        body, name=name, grid=(layers, rows // tr), in_specs=[blk] * 4, out_specs=[blk] * 3,
        out_shape=[jax.ShapeDtypeStruct((layers, rows, cols), F32)] * 3,
        compiler_params=_params(("parallel", "parallel")),
    )(w, g, m, v)
    return [o.reshape(shape) for o in outs]


def _coords():
    return lax.axis_index("x"), lax.axis_index("y"), lax.axis_index("c")


def _other_chips(x, y):
    return [(1 - x, y), (x, 1 - y), (1 - x, 1 - y)]


def _all_gather_small(blk, *, name):
    m_per, n = blk.shape

    def body(x_ref, out_ref, send_sems, recv_sems, local_sem):
        x, y, c = _coords()
        me, sibling = (x, y, c), (x, y, 1 - c)
        chips = _other_chips(x, y)

        def rows(px, py, pc):
            return out_ref.at[pl.ds((4 * px + 2 * py + pc) * m_per, m_per), :]

        def copy(k, block, to, src=None):
            return pltpu.make_async_remote_copy(
                src_ref=rows(*block) if src is None else src, dst_ref=rows(*block),
                send_sem=send_sems.at[k], recv_sem=recv_sems.at[k], device_id=to, device_id_type=MESH)

        mine = pltpu.make_async_copy(x_ref, rows(*me), local_sem)
        mine.start()
        first = [copy(0, me, sibling, src=x_ref)]
        first += [copy(1 + j, me, (*chip, c), src=x_ref) for j, chip in enumerate(chips)]
        for cp in first:
            cp.start()
        passed = [copy(4 + j, (*chip, c), sibling) for j, chip in enumerate(chips)]
        for j, chip in enumerate(chips):
            copy(1 + j, (*chip, c), me).wait_recv()
            passed[j].start()
        copy(0, sibling, me).wait_recv()
        for j, chip in enumerate(chips):
            copy(4 + j, (*chip, 1 - c), me).wait_recv()
        for cp in first + passed:
            cp.wait_send()
        mine.wait()

    return pl.pallas_call(
        body, name=name,
        out_shape=jax.ShapeDtypeStruct((N_DEV * m_per, n), blk.dtype),
        in_specs=[pl.BlockSpec(memory_space=pltpu.VMEM)],
        out_specs=pl.BlockSpec(memory_space=pltpu.VMEM),
        scratch_shapes=[pltpu.SemaphoreType.DMA((7,)), pltpu.SemaphoreType.DMA((7,)), pltpu.SemaphoreType.DMA],
        compiler_params=_params(),
    )(blk)


HBM_SPEC = pl.BlockSpec(memory_space=pltpu.HBM)
SEM_SPEC = pl.BlockSpec(memory_space=pltpu.SEMAPHORE)
DATAFLOW = pltpu.SideEffectType.DATAFLOW_SIDE_EFFECTING


def _chip_copies(src_ref, land_ref, send_sems, recv_sems, scatter):
    x, y, c = _coords()
    my = 2 * x + y
    outgoing, incoming = [], []
    for k, (px, py) in enumerate(_other_chips(x, y)):
        peer = 2 * px + py

        def copy(src_slot, dst_slot):
            return pltpu.make_async_remote_copy(
                src_ref=src_ref.at[src_slot] if scatter else src_ref, dst_ref=land_ref.at[dst_slot],
                send_sem=send_sems.at[k], recv_sem=recv_sems.at[k], device_id=(px, py, c), device_id_type=MESH)

        outgoing.append(copy(peer, my))
        incoming.append(copy(my, peer))
    return outgoing, incoming


def _exchange_start(src, *, name, scatter):
    land_shape = src.shape if scatter else (N_CHIPS,) + src.shape

    def body(src_ref, land_ref, send_sems, recv_sems, src_thru, land_thru, token):
        outgoing, _ = _chip_copies(src_ref, land_ref, send_sems, recv_sems, scatter)
        for cp in outgoing:
            cp.start()
        token[...] = jnp.zeros_like(token)

    return pl.pallas_call(
        body, name=name,
        out_shape=(pltpu.SemaphoreType.DMA((N_CHIPS - 1,)), pltpu.SemaphoreType.DMA((N_CHIPS - 1,)),
                   pltpu.HBM(src.shape, src.dtype), pltpu.HBM(land_shape, src.dtype), jax.ShapeDtypeStruct((8, LANES), F32)),
        in_specs=(HBM_SPEC, HBM_SPEC),
        out_specs=(SEM_SPEC, SEM_SPEC, HBM_SPEC, HBM_SPEC, pl.BlockSpec(memory_space=pltpu.VMEM)),
        input_output_aliases={0: 2, 1: 3},
        compiler_params=pltpu.CompilerParams(has_side_effects=DATAFLOW),
    )(pltpu.with_memory_space_constraint(src, pltpu.HBM),
      pltpu.with_memory_space_constraint(lax.empty(land_shape, src.dtype), pltpu.HBM))


def _exchange_wait(started, after, *, name, scatter):
    send_sems, recv_sems, src_thru, land_thru, _ = started

    def body(src_ref, land_ref, send_sems, recv_sems, after_ref, src_dead, got_ref):
        outgoing, incoming = _chip_copies(src_ref, land_ref, send_sems, recv_sems, scatter)
        for cp in outgoing:
            cp.wait_send()
        for cp in incoming:
            cp.wait_recv()

    return pl.pallas_call(
        body, name=name,
        out_shape=(pltpu.HBM(src_thru.shape, src_thru.dtype), pltpu.HBM(land_thru.shape, land_thru.dtype)),
        in_specs=(HBM_SPEC, HBM_SPEC, SEM_SPEC, SEM_SPEC, pl.BlockSpec(memory_space=pl.ANY)),
        out_specs=(HBM_SPEC, HBM_SPEC),
        input_output_aliases={0: 0, 1: 1},
        compiler_params=pltpu.CompilerParams(has_side_effects=DATAFLOW),
    )(src_thru, land_thru, send_sems, recv_sems, after)


def _exchange_sibling(src, *, name):
    def body(src_ref, out_ref, send_sem, recv_sem):
        x, y, c = _coords()
        cp = pltpu.make_async_remote_copy(src_ref=src_ref, dst_ref=out_ref, send_sem=send_sem, recv_sem=recv_sem,
                                          device_id=(x, y, 1 - c), device_id_type=MESH)
        cp.start()
        cp.wait()

    return pl.pallas_call(
        body, name=name,
        out_shape=jax.ShapeDtypeStruct(src.shape, src.dtype),
        in_specs=[pl.BlockSpec(memory_space=pl.ANY)],
        out_specs=pl.BlockSpec(memory_space=pl.ANY),
        scratch_shapes=[pltpu.SemaphoreType.DMA, pltpu.SemaphoreType.DMA],
        compiler_params=_params(),
    )(src)


def _pack_rows(a):
    return a.reshape(-1, D_MODEL)


def _pad_heads(w, width):
    r = w.shape[0]
    return jnp.pad(w, ((0, 0), (0, 0), (0, HEAD_PAD - width))).reshape(r, N_HEADS * HEAD_PAD)


MIX_NAMES = ("w_uq", "w_uk", "w_uv", "p_pool", "p_attn", "w_out")
GROUPS = ("in", "mix", "ff1", "ff2")
ROWS_OF = dict(PACK_ROWS)
W_IN_COLS = 3232
W_IN_SHARD = W_IN_COLS // N_CHIPS


def _local_shard(weights, l, group, zero):
    if group == "mix":
        shard = jnp.concatenate([_pack_rows(weights[n][l]) for n in MIX_NAMES], axis=0)
    else:
        shard = weights[{"in": "w_in", "ff1": "w_ff1", "ff2": "w_ff2"}[group]][l]
    return (shard + zero).astype(BF16)


def _unpack_weights(gathered, group):
    def cols(a, k):
        return a.reshape(N_CHIPS, k, -1).transpose(1, 0, 2).reshape(k, -1)

    if group == "in":
        full = gathered.reshape(W_IN_COLS, D_MODEL)
        kr = jnp.pad(full[1152:1184], ((QK_NOPE, HEAD_PAD - QK_DIM), (0, 0)))
        return dict(w_in=jnp.concatenate([full[512:896], kr, full[0:512], full[1184:3232], full[896:1152]], axis=0))
    if group == "ff1":
        return dict(w_ff1=gathered)
    if group == "ff2":
        return dict(w_ff2=gathered.reshape(D_FF, D_MODEL))

    def p_attn(a):
        full = cols(a, ATTN_DIM).reshape(N_HEADS, V_DIM, D_MODEL)
        return jnp.pad(full, ((0, 0), (0, HEAD_PAD - V_DIM), (0, 0))).reshape(N_HEADS * HEAD_PAD, D_MODEL)

    build = dict(
        w_uq=lambda a: _pad_heads(a.reshape(Q_LORA, N_HEADS, QK_DIM), QK_DIM),
        w_uk=lambda a: _pad_heads(a.reshape(KV_LORA, N_HEADS, QK_NOPE), QK_NOPE),
        w_uv=lambda a: _pad_heads(a.reshape(KV_LORA, N_HEADS, V_DIM), V_DIM),
        p_pool=lambda a: cols(a, POOL_DIM),
        p_attn=p_attn,
        w_out=lambda a: a.reshape(D_MODEL, D_MODEL),
    )
    w, off = {}, 0
    for name in MIX_NAMES:
        w[name] = build[name](gathered[:, off:off + ROWS_OF[name]])
        off += ROWS_OF[name]
    return w


def _pack_grads(g, group):
    def cols(a):
        k = a.shape[0]
        return a.reshape(k, N_CHIPS, -1).transpose(1, 0, 2).reshape(N_CHIPS, -1, D_MODEL)

    def rows(a):
        return a.reshape(N_CHIPS, -1, D_MODEL)

    def heads(width):
        return lambda a: rows(a.reshape(a.shape[0], N_HEADS, HEAD_PAD)[:, :, :width])

    if group == "in":
        gi = g["w_in"]
        full = jnp.concatenate([gi[ZC_U:ZC_U + 512], gi[ZC_CQ:ZC_CQ + 384], gi[ZC_CKV:ZC_CKV + 256],
                                gi[ZC_KR + QK_NOPE:ZC_KR + QK_DIM], gi[ZC_GA:ZC_GA + 2048]], axis=0)
        return full.reshape(N_CHIPS, W_IN_SHARD, D_MODEL)
    if group == "ff1":
        return g["w_ff1"]
    if group == "ff2":
        return g["w_ff2"].reshape(N_CHIPS, D_FF // N_CHIPS, D_MODEL)

    def p_attn(a):
        return cols(a.reshape(N_HEADS, HEAD_PAD, D_MODEL)[:, :V_DIM].reshape(ATTN_DIM, D_MODEL))

    build = dict(w_uq=heads(QK_DIM), w_uk=heads(QK_NOPE), w_uv=heads(V_DIM), p_pool=cols, p_attn=p_attn, w_out=rows)
    return jnp.concatenate([build[name](g[name]) for name in MIX_NAMES], axis=1)


def _per_head(fn, acc, *tables):
    return jnp.concatenate([fn(acc[:, h * HEAD_PAD:(h + 1) * HEAD_PAD], *tables) for h in range(N_HEADS)], axis=1)


def _rope_head(a, cos, sin):
    lane = lax.broadcasted_iota(jnp.int32, a.shape, 1)
    return a * (cos + jnp.where(lane < QK_NOPE, 1.0, 0.0)) + _rotate_half(a) * sin


def _layer_fwd(l, x, mod, get_weights, small, cos_t, sin_t):
    sh1, sc1, g1, sh2, sc2, g2 = mod
    tag = f"_l{l}"
    h, r1 = _norm_mod(x, small["ln1_g"], sc1, sh1, name="norm1" + tag)
    w = dict(get_weights("in", h))
    (z,) = _mm(h, w["w_in"], tb=True, name="in_proj" + tag, out_dtypes=(BF16,))
    p, yp, cq, ckv, kr, rq, rkv = _mixer_pre(z, cos_t, sin_t, small["w_pool"], small["pool_scale"],
                                              small["q_norm_g"], small["kv_norm_g"], name="mixer_pre" + tag)
    w.update(get_weights("mix", yp))
    (ya,) = _mm(yp, w["p_pool"], name="pool_out" + tag, out_dtypes=(BF16,))
    (q,) = _mm(cq, w["w_uq"], name="q_proj" + tag, out_dtypes=(BF16,),
               epilogue=lambda acc, cos, sin: (_per_head(_rope_head, acc, cos, sin),),
               extras=((cos_t, "table"), (sin_t, "table")))
    (k,) = _mm(ckv, w["w_uk"], name="k_proj" + tag, out_dtypes=(BF16,),
               epilogue=lambda acc, krv: (_per_head(lambda a, b: a + b, acc, krv),), extras=((kr, "table"),))
    (v,) = _mm(ckv, w["w_uv"], name="v_proj" + tag, out_dtypes=(BF16,))
    o, lse = _attn_fwd(q, k, v, name="attn_fwd" + tag)
    yb, merged = _mm(o, w["p_attn"], name="attn_out" + tag, out_dtypes=(BF16, BF16), tm=512,
                     epilogue=lambda acc, ga, gb, yav: (acc, _sigmoid(ga) * yav + _sigmoid(gb) * acc),
                     extras=((z, ("tile", ZC_GA // D_MODEL)), (z, ("tile", ZC_GB // D_MODEL)), (ya, "tile")))
    mo, x1 = _mm(merged, w["w_out"], name="mix_out" + tag, out_dtypes=(BF16, F32),
                 epilogue=lambda acc, xr, g: (acc, xr + g * acc), extras=((x, "tile"), (g1, "row")))
    h2, r2 = _norm_mod(x1, small["ln2_g"], sc2, sh2, name="norm2" + tag)
    w.update(get_weights("ff1", merged))
    f, act = _mm(h2, w["w_ff1"], b_stack=True, name="ff1" + tag, out_dtypes=(BF16, BF16),
                 epilogue=lambda acc: (acc, jnp.square(jnp.maximum(acc, 0.0))))
    w.update(get_weights("ff2", act))
    m2, x2 = _mm(act, w["w_ff2"], name="ff2" + tag, out_dtypes=(BF16, F32),
                 epilogue=lambda acc, xr, g: (acc, xr + g * acc), extras=((x1, "tile"), (g2, "row")))
    saved = dict(x=x, h=h, r1=r1, z=z, p=p, yp=yp, cq=cq, ckv=ckv, rq=rq, rkv=rkv, ya=ya, q=q, k=k, v=v, o=o, lse=lse,
                 yb=yb, merged=merged, mo=mo, x1=x1, h2=h2, r2=r2, f=f, act=act, m2=m2)
    return x2, saved, w


def _merge_grads(dm, ga, gb, ya, yb):
    sa, sb = _sigmoid(ga), _sigmoid(gb)
    return dm * sa, dm * sb, dm * ya * (sa * (1.0 - sa)), dm * yb * (sb * (1.0 - sb))


def _layer_bwd(l, dx2, dm2, dg2, sv, mod, w, small, cos_t, sin_t, send_grads, gate_below):
    sh1, sc1, g1, sh2, sc2, g2 = mod
    tag = f"_l{l}"
    gw = {}
    (df,) = _mm(dm2, w["w_ff2"], tb=True, name="ff2_dx" + tag, out_dtypes=(BF16,),
                epilogue=lambda acc, f: (acc * (2.0 * jnp.maximum(f, 0.0)),), extras=((sv["f"], "tile"),))
    (g_ff2,) = _mm(sv["act"], dm2, ta=True, name="ff2_dw" + tag, out_dtypes=(BF16,))
    (g_ff1,) = _mm(sv["h2"], df, ta=True, out_stack=N_CHIPS, name="ff1_dw" + tag, out_dtypes=(BF16,))
    sc2 = sc2 + send_grads("ff2", dict(w_ff2=g_ff2)) + send_grads("ff1", dict(w_ff1=g_ff1))
    (dh2,) = _mm(df, w["w_ff1"], tb=True, b_stack=True, name="ff1_dx" + tag)
    dx1, dln2, dsc2, dsh2, dmo, dg1 = _norm_mod_bwd(dh2, sv["x1"], sv["r2"], small["ln2_g"], sc2, dx2,
                                                    gate=(sv["mo"], g1), name="norm2_bwd" + tag)
    dya, dyb, dga, dgb = _mm(dmo, w["w_out"], tb=True, name="mix_out_dx" + tag, out_dtypes=(BF16,) * 4, tm=512,
                             epilogue=_merge_grads,
                             extras=((sv["z"], ("tile", ZC_GA // D_MODEL)), (sv["z"], ("tile", ZC_GB // D_MODEL)),
                                     (sv["ya"], "tile"), (sv["yb"], "tile")))
    (gw["w_out"],) = _mm(sv["merged"], dmo, ta=True, name="mix_out_dw" + tag, out_dtypes=(BF16,))
    (gw["p_pool"],) = _mm(sv["yp"], dya, ta=True, name="pool_out_dw" + tag, out_dtypes=(BF16,))
    (dyp,) = _mm(dya, w["p_pool"], tb=True, name="pool_out_dx" + tag)
    du, g_w_pool, g_pool_scale = _pool_bwd(dyp, sv["p"], small["w_pool"], small["pool_scale"], name="pool_bwd" + tag)
    (gw["p_attn"],) = _mm(sv["o"], dyb, ta=True, name="attn_out_dw" + tag, out_dtypes=(BF16,))
    (do,) = _mm(dyb, w["p_attn"], tb=True, name="attn_out_dx" + tag, out_dtypes=(BF16,))
    delta = _attn_delta(do, sv["o"], name="attn_delta" + tag)
    dq, dk, dv = _attn_bwd(sv["q"], sv["k"], sv["v"], do, sv["lse"], delta, name="attn_bwd" + tag)
    dql = _rope_bwd_q(dq, cos_t, sin_t, name="rope_bwd_q" + tag)
    dkb, dkr = _key_bwd(dk, cos_t, sin_t, name="key_bwd" + tag)
    (gw["w_uq"],) = _mm(sv["cq"], dql, ta=True, name="q_proj_dw" + tag, out_dtypes=(BF16,))
    (gw["w_uk"],) = _mm(sv["ckv"], dkb, ta=True, name="k_proj_dw" + tag, out_dtypes=(BF16,))
    (gw["w_uv"],) = _mm(sv["ckv"], dv, ta=True, name="v_proj_dw" + tag, out_dtypes=(BF16,))
    (dcq,) = _mm(dql, w["w_uq"], tb=True, name="q_proj_dx" + tag)
    (dckv,) = _mm(dkb, w["w_uk"], tb=True, second=(dv, w["w_uv"]), name="kv_proj_dx" + tag)
    q_norm_g = small["q_norm_g"] + send_grads("mix", gw)
    dcq_raw, g_qn = _rms_bwd(dcq, sv["z"], ZC_CQ, sv["rq"], q_norm_g, name="q_norm_bwd" + tag)
    dckv_raw, g_kvn = _rms_bwd(dckv, sv["z"], ZC_CKV, sv["rkv"], small["kv_norm_g"], name="kv_norm_bwd" + tag)
    dz = jnp.concatenate([dcq_raw, dkr, du, dga, dgb, dckv_raw], axis=1)
    (g_in,) = _mm(dz, sv["h"], ta=True, name="in_proj_dw" + tag, out_dtypes=(BF16,))
    sc1 = sc1 + send_grads("in", dict(w_in=g_in))
    (dh,) = _mm(dz, w["w_in"], name="in_proj_dx" + tag)
    dx, dln1, dsc1, dsh1, *below = _norm_mod_bwd(dh, sv["x"], sv["r1"], small["ln1_g"], sc1, dx1, gate=gate_below,
                                                 name="norm1_bwd" + tag)
    dmod = jnp.concatenate([dsh1, dsc1, dg1, dsh2, dsc2, dg2], axis=0)
    gsmall = dict(ln1_g=dln1, ln2_g=dln2, q_norm_g=g_qn, kv_norm_g=g_kvn, w_pool=g_w_pool, pool_scale=g_pool_scale)
    return dx, dmod, gsmall, below


SMALL_LOSS = 6
SMALL_SINGLES = 16
SMALL_POOL = 24
SMALL_POOL_ROWS = len(POOL_WINDOWS) * POOL_GROUP * POOL_GROUP // D_MODEL
SMALL_ROWS = SMALL_POOL + DEPTH * SMALL_POOL_ROWS


def _pack_small(parts, *, name):
    def body(*refs):
        out_ref = refs[-1]
        out_ref[...] = jnp.zeros_like(out_ref)
        for ref, (_, row) in zip(refs[:-1], parts):
            out_ref[row:row + ref.shape[0], :] = ref[...]

    return pl.pallas_call(body, name=name, out_shape=jax.ShapeDtypeStruct((SMALL_ROWS, D_MODEL), F32),
                          compiler_params=_params())(*[a for a, _ in parts])


def kernel(x, c, positions, ln1_g, ln2_g, w_ada, b_ada, w_in, q_norm_g, w_uq, kv_norm_g, w_uk, w_uv, w_pool, pool_scale, p_pool, p_attn, w_out, w_ff1, w_ff2, final_g, loss_target, m_ln1_g, m_ln2_g, m_w_ada, m_b_ada, m_w_in, m_q_norm_g, m_w_uq, m_kv_norm_g, m_w_uk, m_w_uv, m_w_pool, m_pool_scale, m_p_pool, m_p_attn, m_w_out, m_w_ff1, m_w_ff2, m_final_g, v_ln1_g, v_ln2_g, v_w_ada, v_b_ada, v_w_in, v_q_norm_g, v_w_uq, v_kv_norm_g, v_w_uk, v_w_uv, v_w_pool, v_pool_scale, v_p_pool, v_p_attn, v_w_out, v_w_ff1, v_w_ff2, v_final_g):
    weights = dict(ln1_g=ln1_g, ln2_g=ln2_g, w_ada=w_ada, b_ada=b_ada, w_in=w_in, q_norm_g=q_norm_g, w_uq=w_uq,
                   kv_norm_g=kv_norm_g, w_uk=w_uk, w_uv=w_uv, w_pool=w_pool, pool_scale=pool_scale, p_pool=p_pool,
                   p_attn=p_attn, w_out=w_out, w_ff1=w_ff1, w_ff2=w_ff2, final_g=final_g)
    moms = dict(ln1_g=m_ln1_g, ln2_g=m_ln2_g, w_ada=m_w_ada, b_ada=m_b_ada, w_in=m_w_in, q_norm_g=m_q_norm_g,
                w_uq=m_w_uq, kv_norm_g=m_kv_norm_g, w_uk=m_w_uk, w_uv=m_w_uv, w_pool=m_w_pool,
                pool_scale=m_pool_scale, p_pool=m_p_pool, p_attn=m_p_attn, w_out=m_w_out, w_ff1=m_w_ff1,
                w_ff2=m_w_ff2, final_g=m_final_g)
    vels = dict(ln1_g=v_ln1_g, ln2_g=v_ln2_g, w_ada=v_w_ada, b_ada=v_b_ada, w_in=v_w_in, q_norm_g=v_q_norm_g,
                w_uq=v_w_uq, kv_norm_g=v_kv_norm_g, w_uk=v_w_uk, w_uv=v_w_uv, w_pool=v_w_pool,
                pool_scale=v_pool_scale, p_pool=v_p_pool, p_attn=v_p_attn, w_out=v_w_out, w_ff1=v_w_ff1,
                w_ff2=v_w_ff2, final_g=v_final_g)
    order = list(weights)
    for table in (weights, moms, vels):
        table["w_in"] = jnp.swapaxes(table["w_in"], 1, 2)
    seq = x.shape[1]
    my_chip = 2 * lax.axis_index("x") + lax.axis_index("y")
    my_dev = 2 * my_chip + lax.axis_index("c")
    ada_cols = w_ada.shape[2]

    small = [dict(ln1_g=ln1_g[l:l + 1], ln2_g=ln2_g[l:l + 1], q_norm_g=q_norm_g[l:l + 1], kv_norm_g=kv_norm_g[l:l + 1],
                  w_pool=w_pool[l], pool_scale=pool_scale[l:l + 1]) for l in range(DEPTH)]

    c_all = _all_gather_small(jnp.pad(c, ((0, 7), (0, 0))), name="cond_all_gather")
    c_act = _silu(c_all, name="cond_silu")
    b_mine = lax.dynamic_slice_in_dim(b_ada, my_chip * ada_cols, ada_cols, axis=1).reshape(1, DEPTH * ada_cols)
    (mod_cat,) = _mm(c_act, w_ada, b_stack=True, name="ada_fwd", epilogue=lambda acc, b: (acc + b,),
                     extras=((b_mine, "row"),))
    mod_mine = jnp.concatenate([mod_cat[::8, l * ada_cols:(l + 1) * ada_cols] for l in range(DEPTH)], axis=0)
    mod_all = _all_gather_small(mod_mine, name="mod_all_gather").reshape(N_DEV, DEPTH, N_DEV, ada_cols)

    zero = mod_all[0, 0, 0, 0] * 0.0
    started = {}
    for l in range(DEPTH):
        for group in GROUPS:
            started[l, group] = _exchange_start(_local_shard(weights, l, group, zero), name=f"weights_send_l{l}_{group}",
                                                scatter=False)
    pin = sum(st[4][0:1, 0:1] for st in started.values())

    def gathered_weights(l, group, after):
        mine, land = _exchange_wait(started[l, group], after, name=f"weights_wait_l{l}_{group}", scatter=False)
        land = lax.dynamic_update_slice_in_dim(land, mine[None], my_chip, axis=0)
        return _unpack_weights(land, group)

    mods = []
    for l in range(DEPTH):
        row = jnp.concatenate([lax.dynamic_index_in_dim(mod_all[2 * j, l], my_dev, axis=0, keepdims=True)
                               for j in range(N_CHIPS)], axis=1) + pin
        mods.append([row[:, i * D_MODEL:(i + 1) * D_MODEL] for i in range(N_MOD)])

    inv_freq = ROPE_THETA ** (-jnp.arange(0, QK_ROPE, 2, dtype=F32) / QK_ROPE)
    freq_lanes = jnp.concatenate([jnp.zeros((QK_NOPE,), F32), inv_freq, inv_freq,
                                  jnp.zeros((HEAD_PAD - QK_DIM,), F32)]).reshape(1, LANES)
    cos_t, sin_t = _rope_tables(positions.reshape(seq, 1), freq_lanes, name="rope_tables")

    xs, saved, wl = x.reshape(seq, D_MODEL), [], []
    for l in range(DEPTH):
        xs, sv, w_l = _layer_fwd(l, xs, mods[l], functools.partial(gathered_weights, l), small[l], cos_t, sin_t)
        saved.append(sv)
        wl.append(w_l)
    dx, loss_part, g_final, dm2, dg2 = _final_loss(xs, final_g.reshape(1, D_MODEL), loss_target.reshape(seq, D_MODEL),
                                                   saved[-1]["m2"], mods[-1][5], name="final_loss")

    sent = []

    def send_grads(l, group, g):
        gpack = _pack_grads(g, group)
        started_g = _exchange_start(gpack, name=f"grads_send_l{l}_{group}", scatter=True)
        sent.append((l, group, started_g))
        return started_g[4][0:1, 0:1]

    dmod, gsmall = [None] * DEPTH, [None] * DEPTH
    for l in reversed(range(DEPTH)):
        gate_below = (saved[l - 1]["m2"], mods[l - 1][5]) if l > 0 else None
        dx, dmod[l], gsmall[l], below = _layer_bwd(l, dx, dm2, dg2, saved[l], mods[l], wl[l], small[l], cos_t, sin_t,
                                                   functools.partial(send_grads, l), gate_below)
        dm2, dg2 = below if below else (None, None)
    grads = dict(x=dx.reshape(1, seq, D_MODEL))

    def lanes(a):
        flat = a.reshape(1, -1)
        return jnp.pad(flat, ((0, 0), (0, D_MODEL - flat.shape[1])))

    singles = [gsmall[0]["ln1_g"], gsmall[1]["ln1_g"], gsmall[0]["ln2_g"], gsmall[1]["ln2_g"], g_final,
               lanes(jnp.concatenate([gsmall[l]["pool_scale"] for l in range(DEPTH)], axis=1)),
               lanes(jnp.concatenate([gsmall[l]["q_norm_g"] for l in range(DEPTH)], axis=1)),
               lanes(jnp.concatenate([gsmall[l]["kv_norm_g"] for l in range(DEPTH)], axis=1))]
    parts = [(dmod[0], 0), (lanes(loss_part), SMALL_LOSS), (dmod[1], 8)]
    parts += [(a, SMALL_SINGLES + i) for i, a in enumerate(singles)]
    parts += [(gsmall[l]["w_pool"].reshape(-1, D_MODEL), SMALL_POOL + l * SMALL_POOL_ROWS) for l in range(DEPTH)]
    small_all = _all_gather_small(_pack_small(parts, name="small_grads_pack"), name="small_grads_all_gather")
    small_all = small_all.reshape(N_DEV, SMALL_ROWS, D_MODEL)
    ssum = _sum_slots(small_all, N_DEV, name="small_grads_sum")
    loss = ssum[SMALL_LOSS, 0]
    grads["b_ada"] = jnp.stack([ssum[8 * l:8 * l + N_MOD] for l in range(DEPTH)]).reshape(DEPTH, N_MOD * D_MODEL)
    grads["ln1_g"] = ssum[SMALL_SINGLES:SMALL_SINGLES + 2]
    grads["ln2_g"] = ssum[SMALL_SINGLES + 2:SMALL_SINGLES + 4]
    grads["final_g"] = ssum[SMALL_SINGLES + 4]
    grads["pool_scale"] = ssum[SMALL_SINGLES + 5].reshape(DEPTH, POOL_DIM)
    grads["q_norm_g"] = ssum[SMALL_SINGLES + 6, :DEPTH * Q_LORA].reshape(DEPTH, Q_LORA)
    grads["kv_norm_g"] = ssum[SMALL_SINGLES + 7, :DEPTH * KV_LORA].reshape(DEPTH, KV_LORA)
    grads["w_pool"] = ssum[SMALL_POOL:SMALL_ROWS].reshape(w_pool.shape)

    gsum, after = {}, ssum
    for l, group, started_g in sent:
        tg = f"_l{l}_{group}"
        gpack, land = _exchange_wait(started_g, after, name="grads_wait" + tg, scatter=True)
        own = lax.dynamic_index_in_dim(gpack, my_chip, axis=0, keepdims=True)
        land = lax.dynamic_update_slice_in_dim(land, own, my_chip, axis=0)
        part = _sum_slots(land, N_CHIPS, name="grads_sum_chips" + tg)
        other = _exchange_sibling(part, name="grads_swap_cores" + tg)
        gsum[group] = _add2_stacked(part, other, gsum.get(group), l, name="grads_sum_cores" + tg)
        after = gsum[group]
    grads.update(w_in=gsum["in"], w_ff1=gsum["ff1"], w_ff2=gsum["ff2"])
    off = 0
    for name in MIX_NAMES:
        grads[name] = gsum["mix"][:, off:off + ROWS_OF[name]].reshape(weights[name].shape)
        off += ROWS_OF[name]

    c_act_t = jnp.pad(c_act[::8].T, ((0, 0), (0, LANES - N_DEV)))
    d_mine = []
    for l in range(DEPTH):
        d_all = small_all[:, 8 * l:8 * l + N_MOD].reshape(N_DEV, N_MOD * D_MODEL)
        d_mine.append(lax.dynamic_slice_in_dim(d_all, my_chip * ada_cols, ada_cols, axis=1))
    d_cat = jnp.pad(jnp.concatenate(d_mine, axis=1), ((0, LANES - N_DEV), (0, 0)))
    (grads["w_ada"],) = _mm(c_act_t, d_cat, out_stack=DEPTH, name="ada_dw")

    def view(a):
        return a.reshape(1, -1) if a.ndim == 1 else a if a.ndim == 3 else a.reshape(-1, a.shape[-1])

    delta, new_m, new_v = {}, {}, {}
    for name in order:
        shape = weights[name].shape
        d, nm, nv = _adamw(view(weights[name]), view(grads[name]), view(moms[name]), view(vels[name]),
                           name="adamw_" + name)
        delta[name], new_m[name], new_v[name] = d.reshape(shape), nm.reshape(shape), nv.reshape(shape)
    for table in (grads, delta, new_m, new_v):
        table["w_in"] = jnp.swapaxes(table["w_in"], 1, 2)
    return (loss, grads["x"], *[grads[n] for n in order], *[delta[n] for n in order],
            *[new_m[n] for n in order], *[new_v[n] for n in order])
```

```python
import functools
import math

import jax
import jax.numpy as jnp
from jax import lax
from jax.experimental import pallas as pl
from jax.experimental.pallas import tpu as pltpu

F32 = jnp.float32
BF16 = jnp.bfloat16
MESH = pl.DeviceIdType.MESH

D_MODEL = 1024
DEPTH = 2
POOL_WINDOWS = (2, 4, 8, 16)
POOL_GROUP = 128
POOL_DIM = 512
N_HEADS = 8
QK_NOPE = 64
QK_ROPE = 32
QK_DIM = QK_NOPE + QK_ROPE
V_DIM = 64
HEAD_PAD = 128
Q_LORA = 384
KV_LORA = 256
ROPE_THETA = 10000.0
ATTN_DIM = N_HEADS * V_DIM
D_FF = 4 * D_MODEL
N_MOD = 6
EPS = 1e-6
N_CHIPS = 4
N_DEV = 8

ADAM_LR = 0.001
ADAM_B1 = 0.9
ADAM_B2 = 0.999
ADAM_EPS = 1e-08
ADAM_WD = 0.01
ADAM_STEP = 10

VMEM_LIMIT_BYTES = 56 * 1024 * 1024
LANES = 128
HALO = 16

ZC_CQ = 0
ZC_KR = 384
ZC_U = 512
ZC_GA = 1024
ZC_GB = 2048
ZC_CKV = 3072
Z_DIM = 3328
Z_OFFSETS = dict(cq=ZC_CQ, kr=ZC_KR, u=ZC_U, ga=ZC_GA, gb=ZC_GB, ckv=ZC_CKV)

PACK_ROWS = (("w_in", 808), ("w_uq", 72), ("w_uk", 32), ("w_uv", 32), ("p_pool", 128), ("p_attn", 128),
             ("w_out", 256), ("w_ff1", 1024), ("w_ff2", 1024))


def _params(sem=None, **kw):
    return pltpu.CompilerParams(dimension_semantics=sem, vmem_limit_bytes=VMEM_LIMIT_BYTES, **kw)


def _tile(n, target, unit=LANES):
    best = None
    for t in range(unit, min(n, target) + 1, unit):
        if n % t == 0:
            best = t
    return best if best is not None and 4 * best >= min(n, target) else n


def _near_tile(n, target):
    cands = [t for t in range(LANES, n + 1, LANES) if n % t == 0]
    return min(cands, key=lambda t: abs(math.log(t / target))) if cands else n


def _mm(a, b, *, name, ta=False, tb=False, out_dtypes=(F32,), epilogue=None, extras=(), tm=1024, tn=1024, tk=1024,
        second=None, b_stack=False, out_stack=None):
    (k_dim, m_dim) = a.shape if ta else a.shape[::-1]
    if b_stack:
        g_b, k_b, n_shard = b.shape
        n_dim, k_b = (k_b, g_b * n_shard) if tb else (g_b * n_shard, k_b)
    else:
        (n_dim, k_b) = b.shape if tb else b.shape[::-1]
    assert k_dim == k_b, (a.shape, b.shape)
    n_unit = n_shard if b_stack and not tb else n_dim // out_stack if out_stack else n_dim
    k_unit = n_shard if b_stack and tb else k_dim
    tm, tn, tk = _near_tile(m_dim, tm), _near_tile(n_unit, tn), _near_tile(k_unit, tk)
    nk = k_dim // tk
    n_extra, n_out = len(extras), len(out_dtypes)
    n_lhs = 4 if second else 2
    dims = (((0 if ta else 1,), (1 if tb else 0,)), ((), ()))
    if epilogue is None:
        epilogue = lambda acc: (acc,) * n_out

    def body(*refs):
        operand_refs, rest = refs[:n_lhs], refs[n_lhs:]
        extra_refs, out_refs = rest[:n_extra], rest[n_extra:n_extra + n_out]

        def product():
            total = None
            for a_ref, b_ref in zip(operand_refs[0::2], operand_refs[1::2]):
                part = lax.dot_general(a_ref[...].astype(BF16), b_ref[...].astype(BF16), dims, preferred_element_type=F32)
                total = part if total is None else total + part
            return total

        def finish(acc):
            outs = epilogue(acc, *[r[...] for r in extra_refs])
            for o_ref, o in zip(out_refs, outs):
                o_ref[...] = o.astype(o_ref.dtype)

        if nk == 1:
            finish(product())
            return
        acc_ref = rest[-1]
        k = pl.program_id(2)

        @pl.when(k == 0)
        def _():
            acc_ref[...] = product()

        @pl.when((k > 0) & (k < nk - 1))
        def _():
            acc_ref[...] += product()

        @pl.when(k == nk - 1)
        def _():
            finish(acc_ref[...] + product())

    a_spec = pl.BlockSpec((tk, tm), lambda i, j, k: (k, i)) if ta else pl.BlockSpec((tm, tk), lambda i, j, k: (i, k))
    if b_stack and tb:
        per = n_shard // tk
        b_spec = pl.BlockSpec((None, tn, tk), lambda i, j, k: (k // per, j, k % per))
    elif b_stack:
        per = n_shard // tn
        b_spec = pl.BlockSpec((None, tk, tn), lambda i, j, k: (j // per, k, j % per))
    elif tb:
        b_spec = pl.BlockSpec((tn, tk), lambda i, j, k: (j, k))
    else:
        b_spec = pl.BlockSpec((tk, tn), lambda i, j, k: (k, j))
    if out_stack:
        per_out = (n_dim // out_stack) // tn
        out_spec = pl.BlockSpec((None, tm, tn), lambda i, j, k: (j // per_out, i, j % per_out))
        out_dims = (out_stack, m_dim, n_dim // out_stack)
    else:
        out_spec = pl.BlockSpec((tm, tn), lambda i, j, k: (i, j))
        out_dims = (m_dim, n_dim)
    extra_specs = []
    for arr, kind in extras:
        if kind == "tile":
            extra_specs.append(pl.BlockSpec((tm, tn), lambda i, j, k: (i, j)))
        elif isinstance(kind, tuple):
            extra_specs.append(pl.BlockSpec((tm, tn), functools.partial(lambda i, j, k, c: (i, j + c), c=kind[1])))
        elif kind == "row":
            extra_specs.append(pl.BlockSpec((1, tn), lambda i, j, k: (0, j)))
        elif kind == "col":
            extra_specs.append(pl.BlockSpec((tm, 1), lambda i, j, k: (i, 0)))
        else:
            assert kind == "table", kind
            extra_specs.append(pl.BlockSpec((tm, LANES), lambda i, j, k: (i, 0)))
    return pl.pallas_call(
        body,
        name=name,
        grid=(m_dim // tm, n_dim // tn, nk),
        in_specs=[a_spec, b_spec] * (n_lhs // 2) + extra_specs,
        out_specs=[out_spec for _ in out_dtypes],
        out_shape=[jax.ShapeDtypeStruct(out_dims, dt) for dt in out_dtypes],
        scratch_shapes=[pltpu.VMEM((tm, tn), F32)] if nk > 1 else [],
        compiler_params=_params(("parallel", "parallel", "arbitrary")),
    )(a, b, *(second or ()), *[arr for arr, _ in extras])


def _mm_sum(pieces, b, offsets, *, name, tm=1024, tn=1024):
    m_dim, n_dim = pieces[0].shape[0], b.shape[1]
    tm, tn = _near_tile(m_dim, tm), _near_tile(n_dim, tn)
    n_pieces = len(pieces)

    def body(*refs):
        total = None
        for a_ref, b_ref in zip(refs[:n_pieces], refs[n_pieces:2 * n_pieces]):
            part = jnp.dot(a_ref[...], b_ref[...], preferred_element_type=F32)
            total = part if total is None else total + part
        refs[-1][...] = total

    a_specs = [pl.BlockSpec((tm, p.shape[1]), lambda i, j: (i, 0)) for p in pieces]
    b_specs = [pl.BlockSpec((p.shape[1], tn), functools.partial(lambda i, j, blk: (blk, j), blk=off // p.shape[1]))
               for p, off in zip(pieces, offsets)]
    return pl.pallas_call(
        body, name=name, grid=(m_dim // tm, n_dim // tn),
        in_specs=a_specs + b_specs,
        out_specs=pl.BlockSpec((tm, tn), lambda i, j: (i, j)),
        out_shape=jax.ShapeDtypeStruct((m_dim, n_dim), F32),
        compiler_params=_params(("parallel", "parallel")),
    )(*pieces, *[b] * n_pieces)


def _rows(s):
    return min(512, s)


def _rope_tables(pos_col, inv_freq_lanes, *, name):
    s = pos_col.shape[0]
    tb = _rows(s)

    def body(pos_ref, f_ref, cos_ref, sin_ref):
        ang = pos_ref[...].astype(F32) * f_ref[...]
        lane = lax.broadcasted_iota(jnp.int32, ang.shape, 1)
        on = (lane >= QK_NOPE) & (lane < QK_DIM)
        cos_ref[...] = jnp.where(on, jnp.cos(ang), 0.0)
        sin_ref[...] = jnp.where(on, jnp.sin(ang), 0.0)

    return pl.pallas_call(
        body, name=name, grid=(s // tb,),
        in_specs=[pl.BlockSpec((tb, 1), lambda i: (i, 0)), pl.BlockSpec((1, LANES), lambda i: (0, 0))],
        out_specs=[pl.BlockSpec((tb, LANES), lambda i: (i, 0))] * 2,
        out_shape=[jax.ShapeDtypeStruct((s, LANES), F32)] * 2,
        compiler_params=_params(("parallel",)),
    )(pos_col, inv_freq_lanes)


def _rotate_half(x):
    lane = lax.broadcasted_iota(jnp.int32, x.shape, 1)
    half = QK_ROPE // 2
    first = (lane >= QK_NOPE) & (lane < QK_NOPE + half)
    second = (lane >= QK_NOPE + half) & (lane < QK_DIM)
    return jnp.where(first, -pltpu.roll(x, LANES - half, 1), jnp.where(second, pltpu.roll(x, half, 1), 0.0))


def _norm_mod(x, g, sc, sh, *, name):
    s, d = x.shape
    tb = _rows(s)

    def body(x_ref, g_ref, sc_ref, sh_ref, h_ref, r_ref):
        xv = x_ref[...]
        r = lax.rsqrt(jnp.mean(xv * xv, axis=-1, keepdims=True) + EPS)
        r_ref[...] = r
        h_ref[...] = (((xv * r) * g_ref[...]) * (1.0 + sc_ref[...]) + sh_ref[...]).astype(BF16)

    vec = pl.BlockSpec((1, d), lambda i: (0, 0))
    return pl.pallas_call(
        body, name=name, grid=(s // tb,),
        in_specs=[pl.BlockSpec((tb, d), lambda i: (i, 0)), vec, vec, vec],
        out_specs=[pl.BlockSpec((tb, d), lambda i: (i, 0)), pl.BlockSpec((tb, 1), lambda i: (i, 0))],
        out_shape=[jax.ShapeDtypeStruct((s, d), BF16), jax.ShapeDtypeStruct((s, 1), F32)],
        compiler_params=_params(("parallel",)),
    )(x, g, sc, sh)


def _window_sums(ext, sign):
    n = ext.shape[0]
    sums, cur, k = [], ext, 1
    for _ in POOL_WINDOWS:
        cur = cur + pltpu.roll(cur, k if sign > 0 else n - k, 0)
        sums.append(cur)
        k *= 2
    return sums


def _mixer_pre(z, cos_t, sin_t, w_pool, pool_scale, gq, gkv, *, name):
    s = z.shape[0]
    tb = _rows(s)
    hb = tb // HALO

    def body(zcq_ref, zkr_ref, zu_ref, zuh_ref, zckv_ref, cos_ref, sin_ref, wp_ref, ps_ref, gq_ref, gkv_ref,
             p_ref, yp_ref, cq_ref, ckv_ref, kr_ref, rq_ref, rkv_ref):
        i = pl.program_id(0)
        u = zu_ref[...].astype(F32)
        halo = jnp.where(i > 0, zuh_ref[...].astype(F32), 0.0)
        ext = jnp.concatenate([halo, u], axis=0)
        t = i * tb + lax.broadcasted_iota(jnp.int32, (tb, 1), 0)
        for g, (w, sw) in enumerate(zip(POOL_WINDOWS, _window_sums(ext, +1))):
            cols = slice(g * POOL_GROUP, (g + 1) * POOL_GROUP)
            cnt = jnp.minimum(t + 1, w).astype(F32)
            pg = (sw[HALO:, cols] / cnt - u[:, cols]).astype(BF16)
            p_ref[:, cols] = pg
            yg = jnp.dot(pg, wp_ref[g].astype(BF16), preferred_element_type=F32)
            yp_ref[:, cols] = (yg * ps_ref[:, cols]).astype(BF16)

        def rms(x_ref, g_ref, out_ref, r_ref):
            xv = x_ref[...].astype(F32)
            r = lax.rsqrt(jnp.mean(xv * xv, axis=-1, keepdims=True) + EPS)
            r_ref[...] = r
            out_ref[...] = ((xv * r) * g_ref[...]).astype(BF16)

        rms(zcq_ref, gq_ref, cq_ref, rq_ref)
        rms(zckv_ref, gkv_ref, ckv_ref, rkv_ref)
        kr = zkr_ref[...].astype(F32)
        kr_ref[...] = (kr * cos_ref[...] + _rotate_half(kr) * sin_ref[...]).astype(BF16)

    def zcol(width, off):
        return pl.BlockSpec((tb, width), lambda i: (i, off // width))

    def full(a):
        return pl.BlockSpec(a.shape, lambda i: (0,) * a.ndim)

    def out(width, dt):
        return pl.BlockSpec((tb, width), lambda i: (i, 0)), jax.ShapeDtypeStruct((s, width), dt)

    outs = [out(POOL_DIM, BF16), out(POOL_DIM, BF16), out(Q_LORA, BF16), out(KV_LORA, BF16), out(LANES, BF16),
            out(1, F32), out(1, F32)]
    return pl.pallas_call(
        body, name=name, grid=(s // tb,),
        in_specs=[zcol(Q_LORA, ZC_CQ), zcol(LANES, ZC_KR), zcol(POOL_DIM, ZC_U),
                  pl.BlockSpec((HALO, POOL_DIM), lambda i: (jnp.maximum(i * hb - 1, 0), ZC_U // POOL_DIM)),
                  zcol(KV_LORA, ZC_CKV),
                  pl.BlockSpec((tb, LANES), lambda i: (i, 0)), pl.BlockSpec((tb, LANES), lambda i: (i, 0)),
                  full(w_pool), full(pool_scale), full(gq), full(gkv)],
        out_specs=[o[0] for o in outs], out_shape=[o[1] for o in outs],
        compiler_params=_params(("parallel",)),
    )(z, z, z, z, z, cos_t, sin_t, w_pool, pool_scale, gq, gkv)


def _sigmoid(x):
    return 1.0 / (1.0 + jnp.exp(-x.astype(F32)))


ATTN_SCALE = 1.0 / math.sqrt(QK_DIM)
NEG_BIG = -1e30


LOG2_E = math.log2(math.e)
EXP2_SCALE = ATTN_SCALE * LOG2_E
NT_DIMS = (((1,), (1,)), ((), ()))
TN_DIMS = (((0,), (0,)), ((), ()))


def _on_or_below_diagonal(t):
    return lax.broadcasted_iota(jnp.int32, (t, t), 0) >= lax.broadcasted_iota(jnp.int32, (t, t), 1)


HEADS_PER_STEP = 2
HEAD_COLS = [slice(g * HEAD_PAD, (g + 1) * HEAD_PAD) for g in range(HEADS_PER_STEP)]


def _attn_fwd(q, k, v, *, name):
    s = q.shape[0]
    t = _rows(s)
    wide = HEADS_PER_STEP * HEAD_PAD

    def body(q_ref, k_ref, v_ref, o_ref, lse_ref):
        qi = pl.program_id(1)
        qs = [q_ref[:, cols] for cols in HEAD_COLS]

        def block(j, carry, diagonal):
            rows = pl.ds(pl.multiple_of(j * t, t), t)
            out = []
            for qv, cols, (m, l, acc) in zip(qs, HEAD_COLS, carry):
                sc = lax.dot_general(qv, k_ref[rows, cols], NT_DIMS, preferred_element_type=F32)
                if diagonal:
                    sc = jnp.where(_on_or_below_diagonal(t), sc, NEG_BIG)
                m_new = jnp.maximum(m, jnp.max(sc, axis=-1, keepdims=True))
                p = jnp.exp2((sc - m_new) * EXP2_SCALE)
                alpha = jnp.exp2((m - m_new) * EXP2_SCALE)
                l = alpha * l + jnp.sum(p, axis=-1, keepdims=True)
                acc = alpha * acc + jnp.dot(p.astype(BF16), v_ref[rows, cols], preferred_element_type=F32)
                out.append((m_new, l, acc))
            return tuple(out)

        init = tuple((jnp.full((t, 1), -jnp.inf, F32), jnp.zeros((t, 1), F32), jnp.zeros((t, HEAD_PAD), F32))
                     for _ in HEAD_COLS)
        carry = lax.fori_loop(0, qi, lambda j, c: block(j, c, False), init)
        for g, (cols, (m, l, acc)) in enumerate(zip(HEAD_COLS, block(qi, carry, True))):
            o_ref[:, cols] = (acc / l).astype(BF16)
            lse_ref[g] = m * ATTN_SCALE + jnp.log(l)

    q_spec = pl.BlockSpec((t, wide), lambda h, i: (i, h))
    kv_spec = pl.BlockSpec((s, wide), lambda h, i: (0, h))
    return pl.pallas_call(
        body, name=name, grid=(N_HEADS // HEADS_PER_STEP, s // t),
        in_specs=[q_spec, kv_spec, kv_spec],
        out_specs=[q_spec, pl.BlockSpec((HEADS_PER_STEP, t, 1), lambda h, i: (h, i, 0))],
        out_shape=[jax.ShapeDtypeStruct((s, N_HEADS * HEAD_PAD), BF16), jax.ShapeDtypeStruct((N_HEADS, s, 1), F32)],
        compiler_params=_params(("parallel", "parallel")),
    )(q, k, v)


def _attn_delta(do, o, *, name):
    s = o.shape[0]
    t = _rows(s)

    def body(do_ref, o_ref, out_ref):
        for h in range(N_HEADS):
            cols = slice(h * HEAD_PAD, (h + 1) * HEAD_PAD)
            out_ref[h] = jnp.sum(do_ref[:, cols].astype(F32) * o_ref[:, cols].astype(F32), axis=-1, keepdims=True)

    blk = pl.BlockSpec((t, N_HEADS * HEAD_PAD), lambda i: (i, 0))
    return pl.pallas_call(
        body, name=name, grid=(s // t,), in_specs=[blk, blk],
        out_specs=pl.BlockSpec((N_HEADS, t, 1), lambda i: (0, i, 0)),
        out_shape=jax.ShapeDtypeStruct((N_HEADS, s, 1), F32),
        compiler_params=_params(("parallel",)),
    )(do, o)


def _attn_bwd(q, k, v, do, lse, delta, *, name):
    s = q.shape[0]
    t = _rows(s)
    nt = s // t

    def body(q_ref, k_ref, v_ref, do_ref, lse_ref, dl_ref, dq_ref, dk_ref, dv_ref):
        kj = pl.program_id(1)

        @pl.when(kj == 0)
        def _():
            dq_ref[...] = jnp.zeros_like(dq_ref)

        kvs = [(k_ref[:, cols], v_ref[:, cols]) for cols in HEAD_COLS]

        def block(i, carry, diagonal):
            rows = pl.ds(pl.multiple_of(i * t, t), t)
            out = []
            for g, (cols, (kv, vv), (dk, dv)) in enumerate(zip(HEAD_COLS, kvs, carry)):
                qv, dov = q_ref[rows, cols], do_ref[rows, cols]
                sc = lax.dot_general(qv, kv, NT_DIMS, preferred_element_type=F32)
                p = jnp.exp2(sc * EXP2_SCALE - lse_ref[g, rows, :] * LOG2_E)
                if diagonal:
                    p = jnp.where(_on_or_below_diagonal(t), p, 0.0)
                dp = lax.dot_general(dov, vv, NT_DIMS, preferred_element_type=F32)
                ds = (p * (dp - dl_ref[g, rows, :])).astype(BF16)
                dv = dv + lax.dot_general(p.astype(BF16), dov, TN_DIMS, preferred_element_type=F32)
                dk = dk + lax.dot_general(ds, qv, TN_DIMS, preferred_element_type=F32)
                dq_ref[rows, cols] += jnp.dot(ds, kv, preferred_element_type=F32) * ATTN_SCALE
                out.append((dk, dv))
            return tuple(out)

        zero = jnp.zeros((t, HEAD_PAD), F32)
        carry = block(kj, tuple((zero, zero) for _ in HEAD_COLS), True)
        for cols, (dk, dv) in zip(HEAD_COLS, lax.fori_loop(kj + 1, nt, lambda i, c: block(i, c, False), carry)):
            dk_ref[:, cols] = dk * ATTN_SCALE
            dv_ref[:, cols] = dv.astype(BF16)

    full_spec = pl.BlockSpec((s, HEADS_PER_STEP * HEAD_PAD), lambda h, j: (0, h))
    kv_spec = pl.BlockSpec((t, HEADS_PER_STEP * HEAD_PAD), lambda h, j: (j, h))
    vec_spec = pl.BlockSpec((HEADS_PER_STEP, s, 1), lambda h, j: (h, 0, 0))
    wide = (s, N_HEADS * HEAD_PAD)
    return pl.pallas_call(
        body, name=name, grid=(N_HEADS // HEADS_PER_STEP, nt),
        in_specs=[full_spec, kv_spec, kv_spec, full_spec, vec_spec, vec_spec],
        out_specs=[full_spec, kv_spec, kv_spec],
        out_shape=[jax.ShapeDtypeStruct(wide, F32), jax.ShapeDtypeStruct(wide, F32), jax.ShapeDtypeStruct(wide, BF16)],
        compiler_params=_params(("parallel", "arbitrary")),
    )(q, k, v, do, lse, delta)


def _acc_specs(widths):
    return ([pl.BlockSpec((1, w), lambda i: (0, 0)) for w in widths],
            [jax.ShapeDtypeStruct((1, w), F32) for w in widths])


def _gate_grads(dxv, m_ref, gate_ref, dm_ref, dgate_ref):
    dm_ref[...] = (dxv * gate_ref[...]).astype(BF16)
    dgate_ref[...] += jnp.sum(dxv * m_ref[...], axis=0, keepdims=True)


def _final_loss(x, g, target, m, gate, *, name):
    s, d = x.shape
    tb = _rows(s)

    def body(x_ref, g_ref, t_ref, m_ref, gate_ref, dx_ref, loss_ref, dg_ref, dm_ref, dgate_ref):
        @pl.when(pl.program_id(0) == 0)
        def _():
            loss_ref[...] = jnp.zeros_like(loss_ref)
            dg_ref[...] = jnp.zeros_like(dg_ref)
            dgate_ref[...] = jnp.zeros_like(dgate_ref)

        xv = x_ref[...]
        r = lax.rsqrt(jnp.mean(xv * xv, axis=-1, keepdims=True) + EPS)
        xn = xv * r
        err = xn * g_ref[...] - t_ref[...]
        loss_ref[...] += 0.5 * jnp.sum(jnp.mean(err * err, axis=-1, keepdims=True), axis=0, keepdims=True)
        dy = err / d
        dg_ref[...] += jnp.sum(dy * xn, axis=0, keepdims=True)
        dxn = dy * g_ref[...]
        dxv = r * (dxn - xn * jnp.mean(dxn * xn, axis=-1, keepdims=True))
        dx_ref[...] = dxv
        _gate_grads(dxv, m_ref, gate_ref, dm_ref, dgate_ref)

    blk = pl.BlockSpec((tb, d), lambda i: (i, 0))
    vec = pl.BlockSpec((1, d), lambda i: (0, 0))
    acc_specs, acc_shapes = _acc_specs((LANES, d))
    return pl.pallas_call(
        body, name=name, grid=(s // tb,),
        in_specs=[blk, vec, blk, blk, vec],
        out_specs=[blk] + acc_specs + [blk, vec],
        out_shape=[jax.ShapeDtypeStruct((s, d), F32)] + acc_shapes + [jax.ShapeDtypeStruct((s, d), BF16),
                                                                     jax.ShapeDtypeStruct((1, d), F32)],
        compiler_params=_params(("arbitrary",)),
    )(x, g, target, m, gate)


def _norm_mod_bwd(dh, x, r, g, sc, dx_skip, *, name, gate=None):
    s, d = x.shape
    tb = _rows(s)
    nb = s // tb
    n_gate = 2 if gate else 0

    def body(dh_ref, x_ref, r_ref, g_ref, sc_ref, skip_ref, *rest):
        gate_refs, (dx_ref, dg_ref, dsc_ref, dsh_ref) = rest[:n_gate], rest[n_gate:n_gate + 4]
        gate_outs, da_sc = rest[n_gate + 4:-1], rest[-1]
        i = pl.program_id(0)

        @pl.when(i == 0)
        def _():
            da_sc[...] = jnp.zeros_like(da_sc)
            dsh_ref[...] = jnp.zeros_like(dsh_ref)
            if gate:
                gate_outs[1][...] = jnp.zeros_like(gate_outs[1])

        dhv, rv = dh_ref[...], r_ref[...]
        xn = x_ref[...] * rv
        dsh_ref[...] += jnp.sum(dhv, axis=0, keepdims=True)
        da_sc[...] += jnp.sum(dhv * xn, axis=0, keepdims=True)
        dxn = dhv * (g_ref[...] * (1.0 + sc_ref[...]))
        dxv = skip_ref[...] + rv * (dxn - xn * jnp.mean(dxn * xn, axis=-1, keepdims=True))
        dx_ref[...] = dxv
        if gate:
            _gate_grads(dxv, *gate_refs, *gate_outs)

        @pl.when(i == nb - 1)
        def _():
            dg_ref[...] = da_sc[...] * (1.0 + sc_ref[...])
            dsc_ref[...] = da_sc[...] * g_ref[...]

    blk = pl.BlockSpec((tb, d), lambda i: (i, 0))
    vec = pl.BlockSpec((1, d), lambda i: (0, 0))
    acc_specs, acc_shapes = _acc_specs((d, d, d))
    gate_specs = [blk, vec] if gate else []
    gate_shapes = [jax.ShapeDtypeStruct((s, d), BF16), jax.ShapeDtypeStruct((1, d), F32)] if gate else []
    return pl.pallas_call(
        body, name=name, grid=(nb,),
        in_specs=[blk, blk, pl.BlockSpec((tb, 1), lambda i: (i, 0)), vec, vec, blk] + gate_specs,
        out_specs=[blk] + acc_specs + gate_specs,
        out_shape=[jax.ShapeDtypeStruct((s, d), F32)] + acc_shapes + gate_shapes,
        scratch_shapes=[pltpu.VMEM((1, d), F32)],
        compiler_params=_params(("arbitrary",)),
    )(dh, x, r, g, sc, dx_skip, *(gate or ()))


def _pool_bwd(dyp, p, w_pool, pool_scale, *, name):
    s = dyp.shape[0]
    tb = _rows(s)
    nb = s // tb
    hb = tb // HALO
    nt_dims = (((1,), (1,)), ((), ()))
    tn_dims = (((0,), (0,)), ((), ()))

    def body(dy_ref, dyn_ref, p_ref, wp_ref, ps_ref, du_ref, gwp_ref, gps_ref):
        i = pl.program_id(0)

        @pl.when(i == 0)
        def _():
            gwp_ref[...] = jnp.zeros_like(gwp_ref)
            gps_ref[...] = jnp.zeros_like(gps_ref)

        cur = dy_ref[...]
        nxt = jnp.where(i < nb - 1, dyn_ref[...], 0.0)
        dpw = (jnp.concatenate([cur, nxt], axis=0) * ps_ref[...]).astype(BF16)
        t = i * tb + lax.broadcasted_iota(jnp.int32, (tb + HALO, 1), 0)
        for g, w in enumerate(POOL_WINDOWS):
            cols = slice(g * POOL_GROUP, (g + 1) * POOL_GROUP)
            wg = wp_ref[g].astype(BF16)
            dp = lax.dot_general(dpw[:, cols], wg, nt_dims, preferred_element_type=F32)
            e = dp / jnp.minimum(t + 1, w).astype(F32)
            lead = _window_sums(e, -1)[g]
            du_ref[:, cols] = (lead[:tb] - dp[:tb]).astype(BF16)
            pg = p_ref[:, cols]
            pw = jnp.dot(pg, wg, preferred_element_type=F32)
            gps_ref[:, cols] += jnp.sum(cur[:, cols] * pw, axis=0, keepdims=True)
            gwp_ref[g] += lax.dot_general(pg, dpw[:tb, cols], tn_dims, preferred_element_type=F32)

    blk = pl.BlockSpec((tb, POOL_DIM), lambda i: (i, 0))
    return pl.pallas_call(
        body, name=name, grid=(nb,),
        in_specs=[blk, pl.BlockSpec((HALO, POOL_DIM), lambda i: (jnp.minimum((i + 1) * hb, s // HALO - 1), 0)), blk,
                  pl.BlockSpec(w_pool.shape, lambda i: (0, 0, 0)), pl.BlockSpec((1, POOL_DIM), lambda i: (0, 0))],
        out_specs=[blk, pl.BlockSpec(w_pool.shape, lambda i: (0, 0, 0)), pl.BlockSpec((1, POOL_DIM), lambda i: (0, 0))],
        out_shape=[jax.ShapeDtypeStruct((s, POOL_DIM), BF16), jax.ShapeDtypeStruct(w_pool.shape, F32),
                   jax.ShapeDtypeStruct((1, POOL_DIM), F32)],
        compiler_params=_params(("arbitrary",)),
    )(dyp, dyp, p, w_pool, pool_scale)


def _rope_bwd_q(dq, cos_t, sin_t, *, name):
    s = dq.shape[0]
    tb = _rows(s)

    def body(dq_ref, cos_ref, sin_ref, out_ref):
        sin = sin_ref[...]
        lane = lax.broadcasted_iota(jnp.int32, sin.shape, 1)
        cos_q = cos_ref[...] + jnp.where(lane < QK_NOPE, 1.0, 0.0)
        for h in range(N_HEADS):
            cols = slice(h * HEAD_PAD, (h + 1) * HEAD_PAD)
            dqv = dq_ref[:, cols]
            out_ref[:, cols] = (dqv * cos_q - _rotate_half(dqv * sin)).astype(BF16)

    blk = pl.BlockSpec((tb, N_HEADS * HEAD_PAD), lambda i: (i, 0))
    tab = pl.BlockSpec((tb, LANES), lambda i: (i, 0))
    return pl.pallas_call(
        body, name=name, grid=(s // tb,), in_specs=[blk, tab, tab], out_specs=blk,
        out_shape=jax.ShapeDtypeStruct(dq.shape, BF16),
        compiler_params=_params(("parallel",)),
    )(dq, cos_t, sin_t)


def _key_bwd(dk, cos_t, sin_t, *, name):
    s = dk.shape[0]
    tb = _rows(s)

    def body(dk_ref, cos_ref, sin_ref, dkb_ref, dkr_ref):
        dkv = dk_ref[...]
        dkb_ref[...] = dkv.astype(BF16)
        tot = dkv[:, :HEAD_PAD]
        for h in range(1, N_HEADS):
            tot = tot + dkv[:, h * HEAD_PAD:(h + 1) * HEAD_PAD]
        dkr_ref[...] = (tot * cos_ref[...] - _rotate_half(tot * sin_ref[...])).astype(BF16)

    blk = pl.BlockSpec((tb, N_HEADS * HEAD_PAD), lambda i: (i, 0))
    tab = pl.BlockSpec((tb, LANES), lambda i: (i, 0))
    return pl.pallas_call(
        body, name=name, grid=(s // tb,), in_specs=[blk, tab, tab], out_specs=[blk, tab],
        out_shape=[jax.ShapeDtypeStruct(dk.shape, BF16), jax.ShapeDtypeStruct((s, LANES), BF16)],
        compiler_params=_params(("parallel",)),
    )(dk, cos_t, sin_t)


def _rms_bwd(dy, z, z_off, r, g, *, name):
    s, n = dy.shape
    tb = _rows(s)

    def body(dy_ref, x_ref, r_ref, g_ref, dx_ref, dg_ref):
        @pl.when(pl.program_id(0) == 0)
        def _():
            dg_ref[...] = jnp.zeros_like(dg_ref)

        dyv, rv = dy_ref[...], r_ref[...]
        xn = x_ref[...].astype(F32) * rv
        dg_ref[...] += jnp.sum(dyv * xn, axis=0, keepdims=True)
        dxn = dyv * g_ref[...]
        dx_ref[...] = (rv * (dxn - xn * jnp.mean(dxn * xn, axis=-1, keepdims=True))).astype(BF16)

    blk = pl.BlockSpec((tb, n), lambda i: (i, 0))
    acc_specs, acc_shapes = _acc_specs((n,))
    return pl.pallas_call(
        body, name=name, grid=(s // tb,),
        in_specs=[blk, pl.BlockSpec((tb, n), lambda i: (i, z_off // n)), pl.BlockSpec((tb, 1), lambda i: (i, 0)),
                  pl.BlockSpec((1, n), lambda i: (0, 0))],
        out_specs=[blk] + acc_specs, out_shape=[jax.ShapeDtypeStruct((s, n), BF16)] + acc_shapes,
        compiler_params=_params(("arbitrary",)),
    )(dy, z, r, g)


def _silu(c, *, name):
    def body(c_ref, out_ref):
        cv = c_ref[...]
        out_ref[...] = (cv * _sigmoid(cv)).astype(BF16)

    return pl.pallas_call(body, name=name, out_shape=jax.ShapeDtypeStruct(c.shape, BF16),
                          compiler_params=_params())(c)


def _sum_slots(a, n, *, name, out_dtype=F32):
    _, rows, cols = a.shape
    tr = _tile(rows, 256, 8)

    def body(a_ref, out_ref):
        tot = a_ref[0].astype(F32)
        for j in range(1, n):
            tot = tot + a_ref[j].astype(F32)
        out_ref[...] = tot.astype(out_dtype)

    return pl.pallas_call(
        body, name=name, grid=(rows // tr,),
        in_specs=[pl.BlockSpec((n, tr, cols), lambda i: (0, i, 0))],
        out_specs=pl.BlockSpec((tr, cols), lambda i: (i, 0)),
        out_shape=jax.ShapeDtypeStruct((rows, cols), out_dtype),
        compiler_params=_params(("parallel",)),
    )(a)


def _add2_stacked(a, b, stacked, l, *, name):
    rows, cols = a.shape
    tr = _tile(rows, 256, 8)

    def body(a_ref, b_ref, *rest):
        rest[-1][...] = a_ref[...] + b_ref[...]

    blk = pl.BlockSpec((tr, cols), lambda i: (i, 0))
    carried = [] if stacked is None else [stacked]
    return pl.pallas_call(
        body, name=name, grid=(rows // tr,),
        in_specs=[blk, blk] + [pl.BlockSpec(memory_space=pl.ANY) for _ in carried],
        out_specs=pl.BlockSpec((None, tr, cols), lambda i: (l, i, 0)),
        out_shape=jax.ShapeDtypeStruct((DEPTH, rows, cols), F32),
        input_output_aliases={2: 0} if carried else {},
        compiler_params=_params(("parallel",)),
    )(a, b, *carried)


def _adamw(w, g, m, v, *, name):
    shape = w.shape
    if w.ndim == 2:
        w, g, m, v = (a.reshape((1,) + shape) for a in (w, g, m, v))
    layers, rows, cols = w.shape
    tr = _tile(rows, max(8, (1 << 18) // cols), 8)
    c1 = 1.0 - ADAM_B1 ** ADAM_STEP
    c2 = 1.0 - ADAM_B2 ** ADAM_STEP

    def body(w_ref, g_ref, m_ref, v_ref, d_ref, nm_ref, nv_ref):
        gv = g_ref[...]
        nm = ADAM_B1 * m_ref[...] + (1.0 - ADAM_B1) * gv
        nv = ADAM_B2 * v_ref[...] + (1.0 - ADAM_B2) * (gv * gv)
        nm_ref[...] = nm
        nv_ref[...] = nv
        d_ref[...] = -ADAM_LR * ((nm / c1) / (jnp.sqrt(nv / c2) + ADAM_EPS) + ADAM_WD * w_ref[...])

    blk = pl.BlockSpec((None, tr, cols), lambda l, i: (l, i, 0))
    outs = pl.pallas_call(
        body, name=name, grid=(layers, rows // tr), in_specs=[blk] * 4, out_specs=[blk] * 3,
        out_shape=[jax.ShapeDtypeStruct((layers, rows, cols), F32)] * 3,
        compiler_params=_params(("parallel", "parallel")),
    )(w, g, m, v)
    return [o.reshape(shape) for o in outs]


def _coords():
    return lax.axis_index("x"), lax.axis_index("y"), lax.axis_index("c")


def _other_chips(x, y):
    return [(1 - x, y), (x, 1 - y), (1 - x, 1 - y)]


def _all_gather_small(blk, *, name):
    m_per, n = blk.shape

    def body(x_ref, out_ref, send_sems, recv_sems, local_sem):
        x, y, c = _coords()
        me, sibling = (x, y, c), (x, y, 1 - c)
        chips = _other_chips(x, y)

        def rows(px, py, pc):
            return out_ref.at[pl.ds((4 * px + 2 * py + pc) * m_per, m_per), :]

        def copy(k, block, to, src=None):
            return pltpu.make_async_remote_copy(
                src_ref=rows(*block) if src is None else src, dst_ref=rows(*block),
                send_sem=send_sems.at[k], recv_sem=recv_sems.at[k], device_id=to, device_id_type=MESH)

        mine = pltpu.make_async_copy(x_ref, rows(*me), local_sem)
        mine.start()
        first = [copy(0, me, sibling, src=x_ref)]
        first += [copy(1 + j, me, (*chip, c), src=x_ref) for j, chip in enumerate(chips)]
        for cp in first:
            cp.start()
        passed = [copy(4 + j, (*chip, c), sibling) for j, chip in enumerate(chips)]
        for j, chip in enumerate(chips):
            copy(1 + j, (*chip, c), me).wait_recv()
            passed[j].start()
        copy(0, sibling, me).wait_recv()
        for j, chip in enumerate(chips):
            copy(4 + j, (*chip, 1 - c), me).wait_recv()
        for cp in first + passed:
            cp.wait_send()
        mine.wait()

    return pl.pallas_call(
        body, name=name,
        out_shape=jax.ShapeDtypeStruct((N_DEV * m_per, n), blk.dtype),
        in_specs=[pl.BlockSpec(memory_space=pltpu.VMEM)],
        out_specs=pl.BlockSpec(memory_space=pltpu.VMEM),
        scratch_shapes=[pltpu.SemaphoreType.DMA((7,)), pltpu.SemaphoreType.DMA((7,)), pltpu.SemaphoreType.DMA],
        compiler_params=_params(),
    )(blk)


HBM_SPEC = pl.BlockSpec(memory_space=pltpu.HBM)
SEM_SPEC = pl.BlockSpec(memory_space=pltpu.SEMAPHORE)
DATAFLOW = pltpu.SideEffectType.DATAFLOW_SIDE_EFFECTING


def _chip_copies(src_ref, land_ref, send_sems, recv_sems, scatter):
    x, y, c = _coords()
    my = 2 * x + y
    outgoing, incoming = [], []
    for k, (px, py) in enumerate(_other_chips(x, y)):
        peer = 2 * px + py

        def copy(src_slot, dst_slot):
            return pltpu.make_async_remote_copy(
                src_ref=src_ref.at[src_slot] if scatter else src_ref, dst_ref=land_ref.at[dst_slot],
                send_sem=send_sems.at[k], recv_sem=recv_sems.at[k], device_id=(px, py, c), device_id_type=MESH)

        outgoing.append(copy(peer, my))
        incoming.append(copy(my, peer))
    return outgoing, incoming


def _exchange_start(src, *, name, scatter):
    land_shape = src.shape if scatter else (N_CHIPS,) + src.shape

    def body(src_ref, land_ref, send_sems, recv_sems, src_thru, land_thru, token):
        outgoing, _ = _chip_copies(src_ref, land_ref, send_sems, recv_sems, scatter)
        for cp in outgoing:
            cp.start()
        token[...] = jnp.zeros_like(token)

    return pl.pallas_call(
        body, name=name,
        out_shape=(pltpu.SemaphoreType.DMA((N_CHIPS - 1,)), pltpu.SemaphoreType.DMA((N_CHIPS - 1,)),
                   pltpu.HBM(src.shape, src.dtype), pltpu.HBM(land_shape, src.dtype), jax.ShapeDtypeStruct((8, LANES), F32)),
        in_specs=(HBM_SPEC, HBM_SPEC),
        out_specs=(SEM_SPEC, SEM_SPEC, HBM_SPEC, HBM_SPEC, pl.BlockSpec(memory_space=pltpu.VMEM)),
        input_output_aliases={0: 2, 1: 3},
        compiler_params=pltpu.CompilerParams(has_side_effects=DATAFLOW),
    )(pltpu.with_memory_space_constraint(src, pltpu.HBM),
      pltpu.with_memory_space_constraint(lax.empty(land_shape, src.dtype), pltpu.HBM))


def _exchange_wait(started, after, *, name, scatter):
    send_sems, recv_sems, src_thru, land_thru, _ = started

    def body(src_ref, land_ref, send_sems, recv_sems, after_ref, src_dead, got_ref):
        outgoing, incoming = _chip_copies(src_ref, land_ref, send_sems, recv_sems, scatter)
        for cp in outgoing:
            cp.wait_send()
        for cp in incoming:
            cp.wait_recv()

    return pl.pallas_call(
        body, name=name,
        out_shape=(pltpu.HBM(src_thru.shape, src_thru.dtype), pltpu.HBM(land_thru.shape, land_thru.dtype)),
        in_specs=(HBM_SPEC, HBM_SPEC, SEM_SPEC, SEM_SPEC, pl.BlockSpec(memory_space=pl.ANY)),
        out_specs=(HBM_SPEC, HBM_SPEC),
        input_output_aliases={0: 0, 1: 1},
        compiler_params=pltpu.CompilerParams(has_side_effects=DATAFLOW),
    )(src_thru, land_thru, send_sems, recv_sems, after)


def _exchange_sibling(src, *, name):
    def body(src_ref, out_ref, send_sem, recv_sem):
        x, y, c = _coords()
        cp = pltpu.make_async_remote_copy(src_ref=src_ref, dst_ref=out_ref, send_sem=send_sem, recv_sem=recv_sem,
                                          device_id=(x, y, 1 - c), device_id_type=MESH)
        cp.start()
        cp.wait()

    return pl.pallas_call(
        body, name=name,
        out_shape=jax.ShapeDtypeStruct(src.shape, src.dtype),
        in_specs=[pl.BlockSpec(memory_space=pl.ANY)],
        out_specs=pl.BlockSpec(memory_space=pl.ANY),
        scratch_shapes=[pltpu.SemaphoreType.DMA, pltpu.SemaphoreType.DMA],
        compiler_params=_params(),
    )(src)


def _pack_rows(a):
    return a.reshape(-1, D_MODEL)


def _pad_heads(w, width):
    r = w.shape[0]
    return jnp.pad(w, ((0, 0), (0, 0), (0, HEAD_PAD - width))).reshape(r, N_HEADS * HEAD_PAD)


MIX_NAMES = ("w_uq", "w_uk", "w_uv", "p_pool", "p_attn", "w_out")
GROUPS = ("in", "mix", "ff1", "ff2")
ROWS_OF = dict(PACK_ROWS)
W_IN_COLS = 3232
W_IN_SHARD = W_IN_COLS // N_CHIPS


def _local_shard(weights, l, group, zero):
    if group == "mix":
        shard = jnp.concatenate([_pack_rows(weights[n][l]) for n in MIX_NAMES], axis=0)
    else:
        shard = weights[{"in": "w_in", "ff1": "w_ff1", "ff2": "w_ff2"}[group]][l]
    return (shard + zero).astype(BF16)


def _unpack_weights(gathered, group):
    def cols(a, k):
        return a.reshape(N_CHIPS, k, -1).transpose(1, 0, 2).reshape(k, -1)

    if group == "in":
        full = gathered.reshape(W_IN_COLS, D_MODEL)
        kr = jnp.pad(full[1152:1184], ((QK_NOPE, HEAD_PAD - QK_DIM), (0, 0)))
        return dict(w_in=jnp.concatenate([full[512:896], kr, full[0:512], full[1184:3232], full[896:1152]], axis=0))
    if group == "ff1":
        return dict(w_ff1=gathered)
    if group == "ff2":
        return dict(w_ff2=gathered.reshape(D_FF, D_MODEL))

    def p_attn(a):
        full = cols(a, ATTN_DIM).reshape(N_HEADS, V_DIM, D_MODEL)
        return jnp.pad(full, ((0, 0), (0, HEAD_PAD - V_DIM), (0, 0))).reshape(N_HEADS * HEAD_PAD, D_MODEL)

    build = dict(
        w_uq=lambda a: _pad_heads(a.reshape(Q_LORA, N_HEADS, QK_DIM), QK_DIM),
        w_uk=lambda a: _pad_heads(a.reshape(KV_LORA, N_HEADS, QK_NOPE), QK_NOPE),
        w_uv=lambda a: _pad_heads(a.reshape(KV_LORA, N_HEADS, V_DIM), V_DIM),
        p_pool=lambda a: cols(a, POOL_DIM),
        p_attn=p_attn,
        w_out=lambda a: a.reshape(D_MODEL, D_MODEL),
    )
    w, off = {}, 0
    for name in MIX_NAMES:
        w[name] = build[name](gathered[:, off:off + ROWS_OF[name]])
        off += ROWS_OF[name]
    return w


def _pack_grads(g, group):
    def cols(a):
        k = a.shape[0]
        return a.reshape(k, N_CHIPS, -1).transpose(1, 0, 2).reshape(N_CHIPS, -1, D_MODEL)

    def rows(a):
        return a.reshape(N_CHIPS, -1, D_MODEL)

    def heads(width):
        return lambda a: rows(a.reshape(a.shape[0], N_HEADS, HEAD_PAD)[:, :, :width])

    if group == "in":
        full = jnp.concatenate([g["u"], g["cq"], g["ckv"], g["kr"][QK_NOPE:QK_DIM], g["ga"], g["gb"]], axis=0)
        return full.reshape(N_CHIPS, W_IN_SHARD, D_MODEL)
    if group == "ff1":
        return g["w_ff1"]
    if group == "ff2":
        return g["w_ff2"].reshape(N_CHIPS, D_FF // N_CHIPS, D_MODEL)

    def p_attn(a):
        return cols(a.reshape(N_HEADS, HEAD_PAD, D_MODEL)[:, :V_DIM].reshape(ATTN_DIM, D_MODEL))

    build = dict(w_uq=heads(QK_DIM), w_uk=heads(QK_NOPE), w_uv=heads(V_DIM), p_pool=cols, p_attn=p_attn, w_out=rows)
    return jnp.concatenate([build[name](g[name]) for name in MIX_NAMES], axis=1)


def _per_head(fn, acc, *tables):
    return jnp.concatenate([fn(acc[:, h * HEAD_PAD:(h + 1) * HEAD_PAD], *tables) for h in range(N_HEADS)], axis=1)


def _rope_head(a, cos, sin):
    lane = lax.broadcasted_iota(jnp.int32, a.shape, 1)
    return a * (cos + jnp.where(lane < QK_NOPE, 1.0, 0.0)) + _rotate_half(a) * sin


def _layer_fwd(l, x, mod, get_weights, small, cos_t, sin_t):
    sh1, sc1, g1, sh2, sc2, g2 = mod
    tag = f"_l{l}"
    h, r1 = _norm_mod(x, small["ln1_g"], sc1, sh1, name="norm1" + tag)
    w = dict(get_weights("in", h))
    (z,) = _mm(h, w["w_in"], tb=True, name="in_proj" + tag, out_dtypes=(BF16,))
    p, yp, cq, ckv, kr, rq, rkv = _mixer_pre(z, cos_t, sin_t, small["w_pool"], small["pool_scale"],
                                              small["q_norm_g"], small["kv_norm_g"], name="mixer_pre" + tag)
    w.update(get_weights("mix", yp))
    (ya,) = _mm(yp, w["p_pool"], name="pool_out" + tag, out_dtypes=(BF16,))
    (q,) = _mm(cq, w["w_uq"], name="q_proj" + tag, out_dtypes=(BF16,),
               epilogue=lambda acc, cos, sin: (_per_head(_rope_head, acc, cos, sin),),
               extras=((cos_t, "table"), (sin_t, "table")))
    (k,) = _mm(ckv, w["w_uk"], name="k_proj" + tag, out_dtypes=(BF16,),
               epilogue=lambda acc, krv: (_per_head(lambda a, b: a + b, acc, krv),), extras=((kr, "table"),))
    (v,) = _mm(ckv, w["w_uv"], name="v_proj" + tag, out_dtypes=(BF16,))
    o, lse = _attn_fwd(q, k, v, name="attn_fwd" + tag)
    yb, merged = _mm(o, w["p_attn"], name="attn_out" + tag, out_dtypes=(BF16, BF16), tm=512,
                     epilogue=lambda acc, ga, gb, yav: (acc, _sigmoid(ga) * yav + _sigmoid(gb) * acc),
                     extras=((z, ("tile", ZC_GA // D_MODEL)), (z, ("tile", ZC_GB // D_MODEL)), (ya, "tile")))
    mo, x1 = _mm(merged, w["w_out"], name="mix_out" + tag, out_dtypes=(BF16, F32),
                 epilogue=lambda acc, xr, g: (acc, xr + g * acc), extras=((x, "tile"), (g1, "row")))
    h2, r2 = _norm_mod(x1, small["ln2_g"], sc2, sh2, name="norm2" + tag)
    w.update(get_weights("ff1", merged))
    f, act = _mm(h2, w["w_ff1"], b_stack=True, name="ff1" + tag, out_dtypes=(BF16, BF16),
                 epilogue=lambda acc: (acc, jnp.square(jnp.maximum(acc, 0.0))))
    w.update(get_weights("ff2", act))
    m2, x2 = _mm(act, w["w_ff2"], name="ff2" + tag, out_dtypes=(BF16, F32),
                 epilogue=lambda acc, xr, g: (acc, xr + g * acc), extras=((x1, "tile"), (g2, "row")))
    saved = dict(x=x, h=h, r1=r1, z=z, p=p, yp=yp, cq=cq, ckv=ckv, rq=rq, rkv=rkv, ya=ya, q=q, k=k, v=v, o=o, lse=lse,
                 yb=yb, merged=merged, mo=mo, x1=x1, h2=h2, r2=r2, f=f, act=act, m2=m2)
    return x2, saved, w


def _merge_grads(dm, ga, gb, ya, yb):
    sa, sb = _sigmoid(ga), _sigmoid(gb)
    return dm * sa, dm * sb, dm * ya * (sa * (1.0 - sa)), dm * yb * (sb * (1.0 - sb))


def _layer_bwd(l, dx2, dm2, dg2, sv, mod, w, small, cos_t, sin_t, send_grads, gate_below):
    sh1, sc1, g1, sh2, sc2, g2 = mod
    tag = f"_l{l}"
    gw = {}
    (df,) = _mm(dm2, w["w_ff2"], tb=True, name="ff2_dx" + tag, out_dtypes=(BF16,),
                epilogue=lambda acc, f: (acc * (2.0 * jnp.maximum(f, 0.0)),), extras=((sv["f"], "tile"),))
    (g_ff2,) = _mm(sv["act"], dm2, ta=True, name="ff2_dw" + tag, out_dtypes=(BF16,))
    (g_ff1,) = _mm(sv["h2"], df, ta=True, out_stack=N_CHIPS, name="ff1_dw" + tag, out_dtypes=(BF16,))
    sc2 = sc2 + send_grads("ff2", dict(w_ff2=g_ff2)) + send_grads("ff1", dict(w_ff1=g_ff1))
    (dh2,) = _mm(df, w["w_ff1"], tb=True, b_stack=True, name="ff1_dx" + tag)
    dx1, dln2, dsc2, dsh2, dmo, dg1 = _norm_mod_bwd(dh2, sv["x1"], sv["r2"], small["ln2_g"], sc2, dx2,
                                                    gate=(sv["mo"], g1), name="norm2_bwd" + tag)
    dya, dyb, dga, dgb = _mm(dmo, w["w_out"], tb=True, name="mix_out_dx" + tag, out_dtypes=(BF16,) * 4, tm=512,
                             epilogue=_merge_grads,
                             extras=((sv["z"], ("tile", ZC_GA // D_MODEL)), (sv["z"], ("tile", ZC_GB // D_MODEL)),
                                     (sv["ya"], "tile"), (sv["yb"], "tile")))
    (gw["w_out"],) = _mm(sv["merged"], dmo, ta=True, name="mix_out_dw" + tag, out_dtypes=(BF16,))
    (gw["p_pool"],) = _mm(sv["yp"], dya, ta=True, name="pool_out_dw" + tag, out_dtypes=(BF16,))
    (dyp,) = _mm(dya, w["p_pool"], tb=True, name="pool_out_dx" + tag)
    du, g_w_pool, g_pool_scale = _pool_bwd(dyp, sv["p"], small["w_pool"], small["pool_scale"], name="pool_bwd" + tag)
    (gw["p_attn"],) = _mm(sv["o"], dyb, ta=True, name="attn_out_dw" + tag, out_dtypes=(BF16,))
    (do,) = _mm(dyb, w["p_attn"], tb=True, name="attn_out_dx" + tag, out_dtypes=(BF16,))
    delta = _attn_delta(do, sv["o"], name="attn_delta" + tag)
    dq, dk, dv = _attn_bwd(sv["q"], sv["k"], sv["v"], do, sv["lse"], delta, name="attn_bwd" + tag)
    dql = _rope_bwd_q(dq, cos_t, sin_t, name="rope_bwd_q" + tag)
    dkb, dkr = _key_bwd(dk, cos_t, sin_t, name="key_bwd" + tag)
    (gw["w_uq"],) = _mm(sv["cq"], dql, ta=True, name="q_proj_dw" + tag, out_dtypes=(BF16,))
    (gw["w_uk"],) = _mm(sv["ckv"], dkb, ta=True, name="k_proj_dw" + tag, out_dtypes=(BF16,))
    (gw["w_uv"],) = _mm(sv["ckv"], dv, ta=True, name="v_proj_dw" + tag, out_dtypes=(BF16,))
    (dcq,) = _mm(dql, w["w_uq"], tb=True, name="q_proj_dx" + tag)
    (dckv,) = _mm(dkb, w["w_uk"], tb=True, second=(dv, w["w_uv"]), name="kv_proj_dx" + tag)
    q_norm_g = small["q_norm_g"] + send_grads("mix", gw)
    dcq_raw, g_qn = _rms_bwd(dcq, sv["z"], ZC_CQ, sv["rq"], q_norm_g, name="q_norm_bwd" + tag)
    dckv_raw, g_kvn = _rms_bwd(dckv, sv["z"], ZC_CKV, sv["rkv"], small["kv_norm_g"], name="kv_norm_bwd" + tag)
    dz = dict(cq=dcq_raw, kr=dkr, u=du, ga=dga, gb=dgb, ckv=dckv_raw)
    g_in = {n: _mm(piece, sv["h"], ta=True, name=f"in_proj_dw_{n}" + tag, out_dtypes=(BF16,))[0]
            for n, piece in dz.items()}
    sc1 = sc1 + send_grads("in", g_in)
    dh = _mm_sum(list(dz.values()), w["w_in"], [Z_OFFSETS[n] for n in dz], name="in_proj_dx" + tag)
    dx, dln1, dsc1, dsh1, *below = _norm_mod_bwd(dh, sv["x"], sv["r1"], small["ln1_g"], sc1, dx1, gate=gate_below,
                                                 name="norm1_bwd" + tag)
    dmod = jnp.concatenate([dsh1, dsc1, dg1, dsh2, dsc2, dg2], axis=0)
    gsmall = dict(ln1_g=dln1, ln2_g=dln2, q_norm_g=g_qn, kv_norm_g=g_kvn, w_pool=g_w_pool, pool_scale=g_pool_scale)
    return dx, dmod, gsmall, below


SMALL_LOSS = 6
SMALL_SINGLES = 16
SMALL_POOL = 24
SMALL_POOL_ROWS = len(POOL_WINDOWS) * POOL_GROUP * POOL_GROUP // D_MODEL
SMALL_ROWS = SMALL_POOL + DEPTH * SMALL_POOL_ROWS


def _pack_small(parts, *, name):
    def body(*refs):
        out_ref = refs[-1]
        out_ref[...] = jnp.zeros_like(out_ref)
        for ref, (_, row) in zip(refs[:-1], parts):
            out_ref[row:row + ref.shape[0], :] = ref[...]

    return pl.pallas_call(body, name=name, out_shape=jax.ShapeDtypeStruct((SMALL_ROWS, D_MODEL), F32),
                          compiler_params=_params())(*[a for a, _ in parts])


def kernel(x, c, positions, ln1_g, ln2_g, w_ada, b_ada, w_in, q_norm_g, w_uq, kv_norm_g, w_uk, w_uv, w_pool, pool_scale, p_pool, p_attn, w_out, w_ff1, w_ff2, final_g, loss_target, m_ln1_g, m_ln2_g, m_w_ada, m_b_ada, m_w_in, m_q_norm_g, m_w_uq, m_kv_norm_g, m_w_uk, m_w_uv, m_w_pool, m_pool_scale, m_p_pool, m_p_attn, m_w_out, m_w_ff1, m_w_ff2, m_final_g, v_ln1_g, v_ln2_g, v_w_ada, v_b_ada, v_w_in, v_q_norm_g, v_w_uq, v_kv_norm_g, v_w_uk, v_w_uv, v_w_pool, v_pool_scale, v_p_pool, v_p_attn, v_w_out, v_w_ff1, v_w_ff2, v_final_g):
    weights = dict(ln1_g=ln1_g, ln2_g=ln2_g, w_ada=w_ada, b_ada=b_ada, w_in=w_in, q_norm_g=q_norm_g, w_uq=w_uq,
                   kv_norm_g=kv_norm_g, w_uk=w_uk, w_uv=w_uv, w_pool=w_pool, pool_scale=pool_scale, p_pool=p_pool,
                   p_attn=p_attn, w_out=w_out, w_ff1=w_ff1, w_ff2=w_ff2, final_g=final_g)
    moms = dict(ln1_g=m_ln1_g, ln2_g=m_ln2_g, w_ada=m_w_ada, b_ada=m_b_ada, w_in=m_w_in, q_norm_g=m_q_norm_g,
                w_uq=m_w_uq, kv_norm_g=m_kv_norm_g, w_uk=m_w_uk, w_uv=m_w_uv, w_pool=m_w_pool,
                pool_scale=m_pool_scale, p_pool=m_p_pool, p_attn=m_p_attn, w_out=m_w_out, w_ff1=m_w_ff1,
                w_ff2=m_w_ff2, final_g=m_final_g)
    vels = dict(ln1_g=v_ln1_g, ln2_g=v_ln2_g, w_ada=v_w_ada, b_ada=v_b_ada, w_in=v_w_in, q_norm_g=v_q_norm_g,
                w_uq=v_w_uq, kv_norm_g=v_kv_norm_g, w_uk=v_w_uk, w_uv=v_w_uv, w_pool=v_w_pool,
                pool_scale=v_pool_scale, p_pool=v_p_pool, p_attn=v_p_attn, w_out=v_w_out, w_ff1=v_w_ff1,
                w_ff2=v_w_ff2, final_g=v_final_g)
    order = list(weights)
    for table in (weights, moms, vels):
        table["w_in"] = jnp.swapaxes(table["w_in"], 1, 2)
    seq = x.shape[1]
    my_chip = 2 * lax.axis_index("x") + lax.axis_index("y")
    my_dev = 2 * my_chip + lax.axis_index("c")
    ada_cols = w_ada.shape[2]

    small = [dict(ln1_g=ln1_g[l:l + 1], ln2_g=ln2_g[l:l + 1], q_norm_g=q_norm_g[l:l + 1], kv_norm_g=kv_norm_g[l:l + 1],
                  w_pool=w_pool[l], pool_scale=pool_scale[l:l + 1]) for l in range(DEPTH)]

    c_all = _all_gather_small(jnp.pad(c, ((0, 7), (0, 0))), name="cond_all_gather")
    c_act = _silu(c_all, name="cond_silu")
    b_mine = lax.dynamic_slice_in_dim(b_ada, my_chip * ada_cols, ada_cols, axis=1).reshape(1, DEPTH * ada_cols)
    (mod_cat,) = _mm(c_act, w_ada, b_stack=True, name="ada_fwd", epilogue=lambda acc, b: (acc + b,),
                     extras=((b_mine, "row"),))
    mod_mine = jnp.concatenate([mod_cat[::8, l * ada_cols:(l + 1) * ada_cols] for l in range(DEPTH)], axis=0)
    mod_all = _all_gather_small(mod_mine, name="mod_all_gather").reshape(N_DEV, DEPTH, N_DEV, ada_cols)

    zero = mod_all[0, 0, 0, 0] * 0.0
    started = {}
    for l in range(DEPTH):
        for group in GROUPS:
            started[l, group] = _exchange_start(_local_shard(weights, l, group, zero), name=f"weights_send_l{l}_{group}",
                                                scatter=False)
    pin = sum(st[4][0:1, 0:1] for st in started.values())

    def gathered_weights(l, group, after):
        mine, land = _exchange_wait(started[l, group], after, name=f"weights_wait_l{l}_{group}", scatter=False)
        land = lax.dynamic_update_slice_in_dim(land, mine[None], my_chip, axis=0)
        return _unpack_weights(land, group)

    mods = []
    for l in range(DEPTH):
        row = jnp.concatenate([lax.dynamic_index_in_dim(mod_all[2 * j, l], my_dev, axis=0, keepdims=True)
                               for j in range(N_CHIPS)], axis=1) + pin
        mods.append([row[:, i * D_MODEL:(i + 1) * D_MODEL] for i in range(N_MOD)])

    inv_freq = ROPE_THETA ** (-jnp.arange(0, QK_ROPE, 2, dtype=F32) / QK_ROPE)
    freq_lanes = jnp.concatenate([jnp.zeros((QK_NOPE,), F32), inv_freq, inv_freq,
                                  jnp.zeros((HEAD_PAD - QK_DIM,), F32)]).reshape(1, LANES)
    cos_t, sin_t = _rope_tables(positions.reshape(seq, 1), freq_lanes, name="rope_tables")

    xs, saved, wl = x.reshape(seq, D_MODEL), [], []
    for l in range(DEPTH):
        xs, sv, w_l = _layer_fwd(l, xs, mods[l], functools.partial(gathered_weights, l), small[l], cos_t, sin_t)
        saved.append(sv)
        wl.append(w_l)
    dx, loss_part, g_final, dm2, dg2 = _final_loss(xs, final_g.reshape(1, D_MODEL), loss_target.reshape(seq, D_MODEL),
                                                   saved[-1]["m2"], mods[-1][5], name="final_loss")

    sent = []

    def send_grads(l, group, g):
        gpack = _pack_grads(g, group)
        started_g = _exchange_start(gpack, name=f"grads_send_l{l}_{group}", scatter=True)
        sent.append((l, group, started_g))
        return started_g[4][0:1, 0:1]

    dmod, gsmall = [None] * DEPTH, [None] * DEPTH
    for l in reversed(range(DEPTH)):
        gate_below = (saved[l - 1]["m2"], mods[l - 1][5]) if l > 0 else None
        dx, dmod[l], gsmall[l], below = _layer_bwd(l, dx, dm2, dg2, saved[l], mods[l], wl[l], small[l], cos_t, sin_t,
                                                   functools.partial(send_grads, l), gate_below)
        dm2, dg2 = below if below else (None, None)
    grads = dict(x=dx.reshape(1, seq, D_MODEL))

    def lanes(a):
        flat = a.reshape(1, -1)
        return jnp.pad(flat, ((0, 0), (0, D_MODEL - flat.shape[1])))

    singles = [gsmall[0]["ln1_g"], gsmall[1]["ln1_g"], gsmall[0]["ln2_g"], gsmall[1]["ln2_g"], g_final,
               lanes(jnp.concatenate([gsmall[l]["pool_scale"] for l in range(DEPTH)], axis=1)),
               lanes(jnp.concatenate([gsmall[l]["q_norm_g"] for l in range(DEPTH)], axis=1)),
               lanes(jnp.concatenate([gsmall[l]["kv_norm_g"] for l in range(DEPTH)], axis=1))]
    parts = [(dmod[0], 0), (lanes(loss_part), SMALL_LOSS), (dmod[1], 8)]
    parts += [(a, SMALL_SINGLES + i) for i, a in enumerate(singles)]
    parts += [(gsmall[l]["w_pool"].reshape(-1, D_MODEL), SMALL_POOL + l * SMALL_POOL_ROWS) for l in range(DEPTH)]
    small_all = _all_gather_small(_pack_small(parts, name="small_grads_pack"), name="small_grads_all_gather")
    small_all = small_all.reshape(N_DEV, SMALL_ROWS, D_MODEL)
    ssum = _sum_slots(small_all, N_DEV, name="small_grads_sum")
    loss = ssum[SMALL_LOSS, 0]
    grads["b_ada"] = jnp.stack([ssum[8 * l:8 * l + N_MOD] for l in range(DEPTH)]).reshape(DEPTH, N_MOD * D_MODEL)
    grads["ln1_g"] = ssum[SMALL_SINGLES:SMALL_SINGLES + 2]
    grads["ln2_g"] = ssum[SMALL_SINGLES + 2:SMALL_SINGLES + 4]
    grads["final_g"] = ssum[SMALL_SINGLES + 4]
    grads["pool_scale"] = ssum[SMALL_SINGLES + 5].reshape(DEPTH, POOL_DIM)
    grads["q_norm_g"] = ssum[SMALL_SINGLES + 6, :DEPTH * Q_LORA].reshape(DEPTH, Q_LORA)
    grads["kv_norm_g"] = ssum[SMALL_SINGLES + 7, :DEPTH * KV_LORA].reshape(DEPTH, KV_LORA)
    grads["w_pool"] = ssum[SMALL_POOL:SMALL_ROWS].reshape(w_pool.shape)

    gsum, after = {}, ssum
    for l, group, started_g in sent:
        tg = f"_l{l}_{group}"
        gpack, land = _exchange_wait(started_g, after, name="grads_wait" + tg, scatter=True)
        own = lax.dynamic_index_in_dim(gpack, my_chip, axis=0, keepdims=True)
        land = lax.dynamic_update_slice_in_dim(land, own, my_chip, axis=0)
        part = _sum_slots(land, N_CHIPS, name="grads_sum_chips" + tg)
        other = _exchange_sibling(part, name="grads_swap_cores" + tg)
        gsum[group] = _add2_stacked(part, other, gsum.get(group), l, name="grads_sum_cores" + tg)
        after = gsum[group]
    grads.update(w_in=gsum["in"], w_ff1=gsum["ff1"], w_ff2=gsum["ff2"])
    off = 0
    for name in MIX_NAMES:
        grads[name] = gsum["mix"][:, off:off + ROWS_OF[name]].reshape(weights[name].shape)
        off += ROWS_OF[name]

    c_act_t = jnp.pad(c_act[::8].T, ((0, 0), (0, LANES - N_DEV)))
    d_mine = []
    for l in range(DEPTH):
        d_all = small_all[:, 8 * l:8 * l + N_MOD].reshape(N_DEV, N_MOD * D_MODEL)
        d_mine.append(lax.dynamic_slice_in_dim(d_all, my_chip * ada_cols, ada_cols, axis=1))
    d_cat = jnp.pad(jnp.concatenate(d_mine, axis=1), ((0, LANES - N_DEV), (0, 0)))
    (grads["w_ada"],) = _mm(c_act_t, d_cat, out_stack=DEPTH, name="ada_dw")

    def view(a):
        return a.reshape(1, -1) if a.ndim == 1 else a if a.ndim == 3 else a.reshape(-1, a.shape[-1])

    delta, new_m, new_v = {}, {}, {}
    for name in order:
        shape = weights[name].shape
        d, nm, nv = _adamw(view(weights[name]), view(grads[name]), view(moms[name]), view(vels[name]),
                           name="adamw_" + name)
        delta[name], new_m[name], new_v[name] = d.reshape(shape), nm.reshape(shape), nv.reshape(shape)
    for table in (grads, delta, new_m, new_v):
        table["w_in"] = jnp.swapaxes(table["w_in"], 1, 2)
    return (loss, grads["x"], *[grads[n] for n in order], *[delta[n] for n in order],
            *[new_m[n] for n in order], *[new_v[n] for n in order])
```

```python
import functools
import math

import jax
import jax.numpy as jnp
from jax import lax
from jax.experimental import pallas as pl
from jax.experimental.pallas import tpu as pltpu

F32 = jnp.float32
BF16 = jnp.bfloat16
MESH = pl.DeviceIdType.MESH

D_MODEL = 1024
DEPTH = 2
POOL_WINDOWS = (2, 4, 8, 16)
POOL_GROUP = 128
POOL_DIM = 512
N_HEADS = 8
QK_NOPE = 64
QK_ROPE = 32
QK_DIM = QK_NOPE + QK_ROPE
V_DIM = 64
HEAD_PAD = 128
Q_LORA = 384
KV_LORA = 256
ROPE_THETA = 10000.0
ATTN_DIM = N_HEADS * V_DIM
D_FF = 4 * D_MODEL
N_MOD = 6
EPS = 1e-6
N_CHIPS = 4
N_DEV = 8

ADAM_LR = 0.001
ADAM_B1 = 0.9
ADAM_B2 = 0.999
ADAM_EPS = 1e-08
ADAM_WD = 0.01
ADAM_STEP = 10

VMEM_LIMIT_BYTES = 56 * 1024 * 1024
LANES = 128
HALO = 16

ZC_CQ = 0
ZC_KR = 384
ZC_U = 512
ZC_GA = 1024
ZC_GB = 2048
ZC_CKV = 3072
Z_DIM = 3328
Z_OFFSETS = dict(cq=ZC_CQ, kr=ZC_KR, u=ZC_U, ga=ZC_GA, gb=ZC_GB, ckv=ZC_CKV)

W_IN_U, W_IN_CQ, W_IN_CKV, W_IN_KR, W_IN_GATES = (0, 512), (512, 896), (896, 1152), (1152, 1184), (1184, 3232)
W_IN_COLS = W_IN_GATES[1]
W_IN_SHARD = W_IN_COLS // N_CHIPS

ROWS_OF = dict(w_uq=72, w_uk=32, w_uv=32, p_pool=128, p_attn=128, w_out=256)


def _params(sem=None, **kw):
    return pltpu.CompilerParams(dimension_semantics=sem, vmem_limit_bytes=VMEM_LIMIT_BYTES, **kw)


def _tile(n, target, unit=LANES):
    best = None
    for t in range(unit, min(n, target) + 1, unit):
        if n % t == 0:
            best = t
    return best if best is not None and 4 * best >= min(n, target) else n


def _near_tile(n, target):
    cands = [t for t in range(LANES, n + 1, LANES) if n % t == 0]
    return min(cands, key=lambda t: abs(math.log(t / target))) if cands else n


def _mm(a, b, *, name, ta=False, tb=False, out_dtypes=(F32,), epilogue=None, extras=(), tm=1024, tn=1024, tk=1024,
        second=None, b_stack=False, out_stack=None):
    (k_dim, m_dim) = a.shape if ta else a.shape[::-1]
    if b_stack:
        g_b, k_b, n_shard = b.shape
        n_dim, k_b = (k_b, g_b * n_shard) if tb else (g_b * n_shard, k_b)
    else:
        (n_dim, k_b) = b.shape if tb else b.shape[::-1]
    assert k_dim == k_b, (a.shape, b.shape)
    n_unit = n_shard if b_stack and not tb else n_dim // out_stack if out_stack else n_dim
    k_unit = n_shard if b_stack and tb else k_dim
    tm, tn, tk = _near_tile(m_dim, tm), _near_tile(n_unit, tn), _near_tile(k_unit, tk)
    nk = k_dim // tk
    n_extra, n_out = len(extras), len(out_dtypes)
    n_lhs = 4 if second else 2
    dims = (((0 if ta else 1,), (1 if tb else 0,)), ((), ()))
    if epilogue is None:
        epilogue = lambda acc: (acc,) * n_out

    def body(*refs):
        operand_refs, rest = refs[:n_lhs], refs[n_lhs:]
        extra_refs, out_refs = rest[:n_extra], rest[n_extra:n_extra + n_out]

        def product():
            total = None
            for a_ref, b_ref in zip(operand_refs[0::2], operand_refs[1::2]):
                part = lax.dot_general(a_ref[...].astype(BF16), b_ref[...].astype(BF16), dims, preferred_element_type=F32)
                total = part if total is None else total + part
            return total

        def finish(acc):
            outs = epilogue(acc, *[r[...] for r in extra_refs])
            for o_ref, o in zip(out_refs, outs):
                o_ref[...] = o.astype(o_ref.dtype)

        if nk == 1:
            finish(product())
            return
        acc_ref = rest[-1]
        k = pl.program_id(2)

        @pl.when(k == 0)
        def _():
            acc_ref[...] = product()

        @pl.when((k > 0) & (k < nk - 1))
        def _():
            acc_ref[...] += product()

        @pl.when(k == nk - 1)
        def _():
            finish(acc_ref[...] + product())

    a_spec = pl.BlockSpec((tk, tm), lambda i, j, k: (k, i)) if ta else pl.BlockSpec((tm, tk), lambda i, j, k: (i, k))
    if b_stack and tb:
        per = n_shard // tk
        b_spec = pl.BlockSpec((None, tn, tk), lambda i, j, k: (k // per, j, k % per))
    elif b_stack:
        per = n_shard // tn
        b_spec = pl.BlockSpec((None, tk, tn), lambda i, j, k: (j // per, k, j % per))
    elif tb:
        b_spec = pl.BlockSpec((tn, tk), lambda i, j, k: (j, k))
    else:
        b_spec = pl.BlockSpec((tk, tn), lambda i, j, k: (k, j))
    if out_stack:
        per_out = (n_dim // out_stack) // tn
        out_spec = pl.BlockSpec((None, tm, tn), lambda i, j, k: (j // per_out, i, j % per_out))
        out_dims = (out_stack, m_dim, n_dim // out_stack)
    else:
        out_spec = pl.BlockSpec((tm, tn), lambda i, j, k: (i, j))
        out_dims = (m_dim, n_dim)
    extra_specs = []
    for arr, kind in extras:
        if kind == "tile":
            extra_specs.append(pl.BlockSpec((tm, tn), lambda i, j, k: (i, j)))
        elif isinstance(kind, tuple):
            extra_specs.append(pl.BlockSpec((tm, tn), functools.partial(lambda i, j, k, c: (i, j + c), c=kind[1])))
        elif kind == "row":
            extra_specs.append(pl.BlockSpec((1, tn), lambda i, j, k: (0, j)))
        elif kind == "col":
            extra_specs.append(pl.BlockSpec((tm, 1), lambda i, j, k: (i, 0)))
        else:
            assert kind == "table", kind
            extra_specs.append(pl.BlockSpec((tm, LANES), lambda i, j, k: (i, 0)))
    return pl.pallas_call(
        body,
        name=name,
        grid=(m_dim // tm, n_dim // tn, nk),
        in_specs=[a_spec, b_spec] * (n_lhs // 2) + extra_specs,
        out_specs=[out_spec for _ in out_dtypes],
        out_shape=[jax.ShapeDtypeStruct(out_dims, dt) for dt in out_dtypes],
        scratch_shapes=[pltpu.VMEM((tm, tn), F32)] if nk > 1 else [],
        compiler_params=_params(("parallel", "parallel", "arbitrary")),
    )(a, b, *(second or ()), *[arr for arr, _ in extras])


def _mm_sum(pieces, b, offsets, *, name, tm=1024, tn=1024):
    m_dim, n_dim = pieces[0].shape[0], b.shape[1]
    tm, tn = _near_tile(m_dim, tm), _near_tile(n_dim, tn)
    n_pieces = len(pieces)

    def body(*refs):
        total = None
        for a_ref, b_ref in zip(refs[:n_pieces], refs[n_pieces:2 * n_pieces]):
            part = jnp.dot(a_ref[...], b_ref[...], preferred_element_type=F32)
            total = part if total is None else total + part
        refs[-1][...] = total

    a_specs = [pl.BlockSpec((tm, p.shape[1]), lambda i, j: (i, 0)) for p in pieces]
    b_specs = [pl.BlockSpec((p.shape[1], tn), functools.partial(lambda i, j, blk: (blk, j), blk=off // p.shape[1]))
               for p, off in zip(pieces, offsets)]
    return pl.pallas_call(
        body, name=name, grid=(m_dim // tm, n_dim // tn),
        in_specs=a_specs + b_specs,
        out_specs=pl.BlockSpec((tm, tn), lambda i, j: (i, j)),
        out_shape=jax.ShapeDtypeStruct((m_dim, n_dim), F32),
        compiler_params=_params(("parallel", "parallel")),
    )(*pieces, *[b] * n_pieces)


def _rows(s):
    return min(512, s)


def _rope_tables(pos_col, inv_freq_lanes, *, name):
    s = pos_col.shape[0]
    tb = _rows(s)

    def body(pos_ref, f_ref, cos_ref, sin_ref):
        ang = pos_ref[...].astype(F32) * f_ref[...]
        lane = lax.broadcasted_iota(jnp.int32, ang.shape, 1)
        on = (lane >= QK_NOPE) & (lane < QK_DIM)
        cos_ref[...] = jnp.where(on, jnp.cos(ang), 0.0)
        sin_ref[...] = jnp.where(on, jnp.sin(ang), 0.0)

    return pl.pallas_call(
        body, name=name, grid=(s // tb,),
        in_specs=[pl.BlockSpec((tb, 1), lambda i: (i, 0)), pl.BlockSpec((1, LANES), lambda i: (0, 0))],
        out_specs=[pl.BlockSpec((tb, LANES), lambda i: (i, 0))] * 2,
        out_shape=[jax.ShapeDtypeStruct((s, LANES), F32)] * 2,
        compiler_params=_params(("parallel",)),
    )(pos_col, inv_freq_lanes)


def _rotate_half(x):
    lane = lax.broadcasted_iota(jnp.int32, x.shape, 1)
    half = QK_ROPE // 2
    first = (lane >= QK_NOPE) & (lane < QK_NOPE + half)
    second = (lane >= QK_NOPE + half) & (lane < QK_DIM)
    return jnp.where(first, -pltpu.roll(x, LANES - half, 1), jnp.where(second, pltpu.roll(x, half, 1), 0.0))


def _norm_mod(x, g, sc, sh, *, name):
    s, d = x.shape
    tb = _rows(s)

    def body(x_ref, g_ref, sc_ref, sh_ref, h_ref, r_ref):
        xv = x_ref[...]
        r = lax.rsqrt(jnp.mean(xv * xv, axis=-1, keepdims=True) + EPS)
        r_ref[...] = r
        h_ref[...] = (((xv * r) * g_ref[...]) * (1.0 + sc_ref[...]) + sh_ref[...]).astype(BF16)

    vec = pl.BlockSpec((1, d), lambda i: (0, 0))
    return pl.pallas_call(
        body, name=name, grid=(s // tb,),
        in_specs=[pl.BlockSpec((tb, d), lambda i: (i, 0)), vec, vec, vec],
        out_specs=[pl.BlockSpec((tb, d), lambda i: (i, 0)), pl.BlockSpec((tb, 1), lambda i: (i, 0))],
        out_shape=[jax.ShapeDtypeStruct((s, d), BF16), jax.ShapeDtypeStruct((s, 1), F32)],
        compiler_params=_params(("parallel",)),
    )(x, g, sc, sh)


def _window_sums(ext, sign):
    n = ext.shape[0]
    sums, cur, k = [], ext, 1
    for _ in POOL_WINDOWS:
        cur = cur + pltpu.roll(cur, k if sign > 0 else n - k, 0)
        sums.append(cur)
        k *= 2
    return sums


def _mixer_pre(z, cos_t, sin_t, w_pool, pool_scale, gq, gkv, *, name):
    s = z.shape[0]
    tb = _rows(s)
    hb = tb // HALO

    def body(zcq_ref, zkr_ref, zu_ref, zuh_ref, zckv_ref, cos_ref, sin_ref, wp_ref, ps_ref, gq_ref, gkv_ref,
             p_ref, yp_ref, cq_ref, ckv_ref, kr_ref, rq_ref, rkv_ref):
        i = pl.program_id(0)
        u = zu_ref[...].astype(F32)
        halo = jnp.where(i > 0, zuh_ref[...].astype(F32), 0.0)
        ext = jnp.concatenate([halo, u], axis=0)
        t = i * tb + lax.broadcasted_iota(jnp.int32, (tb, 1), 0)
        for g, (w, sw) in enumerate(zip(POOL_WINDOWS, _window_sums(ext, +1))):
            cols = slice(g * POOL_GROUP, (g + 1) * POOL_GROUP)
            cnt = jnp.minimum(t + 1, w).astype(F32)
            pg = (sw[HALO:, cols] / cnt - u[:, cols]).astype(BF16)
            p_ref[:, cols] = pg
            yg = jnp.dot(pg, wp_ref[g].astype(BF16), preferred_element_type=F32)
            yp_ref[:, cols] = (yg * ps_ref[:, cols]).astype(BF16)

        def rms(x_ref, g_ref, out_ref, r_ref):
            xv = x_ref[...].astype(F32)
            r = lax.rsqrt(jnp.mean(xv * xv, axis=-1, keepdims=True) + EPS)
            r_ref[...] = r
            out_ref[...] = ((xv * r) * g_ref[...]).astype(BF16)

        rms(zcq_ref, gq_ref, cq_ref, rq_ref)
        rms(zckv_ref, gkv_ref, ckv_ref, rkv_ref)
        kr = zkr_ref[...].astype(F32)
        kr_ref[...] = (kr * cos_ref[...] + _rotate_half(kr) * sin_ref[...]).astype(BF16)

    def zcol(width, off):
        return pl.BlockSpec((tb, width), lambda i: (i, off // width))

    def full(a):
        return pl.BlockSpec(a.shape, lambda i: (0,) * a.ndim)

    def out(width, dt):
        return pl.BlockSpec((tb, width), lambda i: (i, 0)), jax.ShapeDtypeStruct((s, width), dt)

    outs = [out(POOL_DIM, BF16), out(POOL_DIM, BF16), out(Q_LORA, BF16), out(KV_LORA, BF16), out(LANES, BF16),
            out(1, F32), out(1, F32)]
    return pl.pallas_call(
        body, name=name, grid=(s // tb,),
        in_specs=[zcol(Q_LORA, ZC_CQ), zcol(LANES, ZC_KR), zcol(POOL_DIM, ZC_U),
                  pl.BlockSpec((HALO, POOL_DIM), lambda i: (jnp.maximum(i * hb - 1, 0), ZC_U // POOL_DIM)),
                  zcol(KV_LORA, ZC_CKV),
                  pl.BlockSpec((tb, LANES), lambda i: (i, 0)), pl.BlockSpec((tb, LANES), lambda i: (i, 0)),
                  full(w_pool), full(pool_scale), full(gq), full(gkv)],
        out_specs=[o[0] for o in outs], out_shape=[o[1] for o in outs],
        compiler_params=_params(("parallel",)),
    )(z, z, z, z, z, cos_t, sin_t, w_pool, pool_scale, gq, gkv)


def _sigmoid(x):
    return 1.0 / (1.0 + jnp.exp(-x.astype(F32)))


ATTN_SCALE = 1.0 / math.sqrt(QK_DIM)
NEG_BIG = -1e30


LOG2_E = math.log2(math.e)
EXP2_SCALE = ATTN_SCALE * LOG2_E
NT_DIMS = (((1,), (1,)), ((), ()))
TN_DIMS = (((0,), (0,)), ((), ()))


def _on_or_below_diagonal(t):
    return lax.broadcasted_iota(jnp.int32, (t, t), 0) >= lax.broadcasted_iota(jnp.int32, (t, t), 1)


HEADS_PER_STEP = 2
HEAD_COLS = [slice(g * HEAD_PAD, (g + 1) * HEAD_PAD) for g in range(HEADS_PER_STEP)]


def _attn_fwd(q, k, v, *, name):
    s = q.shape[0]
    t = _rows(s)
    wide = HEADS_PER_STEP * HEAD_PAD

    def body(q_ref, k_ref, v_ref, o_ref, lse_ref):
        qi = pl.program_id(1)
        qs = [q_ref[:, cols] for cols in HEAD_COLS]

        def block(j, carry, diagonal):
            rows = pl.ds(pl.multiple_of(j * t, t), t)
            out = []
            for qv, cols, (m, l, acc) in zip(qs, HEAD_COLS, carry):
                sc = lax.dot_general(qv, k_ref[rows, cols], NT_DIMS, preferred_element_type=F32)
                if diagonal:
                    sc = jnp.where(_on_or_below_diagonal(t), sc, NEG_BIG)
                m_new = jnp.maximum(m, jnp.max(sc, axis=-1, keepdims=True))
                p = jnp.exp2((sc - m_new) * EXP2_SCALE)
                alpha = jnp.exp2((m - m_new) * EXP2_SCALE)
                l = alpha * l + jnp.sum(p, axis=-1, keepdims=True)
                acc = alpha * acc + jnp.dot(p.astype(BF16), v_ref[rows, cols], preferred_element_type=F32)
                out.append((m_new, l, acc))
            return tuple(out)

        init = tuple((jnp.full((t, 1), -jnp.inf, F32), jnp.zeros((t, 1), F32), jnp.zeros((t, HEAD_PAD), F32))
                     for _ in HEAD_COLS)
        carry = lax.fori_loop(0, qi, lambda j, c: block(j, c, False), init)
        for g, (cols, (m, l, acc)) in enumerate(zip(HEAD_COLS, block(qi, carry, True))):
            o_ref[:, cols] = (acc / l).astype(BF16)
            lse_ref[g] = m * ATTN_SCALE + jnp.log(l)

    q_spec = pl.BlockSpec((t, wide), lambda h, i: (i, h))
    kv_spec = pl.BlockSpec((s, wide), lambda h, i: (0, h))
    return pl.pallas_call(
        body, name=name, grid=(N_HEADS // HEADS_PER_STEP, s // t),
        in_specs=[q_spec, kv_spec, kv_spec],
        out_specs=[q_spec, pl.BlockSpec((HEADS_PER_STEP, t, 1), lambda h, i: (h, i, 0))],
        out_shape=[jax.ShapeDtypeStruct((s, N_HEADS * HEAD_PAD), BF16), jax.ShapeDtypeStruct((N_HEADS, s, 1), F32)],
        compiler_params=_params(("parallel", "parallel")),
    )(q, k, v)


def _attn_delta(do, o, *, name):
    s = o.shape[0]
    t = _rows(s)

    def body(do_ref, o_ref, out_ref):
        for h in range(N_HEADS):
            cols = slice(h * HEAD_PAD, (h + 1) * HEAD_PAD)
            out_ref[h] = jnp.sum(do_ref[:, cols].astype(F32) * o_ref[:, cols].astype(F32), axis=-1, keepdims=True)

    blk = pl.BlockSpec((t, N_HEADS * HEAD_PAD), lambda i: (i, 0))
    return pl.pallas_call(
        body, name=name, grid=(s // t,), in_specs=[blk, blk],
        out_specs=pl.BlockSpec((N_HEADS, t, 1), lambda i: (0, i, 0)),
        out_shape=jax.ShapeDtypeStruct((N_HEADS, s, 1), F32),
        compiler_params=_params(("parallel",)),
    )(do, o)


def _attn_bwd(q, k, v, do, lse, delta, cos_t, sin_t, *, name):
    s = q.shape[0]
    t = _rows(s)
    nt = s // t

    def body(q_ref, k_ref, v_ref, do_ref, lse_ref, dl_ref, cos_ref, sin_ref, dql_ref, dk_ref, dv_ref, dq_ref):
        kj = pl.program_id(1)

        @pl.when(kj == 0)
        def _():
            dq_ref[...] = jnp.zeros_like(dq_ref)

        kvs = [(k_ref[:, cols], v_ref[:, cols]) for cols in HEAD_COLS]

        def block(i, carry, diagonal):
            rows = pl.ds(pl.multiple_of(i * t, t), t)
            out = []
            for g, (cols, (kv, vv), (dk, dv)) in enumerate(zip(HEAD_COLS, kvs, carry)):
                qv, dov = q_ref[rows, cols], do_ref[rows, cols]
                sc = lax.dot_general(qv, kv, NT_DIMS, preferred_element_type=F32)
                p = jnp.exp2(sc * EXP2_SCALE - lse_ref[g, rows, :] * LOG2_E)
                if diagonal:
                    p = jnp.where(_on_or_below_diagonal(t), p, 0.0)
                dp = lax.dot_general(dov, vv, NT_DIMS, preferred_element_type=F32)
                ds = (p * (dp - dl_ref[g, rows, :])).astype(BF16)
                dv = dv + lax.dot_general(p.astype(BF16), dov, TN_DIMS, preferred_element_type=F32)
                dk = dk + lax.dot_general(ds, qv, TN_DIMS, preferred_element_type=F32)
                dq_ref[rows, cols] += jnp.dot(ds, kv, preferred_element_type=F32) * ATTN_SCALE
                out.append((dk, dv))
            return tuple(out)

        zero = jnp.zeros((t, HEAD_PAD), F32)
        carry = block(kj, tuple((zero, zero) for _ in HEAD_COLS), True)
        for cols, (dk, dv) in zip(HEAD_COLS, lax.fori_loop(kj + 1, nt, lambda i, c: block(i, c, False), carry)):
            dk_ref[:, cols] = dk * ATTN_SCALE
            dv_ref[:, cols] = dv.astype(BF16)

        @pl.when(kj == nt - 1)
        def _():
            def rope_bwd(i, carry):
                rows = pl.ds(pl.multiple_of(i * t, t), t)
                sin = sin_ref[rows, :]
                lane = lax.broadcasted_iota(jnp.int32, sin.shape, 1)
                cos_q = cos_ref[rows, :] + jnp.where(lane < QK_NOPE, 1.0, 0.0)
                for cols in HEAD_COLS:
                    dqv = dq_ref[rows, cols]
                    dql_ref[rows, cols] = (dqv * cos_q - _rotate_half(dqv * sin)).astype(BF16)
                return carry

            lax.fori_loop(0, nt, rope_bwd, 0)

    heads_wide = HEADS_PER_STEP * HEAD_PAD
    full_spec = pl.BlockSpec((s, heads_wide), lambda h, j: (0, h))
    kv_spec = pl.BlockSpec((t, heads_wide), lambda h, j: (j, h))
    vec_spec = pl.BlockSpec((HEADS_PER_STEP, s, 1), lambda h, j: (h, 0, 0))
    table_spec = pl.BlockSpec((s, LANES), lambda h, j: (0, 0))
    wide = (s, N_HEADS * HEAD_PAD)
    return pl.pallas_call(
        body, name=name, grid=(N_HEADS // HEADS_PER_STEP, nt),
        in_specs=[full_spec, kv_spec, kv_spec, full_spec, vec_spec, vec_spec, table_spec, table_spec],
        out_specs=[full_spec, kv_spec, kv_spec],
        out_shape=[jax.ShapeDtypeStruct(wide, BF16), jax.ShapeDtypeStruct(wide, F32), jax.ShapeDtypeStruct(wide, BF16)],
        scratch_shapes=[pltpu.VMEM((s, heads_wide), F32)],
        compiler_params=_params(("parallel", "arbitrary")),
    )(q, k, v, do, lse, delta, cos_t, sin_t)


def _acc_specs(widths):
    return ([pl.BlockSpec((1, w), lambda i: (0, 0)) for w in widths],
            [jax.ShapeDtypeStruct((1, w), F32) for w in widths])


def _gate_grads(dxv, m_ref, gate_ref, dm_ref, dgate_ref):
    dm_ref[...] = (dxv * gate_ref[...]).astype(BF16)
    dgate_ref[...] += jnp.sum(dxv * m_ref[...], axis=0, keepdims=True)


def _final_loss(x, g, target, m, gate, *, name):
    s, d = x.shape
    tb = _rows(s)

    def body(x_ref, g_ref, t_ref, m_ref, gate_ref, dx_ref, loss_ref, dg_ref, dm_ref, dgate_ref):
        @pl.when(pl.program_id(0) == 0)
        def _():
            loss_ref[...] = jnp.zeros_like(loss_ref)
            dg_ref[...] = jnp.zeros_like(dg_ref)
            dgate_ref[...] = jnp.zeros_like(dgate_ref)

        xv = x_ref[...]
        r = lax.rsqrt(jnp.mean(xv * xv, axis=-1, keepdims=True) + EPS)
        xn = xv * r
        err = xn * g_ref[...] - t_ref[...]
        loss_ref[...] += 0.5 * jnp.sum(jnp.mean(err * err, axis=-1, keepdims=True), axis=0, keepdims=True)
        dy = err / d
        dg_ref[...] += jnp.sum(dy * xn, axis=0, keepdims=True)
        dxn = dy * g_ref[...]
        dxv = r * (dxn - xn * jnp.mean(dxn * xn, axis=-1, keepdims=True))
        dx_ref[...] = dxv
        _gate_grads(dxv, m_ref, gate_ref, dm_ref, dgate_ref)

    blk = pl.BlockSpec((tb, d), lambda i: (i, 0))
    vec = pl.BlockSpec((1, d), lambda i: (0, 0))
    acc_specs, acc_shapes = _acc_specs((LANES, d))
    return pl.pallas_call(
        body, name=name, grid=(s // tb,),
        in_specs=[blk, vec, blk, blk, vec],
        out_specs=[blk] + acc_specs + [blk, vec],
        out_shape=[jax.ShapeDtypeStruct((s, d), F32)] + acc_shapes + [jax.ShapeDtypeStruct((s, d), BF16),
                                                                     jax.ShapeDtypeStruct((1, d), F32)],
        compiler_params=_params(("arbitrary",)),
    )(x, g, target, m, gate)


def _norm_mod_bwd(dh, x, r, g, sc, dx_skip, *, name, gate=None):
    s, d = x.shape
    tb = _rows(s)
    nb = s // tb
    n_gate = 2 if gate else 0

    def body(dh_ref, x_ref, r_ref, g_ref, sc_ref, skip_ref, *rest):
        gate_refs, (dx_ref, dg_ref, dsc_ref, dsh_ref) = rest[:n_gate], rest[n_gate:n_gate + 4]
        gate_outs, da_sc = rest[n_gate + 4:-1], rest[-1]
        i = pl.program_id(0)

        @pl.when(i == 0)
        def _():
            da_sc[...] = jnp.zeros_like(da_sc)
            dsh_ref[...] = jnp.zeros_like(dsh_ref)
            if gate:
                gate_outs[1][...] = jnp.zeros_like(gate_outs[1])

        dhv, rv = dh_ref[...], r_ref[...]
        xn = x_ref[...] * rv
        dsh_ref[...] += jnp.sum(dhv, axis=0, keepdims=True)
        da_sc[...] += jnp.sum(dhv * xn, axis=0, keepdims=True)
        dxn = dhv * (g_ref[...] * (1.0 + sc_ref[...]))
        dxv = skip_ref[...] + rv * (dxn - xn * jnp.mean(dxn * xn, axis=-1, keepdims=True))
        dx_ref[...] = dxv
        if gate:
            _gate_grads(dxv, *gate_refs, *gate_outs)

        @pl.when(i == nb - 1)
        def _():
            dg_ref[...] = da_sc[...] * (1.0 + sc_ref[...])
            dsc_ref[...] = da_sc[...] * g_ref[...]

    blk = pl.BlockSpec((tb, d), lambda i: (i, 0))
    vec = pl.BlockSpec((1, d), lambda i: (0, 0))
    acc_specs, acc_shapes = _acc_specs((d, d, d))
    gate_specs = [blk, vec] if gate else []
    gate_shapes = [jax.ShapeDtypeStruct((s, d), BF16), jax.ShapeDtypeStruct((1, d), F32)] if gate else []
    return pl.pallas_call(
        body, name=name, grid=(nb,),
        in_specs=[blk, blk, pl.BlockSpec((tb, 1), lambda i: (i, 0)), vec, vec, blk] + gate_specs,
        out_specs=[blk] + acc_specs + gate_specs,
        out_shape=[jax.ShapeDtypeStruct((s, d), F32)] + acc_shapes + gate_shapes,
        scratch_shapes=[pltpu.VMEM((1, d), F32)],
        compiler_params=_params(("arbitrary",)),
    )(dh, x, r, g, sc, dx_skip, *(gate or ()))


def _pool_bwd(dyp, p, w_pool, pool_scale, *, name):
    s = dyp.shape[0]
    tb = _rows(s)
    nb = s // tb
    hb = tb // HALO
    nt_dims = (((1,), (1,)), ((), ()))
    tn_dims = (((0,), (0,)), ((), ()))

    def body(dy_ref, dyn_ref, p_ref, wp_ref, ps_ref, du_ref, gwp_ref, gps_ref):
        i = pl.program_id(0)

        @pl.when(i == 0)
        def _():
            gwp_ref[...] = jnp.zeros_like(gwp_ref)
            gps_ref[...] = jnp.zeros_like(gps_ref)

        cur = dy_ref[...]
        nxt = jnp.where(i < nb - 1, dyn_ref[...], 0.0)
        dpw = (jnp.concatenate([cur, nxt], axis=0) * ps_ref[...]).astype(BF16)
        t = i * tb + lax.broadcasted_iota(jnp.int32, (tb + HALO, 1), 0)
        for g, w in enumerate(POOL_WINDOWS):
            cols = slice(g * POOL_GROUP, (g + 1) * POOL_GROUP)
            wg = wp_ref[g].astype(BF16)
            dp = lax.dot_general(dpw[:, cols], wg, nt_dims, preferred_element_type=F32)
            e = dp / jnp.minimum(t + 1, w).astype(F32)
            lead = _window_sums(e, -1)[g]
            du_ref[:, cols] = (lead[:tb] - dp[:tb]).astype(BF16)
            pg = p_ref[:, cols]
            pw = jnp.dot(pg, wg, preferred_element_type=F32)
            gps_ref[:, cols] += jnp.sum(cur[:, cols] * pw, axis=0, keepdims=True)
            gwp_ref[g] += lax.dot_general(pg, dpw[:tb, cols], tn_dims, preferred_element_type=F32)

    blk = pl.BlockSpec((tb, POOL_DIM), lambda i: (i, 0))
    return pl.pallas_call(
        body, name=name, grid=(nb,),
        in_specs=[blk, pl.BlockSpec((HALO, POOL_DIM), lambda i: (jnp.minimum((i + 1) * hb, s // HALO - 1), 0)), blk,
                  pl.BlockSpec(w_pool.shape, lambda i: (0, 0, 0)), pl.BlockSpec((1, POOL_DIM), lambda i: (0, 0))],
        out_specs=[blk, pl.BlockSpec(w_pool.shape, lambda i: (0, 0, 0)), pl.BlockSpec((1, POOL_DIM), lambda i: (0, 0))],
        out_shape=[jax.ShapeDtypeStruct((s, POOL_DIM), BF16), jax.ShapeDtypeStruct(w_pool.shape, F32),
                   jax.ShapeDtypeStruct((1, POOL_DIM), F32)],
        compiler_params=_params(("arbitrary",)),
    )(dyp, dyp, p, w_pool, pool_scale)


def _key_bwd(dk, cos_t, sin_t, *, name):
    s = dk.shape[0]
    tb = _rows(s)

    def body(dk_ref, cos_ref, sin_ref, dkb_ref, dkr_ref):
        dkv = dk_ref[...]
        dkb_ref[...] = dkv.astype(BF16)
        tot = dkv[:, :HEAD_PAD]
        for h in range(1, N_HEADS):
            tot = tot + dkv[:, h * HEAD_PAD:(h + 1) * HEAD_PAD]
        dkr_ref[...] = (tot * cos_ref[...] - _rotate_half(tot * sin_ref[...])).astype(BF16)

    blk = pl.BlockSpec((tb, N_HEADS * HEAD_PAD), lambda i: (i, 0))
    tab = pl.BlockSpec((tb, LANES), lambda i: (i, 0))
    return pl.pallas_call(
        body, name=name, grid=(s // tb,), in_specs=[blk, tab, tab], out_specs=[blk, tab],
        out_shape=[jax.ShapeDtypeStruct(dk.shape, BF16), jax.ShapeDtypeStruct((s, LANES), BF16)],
        compiler_params=_params(("parallel",)),
    )(dk, cos_t, sin_t)


def _rms_bwd(dy, z, z_off, r, g, *, name):
    s, n = dy.shape
    tb = _rows(s)

    def body(dy_ref, x_ref, r_ref, g_ref, dx_ref, dg_ref):
        @pl.when(pl.program_id(0) == 0)
        def _():
            dg_ref[...] = jnp.zeros_like(dg_ref)

        dyv, rv = dy_ref[...], r_ref[...]
        xn = x_ref[...].astype(F32) * rv
        dg_ref[...] += jnp.sum(dyv * xn, axis=0, keepdims=True)
        dxn = dyv * g_ref[...]
        dx_ref[...] = (rv * (dxn - xn * jnp.mean(dxn * xn, axis=-1, keepdims=True))).astype(BF16)

    blk = pl.BlockSpec((tb, n), lambda i: (i, 0))
    acc_specs, acc_shapes = _acc_specs((n,))
    return pl.pallas_call(
        body, name=name, grid=(s // tb,),
        in_specs=[blk, pl.BlockSpec((tb, n), lambda i: (i, z_off // n)), pl.BlockSpec((tb, 1), lambda i: (i, 0)),
                  pl.BlockSpec((1, n), lambda i: (0, 0))],
        out_specs=[blk] + acc_specs, out_shape=[jax.ShapeDtypeStruct((s, n), BF16)] + acc_shapes,
        compiler_params=_params(("arbitrary",)),
    )(dy, z, r, g)


def _silu(c, *, name):
    def body(c_ref, out_ref):
        cv = c_ref[...]
        out_ref[...] = (cv * _sigmoid(cv)).astype(BF16)

    return pl.pallas_call(body, name=name, out_shape=jax.ShapeDtypeStruct(c.shape, BF16),
                          compiler_params=_params())(c)


def _sum_slots(a, n, *, name, out_dtype=F32):
    _, rows, cols = a.shape
    tr = _tile(rows, 256, 8)

    def body(a_ref, out_ref):
        tot = a_ref[0].astype(F32)
        for j in range(1, n):
            tot = tot + a_ref[j].astype(F32)
        out_ref[...] = tot.astype(out_dtype)

    return pl.pallas_call(
        body, name=name, grid=(rows // tr,),
        in_specs=[pl.BlockSpec((n, tr, cols), lambda i: (0, i, 0))],
        out_specs=pl.BlockSpec((tr, cols), lambda i: (i, 0)),
        out_shape=jax.ShapeDtypeStruct((rows, cols), out_dtype),
        compiler_params=_params(("parallel",)),
    )(a)


def _add2_stacked(a, b, stacked, l, *, name):
    rows, cols = a.shape
    tr = _tile(rows, 256, 8)

    def body(a_ref, b_ref, *rest):
        rest[-1][...] = a_ref[...] + b_ref[...]

    blk = pl.BlockSpec((tr, cols), lambda i: (i, 0))
    carried = [] if stacked is None else [stacked]
    return pl.pallas_call(
        body, name=name, grid=(rows // tr,),
        in_specs=[blk, blk] + [pl.BlockSpec(memory_space=pl.ANY) for _ in carried],
        out_specs=pl.BlockSpec((None, tr, cols), lambda i: (l, i, 0)),
        out_shape=jax.ShapeDtypeStruct((DEPTH, rows, cols), F32),
        input_output_aliases={2: 0} if carried else {},
        compiler_params=_params(("parallel",)),
    )(a, b, *carried)


def _adamw(w, g, m, v, *, name):
    shape = w.shape
    if w.ndim == 2:
        w, g, m, v = (a.reshape((1,) + shape) for a in (w, g, m, v))
    layers, rows, cols = w.shape
    tr = _tile(rows, max(8, (1 << 18) // cols), 8)
    c1 = 1.0 - ADAM_B1 ** ADAM_STEP
    c2 = 1.0 - ADAM_B2 ** ADAM_STEP

    def body(w_ref, g_ref, m_ref, v_ref, d_ref, nm_ref, nv_ref):
        gv = g_ref[...]
        nm = ADAM_B1 * m_ref[...] + (1.0 - ADAM_B1) * gv
        nv = ADAM_B2 * v_ref[...] + (1.0 - ADAM_B2) * (gv * gv)
        nm_ref[...] = nm
        nv_ref[...] = nv
        d_ref[...] = -ADAM_LR * ((nm / c1) / (jnp.sqrt(nv / c2) + ADAM_EPS) + ADAM_WD * w_ref[...])

    blk = pl.BlockSpec((None, tr, cols), lambda l, i: (l, i, 0))
    outs = pl.pallas_call(
        body, name=name, grid=(layers, rows // tr), in_specs=[blk] * 4, out_specs=[blk] * 3,
        out_shape=[jax.ShapeDtypeStruct((layers, rows, cols), F32)] * 3,
        compiler_params=_params(("parallel", "parallel")),
    )(w, g, m, v)
    return [o.reshape(shape) for o in outs]


def _coords():
    return lax.axis_index("x"), lax.axis_index("y"), lax.axis_index("c")


def _other_chips(x, y):
    return [(1 - x, y), (x, 1 - y), (1 - x, 1 - y)]


def _all_gather_small(blk, *, name):
    m_per, n = blk.shape

    def body(x_ref, out_ref, send_sems, recv_sems, local_sem):
        x, y, c = _coords()
        me, sibling = (x, y, c), (x, y, 1 - c)
        chips = _other_chips(x, y)

        def rows(px, py, pc):
            return out_ref.at[pl.ds((4 * px + 2 * py + pc) * m_per, m_per), :]

        def copy(k, block, to, src=None):
            return pltpu.make_async_remote_copy(
                src_ref=rows(*block) if src is None else src, dst_ref=rows(*block),
                send_sem=send_sems.at[k], recv_sem=recv_sems.at[k], device_id=to, device_id_type=MESH)

        mine = pltpu.make_async_copy(x_ref, rows(*me), local_sem)
        mine.start()
        first = [copy(0, me, sibling, src=x_ref)]
        first += [copy(1 + j, me, (*chip, c), src=x_ref) for j, chip in enumerate(chips)]
        for cp in first:
            cp.start()
        passed = [copy(4 + j, (*chip, c), sibling) for j, chip in enumerate(chips)]
        for j, chip in enumerate(chips):
            copy(1 + j, (*chip, c), me).wait_recv()
            passed[j].start()
        copy(0, sibling, me).wait_recv()
        for j, chip in enumerate(chips):
            copy(4 + j, (*chip, 1 - c), me).wait_recv()
        for cp in first + passed:
            cp.wait_send()
        mine.wait()

    return pl.pallas_call(
        body, name=name,
        out_shape=jax.ShapeDtypeStruct((N_DEV * m_per, n), blk.dtype),
        in_specs=[pl.BlockSpec(memory_space=pltpu.VMEM)],
        out_specs=pl.BlockSpec(memory_space=pltpu.VMEM),
        scratch_shapes=[pltpu.SemaphoreType.DMA((7,)), pltpu.SemaphoreType.DMA((7,)), pltpu.SemaphoreType.DMA],
        compiler_params=_params(),
    )(blk)


HBM_SPEC = pl.BlockSpec(memory_space=pltpu.HBM)
SEM_SPEC = pl.BlockSpec(memory_space=pltpu.SEMAPHORE)
DATAFLOW = pltpu.SideEffectType.DATAFLOW_SIDE_EFFECTING


def _chip_copies(src_ref, land_ref, send_sems, recv_sems, scatter):
    x, y, c = _coords()
    my = 2 * x + y
    outgoing, incoming = [], []
    for k, (px, py) in enumerate(_other_chips(x, y)):
        peer = 2 * px + py

        def copy(src_slot, dst_slot):
            return pltpu.make_async_remote_copy(
                src_ref=src_ref.at[src_slot] if scatter else src_ref, dst_ref=land_ref.at[dst_slot],
                send_sem=send_sems.at[k], recv_sem=recv_sems.at[k], device_id=(px, py, c), device_id_type=MESH)

        outgoing.append(copy(peer, my))
        incoming.append(copy(my, peer))
    return outgoing, incoming


def _exchange_start(src, *, name, scatter):
    land_shape = src.shape if scatter else (N_CHIPS,) + src.shape

    def body(src_ref, land_ref, send_sems, recv_sems, src_thru, land_thru, token):
        outgoing, _ = _chip_copies(src_ref, land_ref, send_sems, recv_sems, scatter)
        for cp in outgoing:
            cp.start()
        token[...] = jnp.zeros_like(token)

    return pl.pallas_call(
        body, name=name,
        out_shape=(pltpu.SemaphoreType.DMA((N_CHIPS - 1,)), pltpu.SemaphoreType.DMA((N_CHIPS - 1,)),
                   pltpu.HBM(src.shape, src.dtype), pltpu.HBM(land_shape, src.dtype), jax.ShapeDtypeStruct((8, LANES), F32)),
        in_specs=(HBM_SPEC, HBM_SPEC),
        out_specs=(SEM_SPEC, SEM_SPEC, HBM_SPEC, HBM_SPEC, pl.BlockSpec(memory_space=pltpu.VMEM)),
        input_output_aliases={0: 2, 1: 3},
        compiler_params=pltpu.CompilerParams(has_side_effects=DATAFLOW),
    )(pltpu.with_memory_space_constraint(src, pltpu.HBM),
      pltpu.with_memory_space_constraint(lax.empty(land_shape, src.dtype), pltpu.HBM))


def _exchange_wait(started, after, *, name, scatter):
    send_sems, recv_sems, src_thru, land_thru, _ = started

    def body(src_ref, land_ref, send_sems, recv_sems, after_ref, src_dead, got_ref):
        outgoing, incoming = _chip_copies(src_ref, land_ref, send_sems, recv_sems, scatter)
        for cp in outgoing:
            cp.wait_send()
        for cp in incoming:
            cp.wait_recv()

    return pl.pallas_call(
        body, name=name,
        out_shape=(pltpu.HBM(src_thru.shape, src_thru.dtype), pltpu.HBM(land_thru.shape, land_thru.dtype)),
        in_specs=(HBM_SPEC, HBM_SPEC, SEM_SPEC, SEM_SPEC, pl.BlockSpec(memory_space=pl.ANY)),
        out_specs=(HBM_SPEC, HBM_SPEC),
        input_output_aliases={0: 0, 1: 1},
        compiler_params=pltpu.CompilerParams(has_side_effects=DATAFLOW),
    )(src_thru, land_thru, send_sems, recv_sems, after)


def _exchange_sibling(src, *, name):
    def body(src_ref, out_ref, send_sem, recv_sem):
        x, y, c = _coords()
        cp = pltpu.make_async_remote_copy(src_ref=src_ref, dst_ref=out_ref, send_sem=send_sem, recv_sem=recv_sem,
                                          device_id=(x, y, 1 - c), device_id_type=MESH)
        cp.start()
        cp.wait()

    return pl.pallas_call(
        body, name=name,
        out_shape=jax.ShapeDtypeStruct(src.shape, src.dtype),
        in_specs=[pl.BlockSpec(memory_space=pl.ANY)],
        out_specs=pl.BlockSpec(memory_space=pl.ANY),
        scratch_shapes=[pltpu.SemaphoreType.DMA, pltpu.SemaphoreType.DMA],
        compiler_params=_params(),
    )(src)


def _pack_rows(a):
    return a.reshape(-1, D_MODEL)


def _pad_heads(w, width):
    r = w.shape[0]
    return jnp.pad(w, ((0, 0), (0, 0), (0, HEAD_PAD - width))).reshape(r, N_HEADS * HEAD_PAD)


MIX_NAMES = ("w_uq", "w_uk", "w_uv", "p_pool", "p_attn", "w_out")
GROUPS = ("in", "mix", "ff1", "ff2")


def _local_shard(weights, l, group, zero):
    if group == "mix":
        shard = jnp.concatenate([_pack_rows(weights[n][l]) for n in MIX_NAMES], axis=0)
    else:
        shard = weights[{"in": "w_in", "ff1": "w_ff1", "ff2": "w_ff2"}[group]][l]
    return (shard + zero).astype(BF16)


def _unpack_weights(gathered, group):
    def cols(a, k):
        return a.reshape(N_CHIPS, k, -1).transpose(1, 0, 2).reshape(k, -1)

    if group == "in":
        full = gathered.reshape(W_IN_COLS, D_MODEL)
        u, cq, ckv, kr, gates = (full[a:b] for a, b in (W_IN_U, W_IN_CQ, W_IN_CKV, W_IN_KR, W_IN_GATES))
        kr = jnp.pad(kr, ((QK_NOPE, HEAD_PAD - QK_DIM), (0, 0)))
        return dict(w_in=jnp.concatenate([cq, kr, u, gates, ckv], axis=0))
    if group == "ff1":
        return dict(w_ff1=gathered)
    if group == "ff2":
        return dict(w_ff2=gathered.reshape(D_FF, D_MODEL))

    def p_attn(a):
        full = cols(a, ATTN_DIM).reshape(N_HEADS, V_DIM, D_MODEL)
        return jnp.pad(full, ((0, 0), (0, HEAD_PAD - V_DIM), (0, 0))).reshape(N_HEADS * HEAD_PAD, D_MODEL)

    build = dict(
        w_uq=lambda a: _pad_heads(a.reshape(Q_LORA, N_HEADS, QK_DIM), QK_DIM),
        w_uk=lambda a: _pad_heads(a.reshape(KV_LORA, N_HEADS, QK_NOPE), QK_NOPE),
        w_uv=lambda a: _pad_heads(a.reshape(KV_LORA, N_HEADS, V_DIM), V_DIM),
        p_pool=lambda a: cols(a, POOL_DIM),
        p_attn=p_attn,
        w_out=lambda a: a.reshape(D_MODEL, D_MODEL),
    )
    w, off = {}, 0
    for name in MIX_NAMES:
        w[name] = build[name](gathered[:, off:off + ROWS_OF[name]])
        off += ROWS_OF[name]
    return w


def _pack_grads(g, group):
    def cols(a):
        k = a.shape[0]
        return a.reshape(k, N_CHIPS, -1).transpose(1, 0, 2).reshape(N_CHIPS, -1, D_MODEL)

    def rows(a):
        return a.reshape(N_CHIPS, -1, D_MODEL)

    def heads(width):
        return lambda a: rows(a.reshape(a.shape[0], N_HEADS, HEAD_PAD)[:, :, :width])

    if group == "in":
        full = jnp.concatenate([g["u"], g["cq"], g["ckv"], g["kr"][QK_NOPE:QK_DIM], g["ga"], g["gb"]], axis=0)
        return full.reshape(N_CHIPS, W_IN_SHARD, D_MODEL)
    if group == "ff1":
        return g["w_ff1"]
    if group == "ff2":
        return g["w_ff2"].reshape(N_CHIPS, D_FF // N_CHIPS, D_MODEL)

    def p_attn(a):
        return cols(a.reshape(N_HEADS, HEAD_PAD, D_MODEL)[:, :V_DIM].reshape(ATTN_DIM, D_MODEL))

    build = dict(w_uq=heads(QK_DIM), w_uk=heads(QK_NOPE), w_uv=heads(V_DIM), p_pool=cols, p_attn=p_attn, w_out=rows)
    return jnp.concatenate([build[name](g[name]) for name in MIX_NAMES], axis=1)


def _per_head(fn, acc, *tables):
    return jnp.concatenate([fn(acc[:, h * HEAD_PAD:(h + 1) * HEAD_PAD], *tables) for h in range(N_HEADS)], axis=1)


def _rope_head(a, cos, sin):
    lane = lax.broadcasted_iota(jnp.int32, a.shape, 1)
    return a * (cos + jnp.where(lane < QK_NOPE, 1.0, 0.0)) + _rotate_half(a) * sin


def _layer_fwd(l, x, mod, get_weights, small, cos_t, sin_t):
    sh1, sc1, g1, sh2, sc2, g2 = mod
    tag = f"_l{l}"
    h, r1 = _norm_mod(x, small["ln1_g"], sc1, sh1, name="norm1" + tag)
    w = dict(get_weights("in", h))
    (z,) = _mm(h, w["w_in"], tb=True, name="in_proj" + tag, out_dtypes=(BF16,))
    p, yp, cq, ckv, kr, rq, rkv = _mixer_pre(z, cos_t, sin_t, small["w_pool"], small["pool_scale"],
                                              small["q_norm_g"], small["kv_norm_g"], name="mixer_pre" + tag)
    w.update(get_weights("mix", yp))
    (ya,) = _mm(yp, w["p_pool"], name="pool_out" + tag, out_dtypes=(BF16,))
    (q,) = _mm(cq, w["w_uq"], name="q_proj" + tag, out_dtypes=(BF16,),
               epilogue=lambda acc, cos, sin: (_per_head(_rope_head, acc, cos, sin),),
               extras=((cos_t, "table"), (sin_t, "table")))
    (k,) = _mm(ckv, w["w_uk"], name="k_proj" + tag, out_dtypes=(BF16,),
               epilogue=lambda acc, krv: (_per_head(lambda a, b: a + b, acc, krv),), extras=((kr, "table"),))
    (v,) = _mm(ckv, w["w_uv"], name="v_proj" + tag, out_dtypes=(BF16,))
    o, lse = _attn_fwd(q, k, v, name="attn_fwd" + tag)
    yb, merged = _mm(o, w["p_attn"], name="attn_out" + tag, out_dtypes=(BF16, BF16), tm=512,
                     epilogue=lambda acc, ga, gb, yav: (acc, _sigmoid(ga) * yav + _sigmoid(gb) * acc),
                     extras=((z, ("tile", ZC_GA // D_MODEL)), (z, ("tile", ZC_GB // D_MODEL)), (ya, "tile")))
    mo, x1 = _mm(merged, w["w_out"], name="mix_out" + tag, out_dtypes=(BF16, F32),
                 epilogue=lambda acc, xr, g: (acc, xr + g * acc), extras=((x, "tile"), (g1, "row")))
    h2, r2 = _norm_mod(x1, small["ln2_g"], sc2, sh2, name="norm2" + tag)
    w.update(get_weights("ff1", merged))
    f, act = _mm(h2, w["w_ff1"], b_stack=True, name="ff1" + tag, out_dtypes=(BF16, BF16),
                 epilogue=lambda acc: (acc, jnp.square(jnp.maximum(acc, 0.0))))
    w.update(get_weights("ff2", act))
    m2, x2 = _mm(act, w["w_ff2"], name="ff2" + tag, out_dtypes=(BF16, F32),
                 epilogue=lambda acc, xr, g: (acc, xr + g * acc), extras=((x1, "tile"), (g2, "row")))
    saved = dict(x=x, h=h, r1=r1, z=z, p=p, yp=yp, cq=cq, ckv=ckv, rq=rq, rkv=rkv, ya=ya, q=q, k=k, v=v, o=o, lse=lse,
                 yb=yb, merged=merged, mo=mo, x1=x1, h2=h2, r2=r2, f=f, act=act, m2=m2)
    return x2, saved, w


def _merge_grads(dm, ga, gb, ya, yb):
    sa, sb = _sigmoid(ga), _sigmoid(gb)
    return dm * sa, dm * sb, dm * ya * (sa * (1.0 - sa)), dm * yb * (sb * (1.0 - sb))


def _layer_bwd(l, dx2, dm2, dg2, sv, mod, w, small, cos_t, sin_t, send_grads, gate_below):
    sh1, sc1, g1, sh2, sc2, g2 = mod
    tag = f"_l{l}"
    gw = {}
    (df,) = _mm(dm2, w["w_ff2"], tb=True, name="ff2_dx" + tag, out_dtypes=(BF16,),
                epilogue=lambda acc, f: (acc * (2.0 * jnp.maximum(f, 0.0)),), extras=((sv["f"], "tile"),))
    (g_ff2,) = _mm(sv["act"], dm2, ta=True, name="ff2_dw" + tag, out_dtypes=(BF16,))
    (g_ff1,) = _mm(sv["h2"], df, ta=True, out_stack=N_CHIPS, name="ff1_dw" + tag, out_dtypes=(BF16,))
    sc2 = sc2 + send_grads("ff2", dict(w_ff2=g_ff2)) + send_grads("ff1", dict(w_ff1=g_ff1))
    (dh2,) = _mm(df, w["w_ff1"], tb=True, b_stack=True, name="ff1_dx" + tag)
    dx1, dln2, dsc2, dsh2, dmo, dg1 = _norm_mod_bwd(dh2, sv["x1"], sv["r2"], small["ln2_g"], sc2, dx2,
                                                    gate=(sv["mo"], g1), name="norm2_bwd" + tag)
    dya, dyb, dga, dgb = _mm(dmo, w["w_out"], tb=True, name="mix_out_dx" + tag, out_dtypes=(BF16,) * 4, tm=512,
                             epilogue=_merge_grads,
                             extras=((sv["z"], ("tile", ZC_GA // D_MODEL)), (sv["z"], ("tile", ZC_GB // D_MODEL)),
                                     (sv["ya"], "tile"), (sv["yb"], "tile")))
    (gw["w_out"],) = _mm(sv["merged"], dmo, ta=True, name="mix_out_dw" + tag, out_dtypes=(BF16,))
    (gw["p_pool"],) = _mm(sv["yp"], dya, ta=True, name="pool_out_dw" + tag, out_dtypes=(BF16,))
    (dyp,) = _mm(dya, w["p_pool"], tb=True, name="pool_out_dx" + tag)
    du, g_w_pool, g_pool_scale = _pool_bwd(dyp, sv["p"], small["w_pool"], small["pool_scale"], name="pool_bwd" + tag)
    (gw["p_attn"],) = _mm(sv["o"], dyb, ta=True, name="attn_out_dw" + tag, out_dtypes=(BF16,))
    (do,) = _mm(dyb, w["p_attn"], tb=True, name="attn_out_dx" + tag, out_dtypes=(BF16,))
    delta = _attn_delta(do, sv["o"], name="attn_delta" + tag)
    dql, dk, dv = _attn_bwd(sv["q"], sv["k"], sv["v"], do, sv["lse"], delta, cos_t, sin_t, name="attn_bwd" + tag)
    dkb, dkr = _key_bwd(dk, cos_t, sin_t, name="key_bwd" + tag)
    (gw["w_uq"],) = _mm(sv["cq"], dql, ta=True, name="q_proj_dw" + tag, out_dtypes=(BF16,))
    (gw["w_uk"],) = _mm(sv["ckv"], dkb, ta=True, name="k_proj_dw" + tag, out_dtypes=(BF16,))
    (gw["w_uv"],) = _mm(sv["ckv"], dv, ta=True, name="v_proj_dw" + tag, out_dtypes=(BF16,))
    (dcq,) = _mm(dql, w["w_uq"], tb=True, name="q_proj_dx" + tag)
    (dckv,) = _mm(dkb, w["w_uk"], tb=True, second=(dv, w["w_uv"]), name="kv_proj_dx" + tag)
    q_norm_g = small["q_norm_g"] + send_grads("mix", gw)
    dcq_raw, g_qn = _rms_bwd(dcq, sv["z"], ZC_CQ, sv["rq"], q_norm_g, name="q_norm_bwd" + tag)
    dckv_raw, g_kvn = _rms_bwd(dckv, sv["z"], ZC_CKV, sv["rkv"], small["kv_norm_g"], name="kv_norm_bwd" + tag)
    dz = dict(cq=dcq_raw, kr=dkr, u=du, ga=dga, gb=dgb, ckv=dckv_raw)
    g_in = {n: _mm(piece, sv["h"], ta=True, name=f"in_proj_dw_{n}" + tag, out_dtypes=(BF16,))[0]
            for n, piece in dz.items()}
    sc1 = sc1 + send_grads("in", g_in)
    dh = _mm_sum(list(dz.values()), w["w_in"], [Z_OFFSETS[n] for n in dz], name="in_proj_dx" + tag)
    dx, dln1, dsc1, dsh1, *below = _norm_mod_bwd(dh, sv["x"], sv["r1"], small["ln1_g"], sc1, dx1, gate=gate_below,
                                                 name="norm1_bwd" + tag)
    dmod = jnp.concatenate([dsh1, dsc1, dg1, dsh2, dsc2, dg2], axis=0)
    gsmall = dict(ln1_g=dln1, ln2_g=dln2, q_norm_g=g_qn, kv_norm_g=g_kvn, w_pool=g_w_pool, pool_scale=g_pool_scale)
    return dx, dmod, gsmall, below


SMALL_LOSS = 6
SMALL_SINGLES = 16
SMALL_POOL = 24
SMALL_POOL_ROWS = len(POOL_WINDOWS) * POOL_GROUP * POOL_GROUP // D_MODEL
SMALL_ROWS = SMALL_POOL + DEPTH * SMALL_POOL_ROWS


def _pack_small(parts, *, name):
    def body(*refs):
        out_ref = refs[-1]
        out_ref[...] = jnp.zeros_like(out_ref)
        for ref, (_, row) in zip(refs[:-1], parts):
            out_ref[row:row + ref.shape[0], :] = ref[...]

    return pl.pallas_call(body, name=name, out_shape=jax.ShapeDtypeStruct((SMALL_ROWS, D_MODEL), F32),
                          compiler_params=_params())(*[a for a, _ in parts])


def kernel(x, c, positions, ln1_g, ln2_g, w_ada, b_ada, w_in, q_norm_g, w_uq, kv_norm_g, w_uk, w_uv, w_pool, pool_scale, p_pool, p_attn, w_out, w_ff1, w_ff2, final_g, loss_target, m_ln1_g, m_ln2_g, m_w_ada, m_b_ada, m_w_in, m_q_norm_g, m_w_uq, m_kv_norm_g, m_w_uk, m_w_uv, m_w_pool, m_pool_scale, m_p_pool, m_p_attn, m_w_out, m_w_ff1, m_w_ff2, m_final_g, v_ln1_g, v_ln2_g, v_w_ada, v_b_ada, v_w_in, v_q_norm_g, v_w_uq, v_kv_norm_g, v_w_uk, v_w_uv, v_w_pool, v_pool_scale, v_p_pool, v_p_attn, v_w_out, v_w_ff1, v_w_ff2, v_final_g):
    weights = dict(ln1_g=ln1_g, ln2_g=ln2_g, w_ada=w_ada, b_ada=b_ada, w_in=w_in, q_norm_g=q_norm_g, w_uq=w_uq,
                   kv_norm_g=kv_norm_g, w_uk=w_uk, w_uv=w_uv, w_pool=w_pool, pool_scale=pool_scale, p_pool=p_pool,
                   p_attn=p_attn, w_out=w_out, w_ff1=w_ff1, w_ff2=w_ff2, final_g=final_g)
    moms = dict(ln1_g=m_ln1_g, ln2_g=m_ln2_g, w_ada=m_w_ada, b_ada=m_b_ada, w_in=m_w_in, q_norm_g=m_q_norm_g,
                w_uq=m_w_uq, kv_norm_g=m_kv_norm_g, w_uk=m_w_uk, w_uv=m_w_uv, w_pool=m_w_pool,
                pool_scale=m_pool_scale, p_pool=m_p_pool, p_attn=m_p_attn, w_out=m_w_out, w_ff1=m_w_ff1,
                w_ff2=m_w_ff2, final_g=m_final_g)
    vels = dict(ln1_g=v_ln1_g, ln2_g=v_ln2_g, w_ada=v_w_ada, b_ada=v_b_ada, w_in=v_w_in, q_norm_g=v_q_norm_g,
                w_uq=v_w_uq, kv_norm_g=v_kv_norm_g, w_uk=v_w_uk, w_uv=v_w_uv, w_pool=v_w_pool,
                pool_scale=v_pool_scale, p_pool=v_p_pool, p_attn=v_p_attn, w_out=v_w_out, w_ff1=v_w_ff1,
                w_ff2=v_w_ff2, final_g=v_final_g)
    order = list(weights)
    for table in (weights, moms, vels):
        table["w_in"] = jnp.swapaxes(table["w_in"], 1, 2)
    seq = x.shape[1]
    my_chip = 2 * lax.axis_index("x") + lax.axis_index("y")
    my_dev = 2 * my_chip + lax.axis_index("c")
    ada_cols = w_ada.shape[2]

    small = [dict(ln1_g=ln1_g[l:l + 1], ln2_g=ln2_g[l:l + 1], q_norm_g=q_norm_g[l:l + 1], kv_norm_g=kv_norm_g[l:l + 1],
                  w_pool=w_pool[l], pool_scale=pool_scale[l:l + 1]) for l in range(DEPTH)]

    c_all = _all_gather_small(jnp.pad(c, ((0, 7), (0, 0))), name="cond_all_gather")
    c_act = _silu(c_all, name="cond_silu")
    b_mine = lax.dynamic_slice_in_dim(b_ada, my_chip * ada_cols, ada_cols, axis=1).reshape(1, DEPTH * ada_cols)
    (mod_cat,) = _mm(c_act, w_ada, b_stack=True, name="ada_fwd", epilogue=lambda acc, b: (acc + b,),
                     extras=((b_mine, "row"),))
    mod_mine = jnp.concatenate([mod_cat[::8, l * ada_cols:(l + 1) * ada_cols] for l in range(DEPTH)], axis=0)
    mod_all = _all_gather_small(mod_mine, name="mod_all_gather").reshape(N_DEV, DEPTH, N_DEV, ada_cols)

    zero = mod_all[0, 0, 0, 0] * 0.0
    started = {}
    for l in range(DEPTH):
        for group in GROUPS:
            started[l, group] = _exchange_start(_local_shard(weights, l, group, zero), name=f"weights_send_l{l}_{group}",
                                                scatter=False)
    pin = sum(st[4][0:1, 0:1] for st in started.values())

    def gathered_weights(l, group, after):
        mine, land = _exchange_wait(started[l, group], after, name=f"weights_wait_l{l}_{group}", scatter=False)
        land = lax.dynamic_update_slice_in_dim(land, mine[None], my_chip, axis=0)
        return _unpack_weights(land, group)

    mods = []
    for l in range(DEPTH):
        row = jnp.concatenate([lax.dynamic_index_in_dim(mod_all[2 * j, l], my_dev, axis=0, keepdims=True)
                               for j in range(N_CHIPS)], axis=1) + pin
        mods.append([row[:, i * D_MODEL:(i + 1) * D_MODEL] for i in range(N_MOD)])

    inv_freq = ROPE_THETA ** (-jnp.arange(0, QK_ROPE, 2, dtype=F32) / QK_ROPE)
    freq_lanes = jnp.concatenate([jnp.zeros((QK_NOPE,), F32), inv_freq, inv_freq,
                                  jnp.zeros((HEAD_PAD - QK_DIM,), F32)]).reshape(1, LANES)
    cos_t, sin_t = _rope_tables(positions.reshape(seq, 1), freq_lanes, name="rope_tables")

    xs, saved, wl = x.reshape(seq, D_MODEL), [], []
    for l in range(DEPTH):
        xs, sv, w_l = _layer_fwd(l, xs, mods[l], functools.partial(gathered_weights, l), small[l], cos_t, sin_t)
        saved.append(sv)
        wl.append(w_l)
    dx, loss_part, g_final, dm2, dg2 = _final_loss(xs, final_g.reshape(1, D_MODEL), loss_target.reshape(seq, D_MODEL),
                                                   saved[-1]["m2"], mods[-1][5], name="final_loss")

    sent = []

    def send_grads(l, group, g):
        gpack = _pack_grads(g, group)
        started_g = _exchange_start(gpack, name=f"grads_send_l{l}_{group}", scatter=True)
        sent.append((l, group, started_g))
        return started_g[4][0:1, 0:1]

    dmod, gsmall = [None] * DEPTH, [None] * DEPTH
    for l in reversed(range(DEPTH)):
        gate_below = (saved[l - 1]["m2"], mods[l - 1][5]) if l > 0 else None
        dx, dmod[l], gsmall[l], below = _layer_bwd(l, dx, dm2, dg2, saved[l], mods[l], wl[l], small[l], cos_t, sin_t,
                                                   functools.partial(send_grads, l), gate_below)
        dm2, dg2 = below if below else (None, None)
    grads = dict(x=dx.reshape(1, seq, D_MODEL))

    def lanes(a):
        flat = a.reshape(1, -1)
        return jnp.pad(flat, ((0, 0), (0, D_MODEL - flat.shape[1])))

    singles = [gsmall[0]["ln1_g"], gsmall[1]["ln1_g"], gsmall[0]["ln2_g"], gsmall[1]["ln2_g"], g_final,
               lanes(jnp.concatenate([gsmall[l]["pool_scale"] for l in range(DEPTH)], axis=1)),
               lanes(jnp.concatenate([gsmall[l]["q_norm_g"] for l in range(DEPTH)], axis=1)),
               lanes(jnp.concatenate([gsmall[l]["kv_norm_g"] for l in range(DEPTH)], axis=1))]
    parts = [(dmod[0], 0), (lanes(loss_part), SMALL_LOSS), (dmod[1], 8)]
    parts += [(a, SMALL_SINGLES + i) for i, a in enumerate(singles)]
    parts += [(gsmall[l]["w_pool"].reshape(-1, D_MODEL), SMALL_POOL + l * SMALL_POOL_ROWS) for l in range(DEPTH)]
    small_all = _all_gather_small(_pack_small(parts, name="small_grads_pack"), name="small_grads_all_gather")
    small_all = small_all.reshape(N_DEV, SMALL_ROWS, D_MODEL)
    ssum = _sum_slots(small_all, N_DEV, name="small_grads_sum")
    loss = ssum[SMALL_LOSS, 0]
    grads["b_ada"] = jnp.stack([ssum[8 * l:8 * l + N_MOD] for l in range(DEPTH)]).reshape(DEPTH, N_MOD * D_MODEL)
    grads["ln1_g"] = ssum[SMALL_SINGLES:SMALL_SINGLES + 2]
    grads["ln2_g"] = ssum[SMALL_SINGLES + 2:SMALL_SINGLES + 4]
    grads["final_g"] = ssum[SMALL_SINGLES + 4]
    grads["pool_scale"] = ssum[SMALL_SINGLES + 5].reshape(DEPTH, POOL_DIM)
    grads["q_norm_g"] = ssum[SMALL_SINGLES + 6, :DEPTH * Q_LORA].reshape(DEPTH, Q_LORA)
    grads["kv_norm_g"] = ssum[SMALL_SINGLES + 7, :DEPTH * KV_LORA].reshape(DEPTH, KV_LORA)
    grads["w_pool"] = ssum[SMALL_POOL:SMALL_ROWS].reshape(w_pool.shape)

    gsum, after = {}, ssum
    for l, group, started_g in sent:
        tg = f"_l{l}_{group}"
        gpack, land = _exchange_wait(started_g, after, name="grads_wait" + tg, scatter=True)
        own = lax.dynamic_index_in_dim(gpack, my_chip, axis=0, keepdims=True)
        land = lax.dynamic_update_slice_in_dim(land, own, my_chip, axis=0)
        part = _sum_slots(land, N_CHIPS, name="grads_sum_chips" + tg)
        other = _exchange_sibling(part, name="grads_swap_cores" + tg)
        gsum[group] = _add2_stacked(part, other, gsum.get(group), l, name="grads_sum_cores" + tg)
        after = gsum[group]
    grads.update(w_in=gsum["in"], w_ff1=gsum["ff1"], w_ff2=gsum["ff2"])
    off = 0
    for name in MIX_NAMES:
        grads[name] = gsum["mix"][:, off:off + ROWS_OF[name]].reshape(weights[name].shape)
        off += ROWS_OF[name]

    c_act_t = jnp.pad(c_act[::8].T, ((0, 0), (0, LANES - N_DEV)))
    d_mine = []
    for l in range(DEPTH):
        d_all = small_all[:, 8 * l:8 * l + N_MOD].reshape(N_DEV, N_MOD * D_MODEL)
        d_mine.append(lax.dynamic_slice_in_dim(d_all, my_chip * ada_cols, ada_cols, axis=1))
    d_cat = jnp.pad(jnp.concatenate(d_mine, axis=1), ((0, LANES - N_DEV), (0, 0)))
    (grads["w_ada"],) = _mm(c_act_t, d_cat, out_stack=DEPTH, name="ada_dw")

    def view(a):
        return a.reshape(1, -1) if a.ndim == 1 else a if a.ndim == 3 else a.reshape(-1, a.shape[-1])

    delta, new_m, new_v = {}, {}, {}
    for name in order:
        shape = weights[name].shape
        d, nm, nv = _adamw(view(weights[name]), view(grads[name]), view(moms[name]), view(vels[name]),
                           name="adamw_" + name)
        delta[name], new_m[name], new_v[name] = d.reshape(shape), nm.reshape(shape), nv.reshape(shape)
    for table in (grads, delta, new_m, new_v):
        table["w_in"] = jnp.swapaxes(table["w_in"], 1, 2)
    return (loss, grads["x"], *[grads[n] for n in order], *[delta[n] for n in order],
            *[new_m[n] for n in order], *[new_v[n] for n in order])
```

```python
import functools
import math

import jax
import jax.numpy as jnp
from jax import lax
from jax.experimental import pallas as pl
from jax.experimental.pallas import tpu as pltpu

F32 = jnp.float32
BF16 = jnp.bfloat16
MESH = pl.DeviceIdType.MESH

D_MODEL = 1024
DEPTH = 2
POOL_WINDOWS = (2, 4, 8, 16)
POOL_GROUP = 128
POOL_DIM = 512
N_HEADS = 8
QK_NOPE = 64
QK_ROPE = 32
QK_DIM = QK_NOPE + QK_ROPE
V_DIM = 64
HEAD_PAD = 128
Q_LORA = 384
KV_LORA = 256
ROPE_THETA = 10000.0
ATTN_DIM = N_HEADS * V_DIM
D_FF = 4 * D_MODEL
N_MOD = 6
EPS = 1e-6
N_CHIPS = 4
N_DEV = 8

ADAM_LR = 0.001
ADAM_B1 = 0.9
ADAM_B2 = 0.999
ADAM_EPS = 1e-08
ADAM_WD = 0.01
ADAM_STEP = 10

VMEM_LIMIT_BYTES = 56 * 1024 * 1024
LANES = 128
HALO = 16

ZC_CQ = 0
ZC_KR = 384
ZC_U = 512
ZC_GA = 1024
ZC_GB = 2048
ZC_CKV = 3072
Z_DIM = 3328
Z_OFFSETS = dict(cq=ZC_CQ, kr=ZC_KR, u=ZC_U, ga=ZC_GA, gb=ZC_GB, ckv=ZC_CKV)

W_IN_U, W_IN_CQ, W_IN_CKV, W_IN_KR, W_IN_GATES = (0, 512), (512, 896), (896, 1152), (1152, 1184), (1184, 3232)
W_IN_COLS = W_IN_GATES[1]
W_IN_SHARD = W_IN_COLS // N_CHIPS

ROWS_OF = dict(w_uq=72, w_uk=32, w_uv=32, p_pool=128, p_attn=128, w_out=256)


def _params(sem=None, **kw):
    return pltpu.CompilerParams(dimension_semantics=sem, vmem_limit_bytes=VMEM_LIMIT_BYTES, **kw)


def _tile(n, target, unit=LANES):
    best = None
    for t in range(unit, min(n, target) + 1, unit):
        if n % t == 0:
            best = t
    return best if best is not None and 4 * best >= min(n, target) else n


def _near_tile(n, target):
    cands = [t for t in range(LANES, n + 1, LANES) if n % t == 0]
    return min(cands, key=lambda t: abs(math.log(t / target))) if cands else n


def _mm(a, b, *, name, ta=False, tb=False, out_dtypes=(F32,), epilogue=None, extras=(), tm=1024, tn=1024, tk=1024,
        second=None, b_stack=False, out_stack=None):
    (k_dim, m_dim) = a.shape if ta else a.shape[::-1]
    if b_stack:
        g_b, k_b, n_shard = b.shape
        n_dim, k_b = (k_b, g_b * n_shard) if tb else (g_b * n_shard, k_b)
    else:
        (n_dim, k_b) = b.shape if tb else b.shape[::-1]
    assert k_dim == k_b, (a.shape, b.shape)
    n_unit = n_shard if b_stack and not tb else n_dim // out_stack if out_stack else n_dim
    k_unit = n_shard if b_stack and tb else k_dim
    tm, tn, tk = _near_tile(m_dim, tm), _near_tile(n_unit, tn), _near_tile(k_unit, tk)
    nk = k_dim // tk
    n_extra, n_out = len(extras), len(out_dtypes)
    n_lhs = 4 if second else 2
    dims = (((0 if ta else 1,), (1 if tb else 0,)), ((), ()))
    if epilogue is None:
        epilogue = lambda acc: (acc,) * n_out

    def body(*refs):
        operand_refs, rest = refs[:n_lhs], refs[n_lhs:]
        extra_refs, out_refs = rest[:n_extra], rest[n_extra:n_extra + n_out]

        def product():
            total = None
            for a_ref, b_ref in zip(operand_refs[0::2], operand_refs[1::2]):
                part = lax.dot_general(a_ref[...].astype(BF16), b_ref[...].astype(BF16), dims, preferred_element_type=F32)
                total = part if total is None else total + part
            return total

        def finish(acc):
            outs = epilogue(acc, *[r[...] for r in extra_refs])
            for o_ref, o in zip(out_refs, outs):
                o_ref[...] = o.astype(o_ref.dtype)

        if nk == 1:
            finish(product())
            return
        acc_ref = rest[-1]
        k = pl.program_id(2)

        @pl.when(k == 0)
        def _():
            acc_ref[...] = product()

        @pl.when((k > 0) & (k < nk - 1))
        def _():
            acc_ref[...] += product()

        @pl.when(k == nk - 1)
        def _():
            finish(acc_ref[...] + product())

    a_spec = pl.BlockSpec((tk, tm), lambda i, j, k: (k, i)) if ta else pl.BlockSpec((tm, tk), lambda i, j, k: (i, k))
    if b_stack and tb:
        per = n_shard // tk
        b_spec = pl.BlockSpec((None, tn, tk), lambda i, j, k: (k // per, j, k % per))
    elif b_stack:
        per = n_shard // tn
        b_spec = pl.BlockSpec((None, tk, tn), lambda i, j, k: (j // per, k, j % per))
    elif tb:
        b_spec = pl.BlockSpec((tn, tk), lambda i, j, k: (j, k))
    else:
        b_spec = pl.BlockSpec((tk, tn), lambda i, j, k: (k, j))
    if out_stack:
        per_out = (n_dim // out_stack) // tn
        out_spec = pl.BlockSpec((None, tm, tn), lambda i, j, k: (j // per_out, i, j % per_out))
        out_dims = (out_stack, m_dim, n_dim // out_stack)
    else:
        out_spec = pl.BlockSpec((tm, tn), lambda i, j, k: (i, j))
        out_dims = (m_dim, n_dim)
    extra_specs = []
    for arr, kind in extras:
        if kind == "tile":
            extra_specs.append(pl.BlockSpec((tm, tn), lambda i, j, k: (i, j)))
        elif isinstance(kind, tuple):
            extra_specs.append(pl.BlockSpec((tm, tn), functools.partial(lambda i, j, k, c: (i, j + c), c=kind[1])))
        elif kind == "row":
            extra_specs.append(pl.BlockSpec((1, tn), lambda i, j, k: (0, j)))
        elif kind == "col":
            extra_specs.append(pl.BlockSpec((tm, 1), lambda i, j, k: (i, 0)))
        else:
            assert kind == "table", kind
            extra_specs.append(pl.BlockSpec((tm, LANES), lambda i, j, k: (i, 0)))
    return pl.pallas_call(
        body,
        name=name,
        grid=(m_dim // tm, n_dim // tn, nk),
        in_specs=[a_spec, b_spec] * (n_lhs // 2) + extra_specs,
        out_specs=[out_spec for _ in out_dtypes],
        out_shape=[jax.ShapeDtypeStruct(out_dims, dt) for dt in out_dtypes],
        scratch_shapes=[pltpu.VMEM((tm, tn), F32)] if nk > 1 else [],
        compiler_params=_params(("parallel", "parallel", "arbitrary")),
    )(a, b, *(second or ()), *[arr for arr, _ in extras])


def _mm_sum(pieces, b, offsets, *, name, tm=1024, tn=1024):
    m_dim, n_dim = pieces[0].shape[0], b.shape[1]
    tm, tn = _near_tile(m_dim, tm), _near_tile(n_dim, tn)
    n_pieces = len(pieces)

    def body(*refs):
        total = None
        for a_ref, b_ref in zip(refs[:n_pieces], refs[n_pieces:2 * n_pieces]):
            part = jnp.dot(a_ref[...], b_ref[...], preferred_element_type=F32)
            total = part if total is None else total + part
        refs[-1][...] = total

    a_specs = [pl.BlockSpec((tm, p.shape[1]), lambda i, j: (i, 0)) for p in pieces]
    b_specs = [pl.BlockSpec((p.shape[1], tn), functools.partial(lambda i, j, blk: (blk, j), blk=off // p.shape[1]))
               for p, off in zip(pieces, offsets)]
    return pl.pallas_call(
        body, name=name, grid=(m_dim // tm, n_dim // tn),
        in_specs=a_specs + b_specs,
        out_specs=pl.BlockSpec((tm, tn), lambda i, j: (i, j)),
        out_shape=jax.ShapeDtypeStruct((m_dim, n_dim), F32),
        compiler_params=_params(("parallel", "parallel")),
    )(*pieces, *[b] * n_pieces)


def _rows(s):
    return min(512, s)


def _rope_tables(pos_col, inv_freq_lanes, *, name):
    s = pos_col.shape[0]
    tb = _rows(s)

    def body(pos_ref, f_ref, cos_ref, sin_ref):
        ang = pos_ref[...].astype(F32) * f_ref[...]
        lane = lax.broadcasted_iota(jnp.int32, ang.shape, 1)
        on = (lane >= QK_NOPE) & (lane < QK_DIM)
        cos_ref[...] = jnp.where(on, jnp.cos(ang), 0.0)
        sin_ref[...] = jnp.where(on, jnp.sin(ang), 0.0)

    return pl.pallas_call(
        body, name=name, grid=(s // tb,),
        in_specs=[pl.BlockSpec((tb, 1), lambda i: (i, 0)), pl.BlockSpec((1, LANES), lambda i: (0, 0))],
        out_specs=[pl.BlockSpec((tb, LANES), lambda i: (i, 0))] * 2,
        out_shape=[jax.ShapeDtypeStruct((s, LANES), F32)] * 2,
        compiler_params=_params(("parallel",)),
    )(pos_col, inv_freq_lanes)


def _rotate_half(x):
    lane = lax.broadcasted_iota(jnp.int32, x.shape, 1)
    half = QK_ROPE // 2
    first = (lane >= QK_NOPE) & (lane < QK_NOPE + half)
    second = (lane >= QK_NOPE + half) & (lane < QK_DIM)
    return jnp.where(first, -pltpu.roll(x, LANES - half, 1), jnp.where(second, pltpu.roll(x, half, 1), 0.0))


def _norm_mod(x, g, sc, sh, *, name):
    s, d = x.shape
    tb = _rows(s)

    def body(x_ref, g_ref, sc_ref, sh_ref, h_ref, r_ref):
        xv = x_ref[...]
        r = lax.rsqrt(jnp.mean(xv * xv, axis=-1, keepdims=True) + EPS)
        r_ref[...] = r
        h_ref[...] = (((xv * r) * g_ref[...]) * (1.0 + sc_ref[...]) + sh_ref[...]).astype(BF16)

    vec = pl.BlockSpec((1, d), lambda i: (0, 0))
    return pl.pallas_call(
        body, name=name, grid=(s // tb,),
        in_specs=[pl.BlockSpec((tb, d), lambda i: (i, 0)), vec, vec, vec],
        out_specs=[pl.BlockSpec((tb, d), lambda i: (i, 0)), pl.BlockSpec((tb, 1), lambda i: (i, 0))],
        out_shape=[jax.ShapeDtypeStruct((s, d), BF16), jax.ShapeDtypeStruct((s, 1), F32)],
        compiler_params=_params(("parallel",)),
    )(x, g, sc, sh)


def _window_sums(ext, sign):
    n = ext.shape[0]
    sums, cur, k = [], ext, 1
    for _ in POOL_WINDOWS:
        cur = cur + pltpu.roll(cur, k if sign > 0 else n - k, 0)
        sums.append(cur)
        k *= 2
    return sums


def _mixer_pre(z, cos_t, sin_t, w_pool, pool_scale, gq, gkv, *, name):
    s = z.shape[0]
    tb = _rows(s)
    hb = tb // HALO

    def body(zcq_ref, zkr_ref, zu_ref, zuh_ref, zckv_ref, cos_ref, sin_ref, wp_ref, ps_ref, gq_ref, gkv_ref,
             p_ref, yp_ref, cq_ref, ckv_ref, kr_ref, rq_ref, rkv_ref):
        i = pl.program_id(0)
        u = zu_ref[...].astype(F32)
        halo = jnp.where(i > 0, zuh_ref[...].astype(F32), 0.0)
        ext = jnp.concatenate([halo, u], axis=0)
        t = i * tb + lax.broadcasted_iota(jnp.int32, (tb, 1), 0)
        for g, (w, sw) in enumerate(zip(POOL_WINDOWS, _window_sums(ext, +1))):
            cols = slice(g * POOL_GROUP, (g + 1) * POOL_GROUP)
            cnt = jnp.minimum(t + 1, w).astype(F32)
            pg = (sw[HALO:, cols] / cnt - u[:, cols]).astype(BF16)
            p_ref[:, cols] = pg
            yg = jnp.dot(pg, wp_ref[g].astype(BF16), preferred_element_type=F32)
            yp_ref[:, cols] = (yg * ps_ref[:, cols]).astype(BF16)

        def rms(x_ref, g_ref, out_ref, r_ref):
            xv = x_ref[...].astype(F32)
            r = lax.rsqrt(jnp.mean(xv * xv, axis=-1, keepdims=True) + EPS)
            r_ref[...] = r
            out_ref[...] = ((xv * r) * g_ref[...]).astype(BF16)

        rms(zcq_ref, gq_ref, cq_ref, rq_ref)
        rms(zckv_ref, gkv_ref, ckv_ref, rkv_ref)
        kr = zkr_ref[...].astype(F32)
        kr_ref[...] = (kr * cos_ref[...] + _rotate_half(kr) * sin_ref[...]).astype(BF16)

    def zcol(width, off):
        return pl.BlockSpec((tb, width), lambda i: (i, off // width))

    def full(a):
        return pl.BlockSpec(a.shape, lambda i: (0,) * a.ndim)

    def out(width, dt):
        return pl.BlockSpec((tb, width), lambda i: (i, 0)), jax.ShapeDtypeStruct((s, width), dt)

    outs = [out(POOL_DIM, BF16), out(POOL_DIM, BF16), out(Q_LORA, BF16), out(KV_LORA, BF16), out(LANES, BF16),
            out(1, F32), out(1, F32)]
    return pl.pallas_call(
        body, name=name, grid=(s // tb,),
        in_specs=[zcol(Q_LORA, ZC_CQ), zcol(LANES, ZC_KR), zcol(POOL_DIM, ZC_U),
                  pl.BlockSpec((HALO, POOL_DIM), lambda i: (jnp.maximum(i * hb - 1, 0), ZC_U // POOL_DIM)),
                  zcol(KV_LORA, ZC_CKV),
                  pl.BlockSpec((tb, LANES), lambda i: (i, 0)), pl.BlockSpec((tb, LANES), lambda i: (i, 0)),
                  full(w_pool), full(pool_scale), full(gq), full(gkv)],
        out_specs=[o[0] for o in outs], out_shape=[o[1] for o in outs],
        compiler_params=_params(("parallel",)),
    )(z, z, z, z, z, cos_t, sin_t, w_pool, pool_scale, gq, gkv)


def _sigmoid(x):
    return 1.0 / (1.0 + jnp.exp(-x.astype(F32)))


ATTN_SCALE = 1.0 / math.sqrt(QK_DIM)
NEG_BIG = -1e30


LOG2_E = math.log2(math.e)
EXP2_SCALE = ATTN_SCALE * LOG2_E
NT_DIMS = (((1,), (1,)), ((), ()))
TN_DIMS = (((0,), (0,)), ((), ()))


def _on_or_below_diagonal(t):
    return lax.broadcasted_iota(jnp.int32, (t, t), 0) >= lax.broadcasted_iota(jnp.int32, (t, t), 1)


HEADS_PER_STEP = 2
HEAD_COLS = [slice(g * HEAD_PAD, (g + 1) * HEAD_PAD) for g in range(HEADS_PER_STEP)]


def _attn_fwd(q, k, v, *, name):
    s = q.shape[0]
    t = _rows(s)
    wide = HEADS_PER_STEP * HEAD_PAD

    def body(q_ref, k_ref, v_ref, o_ref, lse_ref):
        qi = pl.program_id(1)
        qs = [q_ref[:, cols] for cols in HEAD_COLS]

        def block(j, carry, diagonal):
            rows = pl.ds(pl.multiple_of(j * t, t), t)
            out = []
            for qv, cols, (m, l, acc) in zip(qs, HEAD_COLS, carry):
                sc = lax.dot_general(qv, k_ref[rows, cols], NT_DIMS, preferred_element_type=F32)
                if diagonal:
                    sc = jnp.where(_on_or_below_diagonal(t), sc, NEG_BIG)
                m_new = jnp.maximum(m, jnp.max(sc, axis=-1, keepdims=True))
                p = jnp.exp2((sc - m_new) * EXP2_SCALE)
                alpha = jnp.exp2((m - m_new) * EXP2_SCALE)
                l = alpha * l + jnp.sum(p, axis=-1, keepdims=True)
                acc = alpha * acc + jnp.dot(p.astype(BF16), v_ref[rows, cols], preferred_element_type=F32)
                out.append((m_new, l, acc))
            return tuple(out)

        init = tuple((jnp.full((t, 1), -jnp.inf, F32), jnp.zeros((t, 1), F32), jnp.zeros((t, HEAD_PAD), F32))
                     for _ in HEAD_COLS)
        carry = lax.fori_loop(0, qi, lambda j, c: block(j, c, False), init)
        for g, (cols, (m, l, acc)) in enumerate(zip(HEAD_COLS, block(qi, carry, True))):
            o_ref[:, cols] = (acc / l).astype(BF16)
            lse_ref[g] = m * ATTN_SCALE + jnp.log(l)

    q_spec = pl.BlockSpec((t, wide), lambda h, i: (i, h))
    kv_spec = pl.BlockSpec((s, wide), lambda h, i: (0, h))
    return pl.pallas_call(
        body, name=name, grid=(N_HEADS // HEADS_PER_STEP, s // t),
        in_specs=[q_spec, kv_spec, kv_spec],
        out_specs=[q_spec, pl.BlockSpec((HEADS_PER_STEP, t, 1), lambda h, i: (h, i, 0))],
        out_shape=[jax.ShapeDtypeStruct((s, N_HEADS * HEAD_PAD), BF16), jax.ShapeDtypeStruct((N_HEADS, s, 1), F32)],
        compiler_params=_params(("parallel", "parallel")),
    )(q, k, v)


def _attn_bwd(q, k, v, do, o, lse, cos_t, sin_t, *, name):
    s = q.shape[0]
    t = _rows(s)
    nt = s // t

    def body(q_ref, k_ref, v_ref, do_ref, o_ref, lse_ref, cos_ref, sin_ref, dql_ref, dk_ref, dv_ref, dks_ref,
             dq_ref, dl_ref):
        kj = pl.program_id(1)

        @pl.when(kj == 0)
        def _():
            dq_ref[...] = jnp.zeros_like(dq_ref)

            def delta(i, carry):
                rows = pl.ds(pl.multiple_of(i * t, t), t)
                for g, cols in enumerate(HEAD_COLS):
                    dl_ref[g, rows, :] = jnp.sum(do_ref[rows, cols].astype(F32) * o_ref[rows, cols].astype(F32),
                                                 axis=-1, keepdims=True)
                return carry

            lax.fori_loop(0, nt, delta, 0)

        kvs = [(k_ref[:, cols], v_ref[:, cols]) for cols in HEAD_COLS]

        def block(i, carry, diagonal):
            rows = pl.ds(pl.multiple_of(i * t, t), t)
            out = []
            for g, (cols, (kv, vv), (dk, dv)) in enumerate(zip(HEAD_COLS, kvs, carry)):
                qv, dov = q_ref[rows, cols], do_ref[rows, cols]
                sc = lax.dot_general(qv, kv, NT_DIMS, preferred_element_type=F32)
                p = jnp.exp2(sc * EXP2_SCALE - lse_ref[g, rows, :] * LOG2_E)
                if diagonal:
                    p = jnp.where(_on_or_below_diagonal(t), p, 0.0)
                dp = lax.dot_general(dov, vv, NT_DIMS, preferred_element_type=F32)
                ds = (p * (dp - dl_ref[g, rows, :])).astype(BF16)
                dv = dv + lax.dot_general(p.astype(BF16), dov, TN_DIMS, preferred_element_type=F32)
                dk = dk + lax.dot_general(ds, qv, TN_DIMS, preferred_element_type=F32)
                dq_ref[rows, cols] += jnp.dot(ds, kv, preferred_element_type=F32) * ATTN_SCALE
                out.append((dk, dv))
            return tuple(out)

        zero = jnp.zeros((t, HEAD_PAD), F32)
        carry = block(kj, tuple((zero, zero) for _ in HEAD_COLS), True)
        dk_sum = None
        for cols, (dk, dv) in zip(HEAD_COLS, lax.fori_loop(kj + 1, nt, lambda i, c: block(i, c, False), carry)):
            dk = dk * ATTN_SCALE
            dk_ref[:, cols] = dk.astype(BF16)
            dv_ref[:, cols] = dv.astype(BF16)
            dk_sum = dk if dk_sum is None else dk_sum + dk
        dks_ref[...] = dk_sum

        @pl.when(kj == nt - 1)
        def _():
            def rope_bwd(i, carry):
                rows = pl.ds(pl.multiple_of(i * t, t), t)
                sin = sin_ref[rows, :]
                lane = lax.broadcasted_iota(jnp.int32, sin.shape, 1)
                cos_q = cos_ref[rows, :] + jnp.where(lane < QK_NOPE, 1.0, 0.0)
                for cols in HEAD_COLS:
                    dqv = dq_ref[rows, cols]
                    dql_ref[rows, cols] = (dqv * cos_q - _rotate_half(dqv * sin)).astype(BF16)
                return carry

            lax.fori_loop(0, nt, rope_bwd, 0)

    heads_wide = HEADS_PER_STEP * HEAD_PAD
    full_spec = pl.BlockSpec((s, heads_wide), lambda h, j: (0, h))
    kv_spec = pl.BlockSpec((t, heads_wide), lambda h, j: (j, h))
    vec_spec = pl.BlockSpec((HEADS_PER_STEP, s, 1), lambda h, j: (h, 0, 0))
    table_spec = pl.BlockSpec((s, LANES), lambda h, j: (0, 0))
    wide = jax.ShapeDtypeStruct((s, N_HEADS * HEAD_PAD), BF16)
    n_steps = N_HEADS // HEADS_PER_STEP
    return pl.pallas_call(
        body, name=name, grid=(n_steps, nt),
        in_specs=[full_spec, kv_spec, kv_spec, full_spec, full_spec, vec_spec, table_spec, table_spec],
        out_specs=[full_spec, kv_spec, kv_spec, pl.BlockSpec((None, t, HEAD_PAD), lambda h, j: (h, j, 0))],
        out_shape=[wide, wide, wide, jax.ShapeDtypeStruct((n_steps, s, HEAD_PAD), F32)],
        scratch_shapes=[pltpu.VMEM((s, heads_wide), F32), pltpu.VMEM((HEADS_PER_STEP, s, 1), F32)],
        compiler_params=_params(("parallel", "arbitrary")),
    )(q, k, v, do, o, lse, cos_t, sin_t)


def _acc_specs(widths):
    return ([pl.BlockSpec((1, w), lambda i: (0, 0)) for w in widths],
            [jax.ShapeDtypeStruct((1, w), F32) for w in widths])


def _gate_grads(dxv, m_ref, gate_ref, dm_ref, dgate_ref):
    dm_ref[...] = (dxv * gate_ref[...]).astype(BF16)
    dgate_ref[...] += jnp.sum(dxv * m_ref[...], axis=0, keepdims=True)


def _final_loss(x, g, target, m, gate, *, name):
    s, d = x.shape
    tb = _rows(s)

    def body(x_ref, g_ref, t_ref, m_ref, gate_ref, dx_ref, loss_ref, dg_ref, dm_ref, dgate_ref):
        @pl.when(pl.program_id(0) == 0)
        def _():
            loss_ref[...] = jnp.zeros_like(loss_ref)
            dg_ref[...] = jnp.zeros_like(dg_ref)
            dgate_ref[...] = jnp.zeros_like(dgate_ref)

        xv = x_ref[...]
        r = lax.rsqrt(jnp.mean(xv * xv, axis=-1, keepdims=True) + EPS)
        xn = xv * r
        err = xn * g_ref[...] - t_ref[...]
        loss_ref[...] += 0.5 * jnp.sum(jnp.mean(err * err, axis=-1, keepdims=True), axis=0, keepdims=True)
        dy = err / d
        dg_ref[...] += jnp.sum(dy * xn, axis=0, keepdims=True)
        dxn = dy * g_ref[...]
        dxv = r * (dxn - xn * jnp.mean(dxn * xn, axis=-1, keepdims=True))
        dx_ref[...] = dxv
        _gate_grads(dxv, m_ref, gate_ref, dm_ref, dgate_ref)

    blk = pl.BlockSpec((tb, d), lambda i: (i, 0))
    vec = pl.BlockSpec((1, d), lambda i: (0, 0))
    acc_specs, acc_shapes = _acc_specs((LANES, d))
    return pl.pallas_call(
        body, name=name, grid=(s // tb,),
        in_specs=[blk, vec, blk, blk, vec],
        out_specs=[blk] + acc_specs + [blk, vec],
        out_shape=[jax.ShapeDtypeStruct((s, d), F32)] + acc_shapes + [jax.ShapeDtypeStruct((s, d), BF16),
                                                                     jax.ShapeDtypeStruct((1, d), F32)],
        compiler_params=_params(("arbitrary",)),
    )(x, g, target, m, gate)


def _norm_mod_bwd(dh, x, r, g, sc, dx_skip, *, name, gate=None):
    s, d = x.shape
    tb = _rows(s)
    nb = s // tb
    n_gate = 2 if gate else 0

    def body(dh_ref, x_ref, r_ref, g_ref, sc_ref, skip_ref, *rest):
        gate_refs, (dx_ref, dg_ref, dsc_ref, dsh_ref) = rest[:n_gate], rest[n_gate:n_gate + 4]
        gate_outs, da_sc = rest[n_gate + 4:-1], rest[-1]
        i = pl.program_id(0)

        @pl.when(i == 0)
        def _():
            da_sc[...] = jnp.zeros_like(da_sc)
            dsh_ref[...] = jnp.zeros_like(dsh_ref)
            if gate:
                gate_outs[1][...] = jnp.zeros_like(gate_outs[1])

        dhv, rv = dh_ref[...], r_ref[...]
        xn = x_ref[...] * rv
        dsh_ref[...] += jnp.sum(dhv, axis=0, keepdims=True)
        da_sc[...] += jnp.sum(dhv * xn, axis=0, keepdims=True)
        dxn = dhv * (g_ref[...] * (1.0 + sc_ref[...]))
        dxv = skip_ref[...] + rv * (dxn - xn * jnp.mean(dxn * xn, axis=-1, keepdims=True))
        dx_ref[...] = dxv
        if gate:
            _gate_grads(dxv, *gate_refs, *gate_outs)

        @pl.when(i == nb - 1)
        def _():
            dg_ref[...] = da_sc[...] * (1.0 + sc_ref[...])
            dsc_ref[...] = da_sc[...] * g_ref[...]

    blk = pl.BlockSpec((tb, d), lambda i: (i, 0))
    vec = pl.BlockSpec((1, d), lambda i: (0, 0))
    acc_specs, acc_shapes = _acc_specs((d, d, d))
    gate_specs = [blk, vec] if gate else []
    gate_shapes = [jax.ShapeDtypeStruct((s, d), BF16), jax.ShapeDtypeStruct((1, d), F32)] if gate else []
    return pl.pallas_call(
        body, name=name, grid=(nb,),
        in_specs=[blk, blk, pl.BlockSpec((tb, 1), lambda i: (i, 0)), vec, vec, blk] + gate_specs,
        out_specs=[blk] + acc_specs + gate_specs,
        out_shape=[jax.ShapeDtypeStruct((s, d), F32)] + acc_shapes + gate_shapes,
        scratch_shapes=[pltpu.VMEM((1, d), F32)],
        compiler_params=_params(("arbitrary",)),
    )(dh, x, r, g, sc, dx_skip, *(gate or ()))


def _pool_bwd(dyp, p, w_pool, pool_scale, *, name):
    s = dyp.shape[0]
    tb = _rows(s)
    nb = s // tb
    hb = tb // HALO
    nt_dims = (((1,), (1,)), ((), ()))
    tn_dims = (((0,), (0,)), ((), ()))

    def body(dy_ref, dyn_ref, p_ref, wp_ref, ps_ref, du_ref, gwp_ref, gps_ref):
        i = pl.program_id(0)

        @pl.when(i == 0)
        def _():
            gwp_ref[...] = jnp.zeros_like(gwp_ref)
            gps_ref[...] = jnp.zeros_like(gps_ref)

        cur = dy_ref[...]
        nxt = jnp.where(i < nb - 1, dyn_ref[...], 0.0)
        dpw = (jnp.concatenate([cur, nxt], axis=0) * ps_ref[...]).astype(BF16)
        t = i * tb + lax.broadcasted_iota(jnp.int32, (tb + HALO, 1), 0)
        for g, w in enumerate(POOL_WINDOWS):
            cols = slice(g * POOL_GROUP, (g + 1) * POOL_GROUP)
            wg = wp_ref[g].astype(BF16)
            dp = lax.dot_general(dpw[:, cols], wg, nt_dims, preferred_element_type=F32)
            e = dp / jnp.minimum(t + 1, w).astype(F32)
            lead = _window_sums(e, -1)[g]
            du_ref[:, cols] = (lead[:tb] - dp[:tb]).astype(BF16)
            pg = p_ref[:, cols]
            pw = jnp.dot(pg, wg, preferred_element_type=F32)
            gps_ref[:, cols] += jnp.sum(cur[:, cols] * pw, axis=0, keepdims=True)
            gwp_ref[g] += lax.dot_general(pg, dpw[:tb, cols], tn_dims, preferred_element_type=F32)

    blk = pl.BlockSpec((tb, POOL_DIM), lambda i: (i, 0))
    return pl.pallas_call(
        body, name=name, grid=(nb,),
        in_specs=[blk, pl.BlockSpec((HALO, POOL_DIM), lambda i: (jnp.minimum((i + 1) * hb, s // HALO - 1), 0)), blk,
                  pl.BlockSpec(w_pool.shape, lambda i: (0, 0, 0)), pl.BlockSpec((1, POOL_DIM), lambda i: (0, 0))],
        out_specs=[blk, pl.BlockSpec(w_pool.shape, lambda i: (0, 0, 0)), pl.BlockSpec((1, POOL_DIM), lambda i: (0, 0))],
        out_shape=[jax.ShapeDtypeStruct((s, POOL_DIM), BF16), jax.ShapeDtypeStruct(w_pool.shape, F32),
                   jax.ShapeDtypeStruct((1, POOL_DIM), F32)],
        compiler_params=_params(("arbitrary",)),
    )(dyp, dyp, p, w_pool, pool_scale)


def _key_bwd(dk_sums, cos_t, sin_t, *, name):
    n, s, _ = dk_sums.shape
    tb = _rows(s)

    def body(dk_ref, cos_ref, sin_ref, dkr_ref):
        tot = dk_ref[0]
        for h in range(1, n):
            tot = tot + dk_ref[h]
        dkr_ref[...] = (tot * cos_ref[...] - _rotate_half(tot * sin_ref[...])).astype(BF16)

    tab = pl.BlockSpec((tb, LANES), lambda i: (i, 0))
    return pl.pallas_call(
        body, name=name, grid=(s // tb,),
        in_specs=[pl.BlockSpec((n, tb, LANES), lambda i: (0, i, 0)), tab, tab], out_specs=tab,
        out_shape=jax.ShapeDtypeStruct((s, LANES), BF16),
        compiler_params=_params(("parallel",)),
    )(dk_sums, cos_t, sin_t)


def _rms_bwd(dy, z, z_off, r, g, *, name):
    s, n = dy.shape
    tb = _rows(s)

    def body(dy_ref, x_ref, r_ref, g_ref, dx_ref, dg_ref):
        @pl.when(pl.program_id(0) == 0)
        def _():
            dg_ref[...] = jnp.zeros_like(dg_ref)

        dyv, rv = dy_ref[...], r_ref[...]
        xn = x_ref[...].astype(F32) * rv
        dg_ref[...] += jnp.sum(dyv * xn, axis=0, keepdims=True)
        dxn = dyv * g_ref[...]
        dx_ref[...] = (rv * (dxn - xn * jnp.mean(dxn * xn, axis=-1, keepdims=True))).astype(BF16)

    blk = pl.BlockSpec((tb, n), lambda i: (i, 0))
    acc_specs, acc_shapes = _acc_specs((n,))
    return pl.pallas_call(
        body, name=name, grid=(s // tb,),
        in_specs=[blk, pl.BlockSpec((tb, n), lambda i: (i, z_off // n)), pl.BlockSpec((tb, 1), lambda i: (i, 0)),
                  pl.BlockSpec((1, n), lambda i: (0, 0))],
        out_specs=[blk] + acc_specs, out_shape=[jax.ShapeDtypeStruct((s, n), BF16)] + acc_shapes,
        compiler_params=_params(("arbitrary",)),
    )(dy, z, r, g)


def _silu(c, *, name):
    def body(c_ref, out_ref):
        cv = c_ref[...]
        out_ref[...] = (cv * _sigmoid(cv)).astype(BF16)

    return pl.pallas_call(body, name=name, out_shape=jax.ShapeDtypeStruct(c.shape, BF16),
                          compiler_params=_params())(c)


def _sum_slots(a, n, *, name, out_dtype=F32):
    _, rows, cols = a.shape
    tr = _tile(rows, 256, 8)

    def body(a_ref, out_ref):
        tot = a_ref[0].astype(F32)
        for j in range(1, n):
            tot = tot + a_ref[j].astype(F32)
        out_ref[...] = tot.astype(out_dtype)

    return pl.pallas_call(
        body, name=name, grid=(rows // tr,),
        in_specs=[pl.BlockSpec((n, tr, cols), lambda i: (0, i, 0))],
        out_specs=pl.BlockSpec((tr, cols), lambda i: (i, 0)),
        out_shape=jax.ShapeDtypeStruct((rows, cols), out_dtype),
        compiler_params=_params(("parallel",)),
    )(a)


def _add2_stacked(a, b, stacked, l, *, name):
    rows, cols = a.shape
    tr = _tile(rows, 256, 8)

    def body(a_ref, b_ref, *rest):
        rest[-1][...] = a_ref[...] + b_ref[...]

    blk = pl.BlockSpec((tr, cols), lambda i: (i, 0))
    carried = [] if stacked is None else [stacked]
    return pl.pallas_call(
        body, name=name, grid=(rows // tr,),
        in_specs=[blk, blk] + [pl.BlockSpec(memory_space=pl.ANY) for _ in carried],
        out_specs=pl.BlockSpec((None, tr, cols), lambda i: (l, i, 0)),
        out_shape=jax.ShapeDtypeStruct((DEPTH, rows, cols), F32),
        input_output_aliases={2: 0} if carried else {},
        compiler_params=_params(("parallel",)),
    )(a, b, *carried)


def _adamw(w, g, m, v, *, name):
    shape = w.shape
    if w.ndim == 2:
        w, g, m, v = (a.reshape((1,) + shape) for a in (w, g, m, v))
    layers, rows, cols = w.shape
    tr = _tile(rows, max(8, (1 << 18) // cols), 8)
    c1 = 1.0 - ADAM_B1 ** ADAM_STEP
    c2 = 1.0 - ADAM_B2 ** ADAM_STEP

    def body(w_ref, g_ref, m_ref, v_ref, d_ref, nm_ref, nv_ref):
        gv = g_ref[...]
        nm = ADAM_B1 * m_ref[...] + (1.0 - ADAM_B1) * gv
        nv = ADAM_B2 * v_ref[...] + (1.0 - ADAM_B2) * (gv * gv)
        nm_ref[...] = nm
        nv_ref[...] = nv
        d_ref[...] = -ADAM_LR * ((nm / c1) / (jnp.sqrt(nv / c2) + ADAM_EPS) + ADAM_WD * w_ref[...])

    blk = pl.BlockSpec((None, tr, cols), lambda l, i: (l, i, 0))
    outs = pl.pallas_call(
        body, name=name, grid=(layers, rows // tr), in_specs=[blk] * 4, out_specs=[blk] * 3,
        out_shape=[jax.ShapeDtypeStruct((layers, rows, cols), F32)] * 3,
        compiler_params=_params(("parallel", "parallel")),
    )(w, g, m, v)
    return [o.reshape(shape) for o in outs]


def _coords():
    return lax.axis_index("x"), lax.axis_index("y"), lax.axis_index("c")


def _other_chips(x, y):
    return [(1 - x, y), (x, 1 - y), (1 - x, 1 - y)]


def _all_gather_small(blk, *, name):
    m_per, n = blk.shape

    def body(x_ref, out_ref, send_sems, recv_sems, local_sem):
        x, y, c = _coords()
        me, sibling = (x, y, c), (x, y, 1 - c)
        chips = _other_chips(x, y)

        def rows(px, py, pc):
            return out_ref.at[pl.ds((4 * px + 2 * py + pc) * m_per, m_per), :]

        def copy(k, block, to, src=None):
            return pltpu.make_async_remote_copy(
                src_ref=rows(*block) if src is None else src, dst_ref=rows(*block),
                send_sem=send_sems.at[k], recv_sem=recv_sems.at[k], device_id=to, device_id_type=MESH)

        mine = pltpu.make_async_copy(x_ref, rows(*me), local_sem)
        mine.start()
        first = [copy(0, me, sibling, src=x_ref)]
        first += [copy(1 + j, me, (*chip, c), src=x_ref) for j, chip in enumerate(chips)]
        for cp in first:
            cp.start()
        passed = [copy(4 + j, (*chip, c), sibling) for j, chip in enumerate(chips)]
        for j, chip in enumerate(chips):
            copy(1 + j, (*chip, c), me).wait_recv()
            passed[j].start()
        copy(0, sibling, me).wait_recv()
        for j, chip in enumerate(chips):
            copy(4 + j, (*chip, 1 - c), me).wait_recv()
        for cp in first + passed:
            cp.wait_send()
        mine.wait()

    return pl.pallas_call(
        body, name=name,
        out_shape=jax.ShapeDtypeStruct((N_DEV * m_per, n), blk.dtype),
        in_specs=[pl.BlockSpec(memory_space=pltpu.VMEM)],
        out_specs=pl.BlockSpec(memory_space=pltpu.VMEM),
        scratch_shapes=[pltpu.SemaphoreType.DMA((7,)), pltpu.SemaphoreType.DMA((7,)), pltpu.SemaphoreType.DMA],
        compiler_params=_params(),
    )(blk)


HBM_SPEC = pl.BlockSpec(memory_space=pltpu.HBM)
SEM_SPEC = pl.BlockSpec(memory_space=pltpu.SEMAPHORE)
DATAFLOW = pltpu.SideEffectType.DATAFLOW_SIDE_EFFECTING


def _chip_copies(src_ref, land_ref, send_sems, recv_sems, scatter):
    x, y, c = _coords()
    my = 2 * x + y
    outgoing, incoming = [], []
    for k, (px, py) in enumerate(_other_chips(x, y)):
        peer = 2 * px + py

        def copy(src_slot, dst_slot):
            return pltpu.make_async_remote_copy(
                src_ref=src_ref.at[src_slot] if scatter else src_ref, dst_ref=land_ref.at[dst_slot],
                send_sem=send_sems.at[k], recv_sem=recv_sems.at[k], device_id=(px, py, c), device_id_type=MESH)

        outgoing.append(copy(peer, my))
        incoming.append(copy(my, peer))
    return outgoing, incoming


def _exchange_start(src, *, name, scatter):
    land_shape = src.shape if scatter else (N_CHIPS,) + src.shape

    def body(src_ref, land_ref, send_sems, recv_sems, src_thru, land_thru, token):
        outgoing, _ = _chip_copies(src_ref, land_ref, send_sems, recv_sems, scatter)
        for cp in outgoing:
            cp.start()
        token[...] = jnp.zeros_like(token)

    return pl.pallas_call(
        body, name=name,
        out_shape=(pltpu.SemaphoreType.DMA((N_CHIPS - 1,)), pltpu.SemaphoreType.DMA((N_CHIPS - 1,)),
                   pltpu.HBM(src.shape, src.dtype), pltpu.HBM(land_shape, src.dtype), jax.ShapeDtypeStruct((8, LANES), F32)),
        in_specs=(HBM_SPEC, HBM_SPEC),
        out_specs=(SEM_SPEC, SEM_SPEC, HBM_SPEC, HBM_SPEC, pl.BlockSpec(memory_space=pltpu.VMEM)),
        input_output_aliases={0: 2, 1: 3},
        compiler_params=pltpu.CompilerParams(has_side_effects=DATAFLOW),
    )(pltpu.with_memory_space_constraint(src, pltpu.HBM),
      pltpu.with_memory_space_constraint(lax.empty(land_shape, src.dtype), pltpu.HBM))


def _exchange_wait(started, after, *, name, scatter):
    send_sems, recv_sems, src_thru, land_thru, _ = started

    def body(src_ref, land_ref, send_sems, recv_sems, after_ref, src_dead, got_ref):
        outgoing, incoming = _chip_copies(src_ref, land_ref, send_sems, recv_sems, scatter)
        for cp in outgoing:
            cp.wait_send()
        for cp in incoming:
            cp.wait_recv()

    return pl.pallas_call(
        body, name=name,
        out_shape=(pltpu.HBM(src_thru.shape, src_thru.dtype), pltpu.HBM(land_thru.shape, land_thru.dtype)),
        in_specs=(HBM_SPEC, HBM_SPEC, SEM_SPEC, SEM_SPEC, pl.BlockSpec(memory_space=pl.ANY)),
        out_specs=(HBM_SPEC, HBM_SPEC),
        input_output_aliases={0: 0, 1: 1},
        compiler_params=pltpu.CompilerParams(has_side_effects=DATAFLOW),
    )(src_thru, land_thru, send_sems, recv_sems, after)


def _exchange_sibling(src, *, name):
    def body(src_ref, out_ref, send_sem, recv_sem):
        x, y, c = _coords()
        cp = pltpu.make_async_remote_copy(src_ref=src_ref, dst_ref=out_ref, send_sem=send_sem, recv_sem=recv_sem,
                                          device_id=(x, y, 1 - c), device_id_type=MESH)
        cp.start()
        cp.wait()

    return pl.pallas_call(
        body, name=name,
        out_shape=jax.ShapeDtypeStruct(src.shape, src.dtype),
        in_specs=[pl.BlockSpec(memory_space=pl.ANY)],
        out_specs=pl.BlockSpec(memory_space=pl.ANY),
        scratch_shapes=[pltpu.SemaphoreType.DMA, pltpu.SemaphoreType.DMA],
        compiler_params=_params(),
    )(src)


def _pack_rows(a):
    return a.reshape(-1, D_MODEL)


def _pad_heads(w, width):
    r = w.shape[0]
    return jnp.pad(w, ((0, 0), (0, 0), (0, HEAD_PAD - width))).reshape(r, N_HEADS * HEAD_PAD)


MIX_NAMES = ("w_uq", "w_uk", "w_uv", "p_pool", "p_attn", "w_out")
GROUPS = ("in", "mix", "ff1", "ff2")


def _local_shard(weights, l, group, zero):
    if group == "mix":
        shard = jnp.concatenate([_pack_rows(weights[n][l]) for n in MIX_NAMES], axis=0)
    else:
        shard = weights[{"in": "w_in", "ff1": "w_ff1", "ff2": "w_ff2"}[group]][l]
    return (shard + zero).astype(BF16)


def _unpack_weights(gathered, group):
    def cols(a, k):
        return a.reshape(N_CHIPS, k, -1).transpose(1, 0, 2).reshape(k, -1)

    if group == "in":
        full = gathered.reshape(W_IN_COLS, D_MODEL)
        u, cq, ckv, kr, gates = (full[a:b] for a, b in (W_IN_U, W_IN_CQ, W_IN_CKV, W_IN_KR, W_IN_GATES))
        kr = jnp.pad(kr, ((QK_NOPE, HEAD_PAD - QK_DIM), (0, 0)))
        return dict(w_in=jnp.concatenate([cq, kr, u, gates, ckv], axis=0))
    if group == "ff1":
        return dict(w_ff1=gathered)
    if group == "ff2":
        return dict(w_ff2=gathered.reshape(D_FF, D_MODEL))

    def p_attn(a):
        full = cols(a, ATTN_DIM).reshape(N_HEADS, V_DIM, D_MODEL)
        return jnp.pad(full, ((0, 0), (0, HEAD_PAD - V_DIM), (0, 0))).reshape(N_HEADS * HEAD_PAD, D_MODEL)

    build = dict(
        w_uq=lambda a: _pad_heads(a.reshape(Q_LORA, N_HEADS, QK_DIM), QK_DIM),
        w_uk=lambda a: _pad_heads(a.reshape(KV_LORA, N_HEADS, QK_NOPE), QK_NOPE),
        w_uv=lambda a: _pad_heads(a.reshape(KV_LORA, N_HEADS, V_DIM), V_DIM),
        p_pool=lambda a: cols(a, POOL_DIM),
        p_attn=p_attn,
        w_out=lambda a: a.reshape(D_MODEL, D_MODEL),
    )
    w, off = {}, 0
    for name in MIX_NAMES:
        w[name] = build[name](gathered[:, off:off + ROWS_OF[name]])
        off += ROWS_OF[name]
    return w


def _pack_grads(g, group):
    def cols(a):
        k = a.shape[0]
        return a.reshape(k, N_CHIPS, -1).transpose(1, 0, 2).reshape(N_CHIPS, -1, D_MODEL)

    def rows(a):
        return a.reshape(N_CHIPS, -1, D_MODEL)

    def heads(width):
        return lambda a: rows(a.reshape(a.shape[0], N_HEADS, HEAD_PAD)[:, :, :width])

    if group == "in":
        full = jnp.concatenate([g["u"], g["cq"], g["ckv"], g["kr"][QK_NOPE:QK_DIM], g["ga"], g["gb"]], axis=0)
        return full.reshape(N_CHIPS, W_IN_SHARD, D_MODEL)
    if group == "ff1":
        return g["w_ff1"]
    if group == "ff2":
        return g["w_ff2"].reshape(N_CHIPS, D_FF // N_CHIPS, D_MODEL)

    def p_attn(a):
        return cols(a.reshape(N_HEADS, HEAD_PAD, D_MODEL)[:, :V_DIM].reshape(ATTN_DIM, D_MODEL))

    build = dict(w_uq=heads(QK_DIM), w_uk=heads(QK_NOPE), w_uv=heads(V_DIM), p_pool=cols, p_attn=p_attn, w_out=rows)
    return jnp.concatenate([build[name](g[name]) for name in MIX_NAMES], axis=1)


def _per_head(fn, acc, *tables):
    return jnp.concatenate([fn(acc[:, h * HEAD_PAD:(h + 1) * HEAD_PAD], *tables) for h in range(N_HEADS)], axis=1)


def _rope_head(a, cos, sin):
    lane = lax.broadcasted_iota(jnp.int32, a.shape, 1)
    return a * (cos + jnp.where(lane < QK_NOPE, 1.0, 0.0)) + _rotate_half(a) * sin


def _layer_fwd(l, x, mod, get_weights, small, cos_t, sin_t):
    sh1, sc1, g1, sh2, sc2, g2 = mod
    tag = f"_l{l}"
    h, r1 = _norm_mod(x, small["ln1_g"], sc1, sh1, name="norm1" + tag)
    w = dict(get_weights("in", h))
    (z,) = _mm(h, w["w_in"], tb=True, name="in_proj" + tag, out_dtypes=(BF16,))
    p, yp, cq, ckv, kr, rq, rkv = _mixer_pre(z, cos_t, sin_t, small["w_pool"], small["pool_scale"],
                                              small["q_norm_g"], small["kv_norm_g"], name="mixer_pre" + tag)
    w.update(get_weights("mix", yp))
    (ya,) = _mm(yp, w["p_pool"], name="pool_out" + tag, out_dtypes=(BF16,))
    (q,) = _mm(cq, w["w_uq"], name="q_proj" + tag, out_dtypes=(BF16,),
               epilogue=lambda acc, cos, sin: (_per_head(_rope_head, acc, cos, sin),),
               extras=((cos_t, "table"), (sin_t, "table")))
    (k,) = _mm(ckv, w["w_uk"], name="k_proj" + tag, out_dtypes=(BF16,),
               epilogue=lambda acc, krv: (_per_head(lambda a, b: a + b, acc, krv),), extras=((kr, "table"),))
    (v,) = _mm(ckv, w["w_uv"], name="v_proj" + tag, out_dtypes=(BF16,))
    o, lse = _attn_fwd(q, k, v, name="attn_fwd" + tag)
    yb, merged = _mm(o, w["p_attn"], name="attn_out" + tag, out_dtypes=(BF16, BF16), tm=512,
                     epilogue=lambda acc, ga, gb, yav: (acc, _sigmoid(ga) * yav + _sigmoid(gb) * acc),
                     extras=((z, ("tile", ZC_GA // D_MODEL)), (z, ("tile", ZC_GB // D_MODEL)), (ya, "tile")))
    mo, x1 = _mm(merged, w["w_out"], name="mix_out" + tag, out_dtypes=(BF16, F32),
                 epilogue=lambda acc, xr, g: (acc, xr + g * acc), extras=((x, "tile"), (g1, "row")))
    h2, r2 = _norm_mod(x1, small["ln2_g"], sc2, sh2, name="norm2" + tag)
    w.update(get_weights("ff1", merged))
    f, act = _mm(h2, w["w_ff1"], b_stack=True, name="ff1" + tag, out_dtypes=(BF16, BF16),
                 epilogue=lambda acc: (acc, jnp.square(jnp.maximum(acc, 0.0))))
    w.update(get_weights("ff2", act))
    m2, x2 = _mm(act, w["w_ff2"], name="ff2" + tag, out_dtypes=(BF16, F32),
                 epilogue=lambda acc, xr, g: (acc, xr + g * acc), extras=((x1, "tile"), (g2, "row")))
    saved = dict(x=x, h=h, r1=r1, z=z, p=p, yp=yp, cq=cq, ckv=ckv, rq=rq, rkv=rkv, ya=ya, q=q, k=k, v=v, o=o, lse=lse,
                 yb=yb, merged=merged, mo=mo, x1=x1, h2=h2, r2=r2, f=f, act=act, m2=m2)
    return x2, saved, w


def _merge_grads(dm, ga, gb, ya, yb):
    sa, sb = _sigmoid(ga), _sigmoid(gb)
    return dm * sa, dm * sb, dm * ya * (sa * (1.0 - sa)), dm * yb * (sb * (1.0 - sb))


def _layer_bwd(l, dx2, dm2, dg2, sv, mod, w, small, cos_t, sin_t, send_grads, gate_below):
    sh1, sc1, g1, sh2, sc2, g2 = mod
    tag = f"_l{l}"
    gw = {}
    (df,) = _mm(dm2, w["w_ff2"], tb=True, name="ff2_dx" + tag, out_dtypes=(BF16,),
                epilogue=lambda acc, f: (acc * (2.0 * jnp.maximum(f, 0.0)),), extras=((sv["f"], "tile"),))
    (g_ff2,) = _mm(sv["act"], dm2, ta=True, name="ff2_dw" + tag, out_dtypes=(BF16,))
    (g_ff1,) = _mm(sv["h2"], df, ta=True, out_stack=N_CHIPS, name="ff1_dw" + tag, out_dtypes=(BF16,))
    sc2 = sc2 + send_grads("ff2", dict(w_ff2=g_ff2)) + send_grads("ff1", dict(w_ff1=g_ff1))
    (dh2,) = _mm(df, w["w_ff1"], tb=True, b_stack=True, name="ff1_dx" + tag)
    dx1, dln2, dsc2, dsh2, dmo, dg1 = _norm_mod_bwd(dh2, sv["x1"], sv["r2"], small["ln2_g"], sc2, dx2,
                                                    gate=(sv["mo"], g1), name="norm2_bwd" + tag)
    dya, dyb, dga, dgb = _mm(dmo, w["w_out"], tb=True, name="mix_out_dx" + tag, out_dtypes=(BF16,) * 4, tm=512,
                             epilogue=_merge_grads,
                             extras=((sv["z"], ("tile", ZC_GA // D_MODEL)), (sv["z"], ("tile", ZC_GB // D_MODEL)),
                                     (sv["ya"], "tile"), (sv["yb"], "tile")))
    (gw["w_out"],) = _mm(sv["merged"], dmo, ta=True, name="mix_out_dw" + tag, out_dtypes=(BF16,))
    (gw["p_pool"],) = _mm(sv["yp"], dya, ta=True, name="pool_out_dw" + tag, out_dtypes=(BF16,))
    (dyp,) = _mm(dya, w["p_pool"], tb=True, name="pool_out_dx" + tag)
    du, g_w_pool, g_pool_scale = _pool_bwd(dyp, sv["p"], small["w_pool"], small["pool_scale"], name="pool_bwd" + tag)
    (gw["p_attn"],) = _mm(sv["o"], dyb, ta=True, name="attn_out_dw" + tag, out_dtypes=(BF16,))
    (do,) = _mm(dyb, w["p_attn"], tb=True, name="attn_out_dx" + tag, out_dtypes=(BF16,))
    dql, dkb, dv, dk_sums = _attn_bwd(sv["q"], sv["k"], sv["v"], do, sv["o"], sv["lse"], cos_t, sin_t,
                                      name="attn_bwd" + tag)
    dkr = _key_bwd(dk_sums, cos_t, sin_t, name="key_bwd" + tag)
    (gw["w_uq"],) = _mm(sv["cq"], dql, ta=True, name="q_proj_dw" + tag, out_dtypes=(BF16,))
    (gw["w_uk"],) = _mm(sv["ckv"], dkb, ta=True, name="k_proj_dw" + tag, out_dtypes=(BF16,))
    (gw["w_uv"],) = _mm(sv["ckv"], dv, ta=True, name="v_proj_dw" + tag, out_dtypes=(BF16,))
    (dcq,) = _mm(dql, w["w_uq"], tb=True, name="q_proj_dx" + tag)
    (dckv,) = _mm(dkb, w["w_uk"], tb=True, second=(dv, w["w_uv"]), name="kv_proj_dx" + tag)
    q_norm_g = small["q_norm_g"] + send_grads("mix", gw)
    dcq_raw, g_qn = _rms_bwd(dcq, sv["z"], ZC_CQ, sv["rq"], q_norm_g, name="q_norm_bwd" + tag)
    dckv_raw, g_kvn = _rms_bwd(dckv, sv["z"], ZC_CKV, sv["rkv"], small["kv_norm_g"], name="kv_norm_bwd" + tag)
    dz = dict(cq=dcq_raw, kr=dkr, u=du, ga=dga, gb=dgb, ckv=dckv_raw)
    g_in = {n: _mm(piece, sv["h"], ta=True, name=f"in_proj_dw_{n}" + tag, out_dtypes=(BF16,))[0]
            for n, piece in dz.items()}
    sc1 = sc1 + send_grads("in", g_in)
    dh = _mm_sum(list(dz.values()), w["w_in"], [Z_OFFSETS[n] for n in dz], name="in_proj_dx" + tag)
    dx, dln1, dsc1, dsh1, *below = _norm_mod_bwd(dh, sv["x"], sv["r1"], small["ln1_g"], sc1, dx1, gate=gate_below,
                                                 name="norm1_bwd" + tag)
    dmod = jnp.concatenate([dsh1, dsc1, dg1, dsh2, dsc2, dg2], axis=0)
    gsmall = dict(ln1_g=dln1, ln2_g=dln2, q_norm_g=g_qn, kv_norm_g=g_kvn, w_pool=g_w_pool, pool_scale=g_pool_scale)
    return dx, dmod, gsmall, below


SMALL_LOSS = 6
SMALL_SINGLES = 16
SMALL_POOL = 24
SMALL_POOL_ROWS = len(POOL_WINDOWS) * POOL_GROUP * POOL_GROUP // D_MODEL
SMALL_ROWS = SMALL_POOL + DEPTH * SMALL_POOL_ROWS


def _pack_small(parts, *, name):
    def body(*refs):
        out_ref = refs[-1]
        out_ref[...] = jnp.zeros_like(out_ref)
        for ref, (_, row) in zip(refs[:-1], parts):
            out_ref[row:row + ref.shape[0], :] = ref[...]

    return pl.pallas_call(body, name=name, out_shape=jax.ShapeDtypeStruct((SMALL_ROWS, D_MODEL), F32),
                          compiler_params=_params())(*[a for a, _ in parts])


def kernel(x, c, positions, ln1_g, ln2_g, w_ada, b_ada, w_in, q_norm_g, w_uq, kv_norm_g, w_uk, w_uv, w_pool, pool_scale, p_pool, p_attn, w_out, w_ff1, w_ff2, final_g, loss_target, m_ln1_g, m_ln2_g, m_w_ada, m_b_ada, m_w_in, m_q_norm_g, m_w_uq, m_kv_norm_g, m_w_uk, m_w_uv, m_w_pool, m_pool_scale, m_p_pool, m_p_attn, m_w_out, m_w_ff1, m_w_ff2, m_final_g, v_ln1_g, v_ln2_g, v_w_ada, v_b_ada, v_w_in, v_q_norm_g, v_w_uq, v_kv_norm_g, v_w_uk, v_w_uv, v_w_pool, v_pool_scale, v_p_pool, v_p_attn, v_w_out, v_w_ff1, v_w_ff2, v_final_g):
    weights = dict(ln1_g=ln1_g, ln2_g=ln2_g, w_ada=w_ada, b_ada=b_ada, w_in=w_in, q_norm_g=q_norm_g, w_uq=w_uq,
                   kv_norm_g=kv_norm_g, w_uk=w_uk, w_uv=w_uv, w_pool=w_pool, pool_scale=pool_scale, p_pool=p_pool,
                   p_attn=p_attn, w_out=w_out, w_ff1=w_ff1, w_ff2=w_ff2, final_g=final_g)
    moms = dict(ln1_g=m_ln1_g, ln2_g=m_ln2_g, w_ada=m_w_ada, b_ada=m_b_ada, w_in=m_w_in, q_norm_g=m_q_norm_g,
                w_uq=m_w_uq, kv_norm_g=m_kv_norm_g, w_uk=m_w_uk, w_uv=m_w_uv, w_pool=m_w_pool,
                pool_scale=m_pool_scale, p_pool=m_p_pool, p_attn=m_p_attn, w_out=m_w_out, w_ff1=m_w_ff1,
                w_ff2=m_w_ff2, final_g=m_final_g)
    vels = dict(ln1_g=v_ln1_g, ln2_g=v_ln2_g, w_ada=v_w_ada, b_ada=v_b_ada, w_in=v_w_in, q_norm_g=v_q_norm_g,
                w_uq=v_w_uq, kv_norm_g=v_kv_norm_g, w_uk=v_w_uk, w_uv=v_w_uv, w_pool=v_w_pool,
                pool_scale=v_pool_scale, p_pool=v_p_pool, p_attn=v_p_attn, w_out=v_w_out, w_ff1=v_w_ff1,
                w_ff2=v_w_ff2, final_g=v_final_g)
    order = list(weights)
    for table in (weights, moms, vels):
        table["w_in"] = jnp.swapaxes(table["w_in"], 1, 2)
    seq = x.shape[1]
    my_chip = 2 * lax.axis_index("x") + lax.axis_index("y")
    my_dev = 2 * my_chip + lax.axis_index("c")
    ada_cols = w_ada.shape[2]

    small = [dict(ln1_g=ln1_g[l:l + 1], ln2_g=ln2_g[l:l + 1], q_norm_g=q_norm_g[l:l + 1], kv_norm_g=kv_norm_g[l:l + 1],
                  w_pool=w_pool[l], pool_scale=pool_scale[l:l + 1]) for l in range(DEPTH)]

    c_all = _all_gather_small(jnp.pad(c, ((0, 7), (0, 0))), name="cond_all_gather")
    c_act = _silu(c_all, name="cond_silu")
    b_mine = lax.dynamic_slice_in_dim(b_ada, my_chip * ada_cols, ada_cols, axis=1).reshape(1, DEPTH * ada_cols)
    (mod_cat,) = _mm(c_act, w_ada, b_stack=True, name="ada_fwd", epilogue=lambda acc, b: (acc + b,),
                     extras=((b_mine, "row"),))
    mod_mine = jnp.concatenate([mod_cat[::8, l * ada_cols:(l + 1) * ada_cols] for l in range(DEPTH)], axis=0)
    mod_all = _all_gather_small(mod_mine, name="mod_all_gather").reshape(N_DEV, DEPTH, N_DEV, ada_cols)

    zero = mod_all[0, 0, 0, 0] * 0.0
    started = {}
    for l in range(DEPTH):
        for group in GROUPS:
            started[l, group] = _exchange_start(_local_shard(weights, l, group, zero), name=f"weights_send_l{l}_{group}",
                                                scatter=False)
    pin = sum(st[4][0:1, 0:1] for st in started.values())

    def gathered_weights(l, group, after):
        mine, land = _exchange_wait(started[l, group], after, name=f"weights_wait_l{l}_{group}", scatter=False)
        land = lax.dynamic_update_slice_in_dim(land, mine[None], my_chip, axis=0)
        return _unpack_weights(land, group)

    mods = []
    for l in range(DEPTH):
        row = jnp.concatenate([lax.dynamic_index_in_dim(mod_all[2 * j, l], my_dev, axis=0, keepdims=True)
                               for j in range(N_CHIPS)], axis=1) + pin
        mods.append([row[:, i * D_MODEL:(i + 1) * D_MODEL] for i in range(N_MOD)])

    inv_freq = ROPE_THETA ** (-jnp.arange(0, QK_ROPE, 2, dtype=F32) / QK_ROPE)
    freq_lanes = jnp.concatenate([jnp.zeros((QK_NOPE,), F32), inv_freq, inv_freq,
                                  jnp.zeros((HEAD_PAD - QK_DIM,), F32)]).reshape(1, LANES)
    cos_t, sin_t = _rope_tables(positions.reshape(seq, 1), freq_lanes, name="rope_tables")

    xs, saved, wl = x.reshape(seq, D_MODEL), [], []
    for l in range(DEPTH):
        xs, sv, w_l = _layer_fwd(l, xs, mods[l], functools.partial(gathered_weights, l), small[l], cos_t, sin_t)
        saved.append(sv)
        wl.append(w_l)
    dx, loss_part, g_final, dm2, dg2 = _final_loss(xs, final_g.reshape(1, D_MODEL), loss_target.reshape(seq, D_MODEL),
                                                   saved[-1]["m2"], mods[-1][5], name="final_loss")

    sent = []

    def send_grads(l, group, g):
        gpack = _pack_grads(g, group)
        started_g = _exchange_start(gpack, name=f"grads_send_l{l}_{group}", scatter=True)
        sent.append((l, group, started_g))
        return started_g[4][0:1, 0:1]

    dmod, gsmall = [None] * DEPTH, [None] * DEPTH
    for l in reversed(range(DEPTH)):
        gate_below = (saved[l - 1]["m2"], mods[l - 1][5]) if l > 0 else None
        dx, dmod[l], gsmall[l], below = _layer_bwd(l, dx, dm2, dg2, saved[l], mods[l], wl[l], small[l], cos_t, sin_t,
                                                   functools.partial(send_grads, l), gate_below)
        dm2, dg2 = below if below else (None, None)
    grads = dict(x=dx.reshape(1, seq, D_MODEL))

    def lanes(a):
        flat = a.reshape(1, -1)
        return jnp.pad(flat, ((0, 0), (0, D_MODEL - flat.shape[1])))

    singles = [gsmall[0]["ln1_g"], gsmall[1]["ln1_g"], gsmall[0]["ln2_g"], gsmall[1]["ln2_g"], g_final,
               lanes(jnp.concatenate([gsmall[l]["pool_scale"] for l in range(DEPTH)], axis=1)),
               lanes(jnp.concatenate([gsmall[l]["q_norm_g"] for l in range(DEPTH)], axis=1)),
               lanes(jnp.concatenate([gsmall[l]["kv_norm_g"] for l in range(DEPTH)], axis=1))]
    parts = [(dmod[0], 0), (lanes(loss_part), SMALL_LOSS), (dmod[1], 8)]
    parts += [(a, SMALL_SINGLES + i) for i, a in enumerate(singles)]
    parts += [(gsmall[l]["w_pool"].reshape(-1, D_MODEL), SMALL_POOL + l * SMALL_POOL_ROWS) for l in range(DEPTH)]
    small_all = _all_gather_small(_pack_small(parts, name="small_grads_pack"), name="small_grads_all_gather")
    small_all = small_all.reshape(N_DEV, SMALL_ROWS, D_MODEL)
    ssum = _sum_slots(small_all, N_DEV, name="small_grads_sum")
    loss = ssum[SMALL_LOSS, 0]
    grads["b_ada"] = jnp.stack([ssum[8 * l:8 * l + N_MOD] for l in range(DEPTH)]).reshape(DEPTH, N_MOD * D_MODEL)
    grads["ln1_g"] = ssum[SMALL_SINGLES:SMALL_SINGLES + 2]
    grads["ln2_g"] = ssum[SMALL_SINGLES + 2:SMALL_SINGLES + 4]
    grads["final_g"] = ssum[SMALL_SINGLES + 4]
    grads["pool_scale"] = ssum[SMALL_SINGLES + 5].reshape(DEPTH, POOL_DIM)
    grads["q_norm_g"] = ssum[SMALL_SINGLES + 6, :DEPTH * Q_LORA].reshape(DEPTH, Q_LORA)
    grads["kv_norm_g"] = ssum[SMALL_SINGLES + 7, :DEPTH * KV_LORA].reshape(DEPTH, KV_LORA)
    grads["w_pool"] = ssum[SMALL_POOL:SMALL_ROWS].reshape(w_pool.shape)

    gsum, after = {}, ssum
    for l, group, started_g in sent:
        tg = f"_l{l}_{group}"
        gpack, land = _exchange_wait(started_g, after, name="grads_wait" + tg, scatter=True)
        own = lax.dynamic_index_in_dim(gpack, my_chip, axis=0, keepdims=True)
        land = lax.dynamic_update_slice_in_dim(land, own, my_chip, axis=0)
        part = _sum_slots(land, N_CHIPS, name="grads_sum_chips" + tg)
        other = _exchange_sibling(part, name="grads_swap_cores" + tg)
        gsum[group] = _add2_stacked(part, other, gsum.get(group), l, name="grads_sum_cores" + tg)
        after = gsum[group]
    grads.update(w_in=gsum["in"], w_ff1=gsum["ff1"], w_ff2=gsum["ff2"])
    off = 0
    for name in MIX_NAMES:
        grads[name] = gsum["mix"][:, off:off + ROWS_OF[name]].reshape(weights[name].shape)
        off += ROWS_OF[name]

    c_act_t = jnp.pad(c_act[::8].T, ((0, 0), (0, LANES - N_DEV)))
    d_mine = []
    for l in range(DEPTH):
        d_all = small_all[:, 8 * l:8 * l + N_MOD].reshape(N_DEV, N_MOD * D_MODEL)
        d_mine.append(lax.dynamic_slice_in_dim(d_all, my_chip * ada_cols, ada_cols, axis=1))
    d_cat = jnp.pad(jnp.concatenate(d_mine, axis=1), ((0, LANES - N_DEV), (0, 0)))
    (grads["w_ada"],) = _mm(c_act_t, d_cat, out_stack=DEPTH, name="ada_dw")

    def view(a):
        return a.reshape(1, -1) if a.ndim == 1 else a if a.ndim == 3 else a.reshape(-1, a.shape[-1])

    delta, new_m, new_v = {}, {}, {}
    for name in order:
        shape = weights[name].shape
        d, nm, nv = _adamw(view(weights[name]), view(grads[name]), view(moms[name]), view(vels[name]),
                           name="adamw_" + name)
        delta[name], new_m[name], new_v[name] = d.reshape(shape), nm.reshape(shape), nv.reshape(shape)
    for table in (grads, delta, new_m, new_v):
        table["w_in"] = jnp.swapaxes(table["w_in"], 1, 2)
    return (loss, grads["x"], *[grads[n] for n in order], *[delta[n] for n in order],
            *[new_m[n] for n in order], *[new_v[n] for n in order])
```

```python
import functools
import math

import jax
import jax.numpy as jnp
from jax import lax
from jax.experimental import pallas as pl
from jax.experimental.pallas import tpu as pltpu

F32 = jnp.float32
BF16 = jnp.bfloat16
MESH = pl.DeviceIdType.MESH

D_MODEL = 1024
DEPTH = 2
POOL_WINDOWS = (2, 4, 8, 16)
POOL_GROUP = 128
POOL_DIM = 512
N_HEADS = 8
QK_NOPE = 64
QK_ROPE = 32
QK_DIM = QK_NOPE + QK_ROPE
V_DIM = 64
HEAD_PAD = 128
Q_LORA = 384
KV_LORA = 256
ROPE_THETA = 10000.0
ATTN_DIM = N_HEADS * V_DIM
D_FF = 4 * D_MODEL
N_MOD = 6
EPS = 1e-6
N_CHIPS = 4
N_DEV = 8

ADAM_LR = 0.001
ADAM_B1 = 0.9
ADAM_B2 = 0.999
ADAM_EPS = 1e-08
ADAM_WD = 0.01
ADAM_STEP = 10

VMEM_LIMIT_BYTES = 56 * 1024 * 1024
LANES = 128
HALO = 16

ZC_CQ = 0
ZC_KR = 384
ZC_U = 512
ZC_GA = 1024
ZC_GB = 2048
ZC_CKV = 3072
Z_DIM = 3328
Z_OFFSETS = dict(cq=ZC_CQ, kr=ZC_KR, u=ZC_U, ga=ZC_GA, gb=ZC_GB, ckv=ZC_CKV)

W_IN_U, W_IN_CQ, W_IN_CKV, W_IN_KR, W_IN_GATES = (0, 512), (512, 896), (896, 1152), (1152, 1184), (1184, 3232)
W_IN_COLS = W_IN_GATES[1]
W_IN_SHARD = W_IN_COLS // N_CHIPS

ROWS_OF = dict(w_uq=72, w_uk=32, w_uv=32, p_pool=128, p_attn=128, w_out=256)


def _params(sem=None, **kw):
    return pltpu.CompilerParams(dimension_semantics=sem, vmem_limit_bytes=VMEM_LIMIT_BYTES, **kw)


def _tile(n, target, unit=LANES):
    best = None
    for t in range(unit, min(n, target) + 1, unit):
        if n % t == 0:
            best = t
    return best if best is not None and 4 * best >= min(n, target) else n


def _near_tile(n, target):
    cands = [t for t in range(LANES, n + 1, LANES) if n % t == 0]
    return min(cands, key=lambda t: abs(math.log(t / target))) if cands else n


def _mm(a, b, *, name, ta=False, tb=False, out_dtypes=(F32,), epilogue=None, extras=(), tm=1024, tn=1024, tk=1024,
        second=None, b_stack=False, out_stack=None):
    (k_dim, m_dim) = a.shape if ta else a.shape[::-1]
    if b_stack:
        g_b, k_b, n_shard = b.shape
        n_dim, k_b = (k_b, g_b * n_shard) if tb else (g_b * n_shard, k_b)
    else:
        (n_dim, k_b) = b.shape if tb else b.shape[::-1]
    assert k_dim == k_b, (a.shape, b.shape)
    n_unit = n_shard if b_stack and not tb else n_dim // out_stack if out_stack else n_dim
    k_unit = n_shard if b_stack and tb else k_dim
    tm, tn, tk = _near_tile(m_dim, tm), _near_tile(n_unit, tn), _near_tile(k_unit, tk)
    nk = k_dim // tk
    n_extra, n_out = len(extras), len(out_dtypes)
    n_lhs = 4 if second else 2
    dims = (((0 if ta else 1,), (1 if tb else 0,)), ((), ()))
    if epilogue is None:
        epilogue = lambda acc: (acc,) * n_out

    def body(*refs):
        operand_refs, rest = refs[:n_lhs], refs[n_lhs:]
        extra_refs, out_refs = rest[:n_extra], rest[n_extra:n_extra + n_out]

        def product():
            total = None
            for a_ref, b_ref in zip(operand_refs[0::2], operand_refs[1::2]):
                part = lax.dot_general(a_ref[...].astype(BF16), b_ref[...].astype(BF16), dims, preferred_element_type=F32)
                total = part if total is None else total + part
            return total

        def finish(acc):
            outs = epilogue(acc, *[r[...] for r in extra_refs])
            for o_ref, o in zip(out_refs, outs):
                o_ref[...] = o.astype(o_ref.dtype)

        if nk == 1:
            finish(product())
            return
        acc_ref = rest[-1]
        k = pl.program_id(2)

        @pl.when(k == 0)
        def _():
            acc_ref[...] = product()

        @pl.when((k > 0) & (k < nk - 1))
        def _():
            acc_ref[...] += product()

        @pl.when(k == nk - 1)
        def _():
            finish(acc_ref[...] + product())

    a_spec = pl.BlockSpec((tk, tm), lambda i, j, k: (k, i)) if ta else pl.BlockSpec((tm, tk), lambda i, j, k: (i, k))
    if b_stack and tb:
        per = n_shard // tk
        b_spec = pl.BlockSpec((None, tn, tk), lambda i, j, k: (k // per, j, k % per))
    elif b_stack:
        per = n_shard // tn
        b_spec = pl.BlockSpec((None, tk, tn), lambda i, j, k: (j // per, k, j % per))
    elif tb:
        b_spec = pl.BlockSpec((tn, tk), lambda i, j, k: (j, k))
    else:
        b_spec = pl.BlockSpec((tk, tn), lambda i, j, k: (k, j))
    if out_stack:
        per_out = (n_dim // out_stack) // tn
        out_spec = pl.BlockSpec((None, tm, tn), lambda i, j, k: (j // per_out, i, j % per_out))
        out_dims = (out_stack, m_dim, n_dim // out_stack)
    else:
        out_spec = pl.BlockSpec((tm, tn), lambda i, j, k: (i, j))
        out_dims = (m_dim, n_dim)
    extra_specs = []
    for arr, kind in extras:
        if kind == "tile":
            extra_specs.append(pl.BlockSpec((tm, tn), lambda i, j, k: (i, j)))
        elif isinstance(kind, tuple):
            extra_specs.append(pl.BlockSpec((tm, tn), functools.partial(lambda i, j, k, c: (i, j + c), c=kind[1])))
        elif kind == "row":
            extra_specs.append(pl.BlockSpec((1, tn), lambda i, j, k: (0, j)))
        elif kind == "col":
            extra_specs.append(pl.BlockSpec((tm, 1), lambda i, j, k: (i, 0)))
        else:
            assert kind == "table", kind
            extra_specs.append(pl.BlockSpec((tm, LANES), lambda i, j, k: (i, 0)))
    return pl.pallas_call(
        body,
        name=name,
        grid=(m_dim // tm, n_dim // tn, nk),
        in_specs=[a_spec, b_spec] * (n_lhs // 2) + extra_specs,
        out_specs=[out_spec for _ in out_dtypes],
        out_shape=[jax.ShapeDtypeStruct(out_dims, dt) for dt in out_dtypes],
        scratch_shapes=[pltpu.VMEM((tm, tn), F32)] if nk > 1 else [],
        compiler_params=_params(("parallel", "parallel", "arbitrary")),
    )(a, b, *(second or ()), *[arr for arr, _ in extras])


def _mm_sum(pieces, b, offsets, *, name, tm=1024, tn=1024):
    m_dim, n_dim = pieces[0].shape[0], b.shape[1]
    tm, tn = _near_tile(m_dim, tm), _near_tile(n_dim, tn)
    n_pieces = len(pieces)

    def body(*refs):
        total = None
        for a_ref, b_ref in zip(refs[:n_pieces], refs[n_pieces:2 * n_pieces]):
            part = jnp.dot(a_ref[...], b_ref[...], preferred_element_type=F32)
            total = part if total is None else total + part
        refs[-1][...] = total

    a_specs = [pl.BlockSpec((tm, p.shape[1]), lambda i, j: (i, 0)) for p in pieces]
    b_specs = [pl.BlockSpec((p.shape[1], tn), functools.partial(lambda i, j, blk: (blk, j), blk=off // p.shape[1]))
               for p, off in zip(pieces, offsets)]
    return pl.pallas_call(
        body, name=name, grid=(m_dim // tm, n_dim // tn),
        in_specs=a_specs + b_specs,
        out_specs=pl.BlockSpec((tm, tn), lambda i, j: (i, j)),
        out_shape=jax.ShapeDtypeStruct((m_dim, n_dim), F32),
        compiler_params=_params(("parallel", "parallel")),
    )(*pieces, *[b] * n_pieces)


def _rows(s):
    return min(512, s)


def _rope_tables(pos_col, inv_freq_lanes, *, name):
    s = pos_col.shape[0]
    tb = _rows(s)

    def body(pos_ref, f_ref, cos_ref, sin_ref):
        ang = pos_ref[...].astype(F32) * f_ref[...]
        lane = lax.broadcasted_iota(jnp.int32, ang.shape, 1)
        on = (lane >= QK_NOPE) & (lane < QK_DIM)
        cos_ref[...] = jnp.where(on, jnp.cos(ang), 0.0)
        sin_ref[...] = jnp.where(on, jnp.sin(ang), 0.0)

    return pl.pallas_call(
        body, name=name, grid=(s // tb,),
        in_specs=[pl.BlockSpec((tb, 1), lambda i: (i, 0)), pl.BlockSpec((1, LANES), lambda i: (0, 0))],
        out_specs=[pl.BlockSpec((tb, LANES), lambda i: (i, 0))] * 2,
        out_shape=[jax.ShapeDtypeStruct((s, LANES), F32)] * 2,
        compiler_params=_params(("parallel",)),
    )(pos_col, inv_freq_lanes)


def _rotate_half(x):
    lane = lax.broadcasted_iota(jnp.int32, x.shape, 1)
    half = QK_ROPE // 2
    first = (lane >= QK_NOPE) & (lane < QK_NOPE + half)
    second = (lane >= QK_NOPE + half) & (lane < QK_DIM)
    return jnp.where(first, -pltpu.roll(x, LANES - half, 1), jnp.where(second, pltpu.roll(x, half, 1), 0.0))


def _norm_mod(x, g, sc, sh, *, name):
    s, d = x.shape
    tb = _rows(s)

    def body(x_ref, g_ref, sc_ref, sh_ref, h_ref, r_ref):
        xv = x_ref[...]
        r = lax.rsqrt(jnp.mean(xv * xv, axis=-1, keepdims=True) + EPS)
        r_ref[...] = r
        h_ref[...] = (((xv * r) * g_ref[...]) * (1.0 + sc_ref[...]) + sh_ref[...]).astype(BF16)

    vec = pl.BlockSpec((1, d), lambda i: (0, 0))
    return pl.pallas_call(
        body, name=name, grid=(s // tb,),
        in_specs=[pl.BlockSpec((tb, d), lambda i: (i, 0)), vec, vec, vec],
        out_specs=[pl.BlockSpec((tb, d), lambda i: (i, 0)), pl.BlockSpec((tb, 1), lambda i: (i, 0))],
        out_shape=[jax.ShapeDtypeStruct((s, d), BF16), jax.ShapeDtypeStruct((s, 1), F32)],
        compiler_params=_params(("parallel",)),
    )(x, g, sc, sh)


def _window_sums(ext, sign):
    n = ext.shape[0]
    sums, cur, k = [], ext, 1
    for _ in POOL_WINDOWS:
        cur = cur + pltpu.roll(cur, k if sign > 0 else n - k, 0)
        sums.append(cur)
        k *= 2
    return sums


def _mixer_pre(z, cos_t, sin_t, w_pool, pool_scale, gq, gkv, *, name):
    s = z.shape[0]
    tb = _rows(s)
    hb = tb // HALO

    def body(zcq_ref, zkr_ref, zu_ref, zuh_ref, zckv_ref, cos_ref, sin_ref, wp_ref, ps_ref, gq_ref, gkv_ref,
             p_ref, yp_ref, cq_ref, ckv_ref, kr_ref, rq_ref, rkv_ref):
        i = pl.program_id(0)
        u = zu_ref[...].astype(F32)
        halo = jnp.where(i > 0, zuh_ref[...].astype(F32), 0.0)
        ext = jnp.concatenate([halo, u], axis=0)
        t = i * tb + lax.broadcasted_iota(jnp.int32, (tb, 1), 0)
        for g, (w, sw) in enumerate(zip(POOL_WINDOWS, _window_sums(ext, +1))):
            cols = slice(g * POOL_GROUP, (g + 1) * POOL_GROUP)
            cnt = jnp.minimum(t + 1, w).astype(F32)
            pg = (sw[HALO:, cols] / cnt - u[:, cols]).astype(BF16)
            p_ref[:, cols] = pg
            yg = jnp.dot(pg, wp_ref[g].astype(BF16), preferred_element_type=F32)
            yp_ref[:, cols] = (yg * ps_ref[:, cols]).astype(BF16)

        def rms(x_ref, g_ref, out_ref, r_ref):
            xv = x_ref[...].astype(F32)
            r = lax.rsqrt(jnp.mean(xv * xv, axis=-1, keepdims=True) + EPS)
            r_ref[...] = r
            out_ref[...] = ((xv * r) * g_ref[...]).astype(BF16)

        rms(zcq_ref, gq_ref, cq_ref, rq_ref)
        rms(zckv_ref, gkv_ref, ckv_ref, rkv_ref)
        kr = zkr_ref[...].astype(F32)
        kr_ref[...] = (kr * cos_ref[...] + _rotate_half(kr) * sin_ref[...]).astype(BF16)

    def zcol(width, off):
        return pl.BlockSpec((tb, width), lambda i: (i, off // width))

    def full(a):
        return pl.BlockSpec(a.shape, lambda i: (0,) * a.ndim)

    def out(width, dt):
        return pl.BlockSpec((tb, width), lambda i: (i, 0)), jax.ShapeDtypeStruct((s, width), dt)

    outs = [out(POOL_DIM, BF16), out(POOL_DIM, BF16), out(Q_LORA, BF16), out(KV_LORA, BF16), out(LANES, BF16),
            out(1, F32), out(1, F32)]
    return pl.pallas_call(
        body, name=name, grid=(s // tb,),
        in_specs=[zcol(Q_LORA, ZC_CQ), zcol(LANES, ZC_KR), zcol(POOL_DIM, ZC_U),
                  pl.BlockSpec((HALO, POOL_DIM), lambda i: (jnp.maximum(i * hb - 1, 0), ZC_U // POOL_DIM)),
                  zcol(KV_LORA, ZC_CKV),
                  pl.BlockSpec((tb, LANES), lambda i: (i, 0)), pl.BlockSpec((tb, LANES), lambda i: (i, 0)),
                  full(w_pool), full(pool_scale), full(gq), full(gkv)],
        out_specs=[o[0] for o in outs], out_shape=[o[1] for o in outs],
        compiler_params=_params(("parallel",)),
    )(z, z, z, z, z, cos_t, sin_t, w_pool, pool_scale, gq, gkv)


def _sigmoid(x):
    return 1.0 / (1.0 + jnp.exp(-x.astype(F32)))


ATTN_SCALE = 1.0 / math.sqrt(QK_DIM)
NEG_BIG = -1e30


LOG2_E = math.log2(math.e)
EXP2_SCALE = ATTN_SCALE * LOG2_E
NT_DIMS = (((1,), (1,)), ((), ()))
TN_DIMS = (((0,), (0,)), ((), ()))


def _on_or_below_diagonal(t):
    return lax.broadcasted_iota(jnp.int32, (t, t), 0) >= lax.broadcasted_iota(jnp.int32, (t, t), 1)


HEADS_PER_STEP = 2
HEAD_COLS = [slice(g * HEAD_PAD, (g + 1) * HEAD_PAD) for g in range(HEADS_PER_STEP)]


def _attn_fwd(q, k, v, *, name):
    s = q.shape[0]
    t = _rows(s)
    wide = HEADS_PER_STEP * HEAD_PAD

    def body(q_ref, k_ref, v_ref, o_ref, lse_ref):
        qi = pl.program_id(1)
        qs = [q_ref[:, cols] for cols in HEAD_COLS]

        def block(j, carry, diagonal):
            rows = pl.ds(pl.multiple_of(j * t, t), t)
            out = []
            for qv, cols, (m, l, acc) in zip(qs, HEAD_COLS, carry):
                sc = lax.dot_general(qv, k_ref[rows, cols], NT_DIMS, preferred_element_type=F32)
                if diagonal:
                    sc = jnp.where(_on_or_below_diagonal(t), sc, NEG_BIG)
                m_new = jnp.maximum(m, jnp.max(sc, axis=-1, keepdims=True))
                p = jnp.exp2((sc - m_new) * EXP2_SCALE)
                alpha = jnp.exp2((m - m_new) * EXP2_SCALE)
                l = alpha * l + jnp.sum(p, axis=-1, keepdims=True)
                acc = alpha * acc + jnp.dot(p.astype(BF16), v_ref[rows, cols], preferred_element_type=F32)
                out.append((m_new, l, acc))
            return tuple(out)

        init = tuple((jnp.full((t, 1), -jnp.inf, F32), jnp.zeros((t, 1), F32), jnp.zeros((t, HEAD_PAD), F32))
                     for _ in HEAD_COLS)
        carry = lax.fori_loop(0, qi, lambda j, c: block(j, c, False), init)
        for g, (cols, (m, l, acc)) in enumerate(zip(HEAD_COLS, block(qi, carry, True))):
            o_ref[:, cols] = (acc / l).astype(BF16)
            lse_ref[g] = m * ATTN_SCALE + jnp.log(l)

    q_spec = pl.BlockSpec((t, wide), lambda h, i: (i, h))
    kv_spec = pl.BlockSpec((s, wide), lambda h, i: (0, h))
    return pl.pallas_call(
        body, name=name, grid=(N_HEADS // HEADS_PER_STEP, s // t),
        in_specs=[q_spec, kv_spec, kv_spec],
        out_specs=[q_spec, pl.BlockSpec((HEADS_PER_STEP, t, 1), lambda h, i: (h, i, 0))],
        out_shape=[jax.ShapeDtypeStruct((s, N_HEADS * HEAD_PAD), BF16), jax.ShapeDtypeStruct((N_HEADS, s, 1), F32)],
        compiler_params=_params(("parallel", "parallel")),
    )(q, k, v)


def _attn_bwd(q, k, v, do, o, lse, cos_t, sin_t, *, name):
    s = q.shape[0]
    t = _rows(s)
    nt = s // t

    def body(q_ref, k_ref, v_ref, do_ref, o_ref, lse_ref, cos_ref, sin_ref, dql_ref, dk_ref, dv_ref, dks_ref,
             dq_ref, dl_ref):
        kj = pl.program_id(1)

        @pl.when(kj == 0)
        def _():
            dq_ref[...] = jnp.zeros_like(dq_ref)

            def delta(i, carry):
                rows = pl.ds(pl.multiple_of(i * t, t), t)
                for g, cols in enumerate(HEAD_COLS):
                    dl_ref[g, rows, :] = jnp.sum(do_ref[rows, cols].astype(F32) * o_ref[rows, cols].astype(F32),
                                                 axis=-1, keepdims=True)
                return carry

            lax.fori_loop(0, nt, delta, 0)

        kvs = [(k_ref[:, cols], v_ref[:, cols]) for cols in HEAD_COLS]

        def block(i, carry, diagonal):
            rows = pl.ds(pl.multiple_of(i * t, t), t)
            out = []
            for g, (cols, (kv, vv), (dk, dv)) in enumerate(zip(HEAD_COLS, kvs, carry)):
                qv, dov = q_ref[rows, cols], do_ref[rows, cols]
                sc = lax.dot_general(qv, kv, NT_DIMS, preferred_element_type=F32)
                p = jnp.exp2(sc * EXP2_SCALE - lse_ref[g, rows, :] * LOG2_E)
                if diagonal:
                    p = jnp.where(_on_or_below_diagonal(t), p, 0.0)
                dp = lax.dot_general(dov, vv, NT_DIMS, preferred_element_type=F32)
                ds = (p * (dp - dl_ref[g, rows, :])).astype(BF16)
                dv = dv + lax.dot_general(p.astype(BF16), dov, TN_DIMS, preferred_element_type=F32)
                dk = dk + lax.dot_general(ds, qv, TN_DIMS, preferred_element_type=F32)
                dq_ref[rows, cols] += jnp.dot(ds, kv, preferred_element_type=F32) * ATTN_SCALE
                out.append((dk, dv))
            return tuple(out)

        zero = jnp.zeros((t, HEAD_PAD), F32)
        carry = block(kj, tuple((zero, zero) for _ in HEAD_COLS), True)
        dk_sum = None
        for cols, (dk, dv) in zip(HEAD_COLS, lax.fori_loop(kj + 1, nt, lambda i, c: block(i, c, False), carry)):
            dk = dk * ATTN_SCALE
            dk_ref[:, cols] = dk.astype(BF16)
            dv_ref[:, cols] = dv.astype(BF16)
            dk_sum = dk if dk_sum is None else dk_sum + dk
        dks_ref[...] = dk_sum

        @pl.when(kj == nt - 1)
        def _():
            def rope_bwd(i, carry):
                rows = pl.ds(pl.multiple_of(i * t, t), t)
                sin = sin_ref[rows, :]
                lane = lax.broadcasted_iota(jnp.int32, sin.shape, 1)
                cos_q = cos_ref[rows, :] + jnp.where(lane < QK_NOPE, 1.0, 0.0)
                for cols in HEAD_COLS:
                    dqv = dq_ref[rows, cols]
                    dql_ref[rows, cols] = (dqv * cos_q - _rotate_half(dqv * sin)).astype(BF16)
                return carry

            lax.fori_loop(0, nt, rope_bwd, 0)

    heads_wide = HEADS_PER_STEP * HEAD_PAD
    full_spec = pl.BlockSpec((s, heads_wide), lambda h, j: (0, h))
    kv_spec = pl.BlockSpec((t, heads_wide), lambda h, j: (j, h))
    vec_spec = pl.BlockSpec((HEADS_PER_STEP, s, 1), lambda h, j: (h, 0, 0))
    table_spec = pl.BlockSpec((s, LANES), lambda h, j: (0, 0))
    wide = jax.ShapeDtypeStruct((s, N_HEADS * HEAD_PAD), BF16)
    n_steps = N_HEADS // HEADS_PER_STEP
    return pl.pallas_call(
        body, name=name, grid=(n_steps, nt),
        in_specs=[full_spec, kv_spec, kv_spec, full_spec, full_spec, vec_spec, table_spec, table_spec],
        out_specs=[full_spec, kv_spec, kv_spec, pl.BlockSpec((None, t, HEAD_PAD), lambda h, j: (h, j, 0))],
        out_shape=[wide, wide, wide, jax.ShapeDtypeStruct((n_steps, s, HEAD_PAD), F32)],
        scratch_shapes=[pltpu.VMEM((s, heads_wide), F32), pltpu.VMEM((HEADS_PER_STEP, s, 1), F32)],
        compiler_params=_params(("parallel", "arbitrary")),
    )(q, k, v, do, o, lse, cos_t, sin_t)


def _acc_specs(widths):
    return ([pl.BlockSpec((1, w), lambda i: (0, 0)) for w in widths],
            [jax.ShapeDtypeStruct((1, w), F32) for w in widths])


def _gate_grads(dxv, m_ref, gate_ref, dm_ref, dgate_ref):
    dm_ref[...] = (dxv * gate_ref[...]).astype(BF16)
    dgate_ref[...] += jnp.sum(dxv * m_ref[...], axis=0, keepdims=True)


def _final_loss(x, g, target, m, gate, *, name):
    s, d = x.shape
    tb = _rows(s)

    def body(x_ref, g_ref, t_ref, m_ref, gate_ref, dx_ref, loss_ref, dg_ref, dm_ref, dgate_ref):
        @pl.when(pl.program_id(0) == 0)
        def _():
            loss_ref[...] = jnp.zeros_like(loss_ref)
            dg_ref[...] = jnp.zeros_like(dg_ref)
            dgate_ref[...] = jnp.zeros_like(dgate_ref)

        xv = x_ref[...]
        r = lax.rsqrt(jnp.mean(xv * xv, axis=-1, keepdims=True) + EPS)
        xn = xv * r
        err = xn * g_ref[...] - t_ref[...]
        loss_ref[...] += 0.5 * jnp.sum(jnp.mean(err * err, axis=-1, keepdims=True), axis=0, keepdims=True)
        dy = err / d
        dg_ref[...] += jnp.sum(dy * xn, axis=0, keepdims=True)
        dxn = dy * g_ref[...]
        dxv = r * (dxn - xn * jnp.mean(dxn * xn, axis=-1, keepdims=True))
        dx_ref[...] = dxv
        _gate_grads(dxv, m_ref, gate_ref, dm_ref, dgate_ref)

    blk = pl.BlockSpec((tb, d), lambda i: (i, 0))
    vec = pl.BlockSpec((1, d), lambda i: (0, 0))
    acc_specs, acc_shapes = _acc_specs((LANES, d))
    return pl.pallas_call(
        body, name=name, grid=(s // tb,),
        in_specs=[blk, vec, blk, blk, vec],
        out_specs=[blk] + acc_specs + [blk, vec],
        out_shape=[jax.ShapeDtypeStruct((s, d), F32)] + acc_shapes + [jax.ShapeDtypeStruct((s, d), BF16),
                                                                     jax.ShapeDtypeStruct((1, d), F32)],
        compiler_params=_params(("arbitrary",)),
    )(x, g, target, m, gate)


def _norm_mod_bwd(dh, x, r, g, sc, dx_skip, *, name, gate=None):
    s, d = x.shape
    tb = _rows(s)
    nb = s // tb
    n_gate = 2 if gate else 0

    def body(dh_ref, x_ref, r_ref, g_ref, sc_ref, skip_ref, *rest):
        gate_refs, (dx_ref, dg_ref, dsc_ref, dsh_ref) = rest[:n_gate], rest[n_gate:n_gate + 4]
        gate_outs, da_sc = rest[n_gate + 4:-1], rest[-1]
        i = pl.program_id(0)

        @pl.when(i == 0)
        def _():
            da_sc[...] = jnp.zeros_like(da_sc)
            dsh_ref[...] = jnp.zeros_like(dsh_ref)
            if gate:
                gate_outs[1][...] = jnp.zeros_like(gate_outs[1])

        dhv, rv = dh_ref[...], r_ref[...]
        xn = x_ref[...] * rv
        dsh_ref[...] += jnp.sum(dhv, axis=0, keepdims=True)
        da_sc[...] += jnp.sum(dhv * xn, axis=0, keepdims=True)
        dxn = dhv * (g_ref[...] * (1.0 + sc_ref[...]))
        dxv = skip_ref[...] + rv * (dxn - xn * jnp.mean(dxn * xn, axis=-1, keepdims=True))
        dx_ref[...] = dxv
        if gate:
            _gate_grads(dxv, *gate_refs, *gate_outs)

        @pl.when(i == nb - 1)
        def _():
            dg_ref[...] = da_sc[...] * (1.0 + sc_ref[...])
            dsc_ref[...] = da_sc[...] * g_ref[...]

    blk = pl.BlockSpec((tb, d), lambda i: (i, 0))
    vec = pl.BlockSpec((1, d), lambda i: (0, 0))
    acc_specs, acc_shapes = _acc_specs((d, d, d))
    gate_specs = [blk, vec] if gate else []
    gate_shapes = [jax.ShapeDtypeStruct((s, d), BF16), jax.ShapeDtypeStruct((1, d), F32)] if gate else []
    return pl.pallas_call(
        body, name=name, grid=(nb,),
        in_specs=[blk, blk, pl.BlockSpec((tb, 1), lambda i: (i, 0)), vec, vec, blk] + gate_specs,
        out_specs=[blk] + acc_specs + gate_specs,
        out_shape=[jax.ShapeDtypeStruct((s, d), F32)] + acc_shapes + gate_shapes,
        scratch_shapes=[pltpu.VMEM((1, d), F32)],
        compiler_params=_params(("arbitrary",)),
    )(dh, x, r, g, sc, dx_skip, *(gate or ()))


def _pool_bwd(dyp, p, w_pool, pool_scale, *, name):
    s = dyp.shape[0]
    tb = _rows(s)
    nb = s // tb
    hb = tb // HALO
    nt_dims = (((1,), (1,)), ((), ()))
    tn_dims = (((0,), (0,)), ((), ()))

    def body(dy_ref, dyn_ref, p_ref, wp_ref, ps_ref, du_ref, gwp_ref, gps_ref):
        i = pl.program_id(0)

        @pl.when(i == 0)
        def _():
            gwp_ref[...] = jnp.zeros_like(gwp_ref)
            gps_ref[...] = jnp.zeros_like(gps_ref)

        cur = dy_ref[...]
        nxt = jnp.where(i < nb - 1, dyn_ref[...], 0.0)
        dpw = (jnp.concatenate([cur, nxt], axis=0) * ps_ref[...]).astype(BF16)
        t = i * tb + lax.broadcasted_iota(jnp.int32, (tb + HALO, 1), 0)
        for g, w in enumerate(POOL_WINDOWS):
            cols = slice(g * POOL_GROUP, (g + 1) * POOL_GROUP)
            wg = wp_ref[g].astype(BF16)
            dp = lax.dot_general(dpw[:, cols], wg, nt_dims, preferred_element_type=F32)
            e = dp / jnp.minimum(t + 1, w).astype(F32)
            lead = _window_sums(e, -1)[g]
            du_ref[:, cols] = (lead[:tb] - dp[:tb]).astype(BF16)
            pg = p_ref[:, cols]
            pw = jnp.dot(pg, wg, preferred_element_type=F32)
            gps_ref[:, cols] += jnp.sum(cur[:, cols] * pw, axis=0, keepdims=True)
            gwp_ref[g] += lax.dot_general(pg, dpw[:tb, cols], tn_dims, preferred_element_type=F32)

    blk = pl.BlockSpec((tb, POOL_DIM), lambda i: (i, 0))
    return pl.pallas_call(
        body, name=name, grid=(nb,),
        in_specs=[blk, pl.BlockSpec((HALO, POOL_DIM), lambda i: (jnp.minimum((i + 1) * hb, s // HALO - 1), 0)), blk,
                  pl.BlockSpec(w_pool.shape, lambda i: (0, 0, 0)), pl.BlockSpec((1, POOL_DIM), lambda i: (0, 0))],
        out_specs=[blk, pl.BlockSpec(w_pool.shape, lambda i: (0, 0, 0)), pl.BlockSpec((1, POOL_DIM), lambda i: (0, 0))],
        out_shape=[jax.ShapeDtypeStruct((s, POOL_DIM), BF16), jax.ShapeDtypeStruct(w_pool.shape, F32),
                   jax.ShapeDtypeStruct((1, POOL_DIM), F32)],
        compiler_params=_params(("arbitrary",)),
    )(dyp, dyp, p, w_pool, pool_scale)


def _key_bwd(dk_sums, cos_t, sin_t, *, name):
    n, s, _ = dk_sums.shape
    tb = _rows(s)

    def body(dk_ref, cos_ref, sin_ref, dkr_ref):
        tot = dk_ref[0]
        for h in range(1, n):
            tot = tot + dk_ref[h]
        dkr_ref[...] = (tot * cos_ref[...] - _rotate_half(tot * sin_ref[...])).astype(BF16)

    tab = pl.BlockSpec((tb, LANES), lambda i: (i, 0))
    return pl.pallas_call(
        body, name=name, grid=(s // tb,),
        in_specs=[pl.BlockSpec((n, tb, LANES), lambda i: (0, i, 0)), tab, tab], out_specs=tab,
        out_shape=jax.ShapeDtypeStruct((s, LANES), BF16),
        compiler_params=_params(("parallel",)),
    )(dk_sums, cos_t, sin_t)


def _rms_bwd(dy, z, z_off, r, g, *, name):
    s, n = dy.shape
    tb = _rows(s)

    def body(dy_ref, x_ref, r_ref, g_ref, dx_ref, dg_ref):
        @pl.when(pl.program_id(0) == 0)
        def _():
            dg_ref[...] = jnp.zeros_like(dg_ref)

        dyv, rv = dy_ref[...], r_ref[...]
        xn = x_ref[...].astype(F32) * rv
        dg_ref[...] += jnp.sum(dyv * xn, axis=0, keepdims=True)
        dxn = dyv * g_ref[...]
        dx_ref[...] = (rv * (dxn - xn * jnp.mean(dxn * xn, axis=-1, keepdims=True))).astype(BF16)

    blk = pl.BlockSpec((tb, n), lambda i: (i, 0))
    acc_specs, acc_shapes = _acc_specs((n,))
    return pl.pallas_call(
        body, name=name, grid=(s // tb,),
        in_specs=[blk, pl.BlockSpec((tb, n), lambda i: (i, z_off // n)), pl.BlockSpec((tb, 1), lambda i: (i, 0)),
                  pl.BlockSpec((1, n), lambda i: (0, 0))],
        out_specs=[blk] + acc_specs, out_shape=[jax.ShapeDtypeStruct((s, n), BF16)] + acc_shapes,
        compiler_params=_params(("arbitrary",)),
    )(dy, z, r, g)


def _silu(c, *, name):
    def body(c_ref, out_ref):
        cv = c_ref[...]
        out_ref[...] = (cv * _sigmoid(cv)).astype(BF16)

    return pl.pallas_call(body, name=name, out_shape=jax.ShapeDtypeStruct(c.shape, BF16),
                          compiler_params=_params())(c)


def _sum_slots(a, n, *, name, out_dtype=F32):
    _, rows, cols = a.shape
    tr = _tile(rows, 256, 8)

    def body(a_ref, out_ref):
        tot = a_ref[0].astype(F32)
        for j in range(1, n):
            tot = tot + a_ref[j].astype(F32)
        out_ref[...] = tot.astype(out_dtype)

    return pl.pallas_call(
        body, name=name, grid=(rows // tr,),
        in_specs=[pl.BlockSpec((n, tr, cols), lambda i: (0, i, 0))],
        out_specs=pl.BlockSpec((tr, cols), lambda i: (i, 0)),
        out_shape=jax.ShapeDtypeStruct((rows, cols), out_dtype),
        compiler_params=_params(("parallel",)),
    )(a)


def _add2_stacked(a, b, stacked, l, *, name):
    rows, cols = a.shape
    tr = _tile(rows, 256, 8)

    def body(a_ref, b_ref, *rest):
        rest[-1][...] = a_ref[...] + b_ref[...]

    blk = pl.BlockSpec((tr, cols), lambda i: (i, 0))
    carried = [] if stacked is None else [stacked]
    return pl.pallas_call(
        body, name=name, grid=(rows // tr,),
        in_specs=[blk, blk] + [pl.BlockSpec(memory_space=pl.ANY) for _ in carried],
        out_specs=pl.BlockSpec((None, tr, cols), lambda i: (l, i, 0)),
        out_shape=jax.ShapeDtypeStruct((DEPTH, rows, cols), F32),
        input_output_aliases={2: 0} if carried else {},
        compiler_params=_params(("parallel",)),
    )(a, b, *carried)


def _adamw(w, g, m, v, *, name):
    shape = w.shape
    if w.ndim == 2:
        w, g, m, v = (a.reshape((1,) + shape) for a in (w, g, m, v))
    layers, rows, cols = w.shape
    tr = _tile(rows, max(8, (1 << 18) // cols), 8)
    c1 = 1.0 - ADAM_B1 ** ADAM_STEP
    c2 = 1.0 - ADAM_B2 ** ADAM_STEP

    def body(w_ref, g_ref, m_ref, v_ref, d_ref, nm_ref, nv_ref):
        gv = g_ref[...]
        nm = ADAM_B1 * m_ref[...] + (1.0 - ADAM_B1) * gv
        nv = ADAM_B2 * v_ref[...] + (1.0 - ADAM_B2) * (gv * gv)
        nm_ref[...] = nm
        nv_ref[...] = nv
        d_ref[...] = -ADAM_LR * ((nm / c1) / (jnp.sqrt(nv / c2) + ADAM_EPS) + ADAM_WD * w_ref[...])

    blk = pl.BlockSpec((None, tr, cols), lambda l, i: (l, i, 0))
    outs = pl.pallas_call(
        body, name=name, grid=(layers, rows // tr), in_specs=[blk] * 4, out_specs=[blk] * 3,
        out_shape=[jax.ShapeDtypeStruct((layers, rows, cols), F32)] * 3,
        compiler_params=_params(("parallel", "parallel")),
    )(w, g, m, v)
    return [o.reshape(shape) for o in outs]


def _coords():
    return lax.axis_index("x"), lax.axis_index("y"), lax.axis_index("c")


def _other_chips(x, y):
    return [(1 - x, y), (x, 1 - y), (1 - x, 1 - y)]


def _all_gather_small(blk, *, name):
    m_per, n = blk.shape

    def body(x_ref, out_ref, send_sems, recv_sems, local_sem):
        x, y, c = _coords()
        me, sibling = (x, y, c), (x, y, 1 - c)
        chips = _other_chips(x, y)

        def rows(px, py, pc):
            return out_ref.at[pl.ds((4 * px + 2 * py + pc) * m_per, m_per), :]

        def copy(k, block, to, src=None):
            return pltpu.make_async_remote_copy(
                src_ref=rows(*block) if src is None else src, dst_ref=rows(*block),
                send_sem=send_sems.at[k], recv_sem=recv_sems.at[k], device_id=to, device_id_type=MESH)

        mine = pltpu.make_async_copy(x_ref, rows(*me), local_sem)
        mine.start()
        first = [copy(0, me, sibling, src=x_ref)]
        first += [copy(1 + j, me, (*chip, c), src=x_ref) for j, chip in enumerate(chips)]
        for cp in first:
            cp.start()
        passed = [copy(4 + j, (*chip, c), sibling) for j, chip in enumerate(chips)]
        for j, chip in enumerate(chips):
            copy(1 + j, (*chip, c), me).wait_recv()
            passed[j].start()
        copy(0, sibling, me).wait_recv()
        for j, chip in enumerate(chips):
            copy(4 + j, (*chip, 1 - c), me).wait_recv()
        for cp in first + passed:
            cp.wait_send()
        mine.wait()

    return pl.pallas_call(
        body, name=name,
        out_shape=jax.ShapeDtypeStruct((N_DEV * m_per, n), blk.dtype),
        in_specs=[pl.BlockSpec(memory_space=pltpu.VMEM)],
        out_specs=pl.BlockSpec(memory_space=pltpu.VMEM),
        scratch_shapes=[pltpu.SemaphoreType.DMA((7,)), pltpu.SemaphoreType.DMA((7,)), pltpu.SemaphoreType.DMA],
        compiler_params=_params(),
    )(blk)


HBM_SPEC = pl.BlockSpec(memory_space=pltpu.HBM)
SEM_SPEC = pl.BlockSpec(memory_space=pltpu.SEMAPHORE)
DATAFLOW = pltpu.SideEffectType.DATAFLOW_SIDE_EFFECTING


def _chip_copies(src_ref, land_ref, send_sems, recv_sems, scatter):
    x, y, c = _coords()
    my = 2 * x + y
    outgoing, incoming = [], []
    for k, (px, py) in enumerate(_other_chips(x, y)):
        peer = 2 * px + py

        def copy(src_slot, dst_slot):
            return pltpu.make_async_remote_copy(
                src_ref=src_ref.at[src_slot] if scatter else src_ref, dst_ref=land_ref.at[dst_slot],
                send_sem=send_sems.at[k], recv_sem=recv_sems.at[k], device_id=(px, py, c), device_id_type=MESH)

        outgoing.append(copy(peer, my))
        incoming.append(copy(my, peer))
    return outgoing, incoming


def _exchange_start(srcs, *, name, scatter):
    n = len(srcs)
    land_shapes = [src.shape if scatter else (N_CHIPS,) + src.shape for src in srcs]

    def body(*refs):
        for k in range(n):
            send_sems, recv_sems = refs[2 * n + 4 * k], refs[2 * n + 4 * k + 1]
            outgoing, _ = _chip_copies(refs[k], refs[n + k], send_sems, recv_sems, scatter)
            for cp in outgoing:
                cp.start()
        refs[-1][...] = jnp.zeros_like(refs[-1])

    out_shape, out_specs, aliases = [], [], {}
    for k, (src, land_shape) in enumerate(zip(srcs, land_shapes)):
        out_shape += [pltpu.SemaphoreType.DMA((N_CHIPS - 1,)), pltpu.SemaphoreType.DMA((N_CHIPS - 1,)),
                      pltpu.HBM(src.shape, src.dtype), pltpu.HBM(land_shape, src.dtype)]
        out_specs += [SEM_SPEC, SEM_SPEC, HBM_SPEC, HBM_SPEC]
        aliases.update({k: 4 * k + 2, n + k: 4 * k + 3})
    outs = pl.pallas_call(
        body, name=name,
        out_shape=tuple(out_shape) + (jax.ShapeDtypeStruct((8, LANES), F32),),
        in_specs=(HBM_SPEC,) * (2 * n),
        out_specs=tuple(out_specs) + (pl.BlockSpec(memory_space=pltpu.VMEM),),
        input_output_aliases=aliases,
        compiler_params=pltpu.CompilerParams(has_side_effects=DATAFLOW),
    )(*[pltpu.with_memory_space_constraint(src, pltpu.HBM) for src in srcs],
      *[pltpu.with_memory_space_constraint(lax.empty(shape, src.dtype), pltpu.HBM)
        for src, shape in zip(srcs, land_shapes)])
    return [tuple(outs[4 * k:4 * k + 4]) for k in range(n)], outs[-1]


def _exchange_wait(started, after, *, name, scatter):
    send_sems, recv_sems, src_thru, land_thru = started

    def body(src_ref, land_ref, send_sems, recv_sems, after_ref, src_dead, got_ref):
        outgoing, incoming = _chip_copies(src_ref, land_ref, send_sems, recv_sems, scatter)
        for cp in outgoing:
            cp.wait_send()
        for cp in incoming:
            cp.wait_recv()

    return pl.pallas_call(
        body, name=name,
        out_shape=(pltpu.HBM(src_thru.shape, src_thru.dtype), pltpu.HBM(land_thru.shape, land_thru.dtype)),
        in_specs=(HBM_SPEC, HBM_SPEC, SEM_SPEC, SEM_SPEC, pl.BlockSpec(memory_space=pl.ANY)),
        out_specs=(HBM_SPEC, HBM_SPEC),
        input_output_aliases={0: 0, 1: 1},
        compiler_params=pltpu.CompilerParams(has_side_effects=DATAFLOW),
    )(src_thru, land_thru, send_sems, recv_sems, after)


def _exchange_sibling(srcs, *, name):
    n = len(srcs)

    def body(*refs):
        src_refs, out_refs, send_sems, recv_sems = refs[:n], refs[n:2 * n], refs[2 * n], refs[2 * n + 1]
        x, y, c = _coords()
        copies = [pltpu.make_async_remote_copy(src_ref=src_ref, dst_ref=out_ref, send_sem=send_sems.at[k],
                                               recv_sem=recv_sems.at[k], device_id=(x, y, 1 - c), device_id_type=MESH)
                  for k, (src_ref, out_ref) in enumerate(zip(src_refs, out_refs))]
        for cp in copies:
            cp.start()
        for cp in copies:
            cp.wait()

    return pl.pallas_call(
        body, name=name,
        out_shape=[jax.ShapeDtypeStruct(src.shape, src.dtype) for src in srcs],
        in_specs=[pl.BlockSpec(memory_space=pl.ANY)] * n,
        out_specs=[pl.BlockSpec(memory_space=pl.ANY)] * n,
        scratch_shapes=[pltpu.SemaphoreType.DMA((n,)), pltpu.SemaphoreType.DMA((n,))],
        compiler_params=_params(),
    )(*srcs)


def _pack_rows(a):
    return a.reshape(-1, D_MODEL)


def _pad_heads(w, width):
    r = w.shape[0]
    return jnp.pad(w, ((0, 0), (0, 0), (0, HEAD_PAD - width))).reshape(r, N_HEADS * HEAD_PAD)


MIX_NAMES = ("w_uq", "w_uk", "w_uv", "p_pool", "p_attn", "w_out")
GROUPS = ("in", "mix", "ff1", "ff2")


def _local_shard(weights, l, group, zero):
    if group == "mix":
        shard = jnp.concatenate([_pack_rows(weights[n][l]) for n in MIX_NAMES], axis=0)
    else:
        shard = weights[{"in": "w_in", "ff1": "w_ff1", "ff2": "w_ff2"}[group]][l]
    return (shard + zero).astype(BF16)


def _unpack_weights(gathered, group):
    def cols(a, k):
        return a.reshape(N_CHIPS, k, -1).transpose(1, 0, 2).reshape(k, -1)

    if group == "in":
        full = gathered.reshape(W_IN_COLS, D_MODEL)
        u, cq, ckv, kr, gates = (full[a:b] for a, b in (W_IN_U, W_IN_CQ, W_IN_CKV, W_IN_KR, W_IN_GATES))
        kr = jnp.pad(kr, ((QK_NOPE, HEAD_PAD - QK_DIM), (0, 0)))
        return dict(w_in=jnp.concatenate([cq, kr, u, gates, ckv], axis=0))
    if group == "ff1":
        return dict(w_ff1=gathered)
    if group == "ff2":
        return dict(w_ff2=gathered.reshape(D_FF, D_MODEL))

    def p_attn(a):
        full = cols(a, ATTN_DIM).reshape(N_HEADS, V_DIM, D_MODEL)
        return jnp.pad(full, ((0, 0), (0, HEAD_PAD - V_DIM), (0, 0))).reshape(N_HEADS * HEAD_PAD, D_MODEL)

    build = dict(
        w_uq=lambda a: _pad_heads(a.reshape(Q_LORA, N_HEADS, QK_DIM), QK_DIM),
        w_uk=lambda a: _pad_heads(a.reshape(KV_LORA, N_HEADS, QK_NOPE), QK_NOPE),
        w_uv=lambda a: _pad_heads(a.reshape(KV_LORA, N_HEADS, V_DIM), V_DIM),
        p_pool=lambda a: cols(a, POOL_DIM),
        p_attn=p_attn,
        w_out=lambda a: a.reshape(D_MODEL, D_MODEL),
    )
    w, off = {}, 0
    for name in MIX_NAMES:
        w[name] = build[name](gathered[:, off:off + ROWS_OF[name]])
        off += ROWS_OF[name]
    return w


def _pack_grads(g, group):
    def cols(a):
        k = a.shape[0]
        return a.reshape(k, N_CHIPS, -1).transpose(1, 0, 2).reshape(N_CHIPS, -1, D_MODEL)

    def rows(a):
        return a.reshape(N_CHIPS, -1, D_MODEL)

    def heads(width):
        return lambda a: rows(a.reshape(a.shape[0], N_HEADS, HEAD_PAD)[:, :, :width])

    if group == "in":
        full = jnp.concatenate([g["u"], g["cq"], g["ckv"], g["kr"][QK_NOPE:QK_DIM], g["ga"], g["gb"]], axis=0)
        return full.reshape(N_CHIPS, W_IN_SHARD, D_MODEL)
    if group == "ff1":
        return g["w_ff1"]
    if group == "ff2":
        return g["w_ff2"].reshape(N_CHIPS, D_FF // N_CHIPS, D_MODEL)

    def p_attn(a):
        return cols(a.reshape(N_HEADS, HEAD_PAD, D_MODEL)[:, :V_DIM].reshape(ATTN_DIM, D_MODEL))

    build = dict(w_uq=heads(QK_DIM), w_uk=heads(QK_NOPE), w_uv=heads(V_DIM), p_pool=cols, p_attn=p_attn, w_out=rows)
    return jnp.concatenate([build[name](g[name]) for name in MIX_NAMES], axis=1)


def _per_head(fn, acc, *tables):
    return jnp.concatenate([fn(acc[:, h * HEAD_PAD:(h + 1) * HEAD_PAD], *tables) for h in range(N_HEADS)], axis=1)


def _rope_head(a, cos, sin):
    lane = lax.broadcasted_iota(jnp.int32, a.shape, 1)
    return a * (cos + jnp.where(lane < QK_NOPE, 1.0, 0.0)) + _rotate_half(a) * sin


def _layer_fwd(l, x, mod, get_weights, small, cos_t, sin_t):
    sh1, sc1, g1, sh2, sc2, g2 = mod
    tag = f"_l{l}"
    h, r1 = _norm_mod(x, small["ln1_g"], sc1, sh1, name="norm1" + tag)
    w = dict(get_weights("in", h))
    (z,) = _mm(h, w["w_in"], tb=True, name="in_proj" + tag, out_dtypes=(BF16,))
    p, yp, cq, ckv, kr, rq, rkv = _mixer_pre(z, cos_t, sin_t, small["w_pool"], small["pool_scale"],
                                              small["q_norm_g"], small["kv_norm_g"], name="mixer_pre" + tag)
    w.update(get_weights("mix", yp))
    (ya,) = _mm(yp, w["p_pool"], name="pool_out" + tag, out_dtypes=(BF16,))
    (q,) = _mm(cq, w["w_uq"], name="q_proj" + tag, out_dtypes=(BF16,),
               epilogue=lambda acc, cos, sin: (_per_head(_rope_head, acc, cos, sin),),
               extras=((cos_t, "table"), (sin_t, "table")))
    (k,) = _mm(ckv, w["w_uk"], name="k_proj" + tag, out_dtypes=(BF16,),
               epilogue=lambda acc, krv: (_per_head(lambda a, b: a + b, acc, krv),), extras=((kr, "table"),))
    (v,) = _mm(ckv, w["w_uv"], name="v_proj" + tag, out_dtypes=(BF16,))
    o, lse = _attn_fwd(q, k, v, name="attn_fwd" + tag)
    yb, merged = _mm(o, w["p_attn"], name="attn_out" + tag, out_dtypes=(BF16, BF16), tm=512,
                     epilogue=lambda acc, ga, gb, yav: (acc, _sigmoid(ga) * yav + _sigmoid(gb) * acc),
                     extras=((z, ("tile", ZC_GA // D_MODEL)), (z, ("tile", ZC_GB // D_MODEL)), (ya, "tile")))
    mo, x1 = _mm(merged, w["w_out"], name="mix_out" + tag, out_dtypes=(BF16, F32),
                 epilogue=lambda acc, xr, g: (acc, xr + g * acc), extras=((x, "tile"), (g1, "row")))
    h2, r2 = _norm_mod(x1, small["ln2_g"], sc2, sh2, name="norm2" + tag)
    w.update(get_weights("ff1", merged))
    f, act = _mm(h2, w["w_ff1"], b_stack=True, name="ff1" + tag, out_dtypes=(BF16, BF16),
                 epilogue=lambda acc: (acc, jnp.square(jnp.maximum(acc, 0.0))))
    w.update(get_weights("ff2", act))
    m2, x2 = _mm(act, w["w_ff2"], name="ff2" + tag, out_dtypes=(BF16, F32),
                 epilogue=lambda acc, xr, g: (acc, xr + g * acc), extras=((x1, "tile"), (g2, "row")))
    saved = dict(x=x, h=h, r1=r1, z=z, p=p, yp=yp, cq=cq, ckv=ckv, rq=rq, rkv=rkv, ya=ya, q=q, k=k, v=v, o=o, lse=lse,
                 yb=yb, merged=merged, mo=mo, x1=x1, h2=h2, r2=r2, f=f, act=act, m2=m2)
    return x2, saved, w


def _merge_grads(dm, ga, gb, ya, yb):
    sa, sb = _sigmoid(ga), _sigmoid(gb)
    return dm * sa, dm * sb, dm * ya * (sa * (1.0 - sa)), dm * yb * (sb * (1.0 - sb))


def _layer_bwd(l, dx2, dm2, dg2, sv, mod, w, small, cos_t, sin_t, send_grads, gate_below):
    sh1, sc1, g1, sh2, sc2, g2 = mod
    tag = f"_l{l}"
    gw = {}
    (df,) = _mm(dm2, w["w_ff2"], tb=True, name="ff2_dx" + tag, out_dtypes=(BF16,),
                epilogue=lambda acc, f: (acc * (2.0 * jnp.maximum(f, 0.0)),), extras=((sv["f"], "tile"),))
    (g_ff2,) = _mm(sv["act"], dm2, ta=True, name="ff2_dw" + tag, out_dtypes=(BF16,))
    (g_ff1,) = _mm(sv["h2"], df, ta=True, out_stack=N_CHIPS, name="ff1_dw" + tag, out_dtypes=(BF16,))
    sc2 = sc2 + send_grads("ff2", dict(w_ff2=g_ff2)) + send_grads("ff1", dict(w_ff1=g_ff1))
    (dh2,) = _mm(df, w["w_ff1"], tb=True, b_stack=True, name="ff1_dx" + tag)
    dx1, dln2, dsc2, dsh2, dmo, dg1 = _norm_mod_bwd(dh2, sv["x1"], sv["r2"], small["ln2_g"], sc2, dx2,
                                                    gate=(sv["mo"], g1), name="norm2_bwd" + tag)
    dya, dyb, dga, dgb = _mm(dmo, w["w_out"], tb=True, name="mix_out_dx" + tag, out_dtypes=(BF16,) * 4, tm=512,
                             epilogue=_merge_grads,
                             extras=((sv["z"], ("tile", ZC_GA // D_MODEL)), (sv["z"], ("tile", ZC_GB // D_MODEL)),
                                     (sv["ya"], "tile"), (sv["yb"], "tile")))
    (gw["w_out"],) = _mm(sv["merged"], dmo, ta=True, name="mix_out_dw" + tag, out_dtypes=(BF16,))
    (gw["p_pool"],) = _mm(sv["yp"], dya, ta=True, name="pool_out_dw" + tag, out_dtypes=(BF16,))
    (dyp,) = _mm(dya, w["p_pool"], tb=True, name="pool_out_dx" + tag)
    du, g_w_pool, g_pool_scale = _pool_bwd(dyp, sv["p"], small["w_pool"], small["pool_scale"], name="pool_bwd" + tag)
    (gw["p_attn"],) = _mm(sv["o"], dyb, ta=True, name="attn_out_dw" + tag, out_dtypes=(BF16,))
    (do,) = _mm(dyb, w["p_attn"], tb=True, name="attn_out_dx" + tag, out_dtypes=(BF16,))
    dql, dkb, dv, dk_sums = _attn_bwd(sv["q"], sv["k"], sv["v"], do, sv["o"], sv["lse"], cos_t, sin_t,
                                      name="attn_bwd" + tag)
    dkr = _key_bwd(dk_sums, cos_t, sin_t, name="key_bwd" + tag)
    (gw["w_uq"],) = _mm(sv["cq"], dql, ta=True, name="q_proj_dw" + tag, out_dtypes=(BF16,))
    (gw["w_uk"],) = _mm(sv["ckv"], dkb, ta=True, name="k_proj_dw" + tag, out_dtypes=(BF16,))
    (gw["w_uv"],) = _mm(sv["ckv"], dv, ta=True, name="v_proj_dw" + tag, out_dtypes=(BF16,))
    (dcq,) = _mm(dql, w["w_uq"], tb=True, name="q_proj_dx" + tag)
    (dckv,) = _mm(dkb, w["w_uk"], tb=True, second=(dv, w["w_uv"]), name="kv_proj_dx" + tag)
    q_norm_g = small["q_norm_g"] + send_grads("mix", gw)
    dcq_raw, g_qn = _rms_bwd(dcq, sv["z"], ZC_CQ, sv["rq"], q_norm_g, name="q_norm_bwd" + tag)
    dckv_raw, g_kvn = _rms_bwd(dckv, sv["z"], ZC_CKV, sv["rkv"], small["kv_norm_g"], name="kv_norm_bwd" + tag)
    dz = dict(cq=dcq_raw, kr=dkr, u=du, ga=dga, gb=dgb, ckv=dckv_raw)
    g_in = {n: _mm(piece, sv["h"], ta=True, name=f"in_proj_dw_{n}" + tag, out_dtypes=(BF16,))[0]
            for n, piece in dz.items()}
    sc1 = sc1 + send_grads("in", g_in)
    dh = _mm_sum(list(dz.values()), w["w_in"], [Z_OFFSETS[n] for n in dz], name="in_proj_dx" + tag)
    dx, dln1, dsc1, dsh1, *below = _norm_mod_bwd(dh, sv["x"], sv["r1"], small["ln1_g"], sc1, dx1, gate=gate_below,
                                                 name="norm1_bwd" + tag)
    dmod = jnp.concatenate([dsh1, dsc1, dg1, dsh2, dsc2, dg2], axis=0)
    gsmall = dict(ln1_g=dln1, ln2_g=dln2, q_norm_g=g_qn, kv_norm_g=g_kvn, w_pool=g_w_pool, pool_scale=g_pool_scale)
    return dx, dmod, gsmall, below


SMALL_LOSS = 6
SMALL_SINGLES = 16
SMALL_POOL = 24
SMALL_POOL_ROWS = len(POOL_WINDOWS) * POOL_GROUP * POOL_GROUP // D_MODEL
SMALL_ROWS = SMALL_POOL + DEPTH * SMALL_POOL_ROWS


def _pack_small(parts, *, name):
    def body(*refs):
        out_ref = refs[-1]
        out_ref[...] = jnp.zeros_like(out_ref)
        for ref, (_, row) in zip(refs[:-1], parts):
            out_ref[row:row + ref.shape[0], :] = ref[...]

    return pl.pallas_call(body, name=name, out_shape=jax.ShapeDtypeStruct((SMALL_ROWS, D_MODEL), F32),
                          compiler_params=_params())(*[a for a, _ in parts])


def kernel(x, c, positions, ln1_g, ln2_g, w_ada, b_ada, w_in, q_norm_g, w_uq, kv_norm_g, w_uk, w_uv, w_pool, pool_scale, p_pool, p_attn, w_out, w_ff1, w_ff2, final_g, loss_target, m_ln1_g, m_ln2_g, m_w_ada, m_b_ada, m_w_in, m_q_norm_g, m_w_uq, m_kv_norm_g, m_w_uk, m_w_uv, m_w_pool, m_pool_scale, m_p_pool, m_p_attn, m_w_out, m_w_ff1, m_w_ff2, m_final_g, v_ln1_g, v_ln2_g, v_w_ada, v_b_ada, v_w_in, v_q_norm_g, v_w_uq, v_kv_norm_g, v_w_uk, v_w_uv, v_w_pool, v_pool_scale, v_p_pool, v_p_attn, v_w_out, v_w_ff1, v_w_ff2, v_final_g):
    weights = dict(ln1_g=ln1_g, ln2_g=ln2_g, w_ada=w_ada, b_ada=b_ada, w_in=w_in, q_norm_g=q_norm_g, w_uq=w_uq,
                   kv_norm_g=kv_norm_g, w_uk=w_uk, w_uv=w_uv, w_pool=w_pool, pool_scale=pool_scale, p_pool=p_pool,
                   p_attn=p_attn, w_out=w_out, w_ff1=w_ff1, w_ff2=w_ff2, final_g=final_g)
    moms = dict(ln1_g=m_ln1_g, ln2_g=m_ln2_g, w_ada=m_w_ada, b_ada=m_b_ada, w_in=m_w_in, q_norm_g=m_q_norm_g,
                w_uq=m_w_uq, kv_norm_g=m_kv_norm_g, w_uk=m_w_uk, w_uv=m_w_uv, w_pool=m_w_pool,
                pool_scale=m_pool_scale, p_pool=m_p_pool, p_attn=m_p_attn, w_out=m_w_out, w_ff1=m_w_ff1,
                w_ff2=m_w_ff2, final_g=m_final_g)
    vels = dict(ln1_g=v_ln1_g, ln2_g=v_ln2_g, w_ada=v_w_ada, b_ada=v_b_ada, w_in=v_w_in, q_norm_g=v_q_norm_g,
                w_uq=v_w_uq, kv_norm_g=v_kv_norm_g, w_uk=v_w_uk, w_uv=v_w_uv, w_pool=v_w_pool,
                pool_scale=v_pool_scale, p_pool=v_p_pool, p_attn=v_p_attn, w_out=v_w_out, w_ff1=v_w_ff1,
                w_ff2=v_w_ff2, final_g=v_final_g)
    order = list(weights)
    for table in (weights, moms, vels):
        table["w_in"] = jnp.swapaxes(table["w_in"], 1, 2)
    seq = x.shape[1]
    my_chip = 2 * lax.axis_index("x") + lax.axis_index("y")
    my_dev = 2 * my_chip + lax.axis_index("c")
    ada_cols = w_ada.shape[2]

    small = [dict(ln1_g=ln1_g[l:l + 1], ln2_g=ln2_g[l:l + 1], q_norm_g=q_norm_g[l:l + 1], kv_norm_g=kv_norm_g[l:l + 1],
                  w_pool=w_pool[l], pool_scale=pool_scale[l:l + 1]) for l in range(DEPTH)]

    c_all = _all_gather_small(jnp.pad(c, ((0, 7), (0, 0))), name="cond_all_gather")
    c_act = _silu(c_all, name="cond_silu")
    b_mine = lax.dynamic_slice_in_dim(b_ada, my_chip * ada_cols, ada_cols, axis=1).reshape(1, DEPTH * ada_cols)
    (mod_cat,) = _mm(c_act, w_ada, b_stack=True, name="ada_fwd", epilogue=lambda acc, b: (acc + b,),
                     extras=((b_mine, "row"),))
    mod_mine = jnp.concatenate([mod_cat[::8, l * ada_cols:(l + 1) * ada_cols] for l in range(DEPTH)], axis=0)
    mod_all = _all_gather_small(mod_mine, name="mod_all_gather").reshape(N_DEV, DEPTH, N_DEV, ada_cols)

    zero = mod_all[0, 0, 0, 0] * 0.0
    keys = [(l, group) for l in range(DEPTH) for group in GROUPS]
    exchanges, token = _exchange_start([_local_shard(weights, l, group, zero) for l, group in keys],
                                       name="weights_send", scatter=False)
    started = dict(zip(keys, exchanges))
    pin = token[0:1, 0:1]

    def gathered_weights(l, group, after):
        mine, land = _exchange_wait(started[l, group], after, name=f"weights_wait_l{l}_{group}", scatter=False)
        land = lax.dynamic_update_slice_in_dim(land, mine[None], my_chip, axis=0)
        return _unpack_weights(land, group)

    mods = []
    for l in range(DEPTH):
        row = jnp.concatenate([lax.dynamic_index_in_dim(mod_all[2 * j, l], my_dev, axis=0, keepdims=True)
                               for j in range(N_CHIPS)], axis=1) + pin
        mods.append([row[:, i * D_MODEL:(i + 1) * D_MODEL] for i in range(N_MOD)])

    inv_freq = ROPE_THETA ** (-jnp.arange(0, QK_ROPE, 2, dtype=F32) / QK_ROPE)
    freq_lanes = jnp.concatenate([jnp.zeros((QK_NOPE,), F32), inv_freq, inv_freq,
                                  jnp.zeros((HEAD_PAD - QK_DIM,), F32)]).reshape(1, LANES)
    cos_t, sin_t = _rope_tables(positions.reshape(seq, 1), freq_lanes, name="rope_tables")

    xs, saved, wl = x.reshape(seq, D_MODEL), [], []
    for l in range(DEPTH):
        xs, sv, w_l = _layer_fwd(l, xs, mods[l], functools.partial(gathered_weights, l), small[l], cos_t, sin_t)
        saved.append(sv)
        wl.append(w_l)
    dx, loss_part, g_final, dm2, dg2 = _final_loss(xs, final_g.reshape(1, D_MODEL), loss_target.reshape(seq, D_MODEL),
                                                   saved[-1]["m2"], mods[-1][5], name="final_loss")

    sent = []

    def send_grads(l, group, g):
        gpack = _pack_grads(g, group)
        (started_g,), token_g = _exchange_start([gpack], name=f"grads_send_l{l}_{group}", scatter=True)
        sent.append((l, group, started_g))
        return token_g[0:1, 0:1]

    dmod, gsmall = [None] * DEPTH, [None] * DEPTH
    for l in reversed(range(DEPTH)):
        gate_below = (saved[l - 1]["m2"], mods[l - 1][5]) if l > 0 else None
        dx, dmod[l], gsmall[l], below = _layer_bwd(l, dx, dm2, dg2, saved[l], mods[l], wl[l], small[l], cos_t, sin_t,
                                                   functools.partial(send_grads, l), gate_below)
        dm2, dg2 = below if below else (None, None)
    grads = dict(x=dx.reshape(1, seq, D_MODEL))

    def lanes(a):
        flat = a.reshape(1, -1)
        return jnp.pad(flat, ((0, 0), (0, D_MODEL - flat.shape[1])))

    singles = [gsmall[0]["ln1_g"], gsmall[1]["ln1_g"], gsmall[0]["ln2_g"], gsmall[1]["ln2_g"], g_final,
               lanes(jnp.concatenate([gsmall[l]["pool_scale"] for l in range(DEPTH)], axis=1)),
               lanes(jnp.concatenate([gsmall[l]["q_norm_g"] for l in range(DEPTH)], axis=1)),
               lanes(jnp.concatenate([gsmall[l]["kv_norm_g"] for l in range(DEPTH)], axis=1))]
    parts = [(dmod[0], 0), (lanes(loss_part), SMALL_LOSS), (dmod[1], 8)]
    parts += [(a, SMALL_SINGLES + i) for i, a in enumerate(singles)]
    parts += [(gsmall[l]["w_pool"].reshape(-1, D_MODEL), SMALL_POOL + l * SMALL_POOL_ROWS) for l in range(DEPTH)]
    small_all = _all_gather_small(_pack_small(parts, name="small_grads_pack"), name="small_grads_all_gather")
    small_all = small_all.reshape(N_DEV, SMALL_ROWS, D_MODEL)
    ssum = _sum_slots(small_all, N_DEV, name="small_grads_sum")
    loss = ssum[SMALL_LOSS, 0]
    grads["b_ada"] = jnp.stack([ssum[8 * l:8 * l + N_MOD] for l in range(DEPTH)]).reshape(DEPTH, N_MOD * D_MODEL)
    grads["ln1_g"] = ssum[SMALL_SINGLES:SMALL_SINGLES + 2]
    grads["ln2_g"] = ssum[SMALL_SINGLES + 2:SMALL_SINGLES + 4]
    grads["final_g"] = ssum[SMALL_SINGLES + 4]
    grads["pool_scale"] = ssum[SMALL_SINGLES + 5].reshape(DEPTH, POOL_DIM)
    grads["q_norm_g"] = ssum[SMALL_SINGLES + 6, :DEPTH * Q_LORA].reshape(DEPTH, Q_LORA)
    grads["kv_norm_g"] = ssum[SMALL_SINGLES + 7, :DEPTH * KV_LORA].reshape(DEPTH, KV_LORA)
    grads["w_pool"] = ssum[SMALL_POOL:SMALL_ROWS].reshape(w_pool.shape)

    parts, after = [], ssum
    for l, group, started_g in sent:
        tg = f"_l{l}_{group}"
        gpack, land = _exchange_wait(started_g, after, name="grads_wait" + tg, scatter=True)
        own = lax.dynamic_index_in_dim(gpack, my_chip, axis=0, keepdims=True)
        land = lax.dynamic_update_slice_in_dim(land, own, my_chip, axis=0)
        parts.append(_sum_slots(land, N_CHIPS, name="grads_sum_chips" + tg))
        after = parts[-1]
    gsum = {}
    for (l, group, _), part, other in zip(sent, parts, _exchange_sibling(parts, name="grads_swap_cores")):
        gsum[group] = _add2_stacked(part, other, gsum.get(group), l, name=f"grads_sum_cores_l{l}_{group}")
    grads.update(w_in=gsum["in"], w_ff1=gsum["ff1"], w_ff2=gsum["ff2"])
    off = 0
    for name in MIX_NAMES:
        grads[name] = gsum["mix"][:, off:off + ROWS_OF[name]].reshape(weights[name].shape)
        off += ROWS_OF[name]

    c_act_t = jnp.pad(c_act[::8].T, ((0, 0), (0, LANES - N_DEV)))
    d_mine = []
    for l in range(DEPTH):
        d_all = small_all[:, 8 * l:8 * l + N_MOD].reshape(N_DEV, N_MOD * D_MODEL)
        d_mine.append(lax.dynamic_slice_in_dim(d_all, my_chip * ada_cols, ada_cols, axis=1))
    d_cat = jnp.pad(jnp.concatenate(d_mine, axis=1), ((0, LANES - N_DEV), (0, 0)))
    (grads["w_ada"],) = _mm(c_act_t, d_cat, out_stack=DEPTH, name="ada_dw")

    def view(a):
        return a.reshape(1, -1) if a.ndim == 1 else a if a.ndim == 3 else a.reshape(-1, a.shape[-1])

    delta, new_m, new_v = {}, {}, {}
    for name in order:
        shape = weights[name].shape
        d, nm, nv = _adamw(view(weights[name]), view(grads[name]), view(moms[name]), view(vels[name]),
                           name="adamw_" + name)
        delta[name], new_m[name], new_v[name] = d.reshape(shape), nm.reshape(shape), nv.reshape(shape)
    for table in (grads, delta, new_m, new_v):
        table["w_in"] = jnp.swapaxes(table["w_in"], 1, 2)
    return (loss, grads["x"], *[grads[n] for n in order], *[delta[n] for n in order],
            *[new_m[n] for n in order], *[new_v[n] for n in order])
```

```python
import functools
import math

import jax
import jax.numpy as jnp
from jax import lax
from jax.experimental import pallas as pl
from jax.experimental.pallas import tpu as pltpu

F32 = jnp.float32
BF16 = jnp.bfloat16
MESH = pl.DeviceIdType.MESH

D_MODEL = 1024
DEPTH = 2
POOL_WINDOWS = (2, 4, 8, 16)
POOL_GROUP = 128
POOL_DIM = 512
N_HEADS = 8
QK_NOPE = 64
QK_ROPE = 32
QK_DIM = QK_NOPE + QK_ROPE
V_DIM = 64
HEAD_PAD = 128
Q_LORA = 384
KV_LORA = 256
ROPE_THETA = 10000.0
ATTN_DIM = N_HEADS * V_DIM
D_FF = 4 * D_MODEL
N_MOD = 6
EPS = 1e-6
N_CHIPS = 4
N_DEV = 8

ADAM_LR = 0.001
ADAM_B1 = 0.9
ADAM_B2 = 0.999
ADAM_EPS = 1e-08
ADAM_WD = 0.01
ADAM_STEP = 10

VMEM_LIMIT_BYTES = 56 * 1024 * 1024
LANES = 128
HALO = 16

ZC_CQ = 0
ZC_KR = 384
ZC_U = 512
ZC_GA = 1024
ZC_GB = 2048
ZC_CKV = 3072
Z_DIM = 3328
Z_OFFSETS = dict(cq=ZC_CQ, kr=ZC_KR, u=ZC_U, ga=ZC_GA, gb=ZC_GB, ckv=ZC_CKV)

W_IN_U, W_IN_CQ, W_IN_CKV, W_IN_KR, W_IN_GATES = (0, 512), (512, 896), (896, 1152), (1152, 1184), (1184, 3232)
W_IN_COLS = W_IN_GATES[1]
W_IN_SHARD = W_IN_COLS // N_CHIPS

ROWS_OF = dict(w_uq=72, w_uk=32, w_uv=32, p_pool=128, p_attn=128, w_out=256)


def _params(sem=None, **kw):
    return pltpu.CompilerParams(dimension_semantics=sem, vmem_limit_bytes=VMEM_LIMIT_BYTES, **kw)


def _tile(n, target, unit=LANES):
    best = None
    for t in range(unit, min(n, target) + 1, unit):
        if n % t == 0:
            best = t
    return best if best is not None and 4 * best >= min(n, target) else n


def _near_tile(n, target):
    cands = [t for t in range(LANES, n + 1, LANES) if n % t == 0]
    return min(cands, key=lambda t: abs(math.log(t / target))) if cands else n


def _mm(a, b, *, name, ta=False, tb=False, out_dtypes=(F32,), epilogue=None, extras=(), tm=1024, tn=1024, tk=1024,
        second=None, b_stack=False, out_stack=None):
    (k_dim, m_dim) = a.shape if ta else a.shape[::-1]
    if b_stack:
        g_b, k_b, n_shard = b.shape
        n_dim, k_b = (k_b, g_b * n_shard) if tb else (g_b * n_shard, k_b)
    else:
        (n_dim, k_b) = b.shape if tb else b.shape[::-1]
    assert k_dim == k_b, (a.shape, b.shape)
    n_unit = n_shard if b_stack and not tb else n_dim // out_stack if out_stack else n_dim
    k_unit = n_shard if b_stack and tb else k_dim
    tm, tn, tk = _near_tile(m_dim, tm), _near_tile(n_unit, tn), _near_tile(k_unit, tk)
    nk = k_dim // tk
    n_extra, n_out = len(extras), len(out_dtypes)
    n_lhs = 4 if second else 2
    dims = (((0 if ta else 1,), (1 if tb else 0,)), ((), ()))
    if epilogue is None:
        epilogue = lambda acc: (acc,) * n_out

    def body(*refs):
        operand_refs, rest = refs[:n_lhs], refs[n_lhs:]
        extra_refs, out_refs = rest[:n_extra], rest[n_extra:n_extra + n_out]

        def product():
            total = None
            for a_ref, b_ref in zip(operand_refs[0::2], operand_refs[1::2]):
                part = lax.dot_general(a_ref[...].astype(BF16), b_ref[...].astype(BF16), dims, preferred_element_type=F32)
                total = part if total is None else total + part
            return total

        def finish(acc):
            outs = epilogue(acc, *[r[...] for r in extra_refs])
            for o_ref, o in zip(out_refs, outs):
                o_ref[...] = o.astype(o_ref.dtype)

        if nk == 1:
            finish(product())
            return
        acc_ref = rest[-1]
        k = pl.program_id(2)

        @pl.when(k == 0)
        def _():
            acc_ref[...] = product()

        @pl.when((k > 0) & (k < nk - 1))
        def _():
            acc_ref[...] += product()

        @pl.when(k == nk - 1)
        def _():
            finish(acc_ref[...] + product())

    a_spec = pl.BlockSpec((tk, tm), lambda i, j, k: (k, i)) if ta else pl.BlockSpec((tm, tk), lambda i, j, k: (i, k))
    if b_stack and tb:
        per = n_shard // tk
        b_spec = pl.BlockSpec((None, tn, tk), lambda i, j, k: (k // per, j, k % per))
    elif b_stack:
        per = n_shard // tn
        b_spec = pl.BlockSpec((None, tk, tn), lambda i, j, k: (j // per, k, j % per))
    elif tb:
        b_spec = pl.BlockSpec((tn, tk), lambda i, j, k: (j, k))
    else:
        b_spec = pl.BlockSpec((tk, tn), lambda i, j, k: (k, j))
    if out_stack:
        per_out = (n_dim // out_stack) // tn
        out_spec = pl.BlockSpec((None, tm, tn), lambda i, j, k: (j // per_out, i, j % per_out))
        out_dims = (out_stack, m_dim, n_dim // out_stack)
    else:
        out_spec = pl.BlockSpec((tm, tn), lambda i, j, k: (i, j))
        out_dims = (m_dim, n_dim)
    extra_specs = []
    for arr, kind in extras:
        if kind == "tile":
            extra_specs.append(pl.BlockSpec((tm, tn), lambda i, j, k: (i, j)))
        elif isinstance(kind, tuple):
            extra_specs.append(pl.BlockSpec((tm, tn), functools.partial(lambda i, j, k, c: (i, j + c), c=kind[1])))
        elif kind == "row":
            extra_specs.append(pl.BlockSpec((1, tn), lambda i, j, k: (0, j)))
        elif kind == "col":
            extra_specs.append(pl.BlockSpec((tm, 1), lambda i, j, k: (i, 0)))
        else:
            assert kind == "table", kind
            extra_specs.append(pl.BlockSpec((tm, LANES), lambda i, j, k: (i, 0)))
    return pl.pallas_call(
        body,
        name=name,
        grid=(m_dim // tm, n_dim // tn, nk),
        in_specs=[a_spec, b_spec] * (n_lhs // 2) + extra_specs,
        out_specs=[out_spec for _ in out_dtypes],
        out_shape=[jax.ShapeDtypeStruct(out_dims, dt) for dt in out_dtypes],
        scratch_shapes=[pltpu.VMEM((tm, tn), F32)] if nk > 1 else [],
        compiler_params=_params(("parallel", "parallel", "arbitrary")),
    )(a, b, *(second or ()), *[arr for arr, _ in extras])


def _mm_sum(pieces, b, offsets, *, name, tm=1024, tn=1024):
    m_dim, n_dim = pieces[0].shape[0], b.shape[1]
    tm, tn = _near_tile(m_dim, tm), _near_tile(n_dim, tn)
    n_pieces = len(pieces)

    def body(*refs):
        total = None
        for a_ref, b_ref in zip(refs[:n_pieces], refs[n_pieces:2 * n_pieces]):
            part = jnp.dot(a_ref[...], b_ref[...], preferred_element_type=F32)
            total = part if total is None else total + part
        refs[-1][...] = total

    a_specs = [pl.BlockSpec((tm, p.shape[1]), lambda i, j: (i, 0)) for p in pieces]
    b_specs = [pl.BlockSpec((p.shape[1], tn), functools.partial(lambda i, j, blk: (blk, j), blk=off // p.shape[1]))
               for p, off in zip(pieces, offsets)]
    return pl.pallas_call(
        body, name=name, grid=(m_dim // tm, n_dim // tn),
        in_specs=a_specs + b_specs,
        out_specs=pl.BlockSpec((tm, tn), lambda i, j: (i, j)),
        out_shape=jax.ShapeDtypeStruct((m_dim, n_dim), F32),
        compiler_params=_params(("parallel", "parallel")),
    )(*pieces, *[b] * n_pieces)


def _rows(s):
    return min(512, s)


def _rope_tables(pos_col, inv_freq_lanes, *, name):
    s = pos_col.shape[0]
    tb = _rows(s)

    def body(pos_ref, f_ref, cos_ref, sin_ref):
        ang = pos_ref[...].astype(F32) * f_ref[...]
        lane = lax.broadcasted_iota(jnp.int32, ang.shape, 1)
        on = (lane >= QK_NOPE) & (lane < QK_DIM)
        cos_ref[...] = jnp.where(on, jnp.cos(ang), 0.0)
        sin_ref[...] = jnp.where(on, jnp.sin(ang), 0.0)

    return pl.pallas_call(
        body, name=name, grid=(s // tb,),
        in_specs=[pl.BlockSpec((tb, 1), lambda i: (i, 0)), pl.BlockSpec((1, LANES), lambda i: (0, 0))],
        out_specs=[pl.BlockSpec((tb, LANES), lambda i: (i, 0))] * 2,
        out_shape=[jax.ShapeDtypeStruct((s, LANES), F32)] * 2,
        compiler_params=_params(("parallel",)),
    )(pos_col, inv_freq_lanes)


def _rotate_half(x):
    lane = lax.broadcasted_iota(jnp.int32, x.shape, 1)
    half = QK_ROPE // 2
    first = (lane >= QK_NOPE) & (lane < QK_NOPE + half)
    second = (lane >= QK_NOPE + half) & (lane < QK_DIM)
    return jnp.where(first, -pltpu.roll(x, LANES - half, 1), jnp.where(second, pltpu.roll(x, half, 1), 0.0))


def _norm_mod(x, g, sc, sh, *, name):
    s, d = x.shape
    tb = _rows(s)

    def body(x_ref, g_ref, sc_ref, sh_ref, h_ref, r_ref):
        xv = x_ref[...]
        r = lax.rsqrt(jnp.mean(xv * xv, axis=-1, keepdims=True) + EPS)
        r_ref[...] = r
        h_ref[...] = (((xv * r) * g_ref[...]) * (1.0 + sc_ref[...]) + sh_ref[...]).astype(BF16)

    vec = pl.BlockSpec((1, d), lambda i: (0, 0))
    return pl.pallas_call(
        body, name=name, grid=(s // tb,),
        in_specs=[pl.BlockSpec((tb, d), lambda i: (i, 0)), vec, vec, vec],
        out_specs=[pl.BlockSpec((tb, d), lambda i: (i, 0)), pl.BlockSpec((tb, 1), lambda i: (i, 0))],
        out_shape=[jax.ShapeDtypeStruct((s, d), BF16), jax.ShapeDtypeStruct((s, 1), F32)],
        compiler_params=_params(("parallel",)),
    )(x, g, sc, sh)


def _window_sums(ext, sign):
    n = ext.shape[0]
    sums, cur, k = [], ext, 1
    for _ in POOL_WINDOWS:
        cur = cur + pltpu.roll(cur, k if sign > 0 else n - k, 0)
        sums.append(cur)
        k *= 2
    return sums


def _mixer_pre(z, cos_t, sin_t, w_pool, pool_scale, gq, gkv, *, name):
    s = z.shape[0]
    tb = _rows(s)
    hb = tb // HALO

    def body(zcq_ref, zkr_ref, zu_ref, zuh_ref, zckv_ref, cos_ref, sin_ref, wp_ref, ps_ref, gq_ref, gkv_ref,
             p_ref, yp_ref, cq_ref, ckv_ref, kr_ref, rq_ref, rkv_ref):
        i = pl.program_id(0)
        u = zu_ref[...].astype(F32)
        halo = jnp.where(i > 0, zuh_ref[...].astype(F32), 0.0)
        ext = jnp.concatenate([halo, u], axis=0)
        t = i * tb + lax.broadcasted_iota(jnp.int32, (tb, 1), 0)
        for g, (w, sw) in enumerate(zip(POOL_WINDOWS, _window_sums(ext, +1))):
            cols = slice(g * POOL_GROUP, (g + 1) * POOL_GROUP)
            cnt = jnp.minimum(t + 1, w).astype(F32)
            pg = (sw[HALO:, cols] / cnt - u[:, cols]).astype(BF16)
            p_ref[:, cols] = pg
            yg = jnp.dot(pg, wp_ref[g].astype(BF16), preferred_element_type=F32)
            yp_ref[:, cols] = (yg * ps_ref[:, cols]).astype(BF16)

        def rms(x_ref, g_ref, out_ref, r_ref):
            xv = x_ref[...].astype(F32)
            r = lax.rsqrt(jnp.mean(xv * xv, axis=-1, keepdims=True) + EPS)
            r_ref[...] = r
            out_ref[...] = ((xv * r) * g_ref[...]).astype(BF16)

        rms(zcq_ref, gq_ref, cq_ref, rq_ref)
        rms(zckv_ref, gkv_ref, ckv_ref, rkv_ref)
        kr = zkr_ref[...].astype(F32)
        kr_ref[...] = (kr * cos_ref[...] + _rotate_half(kr) * sin_ref[...]).astype(BF16)

    def zcol(width, off):
        return pl.BlockSpec((tb, width), lambda i: (i, off // width))

    def full(a):
        return pl.BlockSpec(a.shape, lambda i: (0,) * a.ndim)

    def out(width, dt):
        return pl.BlockSpec((tb, width), lambda i: (i, 0)), jax.ShapeDtypeStruct((s, width), dt)

    outs = [out(POOL_DIM, BF16), out(POOL_DIM, BF16), out(Q_LORA, BF16), out(KV_LORA, BF16), out(LANES, BF16),
            out(1, F32), out(1, F32)]
    return pl.pallas_call(
        body, name=name, grid=(s // tb,),
        in_specs=[zcol(Q_LORA, ZC_CQ), zcol(LANES, ZC_KR), zcol(POOL_DIM, ZC_U),
                  pl.BlockSpec((HALO, POOL_DIM), lambda i: (jnp.maximum(i * hb - 1, 0), ZC_U // POOL_DIM)),
                  zcol(KV_LORA, ZC_CKV),
                  pl.BlockSpec((tb, LANES), lambda i: (i, 0)), pl.BlockSpec((tb, LANES), lambda i: (i, 0)),
                  full(w_pool), full(pool_scale), full(gq), full(gkv)],
        out_specs=[o[0] for o in outs], out_shape=[o[1] for o in outs],
        compiler_params=_params(("parallel",)),
    )(z, z, z, z, z, cos_t, sin_t, w_pool, pool_scale, gq, gkv)


def _sigmoid(x):
    return 1.0 / (1.0 + jnp.exp(-x.astype(F32)))


ATTN_SCALE = 1.0 / math.sqrt(QK_DIM)
NEG_BIG = -1e30


LOG2_E = math.log2(math.e)
EXP2_SCALE = ATTN_SCALE * LOG2_E
NT_DIMS = (((1,), (1,)), ((), ()))
TN_DIMS = (((0,), (0,)), ((), ()))


def _on_or_below_diagonal(t):
    return lax.broadcasted_iota(jnp.int32, (t, t), 0) >= lax.broadcasted_iota(jnp.int32, (t, t), 1)


HEADS_PER_STEP = 2
HEAD_COLS = [slice(g * HEAD_PAD, (g + 1) * HEAD_PAD) for g in range(HEADS_PER_STEP)]


def _attn_fwd(q, k, v, *, name):
    s = q.shape[0]
    t = _rows(s)
    wide = HEADS_PER_STEP * HEAD_PAD

    def body(q_ref, k_ref, v_ref, o_ref, lse_ref):
        qi = pl.program_id(1)
        qs = [q_ref[:, cols] for cols in HEAD_COLS]

        def block(j, carry, diagonal):
            rows = pl.ds(pl.multiple_of(j * t, t), t)
            out = []
            for qv, cols, (m, l, acc) in zip(qs, HEAD_COLS, carry):
                sc = lax.dot_general(qv, k_ref[rows, cols], NT_DIMS, preferred_element_type=F32)
                if diagonal:
                    sc = jnp.where(_on_or_below_diagonal(t), sc, NEG_BIG)
                m_new = jnp.maximum(m, jnp.max(sc, axis=-1, keepdims=True))
                p = jnp.exp2((sc - m_new) * EXP2_SCALE)
                alpha = jnp.exp2((m - m_new) * EXP2_SCALE)
                l = alpha * l + jnp.sum(p, axis=-1, keepdims=True)
                acc = alpha * acc + jnp.dot(p.astype(BF16), v_ref[rows, cols], preferred_element_type=F32)
                out.append((m_new, l, acc))
            return tuple(out)

        init = tuple((jnp.full((t, 1), -jnp.inf, F32), jnp.zeros((t, 1), F32), jnp.zeros((t, HEAD_PAD), F32))
                     for _ in HEAD_COLS)
        carry = lax.fori_loop(0, qi, lambda j, c: block(j, c, False), init)
        for g, (cols, (m, l, acc)) in enumerate(zip(HEAD_COLS, block(qi, carry, True))):
            o_ref[:, cols] = (acc / l).astype(BF16)
            lse_ref[g] = m * ATTN_SCALE + jnp.log(l)

    q_spec = pl.BlockSpec((t, wide), lambda h, i: (i, h))
    kv_spec = pl.BlockSpec((s, wide), lambda h, i: (0, h))
    return pl.pallas_call(
        body, name=name, grid=(N_HEADS // HEADS_PER_STEP, s // t),
        in_specs=[q_spec, kv_spec, kv_spec],
        out_specs=[q_spec, pl.BlockSpec((HEADS_PER_STEP, t, 1), lambda h, i: (h, i, 0))],
        out_shape=[jax.ShapeDtypeStruct((s, N_HEADS * HEAD_PAD), BF16), jax.ShapeDtypeStruct((N_HEADS, s, 1), F32)],
        compiler_params=_params(("parallel", "parallel")),
    )(q, k, v)


def _attn_bwd(q, k, v, do, o, lse, cos_t, sin_t, *, name):
    s = q.shape[0]
    t = _rows(s)
    nt = s // t

    def body(q_ref, k_ref, v_ref, do_ref, o_ref, lse_ref, cos_ref, sin_ref, dql_ref, dk_ref, dv_ref, dks_ref,
             dq_ref, dl_ref):
        kj = pl.program_id(1)

        @pl.when(kj == 0)
        def _():
            dq_ref[...] = jnp.zeros_like(dq_ref)

            def delta(i, carry):
                rows = pl.ds(pl.multiple_of(i * t, t), t)
                for g, cols in enumerate(HEAD_COLS):
                    dl_ref[g, rows, :] = jnp.sum(do_ref[rows, cols].astype(F32) * o_ref[rows, cols].astype(F32),
                                                 axis=-1, keepdims=True)
                return carry

            lax.fori_loop(0, nt, delta, 0)

        kvs = [(k_ref[:, cols], v_ref[:, cols]) for cols in HEAD_COLS]

        def block(i, carry, diagonal):
            rows = pl.ds(pl.multiple_of(i * t, t), t)
            out = []
            for g, (cols, (kv, vv), (dk, dv)) in enumerate(zip(HEAD_COLS, kvs, carry)):
                qv, dov = q_ref[rows, cols], do_ref[rows, cols]
                sc = lax.dot_general(qv, kv, NT_DIMS, preferred_element_type=F32)
                p = jnp.exp2(sc * EXP2_SCALE - lse_ref[g, rows, :] * LOG2_E)
                if diagonal:
                    p = jnp.where(_on_or_below_diagonal(t), p, 0.0)
                dp = lax.dot_general(dov, vv, NT_DIMS, preferred_element_type=F32)
                ds = (p * (dp - dl_ref[g, rows, :])).astype(BF16)
                dv = dv + lax.dot_general(p.astype(BF16), dov, TN_DIMS, preferred_element_type=F32)
                dk = dk + lax.dot_general(ds, qv, TN_DIMS, preferred_element_type=F32)
                dq_ref[rows, cols] += jnp.dot(ds, kv, preferred_element_type=F32) * ATTN_SCALE
                out.append((dk, dv))
            return tuple(out)

        zero = jnp.zeros((t, HEAD_PAD), F32)
        carry = block(kj, tuple((zero, zero) for _ in HEAD_COLS), True)
        dk_sum = None
        for cols, (dk, dv) in zip(HEAD_COLS, lax.fori_loop(kj + 1, nt, lambda i, c: block(i, c, False), carry)):
            dk = dk * ATTN_SCALE
            dk_ref[:, cols] = dk.astype(BF16)
            dv_ref[:, cols] = dv.astype(BF16)
            dk_sum = dk if dk_sum is None else dk_sum + dk
        dks_ref[...] = dk_sum

        @pl.when(kj == nt - 1)
        def _():
            def rope_bwd(i, carry):
                rows = pl.ds(pl.multiple_of(i * t, t), t)
                sin = sin_ref[rows, :]
                lane = lax.broadcasted_iota(jnp.int32, sin.shape, 1)
                cos_q = cos_ref[rows, :] + jnp.where(lane < QK_NOPE, 1.0, 0.0)
                for cols in HEAD_COLS:
                    dqv = dq_ref[rows, cols]
                    dql_ref[rows, cols] = (dqv * cos_q - _rotate_half(dqv * sin)).astype(BF16)
                return carry

            lax.fori_loop(0, nt, rope_bwd, 0)

    heads_wide = HEADS_PER_STEP * HEAD_PAD
    full_spec = pl.BlockSpec((s, heads_wide), lambda h, j: (0, h))
    kv_spec = pl.BlockSpec((t, heads_wide), lambda h, j: (j, h))
    vec_spec = pl.BlockSpec((HEADS_PER_STEP, s, 1), lambda h, j: (h, 0, 0))
    table_spec = pl.BlockSpec((s, LANES), lambda h, j: (0, 0))
    wide = jax.ShapeDtypeStruct((s, N_HEADS * HEAD_PAD), BF16)
    n_steps = N_HEADS // HEADS_PER_STEP
    return pl.pallas_call(
        body, name=name, grid=(n_steps, nt),
        in_specs=[full_spec, kv_spec, kv_spec, full_spec, full_spec, vec_spec, table_spec, table_spec],
        out_specs=[full_spec, kv_spec, kv_spec, pl.BlockSpec((None, t, HEAD_PAD), lambda h, j: (h, j, 0))],
        out_shape=[wide, wide, wide, jax.ShapeDtypeStruct((n_steps, s, HEAD_PAD), F32)],
        scratch_shapes=[pltpu.VMEM((s, heads_wide), F32), pltpu.VMEM((HEADS_PER_STEP, s, 1), F32)],
        compiler_params=_params(("parallel", "arbitrary")),
    )(q, k, v, do, o, lse, cos_t, sin_t)


def _acc_specs(widths):
    return ([pl.BlockSpec((1, w), lambda i: (0, 0)) for w in widths],
            [jax.ShapeDtypeStruct((1, w), F32) for w in widths])


def _gate_grads(dxv, m_ref, gate_ref, dm_ref, dgate_ref):
    dm_ref[...] = (dxv * gate_ref[...]).astype(BF16)
    dgate_ref[...] += jnp.sum(dxv * m_ref[...], axis=0, keepdims=True)


def _final_loss(x, g, target, m, gate, *, name):
    s, d = x.shape
    tb = _rows(s)

    def body(x_ref, g_ref, t_ref, m_ref, gate_ref, dx_ref, loss_ref, dg_ref, dm_ref, dgate_ref):
        @pl.when(pl.program_id(0) == 0)
        def _():
            loss_ref[...] = jnp.zeros_like(loss_ref)
            dg_ref[...] = jnp.zeros_like(dg_ref)
            dgate_ref[...] = jnp.zeros_like(dgate_ref)

        xv = x_ref[...]
        r = lax.rsqrt(jnp.mean(xv * xv, axis=-1, keepdims=True) + EPS)
        xn = xv * r
        err = xn * g_ref[...] - t_ref[...]
        loss_ref[...] += 0.5 * jnp.sum(jnp.mean(err * err, axis=-1, keepdims=True), axis=0, keepdims=True)
        dy = err / d
        dg_ref[...] += jnp.sum(dy * xn, axis=0, keepdims=True)
        dxn = dy * g_ref[...]
        dxv = r * (dxn - xn * jnp.mean(dxn * xn, axis=-1, keepdims=True))
        dx_ref[...] = dxv
        _gate_grads(dxv, m_ref, gate_ref, dm_ref, dgate_ref)

    blk = pl.BlockSpec((tb, d), lambda i: (i, 0))
    vec = pl.BlockSpec((1, d), lambda i: (0, 0))
    acc_specs, acc_shapes = _acc_specs((LANES, d))
    return pl.pallas_call(
        body, name=name, grid=(s // tb,),
        in_specs=[blk, vec, blk, blk, vec],
        out_specs=[blk] + acc_specs + [blk, vec],
        out_shape=[jax.ShapeDtypeStruct((s, d), F32)] + acc_shapes + [jax.ShapeDtypeStruct((s, d), BF16),
                                                                     jax.ShapeDtypeStruct((1, d), F32)],
        compiler_params=_params(("arbitrary",)),
    )(x, g, target, m, gate)


def _norm_mod_bwd(dh, x, r, g, sc, dx_skip, *, name, gate=None):
    s, d = x.shape
    tb = _rows(s)
    nb = s // tb
    n_gate = 2 if gate else 0

    def body(dh_ref, x_ref, r_ref, g_ref, sc_ref, skip_ref, *rest):
        gate_refs, (dx_ref, dg_ref, dsc_ref, dsh_ref) = rest[:n_gate], rest[n_gate:n_gate + 4]
        gate_outs, da_sc = rest[n_gate + 4:-1], rest[-1]
        i = pl.program_id(0)

        @pl.when(i == 0)
        def _():
            da_sc[...] = jnp.zeros_like(da_sc)
            dsh_ref[...] = jnp.zeros_like(dsh_ref)
            if gate:
                gate_outs[1][...] = jnp.zeros_like(gate_outs[1])

        dhv, rv = dh_ref[...], r_ref[...]
        xn = x_ref[...] * rv
        dsh_ref[...] += jnp.sum(dhv, axis=0, keepdims=True)
        da_sc[...] += jnp.sum(dhv * xn, axis=0, keepdims=True)
        dxn = dhv * (g_ref[...] * (1.0 + sc_ref[...]))
        dxv = skip_ref[...] + rv * (dxn - xn * jnp.mean(dxn * xn, axis=-1, keepdims=True))
        dx_ref[...] = dxv
        if gate:
            _gate_grads(dxv, *gate_refs, *gate_outs)

        @pl.when(i == nb - 1)
        def _():
            dg_ref[...] = da_sc[...] * (1.0 + sc_ref[...])
            dsc_ref[...] = da_sc[...] * g_ref[...]

    blk = pl.BlockSpec((tb, d), lambda i: (i, 0))
    vec = pl.BlockSpec((1, d), lambda i: (0, 0))
    acc_specs, acc_shapes = _acc_specs((d, d, d))
    gate_specs = [blk, vec] if gate else []
    gate_shapes = [jax.ShapeDtypeStruct((s, d), BF16), jax.ShapeDtypeStruct((1, d), F32)] if gate else []
    return pl.pallas_call(
        body, name=name, grid=(nb,),
        in_specs=[blk, blk, pl.BlockSpec((tb, 1), lambda i: (i, 0)), vec, vec, blk] + gate_specs,
        out_specs=[blk] + acc_specs + gate_specs,
        out_shape=[jax.ShapeDtypeStruct((s, d), F32)] + acc_shapes + gate_shapes,
        scratch_shapes=[pltpu.VMEM((1, d), F32)],
        compiler_params=_params(("arbitrary",)),
    )(dh, x, r, g, sc, dx_skip, *(gate or ()))


def _pool_bwd(dyp, p, w_pool, pool_scale, *, name):
    s = dyp.shape[0]
    tb = _rows(s)
    nb = s // tb
    hb = tb // HALO
    nt_dims = (((1,), (1,)), ((), ()))
    tn_dims = (((0,), (0,)), ((), ()))

    def body(dy_ref, dyn_ref, p_ref, wp_ref, ps_ref, du_ref, gwp_ref, gps_ref):
        i = pl.program_id(0)

        @pl.when(i == 0)
        def _():
            gwp_ref[...] = jnp.zeros_like(gwp_ref)
            gps_ref[...] = jnp.zeros_like(gps_ref)

        cur = dy_ref[...]
        nxt = jnp.where(i < nb - 1, dyn_ref[...], 0.0)
        dpw = (jnp.concatenate([cur, nxt], axis=0) * ps_ref[...]).astype(BF16)
        t = i * tb + lax.broadcasted_iota(jnp.int32, (tb + HALO, 1), 0)
        for g, w in enumerate(POOL_WINDOWS):
            cols = slice(g * POOL_GROUP, (g + 1) * POOL_GROUP)
            wg = wp_ref[g].astype(BF16)
            dp = lax.dot_general(dpw[:, cols], wg, nt_dims, preferred_element_type=F32)
            e = dp / jnp.minimum(t + 1, w).astype(F32)
            lead = _window_sums(e, -1)[g]
            du_ref[:, cols] = (lead[:tb] - dp[:tb]).astype(BF16)
            pg = p_ref[:, cols]
            pw = jnp.dot(pg, wg, preferred_element_type=F32)
            gps_ref[:, cols] += jnp.sum(cur[:, cols] * pw, axis=0, keepdims=True)
            gwp_ref[g] += lax.dot_general(pg, dpw[:tb, cols], tn_dims, preferred_element_type=F32)

    blk = pl.BlockSpec((tb, POOL_DIM), lambda i: (i, 0))
    return pl.pallas_call(
        body, name=name, grid=(nb,),
        in_specs=[blk, pl.BlockSpec((HALO, POOL_DIM), lambda i: (jnp.minimum((i + 1) * hb, s // HALO - 1), 0)), blk,
                  pl.BlockSpec(w_pool.shape, lambda i: (0, 0, 0)), pl.BlockSpec((1, POOL_DIM), lambda i: (0, 0))],
        out_specs=[blk, pl.BlockSpec(w_pool.shape, lambda i: (0, 0, 0)), pl.BlockSpec((1, POOL_DIM), lambda i: (0, 0))],
        out_shape=[jax.ShapeDtypeStruct((s, POOL_DIM), BF16), jax.ShapeDtypeStruct(w_pool.shape, F32),
                   jax.ShapeDtypeStruct((1, POOL_DIM), F32)],
        compiler_params=_params(("arbitrary",)),
    )(dyp, dyp, p, w_pool, pool_scale)


def _key_bwd(dk_sums, cos_t, sin_t, *, name):
    n, s, _ = dk_sums.shape
    tb = _rows(s)

    def body(dk_ref, cos_ref, sin_ref, dkr_ref):
        tot = dk_ref[0]
        for h in range(1, n):
            tot = tot + dk_ref[h]
        dkr_ref[...] = (tot * cos_ref[...] - _rotate_half(tot * sin_ref[...])).astype(BF16)

    tab = pl.BlockSpec((tb, LANES), lambda i: (i, 0))
    return pl.pallas_call(
        body, name=name, grid=(s // tb,),
        in_specs=[pl.BlockSpec((n, tb, LANES), lambda i: (0, i, 0)), tab, tab], out_specs=tab,
        out_shape=jax.ShapeDtypeStruct((s, LANES), BF16),
        compiler_params=_params(("parallel",)),
    )(dk_sums, cos_t, sin_t)


def _rms_bwd(dy, z, z_off, r, g, *, name):
    s, n = dy.shape
    tb = _rows(s)

    def body(dy_ref, x_ref, r_ref, g_ref, dx_ref, dg_ref):
        @pl.when(pl.program_id(0) == 0)
        def _():
            dg_ref[...] = jnp.zeros_like(dg_ref)

        dyv, rv = dy_ref[...], r_ref[...]
        xn = x_ref[...].astype(F32) * rv
        dg_ref[...] += jnp.sum(dyv * xn, axis=0, keepdims=True)
        dxn = dyv * g_ref[...]
        dx_ref[...] = (rv * (dxn - xn * jnp.mean(dxn * xn, axis=-1, keepdims=True))).astype(BF16)

    blk = pl.BlockSpec((tb, n), lambda i: (i, 0))
    acc_specs, acc_shapes = _acc_specs((n,))
    return pl.pallas_call(
        body, name=name, grid=(s // tb,),
        in_specs=[blk, pl.BlockSpec((tb, n), lambda i: (i, z_off // n)), pl.BlockSpec((tb, 1), lambda i: (i, 0)),
                  pl.BlockSpec((1, n), lambda i: (0, 0))],
        out_specs=[blk] + acc_specs, out_shape=[jax.ShapeDtypeStruct((s, n), BF16)] + acc_shapes,
        compiler_params=_params(("arbitrary",)),
    )(dy, z, r, g)


def _silu(c, *, name):
    def body(c_ref, out_ref):
        cv = c_ref[...]
        out_ref[...] = (cv * _sigmoid(cv)).astype(BF16)

    return pl.pallas_call(body, name=name, out_shape=jax.ShapeDtypeStruct(c.shape, BF16),
                          compiler_params=_params())(c)


def _sum_slots(a, n, *, name, out_dtype=F32):
    _, rows, cols = a.shape
    tr = _tile(rows, 256, 8)

    def body(a_ref, out_ref):
        tot = a_ref[0].astype(F32)
        for j in range(1, n):
            tot = tot + a_ref[j].astype(F32)
        out_ref[...] = tot.astype(out_dtype)

    return pl.pallas_call(
        body, name=name, grid=(rows // tr,),
        in_specs=[pl.BlockSpec((n, tr, cols), lambda i: (0, i, 0))],
        out_specs=pl.BlockSpec((tr, cols), lambda i: (i, 0)),
        out_shape=jax.ShapeDtypeStruct((rows, cols), out_dtype),
        compiler_params=_params(("parallel",)),
    )(a)


def _add2_stacked(a, b, stacked, l, *, name):
    rows, cols = a.shape
    tr = _tile(rows, 256, 8)

    def body(a_ref, b_ref, *rest):
        rest[-1][...] = a_ref[...] + b_ref[...]

    blk = pl.BlockSpec((tr, cols), lambda i: (i, 0))
    carried = [] if stacked is None else [stacked]
    return pl.pallas_call(
        body, name=name, grid=(rows // tr,),
        in_specs=[blk, blk] + [pl.BlockSpec(memory_space=pl.ANY) for _ in carried],
        out_specs=pl.BlockSpec((None, tr, cols), lambda i: (l, i, 0)),
        out_shape=jax.ShapeDtypeStruct((DEPTH, rows, cols), F32),
        input_output_aliases={2: 0} if carried else {},
        compiler_params=_params(("parallel",)),
    )(a, b, *carried)


def _adamw(w, g, m, v, *, name):
    shape = w.shape
    if w.ndim == 2:
        w, g, m, v = (a.reshape((1,) + shape) for a in (w, g, m, v))
    layers, rows, cols = w.shape
    tr = _tile(rows, max(8, (1 << 18) // cols), 8)
    c1 = 1.0 - ADAM_B1 ** ADAM_STEP
    c2 = 1.0 - ADAM_B2 ** ADAM_STEP

    def body(w_ref, g_ref, m_ref, v_ref, d_ref, nm_ref, nv_ref):
        gv = g_ref[...]
        nm = ADAM_B1 * m_ref[...] + (1.0 - ADAM_B1) * gv
        nv = ADAM_B2 * v_ref[...] + (1.0 - ADAM_B2) * (gv * gv)
        nm_ref[...] = nm
        nv_ref[...] = nv
        d_ref[...] = -ADAM_LR * ((nm / c1) / (jnp.sqrt(nv / c2) + ADAM_EPS) + ADAM_WD * w_ref[...])

    blk = pl.BlockSpec((None, tr, cols), lambda l, i: (l, i, 0))
    outs = pl.pallas_call(
        body, name=name, grid=(layers, rows // tr), in_specs=[blk] * 4, out_specs=[blk] * 3,
        out_shape=[jax.ShapeDtypeStruct((layers, rows, cols), F32)] * 3,
        compiler_params=_params(("parallel", "parallel")),
    )(w, g, m, v)
    return [o.reshape(shape) for o in outs]


def _coords():
    return lax.axis_index("x"), lax.axis_index("y"), lax.axis_index("c")


def _other_chips(x, y):
    return [(1 - x, y), (x, 1 - y), (1 - x, 1 - y)]


def _all_gather_small(blk, *, name):
    m_per, n = blk.shape

    def body(x_ref, out_ref, send_sems, recv_sems, local_sem):
        x, y, c = _coords()
        me, sibling = (x, y, c), (x, y, 1 - c)
        chips = _other_chips(x, y)

        def rows(px, py, pc):
            return out_ref.at[pl.ds((4 * px + 2 * py + pc) * m_per, m_per), :]

        def copy(k, block, to, src=None):
            return pltpu.make_async_remote_copy(
                src_ref=rows(*block) if src is None else src, dst_ref=rows(*block),
                send_sem=send_sems.at[k], recv_sem=recv_sems.at[k], device_id=to, device_id_type=MESH)

        mine = pltpu.make_async_copy(x_ref, rows(*me), local_sem)
        mine.start()
        first = [copy(0, me, sibling, src=x_ref)]
        first += [copy(1 + j, me, (*chip, c), src=x_ref) for j, chip in enumerate(chips)]
        for cp in first:
            cp.start()
        passed = [copy(4 + j, (*chip, c), sibling) for j, chip in enumerate(chips)]
        for j, chip in enumerate(chips):
            copy(1 + j, (*chip, c), me).wait_recv()
            passed[j].start()
        copy(0, sibling, me).wait_recv()
        for j, chip in enumerate(chips):
            copy(4 + j, (*chip, 1 - c), me).wait_recv()
        for cp in first + passed:
            cp.wait_send()
        mine.wait()

    return pl.pallas_call(
        body, name=name,
        out_shape=jax.ShapeDtypeStruct((N_DEV * m_per, n), blk.dtype),
        in_specs=[pl.BlockSpec(memory_space=pltpu.VMEM)],
        out_specs=pl.BlockSpec(memory_space=pltpu.VMEM),
        scratch_shapes=[pltpu.SemaphoreType.DMA((7,)), pltpu.SemaphoreType.DMA((7,)), pltpu.SemaphoreType.DMA],
        compiler_params=_params(),
    )(blk)


HBM_SPEC = pl.BlockSpec(memory_space=pltpu.HBM)
SEM_SPEC = pl.BlockSpec(memory_space=pltpu.SEMAPHORE)
DATAFLOW = pltpu.SideEffectType.DATAFLOW_SIDE_EFFECTING


def _chip_copies(src_ref, land_ref, send_sems, recv_sems, scatter):
    x, y, c = _coords()
    my = 2 * x + y
    outgoing, incoming = [], []
    for k, (px, py) in enumerate(_other_chips(x, y)):
        peer = 2 * px + py

        def copy(src_slot, dst_slot):
            return pltpu.make_async_remote_copy(
                src_ref=src_ref.at[src_slot] if scatter else src_ref, dst_ref=land_ref.at[dst_slot],
                send_sem=send_sems.at[k], recv_sem=recv_sems.at[k], device_id=(px, py, c), device_id_type=MESH)

        outgoing.append(copy(peer, my))
        incoming.append(copy(my, peer))
    return outgoing, incoming


def _exchange_start(srcs, *, name, scatter):
    n = len(srcs)
    land_shapes = [src.shape if scatter else (N_CHIPS,) + src.shape for src in srcs]

    def body(*refs):
        for k in range(n):
            send_sems, recv_sems = refs[2 * n + 4 * k], refs[2 * n + 4 * k + 1]
            outgoing, _ = _chip_copies(refs[k], refs[n + k], send_sems, recv_sems, scatter)
            for cp in outgoing:
                cp.start()
        refs[-1][...] = jnp.zeros_like(refs[-1])

    out_shape, out_specs, aliases = [], [], {}
    for k, (src, land_shape) in enumerate(zip(srcs, land_shapes)):
        out_shape += [pltpu.SemaphoreType.DMA((N_CHIPS - 1,)), pltpu.SemaphoreType.DMA((N_CHIPS - 1,)),
                      pltpu.HBM(src.shape, src.dtype), pltpu.HBM(land_shape, src.dtype)]
        out_specs += [SEM_SPEC, SEM_SPEC, HBM_SPEC, HBM_SPEC]
        aliases.update({k: 4 * k + 2, n + k: 4 * k + 3})
    outs = pl.pallas_call(
        body, name=name,
        out_shape=tuple(out_shape) + (jax.ShapeDtypeStruct((8, LANES), F32),),
        in_specs=(HBM_SPEC,) * (2 * n),
        out_specs=tuple(out_specs) + (pl.BlockSpec(memory_space=pltpu.VMEM),),
        input_output_aliases=aliases,
        compiler_params=pltpu.CompilerParams(has_side_effects=DATAFLOW),
    )(*[pltpu.with_memory_space_constraint(src, pltpu.HBM) for src in srcs],
      *[pltpu.with_memory_space_constraint(lax.empty(shape, src.dtype), pltpu.HBM)
        for src, shape in zip(srcs, land_shapes)])
    return [tuple(outs[4 * k:4 * k + 4]) for k in range(n)], outs[-1]


def _exchange_wait(started, after, *, name, scatter):
    send_sems, recv_sems, src_thru, land_thru = started

    def body(src_ref, land_ref, send_sems, recv_sems, after_ref, src_dead, got_ref):
        outgoing, incoming = _chip_copies(src_ref, land_ref, send_sems, recv_sems, scatter)
        for cp in outgoing:
            cp.wait_send()
        for cp in incoming:
            cp.wait_recv()

    return pl.pallas_call(
        body, name=name,
        out_shape=(pltpu.HBM(src_thru.shape, src_thru.dtype), pltpu.HBM(land_thru.shape, land_thru.dtype)),
        in_specs=(HBM_SPEC, HBM_SPEC, SEM_SPEC, SEM_SPEC, pl.BlockSpec(memory_space=pl.ANY)),
        out_specs=(HBM_SPEC, HBM_SPEC),
        input_output_aliases={0: 0, 1: 1},
        compiler_params=pltpu.CompilerParams(has_side_effects=DATAFLOW),
    )(src_thru, land_thru, send_sems, recv_sems, after)


def _exchange_sibling(srcs, *, name):
    n = len(srcs)

    def body(*refs):
        src_refs, out_refs, send_sems, recv_sems = refs[:n], refs[n:2 * n], refs[2 * n], refs[2 * n + 1]
        x, y, c = _coords()
        copies = [pltpu.make_async_remote_copy(src_ref=src_ref, dst_ref=out_ref, send_sem=send_sems.at[k],
                                               recv_sem=recv_sems.at[k], device_id=(x, y, 1 - c), device_id_type=MESH)
                  for k, (src_ref, out_ref) in enumerate(zip(src_refs, out_refs))]
        for cp in copies:
            cp.start()
        for cp in copies:
            cp.wait()

    return pl.pallas_call(
        body, name=name,
        out_shape=[jax.ShapeDtypeStruct(src.shape, src.dtype) for src in srcs],
        in_specs=[pl.BlockSpec(memory_space=pl.ANY)] * n,
        out_specs=[pl.BlockSpec(memory_space=pl.ANY)] * n,
        scratch_shapes=[pltpu.SemaphoreType.DMA((n,)), pltpu.SemaphoreType.DMA((n,))],
        compiler_params=_params(),
    )(*srcs)


def _pack_rows(a):
    return a.reshape(-1, D_MODEL)


def _pad_heads(w, width):
    r = w.shape[0]
    return jnp.pad(w, ((0, 0), (0, 0), (0, HEAD_PAD - width))).reshape(r, N_HEADS * HEAD_PAD)


MIX_NAMES = ("w_uq", "w_uk", "w_uv", "p_pool", "p_attn", "w_out")
GROUPS = ("in", "mix", "ff1", "ff2")


def _local_shard(weights, l, group, zero):
    if group == "mix":
        shard = jnp.concatenate([_pack_rows(weights[n][l]) for n in MIX_NAMES], axis=0)
    else:
        shard = weights[{"in": "w_in", "ff1": "w_ff1", "ff2": "w_ff2"}[group]][l]
    return (shard + zero).astype(BF16)


def _unpack_weights(gathered, group):
    def cols(a, k):
        return a.reshape(N_CHIPS, k, -1).transpose(1, 0, 2).reshape(k, -1)

    if group == "in":
        full = gathered.reshape(W_IN_COLS, D_MODEL)
        u, cq, ckv, kr, gates = (full[a:b] for a, b in (W_IN_U, W_IN_CQ, W_IN_CKV, W_IN_KR, W_IN_GATES))
        kr = jnp.pad(kr, ((QK_NOPE, HEAD_PAD - QK_DIM), (0, 0)))
        return dict(w_in=jnp.concatenate([cq, kr, u, gates, ckv], axis=0))
    if group == "ff1":
        return dict(w_ff1=gathered)
    if group == "ff2":
        return dict(w_ff2=gathered.reshape(D_FF, D_MODEL))

    def p_attn(a):
        full = cols(a, ATTN_DIM).reshape(N_HEADS, V_DIM, D_MODEL)
        return jnp.pad(full, ((0, 0), (0, HEAD_PAD - V_DIM), (0, 0))).reshape(N_HEADS * HEAD_PAD, D_MODEL)

    build = dict(
        w_uq=lambda a: _pad_heads(a.reshape(Q_LORA, N_HEADS, QK_DIM), QK_DIM),
        w_uk=lambda a: _pad_heads(a.reshape(KV_LORA, N_HEADS, QK_NOPE), QK_NOPE),
        w_uv=lambda a: _pad_heads(a.reshape(KV_LORA, N_HEADS, V_DIM), V_DIM),
        p_pool=lambda a: cols(a, POOL_DIM),
        p_attn=p_attn,
        w_out=lambda a: a.reshape(D_MODEL, D_MODEL),
    )
    w, off = {}, 0
    for name in MIX_NAMES:
        w[name] = build[name](gathered[:, off:off + ROWS_OF[name]])
        off += ROWS_OF[name]
    return w


def _pack_grads(g, group):
    def cols(a):
        k = a.shape[0]
        return a.reshape(k, N_CHIPS, -1).transpose(1, 0, 2).reshape(N_CHIPS, -1, D_MODEL)

    def rows(a):
        return a.reshape(N_CHIPS, -1, D_MODEL)

    def heads(width):
        return lambda a: rows(a.reshape(a.shape[0], N_HEADS, HEAD_PAD)[:, :, :width])

    if group == "in":
        full = jnp.concatenate([g["u"], g["cq"], g["ckv"], g["kr"][QK_NOPE:QK_DIM], g["ga"], g["gb"]], axis=0)
        return full.reshape(N_CHIPS, W_IN_SHARD, D_MODEL)
    if group == "ff1":
        return g["w_ff1"]
    if group == "ff2":
        return g["w_ff2"].reshape(N_CHIPS, D_FF // N_CHIPS, D_MODEL)

    def p_attn(a):
        return cols(a.reshape(N_HEADS, HEAD_PAD, D_MODEL)[:, :V_DIM].reshape(ATTN_DIM, D_MODEL))

    build = dict(w_uq=heads(QK_DIM), w_uk=heads(QK_NOPE), w_uv=heads(V_DIM), p_pool=cols, p_attn=p_attn, w_out=rows)
    return jnp.concatenate([build[name](g[name]) for name in MIX_NAMES], axis=1)


def _per_head(fn, acc, *tables):
    return jnp.concatenate([fn(acc[:, h * HEAD_PAD:(h + 1) * HEAD_PAD], *tables) for h in range(N_HEADS)], axis=1)


def _rope_head(a, cos, sin):
    lane = lax.broadcasted_iota(jnp.int32, a.shape, 1)
    return a * (cos + jnp.where(lane < QK_NOPE, 1.0, 0.0)) + _rotate_half(a) * sin


def _layer_fwd(l, x, mod, get_weights, small, cos_t, sin_t):
    sh1, sc1, g1, sh2, sc2, g2 = mod
    tag = f"_l{l}"
    h, r1 = _norm_mod(x, small["ln1_g"], sc1, sh1, name="norm1" + tag)
    w = dict(get_weights("in", h))
    (z,) = _mm(h, w["w_in"], tb=True, name="in_proj" + tag, out_dtypes=(BF16,))
    p, yp, cq, ckv, kr, rq, rkv = _mixer_pre(z, cos_t, sin_t, small["w_pool"], small["pool_scale"],
                                              small["q_norm_g"], small["kv_norm_g"], name="mixer_pre" + tag)
    w.update(get_weights("mix", yp))
    (ya,) = _mm(yp, w["p_pool"], name="pool_out" + tag, out_dtypes=(BF16,))
    (q,) = _mm(cq, w["w_uq"], name="q_proj" + tag, out_dtypes=(BF16,),
               epilogue=lambda acc, cos, sin: (_per_head(_rope_head, acc, cos, sin),),
               extras=((cos_t, "table"), (sin_t, "table")))
    (k,) = _mm(ckv, w["w_uk"], name="k_proj" + tag, out_dtypes=(BF16,),
               epilogue=lambda acc, krv: (_per_head(lambda a, b: a + b, acc, krv),), extras=((kr, "table"),))
    (v,) = _mm(ckv, w["w_uv"], name="v_proj" + tag, out_dtypes=(BF16,))
    o, lse = _attn_fwd(q, k, v, name="attn_fwd" + tag)
    yb, merged = _mm(o, w["p_attn"], name="attn_out" + tag, out_dtypes=(BF16, BF16), tm=512,
                     epilogue=lambda acc, ga, gb, yav: (acc, _sigmoid(ga) * yav + _sigmoid(gb) * acc),
                     extras=((z, ("tile", ZC_GA // D_MODEL)), (z, ("tile", ZC_GB // D_MODEL)), (ya, "tile")))
    mo, x1 = _mm(merged, w["w_out"], name="mix_out" + tag, out_dtypes=(BF16, F32),
                 epilogue=lambda acc, xr, g: (acc, xr + g * acc), extras=((x, "tile"), (g1, "row")))
    h2, r2 = _norm_mod(x1, small["ln2_g"], sc2, sh2, name="norm2" + tag)
    w.update(get_weights("ff1", merged))
    f, act = _mm(h2, w["w_ff1"], b_stack=True, name="ff1" + tag, out_dtypes=(BF16, BF16),
                 epilogue=lambda acc: (acc, jnp.square(jnp.maximum(acc, 0.0))))
    w.update(get_weights("ff2", act))
    m2, x2 = _mm(act, w["w_ff2"], name="ff2" + tag, out_dtypes=(BF16, F32),
                 epilogue=lambda acc, xr, g: (acc, xr + g * acc), extras=((x1, "tile"), (g2, "row")))
    saved = dict(x=x, h=h, r1=r1, z=z, p=p, yp=yp, cq=cq, ckv=ckv, rq=rq, rkv=rkv, ya=ya, q=q, k=k, v=v, o=o, lse=lse,
                 yb=yb, merged=merged, mo=mo, x1=x1, h2=h2, r2=r2, f=f, act=act, m2=m2)
    return x2, saved, w


def _merge_grads(dm, ga, gb, ya, yb):
    sa, sb = _sigmoid(ga), _sigmoid(gb)
    return dm * sa, dm * sb, dm * ya * (sa * (1.0 - sa)), dm * yb * (sb * (1.0 - sb))


def _layer_bwd(l, dx2, dm2, dg2, sv, mod, w, small, cos_t, sin_t, send_grads, gate_below):
    sh1, sc1, g1, sh2, sc2, g2 = mod
    tag = f"_l{l}"
    gw = {}
    (df,) = _mm(dm2, w["w_ff2"], tb=True, name="ff2_dx" + tag, out_dtypes=(BF16,),
                epilogue=lambda acc, f: (acc * (2.0 * jnp.maximum(f, 0.0)),), extras=((sv["f"], "tile"),))
    (g_ff2,) = _mm(sv["act"], dm2, ta=True, name="ff2_dw" + tag, out_dtypes=(BF16,))
    (g_ff1,) = _mm(sv["h2"], df, ta=True, out_stack=N_CHIPS, name="ff1_dw" + tag, out_dtypes=(BF16,))
    sc2 = sc2 + send_grads("ff2", dict(w_ff2=g_ff2)) + send_grads("ff1", dict(w_ff1=g_ff1))
    (dh2,) = _mm(df, w["w_ff1"], tb=True, b_stack=True, name="ff1_dx" + tag)
    dx1, dln2, dsc2, dsh2, dmo, dg1 = _norm_mod_bwd(dh2, sv["x1"], sv["r2"], small["ln2_g"], sc2, dx2,
                                                    gate=(sv["mo"], g1), name="norm2_bwd" + tag)
    dya, dyb, dga, dgb = _mm(dmo, w["w_out"], tb=True, name="mix_out_dx" + tag, out_dtypes=(BF16,) * 4, tm=512,
                             epilogue=_merge_grads,
                             extras=((sv["z"], ("tile", ZC_GA // D_MODEL)), (sv["z"], ("tile", ZC_GB // D_MODEL)),
                                     (sv["ya"], "tile"), (sv["yb"], "tile")))
    (gw["w_out"],) = _mm(sv["merged"], dmo, ta=True, name="mix_out_dw" + tag, out_dtypes=(BF16,))
    (gw["p_pool"],) = _mm(sv["yp"], dya, ta=True, name="pool_out_dw" + tag, out_dtypes=(BF16,))
    (dyp,) = _mm(dya, w["p_pool"], tb=True, name="pool_out_dx" + tag)
    du, g_w_pool, g_pool_scale = _pool_bwd(dyp, sv["p"], small["w_pool"], small["pool_scale"], name="pool_bwd" + tag)
    (gw["p_attn"],) = _mm(sv["o"], dyb, ta=True, name="attn_out_dw" + tag, out_dtypes=(BF16,))
    (do,) = _mm(dyb, w["p_attn"], tb=True, name="attn_out_dx" + tag, out_dtypes=(BF16,))
    dql, dkb, dv, dk_sums = _attn_bwd(sv["q"], sv["k"], sv["v"], do, sv["o"], sv["lse"], cos_t, sin_t,
                                      name="attn_bwd" + tag)
    dkr = _key_bwd(dk_sums, cos_t, sin_t, name="key_bwd" + tag)
    (gw["w_uq"],) = _mm(sv["cq"], dql, ta=True, name="q_proj_dw" + tag, out_dtypes=(BF16,))
    (gw["w_uk"],) = _mm(sv["ckv"], dkb, ta=True, name="k_proj_dw" + tag, out_dtypes=(BF16,))
    (gw["w_uv"],) = _mm(sv["ckv"], dv, ta=True, name="v_proj_dw" + tag, out_dtypes=(BF16,))
    (dcq,) = _mm(dql, w["w_uq"], tb=True, name="q_proj_dx" + tag)
    (dckv,) = _mm(dkb, w["w_uk"], tb=True, second=(dv, w["w_uv"]), name="kv_proj_dx" + tag)
    q_norm_g = small["q_norm_g"] + send_grads("mix", gw)
    dcq_raw, g_qn = _rms_bwd(dcq, sv["z"], ZC_CQ, sv["rq"], q_norm_g, name="q_norm_bwd" + tag)
    dckv_raw, g_kvn = _rms_bwd(dckv, sv["z"], ZC_CKV, sv["rkv"], small["kv_norm_g"], name="kv_norm_bwd" + tag)
    dz = dict(cq=dcq_raw, kr=dkr, u=du, ga=dga, gb=dgb, ckv=dckv_raw)
    g_in = {n: _mm(piece, sv["h"], ta=True, name=f"in_proj_dw_{n}" + tag, out_dtypes=(BF16,))[0]
            for n, piece in dz.items()}
    sc1 = sc1 + send_grads("in", g_in)
    dh = _mm_sum(list(dz.values()), w["w_in"], [Z_OFFSETS[n] for n in dz], name="in_proj_dx" + tag)
    dx, dln1, dsc1, dsh1, *below = _norm_mod_bwd(dh, sv["x"], sv["r1"], small["ln1_g"], sc1, dx1, gate=gate_below,
                                                 name="norm1_bwd" + tag)
    dmod = jnp.concatenate([dsh1, dsc1, dg1, dsh2, dsc2, dg2], axis=0)
    gsmall = dict(ln1_g=dln1, ln2_g=dln2, q_norm_g=g_qn, kv_norm_g=g_kvn, w_pool=g_w_pool, pool_scale=g_pool_scale)
    return dx, dmod, gsmall, below


SMALL_LOSS = 6
SMALL_SINGLES = 16
SMALL_POOL = 24
SMALL_POOL_ROWS = len(POOL_WINDOWS) * POOL_GROUP * POOL_GROUP // D_MODEL
SMALL_ROWS = SMALL_POOL + DEPTH * SMALL_POOL_ROWS


def _pack_small(parts, *, name):
    def body(*refs):
        out_ref = refs[-1]
        out_ref[...] = jnp.zeros_like(out_ref)
        for ref, (_, row) in zip(refs[:-1], parts):
            out_ref[row:row + ref.shape[0], :] = ref[...]

    return pl.pallas_call(body, name=name, out_shape=jax.ShapeDtypeStruct((SMALL_ROWS, D_MODEL), F32),
                          compiler_params=_params())(*[a for a, _ in parts])


def kernel(x, c, positions, ln1_g, ln2_g, w_ada, b_ada, w_in, q_norm_g, w_uq, kv_norm_g, w_uk, w_uv, w_pool, pool_scale, p_pool, p_attn, w_out, w_ff1, w_ff2, final_g, loss_target, m_ln1_g, m_ln2_g, m_w_ada, m_b_ada, m_w_in, m_q_norm_g, m_w_uq, m_kv_norm_g, m_w_uk, m_w_uv, m_w_pool, m_pool_scale, m_p_pool, m_p_attn, m_w_out, m_w_ff1, m_w_ff2, m_final_g, v_ln1_g, v_ln2_g, v_w_ada, v_b_ada, v_w_in, v_q_norm_g, v_w_uq, v_kv_norm_g, v_w_uk, v_w_uv, v_w_pool, v_pool_scale, v_p_pool, v_p_attn, v_w_out, v_w_ff1, v_w_ff2, v_final_g):
    weights = dict(ln1_g=ln1_g, ln2_g=ln2_g, w_ada=w_ada, b_ada=b_ada, w_in=w_in, q_norm_g=q_norm_g, w_uq=w_uq,
                   kv_norm_g=kv_norm_g, w_uk=w_uk, w_uv=w_uv, w_pool=w_pool, pool_scale=pool_scale, p_pool=p_pool,
                   p_attn=p_attn, w_out=w_out, w_ff1=w_ff1, w_ff2=w_ff2, final_g=final_g)
    moms = dict(ln1_g=m_ln1_g, ln2_g=m_ln2_g, w_ada=m_w_ada, b_ada=m_b_ada, w_in=m_w_in, q_norm_g=m_q_norm_g,
                w_uq=m_w_uq, kv_norm_g=m_kv_norm_g, w_uk=m_w_uk, w_uv=m_w_uv, w_pool=m_w_pool,
                pool_scale=m_pool_scale, p_pool=m_p_pool, p_attn=m_p_attn, w_out=m_w_out, w_ff1=m_w_ff1,
                w_ff2=m_w_ff2, final_g=m_final_g)
    vels = dict(ln1_g=v_ln1_g, ln2_g=v_ln2_g, w_ada=v_w_ada, b_ada=v_b_ada, w_in=v_w_in, q_norm_g=v_q_norm_g,
                w_uq=v_w_uq, kv_norm_g=v_kv_norm_g, w_uk=v_w_uk, w_uv=v_w_uv, w_pool=v_w_pool,
                pool_scale=v_pool_scale, p_pool=v_p_pool, p_attn=v_p_attn, w_out=v_w_out, w_ff1=v_w_ff1,
                w_ff2=v_w_ff2, final_g=v_final_g)
    order = list(weights)
    for table in (weights, moms, vels):
        table["w_in"] = jnp.swapaxes(table["w_in"], 1, 2)
    seq = x.shape[1]
    my_chip = 2 * lax.axis_index("x") + lax.axis_index("y")
    my_dev = 2 * my_chip + lax.axis_index("c")
    ada_cols = w_ada.shape[2]

    small = [dict(ln1_g=ln1_g[l:l + 1], ln2_g=ln2_g[l:l + 1], q_norm_g=q_norm_g[l:l + 1], kv_norm_g=kv_norm_g[l:l + 1],
                  w_pool=w_pool[l], pool_scale=pool_scale[l:l + 1]) for l in range(DEPTH)]

    c_all = _all_gather_small(jnp.pad(c, ((0, 7), (0, 0))), name="cond_all_gather")
    c_act = _silu(c_all, name="cond_silu")
    b_mine = lax.dynamic_slice_in_dim(b_ada, my_chip * ada_cols, ada_cols, axis=1).reshape(1, DEPTH * ada_cols)
    (mod_cat,) = _mm(c_act, w_ada, b_stack=True, name="ada_fwd", epilogue=lambda acc, b: (acc + b,),
                     extras=((b_mine, "row"),))
    mod_mine = jnp.concatenate([mod_cat[::8, l * ada_cols:(l + 1) * ada_cols] for l in range(DEPTH)], axis=0)
    mod_all = _all_gather_small(mod_mine, name="mod_all_gather").reshape(N_DEV, DEPTH, N_DEV, ada_cols)

    zero = mod_all[0, 0, 0, 0] * 0.0
    keys = [(l, group) for l in range(DEPTH) for group in GROUPS]
    exchanges, token = _exchange_start([_local_shard(weights, l, group, zero) for l, group in keys],
                                       name="weights_send", scatter=False)
    started = dict(zip(keys, exchanges))
    pin = token[0:1, 0:1]

    def gathered_weights(l, group, after):
        mine, land = _exchange_wait(started[l, group], after, name=f"weights_wait_l{l}_{group}", scatter=False)
        land = lax.dynamic_update_slice_in_dim(land, mine[None], my_chip, axis=0)
        return _unpack_weights(land, group)

    mods = []
    for l in range(DEPTH):
        row = jnp.concatenate([lax.dynamic_index_in_dim(mod_all[2 * j, l], my_dev, axis=0, keepdims=True)
                               for j in range(N_CHIPS)], axis=1) + pin
        mods.append([row[:, i * D_MODEL:(i + 1) * D_MODEL] for i in range(N_MOD)])

    inv_freq = ROPE_THETA ** (-jnp.arange(0, QK_ROPE, 2, dtype=F32) / QK_ROPE)
    freq_lanes = jnp.concatenate([jnp.zeros((QK_NOPE,), F32), inv_freq, inv_freq,
                                  jnp.zeros((HEAD_PAD - QK_DIM,), F32)]).reshape(1, LANES)
    cos_t, sin_t = _rope_tables(positions.reshape(seq, 1), freq_lanes, name="rope_tables")

    xs, saved, wl = x.reshape(seq, D_MODEL), [], []
    for l in range(DEPTH):
        xs, sv, w_l = _layer_fwd(l, xs, mods[l], functools.partial(gathered_weights, l), small[l], cos_t, sin_t)
        saved.append(sv)
        wl.append(w_l)
    dx, loss_part, g_final, dm2, dg2 = _final_loss(xs, final_g.reshape(1, D_MODEL), loss_target.reshape(seq, D_MODEL),
                                                   saved[-1]["m2"], mods[-1][5], name="final_loss")

    sent, pending = [], []
    send_after = {(0, "ff1"), (0, "mix"), (0, "in")}

    def send_grads(l, group, g):
        pending.append((l, group, _pack_grads(g, group)))
        if (l, group) not in send_after:
            return jnp.zeros((1, 1), F32)
        exchanges, token_g = _exchange_start([gpack for _, _, gpack in pending], name=f"grads_send_l{l}_{group}",
                                             scatter=True)
        sent.extend((item[0], item[1], exchange) for item, exchange in zip(pending, exchanges))
        pending.clear()
        return token_g[0:1, 0:1]

    dmod, gsmall = [None] * DEPTH, [None] * DEPTH
    for l in reversed(range(DEPTH)):
        gate_below = (saved[l - 1]["m2"], mods[l - 1][5]) if l > 0 else None
        dx, dmod[l], gsmall[l], below = _layer_bwd(l, dx, dm2, dg2, saved[l], mods[l], wl[l], small[l], cos_t, sin_t,
                                                   functools.partial(send_grads, l), gate_below)
        dm2, dg2 = below if below else (None, None)
    grads = dict(x=dx.reshape(1, seq, D_MODEL))

    def lanes(a):
        flat = a.reshape(1, -1)
        return jnp.pad(flat, ((0, 0), (0, D_MODEL - flat.shape[1])))

    singles = [gsmall[0]["ln1_g"], gsmall[1]["ln1_g"], gsmall[0]["ln2_g"], gsmall[1]["ln2_g"], g_final,
               lanes(jnp.concatenate([gsmall[l]["pool_scale"] for l in range(DEPTH)], axis=1)),
               lanes(jnp.concatenate([gsmall[l]["q_norm_g"] for l in range(DEPTH)], axis=1)),
               lanes(jnp.concatenate([gsmall[l]["kv_norm_g"] for l in range(DEPTH)], axis=1))]
    parts = [(dmod[0], 0), (lanes(loss_part), SMALL_LOSS), (dmod[1], 8)]
    parts += [(a, SMALL_SINGLES + i) for i, a in enumerate(singles)]
    parts += [(gsmall[l]["w_pool"].reshape(-1, D_MODEL), SMALL_POOL + l * SMALL_POOL_ROWS) for l in range(DEPTH)]
    small_all = _all_gather_small(_pack_small(parts, name="small_grads_pack"), name="small_grads_all_gather")
    small_all = small_all.reshape(N_DEV, SMALL_ROWS, D_MODEL)
    ssum = _sum_slots(small_all, N_DEV, name="small_grads_sum")
    loss = ssum[SMALL_LOSS, 0]
    grads["b_ada"] = jnp.stack([ssum[8 * l:8 * l + N_MOD] for l in range(DEPTH)]).reshape(DEPTH, N_MOD * D_MODEL)
    grads["ln1_g"] = ssum[SMALL_SINGLES:SMALL_SINGLES + 2]
    grads["ln2_g"] = ssum[SMALL_SINGLES + 2:SMALL_SINGLES + 4]
    grads["final_g"] = ssum[SMALL_SINGLES + 4]
    grads["pool_scale"] = ssum[SMALL_SINGLES + 5].reshape(DEPTH, POOL_DIM)
    grads["q_norm_g"] = ssum[SMALL_SINGLES + 6, :DEPTH * Q_LORA].reshape(DEPTH, Q_LORA)
    grads["kv_norm_g"] = ssum[SMALL_SINGLES + 7, :DEPTH * KV_LORA].reshape(DEPTH, KV_LORA)
    grads["w_pool"] = ssum[SMALL_POOL:SMALL_ROWS].reshape(w_pool.shape)

    parts, after = [], ssum
    for l, group, started_g in sent:
        tg = f"_l{l}_{group}"
        gpack, land = _exchange_wait(started_g, after, name="grads_wait" + tg, scatter=True)
        own = lax.dynamic_index_in_dim(gpack, my_chip, axis=0, keepdims=True)
        land = lax.dynamic_update_slice_in_dim(land, own, my_chip, axis=0)
        parts.append(_sum_slots(land, N_CHIPS, name="grads_sum_chips" + tg))
        after = parts[-1]
    gsum = {}
    for (l, group, _), part, other in zip(sent, parts, _exchange_sibling(parts, name="grads_swap_cores")):
        gsum[group] = _add2_stacked(part, other, gsum.get(group), l, name=f"grads_sum_cores_l{l}_{group}")
    grads.update(w_in=gsum["in"], w_ff1=gsum["ff1"], w_ff2=gsum["ff2"])
    off = 0
    for name in MIX_NAMES:
        grads[name] = gsum["mix"][:, off:off + ROWS_OF[name]].reshape(weights[name].shape)
        off += ROWS_OF[name]

    c_act_t = jnp.pad(c_act[::8].T, ((0, 0), (0, LANES - N_DEV)))
    d_mine = []
    for l in range(DEPTH):
        d_all = small_all[:, 8 * l:8 * l + N_MOD].reshape(N_DEV, N_MOD * D_MODEL)
        d_mine.append(lax.dynamic_slice_in_dim(d_all, my_chip * ada_cols, ada_cols, axis=1))
    d_cat = jnp.pad(jnp.concatenate(d_mine, axis=1), ((0, LANES - N_DEV), (0, 0)))
    (grads["w_ada"],) = _mm(c_act_t, d_cat, out_stack=DEPTH, name="ada_dw")

    def view(a):
        return a.reshape(1, -1) if a.ndim == 1 else a if a.ndim == 3 else a.reshape(-1, a.shape[-1])

    delta, new_m, new_v = {}, {}, {}
    for name in order:
        shape = weights[name].shape
        d, nm, nv = _adamw(view(weights[name]), view(grads[name]), view(moms[name]), view(vels[name]),
                           name="adamw_" + name)
        delta[name], new_m[name], new_v[name] = d.reshape(shape), nm.reshape(shape), nv.reshape(shape)
    for table in (grads, delta, new_m, new_v):
        table["w_in"] = jnp.swapaxes(table["w_in"], 1, 2)
    return (loss, grads["x"], *[grads[n] for n in order], *[delta[n] for n in order],
            *[new_m[n] for n in order], *[new_v[n] for n in order])
```

```python
import functools
import math

import jax
import jax.numpy as jnp
from jax import lax
from jax.experimental import pallas as pl
from jax.experimental.pallas import tpu as pltpu

F32 = jnp.float32
BF16 = jnp.bfloat16
MESH = pl.DeviceIdType.MESH

D_MODEL = 1024
DEPTH = 2
POOL_WINDOWS = (2, 4, 8, 16)
POOL_GROUP = 128
POOL_DIM = 512
N_HEADS = 8
QK_NOPE = 64
QK_ROPE = 32
QK_DIM = QK_NOPE + QK_ROPE
V_DIM = 64
HEAD_PAD = 128
Q_LORA = 384
KV_LORA = 256
ROPE_THETA = 10000.0
ATTN_DIM = N_HEADS * V_DIM
D_FF = 4 * D_MODEL
N_MOD = 6
EPS = 1e-6
N_CHIPS = 4
N_DEV = 8

ADAM_LR = 0.001
ADAM_B1 = 0.9
ADAM_B2 = 0.999
ADAM_EPS = 1e-08
ADAM_WD = 0.01
ADAM_STEP = 10

VMEM_LIMIT_BYTES = 56 * 1024 * 1024
LANES = 128
HALO = 16

ZC_CQ = 0
ZC_KR = 384
ZC_U = 512
ZC_GA = 1024
ZC_GB = 2048
ZC_CKV = 3072
Z_DIM = 3328
Z_OFFSETS = dict(cq=ZC_CQ, kr=ZC_KR, u=ZC_U, ga=ZC_GA, gb=ZC_GB, ckv=ZC_CKV)

W_IN_U, W_IN_CQ, W_IN_CKV, W_IN_KR, W_IN_GATES = (0, 512), (512, 896), (896, 1152), (1152, 1184), (1184, 3232)
W_IN_COLS = W_IN_GATES[1]
W_IN_SHARD = W_IN_COLS // N_CHIPS

ROWS_OF = dict(w_uq=72, w_uk=32, w_uv=32, p_pool=128, p_attn=128, w_out=256)


def _params(sem=None, **kw):
    return pltpu.CompilerParams(dimension_semantics=sem, vmem_limit_bytes=VMEM_LIMIT_BYTES, **kw)


def _tile(n, target, unit=LANES):
    best = None
    for t in range(unit, min(n, target) + 1, unit):
        if n % t == 0:
            best = t
    return best if best is not None and 4 * best >= min(n, target) else n


def _near_tile(n, target):
    cands = [t for t in range(LANES, n + 1, LANES) if n % t == 0]
    return min(cands, key=lambda t: abs(math.log(t / target))) if cands else n


def _mm(a, b, *, name, ta=False, tb=False, out_dtypes=(F32,), epilogue=None, extras=(), tm=1024, tn=1024, tk=1024,
        second=None, b_stack=False, out_stack=None):
    (k_dim, m_dim) = a.shape if ta else a.shape[::-1]
    if b_stack:
        g_b, k_b, n_shard = b.shape
        n_dim, k_b = (k_b, g_b * n_shard) if tb else (g_b * n_shard, k_b)
    else:
        (n_dim, k_b) = b.shape if tb else b.shape[::-1]
    assert k_dim == k_b, (a.shape, b.shape)
    n_unit = n_shard if b_stack and not tb else n_dim // out_stack if out_stack else n_dim
    k_unit = n_shard if b_stack and tb else k_dim
    tm, tn, tk = _near_tile(m_dim, tm), _near_tile(n_unit, tn), _near_tile(k_unit, tk)
    nk = k_dim // tk
    n_extra, n_out = len(extras), len(out_dtypes)
    n_lhs = 4 if second else 2
    dims = (((0 if ta else 1,), (1 if tb else 0,)), ((), ()))
    if epilogue is None:
        epilogue = lambda acc: (acc,) * n_out

    def body(*refs):
        operand_refs, rest = refs[:n_lhs], refs[n_lhs:]
        extra_refs, out_refs = rest[:n_extra], rest[n_extra:n_extra + n_out]

        def product():
            total = None
            for a_ref, b_ref in zip(operand_refs[0::2], operand_refs[1::2]):
                part = lax.dot_general(a_ref[...].astype(BF16), b_ref[...].astype(BF16), dims, preferred_element_type=F32)
                total = part if total is None else total + part
            return total

        def finish(acc):
            outs = epilogue(acc, *[r[...] for r in extra_refs])
            for o_ref, o in zip(out_refs, outs):
                o_ref[...] = o.astype(o_ref.dtype)

        if nk == 1:
            finish(product())
            return
        acc_ref = rest[-1]
        k = pl.program_id(2)

        @pl.when(k == 0)
        def _():
            acc_ref[...] = product()

        @pl.when((k > 0) & (k < nk - 1))
        def _():
            acc_ref[...] += product()

        @pl.when(k == nk - 1)
        def _():
            finish(acc_ref[...] + product())

    a_spec = pl.BlockSpec((tk, tm), lambda i, j, k: (k, i)) if ta else pl.BlockSpec((tm, tk), lambda i, j, k: (i, k))
    if b_stack and tb:
        per = n_shard // tk
        b_spec = pl.BlockSpec((None, tn, tk), lambda i, j, k: (k // per, j, k % per))
    elif b_stack:
        per = n_shard // tn
        b_spec = pl.BlockSpec((None, tk, tn), lambda i, j, k: (j // per, k, j % per))
    elif tb:
        b_spec = pl.BlockSpec((tn, tk), lambda i, j, k: (j, k))
    else:
        b_spec = pl.BlockSpec((tk, tn), lambda i, j, k: (k, j))
    if out_stack:
        per_out = (n_dim // out_stack) // tn
        out_spec = pl.BlockSpec((None, tm, tn), lambda i, j, k: (j // per_out, i, j % per_out))
        out_dims = (out_stack, m_dim, n_dim // out_stack)
    else:
        out_spec = pl.BlockSpec((tm, tn), lambda i, j, k: (i, j))
        out_dims = (m_dim, n_dim)
    extra_specs = []
    for arr, kind in extras:
        if kind == "tile":
            extra_specs.append(pl.BlockSpec((tm, tn), lambda i, j, k: (i, j)))
        elif isinstance(kind, tuple):
            extra_specs.append(pl.BlockSpec((tm, tn), functools.partial(lambda i, j, k, c: (i, j + c), c=kind[1])))
        elif kind == "row":
            extra_specs.append(pl.BlockSpec((1, tn), lambda i, j, k: (0, j)))
        elif kind == "col":
            extra_specs.append(pl.BlockSpec((tm, 1), lambda i, j, k: (i, 0)))
        else:
            assert kind == "table", kind
            extra_specs.append(pl.BlockSpec((tm, LANES), lambda i, j, k: (i, 0)))
    return pl.pallas_call(
        body,
        name=name,
        grid=(m_dim // tm, n_dim // tn, nk),
        in_specs=[a_spec, b_spec] * (n_lhs // 2) + extra_specs,
        out_specs=[out_spec for _ in out_dtypes],
        out_shape=[jax.ShapeDtypeStruct(out_dims, dt) for dt in out_dtypes],
        scratch_shapes=[pltpu.VMEM((tm, tn), F32)] if nk > 1 else [],
        compiler_params=_params(("parallel", "parallel", "arbitrary")),
    )(a, b, *(second or ()), *[arr for arr, _ in extras])


def _mm_sum(pieces, b, offsets, *, name, tm=1024, tn=1024):
    m_dim, n_dim = pieces[0].shape[0], b.shape[1]
    tm, tn = _near_tile(m_dim, tm), _near_tile(n_dim, tn)
    n_pieces = len(pieces)

    def body(*refs):
        total = None
        for a_ref, b_ref in zip(refs[:n_pieces], refs[n_pieces:2 * n_pieces]):
            part = jnp.dot(a_ref[...], b_ref[...], preferred_element_type=F32)
            total = part if total is None else total + part
        refs[-1][...] = total

    a_specs = [pl.BlockSpec((tm, p.shape[1]), lambda i, j: (i, 0)) for p in pieces]
    b_specs = [pl.BlockSpec((p.shape[1], tn), functools.partial(lambda i, j, blk: (blk, j), blk=off // p.shape[1]))
               for p, off in zip(pieces, offsets)]
    return pl.pallas_call(
        body, name=name, grid=(m_dim // tm, n_dim // tn),
        in_specs=a_specs + b_specs,
        out_specs=pl.BlockSpec((tm, tn), lambda i, j: (i, j)),
        out_shape=jax.ShapeDtypeStruct((m_dim, n_dim), F32),
        compiler_params=_params(("parallel", "parallel")),
    )(*pieces, *[b] * n_pieces)


def _rows(s):
    return min(512, s)


def _rope_tables(pos_col, inv_freq_lanes, *, name):
    s = pos_col.shape[0]
    tb = _rows(s)

    def body(pos_ref, f_ref, cos_ref, sin_ref):
        ang = pos_ref[...].astype(F32) * f_ref[...]
        lane = lax.broadcasted_iota(jnp.int32, ang.shape, 1)
        on = (lane >= QK_NOPE) & (lane < QK_DIM)
        cos_ref[...] = jnp.where(on, jnp.cos(ang), 0.0)
        sin_ref[...] = jnp.where(on, jnp.sin(ang), 0.0)

    return pl.pallas_call(
        body, name=name, grid=(s // tb,),
        in_specs=[pl.BlockSpec((tb, 1), lambda i: (i, 0)), pl.BlockSpec((1, LANES), lambda i: (0, 0))],
        out_specs=[pl.BlockSpec((tb, LANES), lambda i: (i, 0))] * 2,
        out_shape=[jax.ShapeDtypeStruct((s, LANES), F32)] * 2,
        compiler_params=_params(("parallel",)),
    )(pos_col, inv_freq_lanes)


def _rotate_half(x):
    lane = lax.broadcasted_iota(jnp.int32, x.shape, 1)
    half = QK_ROPE // 2
    first = (lane >= QK_NOPE) & (lane < QK_NOPE + half)
    second = (lane >= QK_NOPE + half) & (lane < QK_DIM)
    return jnp.where(first, -pltpu.roll(x, LANES - half, 1), jnp.where(second, pltpu.roll(x, half, 1), 0.0))


def _norm_mod(x, g, sc, sh, *, name):
    s, d = x.shape
    tb = _rows(s)

    def body(x_ref, g_ref, sc_ref, sh_ref, h_ref, r_ref):
        xv = x_ref[...]
        r = lax.rsqrt(jnp.mean(xv * xv, axis=-1, keepdims=True) + EPS)
        r_ref[...] = r
        h_ref[...] = (((xv * r) * g_ref[...]) * (1.0 + sc_ref[...]) + sh_ref[...]).astype(BF16)

    vec = pl.BlockSpec((1, d), lambda i: (0, 0))
    return pl.pallas_call(
        body, name=name, grid=(s // tb,),
        in_specs=[pl.BlockSpec((tb, d), lambda i: (i, 0)), vec, vec, vec],
        out_specs=[pl.BlockSpec((tb, d), lambda i: (i, 0)), pl.BlockSpec((tb, 1), lambda i: (i, 0))],
        out_shape=[jax.ShapeDtypeStruct((s, d), BF16), jax.ShapeDtypeStruct((s, 1), F32)],
        compiler_params=_params(("parallel",)),
    )(x, g, sc, sh)


def _window_sums(ext, sign):
    n = ext.shape[0]
    sums, cur, k = [], ext, 1
    for _ in POOL_WINDOWS:
        cur = cur + pltpu.roll(cur, k if sign > 0 else n - k, 0)
        sums.append(cur)
        k *= 2
    return sums


def _mixer_pre(z, cos_t, sin_t, w_pool, pool_scale, gq, gkv, *, name):
    s = z.shape[0]
    tb = _rows(s)
    hb = tb // HALO

    def body(zcq_ref, zkr_ref, zu_ref, zuh_ref, zckv_ref, cos_ref, sin_ref, wp_ref, ps_ref, gq_ref, gkv_ref,
             p_ref, yp_ref, cq_ref, ckv_ref, kr_ref, rq_ref, rkv_ref):
        i = pl.program_id(0)
        u = zu_ref[...].astype(F32)
        halo = jnp.where(i > 0, zuh_ref[...].astype(F32), 0.0)
        ext = jnp.concatenate([halo, u], axis=0)
        t = i * tb + lax.broadcasted_iota(jnp.int32, (tb, 1), 0)
        for g, (w, sw) in enumerate(zip(POOL_WINDOWS, _window_sums(ext, +1))):
            cols = slice(g * POOL_GROUP, (g + 1) * POOL_GROUP)
            cnt = jnp.minimum(t + 1, w).astype(F32)
            pg = (sw[HALO:, cols] / cnt - u[:, cols]).astype(BF16)
            p_ref[:, cols] = pg
            yg = jnp.dot(pg, wp_ref[g].astype(BF16), preferred_element_type=F32)
            yp_ref[:, cols] = (yg * ps_ref[:, cols]).astype(BF16)

        def rms(x_ref, g_ref, out_ref, r_ref):
            xv = x_ref[...].astype(F32)
            r = lax.rsqrt(jnp.mean(xv * xv, axis=-1, keepdims=True) + EPS)
            r_ref[...] = r
            out_ref[...] = ((xv * r) * g_ref[...]).astype(BF16)

        rms(zcq_ref, gq_ref, cq_ref, rq_ref)
        rms(zckv_ref, gkv_ref, ckv_ref, rkv_ref)
        kr = zkr_ref[...].astype(F32)
        kr_ref[...] = (kr * cos_ref[...] + _rotate_half(kr) * sin_ref[...]).astype(BF16)

    def zcol(width, off):
        return pl.BlockSpec((tb, width), lambda i: (i, off // width))

    def full(a):
        return pl.BlockSpec(a.shape, lambda i: (0,) * a.ndim)

    def out(width, dt):
        return pl.BlockSpec((tb, width), lambda i: (i, 0)), jax.ShapeDtypeStruct((s, width), dt)

    outs = [out(POOL_DIM, BF16), out(POOL_DIM, BF16), out(Q_LORA, BF16), out(KV_LORA, BF16), out(LANES, BF16),
            out(1, F32), out(1, F32)]
    return pl.pallas_call(
        body, name=name, grid=(s // tb,),
        in_specs=[zcol(Q_LORA, ZC_CQ), zcol(LANES, ZC_KR), zcol(POOL_DIM, ZC_U),
                  pl.BlockSpec((HALO, POOL_DIM), lambda i: (jnp.maximum(i * hb - 1, 0), ZC_U // POOL_DIM)),
                  zcol(KV_LORA, ZC_CKV),
                  pl.BlockSpec((tb, LANES), lambda i: (i, 0)), pl.BlockSpec((tb, LANES), lambda i: (i, 0)),
                  full(w_pool), full(pool_scale), full(gq), full(gkv)],
        out_specs=[o[0] for o in outs], out_shape=[o[1] for o in outs],
        compiler_params=_params(("parallel",)),
    )(z, z, z, z, z, cos_t, sin_t, w_pool, pool_scale, gq, gkv)


def _sigmoid(x):
    return 1.0 / (1.0 + jnp.exp(-x.astype(F32)))


ATTN_SCALE = 1.0 / math.sqrt(QK_DIM)
NEG_BIG = -1e30


LOG2_E = math.log2(math.e)
EXP2_SCALE = ATTN_SCALE * LOG2_E
NT_DIMS = (((1,), (1,)), ((), ()))
TN_DIMS = (((0,), (0,)), ((), ()))


def _on_or_below_diagonal(t):
    return lax.broadcasted_iota(jnp.int32, (t, t), 0) >= lax.broadcasted_iota(jnp.int32, (t, t), 1)


HEADS_PER_STEP = 2
HEAD_COLS = [slice(g * HEAD_PAD, (g + 1) * HEAD_PAD) for g in range(HEADS_PER_STEP)]


def _attn_fwd(q, k, v, *, name):
    s = q.shape[0]
    t = _rows(s)
    wide = HEADS_PER_STEP * HEAD_PAD

    def body(q_ref, k_ref, v_ref, o_ref, lse_ref):
        qi = pl.program_id(1)
        qs = [q_ref[:, cols] for cols in HEAD_COLS]

        def block(j, carry, diagonal):
            rows = pl.ds(pl.multiple_of(j * t, t), t)
            out = []
            for qv, cols, (m, l, acc) in zip(qs, HEAD_COLS, carry):
                sc = lax.dot_general(qv, k_ref[rows, cols], NT_DIMS, preferred_element_type=F32)
                if diagonal:
                    sc = jnp.where(_on_or_below_diagonal(t), sc, NEG_BIG)
                m_new = jnp.maximum(m, jnp.max(sc, axis=-1, keepdims=True))
                p = jnp.exp2((sc - m_new) * EXP2_SCALE)
                alpha = jnp.exp2((m - m_new) * EXP2_SCALE)
                l = alpha * l + jnp.sum(p, axis=-1, keepdims=True)
                acc = alpha * acc + jnp.dot(p.astype(BF16), v_ref[rows, cols], preferred_element_type=F32)
                out.append((m_new, l, acc))
            return tuple(out)

        init = tuple((jnp.full((t, 1), -jnp.inf, F32), jnp.zeros((t, 1), F32), jnp.zeros((t, HEAD_PAD), F32))
                     for _ in HEAD_COLS)
        carry = lax.fori_loop(0, qi, lambda j, c: block(j, c, False), init)
        for g, (cols, (m, l, acc)) in enumerate(zip(HEAD_COLS, block(qi, carry, True))):
            o_ref[:, cols] = (acc / l).astype(BF16)
            lse_ref[g] = m * ATTN_SCALE + jnp.log(l)

    q_spec = pl.BlockSpec((t, wide), lambda h, i: (i, h))
    kv_spec = pl.BlockSpec((s, wide), lambda h, i: (0, h))
    return pl.pallas_call(
        body, name=name, grid=(N_HEADS // HEADS_PER_STEP, s // t),
        in_specs=[q_spec, kv_spec, kv_spec],
        out_specs=[q_spec, pl.BlockSpec((HEADS_PER_STEP, t, 1), lambda h, i: (h, i, 0))],
        out_shape=[jax.ShapeDtypeStruct((s, N_HEADS * HEAD_PAD), BF16), jax.ShapeDtypeStruct((N_HEADS, s, 1), F32)],
        compiler_params=_params(("parallel", "parallel")),
    )(q, k, v)


def _attn_bwd(q, k, v, do, o, lse, cos_t, sin_t, *, name):
    s = q.shape[0]
    t = _rows(s)
    nt = s // t

    def body(q_ref, k_ref, v_ref, do_ref, o_ref, lse_ref, cos_ref, sin_ref, dql_ref, dk_ref, dv_ref, dks_ref,
             dq_ref, dl_ref):
        kj = pl.program_id(1)

        @pl.when(kj == 0)
        def _():
            dq_ref[...] = jnp.zeros_like(dq_ref)

            def delta(i, carry):
                rows = pl.ds(pl.multiple_of(i * t, t), t)
                for g, cols in enumerate(HEAD_COLS):
                    dl_ref[g, rows, :] = jnp.sum(do_ref[rows, cols].astype(F32) * o_ref[rows, cols].astype(F32),
                                                 axis=-1, keepdims=True)
                return carry

            lax.fori_loop(0, nt, delta, 0)

        kvs = [(k_ref[:, cols], v_ref[:, cols]) for cols in HEAD_COLS]

        def block(i, carry, diagonal):
            rows = pl.ds(pl.multiple_of(i * t, t), t)
            out = []
            for g, (cols, (kv, vv), (dk, dv)) in enumerate(zip(HEAD_COLS, kvs, carry)):
                qv, dov = q_ref[rows, cols], do_ref[rows, cols]
                sc = lax.dot_general(qv, kv, NT_DIMS, preferred_element_type=F32)
                p = jnp.exp2(sc * EXP2_SCALE - lse_ref[g, rows, :] * LOG2_E)
                if diagonal:
                    p = jnp.where(_on_or_below_diagonal(t), p, 0.0)
                dp = lax.dot_general(dov, vv, NT_DIMS, preferred_element_type=F32)
                ds = (p * (dp - dl_ref[g, rows, :])).astype(BF16)
                dv = dv + lax.dot_general(p.astype(BF16), dov, TN_DIMS, preferred_element_type=F32)
                dk = dk + lax.dot_general(ds, qv, TN_DIMS, preferred_element_type=F32)
                dq_ref[rows, cols] += jnp.dot(ds, kv, preferred_element_type=F32) * ATTN_SCALE
                out.append((dk, dv))
            return tuple(out)

        zero = jnp.zeros((t, HEAD_PAD), F32)
        carry = block(kj, tuple((zero, zero) for _ in HEAD_COLS), True)
        dk_sum = None
        for cols, (dk, dv) in zip(HEAD_COLS, lax.fori_loop(kj + 1, nt, lambda i, c: block(i, c, False), carry)):
            dk = dk * ATTN_SCALE
            dk_ref[:, cols] = dk.astype(BF16)
            dv_ref[:, cols] = dv.astype(BF16)
            dk_sum = dk if dk_sum is None else dk_sum + dk
        dks_ref[...] = dk_sum

        @pl.when(kj == nt - 1)
        def _():
            def rope_bwd(i, carry):
                rows = pl.ds(pl.multiple_of(i * t, t), t)
                sin = sin_ref[rows, :]
                lane = lax.broadcasted_iota(jnp.int32, sin.shape, 1)
                cos_q = cos_ref[rows, :] + jnp.where(lane < QK_NOPE, 1.0, 0.0)
                for cols in HEAD_COLS:
                    dqv = dq_ref[rows, cols]
                    dql_ref[rows, cols] = (dqv * cos_q - _rotate_half(dqv * sin)).astype(BF16)
                return carry

            lax.fori_loop(0, nt, rope_bwd, 0)

    heads_wide = HEADS_PER_STEP * HEAD_PAD
    full_spec = pl.BlockSpec((s, heads_wide), lambda h, j: (0, h))
    kv_spec = pl.BlockSpec((t, heads_wide), lambda h, j: (j, h))
    vec_spec = pl.BlockSpec((HEADS_PER_STEP, s, 1), lambda h, j: (h, 0, 0))
    table_spec = pl.BlockSpec((s, LANES), lambda h, j: (0, 0))
    wide = jax.ShapeDtypeStruct((s, N_HEADS * HEAD_PAD), BF16)
    n_steps = N_HEADS // HEADS_PER_STEP
    return pl.pallas_call(
        body, name=name, grid=(n_steps, nt),
        in_specs=[full_spec, kv_spec, kv_spec, full_spec, full_spec, vec_spec, table_spec, table_spec],
        out_specs=[full_spec, kv_spec, kv_spec, pl.BlockSpec((None, t, HEAD_PAD), lambda h, j: (h, j, 0))],
        out_shape=[wide, wide, wide, jax.ShapeDtypeStruct((n_steps, s, HEAD_PAD), F32)],
        scratch_shapes=[pltpu.VMEM((s, heads_wide), F32), pltpu.VMEM((HEADS_PER_STEP, s, 1), F32)],
        compiler_params=_params(("parallel", "arbitrary")),
    )(q, k, v, do, o, lse, cos_t, sin_t)


def _acc_specs(widths):
    return ([pl.BlockSpec((1, w), lambda i: (0, 0)) for w in widths],
            [jax.ShapeDtypeStruct((1, w), F32) for w in widths])


def _gate_grads(dxv, m_ref, gate_ref, dm_ref, dgate_ref):
    dm_ref[...] = (dxv * gate_ref[...]).astype(BF16)
    dgate_ref[...] += jnp.sum(dxv * m_ref[...], axis=0, keepdims=True)


def _final_loss(x, g, target, m, gate, *, name):
    s, d = x.shape
    tb = _rows(s)

    def body(x_ref, g_ref, t_ref, m_ref, gate_ref, dx_ref, loss_ref, dg_ref, dm_ref, dgate_ref):
        @pl.when(pl.program_id(0) == 0)
        def _():
            loss_ref[...] = jnp.zeros_like(loss_ref)
            dg_ref[...] = jnp.zeros_like(dg_ref)
            dgate_ref[...] = jnp.zeros_like(dgate_ref)

        xv = x_ref[...]
        r = lax.rsqrt(jnp.mean(xv * xv, axis=-1, keepdims=True) + EPS)
        xn = xv * r
        err = xn * g_ref[...] - t_ref[...]
        loss_ref[...] += 0.5 * jnp.sum(jnp.mean(err * err, axis=-1, keepdims=True), axis=0, keepdims=True)
        dy = err / d
        dg_ref[...] += jnp.sum(dy * xn, axis=0, keepdims=True)
        dxn = dy * g_ref[...]
        dxv = r * (dxn - xn * jnp.mean(dxn * xn, axis=-1, keepdims=True))
        dx_ref[...] = dxv
        _gate_grads(dxv, m_ref, gate_ref, dm_ref, dgate_ref)

    blk = pl.BlockSpec((tb, d), lambda i: (i, 0))
    vec = pl.BlockSpec((1, d), lambda i: (0, 0))
    acc_specs, acc_shapes = _acc_specs((LANES, d))
    return pl.pallas_call(
        body, name=name, grid=(s // tb,),
        in_specs=[blk, vec, blk, blk, vec],
        out_specs=[blk] + acc_specs + [blk, vec],
        out_shape=[jax.ShapeDtypeStruct((s, d), F32)] + acc_shapes + [jax.ShapeDtypeStruct((s, d), BF16),
                                                                     jax.ShapeDtypeStruct((1, d), F32)],
        compiler_params=_params(("arbitrary",)),
    )(x, g, target, m, gate)


def _norm_mod_bwd(dh, x, r, g, sc, dx_skip, *, name, gate=None):
    s, d = x.shape
    tb = _rows(s)
    nb = s // tb
    n_gate = 2 if gate else 0

    def body(dh_ref, x_ref, r_ref, g_ref, sc_ref, skip_ref, *rest):
        gate_refs, (dx_ref, dg_ref, dsc_ref, dsh_ref) = rest[:n_gate], rest[n_gate:n_gate + 4]
        gate_outs, da_sc = rest[n_gate + 4:-1], rest[-1]
        i = pl.program_id(0)

        @pl.when(i == 0)
        def _():
            da_sc[...] = jnp.zeros_like(da_sc)
            dsh_ref[...] = jnp.zeros_like(dsh_ref)
            if gate:
                gate_outs[1][...] = jnp.zeros_like(gate_outs[1])

        dhv, rv = dh_ref[...], r_ref[...]
        xn = x_ref[...] * rv
        dsh_ref[...] += jnp.sum(dhv, axis=0, keepdims=True)
        da_sc[...] += jnp.sum(dhv * xn, axis=0, keepdims=True)
        dxn = dhv * (g_ref[...] * (1.0 + sc_ref[...]))
        dxv = skip_ref[...] + rv * (dxn - xn * jnp.mean(dxn * xn, axis=-1, keepdims=True))
        dx_ref[...] = dxv
        if gate:
            _gate_grads(dxv, *gate_refs, *gate_outs)

        @pl.when(i == nb - 1)
        def _():
            dg_ref[...] = da_sc[...] * (1.0 + sc_ref[...])
            dsc_ref[...] = da_sc[...] * g_ref[...]

    blk = pl.BlockSpec((tb, d), lambda i: (i, 0))
    vec = pl.BlockSpec((1, d), lambda i: (0, 0))
    acc_specs, acc_shapes = _acc_specs((d, d, d))
    gate_specs = [blk, vec] if gate else []
    gate_shapes = [jax.ShapeDtypeStruct((s, d), BF16), jax.ShapeDtypeStruct((1, d), F32)] if gate else []
    return pl.pallas_call(
        body, name=name, grid=(nb,),
        in_specs=[blk, blk, pl.BlockSpec((tb, 1), lambda i: (i, 0)), vec, vec, blk] + gate_specs,
        out_specs=[blk] + acc_specs + gate_specs,
        out_shape=[jax.ShapeDtypeStruct((s, d), F32)] + acc_shapes + gate_shapes,
        scratch_shapes=[pltpu.VMEM((1, d), F32)],
        compiler_params=_params(("arbitrary",)),
    )(dh, x, r, g, sc, dx_skip, *(gate or ()))


def _pool_bwd(dyp, p, w_pool, pool_scale, *, name):
    s = dyp.shape[0]
    tb = _rows(s)
    nb = s // tb
    hb = tb // HALO
    nt_dims = (((1,), (1,)), ((), ()))
    tn_dims = (((0,), (0,)), ((), ()))

    def body(dy_ref, dyn_ref, p_ref, wp_ref, ps_ref, du_ref, gwp_ref, gps_ref):
        i = pl.program_id(0)

        @pl.when(i == 0)
        def _():
            gwp_ref[...] = jnp.zeros_like(gwp_ref)
            gps_ref[...] = jnp.zeros_like(gps_ref)

        cur = dy_ref[...]
        nxt = jnp.where(i < nb - 1, dyn_ref[...], 0.0)
        dpw = (jnp.concatenate([cur, nxt], axis=0) * ps_ref[...]).astype(BF16)
        t = i * tb + lax.broadcasted_iota(jnp.int32, (tb + HALO, 1), 0)
        for g, w in enumerate(POOL_WINDOWS):
            cols = slice(g * POOL_GROUP, (g + 1) * POOL_GROUP)
            wg = wp_ref[g].astype(BF16)
            dp = lax.dot_general(dpw[:, cols], wg, nt_dims, preferred_element_type=F32)
            e = dp / jnp.minimum(t + 1, w).astype(F32)
            lead = _window_sums(e, -1)[g]
            du_ref[:, cols] = (lead[:tb] - dp[:tb]).astype(BF16)
            pg = p_ref[:, cols]
            pw = jnp.dot(pg, wg, preferred_element_type=F32)
            gps_ref[:, cols] += jnp.sum(cur[:, cols] * pw, axis=0, keepdims=True)
            gwp_ref[g] += lax.dot_general(pg, dpw[:tb, cols], tn_dims, preferred_element_type=F32)

    blk = pl.BlockSpec((tb, POOL_DIM), lambda i: (i, 0))
    return pl.pallas_call(
        body, name=name, grid=(nb,),
        in_specs=[blk, pl.BlockSpec((HALO, POOL_DIM), lambda i: (jnp.minimum((i + 1) * hb, s // HALO - 1), 0)), blk,
                  pl.BlockSpec(w_pool.shape, lambda i: (0, 0, 0)), pl.BlockSpec((1, POOL_DIM), lambda i: (0, 0))],
        out_specs=[blk, pl.BlockSpec(w_pool.shape, lambda i: (0, 0, 0)), pl.BlockSpec((1, POOL_DIM), lambda i: (0, 0))],
        out_shape=[jax.ShapeDtypeStruct((s, POOL_DIM), BF16), jax.ShapeDtypeStruct(w_pool.shape, F32),
                   jax.ShapeDtypeStruct((1, POOL_DIM), F32)],
        compiler_params=_params(("arbitrary",)),
    )(dyp, dyp, p, w_pool, pool_scale)


def _key_bwd(dk_sums, cos_t, sin_t, *, name):
    n, s, _ = dk_sums.shape
    tb = _rows(s)

    def body(dk_ref, cos_ref, sin_ref, dkr_ref):
        tot = dk_ref[0]
        for h in range(1, n):
            tot = tot + dk_ref[h]
        dkr_ref[...] = (tot * cos_ref[...] - _rotate_half(tot * sin_ref[...])).astype(BF16)

    tab = pl.BlockSpec((tb, LANES), lambda i: (i, 0))
    return pl.pallas_call(
        body, name=name, grid=(s // tb,),
        in_specs=[pl.BlockSpec((n, tb, LANES), lambda i: (0, i, 0)), tab, tab], out_specs=tab,
        out_shape=jax.ShapeDtypeStruct((s, LANES), BF16),
        compiler_params=_params(("parallel",)),
    )(dk_sums, cos_t, sin_t)


def _rms_bwd(dy, z, z_off, r, g, *, name):
    s, n = dy.shape
    tb = _rows(s)

    def body(dy_ref, x_ref, r_ref, g_ref, dx_ref, dg_ref):
        @pl.when(pl.program_id(0) == 0)
        def _():
            dg_ref[...] = jnp.zeros_like(dg_ref)

        dyv, rv = dy_ref[...], r_ref[...]
        xn = x_ref[...].astype(F32) * rv
        dg_ref[...] += jnp.sum(dyv * xn, axis=0, keepdims=True)
        dxn = dyv * g_ref[...]
        dx_ref[...] = (rv * (dxn - xn * jnp.mean(dxn * xn, axis=-1, keepdims=True))).astype(BF16)

    blk = pl.BlockSpec((tb, n), lambda i: (i, 0))
    acc_specs, acc_shapes = _acc_specs((n,))
    return pl.pallas_call(
        body, name=name, grid=(s // tb,),
        in_specs=[blk, pl.BlockSpec((tb, n), lambda i: (i, z_off // n)), pl.BlockSpec((tb, 1), lambda i: (i, 0)),
                  pl.BlockSpec((1, n), lambda i: (0, 0))],
        out_specs=[blk] + acc_specs, out_shape=[jax.ShapeDtypeStruct((s, n), BF16)] + acc_shapes,
        compiler_params=_params(("arbitrary",)),
    )(dy, z, r, g)


def _silu(c, *, name):
    def body(c_ref, out_ref):
        cv = c_ref[...]
        out_ref[...] = (cv * _sigmoid(cv)).astype(BF16)

    return pl.pallas_call(body, name=name, out_shape=jax.ShapeDtypeStruct(c.shape, BF16),
                          compiler_params=_params())(c)


def _sum_slots(a, n, *, name, out_dtype=F32):
    _, rows, cols = a.shape
    tr = _tile(rows, 256, 8)

    def body(a_ref, out_ref):
        tot = a_ref[0].astype(F32)
        for j in range(1, n):
            tot = tot + a_ref[j].astype(F32)
        out_ref[...] = tot.astype(out_dtype)

    return pl.pallas_call(
        body, name=name, grid=(rows // tr,),
        in_specs=[pl.BlockSpec((n, tr, cols), lambda i: (0, i, 0))],
        out_specs=pl.BlockSpec((tr, cols), lambda i: (i, 0)),
        out_shape=jax.ShapeDtypeStruct((rows, cols), out_dtype),
        compiler_params=_params(("parallel",)),
    )(a)


def _add2_stacked(a, b, stacked, l, *, name):
    rows, cols = a.shape
    tr = _tile(rows, 256, 8)

    def body(a_ref, b_ref, *rest):
        rest[-1][...] = a_ref[...] + b_ref[...]

    blk = pl.BlockSpec((tr, cols), lambda i: (i, 0))
    carried = [] if stacked is None else [stacked]
    return pl.pallas_call(
        body, name=name, grid=(rows // tr,),
        in_specs=[blk, blk] + [pl.BlockSpec(memory_space=pl.ANY) for _ in carried],
        out_specs=pl.BlockSpec((None, tr, cols), lambda i: (l, i, 0)),
        out_shape=jax.ShapeDtypeStruct((DEPTH, rows, cols), F32),
        input_output_aliases={2: 0} if carried else {},
        compiler_params=_params(("parallel",)),
    )(a, b, *carried)


def _adamw(w, g, m, v, *, name):
    shape = w.shape
    if w.ndim == 2:
        w, g, m, v = (a.reshape((1,) + shape) for a in (w, g, m, v))
    layers, rows, cols = w.shape
    tr = _tile(rows, max(8, (1 << 18) // cols), 8)
    c1 = 1.0 - ADAM_B1 ** ADAM_STEP
    c2 = 1.0 - ADAM_B2 ** ADAM_STEP

    def body(w_ref, g_ref, m_ref, v_ref, d_ref, nm_ref, nv_ref):
        gv = g_ref[...]
        nm = ADAM_B1 * m_ref[...] + (1.0 - ADAM_B1) * gv
        nv = ADAM_B2 * v_ref[...] + (1.0 - ADAM_B2) * (gv * gv)
        nm_ref[...] = nm
        nv_ref[...] = nv
        d_ref[...] = -ADAM_LR * ((nm / c1) / (jnp.sqrt(nv / c2) + ADAM_EPS) + ADAM_WD * w_ref[...])

    blk = pl.BlockSpec((None, tr, cols), lambda l, i: (l, i, 0))
    outs = pl.pallas_call(
        body, name=name, grid=(layers, rows // tr), in_specs=[blk] * 4, out_specs=[blk] * 3,
        out_shape=[jax.ShapeDtypeStruct((layers, rows, cols), F32)] * 3,
        compiler_params=_params(("parallel", "parallel")),
    )(w, g, m, v)
    return [o.reshape(shape) for o in outs]


def _coords():
    return lax.axis_index("x"), lax.axis_index("y"), lax.axis_index("c")


def _other_chips(x, y):
    return [(1 - x, y), (x, 1 - y), (1 - x, 1 - y)]


def _all_gather_small(blk, *, name, swaps=()):
    m_per, n = blk.shape
    n_swaps = len(swaps)

    def body(x_ref, *refs):
        swap_srcs, out_ref, swap_outs = refs[:n_swaps], refs[n_swaps], refs[n_swaps + 1:2 * n_swaps + 1]
        send_sems, recv_sems, local_sem = refs[2 * n_swaps + 1:2 * n_swaps + 4]
        x, y, c = _coords()
        me, sibling = (x, y, c), (x, y, 1 - c)
        chips = _other_chips(x, y)
        swapping = [pltpu.make_async_remote_copy(src_ref=src, dst_ref=dst, send_sem=refs[-2].at[k], recv_sem=refs[-1].at[k],
                                                 device_id=sibling, device_id_type=MESH)
                    for k, (src, dst) in enumerate(zip(swap_srcs, swap_outs))]
        for cp in swapping:
            cp.start()

        def rows(px, py, pc):
            return out_ref.at[pl.ds((4 * px + 2 * py + pc) * m_per, m_per), :]

        def copy(k, block, to, src=None):
            return pltpu.make_async_remote_copy(
                src_ref=rows(*block) if src is None else src, dst_ref=rows(*block),
                send_sem=send_sems.at[k], recv_sem=recv_sems.at[k], device_id=to, device_id_type=MESH)

        mine = pltpu.make_async_copy(x_ref, rows(*me), local_sem)
        mine.start()
        first = [copy(0, me, sibling, src=x_ref)]
        first += [copy(1 + j, me, (*chip, c), src=x_ref) for j, chip in enumerate(chips)]
        for cp in first:
            cp.start()
        passed = [copy(4 + j, (*chip, c), sibling) for j, chip in enumerate(chips)]
        for j, chip in enumerate(chips):
            copy(1 + j, (*chip, c), me).wait_recv()
            passed[j].start()
        copy(0, sibling, me).wait_recv()
        for j, chip in enumerate(chips):
            copy(4 + j, (*chip, 1 - c), me).wait_recv()
        for cp in first + passed:
            cp.wait_send()
        mine.wait()
        for cp in swapping:
            cp.wait()

    any_spec = pl.BlockSpec(memory_space=pl.ANY)
    outs = pl.pallas_call(
        body, name=name,
        out_shape=[jax.ShapeDtypeStruct((N_DEV * m_per, n), blk.dtype)] + [jax.ShapeDtypeStruct(a.shape, a.dtype)
                                                                           for a in swaps],
        in_specs=[pl.BlockSpec(memory_space=pltpu.VMEM)] + [any_spec] * n_swaps,
        out_specs=[pl.BlockSpec(memory_space=pltpu.VMEM)] + [any_spec] * n_swaps,
        scratch_shapes=[pltpu.SemaphoreType.DMA((7,)), pltpu.SemaphoreType.DMA((7,)), pltpu.SemaphoreType.DMA]
        + ([pltpu.SemaphoreType.DMA((n_swaps,)), pltpu.SemaphoreType.DMA((n_swaps,))] if swaps else []),
        compiler_params=_params(),
    )(blk, *swaps)
    return (outs[0], outs[1:]) if swaps else outs[0]


HBM_SPEC = pl.BlockSpec(memory_space=pltpu.HBM)
SEM_SPEC = pl.BlockSpec(memory_space=pltpu.SEMAPHORE)
DATAFLOW = pltpu.SideEffectType.DATAFLOW_SIDE_EFFECTING


def _chip_copies(src_ref, land_ref, send_sems, recv_sems, scatter):
    x, y, c = _coords()
    my = 2 * x + y
    outgoing, incoming = [], []
    for k, (px, py) in enumerate(_other_chips(x, y)):
        peer = 2 * px + py

        def copy(src_slot, dst_slot):
            return pltpu.make_async_remote_copy(
                src_ref=src_ref.at[src_slot] if scatter else src_ref, dst_ref=land_ref.at[dst_slot],
                send_sem=send_sems.at[k], recv_sem=recv_sems.at[k], device_id=(px, py, c), device_id_type=MESH)

        outgoing.append(copy(peer, my))
        incoming.append(copy(my, peer))
    return outgoing, incoming


def _exchange_start(srcs, *, name, scatter):
    n = len(srcs)
    land_shapes = [src.shape if scatter else (N_CHIPS,) + src.shape for src in srcs]

    def body(*refs):
        for k in range(n):
            send_sems, recv_sems = refs[2 * n + 4 * k], refs[2 * n + 4 * k + 1]
            outgoing, _ = _chip_copies(refs[k], refs[n + k], send_sems, recv_sems, scatter)
            for cp in outgoing:
                cp.start()
        refs[-1][...] = jnp.zeros_like(refs[-1])

    out_shape, out_specs, aliases = [], [], {}
    for k, (src, land_shape) in enumerate(zip(srcs, land_shapes)):
        out_shape += [pltpu.SemaphoreType.DMA((N_CHIPS - 1,)), pltpu.SemaphoreType.DMA((N_CHIPS - 1,)),
                      pltpu.HBM(src.shape, src.dtype), pltpu.HBM(land_shape, src.dtype)]
        out_specs += [SEM_SPEC, SEM_SPEC, HBM_SPEC, HBM_SPEC]
        aliases.update({k: 4 * k + 2, n + k: 4 * k + 3})
    outs = pl.pallas_call(
        body, name=name,
        out_shape=tuple(out_shape) + (jax.ShapeDtypeStruct((8, LANES), F32),),
        in_specs=(HBM_SPEC,) * (2 * n),
        out_specs=tuple(out_specs) + (pl.BlockSpec(memory_space=pltpu.VMEM),),
        input_output_aliases=aliases,
        compiler_params=pltpu.CompilerParams(has_side_effects=DATAFLOW),
    )(*[pltpu.with_memory_space_constraint(src, pltpu.HBM) for src in srcs],
      *[pltpu.with_memory_space_constraint(lax.empty(shape, src.dtype), pltpu.HBM)
        for src, shape in zip(srcs, land_shapes)])
    return [tuple(outs[4 * k:4 * k + 4]) for k in range(n)], outs[-1]


def _exchange_wait(started, after, *, name, scatter):
    send_sems, recv_sems, src_thru, land_thru = started

    def body(src_ref, land_ref, send_sems, recv_sems, after_ref, src_dead, got_ref):
        outgoing, incoming = _chip_copies(src_ref, land_ref, send_sems, recv_sems, scatter)
        for cp in outgoing:
            cp.wait_send()
        for cp in incoming:
            cp.wait_recv()

    return pl.pallas_call(
        body, name=name,
        out_shape=(pltpu.HBM(src_thru.shape, src_thru.dtype), pltpu.HBM(land_thru.shape, land_thru.dtype)),
        in_specs=(HBM_SPEC, HBM_SPEC, SEM_SPEC, SEM_SPEC, pl.BlockSpec(memory_space=pl.ANY)),
        out_specs=(HBM_SPEC, HBM_SPEC),
        input_output_aliases={0: 0, 1: 1},
        compiler_params=pltpu.CompilerParams(has_side_effects=DATAFLOW),
    )(src_thru, land_thru, send_sems, recv_sems, after)


def _pack_rows(a):
    return a.reshape(-1, D_MODEL)


def _pad_heads(w, width):
    r = w.shape[0]
    return jnp.pad(w, ((0, 0), (0, 0), (0, HEAD_PAD - width))).reshape(r, N_HEADS * HEAD_PAD)


MIX_NAMES = ("w_uq", "w_uk", "w_uv", "p_pool", "p_attn", "w_out")
GROUPS = ("in", "mix", "ff1", "ff2")


def _local_shard(weights, l, group, zero):
    if group == "mix":
        shard = jnp.concatenate([_pack_rows(weights[n][l]) for n in MIX_NAMES], axis=0)
    else:
        shard = weights[{"in": "w_in", "ff1": "w_ff1", "ff2": "w_ff2"}[group]][l]
    return (shard + zero).astype(BF16)


def _unpack_weights(gathered, group):
    def cols(a, k):
        return a.reshape(N_CHIPS, k, -1).transpose(1, 0, 2).reshape(k, -1)

    if group == "in":
        full = gathered.reshape(W_IN_COLS, D_MODEL)
        u, cq, ckv, kr, gates = (full[a:b] for a, b in (W_IN_U, W_IN_CQ, W_IN_CKV, W_IN_KR, W_IN_GATES))
        kr = jnp.pad(kr, ((QK_NOPE, HEAD_PAD - QK_DIM), (0, 0)))
        return dict(w_in=jnp.concatenate([cq, kr, u, gates, ckv], axis=0))
    if group == "ff1":
        return dict(w_ff1=gathered)
    if group == "ff2":
        return dict(w_ff2=gathered.reshape(D_FF, D_MODEL))

    def p_attn(a):
        full = cols(a, ATTN_DIM).reshape(N_HEADS, V_DIM, D_MODEL)
        return jnp.pad(full, ((0, 0), (0, HEAD_PAD - V_DIM), (0, 0))).reshape(N_HEADS * HEAD_PAD, D_MODEL)

    build = dict(
        w_uq=lambda a: _pad_heads(a.reshape(Q_LORA, N_HEADS, QK_DIM), QK_DIM),
        w_uk=lambda a: _pad_heads(a.reshape(KV_LORA, N_HEADS, QK_NOPE), QK_NOPE),
        w_uv=lambda a: _pad_heads(a.reshape(KV_LORA, N_HEADS, V_DIM), V_DIM),
        p_pool=lambda a: cols(a, POOL_DIM),
        p_attn=p_attn,
        w_out=lambda a: a.reshape(D_MODEL, D_MODEL),
    )
    w, off = {}, 0
    for name in MIX_NAMES:
        w[name] = build[name](gathered[:, off:off + ROWS_OF[name]])
        off += ROWS_OF[name]
    return w


def _pack_grads(g, group):
    def cols(a):
        k = a.shape[0]
        return a.reshape(k, N_CHIPS, -1).transpose(1, 0, 2).reshape(N_CHIPS, -1, D_MODEL)

    def rows(a):
        return a.reshape(N_CHIPS, -1, D_MODEL)

    def heads(width):
        return lambda a: rows(a.reshape(a.shape[0], N_HEADS, HEAD_PAD)[:, :, :width])

    if group == "in":
        full = jnp.concatenate([g["u"], g["cq"], g["ckv"], g["kr"][QK_NOPE:QK_DIM], g["ga"], g["gb"]], axis=0)
        return full.reshape(N_CHIPS, W_IN_SHARD, D_MODEL)
    if group == "ff1":
        return g["w_ff1"]
    if group == "ff2":
        return g["w_ff2"].reshape(N_CHIPS, D_FF // N_CHIPS, D_MODEL)

    def p_attn(a):
        return cols(a.reshape(N_HEADS, HEAD_PAD, D_MODEL)[:, :V_DIM].reshape(ATTN_DIM, D_MODEL))

    build = dict(w_uq=heads(QK_DIM), w_uk=heads(QK_NOPE), w_uv=heads(V_DIM), p_pool=cols, p_attn=p_attn, w_out=rows)
    return jnp.concatenate([build[name](g[name]) for name in MIX_NAMES], axis=1)


def _per_head(fn, acc, *tables):
    return jnp.concatenate([fn(acc[:, h * HEAD_PAD:(h + 1) * HEAD_PAD], *tables) for h in range(N_HEADS)], axis=1)


def _rope_head(a, cos, sin):
    lane = lax.broadcasted_iota(jnp.int32, a.shape, 1)
    return a * (cos + jnp.where(lane < QK_NOPE, 1.0, 0.0)) + _rotate_half(a) * sin


def _layer_fwd(l, x, mod, get_weights, small, cos_t, sin_t):
    sh1, sc1, g1, sh2, sc2, g2 = mod
    tag = f"_l{l}"
    h, r1 = _norm_mod(x, small["ln1_g"], sc1, sh1, name="norm1" + tag)
    w = dict(get_weights("in", h))
    (z,) = _mm(h, w["w_in"], tb=True, name="in_proj" + tag, out_dtypes=(BF16,))
    p, yp, cq, ckv, kr, rq, rkv = _mixer_pre(z, cos_t, sin_t, small["w_pool"], small["pool_scale"],
                                              small["q_norm_g"], small["kv_norm_g"], name="mixer_pre" + tag)
    w.update(get_weights("mix", yp))
    (ya,) = _mm(yp, w["p_pool"], name="pool_out" + tag, out_dtypes=(BF16,))
    (q,) = _mm(cq, w["w_uq"], name="q_proj" + tag, out_dtypes=(BF16,),
               epilogue=lambda acc, cos, sin: (_per_head(_rope_head, acc, cos, sin),),
               extras=((cos_t, "table"), (sin_t, "table")))
    (k,) = _mm(ckv, w["w_uk"], name="k_proj" + tag, out_dtypes=(BF16,),
               epilogue=lambda acc, krv: (_per_head(lambda a, b: a + b, acc, krv),), extras=((kr, "table"),))
    (v,) = _mm(ckv, w["w_uv"], name="v_proj" + tag, out_dtypes=(BF16,))
    o, lse = _attn_fwd(q, k, v, name="attn_fwd" + tag)
    yb, merged = _mm(o, w["p_attn"], name="attn_out" + tag, out_dtypes=(BF16, BF16), tm=512,
                     epilogue=lambda acc, ga, gb, yav: (acc, _sigmoid(ga) * yav + _sigmoid(gb) * acc),
                     extras=((z, ("tile", ZC_GA // D_MODEL)), (z, ("tile", ZC_GB // D_MODEL)), (ya, "tile")))
    mo, x1 = _mm(merged, w["w_out"], name="mix_out" + tag, out_dtypes=(BF16, F32),
                 epilogue=lambda acc, xr, g: (acc, xr + g * acc), extras=((x, "tile"), (g1, "row")))
    h2, r2 = _norm_mod(x1, small["ln2_g"], sc2, sh2, name="norm2" + tag)
    w.update(get_weights("ff1", merged))
    f, act = _mm(h2, w["w_ff1"], b_stack=True, name="ff1" + tag, out_dtypes=(BF16, BF16),
                 epilogue=lambda acc: (acc, jnp.square(jnp.maximum(acc, 0.0))))
    w.update(get_weights("ff2", act))
    m2, x2 = _mm(act, w["w_ff2"], name="ff2" + tag, out_dtypes=(BF16, F32),
                 epilogue=lambda acc, xr, g: (acc, xr + g * acc), extras=((x1, "tile"), (g2, "row")))
    saved = dict(x=x, h=h, r1=r1, z=z, p=p, yp=yp, cq=cq, ckv=ckv, rq=rq, rkv=rkv, ya=ya, q=q, k=k, v=v, o=o, lse=lse,
                 yb=yb, merged=merged, mo=mo, x1=x1, h2=h2, r2=r2, f=f, act=act, m2=m2)
    return x2, saved, w


def _merge_grads(dm, ga, gb, ya, yb):
    sa, sb = _sigmoid(ga), _sigmoid(gb)
    return dm * sa, dm * sb, dm * ya * (sa * (1.0 - sa)), dm * yb * (sb * (1.0 - sb))


def _layer_bwd(l, dx2, dm2, dg2, sv, mod, w, small, cos_t, sin_t, send_grads, gate_below):
    sh1, sc1, g1, sh2, sc2, g2 = mod
    tag = f"_l{l}"
    gw = {}
    (df,) = _mm(dm2, w["w_ff2"], tb=True, name="ff2_dx" + tag, out_dtypes=(BF16,),
                epilogue=lambda acc, f: (acc * (2.0 * jnp.maximum(f, 0.0)),), extras=((sv["f"], "tile"),))
    (g_ff2,) = _mm(sv["act"], dm2, ta=True, name="ff2_dw" + tag, out_dtypes=(BF16,))
    (g_ff1,) = _mm(sv["h2"], df, ta=True, out_stack=N_CHIPS, name="ff1_dw" + tag, out_dtypes=(BF16,))
    sc2 = sc2 + send_grads("ff2", dict(w_ff2=g_ff2)) + send_grads("ff1", dict(w_ff1=g_ff1))
    (dh2,) = _mm(df, w["w_ff1"], tb=True, b_stack=True, name="ff1_dx" + tag)
    dx1, dln2, dsc2, dsh2, dmo, dg1 = _norm_mod_bwd(dh2, sv["x1"], sv["r2"], small["ln2_g"], sc2, dx2,
                                                    gate=(sv["mo"], g1), name="norm2_bwd" + tag)
    dya, dyb, dga, dgb = _mm(dmo, w["w_out"], tb=True, name="mix_out_dx" + tag, out_dtypes=(BF16,) * 4, tm=512,
                             epilogue=_merge_grads,
                             extras=((sv["z"], ("tile", ZC_GA // D_MODEL)), (sv["z"], ("tile", ZC_GB // D_MODEL)),
                                     (sv["ya"], "tile"), (sv["yb"], "tile")))
    (gw["w_out"],) = _mm(sv["merged"], dmo, ta=True, name="mix_out_dw" + tag, out_dtypes=(BF16,))
    (gw["p_pool"],) = _mm(sv["yp"], dya, ta=True, name="pool_out_dw" + tag, out_dtypes=(BF16,))
    (dyp,) = _mm(dya, w["p_pool"], tb=True, name="pool_out_dx" + tag)
    du, g_w_pool, g_pool_scale = _pool_bwd(dyp, sv["p"], small["w_pool"], small["pool_scale"], name="pool_bwd" + tag)
    (gw["p_attn"],) = _mm(sv["o"], dyb, ta=True, name="attn_out_dw" + tag, out_dtypes=(BF16,))
    (do,) = _mm(dyb, w["p_attn"], tb=True, name="attn_out_dx" + tag, out_dtypes=(BF16,))
    dql, dkb, dv, dk_sums = _attn_bwd(sv["q"], sv["k"], sv["v"], do, sv["o"], sv["lse"], cos_t, sin_t,
                                      name="attn_bwd" + tag)
    dkr = _key_bwd(dk_sums, cos_t, sin_t, name="key_bwd" + tag)
    (gw["w_uq"],) = _mm(sv["cq"], dql, ta=True, name="q_proj_dw" + tag, out_dtypes=(BF16,))
    (gw["w_uk"],) = _mm(sv["ckv"], dkb, ta=True, name="k_proj_dw" + tag, out_dtypes=(BF16,))
    (gw["w_uv"],) = _mm(sv["ckv"], dv, ta=True, name="v_proj_dw" + tag, out_dtypes=(BF16,))
    (dcq,) = _mm(dql, w["w_uq"], tb=True, name="q_proj_dx" + tag)
    (dckv,) = _mm(dkb, w["w_uk"], tb=True, second=(dv, w["w_uv"]), name="kv_proj_dx" + tag)
    q_norm_g = small["q_norm_g"] + send_grads("mix", gw)
    dcq_raw, g_qn = _rms_bwd(dcq, sv["z"], ZC_CQ, sv["rq"], q_norm_g, name="q_norm_bwd" + tag)
    dckv_raw, g_kvn = _rms_bwd(dckv, sv["z"], ZC_CKV, sv["rkv"], small["kv_norm_g"], name="kv_norm_bwd" + tag)
    dz = dict(cq=dcq_raw, kr=dkr, u=du, ga=dga, gb=dgb, ckv=dckv_raw)
    g_in = {n: _mm(piece, sv["h"], ta=True, name=f"in_proj_dw_{n}" + tag, out_dtypes=(BF16,))[0]
            for n, piece in dz.items()}
    sc1 = sc1 + send_grads("in", g_in)
    dh = _mm_sum(list(dz.values()), w["w_in"], [Z_OFFSETS[n] for n in dz], name="in_proj_dx" + tag)
    dx, dln1, dsc1, dsh1, *below = _norm_mod_bwd(dh, sv["x"], sv["r1"], small["ln1_g"], sc1, dx1, gate=gate_below,
                                                 name="norm1_bwd" + tag)
    dmod = jnp.concatenate([dsh1, dsc1, dg1, dsh2, dsc2, dg2], axis=0)
    gsmall = dict(ln1_g=dln1, ln2_g=dln2, q_norm_g=g_qn, kv_norm_g=g_kvn, w_pool=g_w_pool, pool_scale=g_pool_scale)
    return dx, dmod, gsmall, below


SMALL_LOSS = 6
SMALL_SINGLES = 16
SMALL_POOL = 24
SMALL_POOL_ROWS = len(POOL_WINDOWS) * POOL_GROUP * POOL_GROUP // D_MODEL
SMALL_ROWS = SMALL_POOL + DEPTH * SMALL_POOL_ROWS


def _pack_small(parts, *, name):
    def body(*refs):
        out_ref = refs[-1]
        out_ref[...] = jnp.zeros_like(out_ref)
        for ref, (_, row) in zip(refs[:-1], parts):
            out_ref[row:row + ref.shape[0], :] = ref[...]

    return pl.pallas_call(body, name=name, out_shape=jax.ShapeDtypeStruct((SMALL_ROWS, D_MODEL), F32),
                          compiler_params=_params())(*[a for a, _ in parts])


def kernel(x, c, positions, ln1_g, ln2_g, w_ada, b_ada, w_in, q_norm_g, w_uq, kv_norm_g, w_uk, w_uv, w_pool, pool_scale, p_pool, p_attn, w_out, w_ff1, w_ff2, final_g, loss_target, m_ln1_g, m_ln2_g, m_w_ada, m_b_ada, m_w_in, m_q_norm_g, m_w_uq, m_kv_norm_g, m_w_uk, m_w_uv, m_w_pool, m_pool_scale, m_p_pool, m_p_attn, m_w_out, m_w_ff1, m_w_ff2, m_final_g, v_ln1_g, v_ln2_g, v_w_ada, v_b_ada, v_w_in, v_q_norm_g, v_w_uq, v_kv_norm_g, v_w_uk, v_w_uv, v_w_pool, v_pool_scale, v_p_pool, v_p_attn, v_w_out, v_w_ff1, v_w_ff2, v_final_g):
    weights = dict(ln1_g=ln1_g, ln2_g=ln2_g, w_ada=w_ada, b_ada=b_ada, w_in=w_in, q_norm_g=q_norm_g, w_uq=w_uq,
                   kv_norm_g=kv_norm_g, w_uk=w_uk, w_uv=w_uv, w_pool=w_pool, pool_scale=pool_scale, p_pool=p_pool,
                   p_attn=p_attn, w_out=w_out, w_ff1=w_ff1, w_ff2=w_ff2, final_g=final_g)
    moms = dict(ln1_g=m_ln1_g, ln2_g=m_ln2_g, w_ada=m_w_ada, b_ada=m_b_ada, w_in=m_w_in, q_norm_g=m_q_norm_g,
                w_uq=m_w_uq, kv_norm_g=m_kv_norm_g, w_uk=m_w_uk, w_uv=m_w_uv, w_pool=m_w_pool,
                pool_scale=m_pool_scale, p_pool=m_p_pool, p_attn=m_p_attn, w_out=m_w_out, w_ff1=m_w_ff1,
                w_ff2=m_w_ff2, final_g=m_final_g)
    vels = dict(ln1_g=v_ln1_g, ln2_g=v_ln2_g, w_ada=v_w_ada, b_ada=v_b_ada, w_in=v_w_in, q_norm_g=v_q_norm_g,
                w_uq=v_w_uq, kv_norm_g=v_kv_norm_g, w_uk=v_w_uk, w_uv=v_w_uv, w_pool=v_w_pool,
                pool_scale=v_pool_scale, p_pool=v_p_pool, p_attn=v_p_attn, w_out=v_w_out, w_ff1=v_w_ff1,
                w_ff2=v_w_ff2, final_g=v_final_g)
    order = list(weights)
    for table in (weights, moms, vels):
        table["w_in"] = jnp.swapaxes(table["w_in"], 1, 2)
    seq = x.shape[1]
    my_chip = 2 * lax.axis_index("x") + lax.axis_index("y")
    my_dev = 2 * my_chip + lax.axis_index("c")
    ada_cols = w_ada.shape[2]

    small = [dict(ln1_g=ln1_g[l:l + 1], ln2_g=ln2_g[l:l + 1], q_norm_g=q_norm_g[l:l + 1], kv_norm_g=kv_norm_g[l:l + 1],
                  w_pool=w_pool[l], pool_scale=pool_scale[l:l + 1]) for l in range(DEPTH)]

    c_all = _all_gather_small(jnp.pad(c, ((0, 7), (0, 0))), name="cond_all_gather")
    c_act = _silu(c_all, name="cond_silu")
    b_mine = lax.dynamic_slice_in_dim(b_ada, my_chip * ada_cols, ada_cols, axis=1).reshape(1, DEPTH * ada_cols)
    (mod_cat,) = _mm(c_act, w_ada, b_stack=True, name="ada_fwd", epilogue=lambda acc, b: (acc + b,),
                     extras=((b_mine, "row"),))
    mod_mine = jnp.concatenate([mod_cat[::8, l * ada_cols:(l + 1) * ada_cols] for l in range(DEPTH)], axis=0)
    mod_all = _all_gather_small(mod_mine, name="mod_all_gather").reshape(N_DEV, DEPTH, N_DEV, ada_cols)

    zero = mod_all[0, 0, 0, 0] * 0.0
    keys = [(l, group) for l in range(DEPTH) for group in GROUPS]
    exchanges, token = _exchange_start([_local_shard(weights, l, group, zero) for l, group in keys],
                                       name="weights_send", scatter=False)
    started = dict(zip(keys, exchanges))
    pin = token[0:1, 0:1]

    def gathered_weights(l, group, after):
        mine, land = _exchange_wait(started[l, group], after, name=f"weights_wait_l{l}_{group}", scatter=False)
        land = lax.dynamic_update_slice_in_dim(land, mine[None], my_chip, axis=0)
        return _unpack_weights(land, group)

    mods = []
    for l in range(DEPTH):
        row = jnp.concatenate([lax.dynamic_index_in_dim(mod_all[2 * j, l], my_dev, axis=0, keepdims=True)
                               for j in range(N_CHIPS)], axis=1) + pin
        mods.append([row[:, i * D_MODEL:(i + 1) * D_MODEL] for i in range(N_MOD)])

    inv_freq = ROPE_THETA ** (-jnp.arange(0, QK_ROPE, 2, dtype=F32) / QK_ROPE)
    freq_lanes = jnp.concatenate([jnp.zeros((QK_NOPE,), F32), inv_freq, inv_freq,
                                  jnp.zeros((HEAD_PAD - QK_DIM,), F32)]).reshape(1, LANES)
    cos_t, sin_t = _rope_tables(positions.reshape(seq, 1), freq_lanes, name="rope_tables")

    xs, saved, wl = x.reshape(seq, D_MODEL), [], []
    for l in range(DEPTH):
        xs, sv, w_l = _layer_fwd(l, xs, mods[l], functools.partial(gathered_weights, l), small[l], cos_t, sin_t)
        saved.append(sv)
        wl.append(w_l)
    dx, loss_part, g_final, dm2, dg2 = _final_loss(xs, final_g.reshape(1, D_MODEL), loss_target.reshape(seq, D_MODEL),
                                                   saved[-1]["m2"], mods[-1][5], name="final_loss")

    sent, pending = [], []
    send_after = {(0, "ff1"), (0, "mix"), (0, "in")}

    def send_grads(l, group, g):
        pending.append((l, group, _pack_grads(g, group)))
        if (l, group) not in send_after:
            return jnp.zeros((1, 1), F32)
        exchanges, token_g = _exchange_start([gpack for _, _, gpack in pending], name=f"grads_send_l{l}_{group}",
                                             scatter=True)
        sent.extend((item[0], item[1], exchange) for item, exchange in zip(pending, exchanges))
        pending.clear()
        return token_g[0:1, 0:1]

    dmod, gsmall = [None] * DEPTH, [None] * DEPTH
    for l in reversed(range(DEPTH)):
        gate_below = (saved[l - 1]["m2"], mods[l - 1][5]) if l > 0 else None
        dx, dmod[l], gsmall[l], below = _layer_bwd(l, dx, dm2, dg2, saved[l], mods[l], wl[l], small[l], cos_t, sin_t,
                                                   functools.partial(send_grads, l), gate_below)
        dm2, dg2 = below if below else (None, None)
    grads = dict(x=dx.reshape(1, seq, D_MODEL))

    big_parts, after = [], dmod[0]
    for l, group, started_g in sent:
        tg = f"_l{l}_{group}"
        gpack, land = _exchange_wait(started_g, after, name="grads_wait" + tg, scatter=True)
        own = lax.dynamic_index_in_dim(gpack, my_chip, axis=0, keepdims=True)
        land = lax.dynamic_update_slice_in_dim(land, own, my_chip, axis=0)
        big_parts.append(_sum_slots(land, N_CHIPS, name="grads_sum_chips" + tg))
        after = big_parts[-1]

    def lanes(a):
        flat = a.reshape(1, -1)
        return jnp.pad(flat, ((0, 0), (0, D_MODEL - flat.shape[1])))

    singles = [gsmall[0]["ln1_g"], gsmall[1]["ln1_g"], gsmall[0]["ln2_g"], gsmall[1]["ln2_g"], g_final,
               lanes(jnp.concatenate([gsmall[l]["pool_scale"] for l in range(DEPTH)], axis=1)),
               lanes(jnp.concatenate([gsmall[l]["q_norm_g"] for l in range(DEPTH)], axis=1)),
               lanes(jnp.concatenate([gsmall[l]["kv_norm_g"] for l in range(DEPTH)], axis=1))]
    parts = [(dmod[0], 0), (lanes(loss_part), SMALL_LOSS), (dmod[1], 8)]
    parts += [(a, SMALL_SINGLES + i) for i, a in enumerate(singles)]
    parts += [(gsmall[l]["w_pool"].reshape(-1, D_MODEL), SMALL_POOL + l * SMALL_POOL_ROWS) for l in range(DEPTH)]
    small_all, big_others = _all_gather_small(_pack_small(parts, name="small_grads_pack"), swaps=big_parts,
                                              name="small_grads_all_gather")
    small_all = small_all.reshape(N_DEV, SMALL_ROWS, D_MODEL)
    ssum = _sum_slots(small_all, N_DEV, name="small_grads_sum")
    loss = ssum[SMALL_LOSS, 0]
    grads["b_ada"] = jnp.stack([ssum[8 * l:8 * l + N_MOD] for l in range(DEPTH)]).reshape(DEPTH, N_MOD * D_MODEL)
    grads["ln1_g"] = ssum[SMALL_SINGLES:SMALL_SINGLES + 2]
    grads["ln2_g"] = ssum[SMALL_SINGLES + 2:SMALL_SINGLES + 4]
    grads["final_g"] = ssum[SMALL_SINGLES + 4]
    grads["pool_scale"] = ssum[SMALL_SINGLES + 5].reshape(DEPTH, POOL_DIM)
    grads["q_norm_g"] = ssum[SMALL_SINGLES + 6, :DEPTH * Q_LORA].reshape(DEPTH, Q_LORA)
    grads["kv_norm_g"] = ssum[SMALL_SINGLES + 7, :DEPTH * KV_LORA].reshape(DEPTH, KV_LORA)
    grads["w_pool"] = ssum[SMALL_POOL:SMALL_ROWS].reshape(w_pool.shape)

    gsum = {}
    for (l, group, _), part, other in zip(sent, big_parts, big_others):
        gsum[group] = _add2_stacked(part, other, gsum.get(group), l, name=f"grads_sum_cores_l{l}_{group}")
    grads.update(w_in=gsum["in"], w_ff1=gsum["ff1"], w_ff2=gsum["ff2"])
    off = 0
    for name in MIX_NAMES:
        grads[name] = gsum["mix"][:, off:off + ROWS_OF[name]].reshape(weights[name].shape)
        off += ROWS_OF[name]

    c_act_t = jnp.pad(c_act[::8].T, ((0, 0), (0, LANES - N_DEV)))
    d_mine = []
    for l in range(DEPTH):
        d_all = small_all[:, 8 * l:8 * l + N_MOD].reshape(N_DEV, N_MOD * D_MODEL)
        d_mine.append(lax.dynamic_slice_in_dim(d_all, my_chip * ada_cols, ada_cols, axis=1))
    d_cat = jnp.pad(jnp.concatenate(d_mine, axis=1), ((0, LANES - N_DEV), (0, 0)))
    (grads["w_ada"],) = _mm(c_act_t, d_cat, out_stack=DEPTH, name="ada_dw")

    def view(a):
        return a.reshape(1, -1) if a.ndim == 1 else a if a.ndim == 3 else a.reshape(-1, a.shape[-1])

    delta, new_m, new_v = {}, {}, {}
    for name in order:
        shape = weights[name].shape
        d, nm, nv = _adamw(view(weights[name]), view(grads[name]), view(moms[name]), view(vels[name]),
                           name="adamw_" + name)
        delta[name], new_m[name], new_v[name] = d.reshape(shape), nm.reshape(shape), nv.reshape(shape)
    for table in (grads, delta, new_m, new_v):
        table["w_in"] = jnp.swapaxes(table["w_in"], 1, 2)
    return (loss, grads["x"], *[grads[n] for n in order], *[delta[n] for n in order],
            *[new_m[n] for n in order], *[new_v[n] for n in order])
```

```python
import functools
import math

import jax
import jax.numpy as jnp
from jax import lax
from jax.experimental import pallas as pl
from jax.experimental.pallas import tpu as pltpu

F32 = jnp.float32
BF16 = jnp.bfloat16
MESH = pl.DeviceIdType.MESH

D_MODEL = 1024
DEPTH = 2
POOL_WINDOWS = (2, 4, 8, 16)
POOL_GROUP = 128
POOL_DIM = 512
N_HEADS = 8
QK_NOPE = 64
QK_ROPE = 32
QK_DIM = QK_NOPE + QK_ROPE
V_DIM = 64
HEAD_PAD = 128
Q_LORA = 384
KV_LORA = 256
ROPE_THETA = 10000.0
ATTN_DIM = N_HEADS * V_DIM
D_FF = 4 * D_MODEL
N_MOD = 6
EPS = 1e-6
N_CHIPS = 4
N_DEV = 8

ADAM_LR = 0.001
ADAM_B1 = 0.9
ADAM_B2 = 0.999
ADAM_EPS = 1e-08
ADAM_WD = 0.01
ADAM_STEP = 10

VMEM_LIMIT_BYTES = 56 * 1024 * 1024
LANES = 128
HALO = 16

ZC_CQ = 0
ZC_KR = 384
ZC_U = 512
ZC_GA = 1024
ZC_GB = 2048
ZC_CKV = 3072
Z_DIM = 3328
Z_OFFSETS = dict(cq=ZC_CQ, kr=ZC_KR, u=ZC_U, ga=ZC_GA, gb=ZC_GB, ckv=ZC_CKV)

W_IN_U, W_IN_CQ, W_IN_CKV, W_IN_KR, W_IN_GATES = (0, 512), (512, 896), (896, 1152), (1152, 1184), (1184, 3232)
W_IN_COLS = W_IN_GATES[1]
W_IN_SHARD = W_IN_COLS // N_CHIPS

ROWS_OF = dict(w_uq=72, w_uk=32, w_uv=32, p_pool=128, p_attn=128, w_out=256)


def _params(sem=None, **kw):
    return pltpu.CompilerParams(dimension_semantics=sem, vmem_limit_bytes=VMEM_LIMIT_BYTES, **kw)


def _tile(n, target, unit=LANES):
    best = None
    for t in range(unit, min(n, target) + 1, unit):
        if n % t == 0:
            best = t
    return best if best is not None and 4 * best >= min(n, target) else n


def _near_tile(n, target):
    cands = [t for t in range(LANES, n + 1, LANES) if n % t == 0]
    return min(cands, key=lambda t: abs(math.log(t / target))) if cands else n


def _mm(a, b, *, name, ta=False, tb=False, out_dtypes=(F32,), epilogue=None, extras=(), tm=1024, tn=1024, tk=1024,
        second=None, b_stack=False, out_stack=None):
    (k_dim, m_dim) = a.shape if ta else a.shape[::-1]
    if b_stack:
        g_b, k_b, n_shard = b.shape
        n_dim, k_b = (k_b, g_b * n_shard) if tb else (g_b * n_shard, k_b)
    else:
        (n_dim, k_b) = b.shape if tb else b.shape[::-1]
    assert k_dim == k_b, (a.shape, b.shape)
    n_unit = n_shard if b_stack and not tb else n_dim // out_stack if out_stack else n_dim
    k_unit = n_shard if b_stack and tb else k_dim
    tm, tn, tk = _near_tile(m_dim, tm), _near_tile(n_unit, tn), _near_tile(k_unit, tk)
    nk = k_dim // tk
    n_extra, n_out = len(extras), len(out_dtypes)
    n_lhs = 4 if second else 2
    dims = (((0 if ta else 1,), (1 if tb else 0,)), ((), ()))
    if epilogue is None:
        epilogue = lambda acc: (acc,) * n_out

    def body(*refs):
        operand_refs, rest = refs[:n_lhs], refs[n_lhs:]
        extra_refs, out_refs = rest[:n_extra], rest[n_extra:n_extra + n_out]

        def product():
            total = None
            for a_ref, b_ref in zip(operand_refs[0::2], operand_refs[1::2]):
                part = lax.dot_general(a_ref[...].astype(BF16), b_ref[...].astype(BF16), dims, preferred_element_type=F32)
                total = part if total is None else total + part
            return total

        def finish(acc):
            outs = epilogue(acc, *[r[...] for r in extra_refs])
            for o_ref, o in zip(out_refs, outs):
                o_ref[...] = o.astype(o_ref.dtype)

        if nk == 1:
            finish(product())
            return
        acc_ref = rest[-1]
        k = pl.program_id(2)

        @pl.when(k == 0)
        def _():
            acc_ref[...] = product()

        @pl.when((k > 0) & (k < nk - 1))
        def _():
            acc_ref[...] += product()

        @pl.when(k == nk - 1)
        def _():
            finish(acc_ref[...] + product())

    a_spec = pl.BlockSpec((tk, tm), lambda i, j, k: (k, i)) if ta else pl.BlockSpec((tm, tk), lambda i, j, k: (i, k))
    if b_stack and tb:
        per = n_shard // tk
        b_spec = pl.BlockSpec((None, tn, tk), lambda i, j, k: (k // per, j, k % per))
    elif b_stack:
        per = n_shard // tn
        b_spec = pl.BlockSpec((None, tk, tn), lambda i, j, k: (j // per, k, j % per))
    elif tb:
        b_spec = pl.BlockSpec((tn, tk), lambda i, j, k: (j, k))
    else:
        b_spec = pl.BlockSpec((tk, tn), lambda i, j, k: (k, j))
    if out_stack:
        per_out = (n_dim // out_stack) // tn
        out_spec = pl.BlockSpec((None, tm, tn), lambda i, j, k: (j // per_out, i, j % per_out))
        out_dims = (out_stack, m_dim, n_dim // out_stack)
    else:
        out_spec = pl.BlockSpec((tm, tn), lambda i, j, k: (i, j))
        out_dims = (m_dim, n_dim)
    extra_specs = []
    for arr, kind in extras:
        if kind == "tile":
            extra_specs.append(pl.BlockSpec((tm, tn), lambda i, j, k: (i, j)))
        elif isinstance(kind, tuple):
            extra_specs.append(pl.BlockSpec((tm, tn), functools.partial(lambda i, j, k, c: (i, j + c), c=kind[1])))
        elif kind == "row":
            extra_specs.append(pl.BlockSpec((1, tn), lambda i, j, k: (0, j)))
        elif kind == "col":
            extra_specs.append(pl.BlockSpec((tm, 1), lambda i, j, k: (i, 0)))
        else:
            assert kind == "table", kind
            extra_specs.append(pl.BlockSpec((tm, LANES), lambda i, j, k: (i, 0)))
    return pl.pallas_call(
        body,
        name=name,
        grid=(m_dim // tm, n_dim // tn, nk),
        in_specs=[a_spec, b_spec] * (n_lhs // 2) + extra_specs,
        out_specs=[out_spec for _ in out_dtypes],
        out_shape=[jax.ShapeDtypeStruct(out_dims, dt) for dt in out_dtypes],
        scratch_shapes=[pltpu.VMEM((tm, tn), F32)] if nk > 1 else [],
        compiler_params=_params(("parallel", "parallel", "arbitrary")),
    )(a, b, *(second or ()), *[arr for arr, _ in extras])


def _mm_sum(pieces, b, offsets, *, name, tm=1024, tn=1024):
    m_dim, n_dim = pieces[0].shape[0], b.shape[1]
    tm, tn = _near_tile(m_dim, tm), _near_tile(n_dim, tn)
    n_pieces = len(pieces)

    def body(*refs):
        total = None
        for a_ref, b_ref in zip(refs[:n_pieces], refs[n_pieces:2 * n_pieces]):
            part = jnp.dot(a_ref[...], b_ref[...], preferred_element_type=F32)
            total = part if total is None else total + part
        refs[-1][...] = total

    a_specs = [pl.BlockSpec((tm, p.shape[1]), lambda i, j: (i, 0)) for p in pieces]
    b_specs = [pl.BlockSpec((p.shape[1], tn), functools.partial(lambda i, j, blk: (blk, j), blk=off // p.shape[1]))
               for p, off in zip(pieces, offsets)]
    return pl.pallas_call(
        body, name=name, grid=(m_dim // tm, n_dim // tn),
        in_specs=a_specs + b_specs,
        out_specs=pl.BlockSpec((tm, tn), lambda i, j: (i, j)),
        out_shape=jax.ShapeDtypeStruct((m_dim, n_dim), F32),
        compiler_params=_params(("parallel", "parallel")),
    )(*pieces, *[b] * n_pieces)


def _rows(s):
    return min(512, s)


def _rope_tables(pos_col, inv_freq_lanes, *, name):
    s = pos_col.shape[0]
    tb = _rows(s)

    def body(pos_ref, f_ref, cos_ref, sin_ref):
        ang = pos_ref[...].astype(F32) * f_ref[...]
        lane = lax.broadcasted_iota(jnp.int32, ang.shape, 1)
        on = (lane >= QK_NOPE) & (lane < QK_DIM)
        cos_ref[...] = jnp.where(on, jnp.cos(ang), 0.0)
        sin_ref[...] = jnp.where(on, jnp.sin(ang), 0.0)

    return pl.pallas_call(
        body, name=name, grid=(s // tb,),
        in_specs=[pl.BlockSpec((tb, 1), lambda i: (i, 0)), pl.BlockSpec((1, LANES), lambda i: (0, 0))],
        out_specs=[pl.BlockSpec((tb, LANES), lambda i: (i, 0))] * 2,
        out_shape=[jax.ShapeDtypeStruct((s, LANES), F32)] * 2,
        compiler_params=_params(("parallel",)),
    )(pos_col, inv_freq_lanes)


def _rotate_half(x):
    lane = lax.broadcasted_iota(jnp.int32, x.shape, 1)
    half = QK_ROPE // 2
    first = (lane >= QK_NOPE) & (lane < QK_NOPE + half)
    second = (lane >= QK_NOPE + half) & (lane < QK_DIM)
    return jnp.where(first, -pltpu.roll(x, LANES - half, 1), jnp.where(second, pltpu.roll(x, half, 1), 0.0))


def _norm_mod(x, g, sc, sh, *, name):
    s, d = x.shape
    tb = _rows(s)

    def body(x_ref, g_ref, sc_ref, sh_ref, h_ref, r_ref):
        xv = x_ref[...]
        r = lax.rsqrt(jnp.mean(xv * xv, axis=-1, keepdims=True) + EPS)
        r_ref[...] = r
        h_ref[...] = (((xv * r) * g_ref[...]) * (1.0 + sc_ref[...]) + sh_ref[...]).astype(BF16)

    vec = pl.BlockSpec((1, d), lambda i: (0, 0))
    return pl.pallas_call(
        body, name=name, grid=(s // tb,),
        in_specs=[pl.BlockSpec((tb, d), lambda i: (i, 0)), vec, vec, vec],
        out_specs=[pl.BlockSpec((tb, d), lambda i: (i, 0)), pl.BlockSpec((tb, 1), lambda i: (i, 0))],
        out_shape=[jax.ShapeDtypeStruct((s, d), BF16), jax.ShapeDtypeStruct((s, 1), F32)],
        compiler_params=_params(("parallel",)),
    )(x, g, sc, sh)


def _window_sums(ext, sign):
    n = ext.shape[0]
    sums, cur, k = [], ext, 1
    for _ in POOL_WINDOWS:
        cur = cur + pltpu.roll(cur, k if sign > 0 else n - k, 0)
        sums.append(cur)
        k *= 2
    return sums


def _mixer_pre(z, cos_t, sin_t, w_pool, pool_scale, gq, gkv, *, name):
    s = z.shape[0]
    tb = _rows(s)
    hb = tb // HALO

    def body(zcq_ref, zkr_ref, zu_ref, zuh_ref, zckv_ref, cos_ref, sin_ref, wp_ref, ps_ref, gq_ref, gkv_ref,
             p_ref, yp_ref, cq_ref, ckv_ref, kr_ref, rq_ref, rkv_ref):
        i = pl.program_id(0)
        u = zu_ref[...].astype(F32)
        halo = jnp.where(i > 0, zuh_ref[...].astype(F32), 0.0)
        ext = jnp.concatenate([halo, u], axis=0)
        t = i * tb + lax.broadcasted_iota(jnp.int32, (tb, 1), 0)
        for g, (w, sw) in enumerate(zip(POOL_WINDOWS, _window_sums(ext, +1))):
            cols = slice(g * POOL_GROUP, (g + 1) * POOL_GROUP)
            cnt = jnp.minimum(t + 1, w).astype(F32)
            pg = (sw[HALO:, cols] / cnt - u[:, cols]).astype(BF16)
            p_ref[:, cols] = pg
            yg = jnp.dot(pg, wp_ref[g].astype(BF16), preferred_element_type=F32)
            yp_ref[:, cols] = (yg * ps_ref[:, cols]).astype(BF16)

        def rms(x_ref, g_ref, out_ref, r_ref):
            xv = x_ref[...].astype(F32)
            r = lax.rsqrt(jnp.mean(xv * xv, axis=-1, keepdims=True) + EPS)
            r_ref[...] = r
            out_ref[...] = ((xv * r) * g_ref[...]).astype(BF16)

        rms(zcq_ref, gq_ref, cq_ref, rq_ref)
        rms(zckv_ref, gkv_ref, ckv_ref, rkv_ref)
        kr = zkr_ref[...].astype(F32)
        kr_ref[...] = (kr * cos_ref[...] + _rotate_half(kr) * sin_ref[...]).astype(BF16)

    def zcol(width, off):
        return pl.BlockSpec((tb, width), lambda i: (i, off // width))

    def full(a):
        return pl.BlockSpec(a.shape, lambda i: (0,) * a.ndim)

    def out(width, dt):
        return pl.BlockSpec((tb, width), lambda i: (i, 0)), jax.ShapeDtypeStruct((s, width), dt)

    outs = [out(POOL_DIM, BF16), out(POOL_DIM, BF16), out(Q_LORA, BF16), out(KV_LORA, BF16), out(LANES, BF16),
            out(1, F32), out(1, F32)]
    return pl.pallas_call(
        body, name=name, grid=(s // tb,),
        in_specs=[zcol(Q_LORA, ZC_CQ), zcol(LANES, ZC_KR), zcol(POOL_DIM, ZC_U),
                  pl.BlockSpec((HALO, POOL_DIM), lambda i: (jnp.maximum(i * hb - 1, 0), ZC_U // POOL_DIM)),
                  zcol(KV_LORA, ZC_CKV),
                  pl.BlockSpec((tb, LANES), lambda i: (i, 0)), pl.BlockSpec((tb, LANES), lambda i: (i, 0)),
                  full(w_pool), full(pool_scale), full(gq), full(gkv)],
        out_specs=[o[0] for o in outs], out_shape=[o[1] for o in outs],
        compiler_params=_params(("parallel",)),
    )(z, z, z, z, z, cos_t, sin_t, w_pool, pool_scale, gq, gkv)


def _sigmoid(x):
    return 1.0 / (1.0 + jnp.exp(-x.astype(F32)))


ATTN_SCALE = 1.0 / math.sqrt(QK_DIM)
NEG_BIG = -1e30


LOG2_E = math.log2(math.e)
EXP2_SCALE = ATTN_SCALE * LOG2_E
NT_DIMS = (((1,), (1,)), ((), ()))
TN_DIMS = (((0,), (0,)), ((), ()))


def _on_or_below_diagonal(t):
    return lax.broadcasted_iota(jnp.int32, (t, t), 0) >= lax.broadcasted_iota(jnp.int32, (t, t), 1)


HEADS_PER_STEP = 2
HEAD_COLS = [slice(g * HEAD_PAD, (g + 1) * HEAD_PAD) for g in range(HEADS_PER_STEP)]


def _attn_fwd(q, k, v, *, name):
    s = q.shape[0]
    t = _rows(s)
    wide = HEADS_PER_STEP * HEAD_PAD

    def body(q_ref, k_ref, v_ref, o_ref, lse_ref):
        qi = pl.program_id(1)
        qs = [q_ref[:, cols] for cols in HEAD_COLS]

        def block(j, carry, diagonal):
            rows = pl.ds(pl.multiple_of(j * t, t), t)
            out = []
            for qv, cols, (m, l, acc) in zip(qs, HEAD_COLS, carry):
                sc = lax.dot_general(qv, k_ref[rows, cols], NT_DIMS, preferred_element_type=F32)
                if diagonal:
                    sc = jnp.where(_on_or_below_diagonal(t), sc, NEG_BIG)
                m_new = jnp.maximum(m, jnp.max(sc, axis=-1, keepdims=True))
                p = jnp.exp2((sc - m_new) * EXP2_SCALE)
                alpha = jnp.exp2((m - m_new) * EXP2_SCALE)
                l = alpha * l + jnp.sum(p, axis=-1, keepdims=True)
                acc = alpha * acc + jnp.dot(p.astype(BF16), v_ref[rows, cols], preferred_element_type=F32)
                out.append((m_new, l, acc))
            return tuple(out)

        init = tuple((jnp.full((t, 1), -jnp.inf, F32), jnp.zeros((t, 1), F32), jnp.zeros((t, HEAD_PAD), F32))
                     for _ in HEAD_COLS)
        carry = lax.fori_loop(0, qi, lambda j, c: block(j, c, False), init)
        for g, (cols, (m, l, acc)) in enumerate(zip(HEAD_COLS, block(qi, carry, True))):
            o_ref[:, cols] = (acc / l).astype(BF16)
            lse_ref[g] = m * ATTN_SCALE + jnp.log(l)

    q_spec = pl.BlockSpec((t, wide), lambda h, i: (i, h))
    kv_spec = pl.BlockSpec((s, wide), lambda h, i: (0, h))
    return pl.pallas_call(
        body, name=name, grid=(N_HEADS // HEADS_PER_STEP, s // t),
        in_specs=[q_spec, kv_spec, kv_spec],
        out_specs=[q_spec, pl.BlockSpec((HEADS_PER_STEP, t, 1), lambda h, i: (h, i, 0))],
        out_shape=[jax.ShapeDtypeStruct((s, N_HEADS * HEAD_PAD), BF16), jax.ShapeDtypeStruct((N_HEADS, s, 1), F32)],
        compiler_params=_params(("parallel", "parallel")),
    )(q, k, v)


def _attn_bwd(q, k, v, do, o, lse, cos_t, sin_t, *, name):
    s = q.shape[0]
    t = _rows(s)
    nt = s // t

    def body(q_ref, k_ref, v_ref, do_ref, o_ref, lse_ref, cos_ref, sin_ref, dql_ref, dk_ref, dv_ref, dks_ref,
             dq_ref, dl_ref):
        kj = pl.program_id(1)

        @pl.when(kj == 0)
        def _():
            dq_ref[...] = jnp.zeros_like(dq_ref)

            def delta(i, carry):
                rows = pl.ds(pl.multiple_of(i * t, t), t)
                for g, cols in enumerate(HEAD_COLS):
                    dl_ref[g, rows, :] = jnp.sum(do_ref[rows, cols].astype(F32) * o_ref[rows, cols].astype(F32),
                                                 axis=-1, keepdims=True)
                return carry

            lax.fori_loop(0, nt, delta, 0)

        kvs = [(k_ref[:, cols], v_ref[:, cols]) for cols in HEAD_COLS]

        def block(i, carry, diagonal):
            rows = pl.ds(pl.multiple_of(i * t, t), t)
            out = []
            for g, (cols, (kv, vv), (dk, dv)) in enumerate(zip(HEAD_COLS, kvs, carry)):
                qv, dov = q_ref[rows, cols], do_ref[rows, cols]
                sc = lax.dot_general(qv, kv, NT_DIMS, preferred_element_type=F32)
                p = jnp.exp2(sc * EXP2_SCALE - lse_ref[g, rows, :] * LOG2_E)
                if diagonal:
                    p = jnp.where(_on_or_below_diagonal(t), p, 0.0)
                dp = lax.dot_general(dov, vv, NT_DIMS, preferred_element_type=F32)
                ds = (p * (dp - dl_ref[g, rows, :])).astype(BF16)
                dv = dv + lax.dot_general(p.astype(BF16), dov, TN_DIMS, preferred_element_type=F32)
                dk = dk + lax.dot_general(ds, qv, TN_DIMS, preferred_element_type=F32)
                dq_ref[rows, cols] += jnp.dot(ds, kv, preferred_element_type=F32) * ATTN_SCALE
                out.append((dk, dv))
            return tuple(out)

        zero = jnp.zeros((t, HEAD_PAD), F32)
        carry = block(kj, tuple((zero, zero) for _ in HEAD_COLS), True)
        dk_sum = None
        for cols, (dk, dv) in zip(HEAD_COLS, lax.fori_loop(kj + 1, nt, lambda i, c: block(i, c, False), carry)):
            dk = dk * ATTN_SCALE
            dk_ref[:, cols] = dk.astype(BF16)
            dv_ref[:, cols] = dv.astype(BF16)
            dk_sum = dk if dk_sum is None else dk_sum + dk
        dks_ref[...] = dk_sum

        @pl.when(kj == nt - 1)
        def _():
            def rope_bwd(i, carry):
                rows = pl.ds(pl.multiple_of(i * t, t), t)
                sin = sin_ref[rows, :]
                lane = lax.broadcasted_iota(jnp.int32, sin.shape, 1)
                cos_q = cos_ref[rows, :] + jnp.where(lane < QK_NOPE, 1.0, 0.0)
                for cols in HEAD_COLS:
                    dqv = dq_ref[rows, cols]
                    dql_ref[rows, cols] = (dqv * cos_q - _rotate_half(dqv * sin)).astype(BF16)
                return carry

            lax.fori_loop(0, nt, rope_bwd, 0)

    heads_wide = HEADS_PER_STEP * HEAD_PAD
    full_spec = pl.BlockSpec((s, heads_wide), lambda h, j: (0, h))
    kv_spec = pl.BlockSpec((t, heads_wide), lambda h, j: (j, h))
    vec_spec = pl.BlockSpec((HEADS_PER_STEP, s, 1), lambda h, j: (h, 0, 0))
    table_spec = pl.BlockSpec((s, LANES), lambda h, j: (0, 0))
    wide = jax.ShapeDtypeStruct((s, N_HEADS * HEAD_PAD), BF16)
    n_steps = N_HEADS // HEADS_PER_STEP
    return pl.pallas_call(
        body, name=name, grid=(n_steps, nt),
        in_specs=[full_spec, kv_spec, kv_spec, full_spec, full_spec, vec_spec, table_spec, table_spec],
        out_specs=[full_spec, kv_spec, kv_spec, pl.BlockSpec((None, t, HEAD_PAD), lambda h, j: (h, j, 0))],
        out_shape=[wide, wide, wide, jax.ShapeDtypeStruct((n_steps, s, HEAD_PAD), F32)],
        scratch_shapes=[pltpu.VMEM((s, heads_wide), F32), pltpu.VMEM((HEADS_PER_STEP, s, 1), F32)],
        compiler_params=_params(("parallel", "arbitrary")),
    )(q, k, v, do, o, lse, cos_t, sin_t)


def _acc_specs(widths):
    return ([pl.BlockSpec((1, w), lambda i: (0, 0)) for w in widths],
            [jax.ShapeDtypeStruct((1, w), F32) for w in widths])


def _gate_grads(dxv, m_ref, gate_ref, dm_ref, dgate_ref):
    dm_ref[...] = (dxv * gate_ref[...]).astype(BF16)
    dgate_ref[...] += jnp.sum(dxv * m_ref[...], axis=0, keepdims=True)


def _final_loss(x, g, target, m, gate, *, name):
    s, d = x.shape
    tb = _rows(s)

    def body(x_ref, g_ref, t_ref, m_ref, gate_ref, dx_ref, loss_ref, dg_ref, dm_ref, dgate_ref):
        @pl.when(pl.program_id(0) == 0)
        def _():
            loss_ref[...] = jnp.zeros_like(loss_ref)
            dg_ref[...] = jnp.zeros_like(dg_ref)
            dgate_ref[...] = jnp.zeros_like(dgate_ref)

        xv = x_ref[...]
        r = lax.rsqrt(jnp.mean(xv * xv, axis=-1, keepdims=True) + EPS)
        xn = xv * r
        err = xn * g_ref[...] - t_ref[...]
        loss_ref[...] += 0.5 * jnp.sum(jnp.mean(err * err, axis=-1, keepdims=True), axis=0, keepdims=True)
        dy = err / d
        dg_ref[...] += jnp.sum(dy * xn, axis=0, keepdims=True)
        dxn = dy * g_ref[...]
        dxv = r * (dxn - xn * jnp.mean(dxn * xn, axis=-1, keepdims=True))
        dx_ref[...] = dxv
        _gate_grads(dxv, m_ref, gate_ref, dm_ref, dgate_ref)

    blk = pl.BlockSpec((tb, d), lambda i: (i, 0))
    vec = pl.BlockSpec((1, d), lambda i: (0, 0))
    acc_specs, acc_shapes = _acc_specs((LANES, d))
    return pl.pallas_call(
        body, name=name, grid=(s // tb,),
        in_specs=[blk, vec, blk, blk, vec],
        out_specs=[blk] + acc_specs + [blk, vec],
        out_shape=[jax.ShapeDtypeStruct((s, d), F32)] + acc_shapes + [jax.ShapeDtypeStruct((s, d), BF16),
                                                                     jax.ShapeDtypeStruct((1, d), F32)],
        compiler_params=_params(("arbitrary",)),
    )(x, g, target, m, gate)


def _norm_mod_bwd(dh, x, r, g, sc, dx_skip, *, name, gate=None):
    s, d = x.shape
    tb = _rows(s)
    nb = s // tb
    n_gate = 2 if gate else 0

    def body(dh_ref, x_ref, r_ref, g_ref, sc_ref, skip_ref, *rest):
        gate_refs, (dx_ref, dg_ref, dsc_ref, dsh_ref) = rest[:n_gate], rest[n_gate:n_gate + 4]
        gate_outs, da_sc = rest[n_gate + 4:-1], rest[-1]
        i = pl.program_id(0)

        @pl.when(i == 0)
        def _():
            da_sc[...] = jnp.zeros_like(da_sc)
            dsh_ref[...] = jnp.zeros_like(dsh_ref)
            if gate:
                gate_outs[1][...] = jnp.zeros_like(gate_outs[1])

        dhv, rv = dh_ref[...], r_ref[...]
        xn = x_ref[...] * rv
        dsh_ref[...] += jnp.sum(dhv, axis=0, keepdims=True)
        da_sc[...] += jnp.sum(dhv * xn, axis=0, keepdims=True)
        dxn = dhv * (g_ref[...] * (1.0 + sc_ref[...]))
        dxv = skip_ref[...] + rv * (dxn - xn * jnp.mean(dxn * xn, axis=-1, keepdims=True))
        dx_ref[...] = dxv
        if gate:
            _gate_grads(dxv, *gate_refs, *gate_outs)

        @pl.when(i == nb - 1)
        def _():
            dg_ref[...] = da_sc[...] * (1.0 + sc_ref[...])
            dsc_ref[...] = da_sc[...] * g_ref[...]

    blk = pl.BlockSpec((tb, d), lambda i: (i, 0))
    vec = pl.BlockSpec((1, d), lambda i: (0, 0))
    acc_specs, acc_shapes = _acc_specs((d, d, d))
    gate_specs = [blk, vec] if gate else []
    gate_shapes = [jax.ShapeDtypeStruct((s, d), BF16), jax.ShapeDtypeStruct((1, d), F32)] if gate else []
    return pl.pallas_call(
        body, name=name, grid=(nb,),
        in_specs=[blk, blk, pl.BlockSpec((tb, 1), lambda i: (i, 0)), vec, vec, blk] + gate_specs,
        out_specs=[blk] + acc_specs + gate_specs,
        out_shape=[jax.ShapeDtypeStruct((s, d), F32)] + acc_shapes + gate_shapes,
        scratch_shapes=[pltpu.VMEM((1, d), F32)],
        compiler_params=_params(("arbitrary",)),
    )(dh, x, r, g, sc, dx_skip, *(gate or ()))


def _pool_bwd(dyp, p, w_pool, pool_scale, *, name):
    s = dyp.shape[0]
    tb = _rows(s)
    nb = s // tb
    hb = tb // HALO
    nt_dims = (((1,), (1,)), ((), ()))
    tn_dims = (((0,), (0,)), ((), ()))

    def body(dy_ref, dyn_ref, p_ref, wp_ref, ps_ref, du_ref, gwp_ref, gps_ref):
        i = pl.program_id(0)

        @pl.when(i == 0)
        def _():
            gwp_ref[...] = jnp.zeros_like(gwp_ref)
            gps_ref[...] = jnp.zeros_like(gps_ref)

        cur = dy_ref[...]
        nxt = jnp.where(i < nb - 1, dyn_ref[...], 0.0)
        dpw = (jnp.concatenate([cur, nxt], axis=0) * ps_ref[...]).astype(BF16)
        t = i * tb + lax.broadcasted_iota(jnp.int32, (tb + HALO, 1), 0)
        for g, w in enumerate(POOL_WINDOWS):
            cols = slice(g * POOL_GROUP, (g + 1) * POOL_GROUP)
            wg = wp_ref[g].astype(BF16)
            dp = lax.dot_general(dpw[:, cols], wg, nt_dims, preferred_element_type=F32)
            e = dp / jnp.minimum(t + 1, w).astype(F32)
            lead = _window_sums(e, -1)[g]
            du_ref[:, cols] = (lead[:tb] - dp[:tb]).astype(BF16)
            pg = p_ref[:, cols]
            pw = jnp.dot(pg, wg, preferred_element_type=F32)
            gps_ref[:, cols] += jnp.sum(cur[:, cols] * pw, axis=0, keepdims=True)
            gwp_ref[g] += lax.dot_general(pg, dpw[:tb, cols], tn_dims, preferred_element_type=F32)

    blk = pl.BlockSpec((tb, POOL_DIM), lambda i: (i, 0))
    return pl.pallas_call(
        body, name=name, grid=(nb,),
        in_specs=[blk, pl.BlockSpec((HALO, POOL_DIM), lambda i: (jnp.minimum((i + 1) * hb, s // HALO - 1), 0)), blk,
                  pl.BlockSpec(w_pool.shape, lambda i: (0, 0, 0)), pl.BlockSpec((1, POOL_DIM), lambda i: (0, 0))],
        out_specs=[blk, pl.BlockSpec(w_pool.shape, lambda i: (0, 0, 0)), pl.BlockSpec((1, POOL_DIM), lambda i: (0, 0))],
        out_shape=[jax.ShapeDtypeStruct((s, POOL_DIM), BF16), jax.ShapeDtypeStruct(w_pool.shape, F32),
                   jax.ShapeDtypeStruct((1, POOL_DIM), F32)],
        compiler_params=_params(("arbitrary",)),
    )(dyp, dyp, p, w_pool, pool_scale)


def _key_bwd(dk_sums, cos_t, sin_t, *, name):
    n, s, _ = dk_sums.shape
    tb = _rows(s)

    def body(dk_ref, cos_ref, sin_ref, dkr_ref):
        tot = dk_ref[0]
        for h in range(1, n):
            tot = tot + dk_ref[h]
        dkr_ref[...] = (tot * cos_ref[...] - _rotate_half(tot * sin_ref[...])).astype(BF16)

    tab = pl.BlockSpec((tb, LANES), lambda i: (i, 0))
    return pl.pallas_call(
        body, name=name, grid=(s // tb,),
        in_specs=[pl.BlockSpec((n, tb, LANES), lambda i: (0, i, 0)), tab, tab], out_specs=tab,
        out_shape=jax.ShapeDtypeStruct((s, LANES), BF16),
        compiler_params=_params(("parallel",)),
    )(dk_sums, cos_t, sin_t)


def _rms_bwd(dy, z, z_off, r, g, *, name):
    s, n = dy.shape
    tb = _rows(s)

    def body(dy_ref, x_ref, r_ref, g_ref, dx_ref, dg_ref):
        @pl.when(pl.program_id(0) == 0)
        def _():
            dg_ref[...] = jnp.zeros_like(dg_ref)

        dyv, rv = dy_ref[...], r_ref[...]
        xn = x_ref[...].astype(F32) * rv
        dg_ref[...] += jnp.sum(dyv * xn, axis=0, keepdims=True)
        dxn = dyv * g_ref[...]
        dx_ref[...] = (rv * (dxn - xn * jnp.mean(dxn * xn, axis=-1, keepdims=True))).astype(BF16)

    blk = pl.BlockSpec((tb, n), lambda i: (i, 0))
    acc_specs, acc_shapes = _acc_specs((n,))
    return pl.pallas_call(
        body, name=name, grid=(s // tb,),
        in_specs=[blk, pl.BlockSpec((tb, n), lambda i: (i, z_off // n)), pl.BlockSpec((tb, 1), lambda i: (i, 0)),
                  pl.BlockSpec((1, n), lambda i: (0, 0))],
        out_specs=[blk] + acc_specs, out_shape=[jax.ShapeDtypeStruct((s, n), BF16)] + acc_shapes,
        compiler_params=_params(("arbitrary",)),
    )(dy, z, r, g)


def _silu(c, *, name):
    def body(c_ref, out_ref):
        cv = c_ref[...]
        out_ref[...] = (cv * _sigmoid(cv)).astype(BF16)

    return pl.pallas_call(body, name=name, out_shape=jax.ShapeDtypeStruct(c.shape, BF16),
                          compiler_params=_params())(c)


def _sum_slots(a, n, *, name, out_dtype=F32):
    _, rows, cols = a.shape
    tr = _tile(rows, 256, 8)

    def body(a_ref, out_ref):
        tot = a_ref[0].astype(F32)
        for j in range(1, n):
            tot = tot + a_ref[j].astype(F32)
        out_ref[...] = tot.astype(out_dtype)

    return pl.pallas_call(
        body, name=name, grid=(rows // tr,),
        in_specs=[pl.BlockSpec((n, tr, cols), lambda i: (0, i, 0))],
        out_specs=pl.BlockSpec((tr, cols), lambda i: (i, 0)),
        out_shape=jax.ShapeDtypeStruct((rows, cols), out_dtype),
        compiler_params=_params(("parallel",)),
    )(a)


def _add2_stacked(a, b, stacked, l, *, name):
    rows, cols = a.shape
    tr = _tile(rows, 256, 8)

    def body(a_ref, b_ref, *rest):
        rest[-1][...] = a_ref[...] + b_ref[...]

    blk = pl.BlockSpec((tr, cols), lambda i: (i, 0))
    carried = [] if stacked is None else [stacked]
    return pl.pallas_call(
        body, name=name, grid=(rows // tr,),
        in_specs=[blk, blk] + [pl.BlockSpec(memory_space=pl.ANY) for _ in carried],
        out_specs=pl.BlockSpec((None, tr, cols), lambda i: (l, i, 0)),
        out_shape=jax.ShapeDtypeStruct((DEPTH, rows, cols), F32),
        input_output_aliases={2: 0} if carried else {},
        compiler_params=_params(("parallel",)),
    )(a, b, *carried)


def _adamw(w, g, m, v, *, name):
    shape = w.shape
    if w.ndim == 2:
        w, g, m, v = (a.reshape((1,) + shape) for a in (w, g, m, v))
    layers, rows, cols = w.shape
    tr = _tile(rows, max(8, (1 << 18) // cols), 8)
    c1 = 1.0 - ADAM_B1 ** ADAM_STEP
    c2 = 1.0 - ADAM_B2 ** ADAM_STEP

    def body(w_ref, g_ref, m_ref, v_ref, d_ref, nm_ref, nv_ref):
        gv = g_ref[...]
        nm = ADAM_B1 * m_ref[...] + (1.0 - ADAM_B1) * gv
        nv = ADAM_B2 * v_ref[...] + (1.0 - ADAM_B2) * (gv * gv)
        nm_ref[...] = nm
        nv_ref[...] = nv
        d_ref[...] = -ADAM_LR * ((nm / c1) / (jnp.sqrt(nv / c2) + ADAM_EPS) + ADAM_WD * w_ref[...])

    blk = pl.BlockSpec((None, tr, cols), lambda l, i: (l, i, 0))
    outs = pl.pallas_call(
        body, name=name, grid=(layers, rows // tr), in_specs=[blk] * 4, out_specs=[blk] * 3,
        out_shape=[jax.ShapeDtypeStruct((layers, rows, cols), F32)] * 3,
        compiler_params=_params(("parallel", "parallel")),
    )(w, g, m, v)
    return [o.reshape(shape) for o in outs]


def _coords():
    return lax.axis_index("x"), lax.axis_index("y"), lax.axis_index("c")


def _other_chips(x, y):
    return [(1 - x, y), (x, 1 - y), (1 - x, 1 - y)]


def _all_gather_small(blk, *, name, swaps=()):
    m_per, n = blk.shape
    n_swaps = len(swaps)

    def body(x_ref, *refs):
        swap_srcs, out_ref, swap_outs = refs[:n_swaps], refs[n_swaps], refs[n_swaps + 1:2 * n_swaps + 1]
        send_sems, recv_sems, local_sem = refs[2 * n_swaps + 1:2 * n_swaps + 4]
        x, y, c = _coords()
        me, sibling = (x, y, c), (x, y, 1 - c)
        chips = _other_chips(x, y)
        swapping = [pltpu.make_async_remote_copy(src_ref=src, dst_ref=dst, send_sem=refs[-2].at[k], recv_sem=refs[-1].at[k],
                                                 device_id=sibling, device_id_type=MESH)
                    for k, (src, dst) in enumerate(zip(swap_srcs, swap_outs))]
        for cp in swapping:
            cp.start()

        def rows(px, py, pc):
            return out_ref.at[pl.ds((4 * px + 2 * py + pc) * m_per, m_per), :]

        def copy(k, block, to, src=None):
            return pltpu.make_async_remote_copy(
                src_ref=rows(*block) if src is None else src, dst_ref=rows(*block),
                send_sem=send_sems.at[k], recv_sem=recv_sems.at[k], device_id=to, device_id_type=MESH)

        mine = pltpu.make_async_copy(x_ref, rows(*me), local_sem)
        mine.start()
        first = [copy(0, me, sibling, src=x_ref)]
        first += [copy(1 + j, me, (*chip, c), src=x_ref) for j, chip in enumerate(chips)]
        for cp in first:
            cp.start()
        passed = [copy(4 + j, (*chip, c), sibling) for j, chip in enumerate(chips)]
        for j, chip in enumerate(chips):
            copy(1 + j, (*chip, c), me).wait_recv()
            passed[j].start()
        copy(0, sibling, me).wait_recv()
        for j, chip in enumerate(chips):
            copy(4 + j, (*chip, 1 - c), me).wait_recv()
        for cp in first + passed:
            cp.wait_send()
        mine.wait()
        for cp in swapping:
            cp.wait()

    any_spec = pl.BlockSpec(memory_space=pl.ANY)
    outs = pl.pallas_call(
        body, name=name,
        out_shape=[jax.ShapeDtypeStruct((N_DEV * m_per, n), blk.dtype)] + [jax.ShapeDtypeStruct(a.shape, a.dtype)
                                                                           for a in swaps],
        in_specs=[pl.BlockSpec(memory_space=pltpu.VMEM)] + [any_spec] * n_swaps,
        out_specs=[pl.BlockSpec(memory_space=pltpu.VMEM)] + [any_spec] * n_swaps,
        scratch_shapes=[pltpu.SemaphoreType.DMA((7,)), pltpu.SemaphoreType.DMA((7,)), pltpu.SemaphoreType.DMA]
        + ([pltpu.SemaphoreType.DMA((n_swaps,)), pltpu.SemaphoreType.DMA((n_swaps,))] if swaps else []),
        compiler_params=_params(),
    )(blk, *swaps)
    return (outs[0], outs[1:]) if swaps else outs[0]


HBM_SPEC = pl.BlockSpec(memory_space=pltpu.HBM)
SEM_SPEC = pl.BlockSpec(memory_space=pltpu.SEMAPHORE)
DATAFLOW = pltpu.SideEffectType.DATAFLOW_SIDE_EFFECTING


def _chip_copies(src_ref, land_ref, send_sems, recv_sems, scatter):
    x, y, c = _coords()
    my = 2 * x + y
    outgoing, incoming = [], []
    for k, (px, py) in enumerate(_other_chips(x, y)):
        peer = 2 * px + py

        def copy(src_slot, dst_slot):
            return pltpu.make_async_remote_copy(
                src_ref=src_ref.at[src_slot] if scatter else src_ref, dst_ref=land_ref.at[dst_slot],
                send_sem=send_sems.at[k], recv_sem=recv_sems.at[k], device_id=(px, py, c), device_id_type=MESH)

        outgoing.append(copy(peer, my))
        incoming.append(copy(my, peer))
    return outgoing, incoming


def _exchange_start(srcs, *, name, scatter):
    n = len(srcs)
    land_shapes = [src.shape if scatter else (N_CHIPS,) + src.shape for src in srcs]

    def body(*refs):
        for k in range(n):
            send_sems, recv_sems = refs[2 * n + 4 * k], refs[2 * n + 4 * k + 1]
            outgoing, _ = _chip_copies(refs[k], refs[n + k], send_sems, recv_sems, scatter)
            for cp in outgoing:
                cp.start()
        refs[-1][...] = jnp.zeros_like(refs[-1])

    out_shape, out_specs, aliases = [], [], {}
    for k, (src, land_shape) in enumerate(zip(srcs, land_shapes)):
        out_shape += [pltpu.SemaphoreType.DMA((N_CHIPS - 1,)), pltpu.SemaphoreType.DMA((N_CHIPS - 1,)),
                      pltpu.HBM(src.shape, src.dtype), pltpu.HBM(land_shape, src.dtype)]
        out_specs += [SEM_SPEC, SEM_SPEC, HBM_SPEC, HBM_SPEC]
        aliases.update({k: 4 * k + 2, n + k: 4 * k + 3})
    outs = pl.pallas_call(
        body, name=name,
        out_shape=tuple(out_shape) + (jax.ShapeDtypeStruct((8, LANES), F32),),
        in_specs=(HBM_SPEC,) * (2 * n),
        out_specs=tuple(out_specs) + (pl.BlockSpec(memory_space=pltpu.VMEM),),
        input_output_aliases=aliases,
        compiler_params=pltpu.CompilerParams(has_side_effects=DATAFLOW),
    )(*[pltpu.with_memory_space_constraint(src, pltpu.HBM) for src in srcs],
      *[pltpu.with_memory_space_constraint(lax.empty(shape, src.dtype), pltpu.HBM)
        for src, shape in zip(srcs, land_shapes)])
    return [tuple(outs[4 * k:4 * k + 4]) for k in range(n)], outs[-1]


def _exchange_wait(started, after, *, name, scatter):
    send_sems, recv_sems, src_thru, land_thru = started

    def body(src_ref, land_ref, send_sems, recv_sems, after_ref, src_dead, got_ref):
        outgoing, incoming = _chip_copies(src_ref, land_ref, send_sems, recv_sems, scatter)
        for cp in outgoing:
            cp.wait_send()
        for cp in incoming:
            cp.wait_recv()

    return pl.pallas_call(
        body, name=name,
        out_shape=(pltpu.HBM(src_thru.shape, src_thru.dtype), pltpu.HBM(land_thru.shape, land_thru.dtype)),
        in_specs=(HBM_SPEC, HBM_SPEC, SEM_SPEC, SEM_SPEC, pl.BlockSpec(memory_space=pl.ANY)),
        out_specs=(HBM_SPEC, HBM_SPEC),
        input_output_aliases={0: 0, 1: 1},
        compiler_params=pltpu.CompilerParams(has_side_effects=DATAFLOW),
    )(src_thru, land_thru, send_sems, recv_sems, after)


def _pack_rows(a):
    return a.reshape(-1, D_MODEL)


def _pad_heads(w, width):
    r = w.shape[0]
    return jnp.pad(w, ((0, 0), (0, 0), (0, HEAD_PAD - width))).reshape(r, N_HEADS * HEAD_PAD)


MIX_NAMES = ("w_uq", "w_uk", "w_uv", "p_pool", "p_attn", "w_out")
GROUPS = ("in", "mix", "ff1", "ff2")


def _local_shard(weights, l, group, zero):
    if group == "mix":
        shard = jnp.concatenate([_pack_rows(weights[n][l]) for n in MIX_NAMES], axis=0)
    else:
        shard = weights[{"in": "w_in", "ff1": "w_ff1", "ff2": "w_ff2"}[group]][l]
    return (shard + zero).astype(BF16)


def _unpack_weights(gathered, group):
    def cols(a, k):
        return a.reshape(N_CHIPS, k, -1).transpose(1, 0, 2).reshape(k, -1)

    if group == "in":
        full = gathered.reshape(W_IN_COLS, D_MODEL)
        u, cq, ckv, kr, gates = (full[a:b] for a, b in (W_IN_U, W_IN_CQ, W_IN_CKV, W_IN_KR, W_IN_GATES))
        kr = jnp.pad(kr, ((QK_NOPE, HEAD_PAD - QK_DIM), (0, 0)))
        return dict(w_in=jnp.concatenate([cq, kr, u, gates, ckv], axis=0))
    if group == "ff1":
        return dict(w_ff1=gathered)
    if group == "ff2":
        return dict(w_ff2=gathered.reshape(D_FF, D_MODEL))

    def p_attn(a):
        full = cols(a, ATTN_DIM).reshape(N_HEADS, V_DIM, D_MODEL)
        return jnp.pad(full, ((0, 0), (0, HEAD_PAD - V_DIM), (0, 0))).reshape(N_HEADS * HEAD_PAD, D_MODEL)

    build = dict(
        w_uq=lambda a: _pad_heads(a.reshape(Q_LORA, N_HEADS, QK_DIM), QK_DIM),
        w_uk=lambda a: _pad_heads(a.reshape(KV_LORA, N_HEADS, QK_NOPE), QK_NOPE),
        w_uv=lambda a: _pad_heads(a.reshape(KV_LORA, N_HEADS, V_DIM), V_DIM),
        p_pool=lambda a: cols(a, POOL_DIM),
        p_attn=p_attn,
        w_out=lambda a: a.reshape(D_MODEL, D_MODEL),
    )
    w, off = {}, 0
    for name in MIX_NAMES:
        w[name] = build[name](gathered[:, off:off + ROWS_OF[name]])
        off += ROWS_OF[name]
    return w


def _pack_grads(g, group):
    def cols(a):
        k = a.shape[0]
        return a.reshape(k, N_CHIPS, -1).transpose(1, 0, 2).reshape(N_CHIPS, -1, D_MODEL)

    def rows(a):
        return a.reshape(N_CHIPS, -1, D_MODEL)

    def heads(width):
        return lambda a: rows(a.reshape(a.shape[0], N_HEADS, HEAD_PAD)[:, :, :width])

    if group == "in":
        full = jnp.concatenate([g["u"], g["cq"], g["ckv"], g["kr"][QK_NOPE:QK_DIM], g["ga"], g["gb"]], axis=0)
        return full.reshape(N_CHIPS, W_IN_SHARD, D_MODEL)
    if group == "ff1":
        return g["w_ff1"]
    if group == "ff2":
        return g["w_ff2"].reshape(N_CHIPS, D_FF // N_CHIPS, D_MODEL)

    def p_attn(a):
        return cols(a.reshape(N_HEADS, HEAD_PAD, D_MODEL)[:, :V_DIM].reshape(ATTN_DIM, D_MODEL))

    build = dict(w_uq=heads(QK_DIM), w_uk=heads(QK_NOPE), w_uv=heads(V_DIM), p_pool=cols, p_attn=p_attn, w_out=rows)
    return jnp.concatenate([build[name](g[name]) for name in MIX_NAMES], axis=1)


def _per_head(fn, acc, *tables):
    return jnp.concatenate([fn(acc[:, h * HEAD_PAD:(h + 1) * HEAD_PAD], *tables) for h in range(N_HEADS)], axis=1)


def _rope_head(a, cos, sin):
    lane = lax.broadcasted_iota(jnp.int32, a.shape, 1)
    return a * (cos + jnp.where(lane < QK_NOPE, 1.0, 0.0)) + _rotate_half(a) * sin


def _layer_fwd(l, x, mod, get_weights, small, cos_t, sin_t):
    sh1, sc1, g1, sh2, sc2, g2 = mod
    tag = f"_l{l}"
    h, r1 = _norm_mod(x, small["ln1_g"], sc1, sh1, name="norm1" + tag)
    w = dict(get_weights("in", h))
    (z,) = _mm(h, w["w_in"], tb=True, name="in_proj" + tag, out_dtypes=(BF16,))
    p, yp, cq, ckv, kr, rq, rkv = _mixer_pre(z, cos_t, sin_t, small["w_pool"], small["pool_scale"],
                                              small["q_norm_g"], small["kv_norm_g"], name="mixer_pre" + tag)
    w.update(get_weights("mix", yp))
    (ya,) = _mm(yp, w["p_pool"], name="pool_out" + tag, out_dtypes=(BF16,))
    (q,) = _mm(cq, w["w_uq"], name="q_proj" + tag, out_dtypes=(BF16,),
               epilogue=lambda acc, cos, sin: (_per_head(_rope_head, acc, cos, sin),),
               extras=((cos_t, "table"), (sin_t, "table")))
    (k,) = _mm(ckv, w["w_uk"], name="k_proj" + tag, out_dtypes=(BF16,),
               epilogue=lambda acc, krv: (_per_head(lambda a, b: a + b, acc, krv),), extras=((kr, "table"),))
    (v,) = _mm(ckv, w["w_uv"], name="v_proj" + tag, out_dtypes=(BF16,))
    o, lse = _attn_fwd(q, k, v, name="attn_fwd" + tag)
    yb, merged = _mm(o, w["p_attn"], name="attn_out" + tag, out_dtypes=(BF16, BF16), tm=512,
                     epilogue=lambda acc, ga, gb, yav: (acc, _sigmoid(ga) * yav + _sigmoid(gb) * acc),
                     extras=((z, ("tile", ZC_GA // D_MODEL)), (z, ("tile", ZC_GB // D_MODEL)), (ya, "tile")))
    mo, x1 = _mm(merged, w["w_out"], name="mix_out" + tag, out_dtypes=(BF16, F32),
                 epilogue=lambda acc, xr, g: (acc, xr + g * acc), extras=((x, "tile"), (g1, "row")))
    h2, r2 = _norm_mod(x1, small["ln2_g"], sc2, sh2, name="norm2" + tag)
    w.update(get_weights("ff1", merged))
    f, act = _mm(h2, w["w_ff1"], b_stack=True, name="ff1" + tag, out_dtypes=(BF16, BF16),
                 epilogue=lambda acc: (acc, jnp.square(jnp.maximum(acc, 0.0))))
    w.update(get_weights("ff2", act))
    m2, x2 = _mm(act, w["w_ff2"], name="ff2" + tag, out_dtypes=(BF16, F32),
                 epilogue=lambda acc, xr, g: (acc, xr + g * acc), extras=((x1, "tile"), (g2, "row")))
    saved = dict(x=x, h=h, r1=r1, z=z, p=p, yp=yp, cq=cq, ckv=ckv, rq=rq, rkv=rkv, ya=ya, q=q, k=k, v=v, o=o, lse=lse,
                 yb=yb, merged=merged, mo=mo, x1=x1, h2=h2, r2=r2, f=f, act=act, m2=m2)
    return x2, saved, w


def _merge_grads(dm, ga, gb, ya, yb):
    sa, sb = _sigmoid(ga), _sigmoid(gb)
    return dm * sa, dm * sb, dm * ya * (sa * (1.0 - sa)), dm * yb * (sb * (1.0 - sb))


def _layer_bwd(l, dx2, dm2, dg2, sv, mod, w, small, cos_t, sin_t, send_grads, gate_below):
    sh1, sc1, g1, sh2, sc2, g2 = mod
    tag = f"_l{l}"
    gw = {}
    (df,) = _mm(dm2, w["w_ff2"], tb=True, name="ff2_dx" + tag, out_dtypes=(BF16,),
                epilogue=lambda acc, f: (acc * (2.0 * jnp.maximum(f, 0.0)),), extras=((sv["f"], "tile"),))
    (g_ff2,) = _mm(sv["act"], dm2, ta=True, name="ff2_dw" + tag, out_dtypes=(BF16,))
    (g_ff1,) = _mm(sv["h2"], df, ta=True, out_stack=N_CHIPS, name="ff1_dw" + tag, out_dtypes=(BF16,))
    sc2 = sc2 + send_grads("ff2", dict(w_ff2=g_ff2)) + send_grads("ff1", dict(w_ff1=g_ff1))
    (dh2,) = _mm(df, w["w_ff1"], tb=True, b_stack=True, name="ff1_dx" + tag)
    dx1, dln2, dsc2, dsh2, dmo, dg1 = _norm_mod_bwd(dh2, sv["x1"], sv["r2"], small["ln2_g"], sc2, dx2,
                                                    gate=(sv["mo"], g1), name="norm2_bwd" + tag)
    dya, dyb, dga, dgb = _mm(dmo, w["w_out"], tb=True, name="mix_out_dx" + tag, out_dtypes=(BF16,) * 4, tm=512,
                             epilogue=_merge_grads,
                             extras=((sv["z"], ("tile", ZC_GA // D_MODEL)), (sv["z"], ("tile", ZC_GB // D_MODEL)),
                                     (sv["ya"], "tile"), (sv["yb"], "tile")))
    (gw["w_out"],) = _mm(sv["merged"], dmo, ta=True, name="mix_out_dw" + tag, out_dtypes=(BF16,))
    (gw["p_pool"],) = _mm(sv["yp"], dya, ta=True, name="pool_out_dw" + tag, out_dtypes=(BF16,))
    (dyp,) = _mm(dya, w["p_pool"], tb=True, name="pool_out_dx" + tag)
    du, g_w_pool, g_pool_scale = _pool_bwd(dyp, sv["p"], small["w_pool"], small["pool_scale"], name="pool_bwd" + tag)
    (gw["p_attn"],) = _mm(sv["o"], dyb, ta=True, name="attn_out_dw" + tag, out_dtypes=(BF16,))
    (do,) = _mm(dyb, w["p_attn"], tb=True, name="attn_out_dx" + tag, out_dtypes=(BF16,))
    dql, dkb, dv, dk_sums = _attn_bwd(sv["q"], sv["k"], sv["v"], do, sv["o"], sv["lse"], cos_t, sin_t,
                                      name="attn_bwd" + tag)
    dkr = _key_bwd(dk_sums, cos_t, sin_t, name="key_bwd" + tag)
    (gw["w_uq"],) = _mm(sv["cq"], dql, ta=True, name="q_proj_dw" + tag, out_dtypes=(BF16,))
    (gw["w_uk"],) = _mm(sv["ckv"], dkb, ta=True, name="k_proj_dw" + tag, out_dtypes=(BF16,))
    (gw["w_uv"],) = _mm(sv["ckv"], dv, ta=True, name="v_proj_dw" + tag, out_dtypes=(BF16,))
    (dcq,) = _mm(dql, w["w_uq"], tb=True, name="q_proj_dx" + tag)
    (dckv,) = _mm(dkb, w["w_uk"], tb=True, second=(dv, w["w_uv"]), name="kv_proj_dx" + tag)
    q_norm_g = small["q_norm_g"] + send_grads("mix", gw)
    dcq_raw, g_qn = _rms_bwd(dcq, sv["z"], ZC_CQ, sv["rq"], q_norm_g, name="q_norm_bwd" + tag)
    dckv_raw, g_kvn = _rms_bwd(dckv, sv["z"], ZC_CKV, sv["rkv"], small["kv_norm_g"], name="kv_norm_bwd" + tag)
    dz = dict(cq=dcq_raw, kr=dkr, u=du, ga=dga, gb=dgb, ckv=dckv_raw)
    g_in = {n: _mm(piece, sv["h"], ta=True, name=f"in_proj_dw_{n}" + tag, out_dtypes=(BF16,))[0]
            for n, piece in dz.items()}
    sc1 = sc1 + send_grads("in", g_in)
    dh = _mm_sum(list(dz.values()), w["w_in"], [Z_OFFSETS[n] for n in dz], name="in_proj_dx" + tag)
    dx, dln1, dsc1, dsh1, *below = _norm_mod_bwd(dh, sv["x"], sv["r1"], small["ln1_g"], sc1, dx1, gate=gate_below,
                                                 name="norm1_bwd" + tag)
    dmod = jnp.concatenate([dsh1, dsc1, dg1, dsh2, dsc2, dg2], axis=0)
    gsmall = dict(ln1_g=dln1, ln2_g=dln2, q_norm_g=g_qn, kv_norm_g=g_kvn, w_pool=g_w_pool, pool_scale=g_pool_scale)
    return dx, dmod, gsmall, below


SMALL_LOSS = 6
SMALL_SINGLES = 16
SMALL_POOL = 24
SMALL_POOL_ROWS = len(POOL_WINDOWS) * POOL_GROUP * POOL_GROUP // D_MODEL
SMALL_ROWS = SMALL_POOL + DEPTH * SMALL_POOL_ROWS


def _pack_small(parts, *, name):
    def body(*refs):
        out_ref = refs[-1]
        out_ref[...] = jnp.zeros_like(out_ref)
        for ref, (_, row) in zip(refs[:-1], parts):
            out_ref[row:row + ref.shape[0], :] = ref[...]

    return pl.pallas_call(body, name=name, out_shape=jax.ShapeDtypeStruct((SMALL_ROWS, D_MODEL), F32),
                          compiler_params=_params())(*[a for a, _ in parts])


def kernel(x, c, positions, ln1_g, ln2_g, w_ada, b_ada, w_in, q_norm_g, w_uq, kv_norm_g, w_uk, w_uv, w_pool, pool_scale, p_pool, p_attn, w_out, w_ff1, w_ff2, final_g, loss_target, m_ln1_g, m_ln2_g, m_w_ada, m_b_ada, m_w_in, m_q_norm_g, m_w_uq, m_kv_norm_g, m_w_uk, m_w_uv, m_w_pool, m_pool_scale, m_p_pool, m_p_attn, m_w_out, m_w_ff1, m_w_ff2, m_final_g, v_ln1_g, v_ln2_g, v_w_ada, v_b_ada, v_w_in, v_q_norm_g, v_w_uq, v_kv_norm_g, v_w_uk, v_w_uv, v_w_pool, v_pool_scale, v_p_pool, v_p_attn, v_w_out, v_w_ff1, v_w_ff2, v_final_g):
    weights = dict(ln1_g=ln1_g, ln2_g=ln2_g, w_ada=w_ada, b_ada=b_ada, w_in=w_in, q_norm_g=q_norm_g, w_uq=w_uq,
                   kv_norm_g=kv_norm_g, w_uk=w_uk, w_uv=w_uv, w_pool=w_pool, pool_scale=pool_scale, p_pool=p_pool,
                   p_attn=p_attn, w_out=w_out, w_ff1=w_ff1, w_ff2=w_ff2, final_g=final_g)
    moms = dict(ln1_g=m_ln1_g, ln2_g=m_ln2_g, w_ada=m_w_ada, b_ada=m_b_ada, w_in=m_w_in, q_norm_g=m_q_norm_g,
                w_uq=m_w_uq, kv_norm_g=m_kv_norm_g, w_uk=m_w_uk, w_uv=m_w_uv, w_pool=m_w_pool,
                pool_scale=m_pool_scale, p_pool=m_p_pool, p_attn=m_p_attn, w_out=m_w_out, w_ff1=m_w_ff1,
                w_ff2=m_w_ff2, final_g=m_final_g)
    vels = dict(ln1_g=v_ln1_g, ln2_g=v_ln2_g, w_ada=v_w_ada, b_ada=v_b_ada, w_in=v_w_in, q_norm_g=v_q_norm_g,
                w_uq=v_w_uq, kv_norm_g=v_kv_norm_g, w_uk=v_w_uk, w_uv=v_w_uv, w_pool=v_w_pool,
                pool_scale=v_pool_scale, p_pool=v_p_pool, p_attn=v_p_attn, w_out=v_w_out, w_ff1=v_w_ff1,
                w_ff2=v_w_ff2, final_g=v_final_g)
    order = list(weights)
    for table in (weights, moms, vels):
        table["w_in"] = jnp.swapaxes(table["w_in"], 1, 2)
    seq = x.shape[1]
    my_chip = 2 * lax.axis_index("x") + lax.axis_index("y")
    my_dev = 2 * my_chip + lax.axis_index("c")
    ada_cols = w_ada.shape[2]

    small = [dict(ln1_g=ln1_g[l:l + 1], ln2_g=ln2_g[l:l + 1], q_norm_g=q_norm_g[l:l + 1], kv_norm_g=kv_norm_g[l:l + 1],
                  w_pool=w_pool[l], pool_scale=pool_scale[l:l + 1]) for l in range(DEPTH)]

    c_all = _all_gather_small(jnp.pad(c, ((0, 7), (0, 0))), name="cond_all_gather")
    c_act = _silu(c_all, name="cond_silu")
    b_mine = lax.dynamic_slice_in_dim(b_ada, my_chip * ada_cols, ada_cols, axis=1).reshape(1, DEPTH * ada_cols)
    (mod_cat,) = _mm(c_act, w_ada, b_stack=True, name="ada_fwd", epilogue=lambda acc, b: (acc + b,),
                     extras=((b_mine, "row"),))
    mod_mine = jnp.concatenate([mod_cat[::8, l * ada_cols:(l + 1) * ada_cols] for l in range(DEPTH)], axis=0)
    mod_all = _all_gather_small(mod_mine, name="mod_all_gather").reshape(N_DEV, DEPTH, N_DEV, ada_cols)

    zero = mod_all[0, 0, 0, 0] * 0.0
    keys = [(l, group) for l in range(DEPTH) for group in GROUPS]
    exchanges, token = _exchange_start([_local_shard(weights, l, group, zero) for l, group in keys],
                                       name="weights_send", scatter=False)
    started = dict(zip(keys, exchanges))
    pin = token[0:1, 0:1]

    def gathered_weights(l, group, after):
        mine, land = _exchange_wait(started[l, group], after, name=f"weights_wait_l{l}_{group}", scatter=False)
        land = lax.dynamic_update_slice_in_dim(land, mine[None], my_chip, axis=0)
        return _unpack_weights(land, group)

    mods = []
    for l in range(DEPTH):
        row = jnp.concatenate([lax.dynamic_index_in_dim(mod_all[2 * j, l], my_dev, axis=0, keepdims=True)
                               for j in range(N_CHIPS)], axis=1) + pin
        mods.append([row[:, i * D_MODEL:(i + 1) * D_MODEL] for i in range(N_MOD)])

    inv_freq = ROPE_THETA ** (-jnp.arange(0, QK_ROPE, 2, dtype=F32) / QK_ROPE)
    freq_lanes = jnp.concatenate([jnp.zeros((QK_NOPE,), F32), inv_freq, inv_freq,
                                  jnp.zeros((HEAD_PAD - QK_DIM,), F32)]).reshape(1, LANES)
    cos_t, sin_t = _rope_tables(positions.reshape(seq, 1), freq_lanes, name="rope_tables")

    xs, saved, wl = x.reshape(seq, D_MODEL), [], []
    for l in range(DEPTH):
        xs, sv, w_l = _layer_fwd(l, xs, mods[l], functools.partial(gathered_weights, l), small[l], cos_t, sin_t)
        saved.append(sv)
        wl.append(w_l)
    dx, loss_part, g_final, dm2, dg2 = _final_loss(xs, final_g.reshape(1, D_MODEL), loss_target.reshape(seq, D_MODEL),
                                                   saved[-1]["m2"], mods[-1][5], name="final_loss")

    sent, pending = [], []
    send_after = {(0, "ff1"), (0, "in")}

    def send_grads(l, group, g):
        pending.append((l, group, _pack_grads(g, group)))
        if (l, group) not in send_after:
            return jnp.zeros((1, 1), F32)
        exchanges, token_g = _exchange_start([gpack for _, _, gpack in pending], name=f"grads_send_l{l}_{group}",
                                             scatter=True)
        sent.extend((item[0], item[1], exchange) for item, exchange in zip(pending, exchanges))
        pending.clear()
        return token_g[0:1, 0:1]

    dmod, gsmall = [None] * DEPTH, [None] * DEPTH
    for l in reversed(range(DEPTH)):
        gate_below = (saved[l - 1]["m2"], mods[l - 1][5]) if l > 0 else None
        dx, dmod[l], gsmall[l], below = _layer_bwd(l, dx, dm2, dg2, saved[l], mods[l], wl[l], small[l], cos_t, sin_t,
                                                   functools.partial(send_grads, l), gate_below)
        dm2, dg2 = below if below else (None, None)
    grads = dict(x=dx.reshape(1, seq, D_MODEL))

    big_parts, after = [], dmod[0]
    for l, group, started_g in sent:
        tg = f"_l{l}_{group}"
        gpack, land = _exchange_wait(started_g, after, name="grads_wait" + tg, scatter=True)
        own = lax.dynamic_index_in_dim(gpack, my_chip, axis=0, keepdims=True)
        land = lax.dynamic_update_slice_in_dim(land, own, my_chip, axis=0)
        big_parts.append(_sum_slots(land, N_CHIPS, name="grads_sum_chips" + tg))
        after = big_parts[-1]

    def lanes(a):
        flat = a.reshape(1, -1)
        return jnp.pad(flat, ((0, 0), (0, D_MODEL - flat.shape[1])))

    singles = [gsmall[0]["ln1_g"], gsmall[1]["ln1_g"], gsmall[0]["ln2_g"], gsmall[1]["ln2_g"], g_final,
               lanes(jnp.concatenate([gsmall[l]["pool_scale"] for l in range(DEPTH)], axis=1)),
               lanes(jnp.concatenate([gsmall[l]["q_norm_g"] for l in range(DEPTH)], axis=1)),
               lanes(jnp.concatenate([gsmall[l]["kv_norm_g"] for l in range(DEPTH)], axis=1))]
    parts = [(dmod[0], 0), (lanes(loss_part), SMALL_LOSS), (dmod[1], 8)]
    parts += [(a, SMALL_SINGLES + i) for i, a in enumerate(singles)]
    parts += [(gsmall[l]["w_pool"].reshape(-1, D_MODEL), SMALL_POOL + l * SMALL_POOL_ROWS) for l in range(DEPTH)]
    small_all, big_others = _all_gather_small(_pack_small(parts, name="small_grads_pack"), swaps=big_parts,
                                              name="small_grads_all_gather")
    small_all = small_all.reshape(N_DEV, SMALL_ROWS, D_MODEL)
    ssum = _sum_slots(small_all, N_DEV, name="small_grads_sum")
    loss = ssum[SMALL_LOSS, 0]
    grads["b_ada"] = jnp.stack([ssum[8 * l:8 * l + N_MOD] for l in range(DEPTH)]).reshape(DEPTH, N_MOD * D_MODEL)
    grads["ln1_g"] = ssum[SMALL_SINGLES:SMALL_SINGLES + 2]
    grads["ln2_g"] = ssum[SMALL_SINGLES + 2:SMALL_SINGLES + 4]
    grads["final_g"] = ssum[SMALL_SINGLES + 4]
    grads["pool_scale"] = ssum[SMALL_SINGLES + 5].reshape(DEPTH, POOL_DIM)
    grads["q_norm_g"] = ssum[SMALL_SINGLES + 6, :DEPTH * Q_LORA].reshape(DEPTH, Q_LORA)
    grads["kv_norm_g"] = ssum[SMALL_SINGLES + 7, :DEPTH * KV_LORA].reshape(DEPTH, KV_LORA)
    grads["w_pool"] = ssum[SMALL_POOL:SMALL_ROWS].reshape(w_pool.shape)

    gsum = {}
    for (l, group, _), part, other in zip(sent, big_parts, big_others):
        gsum[group] = _add2_stacked(part, other, gsum.get(group), l, name=f"grads_sum_cores_l{l}_{group}")
    grads.update(w_in=gsum["in"], w_ff1=gsum["ff1"], w_ff2=gsum["ff2"])
    off = 0
    for name in MIX_NAMES:
        grads[name] = gsum["mix"][:, off:off + ROWS_OF[name]].reshape(weights[name].shape)
        off += ROWS_OF[name]

    c_act_t = jnp.pad(c_act[::8].T, ((0, 0), (0, LANES - N_DEV)))
    d_mine = []
    for l in range(DEPTH):
        d_all = small_all[:, 8 * l:8 * l + N_MOD].reshape(N_DEV, N_MOD * D_MODEL)
        d_mine.append(lax.dynamic_slice_in_dim(d_all, my_chip * ada_cols, ada_cols, axis=1))
    d_cat = jnp.pad(jnp.concatenate(d_mine, axis=1), ((0, LANES - N_DEV), (0, 0)))
    (grads["w_ada"],) = _mm(c_act_t, d_cat, out_stack=DEPTH, name="ada_dw")

    def view(a):
        return a.reshape(1, -1) if a.ndim == 1 else a if a.ndim == 3 else a.reshape(-1, a.shape[-1])

    delta, new_m, new_v = {}, {}, {}
    for name in order:
        shape = weights[name].shape
        d, nm, nv = _adamw(view(weights[name]), view(grads[name]), view(moms[name]), view(vels[name]),
                           name="adamw_" + name)
        delta[name], new_m[name], new_v[name] = d.reshape(shape), nm.reshape(shape), nv.reshape(shape)
    for table in (grads, delta, new_m, new_v):
        table["w_in"] = jnp.swapaxes(table["w_in"], 1, 2)
    return (loss, grads["x"], *[grads[n] for n in order], *[delta[n] for n in order],
            *[new_m[n] for n in order], *[new_v[n] for n in order])
```

```python
import functools
import math

import jax
import jax.numpy as jnp
from jax import lax
from jax.experimental import pallas as pl
from jax.experimental.pallas import tpu as pltpu

F32 = jnp.float32
BF16 = jnp.bfloat16
MESH = pl.DeviceIdType.MESH

D_MODEL = 1024
DEPTH = 2
POOL_WINDOWS = (2, 4, 8, 16)
POOL_GROUP = 128
POOL_DIM = 512
N_HEADS = 8
QK_NOPE = 64
QK_ROPE = 32
QK_DIM = QK_NOPE + QK_ROPE
V_DIM = 64
HEAD_PAD = 128
Q_LORA = 384
KV_LORA = 256
ROPE_THETA = 10000.0
ATTN_DIM = N_HEADS * V_DIM
D_FF = 4 * D_MODEL
N_MOD = 6
EPS = 1e-6
N_CHIPS = 4
N_DEV = 8

ADAM_LR = 0.001
ADAM_B1 = 0.9
ADAM_B2 = 0.999
ADAM_EPS = 1e-08
ADAM_WD = 0.01
ADAM_STEP = 10

VMEM_LIMIT_BYTES = 56 * 1024 * 1024
LANES = 128
HALO = 16

ZC_CQ = 0
ZC_KR = 384
ZC_U = 512
ZC_GA = 1024
ZC_GB = 2048
ZC_CKV = 3072
Z_DIM = 3328
Z_OFFSETS = dict(cq=ZC_CQ, kr=ZC_KR, u=ZC_U, ga=ZC_GA, gb=ZC_GB, ckv=ZC_CKV)

W_IN_U, W_IN_CQ, W_IN_CKV, W_IN_KR, W_IN_GATES = (0, 512), (512, 896), (896, 1152), (1152, 1184), (1184, 3232)
W_IN_COLS = W_IN_GATES[1]
W_IN_SHARD = W_IN_COLS // N_CHIPS

ROWS_OF = dict(w_uq=72, w_uk=32, w_uv=32, p_pool=128, p_attn=128, w_out=256)


def _params(sem=None, **kw):
    return pltpu.CompilerParams(dimension_semantics=sem, vmem_limit_bytes=VMEM_LIMIT_BYTES, **kw)


def _tile(n, target, unit=LANES):
    best = None
    for t in range(unit, min(n, target) + 1, unit):
        if n % t == 0:
            best = t
    return best if best is not None and 4 * best >= min(n, target) else n


def _near_tile(n, target):
    cands = [t for t in range(LANES, n + 1, LANES) if n % t == 0]
    return min(cands, key=lambda t: abs(math.log(t / target))) if cands else n


def _mm(a, b, *, name, ta=False, tb=False, out_dtypes=(F32,), epilogue=None, extras=(), tm=1024, tn=1024, tk=1024,
        second=None, b_stack=False, out_stack=None):
    (k_dim, m_dim) = a.shape if ta else a.shape[::-1]
    if b_stack:
        g_b, k_b, n_shard = b.shape
        n_dim, k_b = (k_b, g_b * n_shard) if tb else (g_b * n_shard, k_b)
    else:
        (n_dim, k_b) = b.shape if tb else b.shape[::-1]
    assert k_dim == k_b, (a.shape, b.shape)
    n_unit = n_shard if b_stack and not tb else n_dim // out_stack if out_stack else n_dim
    k_unit = n_shard if b_stack and tb else k_dim
    tm, tn, tk = _near_tile(m_dim, tm), _near_tile(n_unit, tn), _near_tile(k_unit, tk)
    nk = k_dim // tk
    n_extra, n_out = len(extras), len(out_dtypes)
    n_lhs = 4 if second else 2
    dims = (((0 if ta else 1,), (1 if tb else 0,)), ((), ()))
    if epilogue is None:
        epilogue = lambda acc: (acc,) * n_out

    def body(*refs):
        operand_refs, rest = refs[:n_lhs], refs[n_lhs:]
        extra_refs, out_refs = rest[:n_extra], rest[n_extra:n_extra + n_out]

        def product():
            total = None
            for a_ref, b_ref in zip(operand_refs[0::2], operand_refs[1::2]):
                part = lax.dot_general(a_ref[...].astype(BF16), b_ref[...].astype(BF16), dims, preferred_element_type=F32)
                total = part if total is None else total + part
            return total

        def finish(acc):
            outs = epilogue(acc, *[r[...] for r in extra_refs])
            for o_ref, o in zip(out_refs, outs):
                o_ref[...] = o.astype(o_ref.dtype)

        if nk == 1:
            finish(product())
            return
        acc_ref = rest[-1]
        k = pl.program_id(2)

        @pl.when(k == 0)
        def _():
            acc_ref[...] = product()

        @pl.when((k > 0) & (k < nk - 1))
        def _():
            acc_ref[...] += product()

        @pl.when(k == nk - 1)
        def _():
            finish(acc_ref[...] + product())

    a_spec = pl.BlockSpec((tk, tm), lambda i, j, k: (k, i)) if ta else pl.BlockSpec((tm, tk), lambda i, j, k: (i, k))
    if b_stack and tb:
        per = n_shard // tk
        b_spec = pl.BlockSpec((None, tn, tk), lambda i, j, k: (k // per, j, k % per))
    elif b_stack:
        per = n_shard // tn
        b_spec = pl.BlockSpec((None, tk, tn), lambda i, j, k: (j // per, k, j % per))
    elif tb:
        b_spec = pl.BlockSpec((tn, tk), lambda i, j, k: (j, k))
    else:
        b_spec = pl.BlockSpec((tk, tn), lambda i, j, k: (k, j))
    if out_stack:
        per_out = (n_dim // out_stack) // tn
        out_spec = pl.BlockSpec((None, tm, tn), lambda i, j, k: (j // per_out, i, j % per_out))
        out_dims = (out_stack, m_dim, n_dim // out_stack)
    else:
        out_spec = pl.BlockSpec((tm, tn), lambda i, j, k: (i, j))
        out_dims = (m_dim, n_dim)
    extra_specs = []
    for arr, kind in extras:
        if kind == "tile":
            extra_specs.append(pl.BlockSpec((tm, tn), lambda i, j, k: (i, j)))
        elif isinstance(kind, tuple):
            extra_specs.append(pl.BlockSpec((tm, tn), functools.partial(lambda i, j, k, c: (i, j + c), c=kind[1])))
        elif kind == "row":
            extra_specs.append(pl.BlockSpec((1, tn), lambda i, j, k: (0, j)))
        elif kind == "col":
            extra_specs.append(pl.BlockSpec((tm, 1), lambda i, j, k: (i, 0)))
        else:
            assert kind == "table", kind
            extra_specs.append(pl.BlockSpec((tm, LANES), lambda i, j, k: (i, 0)))
    return pl.pallas_call(
        body,
        name=name,
        grid=(m_dim // tm, n_dim // tn, nk),
        in_specs=[a_spec, b_spec] * (n_lhs // 2) + extra_specs,
        out_specs=[out_spec for _ in out_dtypes],
        out_shape=[jax.ShapeDtypeStruct(out_dims, dt) for dt in out_dtypes],
        scratch_shapes=[pltpu.VMEM((tm, tn), F32)] if nk > 1 else [],
        compiler_params=_params(("parallel", "parallel", "arbitrary")),
    )(a, b, *(second or ()), *[arr for arr, _ in extras])


def _mm_sum(pieces, b, offsets, *, name, tm=1024, tn=1024):
    m_dim, n_dim = pieces[0].shape[0], b.shape[1]
    tm, tn = _near_tile(m_dim, tm), _near_tile(n_dim, tn)
    n_pieces = len(pieces)

    def body(*refs):
        total = None
        for a_ref, b_ref in zip(refs[:n_pieces], refs[n_pieces:2 * n_pieces]):
            part = jnp.dot(a_ref[...], b_ref[...], preferred_element_type=F32)
            total = part if total is None else total + part
        refs[-1][...] = total

    a_specs = [pl.BlockSpec((tm, p.shape[1]), lambda i, j: (i, 0)) for p in pieces]
    b_specs = [pl.BlockSpec((p.shape[1], tn), functools.partial(lambda i, j, blk: (blk, j), blk=off // p.shape[1]))
               for p, off in zip(pieces, offsets)]
    return pl.pallas_call(
        body, name=name, grid=(m_dim // tm, n_dim // tn),
        in_specs=a_specs + b_specs,
        out_specs=pl.BlockSpec((tm, tn), lambda i, j: (i, j)),
        out_shape=jax.ShapeDtypeStruct((m_dim, n_dim), F32),
        compiler_params=_params(("parallel", "parallel")),
    )(*pieces, *[b] * n_pieces)


def _rows(s):
    return min(512, s)


def _rope_tables(pos_col, inv_freq_lanes, *, name):
    s = pos_col.shape[0]
    tb = _rows(s)

    def body(pos_ref, f_ref, cos_ref, sin_ref):
        ang = pos_ref[...].astype(F32) * f_ref[...]
        lane = lax.broadcasted_iota(jnp.int32, ang.shape, 1)
        on = (lane >= QK_NOPE) & (lane < QK_DIM)
        cos_ref[...] = jnp.where(on, jnp.cos(ang), 0.0)
        sin_ref[...] = jnp.where(on, jnp.sin(ang), 0.0)

    return pl.pallas_call(
        body, name=name, grid=(s // tb,),
        in_specs=[pl.BlockSpec((tb, 1), lambda i: (i, 0)), pl.BlockSpec((1, LANES), lambda i: (0, 0))],
        out_specs=[pl.BlockSpec((tb, LANES), lambda i: (i, 0))] * 2,
        out_shape=[jax.ShapeDtypeStruct((s, LANES), F32)] * 2,
        compiler_params=_params(("parallel",)),
    )(pos_col, inv_freq_lanes)


def _rotate_half(x):
    lane = lax.broadcasted_iota(jnp.int32, x.shape, 1)
    half = QK_ROPE // 2
    first = (lane >= QK_NOPE) & (lane < QK_NOPE + half)
    second = (lane >= QK_NOPE + half) & (lane < QK_DIM)
    return jnp.where(first, -pltpu.roll(x, LANES - half, 1), jnp.where(second, pltpu.roll(x, half, 1), 0.0))


def _norm_mod(x, g, sc, sh, *, name):
    s, d = x.shape
    tb = _rows(s)

    def body(x_ref, g_ref, sc_ref, sh_ref, h_ref, r_ref):
        xv = x_ref[...]
        r = lax.rsqrt(jnp.mean(xv * xv, axis=-1, keepdims=True) + EPS)
        r_ref[...] = r
        h_ref[...] = (((xv * r) * g_ref[...]) * (1.0 + sc_ref[...]) + sh_ref[...]).astype(BF16)

    vec = pl.BlockSpec((1, d), lambda i: (0, 0))
    return pl.pallas_call(
        body, name=name, grid=(s // tb,),
        in_specs=[pl.BlockSpec((tb, d), lambda i: (i, 0)), vec, vec, vec],
        out_specs=[pl.BlockSpec((tb, d), lambda i: (i, 0)), pl.BlockSpec((tb, 1), lambda i: (i, 0))],
        out_shape=[jax.ShapeDtypeStruct((s, d), BF16), jax.ShapeDtypeStruct((s, 1), F32)],
        compiler_params=_params(("parallel",)),
    )(x, g, sc, sh)


def _window_sums(ext, sign):
    n = ext.shape[0]
    sums, cur, k = [], ext, 1
    for _ in POOL_WINDOWS:
        cur = cur + pltpu.roll(cur, k if sign > 0 else n - k, 0)
        sums.append(cur)
        k *= 2
    return sums


def _mixer_pre(z, cos_t, sin_t, w_pool, pool_scale, gq, gkv, *, name):
    s = z.shape[0]
    tb = _rows(s)
    hb = tb // HALO

    def body(zcq_ref, zkr_ref, zu_ref, zuh_ref, zckv_ref, cos_ref, sin_ref, wp_ref, ps_ref, gq_ref, gkv_ref,
             p_ref, yp_ref, cq_ref, ckv_ref, kr_ref, rq_ref, rkv_ref):
        i = pl.program_id(0)
        u = zu_ref[...].astype(F32)
        halo = jnp.where(i > 0, zuh_ref[...].astype(F32), 0.0)
        ext = jnp.concatenate([halo, u], axis=0)
        t = i * tb + lax.broadcasted_iota(jnp.int32, (tb, 1), 0)
        for g, (w, sw) in enumerate(zip(POOL_WINDOWS, _window_sums(ext, +1))):
            cols = slice(g * POOL_GROUP, (g + 1) * POOL_GROUP)
            cnt = jnp.minimum(t + 1, w).astype(F32)
            pg = (sw[HALO:, cols] / cnt - u[:, cols]).astype(BF16)
            p_ref[:, cols] = pg
            yg = jnp.dot(pg, wp_ref[g].astype(BF16), preferred_element_type=F32)
            yp_ref[:, cols] = (yg * ps_ref[:, cols]).astype(BF16)

        def rms(x_ref, g_ref, out_ref, r_ref):
            xv = x_ref[...].astype(F32)
            r = lax.rsqrt(jnp.mean(xv * xv, axis=-1, keepdims=True) + EPS)
            r_ref[...] = r
            out_ref[...] = ((xv * r) * g_ref[...]).astype(BF16)

        rms(zcq_ref, gq_ref, cq_ref, rq_ref)
        rms(zckv_ref, gkv_ref, ckv_ref, rkv_ref)
        kr = zkr_ref[...].astype(F32)
        kr_ref[...] = (kr * cos_ref[...] + _rotate_half(kr) * sin_ref[...]).astype(BF16)

    def zcol(width, off):
        return pl.BlockSpec((tb, width), lambda i: (i, off // width))

    def full(a):
        return pl.BlockSpec(a.shape, lambda i: (0,) * a.ndim)

    def out(width, dt):
        return pl.BlockSpec((tb, width), lambda i: (i, 0)), jax.ShapeDtypeStruct((s, width), dt)

    outs = [out(POOL_DIM, BF16), out(POOL_DIM, BF16), out(Q_LORA, BF16), out(KV_LORA, BF16), out(LANES, BF16),
            out(1, F32), out(1, F32)]
    return pl.pallas_call(
        body, name=name, grid=(s // tb,),
        in_specs=[zcol(Q_LORA, ZC_CQ), zcol(LANES, ZC_KR), zcol(POOL_DIM, ZC_U),
                  pl.BlockSpec((HALO, POOL_DIM), lambda i: (jnp.maximum(i * hb - 1, 0), ZC_U // POOL_DIM)),
                  zcol(KV_LORA, ZC_CKV),
                  pl.BlockSpec((tb, LANES), lambda i: (i, 0)), pl.BlockSpec((tb, LANES), lambda i: (i, 0)),
                  full(w_pool), full(pool_scale), full(gq), full(gkv)],
        out_specs=[o[0] for o in outs], out_shape=[o[1] for o in outs],
        compiler_params=_params(("parallel",)),
    )(z, z, z, z, z, cos_t, sin_t, w_pool, pool_scale, gq, gkv)


def _sigmoid(x):
    return 1.0 / (1.0 + jnp.exp(-x.astype(F32)))


ATTN_SCALE = 1.0 / math.sqrt(QK_DIM)
NEG_BIG = -1e30


LOG2_E = math.log2(math.e)
EXP2_SCALE = ATTN_SCALE * LOG2_E
NT_DIMS = (((1,), (1,)), ((), ()))
TN_DIMS = (((0,), (0,)), ((), ()))


def _on_or_below_diagonal(t):
    return lax.broadcasted_iota(jnp.int32, (t, t), 0) >= lax.broadcasted_iota(jnp.int32, (t, t), 1)


HEADS_PER_STEP = 2
HEAD_COLS = [slice(g * HEAD_PAD, (g + 1) * HEAD_PAD) for g in range(HEADS_PER_STEP)]


def _attn_fwd(q, k, v, *, name):
    s = q.shape[0]
    t = _rows(s)
    wide = HEADS_PER_STEP * HEAD_PAD

    def body(q_ref, k_ref, v_ref, o_ref, lse_ref):
        qi = pl.program_id(1)
        qs = [q_ref[:, cols] for cols in HEAD_COLS]

        def block(j, carry, diagonal):
            rows = pl.ds(pl.multiple_of(j * t, t), t)
            out = []
            for qv, cols, (m, l, acc) in zip(qs, HEAD_COLS, carry):
                sc = lax.dot_general(qv, k_ref[rows, cols], NT_DIMS, preferred_element_type=F32)
                if diagonal:
                    sc = jnp.where(_on_or_below_diagonal(t), sc, NEG_BIG)
                m_new = jnp.maximum(m, jnp.max(sc, axis=-1, keepdims=True))
                p = jnp.exp2((sc - m_new) * EXP2_SCALE)
                alpha = jnp.exp2((m - m_new) * EXP2_SCALE)
                l = alpha * l + jnp.sum(p, axis=-1, keepdims=True)
                acc = alpha * acc + jnp.dot(p.astype(BF16), v_ref[rows, cols], preferred_element_type=F32)
                out.append((m_new, l, acc))
            return tuple(out)

        init = tuple((jnp.full((t, 1), -jnp.inf, F32), jnp.zeros((t, 1), F32), jnp.zeros((t, HEAD_PAD), F32))
                     for _ in HEAD_COLS)
        carry = lax.fori_loop(0, qi, lambda j, c: block(j, c, False), init)
        for g, (cols, (m, l, acc)) in enumerate(zip(HEAD_COLS, block(qi, carry, True))):
            o_ref[:, cols] = (acc / l).astype(BF16)
            lse_ref[g] = m * ATTN_SCALE + jnp.log(l)

    q_spec = pl.BlockSpec((t, wide), lambda h, i: (i, h))
    kv_spec = pl.BlockSpec((s, wide), lambda h, i: (0, h))
    return pl.pallas_call(
        body, name=name, grid=(N_HEADS // HEADS_PER_STEP, s // t),
        in_specs=[q_spec, kv_spec, kv_spec],
        out_specs=[q_spec, pl.BlockSpec((HEADS_PER_STEP, t, 1), lambda h, i: (h, i, 0))],
        out_shape=[jax.ShapeDtypeStruct((s, N_HEADS * HEAD_PAD), BF16), jax.ShapeDtypeStruct((N_HEADS, s, 1), F32)],
        compiler_params=_params(("parallel", "parallel")),
    )(q, k, v)


def _attn_bwd(q, k, v, do, o, lse, cos_t, sin_t, *, name):
    s = q.shape[0]
    t = _rows(s)
    nt = s // t

    def body(q_ref, k_ref, v_ref, do_ref, o_ref, lse_ref, cos_ref, sin_ref, dql_ref, dk_ref, dv_ref, dks_ref,
             dq_ref, dl_ref):
        kj = pl.program_id(1)

        @pl.when(kj == 0)
        def _():
            dq_ref[...] = jnp.zeros_like(dq_ref)

            def delta(i, carry):
                rows = pl.ds(pl.multiple_of(i * t, t), t)
                for g, cols in enumerate(HEAD_COLS):
                    dl_ref[g, rows, :] = jnp.sum(do_ref[rows, cols].astype(F32) * o_ref[rows, cols].astype(F32),
                                                 axis=-1, keepdims=True)
                return carry

            lax.fori_loop(0, nt, delta, 0)

        kvs = [(k_ref[:, cols], v_ref[:, cols]) for cols in HEAD_COLS]

        def block(i, carry, diagonal):
            rows = pl.ds(pl.multiple_of(i * t, t), t)
            out = []
            for g, (cols, (kv, vv), (dk, dv)) in enumerate(zip(HEAD_COLS, kvs, carry)):
                qv, dov = q_ref[rows, cols], do_ref[rows, cols]
                sc = lax.dot_general(qv, kv, NT_DIMS, preferred_element_type=F32)
                p = jnp.exp2(sc * EXP2_SCALE - lse_ref[g, rows, :] * LOG2_E)
                if diagonal:
                    p = jnp.where(_on_or_below_diagonal(t), p, 0.0)
                dp = lax.dot_general(dov, vv, NT_DIMS, preferred_element_type=F32)
                ds = (p * (dp - dl_ref[g, rows, :])).astype(BF16)
                dv = dv + lax.dot_general(p.astype(BF16), dov, TN_DIMS, preferred_element_type=F32)
                dk = dk + lax.dot_general(ds, qv, TN_DIMS, preferred_element_type=F32)
                dq_ref[rows, cols] += jnp.dot(ds, kv, preferred_element_type=F32) * ATTN_SCALE
                out.append((dk, dv))
            return tuple(out)

        zero = jnp.zeros((t, HEAD_PAD), F32)
        carry = block(kj, tuple((zero, zero) for _ in HEAD_COLS), True)
        dk_sum = None
        for cols, (dk, dv) in zip(HEAD_COLS, lax.fori_loop(kj + 1, nt, lambda i, c: block(i, c, False), carry)):
            dk = dk * ATTN_SCALE
            dk_ref[:, cols] = dk.astype(BF16)
            dv_ref[:, cols] = dv.astype(BF16)
            dk_sum = dk if dk_sum is None else dk_sum + dk
        dks_ref[...] = dk_sum

        @pl.when(kj == nt - 1)
        def _():
            def rope_bwd(i, carry):
                rows = pl.ds(pl.multiple_of(i * t, t), t)
                sin = sin_ref[rows, :]
                lane = lax.broadcasted_iota(jnp.int32, sin.shape, 1)
                cos_q = cos_ref[rows, :] + jnp.where(lane < QK_NOPE, 1.0, 0.0)
                for cols in HEAD_COLS:
                    dqv = dq_ref[rows, cols]
                    dql_ref[rows, cols] = (dqv * cos_q - _rotate_half(dqv * sin)).astype(BF16)
                return carry

            lax.fori_loop(0, nt, rope_bwd, 0)

    heads_wide = HEADS_PER_STEP * HEAD_PAD
    full_spec = pl.BlockSpec((s, heads_wide), lambda h, j: (0, h))
    kv_spec = pl.BlockSpec((t, heads_wide), lambda h, j: (j, h))
    vec_spec = pl.BlockSpec((HEADS_PER_STEP, s, 1), lambda h, j: (h, 0, 0))
    table_spec = pl.BlockSpec((s, LANES), lambda h, j: (0, 0))
    wide = jax.ShapeDtypeStruct((s, N_HEADS * HEAD_PAD), BF16)
    n_steps = N_HEADS // HEADS_PER_STEP
    return pl.pallas_call(
        body, name=name, grid=(n_steps, nt),
        in_specs=[full_spec, kv_spec, kv_spec, full_spec, full_spec, vec_spec, table_spec, table_spec],
        out_specs=[full_spec, kv_spec, kv_spec, pl.BlockSpec((None, t, HEAD_PAD), lambda h, j: (h, j, 0))],
        out_shape=[wide, wide, wide, jax.ShapeDtypeStruct((n_steps, s, HEAD_PAD), F32)],
        scratch_shapes=[pltpu.VMEM((s, heads_wide), F32), pltpu.VMEM((HEADS_PER_STEP, s, 1), F32)],
        compiler_params=_params(("parallel", "arbitrary")),
    )(q, k, v, do, o, lse, cos_t, sin_t)


def _acc_specs(widths):
    return ([pl.BlockSpec((1, w), lambda i: (0, 0)) for w in widths],
            [jax.ShapeDtypeStruct((1, w), F32) for w in widths])


def _gate_grads(dxv, m_ref, gate_ref, dm_ref, dgate_ref):
    dm_ref[...] = (dxv * gate_ref[...]).astype(BF16)
    dgate_ref[...] += jnp.sum(dxv * m_ref[...], axis=0, keepdims=True)


def _final_loss(x, g, target, m, gate, *, name):
    s, d = x.shape
    tb = _rows(s)

    def body(x_ref, g_ref, t_ref, m_ref, gate_ref, dx_ref, loss_ref, dg_ref, dm_ref, dgate_ref):
        @pl.when(pl.program_id(0) == 0)
        def _():
            loss_ref[...] = jnp.zeros_like(loss_ref)
            dg_ref[...] = jnp.zeros_like(dg_ref)
            dgate_ref[...] = jnp.zeros_like(dgate_ref)

        xv = x_ref[...]
        r = lax.rsqrt(jnp.mean(xv * xv, axis=-1, keepdims=True) + EPS)
        xn = xv * r
        err = xn * g_ref[...] - t_ref[...]
        loss_ref[...] += 0.5 * jnp.sum(jnp.mean(err * err, axis=-1, keepdims=True), axis=0, keepdims=True)
        dy = err / d
        dg_ref[...] += jnp.sum(dy * xn, axis=0, keepdims=True)
        dxn = dy * g_ref[...]
        dxv = r * (dxn - xn * jnp.mean(dxn * xn, axis=-1, keepdims=True))
        dx_ref[...] = dxv
        _gate_grads(dxv, m_ref, gate_ref, dm_ref, dgate_ref)

    blk = pl.BlockSpec((tb, d), lambda i: (i, 0))
    vec = pl.BlockSpec((1, d), lambda i: (0, 0))
    acc_specs, acc_shapes = _acc_specs((LANES, d))
    return pl.pallas_call(
        body, name=name, grid=(s // tb,),
        in_specs=[blk, vec, blk, blk, vec],
        out_specs=[blk] + acc_specs + [blk, vec],
        out_shape=[jax.ShapeDtypeStruct((s, d), F32)] + acc_shapes + [jax.ShapeDtypeStruct((s, d), BF16),
                                                                     jax.ShapeDtypeStruct((1, d), F32)],
        compiler_params=_params(("arbitrary",)),
    )(x, g, target, m, gate)


def _norm_mod_bwd(dh, x, r, g, sc, dx_skip, *, name, gate=None):
    s, d = x.shape
    tb = _rows(s)
    nb = s // tb
    n_gate = 2 if gate else 0

    def body(dh_ref, x_ref, r_ref, g_ref, sc_ref, skip_ref, *rest):
        gate_refs, (dx_ref, dg_ref, dsc_ref, dsh_ref) = rest[:n_gate], rest[n_gate:n_gate + 4]
        gate_outs, da_sc = rest[n_gate + 4:-1], rest[-1]
        i = pl.program_id(0)

        @pl.when(i == 0)
        def _():
            da_sc[...] = jnp.zeros_like(da_sc)
            dsh_ref[...] = jnp.zeros_like(dsh_ref)
            if gate:
                gate_outs[1][...] = jnp.zeros_like(gate_outs[1])

        dhv, rv = dh_ref[...], r_ref[...]
        xn = x_ref[...] * rv
        dsh_ref[...] += jnp.sum(dhv, axis=0, keepdims=True)
        da_sc[...] += jnp.sum(dhv * xn, axis=0, keepdims=True)
        dxn = dhv * (g_ref[...] * (1.0 + sc_ref[...]))
        dxv = skip_ref[...] + rv * (dxn - xn * jnp.mean(dxn * xn, axis=-1, keepdims=True))
        dx_ref[...] = dxv
        if gate:
            _gate_grads(dxv, *gate_refs, *gate_outs)

        @pl.when(i == nb - 1)
        def _():
            dg_ref[...] = da_sc[...] * (1.0 + sc_ref[...])
            dsc_ref[...] = da_sc[...] * g_ref[...]

    blk = pl.BlockSpec((tb, d), lambda i: (i, 0))
    vec = pl.BlockSpec((1, d), lambda i: (0, 0))
    acc_specs, acc_shapes = _acc_specs((d, d, d))
    gate_specs = [blk, vec] if gate else []
    gate_shapes = [jax.ShapeDtypeStruct((s, d), BF16), jax.ShapeDtypeStruct((1, d), F32)] if gate else []
    return pl.pallas_call(
        body, name=name, grid=(nb,),
        in_specs=[blk, blk, pl.BlockSpec((tb, 1), lambda i: (i, 0)), vec, vec, blk] + gate_specs,
        out_specs=[blk] + acc_specs + gate_specs,
        out_shape=[jax.ShapeDtypeStruct((s, d), F32)] + acc_shapes + gate_shapes,
        scratch_shapes=[pltpu.VMEM((1, d), F32)],
        compiler_params=_params(("arbitrary",)),
    )(dh, x, r, g, sc, dx_skip, *(gate or ()))


def _pool_bwd(dyp, p, w_pool, pool_scale, *, name):
    s = dyp.shape[0]
    tb = _rows(s)
    nb = s // tb
    hb = tb // HALO
    nt_dims = (((1,), (1,)), ((), ()))
    tn_dims = (((0,), (0,)), ((), ()))

    def body(dy_ref, dyn_ref, p_ref, wp_ref, ps_ref, du_ref, gwp_ref, gps_ref):
        i = pl.program_id(0)

        @pl.when(i == 0)
        def _():
            gwp_ref[...] = jnp.zeros_like(gwp_ref)
            gps_ref[...] = jnp.zeros_like(gps_ref)

        cur = dy_ref[...]
        nxt = jnp.where(i < nb - 1, dyn_ref[...], 0.0)
        dpw = (jnp.concatenate([cur, nxt], axis=0) * ps_ref[...]).astype(BF16)
        t = i * tb + lax.broadcasted_iota(jnp.int32, (tb + HALO, 1), 0)
        for g, w in enumerate(POOL_WINDOWS):
            cols = slice(g * POOL_GROUP, (g + 1) * POOL_GROUP)
            wg = wp_ref[g].astype(BF16)
            dp = lax.dot_general(dpw[:, cols], wg, nt_dims, preferred_element_type=F32)
            e = dp / jnp.minimum(t + 1, w).astype(F32)
            lead = _window_sums(e, -1)[g]
            du_ref[:, cols] = (lead[:tb] - dp[:tb]).astype(BF16)
            pg = p_ref[:, cols]
            pw = jnp.dot(pg, wg, preferred_element_type=F32)
            gps_ref[:, cols] += jnp.sum(cur[:, cols] * pw, axis=0, keepdims=True)
            gwp_ref[g] += lax.dot_general(pg, dpw[:tb, cols], tn_dims, preferred_element_type=F32)

    blk = pl.BlockSpec((tb, POOL_DIM), lambda i: (i, 0))
    return pl.pallas_call(
        body, name=name, grid=(nb,),
        in_specs=[blk, pl.BlockSpec((HALO, POOL_DIM), lambda i: (jnp.minimum((i + 1) * hb, s // HALO - 1), 0)), blk,
                  pl.BlockSpec(w_pool.shape, lambda i: (0, 0, 0)), pl.BlockSpec((1, POOL_DIM), lambda i: (0, 0))],
        out_specs=[blk, pl.BlockSpec(w_pool.shape, lambda i: (0, 0, 0)), pl.BlockSpec((1, POOL_DIM), lambda i: (0, 0))],
        out_shape=[jax.ShapeDtypeStruct((s, POOL_DIM), BF16), jax.ShapeDtypeStruct(w_pool.shape, F32),
                   jax.ShapeDtypeStruct((1, POOL_DIM), F32)],
        compiler_params=_params(("arbitrary",)),
    )(dyp, dyp, p, w_pool, pool_scale)


def _key_bwd(dk_sums, cos_t, sin_t, *, name):
    n, s, _ = dk_sums.shape
    tb = _rows(s)

    def body(dk_ref, cos_ref, sin_ref, dkr_ref):
        tot = dk_ref[0]
        for h in range(1, n):
            tot = tot + dk_ref[h]
        dkr_ref[...] = (tot * cos_ref[...] - _rotate_half(tot * sin_ref[...])).astype(BF16)

    tab = pl.BlockSpec((tb, LANES), lambda i: (i, 0))
    return pl.pallas_call(
        body, name=name, grid=(s // tb,),
        in_specs=[pl.BlockSpec((n, tb, LANES), lambda i: (0, i, 0)), tab, tab], out_specs=tab,
        out_shape=jax.ShapeDtypeStruct((s, LANES), BF16),
        compiler_params=_params(("parallel",)),
    )(dk_sums, cos_t, sin_t)


def _rms_bwd(dy, z, z_off, r, g, *, name):
    s, n = dy.shape
    tb = _rows(s)

    def body(dy_ref, x_ref, r_ref, g_ref, dx_ref, dg_ref):
        @pl.when(pl.program_id(0) == 0)
        def _():
            dg_ref[...] = jnp.zeros_like(dg_ref)

        dyv, rv = dy_ref[...], r_ref[...]
        xn = x_ref[...].astype(F32) * rv
        dg_ref[...] += jnp.sum(dyv * xn, axis=0, keepdims=True)
        dxn = dyv * g_ref[...]
        dx_ref[...] = (rv * (dxn - xn * jnp.mean(dxn * xn, axis=-1, keepdims=True))).astype(BF16)

    blk = pl.BlockSpec((tb, n), lambda i: (i, 0))
    acc_specs, acc_shapes = _acc_specs((n,))
    return pl.pallas_call(
        body, name=name, grid=(s // tb,),
        in_specs=[blk, pl.BlockSpec((tb, n), lambda i: (i, z_off // n)), pl.BlockSpec((tb, 1), lambda i: (i, 0)),
                  pl.BlockSpec((1, n), lambda i: (0, 0))],
        out_specs=[blk] + acc_specs, out_shape=[jax.ShapeDtypeStruct((s, n), BF16)] + acc_shapes,
        compiler_params=_params(("arbitrary",)),
    )(dy, z, r, g)


def _silu(c, *, name):
    def body(c_ref, out_ref):
        cv = c_ref[...]
        out_ref[...] = (cv * _sigmoid(cv)).astype(BF16)

    return pl.pallas_call(body, name=name, out_shape=jax.ShapeDtypeStruct(c.shape, BF16),
                          compiler_params=_params())(c)


def _sum_slots(a, n, *, name, out_dtype=F32):
    _, rows, cols = a.shape
    tr = _tile(rows, 256, 8)

    def body(a_ref, out_ref):
        tot = a_ref[0].astype(F32)
        for j in range(1, n):
            tot = tot + a_ref[j].astype(F32)
        out_ref[...] = tot.astype(out_dtype)

    return pl.pallas_call(
        body, name=name, grid=(rows // tr,),
        in_specs=[pl.BlockSpec((n, tr, cols), lambda i: (0, i, 0))],
        out_specs=pl.BlockSpec((tr, cols), lambda i: (i, 0)),
        out_shape=jax.ShapeDtypeStruct((rows, cols), out_dtype),
        compiler_params=_params(("parallel",)),
    )(a)


def _add2_stacked(a, b, stacked, l, *, name):
    rows, cols = a.shape
    tr = _tile(rows, 256, 8)

    def body(a_ref, b_ref, *rest):
        rest[-1][...] = a_ref[...] + b_ref[...]

    blk = pl.BlockSpec((tr, cols), lambda i: (i, 0))
    carried = [] if stacked is None else [stacked]
    return pl.pallas_call(
        body, name=name, grid=(rows // tr,),
        in_specs=[blk, blk] + [pl.BlockSpec(memory_space=pl.ANY) for _ in carried],
        out_specs=pl.BlockSpec((None, tr, cols), lambda i: (l, i, 0)),
        out_shape=jax.ShapeDtypeStruct((DEPTH, rows, cols), F32),
        input_output_aliases={2: 0} if carried else {},
        compiler_params=_params(("parallel",)),
    )(a, b, *carried)


def _adamw(w, g, m, v, *, name):
    shape = w.shape
    if w.ndim == 2:
        w, g, m, v = (a.reshape((1,) + shape) for a in (w, g, m, v))
    layers, rows, cols = w.shape
    tr = _tile(rows, max(8, (1 << 18) // cols), 8)
    c1 = 1.0 - ADAM_B1 ** ADAM_STEP
    c2 = 1.0 - ADAM_B2 ** ADAM_STEP

    def body(w_ref, g_ref, m_ref, v_ref, d_ref, nm_ref, nv_ref):
        gv = g_ref[...]
        nm = ADAM_B1 * m_ref[...] + (1.0 - ADAM_B1) * gv
        nv = ADAM_B2 * v_ref[...] + (1.0 - ADAM_B2) * (gv * gv)
        nm_ref[...] = nm
        nv_ref[...] = nv
        d_ref[...] = -ADAM_LR * ((nm / c1) / (jnp.sqrt(nv / c2) + ADAM_EPS) + ADAM_WD * w_ref[...])

    blk = pl.BlockSpec((None, tr, cols), lambda l, i: (l, i, 0))
    outs = pl.pallas_call(
        body, name=name, grid=(layers, rows // tr), in_specs=[blk] * 4, out_specs=[blk] * 3,
        out_shape=[jax.ShapeDtypeStruct((layers, rows, cols), F32)] * 3,
        compiler_params=_params(("parallel", "parallel")),
    )(w, g, m, v)
    return [o.reshape(shape) for o in outs]


def _adamw_summed(w, halves, m, v, *, name):
    layers, rows, cols = w.shape
    tr = _tile(rows, max(8, (1 << 18) // cols), 8)
    tc = _tile(cols, max(LANES, (1 << 18) // tr))
    nb = rows // tr
    c1 = 1.0 - ADAM_B1 ** ADAM_STEP
    c2 = 1.0 - ADAM_B2 ** ADAM_STEP

    def body(w_ref, m_ref, v_ref, *rest):
        half_refs, (g_ref, d_ref, nm_ref, nv_ref) = rest[:2 * layers], rest[2 * layers:]
        for k in range(layers):
            @pl.when(pl.program_id(0) == k)
            def _(k=k):
                gv = half_refs[2 * k][...] + half_refs[2 * k + 1][...]
                g_ref[...] = gv
                nm = ADAM_B1 * m_ref[...] + (1.0 - ADAM_B1) * gv
                nv = ADAM_B2 * v_ref[...] + (1.0 - ADAM_B2) * (gv * gv)
                nm_ref[...] = nm
                nv_ref[...] = nv
                d_ref[...] = -ADAM_LR * ((nm / c1) / (jnp.sqrt(nv / c2) + ADAM_EPS) + ADAM_WD * w_ref[...])

    def half_spec(k):
        def index(l, i, j):
            at_k = l == k
            return (jnp.where(at_k, i, jnp.where(l < k, 0, nb - 1)), jnp.where(at_k, j, jnp.where(l < k, 0, cols // tc - 1)))
        return pl.BlockSpec((tr, tc), index)

    blk = pl.BlockSpec((None, tr, tc), lambda l, i, j: (l, i, j))
    return pl.pallas_call(
        body, name=name, grid=(layers, nb, cols // tc),
        in_specs=[blk] * 3 + [half_spec(k) for k in range(layers) for _ in range(2)], out_specs=[blk] * 4,
        out_shape=[jax.ShapeDtypeStruct((layers, rows, cols), F32)] * 4,
        compiler_params=_params(("parallel", "parallel", "parallel")),
    )(w, m, v, *[half for pair in halves for half in pair])


def _coords():
    return lax.axis_index("x"), lax.axis_index("y"), lax.axis_index("c")


def _other_chips(x, y):
    return [(1 - x, y), (x, 1 - y), (1 - x, 1 - y)]


def _all_gather_small(blk, *, name, swaps=()):
    m_per, n = blk.shape
    n_swaps = len(swaps)

    def body(x_ref, *refs):
        swap_srcs, out_ref, swap_outs = refs[:n_swaps], refs[n_swaps], refs[n_swaps + 1:2 * n_swaps + 1]
        send_sems, recv_sems, local_sem = refs[2 * n_swaps + 1:2 * n_swaps + 4]
        x, y, c = _coords()
        me, sibling = (x, y, c), (x, y, 1 - c)
        chips = _other_chips(x, y)
        swapping = [pltpu.make_async_remote_copy(src_ref=src, dst_ref=dst, send_sem=refs[-2].at[k], recv_sem=refs[-1].at[k],
                                                 device_id=sibling, device_id_type=MESH)
                    for k, (src, dst) in enumerate(zip(swap_srcs, swap_outs))]
        for cp in swapping:
            cp.start()

        def rows(px, py, pc):
            return out_ref.at[pl.ds((4 * px + 2 * py + pc) * m_per, m_per), :]

        def copy(k, block, to, src=None):
            return pltpu.make_async_remote_copy(
                src_ref=rows(*block) if src is None else src, dst_ref=rows(*block),
                send_sem=send_sems.at[k], recv_sem=recv_sems.at[k], device_id=to, device_id_type=MESH)

        mine = pltpu.make_async_copy(x_ref, rows(*me), local_sem)
        mine.start()
        first = [copy(0, me, sibling, src=x_ref)]
        first += [copy(1 + j, me, (*chip, c), src=x_ref) for j, chip in enumerate(chips)]
        for cp in first:
            cp.start()
        passed = [copy(4 + j, (*chip, c), sibling) for j, chip in enumerate(chips)]
        for j, chip in enumerate(chips):
            copy(1 + j, (*chip, c), me).wait_recv()
            passed[j].start()
        copy(0, sibling, me).wait_recv()
        for j, chip in enumerate(chips):
            copy(4 + j, (*chip, 1 - c), me).wait_recv()
        for cp in first + passed:
            cp.wait_send()
        mine.wait()
        for cp in swapping:
            cp.wait()

    any_spec = pl.BlockSpec(memory_space=pl.ANY)
    outs = pl.pallas_call(
        body, name=name,
        out_shape=[jax.ShapeDtypeStruct((N_DEV * m_per, n), blk.dtype)] + [jax.ShapeDtypeStruct(a.shape, a.dtype)
                                                                           for a in swaps],
        in_specs=[pl.BlockSpec(memory_space=pltpu.VMEM)] + [any_spec] * n_swaps,
        out_specs=[pl.BlockSpec(memory_space=pltpu.VMEM)] + [any_spec] * n_swaps,
        scratch_shapes=[pltpu.SemaphoreType.DMA((7,)), pltpu.SemaphoreType.DMA((7,)), pltpu.SemaphoreType.DMA]
        + ([pltpu.SemaphoreType.DMA((n_swaps,)), pltpu.SemaphoreType.DMA((n_swaps,))] if swaps else []),
        compiler_params=_params(),
    )(blk, *swaps)
    return (outs[0], outs[1:]) if swaps else outs[0]


HBM_SPEC = pl.BlockSpec(memory_space=pltpu.HBM)
SEM_SPEC = pl.BlockSpec(memory_space=pltpu.SEMAPHORE)
DATAFLOW = pltpu.SideEffectType.DATAFLOW_SIDE_EFFECTING


def _chip_copies(src_ref, land_ref, send_sems, recv_sems, scatter):
    x, y, c = _coords()
    my = 2 * x + y
    outgoing, incoming = [], []
    for k, (px, py) in enumerate(_other_chips(x, y)):
        peer = 2 * px + py

        def copy(src_slot, dst_slot):
            return pltpu.make_async_remote_copy(
                src_ref=src_ref.at[src_slot] if scatter else src_ref, dst_ref=land_ref.at[dst_slot],
                send_sem=send_sems.at[k], recv_sem=recv_sems.at[k], device_id=(px, py, c), device_id_type=MESH)

        outgoing.append(copy(peer, my))
        incoming.append(copy(my, peer))
    return outgoing, incoming


def _exchange_start(srcs, *, name, scatter):
    n = len(srcs)
    land_shapes = [src.shape if scatter else (N_CHIPS,) + src.shape for src in srcs]

    def body(*refs):
        for k in range(n):
            send_sems, recv_sems = refs[2 * n + 4 * k], refs[2 * n + 4 * k + 1]
            outgoing, _ = _chip_copies(refs[k], refs[n + k], send_sems, recv_sems, scatter)
            for cp in outgoing:
                cp.start()
        refs[-1][...] = jnp.zeros_like(refs[-1])

    out_shape, out_specs, aliases = [], [], {}
    for k, (src, land_shape) in enumerate(zip(srcs, land_shapes)):
        out_shape += [pltpu.SemaphoreType.DMA((N_CHIPS - 1,)), pltpu.SemaphoreType.DMA((N_CHIPS - 1,)),
                      pltpu.HBM(src.shape, src.dtype), pltpu.HBM(land_shape, src.dtype)]
        out_specs += [SEM_SPEC, SEM_SPEC, HBM_SPEC, HBM_SPEC]
        aliases.update({k: 4 * k + 2, n + k: 4 * k + 3})
    outs = pl.pallas_call(
        body, name=name,
        out_shape=tuple(out_shape) + (jax.ShapeDtypeStruct((8, LANES), F32),),
        in_specs=(HBM_SPEC,) * (2 * n),
        out_specs=tuple(out_specs) + (pl.BlockSpec(memory_space=pltpu.VMEM),),
        input_output_aliases=aliases,
        compiler_params=pltpu.CompilerParams(has_side_effects=DATAFLOW),
    )(*[pltpu.with_memory_space_constraint(src, pltpu.HBM) for src in srcs],
      *[pltpu.with_memory_space_constraint(lax.empty(shape, src.dtype), pltpu.HBM)
        for src, shape in zip(srcs, land_shapes)])
    return [tuple(outs[4 * k:4 * k + 4]) for k in range(n)], outs[-1]


def _exchange_wait(started, after, *, name, scatter):
    send_sems, recv_sems, src_thru, land_thru = started

    def body(src_ref, land_ref, send_sems, recv_sems, after_ref, src_dead, got_ref):
        outgoing, incoming = _chip_copies(src_ref, land_ref, send_sems, recv_sems, scatter)
        for cp in outgoing:
            cp.wait_send()
        for cp in incoming:
            cp.wait_recv()

    return pl.pallas_call(
        body, name=name,
        out_shape=(pltpu.HBM(src_thru.shape, src_thru.dtype), pltpu.HBM(land_thru.shape, land_thru.dtype)),
        in_specs=(HBM_SPEC, HBM_SPEC, SEM_SPEC, SEM_SPEC, pl.BlockSpec(memory_space=pl.ANY)),
        out_specs=(HBM_SPEC, HBM_SPEC),
        input_output_aliases={0: 0, 1: 1},
        compiler_params=pltpu.CompilerParams(has_side_effects=DATAFLOW),
    )(src_thru, land_thru, send_sems, recv_sems, after)


def _pack_rows(a):
    return a.reshape(-1, D_MODEL)


def _pad_heads(w, width):
    r = w.shape[0]
    return jnp.pad(w, ((0, 0), (0, 0), (0, HEAD_PAD - width))).reshape(r, N_HEADS * HEAD_PAD)


MIX_NAMES = ("w_uq", "w_uk", "w_uv", "p_pool", "p_attn", "w_out")
GROUPS = ("in", "mix", "ff1", "ff2")


def _local_shard(weights, l, group, zero):
    if group == "mix":
        shard = jnp.concatenate([_pack_rows(weights[n][l]) for n in MIX_NAMES], axis=0)
    else:
        shard = weights[{"in": "w_in", "ff1": "w_ff1", "ff2": "w_ff2"}[group]][l]
    return (shard + zero).astype(BF16)


def _unpack_weights(gathered, group):
    def cols(a, k):
        return a.reshape(N_CHIPS, k, -1).transpose(1, 0, 2).reshape(k, -1)

    if group == "in":
        full = gathered.reshape(W_IN_COLS, D_MODEL)
        u, cq, ckv, kr, gates = (full[a:b] for a, b in (W_IN_U, W_IN_CQ, W_IN_CKV, W_IN_KR, W_IN_GATES))
        kr = jnp.pad(kr, ((QK_NOPE, HEAD_PAD - QK_DIM), (0, 0)))
        return dict(w_in=jnp.concatenate([cq, kr, u, gates, ckv], axis=0))
    if group == "ff1":
        return dict(w_ff1=gathered)
    if group == "ff2":
        return dict(w_ff2=gathered.reshape(D_FF, D_MODEL))

    def p_attn(a):
        full = cols(a, ATTN_DIM).reshape(N_HEADS, V_DIM, D_MODEL)
        return jnp.pad(full, ((0, 0), (0, HEAD_PAD - V_DIM), (0, 0))).reshape(N_HEADS * HEAD_PAD, D_MODEL)

    build = dict(
        w_uq=lambda a: _pad_heads(a.reshape(Q_LORA, N_HEADS, QK_DIM), QK_DIM),
        w_uk=lambda a: _pad_heads(a.reshape(KV_LORA, N_HEADS, QK_NOPE), QK_NOPE),
        w_uv=lambda a: _pad_heads(a.reshape(KV_LORA, N_HEADS, V_DIM), V_DIM),
        p_pool=lambda a: cols(a, POOL_DIM),
        p_attn=p_attn,
        w_out=lambda a: a.reshape(D_MODEL, D_MODEL),
    )
    w, off = {}, 0
    for name in MIX_NAMES:
        w[name] = build[name](gathered[:, off:off + ROWS_OF[name]])
        off += ROWS_OF[name]
    return w


def _pack_grads(g, group):
    def cols(a):
        k = a.shape[0]
        return a.reshape(k, N_CHIPS, -1).transpose(1, 0, 2).reshape(N_CHIPS, -1, D_MODEL)

    def rows(a):
        return a.reshape(N_CHIPS, -1, D_MODEL)

    def heads(width):
        return lambda a: rows(a.reshape(a.shape[0], N_HEADS, HEAD_PAD)[:, :, :width])

    if group == "in":
        full = jnp.concatenate([g["u"], g["cq"], g["ckv"], g["kr"][QK_NOPE:QK_DIM], g["ga"], g["gb"]], axis=0)
        return full.reshape(N_CHIPS, W_IN_SHARD, D_MODEL)
    if group == "ff1":
        return g["w_ff1"]
    if group == "ff2":
        return g["w_ff2"].reshape(N_CHIPS, D_FF // N_CHIPS, D_MODEL)

    def p_attn(a):
        return cols(a.reshape(N_HEADS, HEAD_PAD, D_MODEL)[:, :V_DIM].reshape(ATTN_DIM, D_MODEL))

    build = dict(w_uq=heads(QK_DIM), w_uk=heads(QK_NOPE), w_uv=heads(V_DIM), p_pool=cols, p_attn=p_attn, w_out=rows)
    return jnp.concatenate([build[name](g[name]) for name in MIX_NAMES], axis=1)


def _per_head(fn, acc, *tables):
    return jnp.concatenate([fn(acc[:, h * HEAD_PAD:(h + 1) * HEAD_PAD], *tables) for h in range(N_HEADS)], axis=1)


def _rope_head(a, cos, sin):
    lane = lax.broadcasted_iota(jnp.int32, a.shape, 1)
    return a * (cos + jnp.where(lane < QK_NOPE, 1.0, 0.0)) + _rotate_half(a) * sin


def _layer_fwd(l, x, mod, get_weights, small, cos_t, sin_t):
    sh1, sc1, g1, sh2, sc2, g2 = mod
    tag = f"_l{l}"
    h, r1 = _norm_mod(x, small["ln1_g"], sc1, sh1, name="norm1" + tag)
    w = dict(get_weights("in", h))
    (z,) = _mm(h, w["w_in"], tb=True, name="in_proj" + tag, out_dtypes=(BF16,))
    p, yp, cq, ckv, kr, rq, rkv = _mixer_pre(z, cos_t, sin_t, small["w_pool"], small["pool_scale"],
                                              small["q_norm_g"], small["kv_norm_g"], name="mixer_pre" + tag)
    w.update(get_weights("mix", yp))
    (ya,) = _mm(yp, w["p_pool"], name="pool_out" + tag, out_dtypes=(BF16,))
    (q,) = _mm(cq, w["w_uq"], name="q_proj" + tag, out_dtypes=(BF16,),
               epilogue=lambda acc, cos, sin: (_per_head(_rope_head, acc, cos, sin),),
               extras=((cos_t, "table"), (sin_t, "table")))
    (k,) = _mm(ckv, w["w_uk"], name="k_proj" + tag, out_dtypes=(BF16,),
               epilogue=lambda acc, krv: (_per_head(lambda a, b: a + b, acc, krv),), extras=((kr, "table"),))
    (v,) = _mm(ckv, w["w_uv"], name="v_proj" + tag, out_dtypes=(BF16,))
    o, lse = _attn_fwd(q, k, v, name="attn_fwd" + tag)
    yb, merged = _mm(o, w["p_attn"], name="attn_out" + tag, out_dtypes=(BF16, BF16), tm=512,
                     epilogue=lambda acc, ga, gb, yav: (acc, _sigmoid(ga) * yav + _sigmoid(gb) * acc),
                     extras=((z, ("tile", ZC_GA // D_MODEL)), (z, ("tile", ZC_GB // D_MODEL)), (ya, "tile")))
    mo, x1 = _mm(merged, w["w_out"], name="mix_out" + tag, out_dtypes=(BF16, F32),
                 epilogue=lambda acc, xr, g: (acc, xr + g * acc), extras=((x, "tile"), (g1, "row")))
    h2, r2 = _norm_mod(x1, small["ln2_g"], sc2, sh2, name="norm2" + tag)
    w.update(get_weights("ff1", merged))
    f, act = _mm(h2, w["w_ff1"], b_stack=True, name="ff1" + tag, out_dtypes=(BF16, BF16),
                 epilogue=lambda acc: (acc, jnp.square(jnp.maximum(acc, 0.0))))
    w.update(get_weights("ff2", act))
    m2, x2 = _mm(act, w["w_ff2"], name="ff2" + tag, out_dtypes=(BF16, F32),
                 epilogue=lambda acc, xr, g: (acc, xr + g * acc), extras=((x1, "tile"), (g2, "row")))
    saved = dict(x=x, h=h, r1=r1, z=z, p=p, yp=yp, cq=cq, ckv=ckv, rq=rq, rkv=rkv, ya=ya, q=q, k=k, v=v, o=o, lse=lse,
                 yb=yb, merged=merged, mo=mo, x1=x1, h2=h2, r2=r2, f=f, act=act, m2=m2)
    return x2, saved, w


def _merge_grads(dm, ga, gb, ya, yb):
    sa, sb = _sigmoid(ga), _sigmoid(gb)
    return dm * sa, dm * sb, dm * ya * (sa * (1.0 - sa)), dm * yb * (sb * (1.0 - sb))


def _layer_bwd(l, dx2, dm2, dg2, sv, mod, w, small, cos_t, sin_t, send_grads, gate_below):
    sh1, sc1, g1, sh2, sc2, g2 = mod
    tag = f"_l{l}"
    gw = {}
    (df,) = _mm(dm2, w["w_ff2"], tb=True, name="ff2_dx" + tag, out_dtypes=(BF16,),
                epilogue=lambda acc, f: (acc * (2.0 * jnp.maximum(f, 0.0)),), extras=((sv["f"], "tile"),))
    (g_ff2,) = _mm(sv["act"], dm2, ta=True, name="ff2_dw" + tag, out_dtypes=(BF16,))
    (g_ff1,) = _mm(sv["h2"], df, ta=True, out_stack=N_CHIPS, name="ff1_dw" + tag, out_dtypes=(BF16,))
    sc2 = sc2 + send_grads("ff2", dict(w_ff2=g_ff2)) + send_grads("ff1", dict(w_ff1=g_ff1))
    (dh2,) = _mm(df, w["w_ff1"], tb=True, b_stack=True, name="ff1_dx" + tag)
    dx1, dln2, dsc2, dsh2, dmo, dg1 = _norm_mod_bwd(dh2, sv["x1"], sv["r2"], small["ln2_g"], sc2, dx2,
                                                    gate=(sv["mo"], g1), name="norm2_bwd" + tag)
    dya, dyb, dga, dgb = _mm(dmo, w["w_out"], tb=True, name="mix_out_dx" + tag, out_dtypes=(BF16,) * 4, tm=512,
                             epilogue=_merge_grads,
                             extras=((sv["z"], ("tile", ZC_GA // D_MODEL)), (sv["z"], ("tile", ZC_GB // D_MODEL)),
                                     (sv["ya"], "tile"), (sv["yb"], "tile")))
    (gw["w_out"],) = _mm(sv["merged"], dmo, ta=True, name="mix_out_dw" + tag, out_dtypes=(BF16,))
    (gw["p_pool"],) = _mm(sv["yp"], dya, ta=True, name="pool_out_dw" + tag, out_dtypes=(BF16,))
    (dyp,) = _mm(dya, w["p_pool"], tb=True, name="pool_out_dx" + tag)
    du, g_w_pool, g_pool_scale = _pool_bwd(dyp, sv["p"], small["w_pool"], small["pool_scale"], name="pool_bwd" + tag)
    (gw["p_attn"],) = _mm(sv["o"], dyb, ta=True, name="attn_out_dw" + tag, out_dtypes=(BF16,))
    (do,) = _mm(dyb, w["p_attn"], tb=True, name="attn_out_dx" + tag, out_dtypes=(BF16,))
    dql, dkb, dv, dk_sums = _attn_bwd(sv["q"], sv["k"], sv["v"], do, sv["o"], sv["lse"], cos_t, sin_t,
                                      name="attn_bwd" + tag)
    dkr = _key_bwd(dk_sums, cos_t, sin_t, name="key_bwd" + tag)
    (gw["w_uq"],) = _mm(sv["cq"], dql, ta=True, name="q_proj_dw" + tag, out_dtypes=(BF16,))
    (gw["w_uk"],) = _mm(sv["ckv"], dkb, ta=True, name="k_proj_dw" + tag, out_dtypes=(BF16,))
    (gw["w_uv"],) = _mm(sv["ckv"], dv, ta=True, name="v_proj_dw" + tag, out_dtypes=(BF16,))
    (dcq,) = _mm(dql, w["w_uq"], tb=True, name="q_proj_dx" + tag)
    (dckv,) = _mm(dkb, w["w_uk"], tb=True, second=(dv, w["w_uv"]), name="kv_proj_dx" + tag)
    q_norm_g = small["q_norm_g"] + send_grads("mix", gw)
    dcq_raw, g_qn = _rms_bwd(dcq, sv["z"], ZC_CQ, sv["rq"], q_norm_g, name="q_norm_bwd" + tag)
    dckv_raw, g_kvn = _rms_bwd(dckv, sv["z"], ZC_CKV, sv["rkv"], small["kv_norm_g"], name="kv_norm_bwd" + tag)
    dz = dict(cq=dcq_raw, kr=dkr, u=du, ga=dga, gb=dgb, ckv=dckv_raw)
    g_in = {n: _mm(piece, sv["h"], ta=True, name=f"in_proj_dw_{n}" + tag, out_dtypes=(BF16,))[0]
            for n, piece in dz.items()}
    sc1 = sc1 + send_grads("in", g_in)
    dh = _mm_sum(list(dz.values()), w["w_in"], [Z_OFFSETS[n] for n in dz], name="in_proj_dx" + tag)
    dx, dln1, dsc1, dsh1, *below = _norm_mod_bwd(dh, sv["x"], sv["r1"], small["ln1_g"], sc1, dx1, gate=gate_below,
                                                 name="norm1_bwd" + tag)
    dmod = jnp.concatenate([dsh1, dsc1, dg1, dsh2, dsc2, dg2], axis=0)
    gsmall = dict(ln1_g=dln1, ln2_g=dln2, q_norm_g=g_qn, kv_norm_g=g_kvn, w_pool=g_w_pool, pool_scale=g_pool_scale)
    return dx, dmod, gsmall, below


SMALL_LOSS = 6
SMALL_SINGLES = 16
SMALL_POOL = 24
SMALL_POOL_ROWS = len(POOL_WINDOWS) * POOL_GROUP * POOL_GROUP // D_MODEL
SMALL_ROWS = SMALL_POOL + DEPTH * SMALL_POOL_ROWS


def _pack_small(parts, *, name):
    def body(*refs):
        out_ref = refs[-1]
        out_ref[...] = jnp.zeros_like(out_ref)
        for ref, (_, row) in zip(refs[:-1], parts):
            out_ref[row:row + ref.shape[0], :] = ref[...]

    return pl.pallas_call(body, name=name, out_shape=jax.ShapeDtypeStruct((SMALL_ROWS, D_MODEL), F32),
                          compiler_params=_params())(*[a for a, _ in parts])


def kernel(x, c, positions, ln1_g, ln2_g, w_ada, b_ada, w_in, q_norm_g, w_uq, kv_norm_g, w_uk, w_uv, w_pool, pool_scale, p_pool, p_attn, w_out, w_ff1, w_ff2, final_g, loss_target, m_ln1_g, m_ln2_g, m_w_ada, m_b_ada, m_w_in, m_q_norm_g, m_w_uq, m_kv_norm_g, m_w_uk, m_w_uv, m_w_pool, m_pool_scale, m_p_pool, m_p_attn, m_w_out, m_w_ff1, m_w_ff2, m_final_g, v_ln1_g, v_ln2_g, v_w_ada, v_b_ada, v_w_in, v_q_norm_g, v_w_uq, v_kv_norm_g, v_w_uk, v_w_uv, v_w_pool, v_pool_scale, v_p_pool, v_p_attn, v_w_out, v_w_ff1, v_w_ff2, v_final_g):
    weights = dict(ln1_g=ln1_g, ln2_g=ln2_g, w_ada=w_ada, b_ada=b_ada, w_in=w_in, q_norm_g=q_norm_g, w_uq=w_uq,
                   kv_norm_g=kv_norm_g, w_uk=w_uk, w_uv=w_uv, w_pool=w_pool, pool_scale=pool_scale, p_pool=p_pool,
                   p_attn=p_attn, w_out=w_out, w_ff1=w_ff1, w_ff2=w_ff2, final_g=final_g)
    moms = dict(ln1_g=m_ln1_g, ln2_g=m_ln2_g, w_ada=m_w_ada, b_ada=m_b_ada, w_in=m_w_in, q_norm_g=m_q_norm_g,
                w_uq=m_w_uq, kv_norm_g=m_kv_norm_g, w_uk=m_w_uk, w_uv=m_w_uv, w_pool=m_w_pool,
                pool_scale=m_pool_scale, p_pool=m_p_pool, p_attn=m_p_attn, w_out=m_w_out, w_ff1=m_w_ff1,
                w_ff2=m_w_ff2, final_g=m_final_g)
    vels = dict(ln1_g=v_ln1_g, ln2_g=v_ln2_g, w_ada=v_w_ada, b_ada=v_b_ada, w_in=v_w_in, q_norm_g=v_q_norm_g,
                w_uq=v_w_uq, kv_norm_g=v_kv_norm_g, w_uk=v_w_uk, w_uv=v_w_uv, w_pool=v_w_pool,
                pool_scale=v_pool_scale, p_pool=v_p_pool, p_attn=v_p_attn, w_out=v_w_out, w_ff1=v_w_ff1,
                w_ff2=v_w_ff2, final_g=v_final_g)
    order = list(weights)
    for table in (weights, moms, vels):
        table["w_in"] = jnp.swapaxes(table["w_in"], 1, 2)
    seq = x.shape[1]
    my_chip = 2 * lax.axis_index("x") + lax.axis_index("y")
    my_dev = 2 * my_chip + lax.axis_index("c")
    ada_cols = w_ada.shape[2]

    small = [dict(ln1_g=ln1_g[l:l + 1], ln2_g=ln2_g[l:l + 1], q_norm_g=q_norm_g[l:l + 1], kv_norm_g=kv_norm_g[l:l + 1],
                  w_pool=w_pool[l], pool_scale=pool_scale[l:l + 1]) for l in range(DEPTH)]

    c_all = _all_gather_small(jnp.pad(c, ((0, 7), (0, 0))), name="cond_all_gather")
    c_act = _silu(c_all, name="cond_silu")
    b_mine = lax.dynamic_slice_in_dim(b_ada, my_chip * ada_cols, ada_cols, axis=1).reshape(1, DEPTH * ada_cols)
    (mod_cat,) = _mm(c_act, w_ada, b_stack=True, name="ada_fwd", epilogue=lambda acc, b: (acc + b,),
                     extras=((b_mine, "row"),))
    mod_mine = jnp.concatenate([mod_cat[::8, l * ada_cols:(l + 1) * ada_cols] for l in range(DEPTH)], axis=0)
    mod_all = _all_gather_small(mod_mine, name="mod_all_gather").reshape(N_DEV, DEPTH, N_DEV, ada_cols)

    zero = mod_all[0, 0, 0, 0] * 0.0
    keys = [(l, group) for l in range(DEPTH) for group in GROUPS]
    exchanges, token = _exchange_start([_local_shard(weights, l, group, zero) for l, group in keys],
                                       name="weights_send", scatter=False)
    started = dict(zip(keys, exchanges))
    pin = token[0:1, 0:1]

    def gathered_weights(l, group, after):
        mine, land = _exchange_wait(started[l, group], after, name=f"weights_wait_l{l}_{group}", scatter=False)
        land = lax.dynamic_update_slice_in_dim(land, mine[None], my_chip, axis=0)
        return _unpack_weights(land, group)

    mods = []
    for l in range(DEPTH):
        row = jnp.concatenate([lax.dynamic_index_in_dim(mod_all[2 * j, l], my_dev, axis=0, keepdims=True)
                               for j in range(N_CHIPS)], axis=1) + pin
        mods.append([row[:, i * D_MODEL:(i + 1) * D_MODEL] for i in range(N_MOD)])

    inv_freq = ROPE_THETA ** (-jnp.arange(0, QK_ROPE, 2, dtype=F32) / QK_ROPE)
    freq_lanes = jnp.concatenate([jnp.zeros((QK_NOPE,), F32), inv_freq, inv_freq,
                                  jnp.zeros((HEAD_PAD - QK_DIM,), F32)]).reshape(1, LANES)
    cos_t, sin_t = _rope_tables(positions.reshape(seq, 1), freq_lanes, name="rope_tables")

    xs, saved, wl = x.reshape(seq, D_MODEL), [], []
    for l in range(DEPTH):
        xs, sv, w_l = _layer_fwd(l, xs, mods[l], functools.partial(gathered_weights, l), small[l], cos_t, sin_t)
        saved.append(sv)
        wl.append(w_l)
    dx, loss_part, g_final, dm2, dg2 = _final_loss(xs, final_g.reshape(1, D_MODEL), loss_target.reshape(seq, D_MODEL),
                                                   saved[-1]["m2"], mods[-1][5], name="final_loss")

    sent, pending = [], []
    send_after = {(0, "ff1"), (0, "mix"), (0, "in")}

    def send_grads(l, group, g):
        pending.append((l, group, _pack_grads(g, group)))
        if (l, group) not in send_after:
            return jnp.zeros((1, 1), F32)
        exchanges, token_g = _exchange_start([gpack for _, _, gpack in pending], name=f"grads_send_l{l}_{group}",
                                             scatter=True)
        sent.extend((item[0], item[1], exchange) for item, exchange in zip(pending, exchanges))
        pending.clear()
        return token_g[0:1, 0:1]

    dmod, gsmall = [None] * DEPTH, [None] * DEPTH
    for l in reversed(range(DEPTH)):
        gate_below = (saved[l - 1]["m2"], mods[l - 1][5]) if l > 0 else None
        dx, dmod[l], gsmall[l], below = _layer_bwd(l, dx, dm2, dg2, saved[l], mods[l], wl[l], small[l], cos_t, sin_t,
                                                   functools.partial(send_grads, l), gate_below)
        dm2, dg2 = below if below else (None, None)
    grads = dict(x=dx.reshape(1, seq, D_MODEL))

    big_parts, after = [], dmod[0]
    for l, group, started_g in sent:
        tg = f"_l{l}_{group}"
        gpack, land = _exchange_wait(started_g, after, name="grads_wait" + tg, scatter=True)
        own = lax.dynamic_index_in_dim(gpack, my_chip, axis=0, keepdims=True)
        land = lax.dynamic_update_slice_in_dim(land, own, my_chip, axis=0)
        big_parts.append(_sum_slots(land, N_CHIPS, name="grads_sum_chips" + tg))
        after = big_parts[-1]

    def lanes(a):
        flat = a.reshape(1, -1)
        return jnp.pad(flat, ((0, 0), (0, D_MODEL - flat.shape[1])))

    singles = [gsmall[0]["ln1_g"], gsmall[1]["ln1_g"], gsmall[0]["ln2_g"], gsmall[1]["ln2_g"], g_final,
               lanes(jnp.concatenate([gsmall[l]["pool_scale"] for l in range(DEPTH)], axis=1)),
               lanes(jnp.concatenate([gsmall[l]["q_norm_g"] for l in range(DEPTH)], axis=1)),
               lanes(jnp.concatenate([gsmall[l]["kv_norm_g"] for l in range(DEPTH)], axis=1))]
    parts = [(dmod[0], 0), (lanes(loss_part), SMALL_LOSS), (dmod[1], 8)]
    parts += [(a, SMALL_SINGLES + i) for i, a in enumerate(singles)]
    parts += [(gsmall[l]["w_pool"].reshape(-1, D_MODEL), SMALL_POOL + l * SMALL_POOL_ROWS) for l in range(DEPTH)]
    small_all, big_others = _all_gather_small(_pack_small(parts, name="small_grads_pack"), swaps=big_parts,
                                              name="small_grads_all_gather")
    small_all = small_all.reshape(N_DEV, SMALL_ROWS, D_MODEL)
    ssum = _sum_slots(small_all, N_DEV, name="small_grads_sum")
    loss = ssum[SMALL_LOSS, 0]
    grads["b_ada"] = jnp.stack([ssum[8 * l:8 * l + N_MOD] for l in range(DEPTH)]).reshape(DEPTH, N_MOD * D_MODEL)
    grads["ln1_g"] = ssum[SMALL_SINGLES:SMALL_SINGLES + 2]
    grads["ln2_g"] = ssum[SMALL_SINGLES + 2:SMALL_SINGLES + 4]
    grads["final_g"] = ssum[SMALL_SINGLES + 4]
    grads["pool_scale"] = ssum[SMALL_SINGLES + 5].reshape(DEPTH, POOL_DIM)
    grads["q_norm_g"] = ssum[SMALL_SINGLES + 6, :DEPTH * Q_LORA].reshape(DEPTH, Q_LORA)
    grads["kv_norm_g"] = ssum[SMALL_SINGLES + 7, :DEPTH * KV_LORA].reshape(DEPTH, KV_LORA)
    grads["w_pool"] = ssum[SMALL_POOL:SMALL_ROWS].reshape(w_pool.shape)

    mix_sum, halves = None, {group: [None] * DEPTH for group in ("in", "ff1", "ff2")}
    for (l, group, _), part, other in zip(sent, big_parts, big_others):
        if group == "mix":
            mix_sum = _add2_stacked(part, other, mix_sum, l, name=f"grads_sum_cores_l{l}_mix")
        else:
            halves[group][l] = (part, other)
    off = 0
    for name in MIX_NAMES:
        grads[name] = mix_sum[:, off:off + ROWS_OF[name]].reshape(weights[name].shape)
        off += ROWS_OF[name]

    c_act_t = jnp.pad(c_act[::8].T, ((0, 0), (0, LANES - N_DEV)))
    d_mine = []
    for l in range(DEPTH):
        d_all = small_all[:, 8 * l:8 * l + N_MOD].reshape(N_DEV, N_MOD * D_MODEL)
        d_mine.append(lax.dynamic_slice_in_dim(d_all, my_chip * ada_cols, ada_cols, axis=1))
    d_cat = jnp.pad(jnp.concatenate(d_mine, axis=1), ((0, LANES - N_DEV), (0, 0)))
    (grads["w_ada"],) = _mm(c_act_t, d_cat, out_stack=DEPTH, name="ada_dw")

    def view(a):
        return a.reshape(1, -1) if a.ndim == 1 else a if a.ndim == 3 else a.reshape(-1, a.shape[-1])

    delta, new_m, new_v = {}, {}, {}
    for name in order:
        shape = weights[name].shape
        group = {"w_in": "in", "w_ff1": "ff1", "w_ff2": "ff2"}.get(name)
        if group:
            grads[name], d, nm, nv = _adamw_summed(weights[name], halves[group], moms[name], vels[name],
                                                   name="adamw_" + name)
        else:
            d, nm, nv = _adamw(view(weights[name]), view(grads[name]), view(moms[name]), view(vels[name]),
                               name="adamw_" + name)
        delta[name], new_m[name], new_v[name] = d.reshape(shape), nm.reshape(shape), nv.reshape(shape)
    for table in (grads, delta, new_m, new_v):
        table["w_in"] = jnp.swapaxes(table["w_in"], 1, 2)
    return (loss, grads["x"], *[grads[n] for n in order], *[delta[n] for n in order],
            *[new_m[n] for n in order], *[new_v[n] for n in order])
```

```python
import functools
import math

import jax
import jax.numpy as jnp
from jax import lax
from jax.experimental import pallas as pl
from jax.experimental.pallas import tpu as pltpu

F32 = jnp.float32
BF16 = jnp.bfloat16
MESH = pl.DeviceIdType.MESH

D_MODEL = 1024
DEPTH = 2
POOL_WINDOWS = (2, 4, 8, 16)
POOL_GROUP = 128
POOL_DIM = 512
N_HEADS = 8
QK_NOPE = 64
QK_ROPE = 32
QK_DIM = QK_NOPE + QK_ROPE
V_DIM = 64
HEAD_PAD = 128
Q_LORA = 384
KV_LORA = 256
ROPE_THETA = 10000.0
ATTN_DIM = N_HEADS * V_DIM
D_FF = 4 * D_MODEL
N_MOD = 6
EPS = 1e-6
N_CHIPS = 4
N_DEV = 8

ADAM_LR = 0.001
ADAM_B1 = 0.9
ADAM_B2 = 0.999
ADAM_EPS = 1e-08
ADAM_WD = 0.01
ADAM_STEP = 10

VMEM_LIMIT_BYTES = 56 * 1024 * 1024
LANES = 128
HALO = 16

ZC_CQ = 0
ZC_KR = 384
ZC_U = 512
ZC_GA = 1024
ZC_GB = 2048
ZC_CKV = 3072
Z_DIM = 3328
Z_OFFSETS = dict(cq=ZC_CQ, kr=ZC_KR, u=ZC_U, ga=ZC_GA, gb=ZC_GB, ckv=ZC_CKV)

W_IN_U, W_IN_CQ, W_IN_CKV, W_IN_KR, W_IN_GATES = (0, 512), (512, 896), (896, 1152), (1152, 1184), (1184, 3232)
W_IN_COLS = W_IN_GATES[1]
W_IN_SHARD = W_IN_COLS // N_CHIPS

ROWS_OF = dict(w_uq=72, w_uk=32, w_uv=32, p_pool=128, p_attn=128, w_out=256)


def _params(sem=None, **kw):
    return pltpu.CompilerParams(dimension_semantics=sem, vmem_limit_bytes=VMEM_LIMIT_BYTES, **kw)


def _tile(n, target, unit=LANES):
    best = None
    for t in range(unit, min(n, target) + 1, unit):
        if n % t == 0:
            best = t
    return best if best is not None and 4 * best >= min(n, target) else n


def _near_tile(n, target):
    cands = [t for t in range(LANES, n + 1, LANES) if n % t == 0]
    return min(cands, key=lambda t: abs(math.log(t / target))) if cands else n


def _mm(a, b, *, name, ta=False, tb=False, out_dtypes=(F32,), epilogue=None, extras=(), tm=1024, tn=1024, tk=1024,
        second=None, b_stack=False, out_stack=None):
    (k_dim, m_dim) = a.shape if ta else a.shape[::-1]
    if b_stack:
        g_b, k_b, n_shard = b.shape
        n_dim, k_b = (k_b, g_b * n_shard) if tb else (g_b * n_shard, k_b)
    else:
        (n_dim, k_b) = b.shape if tb else b.shape[::-1]
    assert k_dim == k_b, (a.shape, b.shape)
    n_unit = n_shard if b_stack and not tb else n_dim // out_stack if out_stack else n_dim
    k_unit = n_shard if b_stack and tb else k_dim
    tm, tn, tk = _near_tile(m_dim, tm), _near_tile(n_unit, tn), _near_tile(k_unit, tk)
    nk = k_dim // tk
    n_extra, n_out = len(extras), len(out_dtypes)
    n_lhs = 4 if second else 2
    dims = (((0 if ta else 1,), (1 if tb else 0,)), ((), ()))
    if epilogue is None:
        epilogue = lambda acc: (acc,) * n_out

    def body(*refs):
        operand_refs, rest = refs[:n_lhs], refs[n_lhs:]
        extra_refs, out_refs = rest[:n_extra], rest[n_extra:n_extra + n_out]

        def product():
            total = None
            for a_ref, b_ref in zip(operand_refs[0::2], operand_refs[1::2]):
                part = lax.dot_general(a_ref[...].astype(BF16), b_ref[...].astype(BF16), dims, preferred_element_type=F32)
                total = part if total is None else total + part
            return total

        def finish(acc):
            outs = epilogue(acc, *[r[...] for r in extra_refs])
            for o_ref, o in zip(out_refs, outs):
                o_ref[...] = o.astype(o_ref.dtype)

        if nk == 1:
            finish(product())
            return
        acc_ref = rest[-1]
        k = pl.program_id(2)

        @pl.when(k == 0)
        def _():
            acc_ref[...] = product()

        @pl.when((k > 0) & (k < nk - 1))
        def _():
            acc_ref[...] += product()

        @pl.when(k == nk - 1)
        def _():
            finish(acc_ref[...] + product())

    a_spec = pl.BlockSpec((tk, tm), lambda i, j, k: (k, i)) if ta else pl.BlockSpec((tm, tk), lambda i, j, k: (i, k))
    if b_stack and tb:
        per = n_shard // tk
        b_spec = pl.BlockSpec((None, tn, tk), lambda i, j, k: (k // per, j, k % per))
    elif b_stack:
        per = n_shard // tn
        b_spec = pl.BlockSpec((None, tk, tn), lambda i, j, k: (j // per, k, j % per))
    elif tb:
        b_spec = pl.BlockSpec((tn, tk), lambda i, j, k: (j, k))
    else:
        b_spec = pl.BlockSpec((tk, tn), lambda i, j, k: (k, j))
    if out_stack:
        per_out = (n_dim // out_stack) // tn
        out_spec = pl.BlockSpec((None, tm, tn), lambda i, j, k: (j // per_out, i, j % per_out))
        out_dims = (out_stack, m_dim, n_dim // out_stack)
    else:
        out_spec = pl.BlockSpec((tm, tn), lambda i, j, k: (i, j))
        out_dims = (m_dim, n_dim)
    extra_specs = []
    for arr, kind in extras:
        if kind == "tile":
            extra_specs.append(pl.BlockSpec((tm, tn), lambda i, j, k: (i, j)))
        elif isinstance(kind, tuple):
            extra_specs.append(pl.BlockSpec((tm, tn), functools.partial(lambda i, j, k, c: (i, j + c), c=kind[1])))
        elif kind == "row":
            extra_specs.append(pl.BlockSpec((1, tn), lambda i, j, k: (0, j)))
        elif kind == "col":
            extra_specs.append(pl.BlockSpec((tm, 1), lambda i, j, k: (i, 0)))
        else:
            assert kind == "table", kind
            extra_specs.append(pl.BlockSpec((tm, LANES), lambda i, j, k: (i, 0)))
    return pl.pallas_call(
        body,
        name=name,
        grid=(m_dim // tm, n_dim // tn, nk),
        in_specs=[a_spec, b_spec] * (n_lhs // 2) + extra_specs,
        out_specs=[out_spec for _ in out_dtypes],
        out_shape=[jax.ShapeDtypeStruct(out_dims, dt) for dt in out_dtypes],
        scratch_shapes=[pltpu.VMEM((tm, tn), F32)] if nk > 1 else [],
        compiler_params=_params(("parallel", "parallel", "arbitrary")),
    )(a, b, *(second or ()), *[arr for arr, _ in extras])


def _mm_sum(pieces, b, offsets, *, name, tm=1024, tn=1024):
    m_dim, n_dim = pieces[0].shape[0], b.shape[1]
    tm, tn = _near_tile(m_dim, tm), _near_tile(n_dim, tn)
    n_pieces = len(pieces)

    def body(*refs):
        total = None
        for a_ref, b_ref in zip(refs[:n_pieces], refs[n_pieces:2 * n_pieces]):
            part = jnp.dot(a_ref[...], b_ref[...], preferred_element_type=F32)
            total = part if total is None else total + part
        refs[-1][...] = total

    a_specs = [pl.BlockSpec((tm, p.shape[1]), lambda i, j: (i, 0)) for p in pieces]
    b_specs = [pl.BlockSpec((p.shape[1], tn), functools.partial(lambda i, j, blk: (blk, j), blk=off // p.shape[1]))
               for p, off in zip(pieces, offsets)]
    return pl.pallas_call(
        body, name=name, grid=(m_dim // tm, n_dim // tn),
        in_specs=a_specs + b_specs,
        out_specs=pl.BlockSpec((tm, tn), lambda i, j: (i, j)),
        out_shape=jax.ShapeDtypeStruct((m_dim, n_dim), F32),
        compiler_params=_params(("parallel", "parallel")),
    )(*pieces, *[b] * n_pieces)


def _rows(s):
    return min(512, s)


def _rope_tables(pos_col, inv_freq_lanes, *, name):
    s = pos_col.shape[0]
    tb = _rows(s)

    def body(pos_ref, f_ref, cos_ref, sin_ref):
        ang = pos_ref[...].astype(F32) * f_ref[...]
        lane = lax.broadcasted_iota(jnp.int32, ang.shape, 1)
        on = (lane >= QK_NOPE) & (lane < QK_DIM)
        cos_ref[...] = jnp.where(on, jnp.cos(ang), 0.0)
        sin_ref[...] = jnp.where(on, jnp.sin(ang), 0.0)

    return pl.pallas_call(
        body, name=name, grid=(s // tb,),
        in_specs=[pl.BlockSpec((tb, 1), lambda i: (i, 0)), pl.BlockSpec((1, LANES), lambda i: (0, 0))],
        out_specs=[pl.BlockSpec((tb, LANES), lambda i: (i, 0))] * 2,
        out_shape=[jax.ShapeDtypeStruct((s, LANES), F32)] * 2,
        compiler_params=_params(("parallel",)),
    )(pos_col, inv_freq_lanes)


def _rotate_half(x):
    lane = lax.broadcasted_iota(jnp.int32, x.shape, 1)
    half = QK_ROPE // 2
    first = (lane >= QK_NOPE) & (lane < QK_NOPE + half)
    second = (lane >= QK_NOPE + half) & (lane < QK_DIM)
    return jnp.where(first, -pltpu.roll(x, LANES - half, 1), jnp.where(second, pltpu.roll(x, half, 1), 0.0))


def _norm_mod(x, g, sc, sh, *, name):
    s, d = x.shape
    tb = _rows(s)

    def body(x_ref, g_ref, sc_ref, sh_ref, h_ref, r_ref):
        xv = x_ref[...]
        r = lax.rsqrt(jnp.mean(xv * xv, axis=-1, keepdims=True) + EPS)
        r_ref[...] = r
        h_ref[...] = (((xv * r) * g_ref[...]) * (1.0 + sc_ref[...]) + sh_ref[...]).astype(BF16)

    vec = pl.BlockSpec((1, d), lambda i: (0, 0))
    return pl.pallas_call(
        body, name=name, grid=(s // tb,),
        in_specs=[pl.BlockSpec((tb, d), lambda i: (i, 0)), vec, vec, vec],
        out_specs=[pl.BlockSpec((tb, d), lambda i: (i, 0)), pl.BlockSpec((tb, 1), lambda i: (i, 0))],
        out_shape=[jax.ShapeDtypeStruct((s, d), BF16), jax.ShapeDtypeStruct((s, 1), F32)],
        compiler_params=_params(("parallel",)),
    )(x, g, sc, sh)


def _window_sums(ext, sign):
    n = ext.shape[0]
    sums, cur, k = [], ext, 1
    for _ in POOL_WINDOWS:
        cur = cur + pltpu.roll(cur, k if sign > 0 else n - k, 0)
        sums.append(cur)
        k *= 2
    return sums


def _mixer_pre(z, cos_t, sin_t, w_pool, pool_scale, gq, gkv, *, name):
    s = z.shape[0]
    tb = _rows(s)
    hb = tb // HALO

    def body(zcq_ref, zkr_ref, zu_ref, zuh_ref, zckv_ref, cos_ref, sin_ref, wp_ref, ps_ref, gq_ref, gkv_ref,
             p_ref, yp_ref, cq_ref, ckv_ref, kr_ref, rq_ref, rkv_ref):
        i = pl.program_id(0)
        u = zu_ref[...].astype(F32)
        halo = jnp.where(i > 0, zuh_ref[...].astype(F32), 0.0)
        ext = jnp.concatenate([halo, u], axis=0)
        t = i * tb + lax.broadcasted_iota(jnp.int32, (tb, 1), 0)
        for g, (w, sw) in enumerate(zip(POOL_WINDOWS, _window_sums(ext, +1))):
            cols = slice(g * POOL_GROUP, (g + 1) * POOL_GROUP)
            cnt = jnp.minimum(t + 1, w).astype(F32)
            pg = (sw[HALO:, cols] / cnt - u[:, cols]).astype(BF16)
            p_ref[:, cols] = pg
            yg = jnp.dot(pg, wp_ref[g].astype(BF16), preferred_element_type=F32)
            yp_ref[:, cols] = (yg * ps_ref[:, cols]).astype(BF16)

        def rms(x_ref, g_ref, out_ref, r_ref):
            xv = x_ref[...].astype(F32)
            r = lax.rsqrt(jnp.mean(xv * xv, axis=-1, keepdims=True) + EPS)
            r_ref[...] = r
            out_ref[...] = ((xv * r) * g_ref[...]).astype(BF16)

        rms(zcq_ref, gq_ref, cq_ref, rq_ref)
        rms(zckv_ref, gkv_ref, ckv_ref, rkv_ref)
        kr = zkr_ref[...].astype(F32)
        kr_ref[...] = (kr * cos_ref[...] + _rotate_half(kr) * sin_ref[...]).astype(BF16)

    def zcol(width, off):
        return pl.BlockSpec((tb, width), lambda i: (i, off // width))

    def full(a):
        return pl.BlockSpec(a.shape, lambda i: (0,) * a.ndim)

    def out(width, dt):
        return pl.BlockSpec((tb, width), lambda i: (i, 0)), jax.ShapeDtypeStruct((s, width), dt)

    outs = [out(POOL_DIM, BF16), out(POOL_DIM, BF16), out(Q_LORA, BF16), out(KV_LORA, BF16), out(LANES, BF16),
            out(1, F32), out(1, F32)]
    return pl.pallas_call(
        body, name=name, grid=(s // tb,),
        in_specs=[zcol(Q_LORA, ZC_CQ), zcol(LANES, ZC_KR), zcol(POOL_DIM, ZC_U),
                  pl.BlockSpec((HALO, POOL_DIM), lambda i: (jnp.maximum(i * hb - 1, 0), ZC_U // POOL_DIM)),
                  zcol(KV_LORA, ZC_CKV),
                  pl.BlockSpec((tb, LANES), lambda i: (i, 0)), pl.BlockSpec((tb, LANES), lambda i: (i, 0)),
                  full(w_pool), full(pool_scale), full(gq), full(gkv)],
        out_specs=[o[0] for o in outs], out_shape=[o[1] for o in outs],
        compiler_params=_params(("parallel",)),
    )(z, z, z, z, z, cos_t, sin_t, w_pool, pool_scale, gq, gkv)


def _sigmoid(x):
    return 1.0 / (1.0 + jnp.exp(-x.astype(F32)))


ATTN_SCALE = 1.0 / math.sqrt(QK_DIM)
NEG_BIG = -1e30


LOG2_E = math.log2(math.e)
EXP2_SCALE = ATTN_SCALE * LOG2_E
NT_DIMS = (((1,), (1,)), ((), ()))
TN_DIMS = (((0,), (0,)), ((), ()))


def _on_or_below_diagonal(t):
    return lax.broadcasted_iota(jnp.int32, (t, t), 0) >= lax.broadcasted_iota(jnp.int32, (t, t), 1)


HEADS_PER_STEP = 2
HEAD_COLS = [slice(g * HEAD_PAD, (g + 1) * HEAD_PAD) for g in range(HEADS_PER_STEP)]


def _attn_fwd(q, k, v, *, name):
    s = q.shape[0]
    t = _rows(s)
    wide = HEADS_PER_STEP * HEAD_PAD

    def body(q_ref, k_ref, v_ref, o_ref, lse_ref):
        qi = pl.program_id(1)
        qs = [q_ref[:, cols] for cols in HEAD_COLS]

        def block(j, carry, diagonal):
            rows = pl.ds(pl.multiple_of(j * t, t), t)
            out = []
            for qv, cols, (m, l, acc) in zip(qs, HEAD_COLS, carry):
                sc = lax.dot_general(qv, k_ref[rows, cols], NT_DIMS, preferred_element_type=F32)
                if diagonal:
                    sc = jnp.where(_on_or_below_diagonal(t), sc, NEG_BIG)
                m_new = jnp.maximum(m, jnp.max(sc, axis=-1, keepdims=True))
                p = jnp.exp2((sc - m_new) * EXP2_SCALE)
                alpha = jnp.exp2((m - m_new) * EXP2_SCALE)
                l = alpha * l + jnp.sum(p, axis=-1, keepdims=True)
                acc = alpha * acc + jnp.dot(p.astype(BF16), v_ref[rows, cols], preferred_element_type=F32)
                out.append((m_new, l, acc))
            return tuple(out)

        init = tuple((jnp.full((t, 1), -jnp.inf, F32), jnp.zeros((t, 1), F32), jnp.zeros((t, HEAD_PAD), F32))
                     for _ in HEAD_COLS)
        carry = lax.fori_loop(0, qi, lambda j, c: block(j, c, False), init)
        for g, (cols, (m, l, acc)) in enumerate(zip(HEAD_COLS, block(qi, carry, True))):
            o_ref[:, cols] = (acc / l).astype(BF16)
            lse_ref[g] = m * ATTN_SCALE + jnp.log(l)

    q_spec = pl.BlockSpec((t, wide), lambda h, i: (i, h))
    kv_spec = pl.BlockSpec((s, wide), lambda h, i: (0, h))
    return pl.pallas_call(
        body, name=name, grid=(N_HEADS // HEADS_PER_STEP, s // t),
        in_specs=[q_spec, kv_spec, kv_spec],
        out_specs=[q_spec, pl.BlockSpec((HEADS_PER_STEP, t, 1), lambda h, i: (h, i, 0))],
        out_shape=[jax.ShapeDtypeStruct((s, N_HEADS * HEAD_PAD), BF16), jax.ShapeDtypeStruct((N_HEADS, s, 1), F32)],
        compiler_params=_params(("parallel", "parallel")),
    )(q, k, v)


def _attn_bwd(q, k, v, do, o, lse, cos_t, sin_t, *, name):
    s = q.shape[0]
    t = _rows(s)
    nt = s // t

    def body(q_ref, k_ref, v_ref, do_ref, o_ref, lse_ref, cos_ref, sin_ref, dql_ref, dk_ref, dv_ref, dks_ref,
             dq_ref, dl_ref):
        kj = pl.program_id(1)

        @pl.when(kj == 0)
        def _():
            dq_ref[...] = jnp.zeros_like(dq_ref)

            def delta(i, carry):
                rows = pl.ds(pl.multiple_of(i * t, t), t)
                for g, cols in enumerate(HEAD_COLS):
                    dl_ref[g, rows, :] = jnp.sum(do_ref[rows, cols].astype(F32) * o_ref[rows, cols].astype(F32),
                                                 axis=-1, keepdims=True)
                return carry

            lax.fori_loop(0, nt, delta, 0)

        kvs = [(k_ref[:, cols], v_ref[:, cols]) for cols in HEAD_COLS]

        def block(i, carry, diagonal):
            rows = pl.ds(pl.multiple_of(i * t, t), t)
            out = []
            for g, (cols, (kv, vv), (dk, dv)) in enumerate(zip(HEAD_COLS, kvs, carry)):
                qv, dov = q_ref[rows, cols], do_ref[rows, cols]
                sc = lax.dot_general(qv, kv, NT_DIMS, preferred_element_type=F32)
                p = jnp.exp2(sc * EXP2_SCALE - lse_ref[g, rows, :] * LOG2_E)
                if diagonal:
                    p = jnp.where(_on_or_below_diagonal(t), p, 0.0)
                dp = lax.dot_general(dov, vv, NT_DIMS, preferred_element_type=F32)
                ds = (p * (dp - dl_ref[g, rows, :])).astype(BF16)
                dv = dv + lax.dot_general(p.astype(BF16), dov, TN_DIMS, preferred_element_type=F32)
                dk = dk + lax.dot_general(ds, qv, TN_DIMS, preferred_element_type=F32)
                dq_ref[rows, cols] += jnp.dot(ds, kv, preferred_element_type=F32) * ATTN_SCALE
                out.append((dk, dv))
            return tuple(out)

        zero = jnp.zeros((t, HEAD_PAD), F32)
        carry = block(kj, tuple((zero, zero) for _ in HEAD_COLS), True)
        dk_sum = None
        for cols, (dk, dv) in zip(HEAD_COLS, lax.fori_loop(kj + 1, nt, lambda i, c: block(i, c, False), carry)):
            dk = dk * ATTN_SCALE
            dk_ref[:, cols] = dk.astype(BF16)
            dv_ref[:, cols] = dv.astype(BF16)
            dk_sum = dk if dk_sum is None else dk_sum + dk
        dks_ref[...] = dk_sum

        @pl.when(kj == nt - 1)
        def _():
            def rope_bwd(i, carry):
                rows = pl.ds(pl.multiple_of(i * t, t), t)
                sin = sin_ref[rows, :]
                lane = lax.broadcasted_iota(jnp.int32, sin.shape, 1)
                cos_q = cos_ref[rows, :] + jnp.where(lane < QK_NOPE, 1.0, 0.0)
                for cols in HEAD_COLS:
                    dqv = dq_ref[rows, cols]
                    dql_ref[rows, cols] = (dqv * cos_q - _rotate_half(dqv * sin)).astype(BF16)
                return carry

            lax.fori_loop(0, nt, rope_bwd, 0)

    heads_wide = HEADS_PER_STEP * HEAD_PAD
    full_spec = pl.BlockSpec((s, heads_wide), lambda h, j: (0, h))
    kv_spec = pl.BlockSpec((t, heads_wide), lambda h, j: (j, h))
    vec_spec = pl.BlockSpec((HEADS_PER_STEP, s, 1), lambda h, j: (h, 0, 0))
    table_spec = pl.BlockSpec((s, LANES), lambda h, j: (0, 0))
    wide = jax.ShapeDtypeStruct((s, N_HEADS * HEAD_PAD), BF16)
    n_steps = N_HEADS // HEADS_PER_STEP
    return pl.pallas_call(
        body, name=name, grid=(n_steps, nt),
        in_specs=[full_spec, kv_spec, kv_spec, full_spec, full_spec, vec_spec, table_spec, table_spec],
        out_specs=[full_spec, kv_spec, kv_spec, pl.BlockSpec((None, t, HEAD_PAD), lambda h, j: (h, j, 0))],
        out_shape=[wide, wide, wide, jax.ShapeDtypeStruct((n_steps, s, HEAD_PAD), F32)],
        scratch_shapes=[pltpu.VMEM((s, heads_wide), F32), pltpu.VMEM((HEADS_PER_STEP, s, 1), F32)],
        compiler_params=_params(("parallel", "arbitrary")),
    )(q, k, v, do, o, lse, cos_t, sin_t)


def _acc_specs(widths):
    return ([pl.BlockSpec((1, w), lambda i: (0, 0)) for w in widths],
            [jax.ShapeDtypeStruct((1, w), F32) for w in widths])


def _gate_grads(dxv, m_ref, gate_ref, dm_ref, dgate_ref):
    dm_ref[...] = (dxv * gate_ref[...]).astype(BF16)
    dgate_ref[...] += jnp.sum(dxv * m_ref[...], axis=0, keepdims=True)


def _final_loss(x, g, target, m, gate, *, name):
    s, d = x.shape
    tb = _rows(s)

    def body(x_ref, g_ref, t_ref, m_ref, gate_ref, dx_ref, loss_ref, dg_ref, dm_ref, dgate_ref):
        @pl.when(pl.program_id(0) == 0)
        def _():
            loss_ref[...] = jnp.zeros_like(loss_ref)
            dg_ref[...] = jnp.zeros_like(dg_ref)
            dgate_ref[...] = jnp.zeros_like(dgate_ref)

        xv = x_ref[...]
        r = lax.rsqrt(jnp.mean(xv * xv, axis=-1, keepdims=True) + EPS)
        xn = xv * r
        err = xn * g_ref[...] - t_ref[...]
        loss_ref[...] += 0.5 * jnp.sum(jnp.mean(err * err, axis=-1, keepdims=True), axis=0, keepdims=True)
        dy = err / d
        dg_ref[...] += jnp.sum(dy * xn, axis=0, keepdims=True)
        dxn = dy * g_ref[...]
        dxv = r * (dxn - xn * jnp.mean(dxn * xn, axis=-1, keepdims=True))
        dx_ref[...] = dxv
        _gate_grads(dxv, m_ref, gate_ref, dm_ref, dgate_ref)

    blk = pl.BlockSpec((tb, d), lambda i: (i, 0))
    vec = pl.BlockSpec((1, d), lambda i: (0, 0))
    acc_specs, acc_shapes = _acc_specs((LANES, d))
    return pl.pallas_call(
        body, name=name, grid=(s // tb,),
        in_specs=[blk, vec, blk, blk, vec],
        out_specs=[blk] + acc_specs + [blk, vec],
        out_shape=[jax.ShapeDtypeStruct((s, d), F32)] + acc_shapes + [jax.ShapeDtypeStruct((s, d), BF16),
                                                                     jax.ShapeDtypeStruct((1, d), F32)],
        compiler_params=_params(("arbitrary",)),
    )(x, g, target, m, gate)


def _norm_mod_bwd(dh, x, r, g, sc, dx_skip, *, name, gate=None):
    s, d = x.shape
    tb = _rows(s)
    nb = s // tb
    n_gate = 2 if gate else 0

    def body(dh_ref, x_ref, r_ref, g_ref, sc_ref, skip_ref, *rest):
        gate_refs, (dx_ref, dg_ref, dsc_ref, dsh_ref) = rest[:n_gate], rest[n_gate:n_gate + 4]
        gate_outs, da_sc = rest[n_gate + 4:-1], rest[-1]
        i = pl.program_id(0)

        @pl.when(i == 0)
        def _():
            da_sc[...] = jnp.zeros_like(da_sc)
            dsh_ref[...] = jnp.zeros_like(dsh_ref)
            if gate:
                gate_outs[1][...] = jnp.zeros_like(gate_outs[1])

        dhv, rv = dh_ref[...], r_ref[...]
        xn = x_ref[...] * rv
        dsh_ref[...] += jnp.sum(dhv, axis=0, keepdims=True)
        da_sc[...] += jnp.sum(dhv * xn, axis=0, keepdims=True)
        dxn = dhv * (g_ref[...] * (1.0 + sc_ref[...]))
        dxv = skip_ref[...] + rv * (dxn - xn * jnp.mean(dxn * xn, axis=-1, keepdims=True))
        dx_ref[...] = dxv
        if gate:
            _gate_grads(dxv, *gate_refs, *gate_outs)

        @pl.when(i == nb - 1)
        def _():
            dg_ref[...] = da_sc[...] * (1.0 + sc_ref[...])
            dsc_ref[...] = da_sc[...] * g_ref[...]

    blk = pl.BlockSpec((tb, d), lambda i: (i, 0))
    vec = pl.BlockSpec((1, d), lambda i: (0, 0))
    acc_specs, acc_shapes = _acc_specs((d, d, d))
    gate_specs = [blk, vec] if gate else []
    gate_shapes = [jax.ShapeDtypeStruct((s, d), BF16), jax.ShapeDtypeStruct((1, d), F32)] if gate else []
    return pl.pallas_call(
        body, name=name, grid=(nb,),
        in_specs=[blk, blk, pl.BlockSpec((tb, 1), lambda i: (i, 0)), vec, vec, blk] + gate_specs,
        out_specs=[blk] + acc_specs + gate_specs,
        out_shape=[jax.ShapeDtypeStruct((s, d), F32)] + acc_shapes + gate_shapes,
        scratch_shapes=[pltpu.VMEM((1, d), F32)],
        compiler_params=_params(("arbitrary",)),
    )(dh, x, r, g, sc, dx_skip, *(gate or ()))


def _pool_bwd(dyp, p, w_pool, pool_scale, *, name):
    s = dyp.shape[0]
    tb = _rows(s)
    nb = s // tb
    hb = tb // HALO
    nt_dims = (((1,), (1,)), ((), ()))
    tn_dims = (((0,), (0,)), ((), ()))

    def body(dy_ref, dyn_ref, p_ref, wp_ref, ps_ref, du_ref, gwp_ref, gps_ref):
        i = pl.program_id(0)

        @pl.when(i == 0)
        def _():
            gwp_ref[...] = jnp.zeros_like(gwp_ref)
            gps_ref[...] = jnp.zeros_like(gps_ref)

        cur = dy_ref[...]
        nxt = jnp.where(i < nb - 1, dyn_ref[...], 0.0)
        dpw = (jnp.concatenate([cur, nxt], axis=0) * ps_ref[...]).astype(BF16)
        t = i * tb + lax.broadcasted_iota(jnp.int32, (tb + HALO, 1), 0)
        for g, w in enumerate(POOL_WINDOWS):
            cols = slice(g * POOL_GROUP, (g + 1) * POOL_GROUP)
            wg = wp_ref[g].astype(BF16)
            dp = lax.dot_general(dpw[:, cols], wg, nt_dims, preferred_element_type=F32)
            e = dp / jnp.minimum(t + 1, w).astype(F32)
            lead = _window_sums(e, -1)[g]
            du_ref[:, cols] = (lead[:tb] - dp[:tb]).astype(BF16)
            pg = p_ref[:, cols]
            pw = jnp.dot(pg, wg, preferred_element_type=F32)
            gps_ref[:, cols] += jnp.sum(cur[:, cols] * pw, axis=0, keepdims=True)
            gwp_ref[g] += lax.dot_general(pg, dpw[:tb, cols], tn_dims, preferred_element_type=F32)

    blk = pl.BlockSpec((tb, POOL_DIM), lambda i: (i, 0))
    return pl.pallas_call(
        body, name=name, grid=(nb,),
        in_specs=[blk, pl.BlockSpec((HALO, POOL_DIM), lambda i: (jnp.minimum((i + 1) * hb, s // HALO - 1), 0)), blk,
                  pl.BlockSpec(w_pool.shape, lambda i: (0, 0, 0)), pl.BlockSpec((1, POOL_DIM), lambda i: (0, 0))],
        out_specs=[blk, pl.BlockSpec(w_pool.shape, lambda i: (0, 0, 0)), pl.BlockSpec((1, POOL_DIM), lambda i: (0, 0))],
        out_shape=[jax.ShapeDtypeStruct((s, POOL_DIM), BF16), jax.ShapeDtypeStruct(w_pool.shape, F32),
                   jax.ShapeDtypeStruct((1, POOL_DIM), F32)],
        compiler_params=_params(("arbitrary",)),
    )(dyp, dyp, p, w_pool, pool_scale)


def _key_bwd(dk_sums, cos_t, sin_t, *, name):
    n, s, _ = dk_sums.shape
    tb = _rows(s)

    def body(dk_ref, cos_ref, sin_ref, dkr_ref):
        tot = dk_ref[0]
        for h in range(1, n):
            tot = tot + dk_ref[h]
        dkr_ref[...] = (tot * cos_ref[...] - _rotate_half(tot * sin_ref[...])).astype(BF16)

    tab = pl.BlockSpec((tb, LANES), lambda i: (i, 0))
    return pl.pallas_call(
        body, name=name, grid=(s // tb,),
        in_specs=[pl.BlockSpec((n, tb, LANES), lambda i: (0, i, 0)), tab, tab], out_specs=tab,
        out_shape=jax.ShapeDtypeStruct((s, LANES), BF16),
        compiler_params=_params(("parallel",)),
    )(dk_sums, cos_t, sin_t)


def _rms_bwd(dy, z, z_off, r, g, *, name):
    s, n = dy.shape
    tb = _rows(s)

    def body(dy_ref, x_ref, r_ref, g_ref, dx_ref, dg_ref):
        @pl.when(pl.program_id(0) == 0)
        def _():
            dg_ref[...] = jnp.zeros_like(dg_ref)

        dyv, rv = dy_ref[...], r_ref[...]
        xn = x_ref[...].astype(F32) * rv
        dg_ref[...] += jnp.sum(dyv * xn, axis=0, keepdims=True)
        dxn = dyv * g_ref[...]
        dx_ref[...] = (rv * (dxn - xn * jnp.mean(dxn * xn, axis=-1, keepdims=True))).astype(BF16)

    blk = pl.BlockSpec((tb, n), lambda i: (i, 0))
    acc_specs, acc_shapes = _acc_specs((n,))
    return pl.pallas_call(
        body, name=name, grid=(s // tb,),
        in_specs=[blk, pl.BlockSpec((tb, n), lambda i: (i, z_off // n)), pl.BlockSpec((tb, 1), lambda i: (i, 0)),
                  pl.BlockSpec((1, n), lambda i: (0, 0))],
        out_specs=[blk] + acc_specs, out_shape=[jax.ShapeDtypeStruct((s, n), BF16)] + acc_shapes,
        compiler_params=_params(("arbitrary",)),
    )(dy, z, r, g)


def _silu(c, *, name):
    def body(c_ref, out_ref):
        cv = c_ref[...]
        out_ref[...] = (cv * _sigmoid(cv)).astype(BF16)

    return pl.pallas_call(body, name=name, out_shape=jax.ShapeDtypeStruct(c.shape, BF16),
                          compiler_params=_params())(c)


def _sum_slots(a, n, *, name, out_dtype=F32):
    _, rows, cols = a.shape
    tr = _tile(rows, 256, 8)

    def body(a_ref, out_ref):
        tot = a_ref[0].astype(F32)
        for j in range(1, n):
            tot = tot + a_ref[j].astype(F32)
        out_ref[...] = tot.astype(out_dtype)

    return pl.pallas_call(
        body, name=name, grid=(rows // tr,),
        in_specs=[pl.BlockSpec((n, tr, cols), lambda i: (0, i, 0))],
        out_specs=pl.BlockSpec((tr, cols), lambda i: (i, 0)),
        out_shape=jax.ShapeDtypeStruct((rows, cols), out_dtype),
        compiler_params=_params(("parallel",)),
    )(a)


def _add2_stacked(a, b, stacked, l, *, name):
    rows, cols = a.shape
    tr = _tile(rows, 256, 8)

    def body(a_ref, b_ref, *rest):
        rest[-1][...] = a_ref[...] + b_ref[...]

    blk = pl.BlockSpec((tr, cols), lambda i: (i, 0))
    carried = [] if stacked is None else [stacked]
    return pl.pallas_call(
        body, name=name, grid=(rows // tr,),
        in_specs=[blk, blk] + [pl.BlockSpec(memory_space=pl.ANY) for _ in carried],
        out_specs=pl.BlockSpec((None, tr, cols), lambda i: (l, i, 0)),
        out_shape=jax.ShapeDtypeStruct((DEPTH, rows, cols), F32),
        input_output_aliases={2: 0} if carried else {},
        compiler_params=_params(("parallel",)),
    )(a, b, *carried)


def _adamw(w, g, m, v, *, name):
    shape = w.shape
    if w.ndim == 2:
        w, g, m, v = (a.reshape((1,) + shape) for a in (w, g, m, v))
    layers, rows, cols = w.shape
    tr = _tile(rows, max(8, (1 << 18) // cols), 8)
    c1 = 1.0 - ADAM_B1 ** ADAM_STEP
    c2 = 1.0 - ADAM_B2 ** ADAM_STEP

    def body(w_ref, g_ref, m_ref, v_ref, d_ref, nm_ref, nv_ref):
        gv = g_ref[...]
        nm = ADAM_B1 * m_ref[...] + (1.0 - ADAM_B1) * gv
        nv = ADAM_B2 * v_ref[...] + (1.0 - ADAM_B2) * (gv * gv)
        nm_ref[...] = nm
        nv_ref[...] = nv
        d_ref[...] = -ADAM_LR * ((nm / c1) / (jnp.sqrt(nv / c2) + ADAM_EPS) + ADAM_WD * w_ref[...])

    blk = pl.BlockSpec((None, tr, cols), lambda l, i: (l, i, 0))
    outs = pl.pallas_call(
        body, name=name, grid=(layers, rows // tr), in_specs=[blk] * 4, out_specs=[blk] * 3,
        out_shape=[jax.ShapeDtypeStruct((layers, rows, cols), F32)] * 3,
        compiler_params=_params(("parallel", "parallel")),
    )(w, g, m, v)
    return [o.reshape(shape) for o in outs]


def _adamw_summed(w, halves, m, v, *, name):
    layers, rows, cols = w.shape
    tr = _tile(rows, max(8, (1 << 18) // cols), 8)
    tc = _tile(cols, max(LANES, (1 << 18) // tr))
    nb = rows // tr
    c1 = 1.0 - ADAM_B1 ** ADAM_STEP
    c2 = 1.0 - ADAM_B2 ** ADAM_STEP

    def body(w_ref, m_ref, v_ref, *rest):
        half_refs, (g_ref, d_ref, nm_ref, nv_ref) = rest[:2 * layers], rest[2 * layers:]
        for k in range(layers):
            @pl.when(pl.program_id(0) == k)
            def _(k=k):
                gv = half_refs[2 * k][...] + half_refs[2 * k + 1][...]
                g_ref[...] = gv
                nm = ADAM_B1 * m_ref[...] + (1.0 - ADAM_B1) * gv
                nv = ADAM_B2 * v_ref[...] + (1.0 - ADAM_B2) * (gv * gv)
                nm_ref[...] = nm
                nv_ref[...] = nv
                d_ref[...] = -ADAM_LR * ((nm / c1) / (jnp.sqrt(nv / c2) + ADAM_EPS) + ADAM_WD * w_ref[...])

    def half_spec(k):
        def index(l, i, j):
            at_k = l == k
            return (jnp.where(at_k, i, jnp.where(l < k, 0, nb - 1)), jnp.where(at_k, j, jnp.where(l < k, 0, cols // tc - 1)))
        return pl.BlockSpec((tr, tc), index)

    blk = pl.BlockSpec((None, tr, tc), lambda l, i, j: (l, i, j))
    return pl.pallas_call(
        body, name=name, grid=(layers, nb, cols // tc),
        in_specs=[blk] * 3 + [half_spec(k) for k in range(layers) for _ in range(2)], out_specs=[blk] * 4,
        out_shape=[jax.ShapeDtypeStruct((layers, rows, cols), F32)] * 4,
        compiler_params=_params(("parallel", "parallel", "parallel")),
    )(w, m, v, *[half for pair in halves for half in pair])


def _coords():
    return lax.axis_index("x"), lax.axis_index("y"), lax.axis_index("c")


def _other_chips(x, y):
    return [(1 - x, y), (x, 1 - y), (1 - x, 1 - y)]


def _gather_rows(x_ref, out_ref, send_sems, recv_sems, local_sem, m_per):
    x, y, c = _coords()
    me, sibling = (x, y, c), (x, y, 1 - c)
    chips = _other_chips(x, y)

    def rows(px, py, pc):
        return out_ref.at[pl.ds((4 * px + 2 * py + pc) * m_per, m_per), :]

    def copy(k, block, to, src=None):
        return pltpu.make_async_remote_copy(
            src_ref=rows(*block) if src is None else src, dst_ref=rows(*block),
            send_sem=send_sems.at[k], recv_sem=recv_sems.at[k], device_id=to, device_id_type=MESH)

    mine = pltpu.make_async_copy(x_ref, rows(*me), local_sem)
    mine.start()
    first = [copy(0, me, sibling, src=x_ref)]
    first += [copy(1 + j, me, (*chip, c), src=x_ref) for j, chip in enumerate(chips)]
    for cp in first:
        cp.start()
    passed = [copy(4 + j, (*chip, c), sibling) for j, chip in enumerate(chips)]
    for j, chip in enumerate(chips):
        copy(1 + j, (*chip, c), me).wait_recv()
        passed[j].start()
    copy(0, sibling, me).wait_recv()
    for j, chip in enumerate(chips):
        copy(4 + j, (*chip, 1 - c), me).wait_recv()
    for cp in first + passed:
        cp.wait_send()
    mine.wait()


def _ada_modulation(c_blk, w_ada, b_mine, *, name):
    depth, _, cols = w_ada.shape
    blk_rows = c_blk.shape[0]
    sems = [pltpu.SemaphoreType.DMA((7,)), pltpu.SemaphoreType.DMA((7,)), pltpu.SemaphoreType.DMA]

    def body(c_ref, w_ref, b_ref, act_ref, mod_ref, c_all, prod_all, mod_mine, *sem_refs):
        _gather_rows(c_ref, c_all, *sem_refs[:3], blk_rows)
        cv = c_all[...]
        c_all[...] = cv * _sigmoid(cv)
        lhs = c_all[...].astype(BF16)
        for d in range(N_DEV):
            act_ref[d:d + 1, :] = c_all[d * blk_rows:d * blk_rows + 1, :]
        for l in range(depth):
            prod_all[...] = (jnp.dot(lhs, w_ref[l].astype(BF16), preferred_element_type=F32)
                             + b_ref[:, l * cols:(l + 1) * cols])
            for d in range(N_DEV):
                mod_mine[l * N_DEV + d:l * N_DEV + d + 1, :] = prod_all[d * blk_rows:d * blk_rows + 1, :]
        _gather_rows(mod_mine, mod_ref, *sem_refs[3:], depth * N_DEV)

    vmem = pl.BlockSpec(memory_space=pltpu.VMEM)
    act, mod = pl.pallas_call(
        body, name=name,
        out_shape=[jax.ShapeDtypeStruct((N_DEV, D_MODEL), F32), jax.ShapeDtypeStruct((N_DEV * depth * N_DEV, cols), F32)],
        in_specs=[vmem, vmem, vmem], out_specs=[vmem, vmem],
        scratch_shapes=[pltpu.VMEM((N_DEV * blk_rows, D_MODEL), F32), pltpu.VMEM((N_DEV * blk_rows, cols), F32),
                        pltpu.VMEM((depth * N_DEV, cols), F32)] + sems + sems,
        compiler_params=_params(),
    )(c_blk, w_ada, b_mine)
    return act.astype(BF16), mod


def _all_gather_small(blk, *, name, swaps=()):
    m_per, n = blk.shape
    n_swaps = len(swaps)

    def body(x_ref, *refs):
        swap_srcs, out_ref, swap_outs = refs[:n_swaps], refs[n_swaps], refs[n_swaps + 1:2 * n_swaps + 1]
        send_sems, recv_sems, local_sem = refs[2 * n_swaps + 1:2 * n_swaps + 4]
        x, y, c = _coords()
        me, sibling = (x, y, c), (x, y, 1 - c)
        chips = _other_chips(x, y)
        swapping = [pltpu.make_async_remote_copy(src_ref=src, dst_ref=dst, send_sem=refs[-2].at[k], recv_sem=refs[-1].at[k],
                                                 device_id=sibling, device_id_type=MESH)
                    for k, (src, dst) in enumerate(zip(swap_srcs, swap_outs))]
        for cp in swapping:
            cp.start()

        def rows(px, py, pc):
            return out_ref.at[pl.ds((4 * px + 2 * py + pc) * m_per, m_per), :]

        def copy(k, block, to, src=None):
            return pltpu.make_async_remote_copy(
                src_ref=rows(*block) if src is None else src, dst_ref=rows(*block),
                send_sem=send_sems.at[k], recv_sem=recv_sems.at[k], device_id=to, device_id_type=MESH)

        mine = pltpu.make_async_copy(x_ref, rows(*me), local_sem)
        mine.start()
        first = [copy(0, me, sibling, src=x_ref)]
        first += [copy(1 + j, me, (*chip, c), src=x_ref) for j, chip in enumerate(chips)]
        for cp in first:
            cp.start()
        passed = [copy(4 + j, (*chip, c), sibling) for j, chip in enumerate(chips)]
        for j, chip in enumerate(chips):
            copy(1 + j, (*chip, c), me).wait_recv()
            passed[j].start()
        copy(0, sibling, me).wait_recv()
        for j, chip in enumerate(chips):
            copy(4 + j, (*chip, 1 - c), me).wait_recv()
        for cp in first + passed:
            cp.wait_send()
        mine.wait()
        for cp in swapping:
            cp.wait()

    any_spec = pl.BlockSpec(memory_space=pl.ANY)
    outs = pl.pallas_call(
        body, name=name,
        out_shape=[jax.ShapeDtypeStruct((N_DEV * m_per, n), blk.dtype)] + [jax.ShapeDtypeStruct(a.shape, a.dtype)
                                                                           for a in swaps],
        in_specs=[pl.BlockSpec(memory_space=pltpu.VMEM)] + [any_spec] * n_swaps,
        out_specs=[pl.BlockSpec(memory_space=pltpu.VMEM)] + [any_spec] * n_swaps,
        scratch_shapes=[pltpu.SemaphoreType.DMA((7,)), pltpu.SemaphoreType.DMA((7,)), pltpu.SemaphoreType.DMA]
        + ([pltpu.SemaphoreType.DMA((n_swaps,)), pltpu.SemaphoreType.DMA((n_swaps,))] if swaps else []),
        compiler_params=_params(),
    )(blk, *swaps)
    return (outs[0], outs[1:]) if swaps else outs[0]


HBM_SPEC = pl.BlockSpec(memory_space=pltpu.HBM)
SEM_SPEC = pl.BlockSpec(memory_space=pltpu.SEMAPHORE)
DATAFLOW = pltpu.SideEffectType.DATAFLOW_SIDE_EFFECTING


def _chip_copies(src_ref, land_ref, send_sems, recv_sems, scatter):
    x, y, c = _coords()
    my = 2 * x + y
    outgoing, incoming = [], []
    for k, (px, py) in enumerate(_other_chips(x, y)):
        peer = 2 * px + py

        def copy(src_slot, dst_slot):
            return pltpu.make_async_remote_copy(
                src_ref=src_ref.at[src_slot] if scatter else src_ref, dst_ref=land_ref.at[dst_slot],
                send_sem=send_sems.at[k], recv_sem=recv_sems.at[k], device_id=(px, py, c), device_id_type=MESH)

        outgoing.append(copy(peer, my))
        incoming.append(copy(my, peer))
    return outgoing, incoming


def _exchange_start(srcs, *, name, scatter):
    n = len(srcs)
    land_shapes = [src.shape if scatter else (N_CHIPS,) + src.shape for src in srcs]

    def body(*refs):
        for k in range(n):
            send_sems, recv_sems = refs[2 * n + 4 * k], refs[2 * n + 4 * k + 1]
            outgoing, _ = _chip_copies(refs[k], refs[n + k], send_sems, recv_sems, scatter)
            for cp in outgoing:
                cp.start()
        refs[-1][...] = jnp.zeros_like(refs[-1])

    out_shape, out_specs, aliases = [], [], {}
    for k, (src, land_shape) in enumerate(zip(srcs, land_shapes)):
        out_shape += [pltpu.SemaphoreType.DMA((N_CHIPS - 1,)), pltpu.SemaphoreType.DMA((N_CHIPS - 1,)),
                      pltpu.HBM(src.shape, src.dtype), pltpu.HBM(land_shape, src.dtype)]
        out_specs += [SEM_SPEC, SEM_SPEC, HBM_SPEC, HBM_SPEC]
        aliases.update({k: 4 * k + 2, n + k: 4 * k + 3})
    outs = pl.pallas_call(
        body, name=name,
        out_shape=tuple(out_shape) + (jax.ShapeDtypeStruct((8, LANES), F32),),
        in_specs=(HBM_SPEC,) * (2 * n),
        out_specs=tuple(out_specs) + (pl.BlockSpec(memory_space=pltpu.VMEM),),
        input_output_aliases=aliases,
        compiler_params=pltpu.CompilerParams(has_side_effects=DATAFLOW),
    )(*[pltpu.with_memory_space_constraint(src, pltpu.HBM) for src in srcs],
      *[pltpu.with_memory_space_constraint(lax.empty(shape, src.dtype), pltpu.HBM)
        for src, shape in zip(srcs, land_shapes)])
    return [tuple(outs[4 * k:4 * k + 4]) for k in range(n)], outs[-1]


def _exchange_wait(started, after, *, name, scatter):
    send_sems, recv_sems, src_thru, land_thru = started

    def body(src_ref, land_ref, send_sems, recv_sems, after_ref, src_dead, got_ref):
        outgoing, incoming = _chip_copies(src_ref, land_ref, send_sems, recv_sems, scatter)
        for cp in outgoing:
            cp.wait_send()
        for cp in incoming:
            cp.wait_recv()

    return pl.pallas_call(
        body, name=name,
        out_shape=(pltpu.HBM(src_thru.shape, src_thru.dtype), pltpu.HBM(land_thru.shape, land_thru.dtype)),
        in_specs=(HBM_SPEC, HBM_SPEC, SEM_SPEC, SEM_SPEC, pl.BlockSpec(memory_space=pl.ANY)),
        out_specs=(HBM_SPEC, HBM_SPEC),
        input_output_aliases={0: 0, 1: 1},
        compiler_params=pltpu.CompilerParams(has_side_effects=DATAFLOW),
    )(src_thru, land_thru, send_sems, recv_sems, after)


def _pack_rows(a):
    return a.reshape(-1, D_MODEL)


def _pad_heads(w, width):
    r = w.shape[0]
    return jnp.pad(w, ((0, 0), (0, 0), (0, HEAD_PAD - width))).reshape(r, N_HEADS * HEAD_PAD)


MIX_NAMES = ("w_uq", "w_uk", "w_uv", "p_pool", "p_attn", "w_out")
GROUPS = ("in", "mix", "ff1", "ff2")


def _local_shard(weights, l, group, zero):
    if group == "mix":
        shard = jnp.concatenate([_pack_rows(weights[n][l]) for n in MIX_NAMES], axis=0)
    else:
        shard = weights[{"in": "w_in", "ff1": "w_ff1", "ff2": "w_ff2"}[group]][l]
    return (shard + zero).astype(BF16)


def _unpack_weights(gathered, group):
    def cols(a, k):
        return a.reshape(N_CHIPS, k, -1).transpose(1, 0, 2).reshape(k, -1)

    if group == "in":
        full = gathered.reshape(W_IN_COLS, D_MODEL)
        u, cq, ckv, kr, gates = (full[a:b] for a, b in (W_IN_U, W_IN_CQ, W_IN_CKV, W_IN_KR, W_IN_GATES))
        kr = jnp.pad(kr, ((QK_NOPE, HEAD_PAD - QK_DIM), (0, 0)))
        return dict(w_in=jnp.concatenate([cq, kr, u, gates, ckv], axis=0))
    if group == "ff1":
        return dict(w_ff1=gathered)
    if group == "ff2":
        return dict(w_ff2=gathered.reshape(D_FF, D_MODEL))

    def p_attn(a):
        full = cols(a, ATTN_DIM).reshape(N_HEADS, V_DIM, D_MODEL)
        return jnp.pad(full, ((0, 0), (0, HEAD_PAD - V_DIM), (0, 0))).reshape(N_HEADS * HEAD_PAD, D_MODEL)

    build = dict(
        w_uq=lambda a: _pad_heads(a.reshape(Q_LORA, N_HEADS, QK_DIM), QK_DIM),
        w_uk=lambda a: _pad_heads(a.reshape(KV_LORA, N_HEADS, QK_NOPE), QK_NOPE),
        w_uv=lambda a: _pad_heads(a.reshape(KV_LORA, N_HEADS, V_DIM), V_DIM),
        p_pool=lambda a: cols(a, POOL_DIM),
        p_attn=p_attn,
        w_out=lambda a: a.reshape(D_MODEL, D_MODEL),
    )
    w, off = {}, 0
    for name in MIX_NAMES:
        w[name] = build[name](gathered[:, off:off + ROWS_OF[name]])
        off += ROWS_OF[name]
    return w


def _pack_grads(g, group):
    def cols(a):
        k = a.shape[0]
        return a.reshape(k, N_CHIPS, -1).transpose(1, 0, 2).reshape(N_CHIPS, -1, D_MODEL)

    def rows(a):
        return a.reshape(N_CHIPS, -1, D_MODEL)

    def heads(width):
        return lambda a: rows(a.reshape(a.shape[0], N_HEADS, HEAD_PAD)[:, :, :width])

    if group == "in":
        full = jnp.concatenate([g["u"], g["cq"], g["ckv"], g["kr"][QK_NOPE:QK_DIM], g["ga"], g["gb"]], axis=0)
        return full.reshape(N_CHIPS, W_IN_SHARD, D_MODEL)
    if group == "ff1":
        return g["w_ff1"]
    if group == "ff2":
        return g["w_ff2"].reshape(N_CHIPS, D_FF // N_CHIPS, D_MODEL)

    def p_attn(a):
        return cols(a.reshape(N_HEADS, HEAD_PAD, D_MODEL)[:, :V_DIM].reshape(ATTN_DIM, D_MODEL))

    build = dict(w_uq=heads(QK_DIM), w_uk=heads(QK_NOPE), w_uv=heads(V_DIM), p_pool=cols, p_attn=p_attn, w_out=rows)
    return jnp.concatenate([build[name](g[name]) for name in MIX_NAMES], axis=1)


def _per_head(fn, acc, *tables):
    return jnp.concatenate([fn(acc[:, h * HEAD_PAD:(h + 1) * HEAD_PAD], *tables) for h in range(N_HEADS)], axis=1)


def _rope_head(a, cos, sin):
    lane = lax.broadcasted_iota(jnp.int32, a.shape, 1)
    return a * (cos + jnp.where(lane < QK_NOPE, 1.0, 0.0)) + _rotate_half(a) * sin


def _layer_fwd(l, x, mod, get_weights, small, cos_t, sin_t):
    sh1, sc1, g1, sh2, sc2, g2 = mod
    tag = f"_l{l}"
    h, r1 = _norm_mod(x, small["ln1_g"], sc1, sh1, name="norm1" + tag)
    w = dict(get_weights("in", h))
    (z,) = _mm(h, w["w_in"], tb=True, name="in_proj" + tag, out_dtypes=(BF16,))
    p, yp, cq, ckv, kr, rq, rkv = _mixer_pre(z, cos_t, sin_t, small["w_pool"], small["pool_scale"],
                                              small["q_norm_g"], small["kv_norm_g"], name="mixer_pre" + tag)
    w.update(get_weights("mix", yp))
    (ya,) = _mm(yp, w["p_pool"], name="pool_out" + tag, out_dtypes=(BF16,))
    (q,) = _mm(cq, w["w_uq"], name="q_proj" + tag, out_dtypes=(BF16,),
               epilogue=lambda acc, cos, sin: (_per_head(_rope_head, acc, cos, sin),),
               extras=((cos_t, "table"), (sin_t, "table")))
    (k,) = _mm(ckv, w["w_uk"], name="k_proj" + tag, out_dtypes=(BF16,),
               epilogue=lambda acc, krv: (_per_head(lambda a, b: a + b, acc, krv),), extras=((kr, "table"),))
    (v,) = _mm(ckv, w["w_uv"], name="v_proj" + tag, out_dtypes=(BF16,))
    o, lse = _attn_fwd(q, k, v, name="attn_fwd" + tag)
    yb, merged = _mm(o, w["p_attn"], name="attn_out" + tag, out_dtypes=(BF16, BF16), tm=512,
                     epilogue=lambda acc, ga, gb, yav: (acc, _sigmoid(ga) * yav + _sigmoid(gb) * acc),
                     extras=((z, ("tile", ZC_GA // D_MODEL)), (z, ("tile", ZC_GB // D_MODEL)), (ya, "tile")))
    mo, x1 = _mm(merged, w["w_out"], name="mix_out" + tag, out_dtypes=(BF16, F32),
                 epilogue=lambda acc, xr, g: (acc, xr + g * acc), extras=((x, "tile"), (g1, "row")))
    h2, r2 = _norm_mod(x1, small["ln2_g"], sc2, sh2, name="norm2" + tag)
    w.update(get_weights("ff1", merged))
    f, act = _mm(h2, w["w_ff1"], b_stack=True, name="ff1" + tag, out_dtypes=(BF16, BF16),
                 epilogue=lambda acc: (acc, jnp.square(jnp.maximum(acc, 0.0))))
    w.update(get_weights("ff2", act))
    m2, x2 = _mm(act, w["w_ff2"], name="ff2" + tag, out_dtypes=(BF16, F32),
                 epilogue=lambda acc, xr, g: (acc, xr + g * acc), extras=((x1, "tile"), (g2, "row")))
    saved = dict(x=x, h=h, r1=r1, z=z, p=p, yp=yp, cq=cq, ckv=ckv, rq=rq, rkv=rkv, ya=ya, q=q, k=k, v=v, o=o, lse=lse,
                 yb=yb, merged=merged, mo=mo, x1=x1, h2=h2, r2=r2, f=f, act=act, m2=m2)
    return x2, saved, w


def _merge_grads(dm, ga, gb, ya, yb):
    sa, sb = _sigmoid(ga), _sigmoid(gb)
    return dm * sa, dm * sb, dm * ya * (sa * (1.0 - sa)), dm * yb * (sb * (1.0 - sb))


def _layer_bwd(l, dx2, dm2, dg2, sv, mod, w, small, cos_t, sin_t, send_grads, gate_below):
    sh1, sc1, g1, sh2, sc2, g2 = mod
    tag = f"_l{l}"
    gw = {}
    (df,) = _mm(dm2, w["w_ff2"], tb=True, name="ff2_dx" + tag, out_dtypes=(BF16,),
                epilogue=lambda acc, f: (acc * (2.0 * jnp.maximum(f, 0.0)),), extras=((sv["f"], "tile"),))
    (g_ff2,) = _mm(sv["act"], dm2, ta=True, name="ff2_dw" + tag, out_dtypes=(BF16,))
    (g_ff1,) = _mm(sv["h2"], df, ta=True, out_stack=N_CHIPS, name="ff1_dw" + tag, out_dtypes=(BF16,))
    sc2 = sc2 + send_grads("ff2", dict(w_ff2=g_ff2)) + send_grads("ff1", dict(w_ff1=g_ff1))
    (dh2,) = _mm(df, w["w_ff1"], tb=True, b_stack=True, name="ff1_dx" + tag)
    dx1, dln2, dsc2, dsh2, dmo, dg1 = _norm_mod_bwd(dh2, sv["x1"], sv["r2"], small["ln2_g"], sc2, dx2,
                                                    gate=(sv["mo"], g1), name="norm2_bwd" + tag)
    dya, dyb, dga, dgb = _mm(dmo, w["w_out"], tb=True, name="mix_out_dx" + tag, out_dtypes=(BF16,) * 4, tm=512,
                             epilogue=_merge_grads,
                             extras=((sv["z"], ("tile", ZC_GA // D_MODEL)), (sv["z"], ("tile", ZC_GB // D_MODEL)),
                                     (sv["ya"], "tile"), (sv["yb"], "tile")))
    (gw["w_out"],) = _mm(sv["merged"], dmo, ta=True, name="mix_out_dw" + tag, out_dtypes=(BF16,))
    (gw["p_pool"],) = _mm(sv["yp"], dya, ta=True, name="pool_out_dw" + tag, out_dtypes=(BF16,))
    (dyp,) = _mm(dya, w["p_pool"], tb=True, name="pool_out_dx" + tag)
    du, g_w_pool, g_pool_scale = _pool_bwd(dyp, sv["p"], small["w_pool"], small["pool_scale"], name="pool_bwd" + tag)
    (gw["p_attn"],) = _mm(sv["o"], dyb, ta=True, name="attn_out_dw" + tag, out_dtypes=(BF16,))
    (do,) = _mm(dyb, w["p_attn"], tb=True, name="attn_out_dx" + tag, out_dtypes=(BF16,))
    dql, dkb, dv, dk_sums = _attn_bwd(sv["q"], sv["k"], sv["v"], do, sv["o"], sv["lse"], cos_t, sin_t,
                                      name="attn_bwd" + tag)
    dkr = _key_bwd(dk_sums, cos_t, sin_t, name="key_bwd" + tag)
    (gw["w_uq"],) = _mm(sv["cq"], dql, ta=True, name="q_proj_dw" + tag, out_dtypes=(BF16,))
    (gw["w_uk"],) = _mm(sv["ckv"], dkb, ta=True, name="k_proj_dw" + tag, out_dtypes=(BF16,))
    (gw["w_uv"],) = _mm(sv["ckv"], dv, ta=True, name="v_proj_dw" + tag, out_dtypes=(BF16,))
    (dcq,) = _mm(dql, w["w_uq"], tb=True, name="q_proj_dx" + tag)
    (dckv,) = _mm(dkb, w["w_uk"], tb=True, second=(dv, w["w_uv"]), name="kv_proj_dx" + tag)
    q_norm_g = small["q_norm_g"] + send_grads("mix", gw)
    dcq_raw, g_qn = _rms_bwd(dcq, sv["z"], ZC_CQ, sv["rq"], q_norm_g, name="q_norm_bwd" + tag)
    dckv_raw, g_kvn = _rms_bwd(dckv, sv["z"], ZC_CKV, sv["rkv"], small["kv_norm_g"], name="kv_norm_bwd" + tag)
    dz = dict(cq=dcq_raw, kr=dkr, u=du, ga=dga, gb=dgb, ckv=dckv_raw)
    g_in = {n: _mm(piece, sv["h"], ta=True, name=f"in_proj_dw_{n}" + tag, out_dtypes=(BF16,))[0]
            for n, piece in dz.items()}
    sc1 = sc1 + send_grads("in", g_in)
    dh = _mm_sum(list(dz.values()), w["w_in"], [Z_OFFSETS[n] for n in dz], name="in_proj_dx" + tag)
    dx, dln1, dsc1, dsh1, *below = _norm_mod_bwd(dh, sv["x"], sv["r1"], small["ln1_g"], sc1, dx1, gate=gate_below,
                                                 name="norm1_bwd" + tag)
    dmod = jnp.concatenate([dsh1, dsc1, dg1, dsh2, dsc2, dg2], axis=0)
    gsmall = dict(ln1_g=dln1, ln2_g=dln2, q_norm_g=g_qn, kv_norm_g=g_kvn, w_pool=g_w_pool, pool_scale=g_pool_scale)
    return dx, dmod, gsmall, below


SMALL_LOSS = 6
SMALL_SINGLES = 16
SMALL_POOL = 24
SMALL_POOL_ROWS = len(POOL_WINDOWS) * POOL_GROUP * POOL_GROUP // D_MODEL
SMALL_ROWS = SMALL_POOL + DEPTH * SMALL_POOL_ROWS


def _pack_small(parts, *, name):
    def body(*refs):
        out_ref = refs[-1]
        out_ref[...] = jnp.zeros_like(out_ref)
        for ref, (_, row) in zip(refs[:-1], parts):
            out_ref[row:row + ref.shape[0], :] = ref[...]

    return pl.pallas_call(body, name=name, out_shape=jax.ShapeDtypeStruct((SMALL_ROWS, D_MODEL), F32),
                          compiler_params=_params())(*[a for a, _ in parts])


def kernel(x, c, positions, ln1_g, ln2_g, w_ada, b_ada, w_in, q_norm_g, w_uq, kv_norm_g, w_uk, w_uv, w_pool, pool_scale, p_pool, p_attn, w_out, w_ff1, w_ff2, final_g, loss_target, m_ln1_g, m_ln2_g, m_w_ada, m_b_ada, m_w_in, m_q_norm_g, m_w_uq, m_kv_norm_g, m_w_uk, m_w_uv, m_w_pool, m_pool_scale, m_p_pool, m_p_attn, m_w_out, m_w_ff1, m_w_ff2, m_final_g, v_ln1_g, v_ln2_g, v_w_ada, v_b_ada, v_w_in, v_q_norm_g, v_w_uq, v_kv_norm_g, v_w_uk, v_w_uv, v_w_pool, v_pool_scale, v_p_pool, v_p_attn, v_w_out, v_w_ff1, v_w_ff2, v_final_g):
    weights = dict(ln1_g=ln1_g, ln2_g=ln2_g, w_ada=w_ada, b_ada=b_ada, w_in=w_in, q_norm_g=q_norm_g, w_uq=w_uq,
                   kv_norm_g=kv_norm_g, w_uk=w_uk, w_uv=w_uv, w_pool=w_pool, pool_scale=pool_scale, p_pool=p_pool,
                   p_attn=p_attn, w_out=w_out, w_ff1=w_ff1, w_ff2=w_ff2, final_g=final_g)
    moms = dict(ln1_g=m_ln1_g, ln2_g=m_ln2_g, w_ada=m_w_ada, b_ada=m_b_ada, w_in=m_w_in, q_norm_g=m_q_norm_g,
                w_uq=m_w_uq, kv_norm_g=m_kv_norm_g, w_uk=m_w_uk, w_uv=m_w_uv, w_pool=m_w_pool,
                pool_scale=m_pool_scale, p_pool=m_p_pool, p_attn=m_p_attn, w_out=m_w_out, w_ff1=m_w_ff1,
                w_ff2=m_w_ff2, final_g=m_final_g)
    vels = dict(ln1_g=v_ln1_g, ln2_g=v_ln2_g, w_ada=v_w_ada, b_ada=v_b_ada, w_in=v_w_in, q_norm_g=v_q_norm_g,
                w_uq=v_w_uq, kv_norm_g=v_kv_norm_g, w_uk=v_w_uk, w_uv=v_w_uv, w_pool=v_w_pool,
                pool_scale=v_pool_scale, p_pool=v_p_pool, p_attn=v_p_attn, w_out=v_w_out, w_ff1=v_w_ff1,
                w_ff2=v_w_ff2, final_g=v_final_g)
    order = list(weights)
    for table in (weights, moms, vels):
        table["w_in"] = jnp.swapaxes(table["w_in"], 1, 2)
    seq = x.shape[1]
    my_chip = 2 * lax.axis_index("x") + lax.axis_index("y")
    my_dev = 2 * my_chip + lax.axis_index("c")
    ada_cols = w_ada.shape[2]

    small = [dict(ln1_g=ln1_g[l:l + 1], ln2_g=ln2_g[l:l + 1], q_norm_g=q_norm_g[l:l + 1], kv_norm_g=kv_norm_g[l:l + 1],
                  w_pool=w_pool[l], pool_scale=pool_scale[l:l + 1]) for l in range(DEPTH)]

    b_mine = lax.dynamic_slice_in_dim(b_ada, my_chip * ada_cols, ada_cols, axis=1).reshape(1, DEPTH * ada_cols)
    c_act, mod_all = _ada_modulation(jnp.pad(c, ((0, 7), (0, 0))), w_ada, b_mine, name="ada_modulation")
    mod_all = mod_all.reshape(N_DEV, DEPTH, N_DEV, ada_cols)

    zero = mod_all[0, 0, 0, 0] * 0.0
    keys = [(l, group) for l in range(DEPTH) for group in GROUPS]
    exchanges, token = _exchange_start([_local_shard(weights, l, group, zero) for l, group in keys],
                                       name="weights_send", scatter=False)
    started = dict(zip(keys, exchanges))
    pin = token[0:1, 0:1]

    def gathered_weights(l, group, after):
        mine, land = _exchange_wait(started[l, group], after, name=f"weights_wait_l{l}_{group}", scatter=False)
        land = lax.dynamic_update_slice_in_dim(land, mine[None], my_chip, axis=0)
        return _unpack_weights(land, group)

    mods = []
    for l in range(DEPTH):
        row = jnp.concatenate([lax.dynamic_index_in_dim(mod_all[2 * j, l], my_dev, axis=0, keepdims=True)
                               for j in range(N_CHIPS)], axis=1) + pin
        mods.append([row[:, i * D_MODEL:(i + 1) * D_MODEL] for i in range(N_MOD)])

    inv_freq = ROPE_THETA ** (-jnp.arange(0, QK_ROPE, 2, dtype=F32) / QK_ROPE)
    freq_lanes = jnp.concatenate([jnp.zeros((QK_NOPE,), F32), inv_freq, inv_freq,
                                  jnp.zeros((HEAD_PAD - QK_DIM,), F32)]).reshape(1, LANES)
    cos_t, sin_t = _rope_tables(positions.reshape(seq, 1), freq_lanes, name="rope_tables")

    xs, saved, wl = x.reshape(seq, D_MODEL), [], []
    for l in range(DEPTH):
        xs, sv, w_l = _layer_fwd(l, xs, mods[l], functools.partial(gathered_weights, l), small[l], cos_t, sin_t)
        saved.append(sv)
        wl.append(w_l)
    dx, loss_part, g_final, dm2, dg2 = _final_loss(xs, final_g.reshape(1, D_MODEL), loss_target.reshape(seq, D_MODEL),
                                                   saved[-1]["m2"], mods[-1][5], name="final_loss")

    sent, pending = [], []
    send_after = {(0, "ff1"), (0, "mix"), (0, "in")}

    def send_grads(l, group, g):
        pending.append((l, group, _pack_grads(g, group)))
        if (l, group) not in send_after:
            return jnp.zeros((1, 1), F32)
        exchanges, token_g = _exchange_start([gpack for _, _, gpack in pending], name=f"grads_send_l{l}_{group}",
                                             scatter=True)
        sent.extend((item[0], item[1], exchange) for item, exchange in zip(pending, exchanges))
        pending.clear()
        return token_g[0:1, 0:1]

    dmod, gsmall = [None] * DEPTH, [None] * DEPTH
    for l in reversed(range(DEPTH)):
        gate_below = (saved[l - 1]["m2"], mods[l - 1][5]) if l > 0 else None
        dx, dmod[l], gsmall[l], below = _layer_bwd(l, dx, dm2, dg2, saved[l], mods[l], wl[l], small[l], cos_t, sin_t,
                                                   functools.partial(send_grads, l), gate_below)
        dm2, dg2 = below if below else (None, None)
    grads = dict(x=dx.reshape(1, seq, D_MODEL))

    big_parts, after = [], dmod[0]
    for l, group, started_g in sent:
        tg = f"_l{l}_{group}"
        gpack, land = _exchange_wait(started_g, after, name="grads_wait" + tg, scatter=True)
        own = lax.dynamic_index_in_dim(gpack, my_chip, axis=0, keepdims=True)
        land = lax.dynamic_update_slice_in_dim(land, own, my_chip, axis=0)
        big_parts.append(_sum_slots(land, N_CHIPS, name="grads_sum_chips" + tg))
        after = big_parts[-1]

    def lanes(a):
        flat = a.reshape(1, -1)
        return jnp.pad(flat, ((0, 0), (0, D_MODEL - flat.shape[1])))

    singles = [gsmall[0]["ln1_g"], gsmall[1]["ln1_g"], gsmall[0]["ln2_g"], gsmall[1]["ln2_g"], g_final,
               lanes(jnp.concatenate([gsmall[l]["pool_scale"] for l in range(DEPTH)], axis=1)),
               lanes(jnp.concatenate([gsmall[l]["q_norm_g"] for l in range(DEPTH)], axis=1)),
               lanes(jnp.concatenate([gsmall[l]["kv_norm_g"] for l in range(DEPTH)], axis=1))]
    parts = [(dmod[0], 0), (lanes(loss_part), SMALL_LOSS), (dmod[1], 8)]
    parts += [(a, SMALL_SINGLES + i) for i, a in enumerate(singles)]
    parts += [(gsmall[l]["w_pool"].reshape(-1, D_MODEL), SMALL_POOL + l * SMALL_POOL_ROWS) for l in range(DEPTH)]
    small_all, big_others = _all_gather_small(_pack_small(parts, name="small_grads_pack"), swaps=big_parts,
                                              name="small_grads_all_gather")
    small_all = small_all.reshape(N_DEV, SMALL_ROWS, D_MODEL)
    ssum = _sum_slots(small_all, N_DEV, name="small_grads_sum")
    loss = ssum[SMALL_LOSS, 0]
    grads["b_ada"] = jnp.stack([ssum[8 * l:8 * l + N_MOD] for l in range(DEPTH)]).reshape(DEPTH, N_MOD * D_MODEL)
    grads["ln1_g"] = ssum[SMALL_SINGLES:SMALL_SINGLES + 2]
    grads["ln2_g"] = ssum[SMALL_SINGLES + 2:SMALL_SINGLES + 4]
    grads["final_g"] = ssum[SMALL_SINGLES + 4]
    grads["pool_scale"] = ssum[SMALL_SINGLES + 5].reshape(DEPTH, POOL_DIM)
    grads["q_norm_g"] = ssum[SMALL_SINGLES + 6, :DEPTH * Q_LORA].reshape(DEPTH, Q_LORA)
    grads["kv_norm_g"] = ssum[SMALL_SINGLES + 7, :DEPTH * KV_LORA].reshape(DEPTH, KV_LORA)
    grads["w_pool"] = ssum[SMALL_POOL:SMALL_ROWS].reshape(w_pool.shape)

    mix_sum, halves = None, {group: [None] * DEPTH for group in ("in", "ff1", "ff2")}
    for (l, group, _), part, other in zip(sent, big_parts, big_others):
        if group == "mix":
            mix_sum = _add2_stacked(part, other, mix_sum, l, name=f"grads_sum_cores_l{l}_mix")
        else:
            halves[group][l] = (part, other)
    off = 0
    for name in MIX_NAMES:
        grads[name] = mix_sum[:, off:off + ROWS_OF[name]].reshape(weights[name].shape)
        off += ROWS_OF[name]

    c_act_t = jnp.pad(c_act.T, ((0, 0), (0, LANES - N_DEV)))
    d_mine = []
    for l in range(DEPTH):
        d_all = small_all[:, 8 * l:8 * l + N_MOD].reshape(N_DEV, N_MOD * D_MODEL)
        d_mine.append(lax.dynamic_slice_in_dim(d_all, my_chip * ada_cols, ada_cols, axis=1))
    d_cat = jnp.pad(jnp.concatenate(d_mine, axis=1), ((0, LANES - N_DEV), (0, 0)))
    (grads["w_ada"],) = _mm(c_act_t, d_cat, out_stack=DEPTH, name="ada_dw")

    def view(a):
        return a.reshape(1, -1) if a.ndim == 1 else a if a.ndim == 3 else a.reshape(-1, a.shape[-1])

    delta, new_m, new_v = {}, {}, {}
    for name in order:
        shape = weights[name].shape
        group = {"w_in": "in", "w_ff1": "ff1", "w_ff2": "ff2"}.get(name)
        if group:
            grads[name], d, nm, nv = _adamw_summed(weights[name], halves[group], moms[name], vels[name],
                                                   name="adamw_" + name)
        else:
            d, nm, nv = _adamw(view(weights[name]), view(grads[name]), view(moms[name]), view(vels[name]),
                               name="adamw_" + name)
        delta[name], new_m[name], new_v[name] = d.reshape(shape), nm.reshape(shape), nv.reshape(shape)
    for table in (grads, delta, new_m, new_v):
        table["w_in"] = jnp.swapaxes(table["w_in"], 1, 2)
    return (loss, grads["x"], *[grads[n] for n in order], *[delta[n] for n in order],
            *[new_m[n] for n in order], *[new_v[n] for n in order])
```

```python
import functools
import math

import jax
import jax.numpy as jnp
from jax import lax
from jax.experimental import pallas as pl
from jax.experimental.pallas import tpu as pltpu

F32 = jnp.float32
BF16 = jnp.bfloat16
MESH = pl.DeviceIdType.MESH

D_MODEL = 1024
DEPTH = 2
POOL_WINDOWS = (2, 4, 8, 16)
POOL_GROUP = 128
POOL_DIM = 512
N_HEADS = 8
QK_NOPE = 64
QK_ROPE = 32
QK_DIM = QK_NOPE + QK_ROPE
V_DIM = 64
HEAD_PAD = 128
Q_LORA = 384
KV_LORA = 256
ROPE_THETA = 10000.0
ATTN_DIM = N_HEADS * V_DIM
D_FF = 4 * D_MODEL
N_MOD = 6
EPS = 1e-6
N_CHIPS = 4
N_DEV = 8

ADAM_LR = 0.001
ADAM_B1 = 0.9
ADAM_B2 = 0.999
ADAM_EPS = 1e-08
ADAM_WD = 0.01
ADAM_STEP = 10

VMEM_LIMIT_BYTES = 56 * 1024 * 1024
LANES = 128
HALO = 16

ZC_CQ = 0
ZC_KR = 384
ZC_U = 512
ZC_GA = 1024
ZC_GB = 2048
ZC_CKV = 3072
Z_DIM = 3328
Z_OFFSETS = dict(cq=ZC_CQ, kr=ZC_KR, u=ZC_U, ga=ZC_GA, gb=ZC_GB, ckv=ZC_CKV)

W_IN_U, W_IN_CQ, W_IN_CKV, W_IN_KR, W_IN_GATES = (0, 512), (512, 896), (896, 1152), (1152, 1184), (1184, 3232)
W_IN_COLS = W_IN_GATES[1]
W_IN_SHARD = W_IN_COLS // N_CHIPS

ROWS_OF = dict(w_uq=72, w_uk=32, w_uv=32, p_pool=128, p_attn=128, w_out=256)


def _params(sem=None, **kw):
    return pltpu.CompilerParams(dimension_semantics=sem, vmem_limit_bytes=VMEM_LIMIT_BYTES, **kw)


def _tile(n, target, unit=LANES):
    best = None
    for t in range(unit, min(n, target) + 1, unit):
        if n % t == 0:
            best = t
    return best if best is not None and 4 * best >= min(n, target) else n


def _near_tile(n, target):
    cands = [t for t in range(LANES, n + 1, LANES) if n % t == 0]
    return min(cands, key=lambda t: abs(math.log(t / target))) if cands else n


def _mm(a, b, *, name, ta=False, tb=False, out_dtypes=(F32,), epilogue=None, extras=(), tm=1024, tn=1024, tk=1024,
        second=None, b_stack=False, out_stack=None):
    (k_dim, m_dim) = a.shape if ta else a.shape[::-1]
    if b_stack:
        g_b, k_b, n_shard = b.shape
        n_dim, k_b = (k_b, g_b * n_shard) if tb else (g_b * n_shard, k_b)
    else:
        (n_dim, k_b) = b.shape if tb else b.shape[::-1]
    assert k_dim == k_b, (a.shape, b.shape)
    n_unit = n_shard if b_stack and not tb else n_dim // out_stack if out_stack else n_dim
    k_unit = n_shard if b_stack and tb else k_dim
    tm, tn, tk = _near_tile(m_dim, tm), _near_tile(n_unit, tn), _near_tile(k_unit, tk)
    nk = k_dim // tk
    n_extra, n_out = len(extras), len(out_dtypes)
    n_lhs = 4 if second else 2
    dims = (((0 if ta else 1,), (1 if tb else 0,)), ((), ()))
    if epilogue is None:
        epilogue = lambda acc: (acc,) * n_out

    def body(*refs):
        operand_refs, rest = refs[:n_lhs], refs[n_lhs:]
        extra_refs, out_refs = rest[:n_extra], rest[n_extra:n_extra + n_out]

        def product():
            total = None
            for a_ref, b_ref in zip(operand_refs[0::2], operand_refs[1::2]):
                part = lax.dot_general(a_ref[...].astype(BF16), b_ref[...].astype(BF16), dims, preferred_element_type=F32)
                total = part if total is None else total + part
            return total

        def finish(acc):
            outs = epilogue(acc, *[r[...] for r in extra_refs])
            for o_ref, o in zip(out_refs, outs):
                o_ref[...] = o.astype(o_ref.dtype)

        if nk == 1:
            finish(product())
            return
        acc_ref = rest[-1]
        k = pl.program_id(2)

        @pl.when(k == 0)
        def _():
            acc_ref[...] = product()

        @pl.when((k > 0) & (k < nk - 1))
        def _():
            acc_ref[...] += product()

        @pl.when(k == nk - 1)
        def _():
            finish(acc_ref[...] + product())

    a_spec = pl.BlockSpec((tk, tm), lambda i, j, k: (k, i)) if ta else pl.BlockSpec((tm, tk), lambda i, j, k: (i, k))
    if b_stack and tb:
        per = n_shard // tk
        b_spec = pl.BlockSpec((None, tn, tk), lambda i, j, k: (k // per, j, k % per))
    elif b_stack:
        per = n_shard // tn
        b_spec = pl.BlockSpec((None, tk, tn), lambda i, j, k: (j // per, k, j % per))
    elif tb:
        b_spec = pl.BlockSpec((tn, tk), lambda i, j, k: (j, k))
    else:
        b_spec = pl.BlockSpec((tk, tn), lambda i, j, k: (k, j))
    if out_stack:
        per_out = (n_dim // out_stack) // tn
        out_spec = pl.BlockSpec((None, tm, tn), lambda i, j, k: (j // per_out, i, j % per_out))
        out_dims = (out_stack, m_dim, n_dim // out_stack)
    else:
        out_spec = pl.BlockSpec((tm, tn), lambda i, j, k: (i, j))
        out_dims = (m_dim, n_dim)
    extra_specs = []
    for arr, kind in extras:
        if kind == "tile":
            extra_specs.append(pl.BlockSpec((tm, tn), lambda i, j, k: (i, j)))
        elif isinstance(kind, tuple):
            extra_specs.append(pl.BlockSpec((tm, tn), functools.partial(lambda i, j, k, c: (i, j + c), c=kind[1])))
        elif kind == "row":
            extra_specs.append(pl.BlockSpec((1, tn), lambda i, j, k: (0, j)))
        elif kind == "col":
            extra_specs.append(pl.BlockSpec((tm, 1), lambda i, j, k: (i, 0)))
        else:
            assert kind == "table", kind
            extra_specs.append(pl.BlockSpec((tm, LANES), lambda i, j, k: (i, 0)))
    return pl.pallas_call(
        body,
        name=name,
        grid=(m_dim // tm, n_dim // tn, nk),
        in_specs=[a_spec, b_spec] * (n_lhs // 2) + extra_specs,
        out_specs=[out_spec for _ in out_dtypes],
        out_shape=[jax.ShapeDtypeStruct(out_dims, dt) for dt in out_dtypes],
        scratch_shapes=[pltpu.VMEM((tm, tn), F32)] if nk > 1 else [],
        compiler_params=_params(("parallel", "parallel", "arbitrary")),
    )(a, b, *(second or ()), *[arr for arr, _ in extras])


def _mm_sum(pieces, b, offsets, *, name, tm=1024, tn=1024):
    m_dim, n_dim = pieces[0].shape[0], b.shape[1]
    tm, tn = _near_tile(m_dim, tm), _near_tile(n_dim, tn)
    n_pieces = len(pieces)

    def body(*refs):
        total = None
        for a_ref, b_ref in zip(refs[:n_pieces], refs[n_pieces:2 * n_pieces]):
            part = jnp.dot(a_ref[...], b_ref[...], preferred_element_type=F32)
            total = part if total is None else total + part
        refs[-1][...] = total

    a_specs = [pl.BlockSpec((tm, p.shape[1]), lambda i, j: (i, 0)) for p in pieces]
    b_specs = [pl.BlockSpec((p.shape[1], tn), functools.partial(lambda i, j, blk: (blk, j), blk=off // p.shape[1]))
               for p, off in zip(pieces, offsets)]
    return pl.pallas_call(
        body, name=name, grid=(m_dim // tm, n_dim // tn),
        in_specs=a_specs + b_specs,
        out_specs=pl.BlockSpec((tm, tn), lambda i, j: (i, j)),
        out_shape=jax.ShapeDtypeStruct((m_dim, n_dim), F32),
        compiler_params=_params(("parallel", "parallel")),
    )(*pieces, *[b] * n_pieces)


def _rows(s):
    return min(512, s)


def _rope_tables(pos_col, inv_freq_lanes, *, name):
    s = pos_col.shape[0]
    tb = _rows(s)

    def body(pos_ref, f_ref, cos_ref, sin_ref):
        ang = pos_ref[...].astype(F32) * f_ref[...]
        lane = lax.broadcasted_iota(jnp.int32, ang.shape, 1)
        on = (lane >= QK_NOPE) & (lane < QK_DIM)
        cos_ref[...] = jnp.where(on, jnp.cos(ang), 0.0)
        sin_ref[...] = jnp.where(on, jnp.sin(ang), 0.0)

    return pl.pallas_call(
        body, name=name, grid=(s // tb,),
        in_specs=[pl.BlockSpec((tb, 1), lambda i: (i, 0)), pl.BlockSpec((1, LANES), lambda i: (0, 0))],
        out_specs=[pl.BlockSpec((tb, LANES), lambda i: (i, 0))] * 2,
        out_shape=[jax.ShapeDtypeStruct((s, LANES), F32)] * 2,
        compiler_params=_params(("parallel",)),
    )(pos_col, inv_freq_lanes)


def _rotate_half(x):
    lane = lax.broadcasted_iota(jnp.int32, x.shape, 1)
    half = QK_ROPE // 2
    first = (lane >= QK_NOPE) & (lane < QK_NOPE + half)
    second = (lane >= QK_NOPE + half) & (lane < QK_DIM)
    return jnp.where(first, -pltpu.roll(x, LANES - half, 1), jnp.where(second, pltpu.roll(x, half, 1), 0.0))


def _norm_mod(x, g, sc, sh, *, name):
    s, d = x.shape
    tb = _rows(s)

    def body(x_ref, g_ref, sc_ref, sh_ref, h_ref, r_ref):
        xv = x_ref[...]
        r = lax.rsqrt(jnp.mean(xv * xv, axis=-1, keepdims=True) + EPS)
        r_ref[...] = r
        h_ref[...] = (((xv * r) * g_ref[...]) * (1.0 + sc_ref[...]) + sh_ref[...]).astype(BF16)

    vec = pl.BlockSpec((1, d), lambda i: (0, 0))
    return pl.pallas_call(
        body, name=name, grid=(s // tb,),
        in_specs=[pl.BlockSpec((tb, d), lambda i: (i, 0)), vec, vec, vec],
        out_specs=[pl.BlockSpec((tb, d), lambda i: (i, 0)), pl.BlockSpec((tb, 1), lambda i: (i, 0))],
        out_shape=[jax.ShapeDtypeStruct((s, d), BF16), jax.ShapeDtypeStruct((s, 1), F32)],
        compiler_params=_params(("parallel",)),
    )(x, g, sc, sh)


def _window_sums(ext, sign):
    n = ext.shape[0]
    sums, cur, k = [], ext, 1
    for _ in POOL_WINDOWS:
        cur = cur + pltpu.roll(cur, k if sign > 0 else n - k, 0)
        sums.append(cur)
        k *= 2
    return sums


def _mixer_pre(z, cos_t, sin_t, w_pool, pool_scale, gq, gkv, *, name):
    s = z.shape[0]
    tb = _rows(s)
    hb = tb // HALO

    def body(zcq_ref, zkr_ref, zu_ref, zuh_ref, zckv_ref, cos_ref, sin_ref, wp_ref, ps_ref, gq_ref, gkv_ref,
             p_ref, yp_ref, cq_ref, ckv_ref, kr_ref, rq_ref, rkv_ref):
        i = pl.program_id(0)
        u = zu_ref[...].astype(F32)
        halo = jnp.where(i > 0, zuh_ref[...].astype(F32), 0.0)
        ext = jnp.concatenate([halo, u], axis=0)
        t = i * tb + lax.broadcasted_iota(jnp.int32, (tb, 1), 0)
        for g, (w, sw) in enumerate(zip(POOL_WINDOWS, _window_sums(ext, +1))):
            cols = slice(g * POOL_GROUP, (g + 1) * POOL_GROUP)
            cnt = jnp.minimum(t + 1, w).astype(F32)
            pg = (sw[HALO:, cols] / cnt - u[:, cols]).astype(BF16)
            p_ref[:, cols] = pg
            yg = jnp.dot(pg, wp_ref[g].astype(BF16), preferred_element_type=F32)
            yp_ref[:, cols] = (yg * ps_ref[:, cols]).astype(BF16)

        def rms(x_ref, g_ref, out_ref, r_ref):
            xv = x_ref[...].astype(F32)
            r = lax.rsqrt(jnp.mean(xv * xv, axis=-1, keepdims=True) + EPS)
            r_ref[...] = r
            out_ref[...] = ((xv * r) * g_ref[...]).astype(BF16)

        rms(zcq_ref, gq_ref, cq_ref, rq_ref)
        rms(zckv_ref, gkv_ref, ckv_ref, rkv_ref)
        kr = zkr_ref[...].astype(F32)
        kr_ref[...] = (kr * cos_ref[...] + _rotate_half(kr) * sin_ref[...]).astype(BF16)

    def zcol(width, off):
        return pl.BlockSpec((tb, width), lambda i: (i, off // width))

    def full(a):
        return pl.BlockSpec(a.shape, lambda i: (0,) * a.ndim)

    def out(width, dt):
        return pl.BlockSpec((tb, width), lambda i: (i, 0)), jax.ShapeDtypeStruct((s, width), dt)

    outs = [out(POOL_DIM, BF16), out(POOL_DIM, BF16), out(Q_LORA, BF16), out(KV_LORA, BF16), out(LANES, BF16),
            out(1, F32), out(1, F32)]
    return pl.pallas_call(
        body, name=name, grid=(s // tb,),
        in_specs=[zcol(Q_LORA, ZC_CQ), zcol(LANES, ZC_KR), zcol(POOL_DIM, ZC_U),
                  pl.BlockSpec((HALO, POOL_DIM), lambda i: (jnp.maximum(i * hb - 1, 0), ZC_U // POOL_DIM)),
                  zcol(KV_LORA, ZC_CKV),
                  pl.BlockSpec((tb, LANES), lambda i: (i, 0)), pl.BlockSpec((tb, LANES), lambda i: (i, 0)),
                  full(w_pool), full(pool_scale), full(gq), full(gkv)],
        out_specs=[o[0] for o in outs], out_shape=[o[1] for o in outs],
        compiler_params=_params(("parallel",)),
    )(z, z, z, z, z, cos_t, sin_t, w_pool, pool_scale, gq, gkv)


def _sigmoid(x):
    return 1.0 / (1.0 + jnp.exp(-x.astype(F32)))


ATTN_SCALE = 1.0 / math.sqrt(QK_DIM)
NEG_BIG = -1e30


LOG2_E = math.log2(math.e)
EXP2_SCALE = ATTN_SCALE * LOG2_E
NT_DIMS = (((1,), (1,)), ((), ()))
TN_DIMS = (((0,), (0,)), ((), ()))


def _on_or_below_diagonal(t):
    return lax.broadcasted_iota(jnp.int32, (t, t), 0) >= lax.broadcasted_iota(jnp.int32, (t, t), 1)


HEADS_PER_STEP = 2
HEAD_COLS = [slice(g * HEAD_PAD, (g + 1) * HEAD_PAD) for g in range(HEADS_PER_STEP)]


def _attn_fwd(q, k, v, *, name):
    s = q.shape[0]
    t = _rows(s)
    wide = HEADS_PER_STEP * HEAD_PAD

    def body(q_ref, k_ref, v_ref, o_ref, lse_ref):
        qi = pl.program_id(1)
        qs = [q_ref[:, cols] for cols in HEAD_COLS]

        def block(j, carry, diagonal):
            rows = pl.ds(pl.multiple_of(j * t, t), t)
            out = []
            for qv, cols, (m, l, acc) in zip(qs, HEAD_COLS, carry):
                sc = lax.dot_general(qv, k_ref[rows, cols], NT_DIMS, preferred_element_type=F32)
                if diagonal:
                    sc = jnp.where(_on_or_below_diagonal(t), sc, NEG_BIG)
                m_new = jnp.maximum(m, jnp.max(sc, axis=-1, keepdims=True))
                p = jnp.exp2((sc - m_new) * EXP2_SCALE)
                alpha = jnp.exp2((m - m_new) * EXP2_SCALE)
                l = alpha * l + jnp.sum(p, axis=-1, keepdims=True)
                acc = alpha * acc + jnp.dot(p.astype(BF16), v_ref[rows, cols], preferred_element_type=F32)
                out.append((m_new, l, acc))
            return tuple(out)

        init = tuple((jnp.full((t, 1), -jnp.inf, F32), jnp.zeros((t, 1), F32), jnp.zeros((t, HEAD_PAD), F32))
                     for _ in HEAD_COLS)
        carry = lax.fori_loop(0, qi, lambda j, c: block(j, c, False), init)
        for g, (cols, (m, l, acc)) in enumerate(zip(HEAD_COLS, block(qi, carry, True))):
            o_ref[:, cols] = (acc / l).astype(BF16)
            lse_ref[g] = m * ATTN_SCALE + jnp.log(l)

    q_spec = pl.BlockSpec((t, wide), lambda h, i: (i, h))
    kv_spec = pl.BlockSpec((s, wide), lambda h, i: (0, h))
    return pl.pallas_call(
        body, name=name, grid=(N_HEADS // HEADS_PER_STEP, s // t),
        in_specs=[q_spec, kv_spec, kv_spec],
        out_specs=[q_spec, pl.BlockSpec((HEADS_PER_STEP, t, 1), lambda h, i: (h, i, 0))],
        out_shape=[jax.ShapeDtypeStruct((s, N_HEADS * HEAD_PAD), BF16), jax.ShapeDtypeStruct((N_HEADS, s, 1), F32)],
        compiler_params=_params(("parallel", "parallel")),
    )(q, k, v)


def _attn_bwd(q, k, v, do, o, lse, cos_t, sin_t, *, name):
    s = q.shape[0]
    t = _rows(s)
    nt = s // t

    def body(q_ref, k_ref, v_ref, do_ref, o_ref, lse_ref, cos_ref, sin_ref, dql_ref, dk_ref, dv_ref, dks_ref,
             dq_ref, dl_ref):
        kj = pl.program_id(1)

        @pl.when(kj == 0)
        def _():
            dq_ref[...] = jnp.zeros_like(dq_ref)

            def delta(i, carry):
                rows = pl.ds(pl.multiple_of(i * t, t), t)
                for g, cols in enumerate(HEAD_COLS):
                    dl_ref[g, rows, :] = jnp.sum(do_ref[rows, cols].astype(F32) * o_ref[rows, cols].astype(F32),
                                                 axis=-1, keepdims=True)
                return carry

            lax.fori_loop(0, nt, delta, 0)

        kvs = [(k_ref[:, cols], v_ref[:, cols]) for cols in HEAD_COLS]

        def block(i, carry, diagonal):
            rows = pl.ds(pl.multiple_of(i * t, t), t)
            out = []
            for g, (cols, (kv, vv), (dk, dv)) in enumerate(zip(HEAD_COLS, kvs, carry)):
                qv, dov = q_ref[rows, cols], do_ref[rows, cols]
                sc = lax.dot_general(qv, kv, NT_DIMS, preferred_element_type=F32)
                p = jnp.exp2(sc * EXP2_SCALE - lse_ref[g, rows, :] * LOG2_E)
                if diagonal:
                    p = jnp.where(_on_or_below_diagonal(t), p, 0.0)
                dp = lax.dot_general(dov, vv, NT_DIMS, preferred_element_type=F32)
                ds = (p * (dp - dl_ref[g, rows, :])).astype(BF16)
                dv = dv + lax.dot_general(p.astype(BF16), dov, TN_DIMS, preferred_element_type=F32)
                dk = dk + lax.dot_general(ds, qv, TN_DIMS, preferred_element_type=F32)
                dq_ref[rows, cols] += jnp.dot(ds, kv, preferred_element_type=F32) * ATTN_SCALE
                out.append((dk, dv))
            return tuple(out)

        zero = jnp.zeros((t, HEAD_PAD), F32)
        carry = block(kj, tuple((zero, zero) for _ in HEAD_COLS), True)
        dk_sum = None
        for cols, (dk, dv) in zip(HEAD_COLS, lax.fori_loop(kj + 1, nt, lambda i, c: block(i, c, False), carry)):
            dk = dk * ATTN_SCALE
            dk_ref[:, cols] = dk.astype(BF16)
            dv_ref[:, cols] = dv.astype(BF16)
            dk_sum = dk if dk_sum is None else dk_sum + dk
        dks_ref[...] = dk_sum

        @pl.when(kj == nt - 1)
        def _():
            def rope_bwd(i, carry):
                rows = pl.ds(pl.multiple_of(i * t, t), t)
                sin = sin_ref[rows, :]
                lane = lax.broadcasted_iota(jnp.int32, sin.shape, 1)
                cos_q = cos_ref[rows, :] + jnp.where(lane < QK_NOPE, 1.0, 0.0)
                for cols in HEAD_COLS:
                    dqv = dq_ref[rows, cols]
                    dql_ref[rows, cols] = (dqv * cos_q - _rotate_half(dqv * sin)).astype(BF16)
                return carry

            lax.fori_loop(0, nt, rope_bwd, 0)

    heads_wide = HEADS_PER_STEP * HEAD_PAD
    full_spec = pl.BlockSpec((s, heads_wide), lambda h, j: (0, h))
    kv_spec = pl.BlockSpec((t, heads_wide), lambda h, j: (j, h))
    vec_spec = pl.BlockSpec((HEADS_PER_STEP, s, 1), lambda h, j: (h, 0, 0))
    table_spec = pl.BlockSpec((s, LANES), lambda h, j: (0, 0))
    wide = jax.ShapeDtypeStruct((s, N_HEADS * HEAD_PAD), BF16)
    n_steps = N_HEADS // HEADS_PER_STEP
    return pl.pallas_call(
        body, name=name, grid=(n_steps, nt),
        in_specs=[full_spec, kv_spec, kv_spec, full_spec, full_spec, vec_spec, table_spec, table_spec],
        out_specs=[full_spec, kv_spec, kv_spec, pl.BlockSpec((None, t, HEAD_PAD), lambda h, j: (h, j, 0))],
        out_shape=[wide, wide, wide, jax.ShapeDtypeStruct((n_steps, s, HEAD_PAD), F32)],
        scratch_shapes=[pltpu.VMEM((s, heads_wide), F32), pltpu.VMEM((HEADS_PER_STEP, s, 1), F32)],
        compiler_params=_params(("parallel", "arbitrary")),
    )(q, k, v, do, o, lse, cos_t, sin_t)


def _acc_specs(widths):
    return ([pl.BlockSpec((1, w), lambda i: (0, 0)) for w in widths],
            [jax.ShapeDtypeStruct((1, w), F32) for w in widths])


def _gate_grads(dxv, m_ref, gate_ref, dm_ref, dgate_ref):
    dm_ref[...] = (dxv * gate_ref[...]).astype(BF16)
    dgate_ref[...] += jnp.sum(dxv * m_ref[...], axis=0, keepdims=True)


def _final_loss(x, g, target, m, gate, *, name):
    s, d = x.shape
    tb = _rows(s)

    def body(x_ref, g_ref, t_ref, m_ref, gate_ref, dx_ref, loss_ref, dg_ref, dm_ref, dgate_ref):
        @pl.when(pl.program_id(0) == 0)
        def _():
            loss_ref[...] = jnp.zeros_like(loss_ref)
            dg_ref[...] = jnp.zeros_like(dg_ref)
            dgate_ref[...] = jnp.zeros_like(dgate_ref)

        xv = x_ref[...]
        r = lax.rsqrt(jnp.mean(xv * xv, axis=-1, keepdims=True) + EPS)
        xn = xv * r
        err = xn * g_ref[...] - t_ref[...]
        loss_ref[...] += 0.5 * jnp.sum(jnp.mean(err * err, axis=-1, keepdims=True), axis=0, keepdims=True)
        dy = err / d
        dg_ref[...] += jnp.sum(dy * xn, axis=0, keepdims=True)
        dxn = dy * g_ref[...]
        dxv = r * (dxn - xn * jnp.mean(dxn * xn, axis=-1, keepdims=True))
        dx_ref[...] = dxv
        _gate_grads(dxv, m_ref, gate_ref, dm_ref, dgate_ref)

    blk = pl.BlockSpec((tb, d), lambda i: (i, 0))
    vec = pl.BlockSpec((1, d), lambda i: (0, 0))
    acc_specs, acc_shapes = _acc_specs((LANES, d))
    return pl.pallas_call(
        body, name=name, grid=(s // tb,),
        in_specs=[blk, vec, blk, blk, vec],
        out_specs=[blk] + acc_specs + [blk, vec],
        out_shape=[jax.ShapeDtypeStruct((s, d), F32)] + acc_shapes + [jax.ShapeDtypeStruct((s, d), BF16),
                                                                     jax.ShapeDtypeStruct((1, d), F32)],
        compiler_params=_params(("arbitrary",)),
    )(x, g, target, m, gate)


def _norm_mod_bwd(dh, x, r, g, sc, dx_skip, *, name, gate=None):
    s, d = x.shape
    tb = _rows(s)
    nb = s // tb
    n_gate = 2 if gate else 0

    def body(dh_ref, x_ref, r_ref, g_ref, sc_ref, skip_ref, *rest):
        gate_refs, (dx_ref, dg_ref, dsc_ref, dsh_ref) = rest[:n_gate], rest[n_gate:n_gate + 4]
        gate_outs, da_sc = rest[n_gate + 4:-1], rest[-1]
        i = pl.program_id(0)

        @pl.when(i == 0)
        def _():
            da_sc[...] = jnp.zeros_like(da_sc)
            dsh_ref[...] = jnp.zeros_like(dsh_ref)
            if gate:
                gate_outs[1][...] = jnp.zeros_like(gate_outs[1])

        dhv, rv = dh_ref[...], r_ref[...]
        xn = x_ref[...] * rv
        dsh_ref[...] += jnp.sum(dhv, axis=0, keepdims=True)
        da_sc[...] += jnp.sum(dhv * xn, axis=0, keepdims=True)
        dxn = dhv * (g_ref[...] * (1.0 + sc_ref[...]))
        dxv = skip_ref[...] + rv * (dxn - xn * jnp.mean(dxn * xn, axis=-1, keepdims=True))
        dx_ref[...] = dxv
        if gate:
            _gate_grads(dxv, *gate_refs, *gate_outs)

        @pl.when(i == nb - 1)
        def _():
            dg_ref[...] = da_sc[...] * (1.0 + sc_ref[...])
            dsc_ref[...] = da_sc[...] * g_ref[...]

    blk = pl.BlockSpec((tb, d), lambda i: (i, 0))
    vec = pl.BlockSpec((1, d), lambda i: (0, 0))
    acc_specs, acc_shapes = _acc_specs((d, d, d))
    gate_specs = [blk, vec] if gate else []
    gate_shapes = [jax.ShapeDtypeStruct((s, d), BF16), jax.ShapeDtypeStruct((1, d), F32)] if gate else []
    return pl.pallas_call(
        body, name=name, grid=(nb,),
        in_specs=[blk, blk, pl.BlockSpec((tb, 1), lambda i: (i, 0)), vec, vec, blk] + gate_specs,
        out_specs=[blk] + acc_specs + gate_specs,
        out_shape=[jax.ShapeDtypeStruct((s, d), F32)] + acc_shapes + gate_shapes,
        scratch_shapes=[pltpu.VMEM((1, d), F32)],
        compiler_params=_params(("arbitrary",)),
    )(dh, x, r, g, sc, dx_skip, *(gate or ()))


def _pool_bwd(dyp, p, w_pool, pool_scale, *, name):
    s = dyp.shape[0]
    tb = _rows(s)
    nb = s // tb
    hb = tb // HALO
    nt_dims = (((1,), (1,)), ((), ()))
    tn_dims = (((0,), (0,)), ((), ()))

    def body(dy_ref, dyn_ref, p_ref, wp_ref, ps_ref, du_ref, gwp_ref, gps_ref):
        i = pl.program_id(0)

        @pl.when(i == 0)
        def _():
            gwp_ref[...] = jnp.zeros_like(gwp_ref)
            gps_ref[...] = jnp.zeros_like(gps_ref)

        cur = dy_ref[...]
        nxt = jnp.where(i < nb - 1, dyn_ref[...], 0.0)
        dpw = (jnp.concatenate([cur, nxt], axis=0) * ps_ref[...]).astype(BF16)
        t = i * tb + lax.broadcasted_iota(jnp.int32, (tb + HALO, 1), 0)
        for g, w in enumerate(POOL_WINDOWS):
            cols = slice(g * POOL_GROUP, (g + 1) * POOL_GROUP)
            wg = wp_ref[g].astype(BF16)
            dp = lax.dot_general(dpw[:, cols], wg, nt_dims, preferred_element_type=F32)
            e = dp / jnp.minimum(t + 1, w).astype(F32)
            lead = _window_sums(e, -1)[g]
            du_ref[:, cols] = (lead[:tb] - dp[:tb]).astype(BF16)
            pg = p_ref[:, cols]
            pw = jnp.dot(pg, wg, preferred_element_type=F32)
            gps_ref[:, cols] += jnp.sum(cur[:, cols] * pw, axis=0, keepdims=True)
            gwp_ref[g] += lax.dot_general(pg, dpw[:tb, cols], tn_dims, preferred_element_type=F32)

    blk = pl.BlockSpec((tb, POOL_DIM), lambda i: (i, 0))
    return pl.pallas_call(
        body, name=name, grid=(nb,),
        in_specs=[blk, pl.BlockSpec((HALO, POOL_DIM), lambda i: (jnp.minimum((i + 1) * hb, s // HALO - 1), 0)), blk,
                  pl.BlockSpec(w_pool.shape, lambda i: (0, 0, 0)), pl.BlockSpec((1, POOL_DIM), lambda i: (0, 0))],
        out_specs=[blk, pl.BlockSpec(w_pool.shape, lambda i: (0, 0, 0)), pl.BlockSpec((1, POOL_DIM), lambda i: (0, 0))],
        out_shape=[jax.ShapeDtypeStruct((s, POOL_DIM), BF16), jax.ShapeDtypeStruct(w_pool.shape, F32),
                   jax.ShapeDtypeStruct((1, POOL_DIM), F32)],
        compiler_params=_params(("arbitrary",)),
    )(dyp, dyp, p, w_pool, pool_scale)


def _key_bwd(dk_sums, cos_t, sin_t, *, name):
    n, s, _ = dk_sums.shape
    tb = _rows(s)

    def body(dk_ref, cos_ref, sin_ref, dkr_ref):
        tot = dk_ref[0]
        for h in range(1, n):
            tot = tot + dk_ref[h]
        dkr_ref[...] = (tot * cos_ref[...] - _rotate_half(tot * sin_ref[...])).astype(BF16)

    tab = pl.BlockSpec((tb, LANES), lambda i: (i, 0))
    return pl.pallas_call(
        body, name=name, grid=(s // tb,),
        in_specs=[pl.BlockSpec((n, tb, LANES), lambda i: (0, i, 0)), tab, tab], out_specs=tab,
        out_shape=jax.ShapeDtypeStruct((s, LANES), BF16),
        compiler_params=_params(("parallel",)),
    )(dk_sums, cos_t, sin_t)


def _rms_bwd(dy, z, z_off, r, g, *, name):
    s, n = dy.shape
    tb = _rows(s)

    def body(dy_ref, x_ref, r_ref, g_ref, dx_ref, dg_ref):
        @pl.when(pl.program_id(0) == 0)
        def _():
            dg_ref[...] = jnp.zeros_like(dg_ref)

        dyv, rv = dy_ref[...], r_ref[...]
        xn = x_ref[...].astype(F32) * rv
        dg_ref[...] += jnp.sum(dyv * xn, axis=0, keepdims=True)
        dxn = dyv * g_ref[...]
        dx_ref[...] = (rv * (dxn - xn * jnp.mean(dxn * xn, axis=-1, keepdims=True))).astype(BF16)

    blk = pl.BlockSpec((tb, n), lambda i: (i, 0))
    acc_specs, acc_shapes = _acc_specs((n,))
    return pl.pallas_call(
        body, name=name, grid=(s // tb,),
        in_specs=[blk, pl.BlockSpec((tb, n), lambda i: (i, z_off // n)), pl.BlockSpec((tb, 1), lambda i: (i, 0)),
                  pl.BlockSpec((1, n), lambda i: (0, 0))],
        out_specs=[blk] + acc_specs, out_shape=[jax.ShapeDtypeStruct((s, n), BF16)] + acc_shapes,
        compiler_params=_params(("arbitrary",)),
    )(dy, z, r, g)


def _sum_slots(a, n, *, name, out_dtype=F32):
    _, rows, cols = a.shape
    tr = _tile(rows, 256, 8)

    def body(a_ref, out_ref):
        tot = a_ref[0].astype(F32)
        for j in range(1, n):
            tot = tot + a_ref[j].astype(F32)
        out_ref[...] = tot.astype(out_dtype)

    return pl.pallas_call(
        body, name=name, grid=(rows // tr,),
        in_specs=[pl.BlockSpec((n, tr, cols), lambda i: (0, i, 0))],
        out_specs=pl.BlockSpec((tr, cols), lambda i: (i, 0)),
        out_shape=jax.ShapeDtypeStruct((rows, cols), out_dtype),
        compiler_params=_params(("parallel",)),
    )(a)


def _add2_stacked(a, b, stacked, l, *, name):
    rows, cols = a.shape
    tr = _tile(rows, 256, 8)

    def body(a_ref, b_ref, *rest):
        rest[-1][...] = a_ref[...] + b_ref[...]

    blk = pl.BlockSpec((tr, cols), lambda i: (i, 0))
    carried = [] if stacked is None else [stacked]
    return pl.pallas_call(
        body, name=name, grid=(rows // tr,),
        in_specs=[blk, blk] + [pl.BlockSpec(memory_space=pl.ANY) for _ in carried],
        out_specs=pl.BlockSpec((None, tr, cols), lambda i: (l, i, 0)),
        out_shape=jax.ShapeDtypeStruct((DEPTH, rows, cols), F32),
        input_output_aliases={2: 0} if carried else {},
        compiler_params=_params(("parallel",)),
    )(a, b, *carried)


def _adamw(w, g, m, v, *, name):
    shape = w.shape
    if w.ndim == 2:
        w, g, m, v = (a.reshape((1,) + shape) for a in (w, g, m, v))
    layers, rows, cols = w.shape
    tr = _tile(rows, max(8, (1 << 18) // cols), 8)
    c1 = 1.0 - ADAM_B1 ** ADAM_STEP
    c2 = 1.0 - ADAM_B2 ** ADAM_STEP

    def body(w_ref, g_ref, m_ref, v_ref, d_ref, nm_ref, nv_ref):
        gv = g_ref[...]
        nm = ADAM_B1 * m_ref[...] + (1.0 - ADAM_B1) * gv
        nv = ADAM_B2 * v_ref[...] + (1.0 - ADAM_B2) * (gv * gv)
        nm_ref[...] = nm
        nv_ref[...] = nv
        d_ref[...] = -ADAM_LR * ((nm / c1) / (jnp.sqrt(nv / c2) + ADAM_EPS) + ADAM_WD * w_ref[...])

    blk = pl.BlockSpec((None, tr, cols), lambda l, i: (l, i, 0))
    outs = pl.pallas_call(
        body, name=name, grid=(layers, rows // tr), in_specs=[blk] * 4, out_specs=[blk] * 3,
        out_shape=[jax.ShapeDtypeStruct((layers, rows, cols), F32)] * 3,
        compiler_params=_params(("parallel", "parallel")),
    )(w, g, m, v)
    return [o.reshape(shape) for o in outs]


def _adamw_summed(w, halves, m, v, *, name):
    layers, rows, cols = w.shape
    tr = _tile(rows, max(8, (1 << 18) // cols), 8)
    tc = _tile(cols, max(LANES, (1 << 18) // tr))
    nb = rows // tr
    c1 = 1.0 - ADAM_B1 ** ADAM_STEP
    c2 = 1.0 - ADAM_B2 ** ADAM_STEP

    def body(w_ref, m_ref, v_ref, *rest):
        half_refs, (g_ref, d_ref, nm_ref, nv_ref) = rest[:2 * layers], rest[2 * layers:]
        for k in range(layers):
            @pl.when(pl.program_id(0) == k)
            def _(k=k):
                gv = half_refs[2 * k][...] + half_refs[2 * k + 1][...]
                g_ref[...] = gv
                nm = ADAM_B1 * m_ref[...] + (1.0 - ADAM_B1) * gv
                nv = ADAM_B2 * v_ref[...] + (1.0 - ADAM_B2) * (gv * gv)
                nm_ref[...] = nm
                nv_ref[...] = nv
                d_ref[...] = -ADAM_LR * ((nm / c1) / (jnp.sqrt(nv / c2) + ADAM_EPS) + ADAM_WD * w_ref[...])

    def half_spec(k):
        def index(l, i, j):
            at_k = l == k
            return (jnp.where(at_k, i, jnp.where(l < k, 0, nb - 1)), jnp.where(at_k, j, jnp.where(l < k, 0, cols // tc - 1)))
        return pl.BlockSpec((tr, tc), index)

    blk = pl.BlockSpec((None, tr, tc), lambda l, i, j: (l, i, j))
    return pl.pallas_call(
        body, name=name, grid=(layers, nb, cols // tc),
        in_specs=[blk] * 3 + [half_spec(k) for k in range(layers) for _ in range(2)], out_specs=[blk] * 4,
        out_shape=[jax.ShapeDtypeStruct((layers, rows, cols), F32)] * 4,
        compiler_params=_params(("parallel", "parallel", "parallel")),
    )(w, m, v, *[half for pair in halves for half in pair])


def _coords():
    return lax.axis_index("x"), lax.axis_index("y"), lax.axis_index("c")


def _other_chips(x, y):
    return [(1 - x, y), (x, 1 - y), (1 - x, 1 - y)]


def _gather_rows(x_ref, out_ref, send_sems, recv_sems, local_sem, m_per):
    x, y, c = _coords()
    me, sibling = (x, y, c), (x, y, 1 - c)
    chips = _other_chips(x, y)

    def rows(px, py, pc):
        return out_ref.at[pl.ds((4 * px + 2 * py + pc) * m_per, m_per), :]

    def copy(k, block, to, src=None):
        return pltpu.make_async_remote_copy(
            src_ref=rows(*block) if src is None else src, dst_ref=rows(*block),
            send_sem=send_sems.at[k], recv_sem=recv_sems.at[k], device_id=to, device_id_type=MESH)

    mine = pltpu.make_async_copy(x_ref, rows(*me), local_sem)
    mine.start()
    first = [copy(0, me, sibling, src=x_ref)]
    first += [copy(1 + j, me, (*chip, c), src=x_ref) for j, chip in enumerate(chips)]
    for cp in first:
        cp.start()
    passed = [copy(4 + j, (*chip, c), sibling) for j, chip in enumerate(chips)]
    for j, chip in enumerate(chips):
        copy(1 + j, (*chip, c), me).wait_recv()
        passed[j].start()
    copy(0, sibling, me).wait_recv()
    for j, chip in enumerate(chips):
        copy(4 + j, (*chip, 1 - c), me).wait_recv()
    for cp in first + passed:
        cp.wait_send()
    mine.wait()


def _ada_modulation(c_blk, w_ada, b_mine, *, name):
    depth, _, cols = w_ada.shape
    blk_rows = c_blk.shape[0]
    sems = [pltpu.SemaphoreType.DMA((7,)), pltpu.SemaphoreType.DMA((7,)), pltpu.SemaphoreType.DMA]

    def body(c_ref, w_ref, b_ref, act_ref, mod_ref, c_all, prod_all, mod_mine, *sem_refs):
        _gather_rows(c_ref, c_all, *sem_refs[:3], blk_rows)
        cv = c_all[...]
        c_all[...] = cv * _sigmoid(cv)
        lhs = c_all[...].astype(BF16)
        for d in range(N_DEV):
            act_ref[d:d + 1, :] = c_all[d * blk_rows:d * blk_rows + 1, :]
        for l in range(depth):
            prod_all[...] = (jnp.dot(lhs, w_ref[l].astype(BF16), preferred_element_type=F32)
                             + b_ref[:, l * cols:(l + 1) * cols])
            for d in range(N_DEV):
                mod_mine[l * N_DEV + d:l * N_DEV + d + 1, :] = prod_all[d * blk_rows:d * blk_rows + 1, :]
        _gather_rows(mod_mine, mod_ref, *sem_refs[3:], depth * N_DEV)

    vmem = pl.BlockSpec(memory_space=pltpu.VMEM)
    act, mod = pl.pallas_call(
        body, name=name,
        out_shape=[jax.ShapeDtypeStruct((N_DEV, D_MODEL), F32), jax.ShapeDtypeStruct((N_DEV * depth * N_DEV, cols), F32)],
        in_specs=[vmem, vmem, vmem], out_specs=[vmem, vmem],
        scratch_shapes=[pltpu.VMEM((N_DEV * blk_rows, D_MODEL), F32), pltpu.VMEM((N_DEV * blk_rows, cols), F32),
                        pltpu.VMEM((depth * N_DEV, cols), F32)] + sems + sems,
        compiler_params=_params(),
    )(c_blk, w_ada, b_mine)
    return act.astype(BF16), mod


def _all_gather_small(blk, *, name, swaps=()):
    m_per, n = blk.shape
    n_swaps = len(swaps)

    def body(x_ref, *refs):
        swap_srcs, out_ref, swap_outs = refs[:n_swaps], refs[n_swaps], refs[n_swaps + 1:2 * n_swaps + 1]
        send_sems, recv_sems, local_sem = refs[2 * n_swaps + 1:2 * n_swaps + 4]
        x, y, c = _coords()
        swapping = [pltpu.make_async_remote_copy(src_ref=src, dst_ref=dst, send_sem=refs[-2].at[k], recv_sem=refs[-1].at[k],
                                                 device_id=(x, y, 1 - c), device_id_type=MESH)
                    for k, (src, dst) in enumerate(zip(swap_srcs, swap_outs))]
        for cp in swapping:
            cp.start()
        _gather_rows(x_ref, out_ref, send_sems, recv_sems, local_sem, m_per)
        for cp in swapping:
            cp.wait()

    any_spec = pl.BlockSpec(memory_space=pl.ANY)
    outs = pl.pallas_call(
        body, name=name,
        out_shape=[jax.ShapeDtypeStruct((N_DEV * m_per, n), blk.dtype)] + [jax.ShapeDtypeStruct(a.shape, a.dtype)
                                                                           for a in swaps],
        in_specs=[pl.BlockSpec(memory_space=pltpu.VMEM)] + [any_spec] * n_swaps,
        out_specs=[pl.BlockSpec(memory_space=pltpu.VMEM)] + [any_spec] * n_swaps,
        scratch_shapes=[pltpu.SemaphoreType.DMA((7,)), pltpu.SemaphoreType.DMA((7,)), pltpu.SemaphoreType.DMA]
        + ([pltpu.SemaphoreType.DMA((n_swaps,)), pltpu.SemaphoreType.DMA((n_swaps,))] if swaps else []),
        compiler_params=_params(),
    )(blk, *swaps)
    return (outs[0], outs[1:]) if swaps else outs[0]


HBM_SPEC = pl.BlockSpec(memory_space=pltpu.HBM)
SEM_SPEC = pl.BlockSpec(memory_space=pltpu.SEMAPHORE)
DATAFLOW = pltpu.SideEffectType.DATAFLOW_SIDE_EFFECTING


def _chip_copies(src_ref, land_ref, send_sems, recv_sems, scatter):
    x, y, c = _coords()
    my = 2 * x + y
    outgoing, incoming = [], []
    for k, (px, py) in enumerate(_other_chips(x, y)):
        peer = 2 * px + py

        def copy(src_slot, dst_slot):
            return pltpu.make_async_remote_copy(
                src_ref=src_ref.at[src_slot] if scatter else src_ref, dst_ref=land_ref.at[dst_slot],
                send_sem=send_sems.at[k], recv_sem=recv_sems.at[k], device_id=(px, py, c), device_id_type=MESH)

        outgoing.append(copy(peer, my))
        incoming.append(copy(my, peer))
    return outgoing, incoming


def _exchange_start(srcs, *, name, scatter):
    n = len(srcs)
    land_shapes = [src.shape if scatter else (N_CHIPS,) + src.shape for src in srcs]

    def body(*refs):
        for k in range(n):
            send_sems, recv_sems = refs[2 * n + 4 * k], refs[2 * n + 4 * k + 1]
            outgoing, _ = _chip_copies(refs[k], refs[n + k], send_sems, recv_sems, scatter)
            for cp in outgoing:
                cp.start()
        refs[-1][...] = jnp.zeros_like(refs[-1])

    out_shape, out_specs, aliases = [], [], {}
    for k, (src, land_shape) in enumerate(zip(srcs, land_shapes)):
        out_shape += [pltpu.SemaphoreType.DMA((N_CHIPS - 1,)), pltpu.SemaphoreType.DMA((N_CHIPS - 1,)),
                      pltpu.HBM(src.shape, src.dtype), pltpu.HBM(land_shape, src.dtype)]
        out_specs += [SEM_SPEC, SEM_SPEC, HBM_SPEC, HBM_SPEC]
        aliases.update({k: 4 * k + 2, n + k: 4 * k + 3})
    outs = pl.pallas_call(
        body, name=name,
        out_shape=tuple(out_shape) + (jax.ShapeDtypeStruct((8, LANES), F32),),
        in_specs=(HBM_SPEC,) * (2 * n),
        out_specs=tuple(out_specs) + (pl.BlockSpec(memory_space=pltpu.VMEM),),
        input_output_aliases=aliases,
        compiler_params=pltpu.CompilerParams(has_side_effects=DATAFLOW),
    )(*[pltpu.with_memory_space_constraint(src, pltpu.HBM) for src in srcs],
      *[pltpu.with_memory_space_constraint(lax.empty(shape, src.dtype), pltpu.HBM)
        for src, shape in zip(srcs, land_shapes)])
    return [tuple(outs[4 * k:4 * k + 4]) for k in range(n)], outs[-1]


def _exchange_wait(started, after, *, name, scatter):
    send_sems, recv_sems, src_thru, land_thru = started

    def body(src_ref, land_ref, send_sems, recv_sems, after_ref, src_dead, got_ref):
        outgoing, incoming = _chip_copies(src_ref, land_ref, send_sems, recv_sems, scatter)
        for cp in outgoing:
            cp.wait_send()
        for cp in incoming:
            cp.wait_recv()

    return pl.pallas_call(
        body, name=name,
        out_shape=(pltpu.HBM(src_thru.shape, src_thru.dtype), pltpu.HBM(land_thru.shape, land_thru.dtype)),
        in_specs=(HBM_SPEC, HBM_SPEC, SEM_SPEC, SEM_SPEC, pl.BlockSpec(memory_space=pl.ANY)),
        out_specs=(HBM_SPEC, HBM_SPEC),
        input_output_aliases={0: 0, 1: 1},
        compiler_params=pltpu.CompilerParams(has_side_effects=DATAFLOW),
    )(src_thru, land_thru, send_sems, recv_sems, after)


def _pack_rows(a):
    return a.reshape(-1, D_MODEL)


def _pad_heads(w, width):
    r = w.shape[0]
    return jnp.pad(w, ((0, 0), (0, 0), (0, HEAD_PAD - width))).reshape(r, N_HEADS * HEAD_PAD)


MIX_NAMES = ("w_uq", "w_uk", "w_uv", "p_pool", "p_attn", "w_out")
GROUPS = ("in", "mix", "ff1", "ff2")


def _local_shard(weights, l, group, zero):
    if group == "mix":
        shard = jnp.concatenate([_pack_rows(weights[n][l]) for n in MIX_NAMES], axis=0)
    else:
        shard = weights[{"in": "w_in", "ff1": "w_ff1", "ff2": "w_ff2"}[group]][l]
    return (shard + zero).astype(BF16)


def _unpack_weights(gathered, group):
    def cols(a, k):
        return a.reshape(N_CHIPS, k, -1).transpose(1, 0, 2).reshape(k, -1)

    if group == "in":
        full = gathered.reshape(W_IN_COLS, D_MODEL)
        u, cq, ckv, kr, gates = (full[a:b] for a, b in (W_IN_U, W_IN_CQ, W_IN_CKV, W_IN_KR, W_IN_GATES))
        kr = jnp.pad(kr, ((QK_NOPE, HEAD_PAD - QK_DIM), (0, 0)))
        return dict(w_in=jnp.concatenate([cq, kr, u, gates, ckv], axis=0))
    if group == "ff1":
        return dict(w_ff1=gathered)
    if group == "ff2":
        return dict(w_ff2=gathered.reshape(D_FF, D_MODEL))

    def p_attn(a):
        full = cols(a, ATTN_DIM).reshape(N_HEADS, V_DIM, D_MODEL)
        return jnp.pad(full, ((0, 0), (0, HEAD_PAD - V_DIM), (0, 0))).reshape(N_HEADS * HEAD_PAD, D_MODEL)

    build = dict(
        w_uq=lambda a: _pad_heads(a.reshape(Q_LORA, N_HEADS, QK_DIM), QK_DIM),
        w_uk=lambda a: _pad_heads(a.reshape(KV_LORA, N_HEADS, QK_NOPE), QK_NOPE),
        w_uv=lambda a: _pad_heads(a.reshape(KV_LORA, N_HEADS, V_DIM), V_DIM),
        p_pool=lambda a: cols(a, POOL_DIM),
        p_attn=p_attn,
        w_out=lambda a: a.reshape(D_MODEL, D_MODEL),
    )
    w, off = {}, 0
    for name in MIX_NAMES:
        w[name] = build[name](gathered[:, off:off + ROWS_OF[name]])
        off += ROWS_OF[name]
    return w


def _pack_grads(g, group):
    def cols(a):
        k = a.shape[0]
        return a.reshape(k, N_CHIPS, -1).transpose(1, 0, 2).reshape(N_CHIPS, -1, D_MODEL)

    def rows(a):
        return a.reshape(N_CHIPS, -1, D_MODEL)

    def heads(width):
        return lambda a: rows(a.reshape(a.shape[0], N_HEADS, HEAD_PAD)[:, :, :width])

    if group == "in":
        full = jnp.concatenate([g["u"], g["cq"], g["ckv"], g["kr"][QK_NOPE:QK_DIM], g["ga"], g["gb"]], axis=0)
        return full.reshape(N_CHIPS, W_IN_SHARD, D_MODEL)
    if group == "ff1":
        return g["w_ff1"]
    if group == "ff2":
        return g["w_ff2"].reshape(N_CHIPS, D_FF // N_CHIPS, D_MODEL)

    def p_attn(a):
        return cols(a.reshape(N_HEADS, HEAD_PAD, D_MODEL)[:, :V_DIM].reshape(ATTN_DIM, D_MODEL))

    build = dict(w_uq=heads(QK_DIM), w_uk=heads(QK_NOPE), w_uv=heads(V_DIM), p_pool=cols, p_attn=p_attn, w_out=rows)
    return jnp.concatenate([build[name](g[name]) for name in MIX_NAMES], axis=1)


def _per_head(fn, acc, *tables):
    return jnp.concatenate([fn(acc[:, h * HEAD_PAD:(h + 1) * HEAD_PAD], *tables) for h in range(N_HEADS)], axis=1)


def _rope_head(a, cos, sin):
    lane = lax.broadcasted_iota(jnp.int32, a.shape, 1)
    return a * (cos + jnp.where(lane < QK_NOPE, 1.0, 0.0)) + _rotate_half(a) * sin


def _layer_fwd(l, x, mod, get_weights, small, cos_t, sin_t):
    sh1, sc1, g1, sh2, sc2, g2 = mod
    tag = f"_l{l}"
    h, r1 = _norm_mod(x, small["ln1_g"], sc1, sh1, name="norm1" + tag)
    w = dict(get_weights("in", h))
    (z,) = _mm(h, w["w_in"], tb=True, name="in_proj" + tag, out_dtypes=(BF16,))
    p, yp, cq, ckv, kr, rq, rkv = _mixer_pre(z, cos_t, sin_t, small["w_pool"], small["pool_scale"],
                                              small["q_norm_g"], small["kv_norm_g"], name="mixer_pre" + tag)
    w.update(get_weights("mix", yp))
    (ya,) = _mm(yp, w["p_pool"], name="pool_out" + tag, out_dtypes=(BF16,))
    (q,) = _mm(cq, w["w_uq"], name="q_proj" + tag, out_dtypes=(BF16,),
               epilogue=lambda acc, cos, sin: (_per_head(_rope_head, acc, cos, sin),),
               extras=((cos_t, "table"), (sin_t, "table")))
    (k,) = _mm(ckv, w["w_uk"], name="k_proj" + tag, out_dtypes=(BF16,),
               epilogue=lambda acc, krv: (_per_head(lambda a, b: a + b, acc, krv),), extras=((kr, "table"),))
    (v,) = _mm(ckv, w["w_uv"], name="v_proj" + tag, out_dtypes=(BF16,))
    o, lse = _attn_fwd(q, k, v, name="attn_fwd" + tag)
    yb, merged = _mm(o, w["p_attn"], name="attn_out" + tag, out_dtypes=(BF16, BF16), tm=512,
                     epilogue=lambda acc, ga, gb, yav: (acc, _sigmoid(ga) * yav + _sigmoid(gb) * acc),
                     extras=((z, ("tile", ZC_GA // D_MODEL)), (z, ("tile", ZC_GB // D_MODEL)), (ya, "tile")))
    mo, x1 = _mm(merged, w["w_out"], name="mix_out" + tag, out_dtypes=(BF16, F32),
                 epilogue=lambda acc, xr, g: (acc, xr + g * acc), extras=((x, "tile"), (g1, "row")))
    h2, r2 = _norm_mod(x1, small["ln2_g"], sc2, sh2, name="norm2" + tag)
    w.update(get_weights("ff1", merged))
    f, act = _mm(h2, w["w_ff1"], b_stack=True, name="ff1" + tag, out_dtypes=(BF16, BF16),
                 epilogue=lambda acc: (acc, jnp.square(jnp.maximum(acc, 0.0))))
    w.update(get_weights("ff2", act))
    m2, x2 = _mm(act, w["w_ff2"], name="ff2" + tag, out_dtypes=(BF16, F32),
                 epilogue=lambda acc, xr, g: (acc, xr + g * acc), extras=((x1, "tile"), (g2, "row")))
    saved = dict(x=x, h=h, r1=r1, z=z, p=p, yp=yp, cq=cq, ckv=ckv, rq=rq, rkv=rkv, ya=ya, q=q, k=k, v=v, o=o, lse=lse,
                 yb=yb, merged=merged, mo=mo, x1=x1, h2=h2, r2=r2, f=f, act=act, m2=m2)
    return x2, saved, w


def _merge_grads(dm, ga, gb, ya, yb):
    sa, sb = _sigmoid(ga), _sigmoid(gb)
    return dm * sa, dm * sb, dm * ya * (sa * (1.0 - sa)), dm * yb * (sb * (1.0 - sb))


def _layer_bwd(l, dx2, dm2, dg2, sv, mod, w, small, cos_t, sin_t, send_grads, gate_below):
    sh1, sc1, g1, sh2, sc2, g2 = mod
    tag = f"_l{l}"
    gw = {}
    (df,) = _mm(dm2, w["w_ff2"], tb=True, name="ff2_dx" + tag, out_dtypes=(BF16,),
                epilogue=lambda acc, f: (acc * (2.0 * jnp.maximum(f, 0.0)),), extras=((sv["f"], "tile"),))
    (g_ff2,) = _mm(sv["act"], dm2, ta=True, name="ff2_dw" + tag, out_dtypes=(BF16,))
    (g_ff1,) = _mm(sv["h2"], df, ta=True, out_stack=N_CHIPS, name="ff1_dw" + tag, out_dtypes=(BF16,))
    sc2 = sc2 + send_grads("ff2", dict(w_ff2=g_ff2)) + send_grads("ff1", dict(w_ff1=g_ff1))
    (dh2,) = _mm(df, w["w_ff1"], tb=True, b_stack=True, name="ff1_dx" + tag)
    dx1, dln2, dsc2, dsh2, dmo, dg1 = _norm_mod_bwd(dh2, sv["x1"], sv["r2"], small["ln2_g"], sc2, dx2,
                                                    gate=(sv["mo"], g1), name="norm2_bwd" + tag)
    dya, dyb, dga, dgb = _mm(dmo, w["w_out"], tb=True, name="mix_out_dx" + tag, out_dtypes=(BF16,) * 4, tm=512,
                             epilogue=_merge_grads,
                             extras=((sv["z"], ("tile", ZC_GA // D_MODEL)), (sv["z"], ("tile", ZC_GB // D_MODEL)),
                                     (sv["ya"], "tile"), (sv["yb"], "tile")))
    (gw["w_out"],) = _mm(sv["merged"], dmo, ta=True, name="mix_out_dw" + tag, out_dtypes=(BF16,))
    (gw["p_pool"],) = _mm(sv["yp"], dya, ta=True, name="pool_out_dw" + tag, out_dtypes=(BF16,))
    (dyp,) = _mm(dya, w["p_pool"], tb=True, name="pool_out_dx" + tag)
    du, g_w_pool, g_pool_scale = _pool_bwd(dyp, sv["p"], small["w_pool"], small["pool_scale"], name="pool_bwd" + tag)
    (gw["p_attn"],) = _mm(sv["o"], dyb, ta=True, name="attn_out_dw" + tag, out_dtypes=(BF16,))
    (do,) = _mm(dyb, w["p_attn"], tb=True, name="attn_out_dx" + tag, out_dtypes=(BF16,))
    dql, dkb, dv, dk_sums = _attn_bwd(sv["q"], sv["k"], sv["v"], do, sv["o"], sv["lse"], cos_t, sin_t,
                                      name="attn_bwd" + tag)
    dkr = _key_bwd(dk_sums, cos_t, sin_t, name="key_bwd" + tag)
    (gw["w_uq"],) = _mm(sv["cq"], dql, ta=True, name="q_proj_dw" + tag, out_dtypes=(BF16,))
    (gw["w_uk"],) = _mm(sv["ckv"], dkb, ta=True, name="k_proj_dw" + tag, out_dtypes=(BF16,))
    (gw["w_uv"],) = _mm(sv["ckv"], dv, ta=True, name="v_proj_dw" + tag, out_dtypes=(BF16,))
    (dcq,) = _mm(dql, w["w_uq"], tb=True, name="q_proj_dx" + tag)
    (dckv,) = _mm(dkb, w["w_uk"], tb=True, second=(dv, w["w_uv"]), name="kv_proj_dx" + tag)
    q_norm_g = small["q_norm_g"] + send_grads("mix", gw)
    dcq_raw, g_qn = _rms_bwd(dcq, sv["z"], ZC_CQ, sv["rq"], q_norm_g, name="q_norm_bwd" + tag)
    dckv_raw, g_kvn = _rms_bwd(dckv, sv["z"], ZC_CKV, sv["rkv"], small["kv_norm_g"], name="kv_norm_bwd" + tag)
    dz = dict(cq=dcq_raw, kr=dkr, u=du, ga=dga, gb=dgb, ckv=dckv_raw)
    g_in = {n: _mm(piece, sv["h"], ta=True, name=f"in_proj_dw_{n}" + tag, out_dtypes=(BF16,))[0]
            for n, piece in dz.items()}
    sc1 = sc1 + send_grads("in", g_in)
    dh = _mm_sum(list(dz.values()), w["w_in"], [Z_OFFSETS[n] for n in dz], name="in_proj_dx" + tag)
    dx, dln1, dsc1, dsh1, *below = _norm_mod_bwd(dh, sv["x"], sv["r1"], small["ln1_g"], sc1, dx1, gate=gate_below,
                                                 name="norm1_bwd" + tag)
    dmod = jnp.concatenate([dsh1, dsc1, dg1, dsh2, dsc2, dg2], axis=0)
    gsmall = dict(ln1_g=dln1, ln2_g=dln2, q_norm_g=g_qn, kv_norm_g=g_kvn, w_pool=g_w_pool, pool_scale=g_pool_scale)
    return dx, dmod, gsmall, below


SMALL_LOSS = 6
SMALL_SINGLES = 16
SMALL_POOL = 24
SMALL_POOL_ROWS = len(POOL_WINDOWS) * POOL_GROUP * POOL_GROUP // D_MODEL
SMALL_ROWS = SMALL_POOL + DEPTH * SMALL_POOL_ROWS


def _pack_small(parts, *, name):
    def body(*refs):
        out_ref = refs[-1]
        out_ref[...] = jnp.zeros_like(out_ref)
        for ref, (_, row) in zip(refs[:-1], parts):
            out_ref[row:row + ref.shape[0], :] = ref[...]

    return pl.pallas_call(body, name=name, out_shape=jax.ShapeDtypeStruct((SMALL_ROWS, D_MODEL), F32),
                          compiler_params=_params())(*[a for a, _ in parts])


def kernel(x, c, positions, ln1_g, ln2_g, w_ada, b_ada, w_in, q_norm_g, w_uq, kv_norm_g, w_uk, w_uv, w_pool, pool_scale, p_pool, p_attn, w_out, w_ff1, w_ff2, final_g, loss_target, m_ln1_g, m_ln2_g, m_w_ada, m_b_ada, m_w_in, m_q_norm_g, m_w_uq, m_kv_norm_g, m_w_uk, m_w_uv, m_w_pool, m_pool_scale, m_p_pool, m_p_attn, m_w_out, m_w_ff1, m_w_ff2, m_final_g, v_ln1_g, v_ln2_g, v_w_ada, v_b_ada, v_w_in, v_q_norm_g, v_w_uq, v_kv_norm_g, v_w_uk, v_w_uv, v_w_pool, v_pool_scale, v_p_pool, v_p_attn, v_w_out, v_w_ff1, v_w_ff2, v_final_g):
    weights = dict(ln1_g=ln1_g, ln2_g=ln2_g, w_ada=w_ada, b_ada=b_ada, w_in=w_in, q_norm_g=q_norm_g, w_uq=w_uq,
                   kv_norm_g=kv_norm_g, w_uk=w_uk, w_uv=w_uv, w_pool=w_pool, pool_scale=pool_scale, p_pool=p_pool,
                   p_attn=p_attn, w_out=w_out, w_ff1=w_ff1, w_ff2=w_ff2, final_g=final_g)
    moms = dict(ln1_g=m_ln1_g, ln2_g=m_ln2_g, w_ada=m_w_ada, b_ada=m_b_ada, w_in=m_w_in, q_norm_g=m_q_norm_g,
                w_uq=m_w_uq, kv_norm_g=m_kv_norm_g, w_uk=m_w_uk, w_uv=m_w_uv, w_pool=m_w_pool,
                pool_scale=m_pool_scale, p_pool=m_p_pool, p_attn=m_p_attn, w_out=m_w_out, w_ff1=m_w_ff1,
                w_ff2=m_w_ff2, final_g=m_final_g)
    vels = dict(ln1_g=v_ln1_g, ln2_g=v_ln2_g, w_ada=v_w_ada, b_ada=v_b_ada, w_in=v_w_in, q_norm_g=v_q_norm_g,
                w_uq=v_w_uq, kv_norm_g=v_kv_norm_g, w_uk=v_w_uk, w_uv=v_w_uv, w_pool=v_w_pool,
                pool_scale=v_pool_scale, p_pool=v_p_pool, p_attn=v_p_attn, w_out=v_w_out, w_ff1=v_w_ff1,
                w_ff2=v_w_ff2, final_g=v_final_g)
    order = list(weights)
    for table in (weights, moms, vels):
        table["w_in"] = jnp.swapaxes(table["w_in"], 1, 2)
    seq = x.shape[1]
    my_chip = 2 * lax.axis_index("x") + lax.axis_index("y")
    my_dev = 2 * my_chip + lax.axis_index("c")
    ada_cols = w_ada.shape[2]

    small = [dict(ln1_g=ln1_g[l:l + 1], ln2_g=ln2_g[l:l + 1], q_norm_g=q_norm_g[l:l + 1], kv_norm_g=kv_norm_g[l:l + 1],
                  w_pool=w_pool[l], pool_scale=pool_scale[l:l + 1]) for l in range(DEPTH)]

    b_mine = lax.dynamic_slice_in_dim(b_ada, my_chip * ada_cols, ada_cols, axis=1).reshape(1, DEPTH * ada_cols)
    c_act, mod_all = _ada_modulation(jnp.pad(c, ((0, 7), (0, 0))), w_ada, b_mine, name="ada_modulation")
    mod_all = mod_all.reshape(N_DEV, DEPTH, N_DEV, ada_cols)

    zero = mod_all[0, 0, 0, 0] * 0.0
    keys = [(l, group) for l in range(DEPTH) for group in GROUPS]
    exchanges, token = _exchange_start([_local_shard(weights, l, group, zero) for l, group in keys],
                                       name="weights_send", scatter=False)
    started = dict(zip(keys, exchanges))
    pin = token[0:1, 0:1]

    def gathered_weights(l, group, after):
        mine, land = _exchange_wait(started[l, group], after, name=f"weights_wait_l{l}_{group}", scatter=False)
        land = lax.dynamic_update_slice_in_dim(land, mine[None], my_chip, axis=0)
        return _unpack_weights(land, group)

    mods = []
    for l in range(DEPTH):
        row = jnp.concatenate([lax.dynamic_index_in_dim(mod_all[2 * j, l], my_dev, axis=0, keepdims=True)
                               for j in range(N_CHIPS)], axis=1) + pin
        mods.append([row[:, i * D_MODEL:(i + 1) * D_MODEL] for i in range(N_MOD)])

    inv_freq = ROPE_THETA ** (-jnp.arange(0, QK_ROPE, 2, dtype=F32) / QK_ROPE)
    freq_lanes = jnp.concatenate([jnp.zeros((QK_NOPE,), F32), inv_freq, inv_freq,
                                  jnp.zeros((HEAD_PAD - QK_DIM,), F32)]).reshape(1, LANES)
    cos_t, sin_t = _rope_tables(positions.reshape(seq, 1), freq_lanes, name="rope_tables")

    xs, saved, wl = x.reshape(seq, D_MODEL), [], []
    for l in range(DEPTH):
        xs, sv, w_l = _layer_fwd(l, xs, mods[l], functools.partial(gathered_weights, l), small[l], cos_t, sin_t)
        saved.append(sv)
        wl.append(w_l)
    dx, loss_part, g_final, dm2, dg2 = _final_loss(xs, final_g.reshape(1, D_MODEL), loss_target.reshape(seq, D_MODEL),
                                                   saved[-1]["m2"], mods[-1][5], name="final_loss")

    sent, pending = [], []
    send_after = {(0, "ff1"), (0, "mix"), (0, "in")}

    def send_grads(l, group, g):
        pending.append((l, group, _pack_grads(g, group)))
        if (l, group) not in send_after:
            return jnp.zeros((1, 1), F32)
        exchanges, token_g = _exchange_start([gpack for _, _, gpack in pending], name=f"grads_send_l{l}_{group}",
                                             scatter=True)
        sent.extend((item[0], item[1], exchange) for item, exchange in zip(pending, exchanges))
        pending.clear()
        return token_g[0:1, 0:1]

    dmod, gsmall = [None] * DEPTH, [None] * DEPTH
    for l in reversed(range(DEPTH)):
        gate_below = (saved[l - 1]["m2"], mods[l - 1][5]) if l > 0 else None
        dx, dmod[l], gsmall[l], below = _layer_bwd(l, dx, dm2, dg2, saved[l], mods[l], wl[l], small[l], cos_t, sin_t,
                                                   functools.partial(send_grads, l), gate_below)
        dm2, dg2 = below if below else (None, None)
    grads = dict(x=dx.reshape(1, seq, D_MODEL))

    big_parts, after = [], dmod[0]
    for l, group, started_g in sent:
        tg = f"_l{l}_{group}"
        gpack, land = _exchange_wait(started_g, after, name="grads_wait" + tg, scatter=True)
        own = lax.dynamic_index_in_dim(gpack, my_chip, axis=0, keepdims=True)
        land = lax.dynamic_update_slice_in_dim(land, own, my_chip, axis=0)
        big_parts.append(_sum_slots(land, N_CHIPS, name="grads_sum_chips" + tg))
        after = big_parts[-1]

    def lanes(a):
        flat = a.reshape(1, -1)
        return jnp.pad(flat, ((0, 0), (0, D_MODEL - flat.shape[1])))

    singles = [gsmall[0]["ln1_g"], gsmall[1]["ln1_g"], gsmall[0]["ln2_g"], gsmall[1]["ln2_g"], g_final,
               lanes(jnp.concatenate([gsmall[l]["pool_scale"] for l in range(DEPTH)], axis=1)),
               lanes(jnp.concatenate([gsmall[l]["q_norm_g"] for l in range(DEPTH)], axis=1)),
               lanes(jnp.concatenate([gsmall[l]["kv_norm_g"] for l in range(DEPTH)], axis=1))]
    parts = [(dmod[0], 0), (lanes(loss_part), SMALL_LOSS), (dmod[1], 8)]
    parts += [(a, SMALL_SINGLES + i) for i, a in enumerate(singles)]
    parts += [(gsmall[l]["w_pool"].reshape(-1, D_MODEL), SMALL_POOL + l * SMALL_POOL_ROWS) for l in range(DEPTH)]
    small_all, big_others = _all_gather_small(_pack_small(parts, name="small_grads_pack"), swaps=big_parts,
                                              name="small_grads_all_gather")
    small_all = small_all.reshape(N_DEV, SMALL_ROWS, D_MODEL)
    ssum = _sum_slots(small_all, N_DEV, name="small_grads_sum")
    loss = ssum[SMALL_LOSS, 0]
    grads["b_ada"] = jnp.stack([ssum[8 * l:8 * l + N_MOD] for l in range(DEPTH)]).reshape(DEPTH, N_MOD * D_MODEL)
    grads["ln1_g"] = ssum[SMALL_SINGLES:SMALL_SINGLES + 2]
    grads["ln2_g"] = ssum[SMALL_SINGLES + 2:SMALL_SINGLES + 4]
    grads["final_g"] = ssum[SMALL_SINGLES + 4]
    grads["pool_scale"] = ssum[SMALL_SINGLES + 5].reshape(DEPTH, POOL_DIM)
    grads["q_norm_g"] = ssum[SMALL_SINGLES + 6, :DEPTH * Q_LORA].reshape(DEPTH, Q_LORA)
    grads["kv_norm_g"] = ssum[SMALL_SINGLES + 7, :DEPTH * KV_LORA].reshape(DEPTH, KV_LORA)
    grads["w_pool"] = ssum[SMALL_POOL:SMALL_ROWS].reshape(w_pool.shape)

    mix_sum, halves = None, {group: [None] * DEPTH for group in ("in", "ff1", "ff2")}
    for (l, group, _), part, other in zip(sent, big_parts, big_others):
        if group == "mix":
            mix_sum = _add2_stacked(part, other, mix_sum, l, name=f"grads_sum_cores_l{l}_mix")
        else:
            halves[group][l] = (part, other)
    off = 0
    for name in MIX_NAMES:
        grads[name] = mix_sum[:, off:off + ROWS_OF[name]].reshape(weights[name].shape)
        off += ROWS_OF[name]

    c_act_t = jnp.pad(c_act.T, ((0, 0), (0, LANES - N_DEV)))
    d_mine = []
    for l in range(DEPTH):
        d_all = small_all[:, 8 * l:8 * l + N_MOD].reshape(N_DEV, N_MOD * D_MODEL)
        d_mine.append(lax.dynamic_slice_in_dim(d_all, my_chip * ada_cols, ada_cols, axis=1))
    d_cat = jnp.pad(jnp.concatenate(d_mine, axis=1), ((0, LANES - N_DEV), (0, 0)))
    (grads["w_ada"],) = _mm(c_act_t, d_cat, out_stack=DEPTH, name="ada_dw")

    def view(a):
        return a.reshape(1, -1) if a.ndim == 1 else a if a.ndim == 3 else a.reshape(-1, a.shape[-1])

    delta, new_m, new_v = {}, {}, {}
    for name in order:
        shape = weights[name].shape
        group = {"w_in": "in", "w_ff1": "ff1", "w_ff2": "ff2"}.get(name)
        if group:
            grads[name], d, nm, nv = _adamw_summed(weights[name], halves[group], moms[name], vels[name],
                                                   name="adamw_" + name)
        else:
            d, nm, nv = _adamw(view(weights[name]), view(grads[name]), view(moms[name]), view(vels[name]),
                               name="adamw_" + name)
        delta[name], new_m[name], new_v[name] = d.reshape(shape), nm.reshape(shape), nv.reshape(shape)
    for table in (grads, delta, new_m, new_v):
        table["w_in"] = jnp.swapaxes(table["w_in"], 1, 2)
    return (loss, grads["x"], *[grads[n] for n in order], *[delta[n] for n in order],
            *[new_m[n] for n in order], *[new_v[n] for n in order])
```

```python
import functools
import math

import jax
import jax.numpy as jnp
from jax import lax
from jax.experimental import pallas as pl
from jax.experimental.pallas import tpu as pltpu

F32 = jnp.float32
BF16 = jnp.bfloat16
MESH = pl.DeviceIdType.MESH

D_MODEL = 1024
DEPTH = 2
POOL_WINDOWS = (2, 4, 8, 16)
POOL_GROUP = 128
POOL_DIM = 512
N_HEADS = 8
QK_NOPE = 64
QK_ROPE = 32
QK_DIM = QK_NOPE + QK_ROPE
V_DIM = 64
HEAD_PAD = 128
Q_LORA = 384
KV_LORA = 256
ROPE_THETA = 10000.0
ATTN_DIM = N_HEADS * V_DIM
D_FF = 4 * D_MODEL
N_MOD = 6
EPS = 1e-6
N_CHIPS = 4
N_DEV = 8

ADAM_LR = 0.001
ADAM_B1 = 0.9
ADAM_B2 = 0.999
ADAM_EPS = 1e-08
ADAM_WD = 0.01
ADAM_STEP = 10

VMEM_LIMIT_BYTES = 56 * 1024 * 1024
LANES = 128
HALO = 16

ZC_CQ = 0
ZC_KR = 384
ZC_U = 512
ZC_GA = 1024
ZC_GB = 2048
ZC_CKV = 3072
Z_DIM = 3328
Z_OFFSETS = dict(cq=ZC_CQ, kr=ZC_KR, u=ZC_U, ga=ZC_GA, gb=ZC_GB, ckv=ZC_CKV)

W_IN_U, W_IN_CQ, W_IN_CKV, W_IN_KR, W_IN_GATES = (0, 512), (512, 896), (896, 1152), (1152, 1184), (1184, 3232)
W_IN_COLS = W_IN_GATES[1]
W_IN_SHARD = W_IN_COLS // N_CHIPS

ROWS_OF = dict(w_uq=72, w_uk=32, w_uv=32, p_pool=128, p_attn=128, w_out=256)


def _params(sem=None, **kw):
    return pltpu.CompilerParams(dimension_semantics=sem, vmem_limit_bytes=VMEM_LIMIT_BYTES, **kw)


def _tile(n, target, unit=LANES):
    best = None
    for t in range(unit, min(n, target) + 1, unit):
        if n % t == 0:
            best = t
    return best if best is not None and 4 * best >= min(n, target) else n


def _near_tile(n, target):
    cands = [t for t in range(LANES, n + 1, LANES) if n % t == 0]
    return min(cands, key=lambda t: abs(math.log(t / target))) if cands else n


def _mm(a, b, *, name, ta=False, tb=False, out_dtypes=(F32,), epilogue=None, extras=(), tm=1024, tn=1024, tk=1024,
        second=None, b_stack=False, out_stack=None):
    (k_dim, m_dim) = a.shape if ta else a.shape[::-1]
    if b_stack:
        g_b, k_b, n_shard = b.shape
        n_dim, k_b = (k_b, g_b * n_shard) if tb else (g_b * n_shard, k_b)
    else:
        (n_dim, k_b) = b.shape if tb else b.shape[::-1]
    assert k_dim == k_b, (a.shape, b.shape)
    n_unit = n_shard if b_stack and not tb else n_dim // out_stack if out_stack else n_dim
    k_unit = n_shard if b_stack and tb else k_dim
    tm, tn, tk = _near_tile(m_dim, tm), _near_tile(n_unit, tn), _near_tile(k_unit, tk)
    nk = k_dim // tk
    n_extra, n_out = len(extras), len(out_dtypes)
    n_lhs = 4 if second else 2
    dims = (((0 if ta else 1,), (1 if tb else 0,)), ((), ()))
    if epilogue is None:
        epilogue = lambda acc: (acc,) * n_out

    def body(*refs):
        operand_refs, rest = refs[:n_lhs], refs[n_lhs:]
        extra_refs, out_refs = rest[:n_extra], rest[n_extra:n_extra + n_out]

        def product():
            total = None
            for a_ref, b_ref in zip(operand_refs[0::2], operand_refs[1::2]):
                part = lax.dot_general(a_ref[...].astype(BF16), b_ref[...].astype(BF16), dims, preferred_element_type=F32)
                total = part if total is None else total + part
            return total

        def finish(acc):
            outs = epilogue(acc, *[r[...] for r in extra_refs])
            for o_ref, o in zip(out_refs, outs):
                o_ref[...] = o.astype(o_ref.dtype)

        if nk == 1:
            finish(product())
            return
        acc_ref = rest[-1]
        k = pl.program_id(2)

        @pl.when(k == 0)
        def _():
            acc_ref[...] = product()

        @pl.when((k > 0) & (k < nk - 1))
        def _():
            acc_ref[...] += product()

        @pl.when(k == nk - 1)
        def _():
            finish(acc_ref[...] + product())

    a_spec = pl.BlockSpec((tk, tm), lambda i, j, k: (k, i)) if ta else pl.BlockSpec((tm, tk), lambda i, j, k: (i, k))
    if b_stack and tb:
        per = n_shard // tk
        b_spec = pl.BlockSpec((None, tn, tk), lambda i, j, k: (k // per, j, k % per))
    elif b_stack:
        per = n_shard // tn
        b_spec = pl.BlockSpec((None, tk, tn), lambda i, j, k: (j // per, k, j % per))
    elif tb:
        b_spec = pl.BlockSpec((tn, tk), lambda i, j, k: (j, k))
    else:
        b_spec = pl.BlockSpec((tk, tn), lambda i, j, k: (k, j))
    if out_stack:
        per_out = (n_dim // out_stack) // tn
        out_spec = pl.BlockSpec((None, tm, tn), lambda i, j, k: (j // per_out, i, j % per_out))
        out_dims = (out_stack, m_dim, n_dim // out_stack)
    else:
        out_spec = pl.BlockSpec((tm, tn), lambda i, j, k: (i, j))
        out_dims = (m_dim, n_dim)
    extra_specs = []
    for arr, kind in extras:
        if kind == "tile":
            extra_specs.append(pl.BlockSpec((tm, tn), lambda i, j, k: (i, j)))
        elif isinstance(kind, tuple):
            extra_specs.append(pl.BlockSpec((tm, tn), functools.partial(lambda i, j, k, c: (i, j + c), c=kind[1])))
        elif kind == "row":
            extra_specs.append(pl.BlockSpec((1, tn), lambda i, j, k: (0, j)))
        elif kind == "col":
            extra_specs.append(pl.BlockSpec((tm, 1), lambda i, j, k: (i, 0)))
        else:
            assert kind == "table", kind
            extra_specs.append(pl.BlockSpec((tm, LANES), lambda i, j, k: (i, 0)))
    return pl.pallas_call(
        body,
        name=name,
        grid=(m_dim // tm, n_dim // tn, nk),
        in_specs=[a_spec, b_spec] * (n_lhs // 2) + extra_specs,
        out_specs=[out_spec for _ in out_dtypes],
        out_shape=[jax.ShapeDtypeStruct(out_dims, dt) for dt in out_dtypes],
        scratch_shapes=[pltpu.VMEM((tm, tn), F32)] if nk > 1 else [],
        compiler_params=_params(("parallel", "parallel", "arbitrary")),
    )(a, b, *(second or ()), *[arr for arr, _ in extras])


def _mm_sum(pieces, b, offsets, *, name, tm=1024, tn=1024):
    m_dim, n_dim = pieces[0].shape[0], b.shape[1]
    tm, tn = _near_tile(m_dim, tm), _near_tile(n_dim, tn)
    n_pieces = len(pieces)

    def body(*refs):
        total = None
        for a_ref, b_ref in zip(refs[:n_pieces], refs[n_pieces:2 * n_pieces]):
            part = jnp.dot(a_ref[...], b_ref[...], preferred_element_type=F32)
            total = part if total is None else total + part
        refs[-1][...] = total

    a_specs = [pl.BlockSpec((tm, p.shape[1]), lambda i, j: (i, 0)) for p in pieces]
    b_specs = [pl.BlockSpec((p.shape[1], tn), functools.partial(lambda i, j, blk: (blk, j), blk=off // p.shape[1]))
               for p, off in zip(pieces, offsets)]
    return pl.pallas_call(
        body, name=name, grid=(m_dim // tm, n_dim // tn),
        in_specs=a_specs + b_specs,
        out_specs=pl.BlockSpec((tm, tn), lambda i, j: (i, j)),
        out_shape=jax.ShapeDtypeStruct((m_dim, n_dim), F32),
        compiler_params=_params(("parallel", "parallel")),
    )(*pieces, *[b] * n_pieces)


def _rows(s):
    return min(512, s)


def _rope_tables(pos_col, inv_freq_lanes, *, name):
    s = pos_col.shape[0]
    tb = _rows(s)

    def body(pos_ref, f_ref, cos_ref, sin_ref):
        ang = pos_ref[...].astype(F32) * f_ref[...]
        lane = lax.broadcasted_iota(jnp.int32, ang.shape, 1)
        on = (lane >= QK_NOPE) & (lane < QK_DIM)
        cos_ref[...] = jnp.where(on, jnp.cos(ang), 0.0)
        sin_ref[...] = jnp.where(on, jnp.sin(ang), 0.0)

    return pl.pallas_call(
        body, name=name, grid=(s // tb,),
        in_specs=[pl.BlockSpec((tb, 1), lambda i: (i, 0)), pl.BlockSpec((1, LANES), lambda i: (0, 0))],
        out_specs=[pl.BlockSpec((tb, LANES), lambda i: (i, 0))] * 2,
        out_shape=[jax.ShapeDtypeStruct((s, LANES), F32)] * 2,
        compiler_params=_params(("parallel",)),
    )(pos_col, inv_freq_lanes)


def _rotate_half(x):
    lane = lax.broadcasted_iota(jnp.int32, x.shape, 1)
    half = QK_ROPE // 2
    first = (lane >= QK_NOPE) & (lane < QK_NOPE + half)
    second = (lane >= QK_NOPE + half) & (lane < QK_DIM)
    return jnp.where(first, -pltpu.roll(x, LANES - half, 1), jnp.where(second, pltpu.roll(x, half, 1), 0.0))


def _norm_mod(x, g, sc, sh, *, name):
    s, d = x.shape
    tb = _rows(s)

    def body(x_ref, g_ref, sc_ref, sh_ref, h_ref, r_ref):
        xv = x_ref[...]
        r = lax.rsqrt(jnp.mean(xv * xv, axis=-1, keepdims=True) + EPS)
        r_ref[...] = r
        h_ref[...] = (((xv * r) * g_ref[...]) * (1.0 + sc_ref[...]) + sh_ref[...]).astype(BF16)

    vec = pl.BlockSpec((1, d), lambda i: (0, 0))
    return pl.pallas_call(
        body, name=name, grid=(s // tb,),
        in_specs=[pl.BlockSpec((tb, d), lambda i: (i, 0)), vec, vec, vec],
        out_specs=[pl.BlockSpec((tb, d), lambda i: (i, 0)), pl.BlockSpec((tb, 1), lambda i: (i, 0))],
        out_shape=[jax.ShapeDtypeStruct((s, d), BF16), jax.ShapeDtypeStruct((s, 1), F32)],
        compiler_params=_params(("parallel",)),
    )(x, g, sc, sh)


def _window_sums(ext, sign):
    n = ext.shape[0]
    sums, cur, k = [], ext, 1
    for _ in POOL_WINDOWS:
        cur = cur + pltpu.roll(cur, k if sign > 0 else n - k, 0)
        sums.append(cur)
        k *= 2
    return sums


def _mixer_pre(z, cos_t, sin_t, w_pool, pool_scale, gq, gkv, *, name):
    s = z.shape[0]
    tb = _rows(s)
    hb = tb // HALO

    def body(zcq_ref, zkr_ref, zu_ref, zuh_ref, zckv_ref, cos_ref, sin_ref, wp_ref, ps_ref, gq_ref, gkv_ref,
             p_ref, yp_ref, cq_ref, ckv_ref, kr_ref, rq_ref, rkv_ref):
        i = pl.program_id(0)
        u = zu_ref[...].astype(F32)
        halo = jnp.where(i > 0, zuh_ref[...].astype(F32), 0.0)
        ext = jnp.concatenate([halo, u], axis=0)
        t = i * tb + lax.broadcasted_iota(jnp.int32, (tb, 1), 0)
        for g, (w, sw) in enumerate(zip(POOL_WINDOWS, _window_sums(ext, +1))):
            cols = slice(g * POOL_GROUP, (g + 1) * POOL_GROUP)
            cnt = jnp.minimum(t + 1, w).astype(F32)
            pg = (sw[HALO:, cols] / cnt - u[:, cols]).astype(BF16)
            p_ref[:, cols] = pg
            yg = jnp.dot(pg, wp_ref[g].astype(BF16), preferred_element_type=F32)
            yp_ref[:, cols] = (yg * ps_ref[:, cols]).astype(BF16)

        def rms(x_ref, g_ref, out_ref, r_ref):
            xv = x_ref[...].astype(F32)
            r = lax.rsqrt(jnp.mean(xv * xv, axis=-1, keepdims=True) + EPS)
            r_ref[...] = r
            out_ref[...] = ((xv * r) * g_ref[...]).astype(BF16)

        rms(zcq_ref, gq_ref, cq_ref, rq_ref)
        rms(zckv_ref, gkv_ref, ckv_ref, rkv_ref)
        kr = zkr_ref[...].astype(F32)
        kr_ref[...] = (kr * cos_ref[...] + _rotate_half(kr) * sin_ref[...]).astype(BF16)

    def zcol(width, off):
        return pl.BlockSpec((tb, width), lambda i: (i, off // width))

    def full(a):
        return pl.BlockSpec(a.shape, lambda i: (0,) * a.ndim)

    def out(width, dt):
        return pl.BlockSpec((tb, width), lambda i: (i, 0)), jax.ShapeDtypeStruct((s, width), dt)

    outs = [out(POOL_DIM, BF16), out(POOL_DIM, BF16), out(Q_LORA, BF16), out(KV_LORA, BF16), out(LANES, BF16),
            out(1, F32), out(1, F32)]
    return pl.pallas_call(
        body, name=name, grid=(s // tb,),
        in_specs=[zcol(Q_LORA, ZC_CQ), zcol(LANES, ZC_KR), zcol(POOL_DIM, ZC_U),
                  pl.BlockSpec((HALO, POOL_DIM), lambda i: (jnp.maximum(i * hb - 1, 0), ZC_U // POOL_DIM)),
                  zcol(KV_LORA, ZC_CKV),
                  pl.BlockSpec((tb, LANES), lambda i: (i, 0)), pl.BlockSpec((tb, LANES), lambda i: (i, 0)),
                  full(w_pool), full(pool_scale), full(gq), full(gkv)],
        out_specs=[o[0] for o in outs], out_shape=[o[1] for o in outs],
        compiler_params=_params(("parallel",)),
    )(z, z, z, z, z, cos_t, sin_t, w_pool, pool_scale, gq, gkv)


def _sigmoid(x):
    return 1.0 / (1.0 + jnp.exp(-x.astype(F32)))


ATTN_SCALE = 1.0 / math.sqrt(QK_DIM)
NEG_BIG = -1e30


LOG2_E = math.log2(math.e)
EXP2_SCALE = ATTN_SCALE * LOG2_E
NT_DIMS = (((1,), (1,)), ((), ()))
TN_DIMS = (((0,), (0,)), ((), ()))


def _on_or_below_diagonal(t):
    return lax.broadcasted_iota(jnp.int32, (t, t), 0) >= lax.broadcasted_iota(jnp.int32, (t, t), 1)


HEADS_PER_STEP = 2
HEAD_COLS = [slice(g * HEAD_PAD, (g + 1) * HEAD_PAD) for g in range(HEADS_PER_STEP)]


def _attn_fwd(q, k, v, *, name):
    s = q.shape[0]
    t = _rows(s)
    wide = HEADS_PER_STEP * HEAD_PAD

    def body(q_ref, k_ref, v_ref, o_ref, lse_ref):
        qi = pl.program_id(1)
        qs = [q_ref[:, cols] for cols in HEAD_COLS]

        def block(j, carry, diagonal):
            rows = pl.ds(pl.multiple_of(j * t, t), t)
            out = []
            for qv, cols, (m, l, acc) in zip(qs, HEAD_COLS, carry):
                sc = lax.dot_general(qv, k_ref[rows, cols], NT_DIMS, preferred_element_type=F32)
                if diagonal:
                    sc = jnp.where(_on_or_below_diagonal(t), sc, NEG_BIG)
                m_new = jnp.maximum(m, jnp.max(sc, axis=-1, keepdims=True))
                p = jnp.exp2((sc - m_new) * EXP2_SCALE)
                alpha = jnp.exp2((m - m_new) * EXP2_SCALE)
                l = alpha * l + jnp.sum(p, axis=-1, keepdims=True)
                acc = alpha * acc + jnp.dot(p.astype(BF16), v_ref[rows, cols], preferred_element_type=F32)
                out.append((m_new, l, acc))
            return tuple(out)

        init = tuple((jnp.full((t, 1), -jnp.inf, F32), jnp.zeros((t, 1), F32), jnp.zeros((t, HEAD_PAD), F32))
                     for _ in HEAD_COLS)
        carry = lax.fori_loop(0, qi, lambda j, c: block(j, c, False), init)
        for g, (cols, (m, l, acc)) in enumerate(zip(HEAD_COLS, block(qi, carry, True))):
            o_ref[:, cols] = (acc / l).astype(BF16)
            lse_ref[g] = m * ATTN_SCALE + jnp.log(l)

    q_spec = pl.BlockSpec((t, wide), lambda h, i: (i, h))
    kv_spec = pl.BlockSpec((s, wide), lambda h, i: (0, h))
    return pl.pallas_call(
        body, name=name, grid=(N_HEADS // HEADS_PER_STEP, s // t),
        in_specs=[q_spec, kv_spec, kv_spec],
        out_specs=[q_spec, pl.BlockSpec((HEADS_PER_STEP, t, 1), lambda h, i: (h, i, 0))],
        out_shape=[jax.ShapeDtypeStruct((s, N_HEADS * HEAD_PAD), BF16), jax.ShapeDtypeStruct((N_HEADS, s, 1), F32)],
        compiler_params=_params(("parallel", "parallel")),
    )(q, k, v)


def _attn_bwd(q, k, v, do, o, lse, cos_t, sin_t, *, name):
    s = q.shape[0]
    t = _rows(s)
    nt = s // t

    def body(q_ref, k_ref, v_ref, do_ref, o_ref, lse_ref, cos_ref, sin_ref, dql_ref, dk_ref, dv_ref, dks_ref,
             dq_ref, dl_ref):
        kj = pl.program_id(1)

        @pl.when(kj == 0)
        def _():
            dq_ref[...] = jnp.zeros_like(dq_ref)

            def delta(i, carry):
                rows = pl.ds(pl.multiple_of(i * t, t), t)
                for g, cols in enumerate(HEAD_COLS):
                    dl_ref[g, rows, :] = jnp.sum(do_ref[rows, cols].astype(F32) * o_ref[rows, cols].astype(F32),
                                                 axis=-1, keepdims=True)
                return carry

            lax.fori_loop(0, nt, delta, 0)

        kvs = [(k_ref[:, cols], v_ref[:, cols]) for cols in HEAD_COLS]

        def block(i, carry, diagonal):
            rows = pl.ds(pl.multiple_of(i * t, t), t)
            out = []
            for g, (cols, (kv, vv), (dk, dv)) in enumerate(zip(HEAD_COLS, kvs, carry)):
                qv, dov = q_ref[rows, cols], do_ref[rows, cols]
                sc = lax.dot_general(qv, kv, NT_DIMS, preferred_element_type=F32)
                p = jnp.exp2(sc * EXP2_SCALE - lse_ref[g, rows, :] * LOG2_E)
                if diagonal:
                    p = jnp.where(_on_or_below_diagonal(t), p, 0.0)
                dp = lax.dot_general(dov, vv, NT_DIMS, preferred_element_type=F32)
                ds = (p * (dp - dl_ref[g, rows, :])).astype(BF16)
                dv = dv + lax.dot_general(p.astype(BF16), dov, TN_DIMS, preferred_element_type=F32)
                dk = dk + lax.dot_general(ds, qv, TN_DIMS, preferred_element_type=F32)
                dq_ref[rows, cols] += jnp.dot(ds, kv, preferred_element_type=F32) * ATTN_SCALE
                out.append((dk, dv))
            return tuple(out)

        zero = jnp.zeros((t, HEAD_PAD), F32)
        carry = block(kj, tuple((zero, zero) for _ in HEAD_COLS), True)
        dk_sum = None
        for cols, (dk, dv) in zip(HEAD_COLS, lax.fori_loop(kj + 1, nt, lambda i, c: block(i, c, False), carry)):
            dk = dk * ATTN_SCALE
            dk_ref[:, cols] = dk.astype(BF16)
            dv_ref[:, cols] = dv.astype(BF16)
            dk_sum = dk if dk_sum is None else dk_sum + dk
        dks_ref[...] = dk_sum

        @pl.when(kj == nt - 1)
        def _():
            def rope_bwd(i, carry):
                rows = pl.ds(pl.multiple_of(i * t, t), t)
                sin = sin_ref[rows, :]
                lane = lax.broadcasted_iota(jnp.int32, sin.shape, 1)
                cos_q = cos_ref[rows, :] + jnp.where(lane < QK_NOPE, 1.0, 0.0)
                for cols in HEAD_COLS:
                    dqv = dq_ref[rows, cols]
                    dql_ref[rows, cols] = (dqv * cos_q - _rotate_half(dqv * sin)).astype(BF16)
                return carry

            lax.fori_loop(0, nt, rope_bwd, 0)

    heads_wide = HEADS_PER_STEP * HEAD_PAD
    full_spec = pl.BlockSpec((s, heads_wide), lambda h, j: (0, h))
    kv_spec = pl.BlockSpec((t, heads_wide), lambda h, j: (j, h))
    vec_spec = pl.BlockSpec((HEADS_PER_STEP, s, 1), lambda h, j: (h, 0, 0))
    table_spec = pl.BlockSpec((s, LANES), lambda h, j: (0, 0))
    wide = jax.ShapeDtypeStruct((s, N_HEADS * HEAD_PAD), BF16)
    n_steps = N_HEADS // HEADS_PER_STEP
    return pl.pallas_call(
        body, name=name, grid=(n_steps, nt),
        in_specs=[full_spec, kv_spec, kv_spec, full_spec, full_spec, vec_spec, table_spec, table_spec],
        out_specs=[full_spec, kv_spec, kv_spec, pl.BlockSpec((None, t, HEAD_PAD), lambda h, j: (h, j, 0))],
        out_shape=[wide, wide, wide, jax.ShapeDtypeStruct((n_steps, s, HEAD_PAD), F32)],
        scratch_shapes=[pltpu.VMEM((s, heads_wide), F32), pltpu.VMEM((HEADS_PER_STEP, s, 1), F32)],
        compiler_params=_params(("parallel", "arbitrary")),
    )(q, k, v, do, o, lse, cos_t, sin_t)


def _acc_specs(widths):
    return ([pl.BlockSpec((1, w), lambda i: (0, 0)) for w in widths],
            [jax.ShapeDtypeStruct((1, w), F32) for w in widths])


def _gate_grads(dxv, m_ref, gate_ref, dm_ref, dgate_ref):
    dm_ref[...] = (dxv * gate_ref[...]).astype(BF16)
    dgate_ref[...] += jnp.sum(dxv * m_ref[...], axis=0, keepdims=True)


def _final_loss(x, g, target, m, gate, *, name):
    s, d = x.shape
    tb = _rows(s)

    def body(x_ref, g_ref, t_ref, m_ref, gate_ref, dx_ref, loss_ref, dg_ref, dm_ref, dgate_ref):
        @pl.when(pl.program_id(0) == 0)
        def _():
            loss_ref[...] = jnp.zeros_like(loss_ref)
            dg_ref[...] = jnp.zeros_like(dg_ref)
            dgate_ref[...] = jnp.zeros_like(dgate_ref)

        xv = x_ref[...]
        r = lax.rsqrt(jnp.mean(xv * xv, axis=-1, keepdims=True) + EPS)
        xn = xv * r
        err = xn * g_ref[...] - t_ref[...]
        loss_ref[...] += 0.5 * jnp.sum(jnp.mean(err * err, axis=-1, keepdims=True), axis=0, keepdims=True)
        dy = err / d
        dg_ref[...] += jnp.sum(dy * xn, axis=0, keepdims=True)
        dxn = dy * g_ref[...]
        dxv = r * (dxn - xn * jnp.mean(dxn * xn, axis=-1, keepdims=True))
        dx_ref[...] = dxv
        _gate_grads(dxv, m_ref, gate_ref, dm_ref, dgate_ref)

    blk = pl.BlockSpec((tb, d), lambda i: (i, 0))
    vec = pl.BlockSpec((1, d), lambda i: (0, 0))
    acc_specs, acc_shapes = _acc_specs((LANES, d))
    return pl.pallas_call(
        body, name=name, grid=(s // tb,),
        in_specs=[blk, vec, blk, blk, vec],
        out_specs=[blk] + acc_specs + [blk, vec],
        out_shape=[jax.ShapeDtypeStruct((s, d), F32)] + acc_shapes + [jax.ShapeDtypeStruct((s, d), BF16),
                                                                     jax.ShapeDtypeStruct((1, d), F32)],
        compiler_params=_params(("arbitrary",)),
    )(x, g, target, m, gate)


def _norm_mod_bwd(dh, x, r, g, sc, dx_skip, *, name, gate=None):
    s, d = x.shape
    tb = _rows(s)
    nb = s // tb
    n_gate = 2 if gate else 0

    def body(dh_ref, x_ref, r_ref, g_ref, sc_ref, skip_ref, *rest):
        gate_refs, (dx_ref, dg_ref, dsc_ref, dsh_ref) = rest[:n_gate], rest[n_gate:n_gate + 4]
        gate_outs, da_sc = rest[n_gate + 4:-1], rest[-1]
        i = pl.program_id(0)

        @pl.when(i == 0)
        def _():
            da_sc[...] = jnp.zeros_like(da_sc)
            dsh_ref[...] = jnp.zeros_like(dsh_ref)
            if gate:
                gate_outs[1][...] = jnp.zeros_like(gate_outs[1])

        dhv, rv = dh_ref[...], r_ref[...]
        xn = x_ref[...] * rv
        dsh_ref[...] += jnp.sum(dhv, axis=0, keepdims=True)
        da_sc[...] += jnp.sum(dhv * xn, axis=0, keepdims=True)
        dxn = dhv * (g_ref[...] * (1.0 + sc_ref[...]))
        dxv = skip_ref[...] + rv * (dxn - xn * jnp.mean(dxn * xn, axis=-1, keepdims=True))
        dx_ref[...] = dxv
        if gate:
            _gate_grads(dxv, *gate_refs, *gate_outs)

        @pl.when(i == nb - 1)
        def _():
            dg_ref[...] = da_sc[...] * (1.0 + sc_ref[...])
            dsc_ref[...] = da_sc[...] * g_ref[...]

    blk = pl.BlockSpec((tb, d), lambda i: (i, 0))
    vec = pl.BlockSpec((1, d), lambda i: (0, 0))
    acc_specs, acc_shapes = _acc_specs((d, d, d))
    gate_specs = [blk, vec] if gate else []
    gate_shapes = [jax.ShapeDtypeStruct((s, d), BF16), jax.ShapeDtypeStruct((1, d), F32)] if gate else []
    return pl.pallas_call(
        body, name=name, grid=(nb,),
        in_specs=[blk, blk, pl.BlockSpec((tb, 1), lambda i: (i, 0)), vec, vec, blk] + gate_specs,
        out_specs=[blk] + acc_specs + gate_specs,
        out_shape=[jax.ShapeDtypeStruct((s, d), F32)] + acc_shapes + gate_shapes,
        scratch_shapes=[pltpu.VMEM((1, d), F32)],
        compiler_params=_params(("arbitrary",)),
    )(dh, x, r, g, sc, dx_skip, *(gate or ()))


def _pool_bwd(dyp, p, w_pool, pool_scale, *, name):
    s = dyp.shape[0]
    tb = _rows(s)
    nb = s // tb
    hb = tb // HALO
    nt_dims = (((1,), (1,)), ((), ()))
    tn_dims = (((0,), (0,)), ((), ()))

    def body(dy_ref, dyn_ref, p_ref, wp_ref, ps_ref, du_ref, gwp_ref, gps_ref):
        i = pl.program_id(0)

        @pl.when(i == 0)
        def _():
            gwp_ref[...] = jnp.zeros_like(gwp_ref)
            gps_ref[...] = jnp.zeros_like(gps_ref)

        cur = dy_ref[...]
        nxt = jnp.where(i < nb - 1, dyn_ref[...], 0.0)
        dpw = (jnp.concatenate([cur, nxt], axis=0) * ps_ref[...]).astype(BF16)
        t = i * tb + lax.broadcasted_iota(jnp.int32, (tb + HALO, 1), 0)
        for g, w in enumerate(POOL_WINDOWS):
            cols = slice(g * POOL_GROUP, (g + 1) * POOL_GROUP)
            wg = wp_ref[g].astype(BF16)
            dp = lax.dot_general(dpw[:, cols], wg, nt_dims, preferred_element_type=F32)
            e = dp / jnp.minimum(t + 1, w).astype(F32)
            lead = _window_sums(e, -1)[g]
            du_ref[:, cols] = (lead[:tb] - dp[:tb]).astype(BF16)
            pg = p_ref[:, cols]
            pw = jnp.dot(pg, wg, preferred_element_type=F32)
            gps_ref[:, cols] += jnp.sum(cur[:, cols] * pw, axis=0, keepdims=True)
            gwp_ref[g] += lax.dot_general(pg, dpw[:tb, cols], tn_dims, preferred_element_type=F32)

    blk = pl.BlockSpec((tb, POOL_DIM), lambda i: (i, 0))
    return pl.pallas_call(
        body, name=name, grid=(nb,),
        in_specs=[blk, pl.BlockSpec((HALO, POOL_DIM), lambda i: (jnp.minimum((i + 1) * hb, s // HALO - 1), 0)), blk,
                  pl.BlockSpec(w_pool.shape, lambda i: (0, 0, 0)), pl.BlockSpec((1, POOL_DIM), lambda i: (0, 0))],
        out_specs=[blk, pl.BlockSpec(w_pool.shape, lambda i: (0, 0, 0)), pl.BlockSpec((1, POOL_DIM), lambda i: (0, 0))],
        out_shape=[jax.ShapeDtypeStruct((s, POOL_DIM), BF16), jax.ShapeDtypeStruct(w_pool.shape, F32),
                   jax.ShapeDtypeStruct((1, POOL_DIM), F32)],
        compiler_params=_params(("arbitrary",)),
    )(dyp, dyp, p, w_pool, pool_scale)


def _key_bwd(dk_sums, cos_t, sin_t, *, name):
    n, s, _ = dk_sums.shape
    tb = _rows(s)

    def body(dk_ref, cos_ref, sin_ref, dkr_ref):
        tot = dk_ref[0]
        for h in range(1, n):
            tot = tot + dk_ref[h]
        dkr_ref[...] = (tot * cos_ref[...] - _rotate_half(tot * sin_ref[...])).astype(BF16)

    tab = pl.BlockSpec((tb, LANES), lambda i: (i, 0))
    return pl.pallas_call(
        body, name=name, grid=(s // tb,),
        in_specs=[pl.BlockSpec((n, tb, LANES), lambda i: (0, i, 0)), tab, tab], out_specs=tab,
        out_shape=jax.ShapeDtypeStruct((s, LANES), BF16),
        compiler_params=_params(("parallel",)),
    )(dk_sums, cos_t, sin_t)


def _rms_bwd(dy, z, z_off, r, g, *, name):
    s, n = dy.shape
    tb = _rows(s)

    def body(dy_ref, x_ref, r_ref, g_ref, dx_ref, dg_ref):
        @pl.when(pl.program_id(0) == 0)
        def _():
            dg_ref[...] = jnp.zeros_like(dg_ref)

        dyv, rv = dy_ref[...], r_ref[...]
        xn = x_ref[...].astype(F32) * rv
        dg_ref[...] += jnp.sum(dyv * xn, axis=0, keepdims=True)
        dxn = dyv * g_ref[...]
        dx_ref[...] = (rv * (dxn - xn * jnp.mean(dxn * xn, axis=-1, keepdims=True))).astype(BF16)

    blk = pl.BlockSpec((tb, n), lambda i: (i, 0))
    acc_specs, acc_shapes = _acc_specs((n,))
    return pl.pallas_call(
        body, name=name, grid=(s // tb,),
        in_specs=[blk, pl.BlockSpec((tb, n), lambda i: (i, z_off // n)), pl.BlockSpec((tb, 1), lambda i: (i, 0)),
                  pl.BlockSpec((1, n), lambda i: (0, 0))],
        out_specs=[blk] + acc_specs, out_shape=[jax.ShapeDtypeStruct((s, n), BF16)] + acc_shapes,
        compiler_params=_params(("arbitrary",)),
    )(dy, z, r, g)


def _sum_slots(a, n, *, name, out_dtype=F32):
    _, rows, cols = a.shape
    tr = _tile(rows, 256, 8)

    def body(a_ref, out_ref):
        tot = a_ref[0].astype(F32)
        for j in range(1, n):
            tot = tot + a_ref[j].astype(F32)
        out_ref[...] = tot.astype(out_dtype)

    return pl.pallas_call(
        body, name=name, grid=(rows // tr,),
        in_specs=[pl.BlockSpec((n, tr, cols), lambda i: (0, i, 0))],
        out_specs=pl.BlockSpec((tr, cols), lambda i: (i, 0)),
        out_shape=jax.ShapeDtypeStruct((rows, cols), out_dtype),
        compiler_params=_params(("parallel",)),
    )(a)


def _add2_stacked(a, b, stacked, l, *, name):
    rows, cols = a.shape
    tr = _tile(rows, 256, 8)

    def body(a_ref, b_ref, *rest):
        rest[-1][...] = a_ref[...] + b_ref[...]

    blk = pl.BlockSpec((tr, cols), lambda i: (i, 0))
    carried = [] if stacked is None else [stacked]
    return pl.pallas_call(
        body, name=name, grid=(rows // tr,),
        in_specs=[blk, blk] + [pl.BlockSpec(memory_space=pl.ANY) for _ in carried],
        out_specs=pl.BlockSpec((None, tr, cols), lambda i: (l, i, 0)),
        out_shape=jax.ShapeDtypeStruct((DEPTH, rows, cols), F32),
        input_output_aliases={2: 0} if carried else {},
        compiler_params=_params(("parallel",)),
    )(a, b, *carried)


def _adamw(w, g, m, v, *, name):
    shape = w.shape
    if w.ndim == 2:
        w, g, m, v = (a.reshape((1,) + shape) for a in (w, g, m, v))
    layers, rows, cols = w.shape
    tr = _tile(rows, max(8, (1 << 18) // cols), 8)
    c1 = 1.0 - ADAM_B1 ** ADAM_STEP
    c2 = 1.0 - ADAM_B2 ** ADAM_STEP

    def body(w_ref, g_ref, m_ref, v_ref, d_ref, nm_ref, nv_ref):
        gv = g_ref[...]
        nm = ADAM_B1 * m_ref[...] + (1.0 - ADAM_B1) * gv
        nv = ADAM_B2 * v_ref[...] + (1.0 - ADAM_B2) * (gv * gv)
        nm_ref[...] = nm
        nv_ref[...] = nv
        d_ref[...] = -ADAM_LR * ((nm / c1) / (jnp.sqrt(nv / c2) + ADAM_EPS) + ADAM_WD * w_ref[...])

    blk = pl.BlockSpec((None, tr, cols), lambda l, i: (l, i, 0))
    outs = pl.pallas_call(
        body, name=name, grid=(layers, rows // tr), in_specs=[blk] * 4, out_specs=[blk] * 3,
        out_shape=[jax.ShapeDtypeStruct((layers, rows, cols), F32)] * 3,
        compiler_params=_params(("parallel", "parallel")),
    )(w, g, m, v)
    return [o.reshape(shape) for o in outs]


def _adamw_summed(w, halves, m, v, *, name):
    layers, rows, cols = w.shape
    tr = _tile(rows, max(8, (1 << 18) // cols), 8)
    tc = _tile(cols, max(LANES, (1 << 18) // tr))
    nb = rows // tr
    c1 = 1.0 - ADAM_B1 ** ADAM_STEP
    c2 = 1.0 - ADAM_B2 ** ADAM_STEP

    def body(w_ref, m_ref, v_ref, *rest):
        half_refs, (g_ref, d_ref, nm_ref, nv_ref) = rest[:2 * layers], rest[2 * layers:]
        for k in range(layers):
            @pl.when(pl.program_id(0) == k)
            def _(k=k):
                gv = half_refs[2 * k][...] + half_refs[2 * k + 1][...]
                g_ref[...] = gv
                nm = ADAM_B1 * m_ref[...] + (1.0 - ADAM_B1) * gv
                nv = ADAM_B2 * v_ref[...] + (1.0 - ADAM_B2) * (gv * gv)
                nm_ref[...] = nm
                nv_ref[...] = nv
                d_ref[...] = -ADAM_LR * ((nm / c1) / (jnp.sqrt(nv / c2) + ADAM_EPS) + ADAM_WD * w_ref[...])

    def half_spec(k):
        def index(l, i, j):
            at_k = l == k
            return (jnp.where(at_k, i, jnp.where(l < k, 0, nb - 1)), jnp.where(at_k, j, jnp.where(l < k, 0, cols // tc - 1)))
        return pl.BlockSpec((tr, tc), index)

    blk = pl.BlockSpec((None, tr, tc), lambda l, i, j: (l, i, j))
    return pl.pallas_call(
        body, name=name, grid=(layers, nb, cols // tc),
        in_specs=[blk] * 3 + [half_spec(k) for k in range(layers) for _ in range(2)], out_specs=[blk] * 4,
        out_shape=[jax.ShapeDtypeStruct((layers, rows, cols), F32)] * 4,
        compiler_params=_params(("parallel", "parallel", "parallel")),
    )(w, m, v, *[half for pair in halves for half in pair])


def _coords():
    return lax.axis_index("x"), lax.axis_index("y"), lax.axis_index("c")


def _other_chips(x, y):
    return [(1 - x, y), (x, 1 - y), (1 - x, 1 - y)]


def _gather_rows(x_ref, out_ref, send_sems, recv_sems, local_sem, m_per):
    x, y, c = _coords()
    me, sibling = (x, y, c), (x, y, 1 - c)
    chips = _other_chips(x, y)

    def rows(px, py, pc):
        return out_ref.at[pl.ds((4 * px + 2 * py + pc) * m_per, m_per), :]

    def copy(k, block, to, src=None):
        return pltpu.make_async_remote_copy(
            src_ref=rows(*block) if src is None else src, dst_ref=rows(*block),
            send_sem=send_sems.at[k], recv_sem=recv_sems.at[k], device_id=to, device_id_type=MESH)

    mine = pltpu.make_async_copy(x_ref, rows(*me), local_sem)
    mine.start()
    first = [copy(0, me, sibling, src=x_ref)]
    first += [copy(1 + j, me, (*chip, c), src=x_ref) for j, chip in enumerate(chips)]
    for cp in first:
        cp.start()
    passed = [copy(4 + j, (*chip, c), sibling) for j, chip in enumerate(chips)]
    for j, chip in enumerate(chips):
        copy(1 + j, (*chip, c), me).wait_recv()
        passed[j].start()
    copy(0, sibling, me).wait_recv()
    for j, chip in enumerate(chips):
        copy(4 + j, (*chip, 1 - c), me).wait_recv()
    for cp in first + passed:
        cp.wait_send()
    mine.wait()


def _ada_modulation(c_blk, w_ada, b_mine, *, name):
    depth, _, cols = w_ada.shape
    blk_rows = c_blk.shape[0]
    sems = [pltpu.SemaphoreType.DMA((7,)), pltpu.SemaphoreType.DMA((7,)), pltpu.SemaphoreType.DMA]

    def body(c_ref, w_ref, b_ref, act_ref, mod_ref, c_all, prod_all, mod_mine, *sem_refs):
        _gather_rows(c_ref, c_all, *sem_refs[:3], blk_rows)
        cv = c_all[...]
        c_all[...] = cv * _sigmoid(cv)
        lhs = c_all[...].astype(BF16)
        for d in range(N_DEV):
            act_ref[d:d + 1, :] = c_all[d * blk_rows:d * blk_rows + 1, :]
        for l in range(depth):
            prod_all[...] = (jnp.dot(lhs, w_ref[l].astype(BF16), preferred_element_type=F32)
                             + b_ref[:, l * cols:(l + 1) * cols])
            for d in range(N_DEV):
                mod_mine[l * N_DEV + d:l * N_DEV + d + 1, :] = prod_all[d * blk_rows:d * blk_rows + 1, :]
        _gather_rows(mod_mine, mod_ref, *sem_refs[3:], depth * N_DEV)

    vmem = pl.BlockSpec(memory_space=pltpu.VMEM)
    act, mod = pl.pallas_call(
        body, name=name,
        out_shape=[jax.ShapeDtypeStruct((N_DEV, D_MODEL), F32), jax.ShapeDtypeStruct((N_DEV * depth * N_DEV, cols), F32)],
        in_specs=[vmem, vmem, vmem], out_specs=[vmem, vmem],
        scratch_shapes=[pltpu.VMEM((N_DEV * blk_rows, D_MODEL), F32), pltpu.VMEM((N_DEV * blk_rows, cols), F32),
                        pltpu.VMEM((depth * N_DEV, cols), F32)] + sems + sems,
        compiler_params=_params(),
    )(c_blk, w_ada, b_mine)
    return act.astype(BF16), mod


def _all_gather_small(blk, *, name, swaps=()):
    m_per, n = blk.shape
    n_swaps = len(swaps)

    def body(x_ref, *refs):
        swap_srcs, out_ref, swap_outs = refs[:n_swaps], refs[n_swaps], refs[n_swaps + 1:2 * n_swaps + 1]
        send_sems, recv_sems, local_sem = refs[2 * n_swaps + 1:2 * n_swaps + 4]
        x, y, c = _coords()
        swapping = [pltpu.make_async_remote_copy(src_ref=src, dst_ref=dst, send_sem=refs[-2].at[k], recv_sem=refs[-1].at[k],
                                                 device_id=(x, y, 1 - c), device_id_type=MESH)
                    for k, (src, dst) in enumerate(zip(swap_srcs, swap_outs))]
        for cp in swapping:
            cp.start()
        _gather_rows(x_ref, out_ref, send_sems, recv_sems, local_sem, m_per)
        for cp in swapping:
            cp.wait()

    any_spec = pl.BlockSpec(memory_space=pl.ANY)
    outs = pl.pallas_call(
        body, name=name,
        out_shape=[jax.ShapeDtypeStruct((N_DEV * m_per, n), blk.dtype)] + [jax.ShapeDtypeStruct(a.shape, a.dtype)
                                                                           for a in swaps],
        in_specs=[pl.BlockSpec(memory_space=pltpu.VMEM)] + [any_spec] * n_swaps,
        out_specs=[pl.BlockSpec(memory_space=pltpu.VMEM)] + [any_spec] * n_swaps,
        scratch_shapes=[pltpu.SemaphoreType.DMA((7,)), pltpu.SemaphoreType.DMA((7,)), pltpu.SemaphoreType.DMA]
        + ([pltpu.SemaphoreType.DMA((n_swaps,)), pltpu.SemaphoreType.DMA((n_swaps,))] if swaps else []),
        compiler_params=_params(),
    )(blk, *swaps)
    return (outs[0], outs[1:]) if swaps else outs[0]


HBM_SPEC = pl.BlockSpec(memory_space=pltpu.HBM)
SEM_SPEC = pl.BlockSpec(memory_space=pltpu.SEMAPHORE)
DATAFLOW = pltpu.SideEffectType.DATAFLOW_SIDE_EFFECTING


def _chip_copies(src_ref, land_ref, send_sems, recv_sems, scatter):
    x, y, c = _coords()
    my = 2 * x + y
    outgoing, incoming = [], []
    for k, (px, py) in enumerate(_other_chips(x, y)):
        peer = 2 * px + py

        def copy(src_slot, dst_slot):
            return pltpu.make_async_remote_copy(
                src_ref=src_ref.at[src_slot] if scatter else src_ref, dst_ref=land_ref.at[dst_slot],
                send_sem=send_sems.at[k], recv_sem=recv_sems.at[k], device_id=(px, py, c), device_id_type=MESH)

        outgoing.append(copy(peer, my))
        incoming.append(copy(my, peer))
    return outgoing, incoming


def _exchange_start(srcs, *, name, scatter):
    n = len(srcs)
    land_shapes = [src.shape if scatter else (N_CHIPS,) + src.shape for src in srcs]

    def body(*refs):
        for k in range(n):
            send_sems, recv_sems = refs[2 * n + 4 * k], refs[2 * n + 4 * k + 1]
            outgoing, _ = _chip_copies(refs[k], refs[n + k], send_sems, recv_sems, scatter)
            for cp in outgoing:
                cp.start()
        refs[-1][...] = jnp.zeros_like(refs[-1])

    out_shape, out_specs, aliases = [], [], {}
    for k, (src, land_shape) in enumerate(zip(srcs, land_shapes)):
        out_shape += [pltpu.SemaphoreType.DMA((N_CHIPS - 1,)), pltpu.SemaphoreType.DMA((N_CHIPS - 1,)),
                      pltpu.HBM(src.shape, src.dtype), pltpu.HBM(land_shape, src.dtype)]
        out_specs += [SEM_SPEC, SEM_SPEC, HBM_SPEC, HBM_SPEC]
        aliases.update({k: 4 * k + 2, n + k: 4 * k + 3})
    outs = pl.pallas_call(
        body, name=name,
        out_shape=tuple(out_shape) + (jax.ShapeDtypeStruct((8, LANES), F32),),
        in_specs=(HBM_SPEC,) * (2 * n),
        out_specs=tuple(out_specs) + (pl.BlockSpec(memory_space=pltpu.VMEM),),
        input_output_aliases=aliases,
        compiler_params=pltpu.CompilerParams(has_side_effects=DATAFLOW),
    )(*[pltpu.with_memory_space_constraint(src, pltpu.HBM) for src in srcs],
      *[pltpu.with_memory_space_constraint(lax.empty(shape, src.dtype), pltpu.HBM)
        for src, shape in zip(srcs, land_shapes)])
    return [tuple(outs[4 * k:4 * k + 4]) for k in range(n)], outs[-1]


def _exchange_wait(started, after, *, name, scatter):
    send_sems, recv_sems, src_thru, land_thru = started

    def body(src_ref, land_ref, send_sems, recv_sems, after_ref, src_dead, got_ref):
        outgoing, incoming = _chip_copies(src_ref, land_ref, send_sems, recv_sems, scatter)
        for cp in outgoing:
            cp.wait_send()
        for cp in incoming:
            cp.wait_recv()

    return pl.pallas_call(
        body, name=name,
        out_shape=(pltpu.HBM(src_thru.shape, src_thru.dtype), pltpu.HBM(land_thru.shape, land_thru.dtype)),
        in_specs=(HBM_SPEC, HBM_SPEC, SEM_SPEC, SEM_SPEC, pl.BlockSpec(memory_space=pl.ANY)),
        out_specs=(HBM_SPEC, HBM_SPEC),
        input_output_aliases={0: 0, 1: 1},
        compiler_params=pltpu.CompilerParams(has_side_effects=DATAFLOW),
    )(src_thru, land_thru, send_sems, recv_sems, after)


def _pack_rows(a):
    return a.reshape(-1, D_MODEL)


def _pad_heads(w, width):
    r = w.shape[0]
    return jnp.pad(w, ((0, 0), (0, 0), (0, HEAD_PAD - width))).reshape(r, N_HEADS * HEAD_PAD)


MIX_NAMES = ("w_uq", "w_uk", "w_uv", "p_pool", "p_attn", "w_out")
GROUPS = ("in", "mix", "ff1", "ff2")


def _local_shard(weights, l, group, zero):
    if group == "mix":
        shard = jnp.concatenate([_pack_rows(weights[n][l]) for n in MIX_NAMES], axis=0)
    else:
        shard = weights[{"in": "w_in", "ff1": "w_ff1", "ff2": "w_ff2"}[group]][l]
    return (shard + zero).astype(BF16)


def _unpack_weights(gathered, group):
    def cols(a, k):
        return a.reshape(N_CHIPS, k, -1).transpose(1, 0, 2).reshape(k, -1)

    if group == "in":
        full = gathered.reshape(W_IN_COLS, D_MODEL)
        u, cq, ckv, kr, gates = (full[a:b] for a, b in (W_IN_U, W_IN_CQ, W_IN_CKV, W_IN_KR, W_IN_GATES))
        kr = jnp.pad(kr, ((QK_NOPE, HEAD_PAD - QK_DIM), (0, 0)))
        return dict(w_in=jnp.concatenate([cq, kr, u, gates, ckv], axis=0))
    if group == "ff1":
        return dict(w_ff1=gathered)
    if group == "ff2":
        return dict(w_ff2=gathered.reshape(D_FF, D_MODEL))

    def p_attn(a):
        full = cols(a, ATTN_DIM).reshape(N_HEADS, V_DIM, D_MODEL)
        return jnp.pad(full, ((0, 0), (0, HEAD_PAD - V_DIM), (0, 0))).reshape(N_HEADS * HEAD_PAD, D_MODEL)

    build = dict(
        w_uq=lambda a: _pad_heads(a.reshape(Q_LORA, N_HEADS, QK_DIM), QK_DIM),
        w_uk=lambda a: _pad_heads(a.reshape(KV_LORA, N_HEADS, QK_NOPE), QK_NOPE),
        w_uv=lambda a: _pad_heads(a.reshape(KV_LORA, N_HEADS, V_DIM), V_DIM),
        p_pool=lambda a: cols(a, POOL_DIM),
        p_attn=p_attn,
        w_out=lambda a: a.reshape(D_MODEL, D_MODEL),
    )
    w, off = {}, 0
    for name in MIX_NAMES:
        w[name] = build[name](gathered[:, off:off + ROWS_OF[name]])
        off += ROWS_OF[name]
    return w


def _pack_grads(g, group):
    def cols(a):
        k = a.shape[0]
        return a.reshape(k, N_CHIPS, -1).transpose(1, 0, 2).reshape(N_CHIPS, -1, D_MODEL)

    def rows(a):
        return a.reshape(N_CHIPS, -1, D_MODEL)

    def heads(width):
        return lambda a: rows(a.reshape(a.shape[0], N_HEADS, HEAD_PAD)[:, :, :width])

    if group == "in":
        full = jnp.concatenate([g["u"], g["cq"], g["ckv"], g["kr"][QK_NOPE:QK_DIM], g["ga"], g["gb"]], axis=0)
        return full.reshape(N_CHIPS, W_IN_SHARD, D_MODEL)
    if group == "ff1":
        return g["w_ff1"]
    if group == "ff2":
        return g["w_ff2"].reshape(N_CHIPS, D_FF // N_CHIPS, D_MODEL)

    def p_attn(a):
        return cols(a.reshape(N_HEADS, HEAD_PAD, D_MODEL)[:, :V_DIM].reshape(ATTN_DIM, D_MODEL))

    build = dict(w_uq=heads(QK_DIM), w_uk=heads(QK_NOPE), w_uv=heads(V_DIM), p_pool=cols, p_attn=p_attn, w_out=rows)
    return jnp.concatenate([build[name](g[name]) for name in MIX_NAMES], axis=1)


def _per_head(fn, acc, *tables):
    return jnp.concatenate([fn(acc[:, h * HEAD_PAD:(h + 1) * HEAD_PAD], *tables) for h in range(N_HEADS)], axis=1)


def _rope_head(a, cos, sin):
    lane = lax.broadcasted_iota(jnp.int32, a.shape, 1)
    return a * (cos + jnp.where(lane < QK_NOPE, 1.0, 0.0)) + _rotate_half(a) * sin


def _layer_fwd(l, x, mod, get_weights, small, cos_t, sin_t):
    sh1, sc1, g1, sh2, sc2, g2 = mod
    tag = f"_l{l}"
    h, r1 = _norm_mod(x, small["ln1_g"], sc1, sh1, name="norm1" + tag)
    w = dict(get_weights("in", h))
    (z,) = _mm(h, w["w_in"], tb=True, name="in_proj" + tag, out_dtypes=(BF16,))
    p, yp, cq, ckv, kr, rq, rkv = _mixer_pre(z, cos_t, sin_t, small["w_pool"], small["pool_scale"],
                                              small["q_norm_g"], small["kv_norm_g"], name="mixer_pre" + tag)
    w.update(get_weights("mix", yp))
    (ya,) = _mm(yp, w["p_pool"], name="pool_out" + tag, out_dtypes=(BF16,))
    (q,) = _mm(cq, w["w_uq"], name="q_proj" + tag, out_dtypes=(BF16,),
               epilogue=lambda acc, cos, sin: (_per_head(_rope_head, acc, cos, sin),),
               extras=((cos_t, "table"), (sin_t, "table")))
    (k,) = _mm(ckv, w["w_uk"], name="k_proj" + tag, out_dtypes=(BF16,),
               epilogue=lambda acc, krv: (_per_head(lambda a, b: a + b, acc, krv),), extras=((kr, "table"),))
    (v,) = _mm(ckv, w["w_uv"], name="v_proj" + tag, out_dtypes=(BF16,))
    o, lse = _attn_fwd(q, k, v, name="attn_fwd" + tag)
    yb, merged = _mm(o, w["p_attn"], name="attn_out" + tag, out_dtypes=(BF16, BF16),
                     epilogue=lambda acc, ga, gb, yav: (acc, _sigmoid(ga) * yav + _sigmoid(gb) * acc),
                     extras=((z, ("tile", ZC_GA // D_MODEL)), (z, ("tile", ZC_GB // D_MODEL)), (ya, "tile")))
    mo, x1 = _mm(merged, w["w_out"], name="mix_out" + tag, out_dtypes=(BF16, F32),
                 epilogue=lambda acc, xr, g: (acc, xr + g * acc), extras=((x, "tile"), (g1, "row")))
    h2, r2 = _norm_mod(x1, small["ln2_g"], sc2, sh2, name="norm2" + tag)
    w.update(get_weights("ff1", merged))
    f, act = _mm(h2, w["w_ff1"], b_stack=True, name="ff1" + tag, out_dtypes=(BF16, BF16),
                 epilogue=lambda acc: (acc, jnp.square(jnp.maximum(acc, 0.0))))
    w.update(get_weights("ff2", act))
    m2, x2 = _mm(act, w["w_ff2"], name="ff2" + tag, out_dtypes=(BF16, F32),
                 epilogue=lambda acc, xr, g: (acc, xr + g * acc), extras=((x1, "tile"), (g2, "row")))
    saved = dict(x=x, h=h, r1=r1, z=z, p=p, yp=yp, cq=cq, ckv=ckv, rq=rq, rkv=rkv, ya=ya, q=q, k=k, v=v, o=o, lse=lse,
                 yb=yb, merged=merged, mo=mo, x1=x1, h2=h2, r2=r2, f=f, act=act, m2=m2)
    return x2, saved, w


def _merge_grads(dm, ga, gb, ya, yb):
    sa, sb = _sigmoid(ga), _sigmoid(gb)
    return dm * sa, dm * sb, dm * ya * (sa * (1.0 - sa)), dm * yb * (sb * (1.0 - sb))


def _layer_bwd(l, dx2, dm2, dg2, sv, mod, w, small, cos_t, sin_t, send_grads, gate_below):
    sh1, sc1, g1, sh2, sc2, g2 = mod
    tag = f"_l{l}"
    gw = {}
    (df,) = _mm(dm2, w["w_ff2"], tb=True, name="ff2_dx" + tag, out_dtypes=(BF16,),
                epilogue=lambda acc, f: (acc * (2.0 * jnp.maximum(f, 0.0)),), extras=((sv["f"], "tile"),))
    (g_ff2,) = _mm(sv["act"], dm2, ta=True, name="ff2_dw" + tag, out_dtypes=(BF16,))
    (g_ff1,) = _mm(sv["h2"], df, ta=True, out_stack=N_CHIPS, name="ff1_dw" + tag, out_dtypes=(BF16,))
    sc2 = sc2 + send_grads("ff2", dict(w_ff2=g_ff2)) + send_grads("ff1", dict(w_ff1=g_ff1))
    (dh2,) = _mm(df, w["w_ff1"], tb=True, b_stack=True, name="ff1_dx" + tag)
    dx1, dln2, dsc2, dsh2, dmo, dg1 = _norm_mod_bwd(dh2, sv["x1"], sv["r2"], small["ln2_g"], sc2, dx2,
                                                    gate=(sv["mo"], g1), name="norm2_bwd" + tag)
    dya, dyb, dga, dgb = _mm(dmo, w["w_out"], tb=True, name="mix_out_dx" + tag, out_dtypes=(BF16,) * 4, tm=512,
                             epilogue=_merge_grads,
                             extras=((sv["z"], ("tile", ZC_GA // D_MODEL)), (sv["z"], ("tile", ZC_GB // D_MODEL)),
                                     (sv["ya"], "tile"), (sv["yb"], "tile")))
    (gw["w_out"],) = _mm(sv["merged"], dmo, ta=True, name="mix_out_dw" + tag, out_dtypes=(BF16,))
    (gw["p_pool"],) = _mm(sv["yp"], dya, ta=True, name="pool_out_dw" + tag, out_dtypes=(BF16,))
    (dyp,) = _mm(dya, w["p_pool"], tb=True, name="pool_out_dx" + tag)
    du, g_w_pool, g_pool_scale = _pool_bwd(dyp, sv["p"], small["w_pool"], small["pool_scale"], name="pool_bwd" + tag)
    (gw["p_attn"],) = _mm(sv["o"], dyb, ta=True, name="attn_out_dw" + tag, out_dtypes=(BF16,))
    (do,) = _mm(dyb, w["p_attn"], tb=True, name="attn_out_dx" + tag, out_dtypes=(BF16,))
    dql, dkb, dv, dk_sums = _attn_bwd(sv["q"], sv["k"], sv["v"], do, sv["o"], sv["lse"], cos_t, sin_t,
                                      name="attn_bwd" + tag)
    dkr = _key_bwd(dk_sums, cos_t, sin_t, name="key_bwd" + tag)
    (gw["w_uq"],) = _mm(sv["cq"], dql, ta=True, name="q_proj_dw" + tag, out_dtypes=(BF16,))
    (gw["w_uk"],) = _mm(sv["ckv"], dkb, ta=True, name="k_proj_dw" + tag, out_dtypes=(BF16,))
    (gw["w_uv"],) = _mm(sv["ckv"], dv, ta=True, name="v_proj_dw" + tag, out_dtypes=(BF16,))
    (dcq,) = _mm(dql, w["w_uq"], tb=True, name="q_proj_dx" + tag)
    (dckv,) = _mm(dkb, w["w_uk"], tb=True, second=(dv, w["w_uv"]), name="kv_proj_dx" + tag)
    q_norm_g = small["q_norm_g"] + send_grads("mix", gw)
    dcq_raw, g_qn = _rms_bwd(dcq, sv["z"], ZC_CQ, sv["rq"], q_norm_g, name="q_norm_bwd" + tag)
    dckv_raw, g_kvn = _rms_bwd(dckv, sv["z"], ZC_CKV, sv["rkv"], small["kv_norm_g"], name="kv_norm_bwd" + tag)
    dz = dict(cq=dcq_raw, kr=dkr, u=du, ga=dga, gb=dgb, ckv=dckv_raw)
    g_in = {n: _mm(piece, sv["h"], ta=True, name=f"in_proj_dw_{n}" + tag, out_dtypes=(BF16,))[0]
            for n, piece in dz.items()}
    sc1 = sc1 + send_grads("in", g_in)
    dh = _mm_sum(list(dz.values()), w["w_in"], [Z_OFFSETS[n] for n in dz], name="in_proj_dx" + tag)
    dx, dln1, dsc1, dsh1, *below = _norm_mod_bwd(dh, sv["x"], sv["r1"], small["ln1_g"], sc1, dx1, gate=gate_below,
                                                 name="norm1_bwd" + tag)
    dmod = jnp.concatenate([dsh1, dsc1, dg1, dsh2, dsc2, dg2], axis=0)
    gsmall = dict(ln1_g=dln1, ln2_g=dln2, q_norm_g=g_qn, kv_norm_g=g_kvn, w_pool=g_w_pool, pool_scale=g_pool_scale)
    return dx, dmod, gsmall, below


SMALL_LOSS = 6
SMALL_SINGLES = 16
SMALL_POOL = 24
SMALL_POOL_ROWS = len(POOL_WINDOWS) * POOL_GROUP * POOL_GROUP // D_MODEL
SMALL_ROWS = SMALL_POOL + DEPTH * SMALL_POOL_ROWS


def _pack_small(parts, *, name):
    def body(*refs):
        out_ref = refs[-1]
        out_ref[...] = jnp.zeros_like(out_ref)
        for ref, (_, row) in zip(refs[:-1], parts):
            out_ref[row:row + ref.shape[0], :] = ref[...]

    return pl.pallas_call(body, name=name, out_shape=jax.ShapeDtypeStruct((SMALL_ROWS, D_MODEL), F32),
                          compiler_params=_params())(*[a for a, _ in parts])


def kernel(x, c, positions, ln1_g, ln2_g, w_ada, b_ada, w_in, q_norm_g, w_uq, kv_norm_g, w_uk, w_uv, w_pool, pool_scale, p_pool, p_attn, w_out, w_ff1, w_ff2, final_g, loss_target, m_ln1_g, m_ln2_g, m_w_ada, m_b_ada, m_w_in, m_q_norm_g, m_w_uq, m_kv_norm_g, m_w_uk, m_w_uv, m_w_pool, m_pool_scale, m_p_pool, m_p_attn, m_w_out, m_w_ff1, m_w_ff2, m_final_g, v_ln1_g, v_ln2_g, v_w_ada, v_b_ada, v_w_in, v_q_norm_g, v_w_uq, v_kv_norm_g, v_w_uk, v_w_uv, v_w_pool, v_pool_scale, v_p_pool, v_p_attn, v_w_out, v_w_ff1, v_w_ff2, v_final_g):
    weights = dict(ln1_g=ln1_g, ln2_g=ln2_g, w_ada=w_ada, b_ada=b_ada, w_in=w_in, q_norm_g=q_norm_g, w_uq=w_uq,
                   kv_norm_g=kv_norm_g, w_uk=w_uk, w_uv=w_uv, w_pool=w_pool, pool_scale=pool_scale, p_pool=p_pool,
                   p_attn=p_attn, w_out=w_out, w_ff1=w_ff1, w_ff2=w_ff2, final_g=final_g)
    moms = dict(ln1_g=m_ln1_g, ln2_g=m_ln2_g, w_ada=m_w_ada, b_ada=m_b_ada, w_in=m_w_in, q_norm_g=m_q_norm_g,
                w_uq=m_w_uq, kv_norm_g=m_kv_norm_g, w_uk=m_w_uk, w_uv=m_w_uv, w_pool=m_w_pool,
                pool_scale=m_pool_scale, p_pool=m_p_pool, p_attn=m_p_attn, w_out=m_w_out, w_ff1=m_w_ff1,
                w_ff2=m_w_ff2, final_g=m_final_g)
    vels = dict(ln1_g=v_ln1_g, ln2_g=v_ln2_g, w_ada=v_w_ada, b_ada=v_b_ada, w_in=v_w_in, q_norm_g=v_q_norm_g,
                w_uq=v_w_uq, kv_norm_g=v_kv_norm_g, w_uk=v_w_uk, w_uv=v_w_uv, w_pool=v_w_pool,
                pool_scale=v_pool_scale, p_pool=v_p_pool, p_attn=v_p_attn, w_out=v_w_out, w_ff1=v_w_ff1,
                w_ff2=v_w_ff2, final_g=v_final_g)
    order = list(weights)
    for table in (weights, moms, vels):
        table["w_in"] = jnp.swapaxes(table["w_in"], 1, 2)
    seq = x.shape[1]
    my_chip = 2 * lax.axis_index("x") + lax.axis_index("y")
    my_dev = 2 * my_chip + lax.axis_index("c")
    ada_cols = w_ada.shape[2]

    small = [dict(ln1_g=ln1_g[l:l + 1], ln2_g=ln2_g[l:l + 1], q_norm_g=q_norm_g[l:l + 1], kv_norm_g=kv_norm_g[l:l + 1],
                  w_pool=w_pool[l], pool_scale=pool_scale[l:l + 1]) for l in range(DEPTH)]

    b_mine = lax.dynamic_slice_in_dim(b_ada, my_chip * ada_cols, ada_cols, axis=1).reshape(1, DEPTH * ada_cols)
    c_act, mod_all = _ada_modulation(jnp.pad(c, ((0, 7), (0, 0))), w_ada, b_mine, name="ada_modulation")
    mod_all = mod_all.reshape(N_DEV, DEPTH, N_DEV, ada_cols)

    zero = mod_all[0, 0, 0, 0] * 0.0
    keys = [(l, group) for l in range(DEPTH) for group in GROUPS]
    exchanges, token = _exchange_start([_local_shard(weights, l, group, zero) for l, group in keys],
                                       name="weights_send", scatter=False)
    started = dict(zip(keys, exchanges))
    pin = token[0:1, 0:1]

    def gathered_weights(l, group, after):
        mine, land = _exchange_wait(started[l, group], after, name=f"weights_wait_l{l}_{group}", scatter=False)
        land = lax.dynamic_update_slice_in_dim(land, mine[None], my_chip, axis=0)
        return _unpack_weights(land, group)

    mods = []
    for l in range(DEPTH):
        row = jnp.concatenate([lax.dynamic_index_in_dim(mod_all[2 * j, l], my_dev, axis=0, keepdims=True)
                               for j in range(N_CHIPS)], axis=1) + pin
        mods.append([row[:, i * D_MODEL:(i + 1) * D_MODEL] for i in range(N_MOD)])

    inv_freq = ROPE_THETA ** (-jnp.arange(0, QK_ROPE, 2, dtype=F32) / QK_ROPE)
    freq_lanes = jnp.concatenate([jnp.zeros((QK_NOPE,), F32), inv_freq, inv_freq,
                                  jnp.zeros((HEAD_PAD - QK_DIM,), F32)]).reshape(1, LANES)
    cos_t, sin_t = _rope_tables(positions.reshape(seq, 1), freq_lanes, name="rope_tables")

    xs, saved, wl = x.reshape(seq, D_MODEL), [], []
    for l in range(DEPTH):
        xs, sv, w_l = _layer_fwd(l, xs, mods[l], functools.partial(gathered_weights, l), small[l], cos_t, sin_t)
        saved.append(sv)
        wl.append(w_l)
    dx, loss_part, g_final, dm2, dg2 = _final_loss(xs, final_g.reshape(1, D_MODEL), loss_target.reshape(seq, D_MODEL),
                                                   saved[-1]["m2"], mods[-1][5], name="final_loss")

    sent, pending = [], []
    send_after = {(0, "ff1"), (0, "mix"), (0, "in")}

    def send_grads(l, group, g):
        pending.append((l, group, _pack_grads(g, group)))
        if (l, group) not in send_after:
            return jnp.zeros((1, 1), F32)
        exchanges, token_g = _exchange_start([gpack for _, _, gpack in pending], name=f"grads_send_l{l}_{group}",
                                             scatter=True)
        sent.extend((item[0], item[1], exchange) for item, exchange in zip(pending, exchanges))
        pending.clear()
        return token_g[0:1, 0:1]

    dmod, gsmall = [None] * DEPTH, [None] * DEPTH
    for l in reversed(range(DEPTH)):
        gate_below = (saved[l - 1]["m2"], mods[l - 1][5]) if l > 0 else None
        dx, dmod[l], gsmall[l], below = _layer_bwd(l, dx, dm2, dg2, saved[l], mods[l], wl[l], small[l], cos_t, sin_t,
                                                   functools.partial(send_grads, l), gate_below)
        dm2, dg2 = below if below else (None, None)
    grads = dict(x=dx.reshape(1, seq, D_MODEL))

    big_parts, after = [], dmod[0]
    for l, group, started_g in sent:
        tg = f"_l{l}_{group}"
        gpack, land = _exchange_wait(started_g, after, name="grads_wait" + tg, scatter=True)
        own = lax.dynamic_index_in_dim(gpack, my_chip, axis=0, keepdims=True)
        land = lax.dynamic_update_slice_in_dim(land, own, my_chip, axis=0)
        big_parts.append(_sum_slots(land, N_CHIPS, name="grads_sum_chips" + tg))
        after = big_parts[-1]

    def lanes(a):
        flat = a.reshape(1, -1)
        return jnp.pad(flat, ((0, 0), (0, D_MODEL - flat.shape[1])))

    singles = [gsmall[0]["ln1_g"], gsmall[1]["ln1_g"], gsmall[0]["ln2_g"], gsmall[1]["ln2_g"], g_final,
               lanes(jnp.concatenate([gsmall[l]["pool_scale"] for l in range(DEPTH)], axis=1)),
               lanes(jnp.concatenate([gsmall[l]["q_norm_g"] for l in range(DEPTH)], axis=1)),
               lanes(jnp.concatenate([gsmall[l]["kv_norm_g"] for l in range(DEPTH)], axis=1))]
    parts = [(dmod[0], 0), (lanes(loss_part), SMALL_LOSS), (dmod[1], 8)]
    parts += [(a, SMALL_SINGLES + i) for i, a in enumerate(singles)]
    parts += [(gsmall[l]["w_pool"].reshape(-1, D_MODEL), SMALL_POOL + l * SMALL_POOL_ROWS) for l in range(DEPTH)]
    small_all, big_others = _all_gather_small(_pack_small(parts, name="small_grads_pack"), swaps=big_parts,
                                              name="small_grads_all_gather")
    small_all = small_all.reshape(N_DEV, SMALL_ROWS, D_MODEL)
    ssum = _sum_slots(small_all, N_DEV, name="small_grads_sum")
    loss = ssum[SMALL_LOSS, 0]
    grads["b_ada"] = jnp.stack([ssum[8 * l:8 * l + N_MOD] for l in range(DEPTH)]).reshape(DEPTH, N_MOD * D_MODEL)
    grads["ln1_g"] = ssum[SMALL_SINGLES:SMALL_SINGLES + 2]
    grads["ln2_g"] = ssum[SMALL_SINGLES + 2:SMALL_SINGLES + 4]
    grads["final_g"] = ssum[SMALL_SINGLES + 4]
    grads["pool_scale"] = ssum[SMALL_SINGLES + 5].reshape(DEPTH, POOL_DIM)
    grads["q_norm_g"] = ssum[SMALL_SINGLES + 6, :DEPTH * Q_LORA].reshape(DEPTH, Q_LORA)
    grads["kv_norm_g"] = ssum[SMALL_SINGLES + 7, :DEPTH * KV_LORA].reshape(DEPTH, KV_LORA)
    grads["w_pool"] = ssum[SMALL_POOL:SMALL_ROWS].reshape(w_pool.shape)

    mix_sum, halves = None, {group: [None] * DEPTH for group in ("in", "ff1", "ff2")}
    for (l, group, _), part, other in zip(sent, big_parts, big_others):
        if group == "mix":
            mix_sum = _add2_stacked(part, other, mix_sum, l, name=f"grads_sum_cores_l{l}_mix")
        else:
            halves[group][l] = (part, other)
    off = 0
    for name in MIX_NAMES:
        grads[name] = mix_sum[:, off:off + ROWS_OF[name]].reshape(weights[name].shape)
        off += ROWS_OF[name]

    c_act_t = jnp.pad(c_act.T, ((0, 0), (0, LANES - N_DEV)))
    d_mine = []
    for l in range(DEPTH):
        d_all = small_all[:, 8 * l:8 * l + N_MOD].reshape(N_DEV, N_MOD * D_MODEL)
        d_mine.append(lax.dynamic_slice_in_dim(d_all, my_chip * ada_cols, ada_cols, axis=1))
    d_cat = jnp.pad(jnp.concatenate(d_mine, axis=1), ((0, LANES - N_DEV), (0, 0)))
    (grads["w_ada"],) = _mm(c_act_t, d_cat, out_stack=DEPTH, name="ada_dw")

    def view(a):
        return a.reshape(1, -1) if a.ndim == 1 else a if a.ndim == 3 else a.reshape(-1, a.shape[-1])

    delta, new_m, new_v = {}, {}, {}
    for name in order:
        shape = weights[name].shape
        group = {"w_in": "in", "w_ff1": "ff1", "w_ff2": "ff2"}.get(name)
        if group:
            grads[name], d, nm, nv = _adamw_summed(weights[name], halves[group], moms[name], vels[name],
                                                   name="adamw_" + name)
        else:
            d, nm, nv = _adamw(view(weights[name]), view(grads[name]), view(moms[name]), view(vels[name]),
                               name="adamw_" + name)
        delta[name], new_m[name], new_v[name] = d.reshape(shape), nm.reshape(shape), nv.reshape(shape)
    for table in (grads, delta, new_m, new_v):
        table["w_in"] = jnp.swapaxes(table["w_in"], 1, 2)
    return (loss, grads["x"], *[grads[n] for n in order], *[delta[n] for n in order],
            *[new_m[n] for n in order], *[new_v[n] for n in order])
```

```python
import functools
import math

import jax
import jax.numpy as jnp
from jax import lax
from jax.experimental import pallas as pl
from jax.experimental.pallas import tpu as pltpu

F32 = jnp.float32
BF16 = jnp.bfloat16
MESH = pl.DeviceIdType.MESH

D_MODEL = 1024
DEPTH = 2
POOL_WINDOWS = (2, 4, 8, 16)
POOL_GROUP = 128
POOL_DIM = 512
N_HEADS = 8
QK_NOPE = 64
QK_ROPE = 32
QK_DIM = QK_NOPE + QK_ROPE
V_DIM = 64
HEAD_PAD = 128
Q_LORA = 384
KV_LORA = 256
ROPE_THETA = 10000.0
ATTN_DIM = N_HEADS * V_DIM
D_FF = 4 * D_MODEL
N_MOD = 6
EPS = 1e-6
N_CHIPS = 4
N_DEV = 8

ADAM_LR = 0.001
ADAM_B1 = 0.9
ADAM_B2 = 0.999
ADAM_EPS = 1e-08
ADAM_WD = 0.01
ADAM_STEP = 10

VMEM_LIMIT_BYTES = 56 * 1024 * 1024
LANES = 128
HALO = 16

ZC_CQ = 0
ZC_KR = 384
ZC_U = 512
ZC_GA = 1024
ZC_GB = 2048
ZC_CKV = 3072
Z_DIM = 3328
Z_OFFSETS = dict(cq=ZC_CQ, kr=ZC_KR, u=ZC_U, ga=ZC_GA, gb=ZC_GB, ckv=ZC_CKV)

W_IN_U, W_IN_CQ, W_IN_CKV, W_IN_KR, W_IN_GATES = (0, 512), (512, 896), (896, 1152), (1152, 1184), (1184, 3232)
W_IN_COLS = W_IN_GATES[1]
W_IN_SHARD = W_IN_COLS // N_CHIPS

ROWS_OF = dict(w_uq=72, w_uk=32, w_uv=32, p_pool=128, p_attn=128, w_out=256)


def _params(sem=None, **kw):
    return pltpu.CompilerParams(dimension_semantics=sem, vmem_limit_bytes=VMEM_LIMIT_BYTES, **kw)


def _tile(n, target, unit=LANES):
    best = None
    for t in range(unit, min(n, target) + 1, unit):
        if n % t == 0:
            best = t
    return best if best is not None and 4 * best >= min(n, target) else n


def _near_tile(n, target):
    cands = [t for t in range(LANES, n + 1, LANES) if n % t == 0]
    return min(cands, key=lambda t: abs(math.log(t / target))) if cands else n


def _mm(a, b, *, name, ta=False, tb=False, out_dtypes=(F32,), epilogue=None, extras=(), tm=1024, tn=1024, tk=1024,
        second=None, b_stack=False, out_stack=None):
    (k_dim, m_dim) = a.shape if ta else a.shape[::-1]
    if b_stack:
        g_b, k_b, n_shard = b.shape
        n_dim, k_b = (k_b, g_b * n_shard) if tb else (g_b * n_shard, k_b)
    else:
        (n_dim, k_b) = b.shape if tb else b.shape[::-1]
    assert k_dim == k_b, (a.shape, b.shape)
    n_unit = n_shard if b_stack and not tb else n_dim // out_stack if out_stack else n_dim
    k_unit = n_shard if b_stack and tb else k_dim
    if ta:
        tk = 2 * tk
    tm, tn, tk = _near_tile(m_dim, tm), _near_tile(n_unit, tn), _near_tile(k_unit, tk)
    nk = k_dim // tk
    n_extra, n_out = len(extras), len(out_dtypes)
    n_lhs = 4 if second else 2
    dims = (((0 if ta else 1,), (1 if tb else 0,)), ((), ()))
    if epilogue is None:
        epilogue = lambda acc: (acc,) * n_out

    def body(*refs):
        operand_refs, rest = refs[:n_lhs], refs[n_lhs:]
        extra_refs, out_refs = rest[:n_extra], rest[n_extra:n_extra + n_out]

        def product():
            total = None
            for a_ref, b_ref in zip(operand_refs[0::2], operand_refs[1::2]):
                part = lax.dot_general(a_ref[...].astype(BF16), b_ref[...].astype(BF16), dims, preferred_element_type=F32)
                total = part if total is None else total + part
            return total

        def finish(acc):
            outs = epilogue(acc, *[r[...] for r in extra_refs])
            for o_ref, o in zip(out_refs, outs):
                o_ref[...] = o.astype(o_ref.dtype)

        if nk == 1:
            finish(product())
            return
        acc_ref = rest[-1]
        k = pl.program_id(2)

        @pl.when(k == 0)
        def _():
            acc_ref[...] = product()

        @pl.when((k > 0) & (k < nk - 1))
        def _():
            acc_ref[...] += product()

        @pl.when(k == nk - 1)
        def _():
            finish(acc_ref[...] + product())

    a_spec = pl.BlockSpec((tk, tm), lambda i, j, k: (k, i)) if ta else pl.BlockSpec((tm, tk), lambda i, j, k: (i, k))
    if b_stack and tb:
        per = n_shard // tk
        b_spec = pl.BlockSpec((None, tn, tk), lambda i, j, k: (k // per, j, k % per))
    elif b_stack:
        per = n_shard // tn
        b_spec = pl.BlockSpec((None, tk, tn), lambda i, j, k: (j // per, k, j % per))
    elif tb:
        b_spec = pl.BlockSpec((tn, tk), lambda i, j, k: (j, k))
    else:
        b_spec = pl.BlockSpec((tk, tn), lambda i, j, k: (k, j))
    if out_stack:
        per_out = (n_dim // out_stack) // tn
        out_spec = pl.BlockSpec((None, tm, tn), lambda i, j, k: (j // per_out, i, j % per_out))
        out_dims = (out_stack, m_dim, n_dim // out_stack)
    else:
        out_spec = pl.BlockSpec((tm, tn), lambda i, j, k: (i, j))
        out_dims = (m_dim, n_dim)
    extra_specs = []
    for arr, kind in extras:
        if kind == "tile":
            extra_specs.append(pl.BlockSpec((tm, tn), lambda i, j, k: (i, j)))
        elif isinstance(kind, tuple):
            extra_specs.append(pl.BlockSpec((tm, tn), functools.partial(lambda i, j, k, c: (i, j + c), c=kind[1])))
        elif kind == "row":
            extra_specs.append(pl.BlockSpec((1, tn), lambda i, j, k: (0, j)))
        elif kind == "col":
            extra_specs.append(pl.BlockSpec((tm, 1), lambda i, j, k: (i, 0)))
        else:
            assert kind == "table", kind
            extra_specs.append(pl.BlockSpec((tm, LANES), lambda i, j, k: (i, 0)))
    return pl.pallas_call(
        body,
        name=name,
        grid=(m_dim // tm, n_dim // tn, nk),
        in_specs=[a_spec, b_spec] * (n_lhs // 2) + extra_specs,
        out_specs=[out_spec for _ in out_dtypes],
        out_shape=[jax.ShapeDtypeStruct(out_dims, dt) for dt in out_dtypes],
        scratch_shapes=[pltpu.VMEM((tm, tn), F32)] if nk > 1 else [],
        compiler_params=_params(("parallel", "parallel", "arbitrary")),
    )(a, b, *(second or ()), *[arr for arr, _ in extras])


def _mm_sum(pieces, b, offsets, *, name, tm=1024, tn=1024):
    m_dim, n_dim = pieces[0].shape[0], b.shape[1]
    tm, tn = _near_tile(m_dim, tm), _near_tile(n_dim, tn)
    n_pieces = len(pieces)

    def body(*refs):
        total = None
        for a_ref, b_ref in zip(refs[:n_pieces], refs[n_pieces:2 * n_pieces]):
            part = jnp.dot(a_ref[...], b_ref[...], preferred_element_type=F32)
            total = part if total is None else total + part
        refs[-1][...] = total

    a_specs = [pl.BlockSpec((tm, p.shape[1]), lambda i, j: (i, 0)) for p in pieces]
    b_specs = [pl.BlockSpec((p.shape[1], tn), functools.partial(lambda i, j, blk: (blk, j), blk=off // p.shape[1]))
               for p, off in zip(pieces, offsets)]
    return pl.pallas_call(
        body, name=name, grid=(m_dim // tm, n_dim // tn),
        in_specs=a_specs + b_specs,
        out_specs=pl.BlockSpec((tm, tn), lambda i, j: (i, j)),
        out_shape=jax.ShapeDtypeStruct((m_dim, n_dim), F32),
        compiler_params=_params(("parallel", "parallel")),
    )(*pieces, *[b] * n_pieces)


def _rows(s):
    return min(512, s)


def _rope_tables(pos_col, inv_freq_lanes, *, name):
    s = pos_col.shape[0]
    tb = _rows(s)

    def body(pos_ref, f_ref, cos_ref, sin_ref):
        ang = pos_ref[...].astype(F32) * f_ref[...]
        lane = lax.broadcasted_iota(jnp.int32, ang.shape, 1)
        on = (lane >= QK_NOPE) & (lane < QK_DIM)
        cos_ref[...] = jnp.where(on, jnp.cos(ang), 0.0)
        sin_ref[...] = jnp.where(on, jnp.sin(ang), 0.0)

    return pl.pallas_call(
        body, name=name, grid=(s // tb,),
        in_specs=[pl.BlockSpec((tb, 1), lambda i: (i, 0)), pl.BlockSpec((1, LANES), lambda i: (0, 0))],
        out_specs=[pl.BlockSpec((tb, LANES), lambda i: (i, 0))] * 2,
        out_shape=[jax.ShapeDtypeStruct((s, LANES), F32)] * 2,
        compiler_params=_params(("parallel",)),
    )(pos_col, inv_freq_lanes)


def _rotate_half(x):
    lane = lax.broadcasted_iota(jnp.int32, x.shape, 1)
    half = QK_ROPE // 2
    first = (lane >= QK_NOPE) & (lane < QK_NOPE + half)
    second = (lane >= QK_NOPE + half) & (lane < QK_DIM)
    return jnp.where(first, -pltpu.roll(x, LANES - half, 1), jnp.where(second, pltpu.roll(x, half, 1), 0.0))


def _norm_mod(x, g, sc, sh, *, name):
    s, d = x.shape
    tb = _rows(s)

    def body(x_ref, g_ref, sc_ref, sh_ref, h_ref, r_ref):
        xv = x_ref[...]
        r = lax.rsqrt(jnp.mean(xv * xv, axis=-1, keepdims=True) + EPS)
        r_ref[...] = r
        h_ref[...] = (((xv * r) * g_ref[...]) * (1.0 + sc_ref[...]) + sh_ref[...]).astype(BF16)

    vec = pl.BlockSpec((1, d), lambda i: (0, 0))
    return pl.pallas_call(
        body, name=name, grid=(s // tb,),
        in_specs=[pl.BlockSpec((tb, d), lambda i: (i, 0)), vec, vec, vec],
        out_specs=[pl.BlockSpec((tb, d), lambda i: (i, 0)), pl.BlockSpec((tb, 1), lambda i: (i, 0))],
        out_shape=[jax.ShapeDtypeStruct((s, d), BF16), jax.ShapeDtypeStruct((s, 1), F32)],
        compiler_params=_params(("parallel",)),
    )(x, g, sc, sh)


def _window_sums(ext, sign):
    n = ext.shape[0]
    sums, cur, k = [], ext, 1
    for _ in POOL_WINDOWS:
        cur = cur + pltpu.roll(cur, k if sign > 0 else n - k, 0)
        sums.append(cur)
        k *= 2
    return sums


def _mixer_pre(z, cos_t, sin_t, w_pool, pool_scale, gq, gkv, *, name):
    s = z.shape[0]
    tb = _rows(s)
    hb = tb // HALO

    def body(zcq_ref, zkr_ref, zu_ref, zuh_ref, zckv_ref, cos_ref, sin_ref, wp_ref, ps_ref, gq_ref, gkv_ref,
             p_ref, yp_ref, cq_ref, ckv_ref, kr_ref, rq_ref, rkv_ref):
        i = pl.program_id(0)
        u = zu_ref[...].astype(F32)
        halo = jnp.where(i > 0, zuh_ref[...].astype(F32), 0.0)
        ext = jnp.concatenate([halo, u], axis=0)
        t = i * tb + lax.broadcasted_iota(jnp.int32, (tb, 1), 0)
        for g, (w, sw) in enumerate(zip(POOL_WINDOWS, _window_sums(ext, +1))):
            cols = slice(g * POOL_GROUP, (g + 1) * POOL_GROUP)
            cnt = jnp.minimum(t + 1, w).astype(F32)
            pg = (sw[HALO:, cols] / cnt - u[:, cols]).astype(BF16)
            p_ref[:, cols] = pg
            yg = jnp.dot(pg, wp_ref[g].astype(BF16), preferred_element_type=F32)
            yp_ref[:, cols] = (yg * ps_ref[:, cols]).astype(BF16)

        def rms(x_ref, g_ref, out_ref, r_ref):
            xv = x_ref[...].astype(F32)
            r = lax.rsqrt(jnp.mean(xv * xv, axis=-1, keepdims=True) + EPS)
            r_ref[...] = r
            out_ref[...] = ((xv * r) * g_ref[...]).astype(BF16)

        rms(zcq_ref, gq_ref, cq_ref, rq_ref)
        rms(zckv_ref, gkv_ref, ckv_ref, rkv_ref)
        kr = zkr_ref[...].astype(F32)
        kr_ref[...] = (kr * cos_ref[...] + _rotate_half(kr) * sin_ref[...]).astype(BF16)

    def zcol(width, off):
        return pl.BlockSpec((tb, width), lambda i: (i, off // width))

    def full(a):
        return pl.BlockSpec(a.shape, lambda i: (0,) * a.ndim)

    def out(width, dt):
        return pl.BlockSpec((tb, width), lambda i: (i, 0)), jax.ShapeDtypeStruct((s, width), dt)

    outs = [out(POOL_DIM, BF16), out(POOL_DIM, BF16), out(Q_LORA, BF16), out(KV_LORA, BF16), out(LANES, BF16),
            out(1, F32), out(1, F32)]
    return pl.pallas_call(
        body, name=name, grid=(s // tb,),
        in_specs=[zcol(Q_LORA, ZC_CQ), zcol(LANES, ZC_KR), zcol(POOL_DIM, ZC_U),
                  pl.BlockSpec((HALO, POOL_DIM), lambda i: (jnp.maximum(i * hb - 1, 0), ZC_U // POOL_DIM)),
                  zcol(KV_LORA, ZC_CKV),
                  pl.BlockSpec((tb, LANES), lambda i: (i, 0)), pl.BlockSpec((tb, LANES), lambda i: (i, 0)),
                  full(w_pool), full(pool_scale), full(gq), full(gkv)],
        out_specs=[o[0] for o in outs], out_shape=[o[1] for o in outs],
        compiler_params=_params(("parallel",)),
    )(z, z, z, z, z, cos_t, sin_t, w_pool, pool_scale, gq, gkv)


def _sigmoid(x):
    return 1.0 / (1.0 + jnp.exp(-x.astype(F32)))


ATTN_SCALE = 1.0 / math.sqrt(QK_DIM)
NEG_BIG = -1e30


LOG2_E = math.log2(math.e)
EXP2_SCALE = ATTN_SCALE * LOG2_E
NT_DIMS = (((1,), (1,)), ((), ()))
TN_DIMS = (((0,), (0,)), ((), ()))


def _on_or_below_diagonal(t):
    return lax.broadcasted_iota(jnp.int32, (t, t), 0) >= lax.broadcasted_iota(jnp.int32, (t, t), 1)


HEADS_PER_STEP = 2
HEAD_COLS = [slice(g * HEAD_PAD, (g + 1) * HEAD_PAD) for g in range(HEADS_PER_STEP)]


def _attn_fwd(q, k, v, *, name):
    s = q.shape[0]
    t = _rows(s)
    wide = HEADS_PER_STEP * HEAD_PAD

    def body(q_ref, k_ref, v_ref, o_ref, lse_ref):
        qi = pl.program_id(1)
        qs = [q_ref[:, cols] for cols in HEAD_COLS]

        def block(j, carry, diagonal):
            rows = pl.ds(pl.multiple_of(j * t, t), t)
            out = []
            for qv, cols, (m, l, acc) in zip(qs, HEAD_COLS, carry):
                sc = lax.dot_general(qv, k_ref[rows, cols], NT_DIMS, preferred_element_type=F32)
                if diagonal:
                    sc = jnp.where(_on_or_below_diagonal(t), sc, NEG_BIG)
                m_new = jnp.maximum(m, jnp.max(sc, axis=-1, keepdims=True))
                p = jnp.exp2((sc - m_new) * EXP2_SCALE)
                alpha = jnp.exp2((m - m_new) * EXP2_SCALE)
                l = alpha * l + jnp.sum(p, axis=-1, keepdims=True)
                acc = alpha * acc + jnp.dot(p.astype(BF16), v_ref[rows, cols], preferred_element_type=F32)
                out.append((m_new, l, acc))
            return tuple(out)

        init = tuple((jnp.full((t, 1), -jnp.inf, F32), jnp.zeros((t, 1), F32), jnp.zeros((t, HEAD_PAD), F32))
                     for _ in HEAD_COLS)
        carry = lax.fori_loop(0, qi, lambda j, c: block(j, c, False), init)
        for g, (cols, (m, l, acc)) in enumerate(zip(HEAD_COLS, block(qi, carry, True))):
            o_ref[:, cols] = (acc / l).astype(BF16)
            lse_ref[g] = m * ATTN_SCALE + jnp.log(l)

    q_spec = pl.BlockSpec((t, wide), lambda h, i: (i, h))
    kv_spec = pl.BlockSpec((s, wide), lambda h, i: (0, h))
    return pl.pallas_call(
        body, name=name, grid=(N_HEADS // HEADS_PER_STEP, s // t),
        in_specs=[q_spec, kv_spec, kv_spec],
        out_specs=[q_spec, pl.BlockSpec((HEADS_PER_STEP, t, 1), lambda h, i: (h, i, 0))],
        out_shape=[jax.ShapeDtypeStruct((s, N_HEADS * HEAD_PAD), BF16), jax.ShapeDtypeStruct((N_HEADS, s, 1), F32)],
        compiler_params=_params(("parallel", "parallel")),
    )(q, k, v)


def _attn_bwd(q, k, v, do, o, lse, cos_t, sin_t, *, name):
    s = q.shape[0]
    t = _rows(s)
    nt = s // t

    def body(q_ref, k_ref, v_ref, do_ref, o_ref, lse_ref, cos_ref, sin_ref, dql_ref, dk_ref, dv_ref, dks_ref,
             dq_ref, dl_ref):
        kj = pl.program_id(1)

        @pl.when(kj == 0)
        def _():
            dq_ref[...] = jnp.zeros_like(dq_ref)

            def delta(i, carry):
                rows = pl.ds(pl.multiple_of(i * t, t), t)
                for g, cols in enumerate(HEAD_COLS):
                    dl_ref[g, rows, :] = jnp.sum(do_ref[rows, cols].astype(F32) * o_ref[rows, cols].astype(F32),
                                                 axis=-1, keepdims=True)
                return carry

            lax.fori_loop(0, nt, delta, 0)

        kvs = [(k_ref[:, cols], v_ref[:, cols]) for cols in HEAD_COLS]

        def block(i, carry, diagonal):
            rows = pl.ds(pl.multiple_of(i * t, t), t)
            out = []
            for g, (cols, (kv, vv), (dk, dv)) in enumerate(zip(HEAD_COLS, kvs, carry)):
                qv, dov = q_ref[rows, cols], do_ref[rows, cols]
                sc = lax.dot_general(qv, kv, NT_DIMS, preferred_element_type=F32)
                p = jnp.exp2(sc * EXP2_SCALE - lse_ref[g, rows, :] * LOG2_E)
                if diagonal:
                    p = jnp.where(_on_or_below_diagonal(t), p, 0.0)
                dp = lax.dot_general(dov, vv, NT_DIMS, preferred_element_type=F32)
                ds = (p * (dp - dl_ref[g, rows, :])).astype(BF16)
                dv = dv + lax.dot_general(p.astype(BF16), dov, TN_DIMS, preferred_element_type=F32)
                dk = dk + lax.dot_general(ds, qv, TN_DIMS, preferred_element_type=F32)
                dq_ref[rows, cols] += jnp.dot(ds, kv, preferred_element_type=F32) * ATTN_SCALE
                out.append((dk, dv))
            return tuple(out)

        zero = jnp.zeros((t, HEAD_PAD), F32)
        carry = block(kj, tuple((zero, zero) for _ in HEAD_COLS), True)
        dk_sum = None
        for cols, (dk, dv) in zip(HEAD_COLS, lax.fori_loop(kj + 1, nt, lambda i, c: block(i, c, False), carry)):
            dk = dk * ATTN_SCALE
            dk_ref[:, cols] = dk.astype(BF16)
            dv_ref[:, cols] = dv.astype(BF16)
            dk_sum = dk if dk_sum is None else dk_sum + dk
        dks_ref[...] = dk_sum

        @pl.when(kj == nt - 1)
        def _():
            def rope_bwd(i, carry):
                rows = pl.ds(pl.multiple_of(i * t, t), t)
                sin = sin_ref[rows, :]
                lane = lax.broadcasted_iota(jnp.int32, sin.shape, 1)
                cos_q = cos_ref[rows, :] + jnp.where(lane < QK_NOPE, 1.0, 0.0)
                for cols in HEAD_COLS:
                    dqv = dq_ref[rows, cols]
                    dql_ref[rows, cols] = (dqv * cos_q - _rotate_half(dqv * sin)).astype(BF16)
                return carry

            lax.fori_loop(0, nt, rope_bwd, 0)

    heads_wide = HEADS_PER_STEP * HEAD_PAD
    full_spec = pl.BlockSpec((s, heads_wide), lambda h, j: (0, h))
    kv_spec = pl.BlockSpec((t, heads_wide), lambda h, j: (j, h))
    vec_spec = pl.BlockSpec((HEADS_PER_STEP, s, 1), lambda h, j: (h, 0, 0))
    table_spec = pl.BlockSpec((s, LANES), lambda h, j: (0, 0))
    wide = jax.ShapeDtypeStruct((s, N_HEADS * HEAD_PAD), BF16)
    n_steps = N_HEADS // HEADS_PER_STEP
    return pl.pallas_call(
        body, name=name, grid=(n_steps, nt),
        in_specs=[full_spec, kv_spec, kv_spec, full_spec, full_spec, vec_spec, table_spec, table_spec],
        out_specs=[full_spec, kv_spec, kv_spec, pl.BlockSpec((None, t, HEAD_PAD), lambda h, j: (h, j, 0))],
        out_shape=[wide, wide, wide, jax.ShapeDtypeStruct((n_steps, s, HEAD_PAD), F32)],
        scratch_shapes=[pltpu.VMEM((s, heads_wide), F32), pltpu.VMEM((HEADS_PER_STEP, s, 1), F32)],
        compiler_params=_params(("parallel", "arbitrary")),
    )(q, k, v, do, o, lse, cos_t, sin_t)


def _acc_specs(widths):
    return ([pl.BlockSpec((1, w), lambda i: (0, 0)) for w in widths],
            [jax.ShapeDtypeStruct((1, w), F32) for w in widths])


def _gate_grads(dxv, m_ref, gate_ref, dm_ref, dgate_ref):
    dm_ref[...] = (dxv * gate_ref[...]).astype(BF16)
    dgate_ref[...] += jnp.sum(dxv * m_ref[...], axis=0, keepdims=True)


def _final_loss(x, g, target, m, gate, *, name):
    s, d = x.shape
    tb = _rows(s)

    def body(x_ref, g_ref, t_ref, m_ref, gate_ref, dx_ref, loss_ref, dg_ref, dm_ref, dgate_ref):
        @pl.when(pl.program_id(0) == 0)
        def _():
            loss_ref[...] = jnp.zeros_like(loss_ref)
            dg_ref[...] = jnp.zeros_like(dg_ref)
            dgate_ref[...] = jnp.zeros_like(dgate_ref)

        xv = x_ref[...]
        r = lax.rsqrt(jnp.mean(xv * xv, axis=-1, keepdims=True) + EPS)
        xn = xv * r
        err = xn * g_ref[...] - t_ref[...]
        loss_ref[...] += 0.5 * jnp.sum(jnp.mean(err * err, axis=-1, keepdims=True), axis=0, keepdims=True)
        dy = err / d
        dg_ref[...] += jnp.sum(dy * xn, axis=0, keepdims=True)
        dxn = dy * g_ref[...]
        dxv = r * (dxn - xn * jnp.mean(dxn * xn, axis=-1, keepdims=True))
        dx_ref[...] = dxv
        _gate_grads(dxv, m_ref, gate_ref, dm_ref, dgate_ref)

    blk = pl.BlockSpec((tb, d), lambda i: (i, 0))
    vec = pl.BlockSpec((1, d), lambda i: (0, 0))
    acc_specs, acc_shapes = _acc_specs((LANES, d))
    return pl.pallas_call(
        body, name=name, grid=(s // tb,),
        in_specs=[blk, vec, blk, blk, vec],
        out_specs=[blk] + acc_specs + [blk, vec],
        out_shape=[jax.ShapeDtypeStruct((s, d), F32)] + acc_shapes + [jax.ShapeDtypeStruct((s, d), BF16),
                                                                     jax.ShapeDtypeStruct((1, d), F32)],
        compiler_params=_params(("arbitrary",)),
    )(x, g, target, m, gate)


def _norm_mod_bwd(dh, x, r, g, sc, dx_skip, *, name, gate=None):
    s, d = x.shape
    tb = _rows(s)
    nb = s // tb
    n_gate = 2 if gate else 0

    def body(dh_ref, x_ref, r_ref, g_ref, sc_ref, skip_ref, *rest):
        gate_refs, (dx_ref, dg_ref, dsc_ref, dsh_ref) = rest[:n_gate], rest[n_gate:n_gate + 4]
        gate_outs, da_sc = rest[n_gate + 4:-1], rest[-1]
        i = pl.program_id(0)

        @pl.when(i == 0)
        def _():
            da_sc[...] = jnp.zeros_like(da_sc)
            dsh_ref[...] = jnp.zeros_like(dsh_ref)
            if gate:
                gate_outs[1][...] = jnp.zeros_like(gate_outs[1])

        dhv, rv = dh_ref[...], r_ref[...]
        xn = x_ref[...] * rv
        dsh_ref[...] += jnp.sum(dhv, axis=0, keepdims=True)
        da_sc[...] += jnp.sum(dhv * xn, axis=0, keepdims=True)
        dxn = dhv * (g_ref[...] * (1.0 + sc_ref[...]))
        dxv = skip_ref[...] + rv * (dxn - xn * jnp.mean(dxn * xn, axis=-1, keepdims=True))
        dx_ref[...] = dxv
        if gate:
            _gate_grads(dxv, *gate_refs, *gate_outs)

        @pl.when(i == nb - 1)
        def _():
            dg_ref[...] = da_sc[...] * (1.0 + sc_ref[...])
            dsc_ref[...] = da_sc[...] * g_ref[...]

    blk = pl.BlockSpec((tb, d), lambda i: (i, 0))
    vec = pl.BlockSpec((1, d), lambda i: (0, 0))
    acc_specs, acc_shapes = _acc_specs((d, d, d))
    gate_specs = [blk, vec] if gate else []
    gate_shapes = [jax.ShapeDtypeStruct((s, d), BF16), jax.ShapeDtypeStruct((1, d), F32)] if gate else []
    return pl.pallas_call(
        body, name=name, grid=(nb,),
        in_specs=[blk, blk, pl.BlockSpec((tb, 1), lambda i: (i, 0)), vec, vec, blk] + gate_specs,
        out_specs=[blk] + acc_specs + gate_specs,
        out_shape=[jax.ShapeDtypeStruct((s, d), F32)] + acc_shapes + gate_shapes,
        scratch_shapes=[pltpu.VMEM((1, d), F32)],
        compiler_params=_params(("arbitrary",)),
    )(dh, x, r, g, sc, dx_skip, *(gate or ()))


def _pool_bwd(dyp, p, w_pool, pool_scale, *, name):
    s = dyp.shape[0]
    tb = _rows(s)
    nb = s // tb
    hb = tb // HALO
    nt_dims = (((1,), (1,)), ((), ()))
    tn_dims = (((0,), (0,)), ((), ()))

    def body(dy_ref, dyn_ref, p_ref, wp_ref, ps_ref, du_ref, gwp_ref, gps_ref):
        i = pl.program_id(0)

        @pl.when(i == 0)
        def _():
            gwp_ref[...] = jnp.zeros_like(gwp_ref)
            gps_ref[...] = jnp.zeros_like(gps_ref)

        cur = dy_ref[...]
        nxt = jnp.where(i < nb - 1, dyn_ref[...], 0.0)
        dpw = (jnp.concatenate([cur, nxt], axis=0) * ps_ref[...]).astype(BF16)
        t = i * tb + lax.broadcasted_iota(jnp.int32, (tb + HALO, 1), 0)
        for g, w in enumerate(POOL_WINDOWS):
            cols = slice(g * POOL_GROUP, (g + 1) * POOL_GROUP)
            wg = wp_ref[g].astype(BF16)
            dp = lax.dot_general(dpw[:, cols], wg, nt_dims, preferred_element_type=F32)
            e = dp / jnp.minimum(t + 1, w).astype(F32)
            lead = _window_sums(e, -1)[g]
            du_ref[:, cols] = (lead[:tb] - dp[:tb]).astype(BF16)
            pg = p_ref[:, cols]
            pw = jnp.dot(pg, wg, preferred_element_type=F32)
            gps_ref[:, cols] += jnp.sum(cur[:, cols] * pw, axis=0, keepdims=True)
            gwp_ref[g] += lax.dot_general(pg, dpw[:tb, cols], tn_dims, preferred_element_type=F32)

    blk = pl.BlockSpec((tb, POOL_DIM), lambda i: (i, 0))
    return pl.pallas_call(
        body, name=name, grid=(nb,),
        in_specs=[blk, pl.BlockSpec((HALO, POOL_DIM), lambda i: (jnp.minimum((i + 1) * hb, s // HALO - 1), 0)), blk,
                  pl.BlockSpec(w_pool.shape, lambda i: (0, 0, 0)), pl.BlockSpec((1, POOL_DIM), lambda i: (0, 0))],
        out_specs=[blk, pl.BlockSpec(w_pool.shape, lambda i: (0, 0, 0)), pl.BlockSpec((1, POOL_DIM), lambda i: (0, 0))],
        out_shape=[jax.ShapeDtypeStruct((s, POOL_DIM), BF16), jax.ShapeDtypeStruct(w_pool.shape, F32),
                   jax.ShapeDtypeStruct((1, POOL_DIM), F32)],
        compiler_params=_params(("arbitrary",)),
    )(dyp, dyp, p, w_pool, pool_scale)


def _key_bwd(dk_sums, cos_t, sin_t, *, name):
    n, s, _ = dk_sums.shape
    tb = _rows(s)

    def body(dk_ref, cos_ref, sin_ref, dkr_ref):
        tot = dk_ref[0]
        for h in range(1, n):
            tot = tot + dk_ref[h]
        dkr_ref[...] = (tot * cos_ref[...] - _rotate_half(tot * sin_ref[...])).astype(BF16)

    tab = pl.BlockSpec((tb, LANES), lambda i: (i, 0))
    return pl.pallas_call(
        body, name=name, grid=(s // tb,),
        in_specs=[pl.BlockSpec((n, tb, LANES), lambda i: (0, i, 0)), tab, tab], out_specs=tab,
        out_shape=jax.ShapeDtypeStruct((s, LANES), BF16),
        compiler_params=_params(("parallel",)),
    )(dk_sums, cos_t, sin_t)


def _rms_bwd(dy, z, z_off, r, g, *, name):
    s, n = dy.shape
    tb = _rows(s)

    def body(dy_ref, x_ref, r_ref, g_ref, dx_ref, dg_ref):
        @pl.when(pl.program_id(0) == 0)
        def _():
            dg_ref[...] = jnp.zeros_like(dg_ref)

        dyv, rv = dy_ref[...], r_ref[...]
        xn = x_ref[...].astype(F32) * rv
        dg_ref[...] += jnp.sum(dyv * xn, axis=0, keepdims=True)
        dxn = dyv * g_ref[...]
        dx_ref[...] = (rv * (dxn - xn * jnp.mean(dxn * xn, axis=-1, keepdims=True))).astype(BF16)

    blk = pl.BlockSpec((tb, n), lambda i: (i, 0))
    acc_specs, acc_shapes = _acc_specs((n,))
    return pl.pallas_call(
        body, name=name, grid=(s // tb,),
        in_specs=[blk, pl.BlockSpec((tb, n), lambda i: (i, z_off // n)), pl.BlockSpec((tb, 1), lambda i: (i, 0)),
                  pl.BlockSpec((1, n), lambda i: (0, 0))],
        out_specs=[blk] + acc_specs, out_shape=[jax.ShapeDtypeStruct((s, n), BF16)] + acc_shapes,
        compiler_params=_params(("arbitrary",)),
    )(dy, z, r, g)


def _sum_slots(a, n, *, name, out_dtype=F32):
    _, rows, cols = a.shape
    tr = _tile(rows, 256, 8)

    def body(a_ref, out_ref):
        tot = a_ref[0].astype(F32)
        for j in range(1, n):
            tot = tot + a_ref[j].astype(F32)
        out_ref[...] = tot.astype(out_dtype)

    return pl.pallas_call(
        body, name=name, grid=(rows // tr,),
        in_specs=[pl.BlockSpec((n, tr, cols), lambda i: (0, i, 0))],
        out_specs=pl.BlockSpec((tr, cols), lambda i: (i, 0)),
        out_shape=jax.ShapeDtypeStruct((rows, cols), out_dtype),
        compiler_params=_params(("parallel",)),
    )(a)


def _add2_stacked(a, b, stacked, l, *, name):
    rows, cols = a.shape
    tr = _tile(rows, 256, 8)

    def body(a_ref, b_ref, *rest):
        rest[-1][...] = a_ref[...] + b_ref[...]

    blk = pl.BlockSpec((tr, cols), lambda i: (i, 0))
    carried = [] if stacked is None else [stacked]
    return pl.pallas_call(
        body, name=name, grid=(rows // tr,),
        in_specs=[blk, blk] + [pl.BlockSpec(memory_space=pl.ANY) for _ in carried],
        out_specs=pl.BlockSpec((None, tr, cols), lambda i: (l, i, 0)),
        out_shape=jax.ShapeDtypeStruct((DEPTH, rows, cols), F32),
        input_output_aliases={2: 0} if carried else {},
        compiler_params=_params(("parallel",)),
    )(a, b, *carried)


def _adamw(w, g, m, v, *, name):
    shape = w.shape
    if w.ndim == 2:
        w, g, m, v = (a.reshape((1,) + shape) for a in (w, g, m, v))
    layers, rows, cols = w.shape
    tr = _tile(rows, max(8, (1 << 18) // cols), 8)
    c1 = 1.0 - ADAM_B1 ** ADAM_STEP
    c2 = 1.0 - ADAM_B2 ** ADAM_STEP

    def body(w_ref, g_ref, m_ref, v_ref, d_ref, nm_ref, nv_ref):
        gv = g_ref[...]
        nm = ADAM_B1 * m_ref[...] + (1.0 - ADAM_B1) * gv
        nv = ADAM_B2 * v_ref[...] + (1.0 - ADAM_B2) * (gv * gv)
        nm_ref[...] = nm
        nv_ref[...] = nv
        d_ref[...] = -ADAM_LR * ((nm / c1) / (jnp.sqrt(nv / c2) + ADAM_EPS) + ADAM_WD * w_ref[...])

    blk = pl.BlockSpec((None, tr, cols), lambda l, i: (l, i, 0))
    outs = pl.pallas_call(
        body, name=name, grid=(layers, rows // tr), in_specs=[blk] * 4, out_specs=[blk] * 3,
        out_shape=[jax.ShapeDtypeStruct((layers, rows, cols), F32)] * 3,
        compiler_params=_params(("parallel", "parallel")),
    )(w, g, m, v)
    return [o.reshape(shape) for o in outs]


def _adamw_summed(w, halves, m, v, *, name):
    layers, rows, cols = w.shape
    tr = _tile(rows, max(8, (1 << 18) // cols), 8)
    tc = _tile(cols, max(LANES, (1 << 18) // tr))
    nb = rows // tr
    c1 = 1.0 - ADAM_B1 ** ADAM_STEP
    c2 = 1.0 - ADAM_B2 ** ADAM_STEP

    def body(w_ref, m_ref, v_ref, *rest):
        half_refs, (g_ref, d_ref, nm_ref, nv_ref) = rest[:2 * layers], rest[2 * layers:]
        for k in range(layers):
            @pl.when(pl.program_id(0) == k)
            def _(k=k):
                gv = half_refs[2 * k][...] + half_refs[2 * k + 1][...]
                g_ref[...] = gv
                nm = ADAM_B1 * m_ref[...] + (1.0 - ADAM_B1) * gv
                nv = ADAM_B2 * v_ref[...] + (1.0 - ADAM_B2) * (gv * gv)
                nm_ref[...] = nm
                nv_ref[...] = nv
                d_ref[...] = -ADAM_LR * ((nm / c1) / (jnp.sqrt(nv / c2) + ADAM_EPS) + ADAM_WD * w_ref[...])

    def half_spec(k):
        def index(l, i, j):
            at_k = l == k
            return (jnp.where(at_k, i, jnp.where(l < k, 0, nb - 1)), jnp.where(at_k, j, jnp.where(l < k, 0, cols // tc - 1)))
        return pl.BlockSpec((tr, tc), index)

    blk = pl.BlockSpec((None, tr, tc), lambda l, i, j: (l, i, j))
    return pl.pallas_call(
        body, name=name, grid=(layers, nb, cols // tc),
        in_specs=[blk] * 3 + [half_spec(k) for k in range(layers) for _ in range(2)], out_specs=[blk] * 4,
        out_shape=[jax.ShapeDtypeStruct((layers, rows, cols), F32)] * 4,
        compiler_params=_params(("parallel", "parallel", "parallel")),
    )(w, m, v, *[half for pair in halves for half in pair])


def _coords():
    return lax.axis_index("x"), lax.axis_index("y"), lax.axis_index("c")


def _other_chips(x, y):
    return [(1 - x, y), (x, 1 - y), (1 - x, 1 - y)]


def _gather_rows(x_ref, out_ref, send_sems, recv_sems, local_sem, m_per):
    x, y, c = _coords()
    me, sibling = (x, y, c), (x, y, 1 - c)
    chips = _other_chips(x, y)

    def rows(px, py, pc):
        return out_ref.at[pl.ds((4 * px + 2 * py + pc) * m_per, m_per), :]

    def copy(k, block, to, src=None):
        return pltpu.make_async_remote_copy(
            src_ref=rows(*block) if src is None else src, dst_ref=rows(*block),
            send_sem=send_sems.at[k], recv_sem=recv_sems.at[k], device_id=to, device_id_type=MESH)

    mine = pltpu.make_async_copy(x_ref, rows(*me), local_sem)
    mine.start()
    first = [copy(0, me, sibling, src=x_ref)]
    first += [copy(1 + j, me, (*chip, c), src=x_ref) for j, chip in enumerate(chips)]
    for cp in first:
        cp.start()
    passed = [copy(4 + j, (*chip, c), sibling) for j, chip in enumerate(chips)]
    for j, chip in enumerate(chips):
        copy(1 + j, (*chip, c), me).wait_recv()
        passed[j].start()
    copy(0, sibling, me).wait_recv()
    for j, chip in enumerate(chips):
        copy(4 + j, (*chip, 1 - c), me).wait_recv()
    for cp in first + passed:
        cp.wait_send()
    mine.wait()


def _ada_modulation(c_blk, w_ada, b_mine, *, name):
    depth, _, cols = w_ada.shape
    blk_rows = c_blk.shape[0]
    sems = [pltpu.SemaphoreType.DMA((7,)), pltpu.SemaphoreType.DMA((7,)), pltpu.SemaphoreType.DMA]

    def body(c_ref, w_ref, b_ref, act_ref, mod_ref, c_all, prod_all, mod_mine, *sem_refs):
        _gather_rows(c_ref, c_all, *sem_refs[:3], blk_rows)
        cv = c_all[...]
        c_all[...] = cv * _sigmoid(cv)
        lhs = c_all[...].astype(BF16)
        for d in range(N_DEV):
            act_ref[d:d + 1, :] = c_all[d * blk_rows:d * blk_rows + 1, :]
        for l in range(depth):
            prod_all[...] = (jnp.dot(lhs, w_ref[l].astype(BF16), preferred_element_type=F32)
                             + b_ref[:, l * cols:(l + 1) * cols])
            for d in range(N_DEV):
                mod_mine[l * N_DEV + d:l * N_DEV + d + 1, :] = prod_all[d * blk_rows:d * blk_rows + 1, :]
        _gather_rows(mod_mine, mod_ref, *sem_refs[3:], depth * N_DEV)

    vmem = pl.BlockSpec(memory_space=pltpu.VMEM)
    act, mod = pl.pallas_call(
        body, name=name,
        out_shape=[jax.ShapeDtypeStruct((N_DEV, D_MODEL), F32), jax.ShapeDtypeStruct((N_DEV * depth * N_DEV, cols), F32)],
        in_specs=[vmem, vmem, vmem], out_specs=[vmem, vmem],
        scratch_shapes=[pltpu.VMEM((N_DEV * blk_rows, D_MODEL), F32), pltpu.VMEM((N_DEV * blk_rows, cols), F32),
                        pltpu.VMEM((depth * N_DEV, cols), F32)] + sems + sems,
        compiler_params=_params(),
    )(c_blk, w_ada, b_mine)
    return act.astype(BF16), mod


def _all_gather_small(blk, *, name, swaps=()):
    m_per, n = blk.shape
    n_swaps = len(swaps)

    def body(x_ref, *refs):
        swap_srcs, out_ref, swap_outs = refs[:n_swaps], refs[n_swaps], refs[n_swaps + 1:2 * n_swaps + 1]
        send_sems, recv_sems, local_sem = refs[2 * n_swaps + 1:2 * n_swaps + 4]
        x, y, c = _coords()
        swapping = [pltpu.make_async_remote_copy(src_ref=src, dst_ref=dst, send_sem=refs[-2].at[k], recv_sem=refs[-1].at[k],
                                                 device_id=(x, y, 1 - c), device_id_type=MESH)
                    for k, (src, dst) in enumerate(zip(swap_srcs, swap_outs))]
        for cp in swapping:
            cp.start()
        _gather_rows(x_ref, out_ref, send_sems, recv_sems, local_sem, m_per)
        for cp in swapping:
            cp.wait()

    any_spec = pl.BlockSpec(memory_space=pl.ANY)
    outs = pl.pallas_call(
        body, name=name,
        out_shape=[jax.ShapeDtypeStruct((N_DEV * m_per, n), blk.dtype)] + [jax.ShapeDtypeStruct(a.shape, a.dtype)
                                                                           for a in swaps],
        in_specs=[pl.BlockSpec(memory_space=pltpu.VMEM)] + [any_spec] * n_swaps,
        out_specs=[pl.BlockSpec(memory_space=pltpu.VMEM)] + [any_spec] * n_swaps,
        scratch_shapes=[pltpu.SemaphoreType.DMA((7,)), pltpu.SemaphoreType.DMA((7,)), pltpu.SemaphoreType.DMA]
        + ([pltpu.SemaphoreType.DMA((n_swaps,)), pltpu.SemaphoreType.DMA((n_swaps,))] if swaps else []),
        compiler_params=_params(),
    )(blk, *swaps)
    return (outs[0], outs[1:]) if swaps else outs[0]


HBM_SPEC = pl.BlockSpec(memory_space=pltpu.HBM)
SEM_SPEC = pl.BlockSpec(memory_space=pltpu.SEMAPHORE)
DATAFLOW = pltpu.SideEffectType.DATAFLOW_SIDE_EFFECTING


def _chip_copies(src_ref, land_ref, send_sems, recv_sems, scatter):
    x, y, c = _coords()
    my = 2 * x + y
    outgoing, incoming = [], []
    for k, (px, py) in enumerate(_other_chips(x, y)):
        peer = 2 * px + py

        def copy(src_slot, dst_slot):
            return pltpu.make_async_remote_copy(
                src_ref=src_ref.at[src_slot] if scatter else src_ref, dst_ref=land_ref.at[dst_slot],
                send_sem=send_sems.at[k], recv_sem=recv_sems.at[k], device_id=(px, py, c), device_id_type=MESH)

        outgoing.append(copy(peer, my))
        incoming.append(copy(my, peer))
    return outgoing, incoming


def _exchange_start(srcs, *, name, scatter):
    n = len(srcs)
    land_shapes = [src.shape if scatter else (N_CHIPS,) + src.shape for src in srcs]

    def body(*refs):
        for k in range(n):
            send_sems, recv_sems = refs[2 * n + 4 * k], refs[2 * n + 4 * k + 1]
            outgoing, _ = _chip_copies(refs[k], refs[n + k], send_sems, recv_sems, scatter)
            for cp in outgoing:
                cp.start()
        refs[-1][...] = jnp.zeros_like(refs[-1])

    out_shape, out_specs, aliases = [], [], {}
    for k, (src, land_shape) in enumerate(zip(srcs, land_shapes)):
        out_shape += [pltpu.SemaphoreType.DMA((N_CHIPS - 1,)), pltpu.SemaphoreType.DMA((N_CHIPS - 1,)),
                      pltpu.HBM(src.shape, src.dtype), pltpu.HBM(land_shape, src.dtype)]
        out_specs += [SEM_SPEC, SEM_SPEC, HBM_SPEC, HBM_SPEC]
        aliases.update({k: 4 * k + 2, n + k: 4 * k + 3})
    outs = pl.pallas_call(
        body, name=name,
        out_shape=tuple(out_shape) + (jax.ShapeDtypeStruct((8, LANES), F32),),
        in_specs=(HBM_SPEC,) * (2 * n),
        out_specs=tuple(out_specs) + (pl.BlockSpec(memory_space=pltpu.VMEM),),
        input_output_aliases=aliases,
        compiler_params=pltpu.CompilerParams(has_side_effects=DATAFLOW),
    )(*[pltpu.with_memory_space_constraint(src, pltpu.HBM) for src in srcs],
      *[pltpu.with_memory_space_constraint(lax.empty(shape, src.dtype), pltpu.HBM)
        for src, shape in zip(srcs, land_shapes)])
    return [tuple(outs[4 * k:4 * k + 4]) for k in range(n)], outs[-1]


def _exchange_wait(started, after, *, name, scatter):
    send_sems, recv_sems, src_thru, land_thru = started

    def body(src_ref, land_ref, send_sems, recv_sems, after_ref, src_dead, got_ref):
        outgoing, incoming = _chip_copies(src_ref, land_ref, send_sems, recv_sems, scatter)
        for cp in outgoing:
            cp.wait_send()
        for cp in incoming:
            cp.wait_recv()

    return pl.pallas_call(
        body, name=name,
        out_shape=(pltpu.HBM(src_thru.shape, src_thru.dtype), pltpu.HBM(land_thru.shape, land_thru.dtype)),
        in_specs=(HBM_SPEC, HBM_SPEC, SEM_SPEC, SEM_SPEC, pl.BlockSpec(memory_space=pl.ANY)),
        out_specs=(HBM_SPEC, HBM_SPEC),
        input_output_aliases={0: 0, 1: 1},
        compiler_params=pltpu.CompilerParams(has_side_effects=DATAFLOW),
    )(src_thru, land_thru, send_sems, recv_sems, after)


def _pack_rows(a):
    return a.reshape(-1, D_MODEL)


def _pad_heads(w, width):
    r = w.shape[0]
    return jnp.pad(w, ((0, 0), (0, 0), (0, HEAD_PAD - width))).reshape(r, N_HEADS * HEAD_PAD)


MIX_NAMES = ("w_uq", "w_uk", "w_uv", "p_pool", "p_attn", "w_out")
GROUPS = ("in", "mix", "ff1", "ff2")


def _local_shard(weights, l, group, zero):
    if group == "mix":
        shard = jnp.concatenate([_pack_rows(weights[n][l]) for n in MIX_NAMES], axis=0)
    else:
        shard = weights[{"in": "w_in", "ff1": "w_ff1", "ff2": "w_ff2"}[group]][l]
    return (shard + zero).astype(BF16)


def _unpack_weights(gathered, group):
    def cols(a, k):
        return a.reshape(N_CHIPS, k, -1).transpose(1, 0, 2).reshape(k, -1)

    if group == "in":
        full = gathered.reshape(W_IN_COLS, D_MODEL)
        u, cq, ckv, kr, gates = (full[a:b] for a, b in (W_IN_U, W_IN_CQ, W_IN_CKV, W_IN_KR, W_IN_GATES))
        kr = jnp.pad(kr, ((QK_NOPE, HEAD_PAD - QK_DIM), (0, 0)))
        return dict(w_in=jnp.concatenate([cq, kr, u, gates, ckv], axis=0))
    if group == "ff1":
        return dict(w_ff1=gathered)
    if group == "ff2":
        return dict(w_ff2=gathered.reshape(D_FF, D_MODEL))

    def p_attn(a):
        full = cols(a, ATTN_DIM).reshape(N_HEADS, V_DIM, D_MODEL)
        return jnp.pad(full, ((0, 0), (0, HEAD_PAD - V_DIM), (0, 0))).reshape(N_HEADS * HEAD_PAD, D_MODEL)

    build = dict(
        w_uq=lambda a: _pad_heads(a.reshape(Q_LORA, N_HEADS, QK_DIM), QK_DIM),
        w_uk=lambda a: _pad_heads(a.reshape(KV_LORA, N_HEADS, QK_NOPE), QK_NOPE),
        w_uv=lambda a: _pad_heads(a.reshape(KV_LORA, N_HEADS, V_DIM), V_DIM),
        p_pool=lambda a: cols(a, POOL_DIM),
        p_attn=p_attn,
        w_out=lambda a: a.reshape(D_MODEL, D_MODEL),
    )
    w, off = {}, 0
    for name in MIX_NAMES:
        w[name] = build[name](gathered[:, off:off + ROWS_OF[name]])
        off += ROWS_OF[name]
    return w


def _pack_grads(g, group):
    def cols(a):
        k = a.shape[0]
        return a.reshape(k, N_CHIPS, -1).transpose(1, 0, 2).reshape(N_CHIPS, -1, D_MODEL)

    def rows(a):
        return a.reshape(N_CHIPS, -1, D_MODEL)

    def heads(width):
        return lambda a: rows(a.reshape(a.shape[0], N_HEADS, HEAD_PAD)[:, :, :width])

    if group == "in":
        full = jnp.concatenate([g["u"], g["cq"], g["ckv"], g["kr"][QK_NOPE:QK_DIM], g["ga"], g["gb"]], axis=0)
        return full.reshape(N_CHIPS, W_IN_SHARD, D_MODEL)
    if group == "ff1":
        return g["w_ff1"]
    if group == "ff2":
        return g["w_ff2"].reshape(N_CHIPS, D_FF // N_CHIPS, D_MODEL)

    def p_attn(a):
        return cols(a.reshape(N_HEADS, HEAD_PAD, D_MODEL)[:, :V_DIM].reshape(ATTN_DIM, D_MODEL))

    build = dict(w_uq=heads(QK_DIM), w_uk=heads(QK_NOPE), w_uv=heads(V_DIM), p_pool=cols, p_attn=p_attn, w_out=rows)
    return jnp.concatenate([build[name](g[name]) for name in MIX_NAMES], axis=1)


def _per_head(fn, acc, *tables):
    return jnp.concatenate([fn(acc[:, h * HEAD_PAD:(h + 1) * HEAD_PAD], *tables) for h in range(N_HEADS)], axis=1)


def _rope_head(a, cos, sin):
    lane = lax.broadcasted_iota(jnp.int32, a.shape, 1)
    return a * (cos + jnp.where(lane < QK_NOPE, 1.0, 0.0)) + _rotate_half(a) * sin


def _layer_fwd(l, x, mod, get_weights, small, cos_t, sin_t):
    sh1, sc1, g1, sh2, sc2, g2 = mod
    tag = f"_l{l}"
    h, r1 = _norm_mod(x, small["ln1_g"], sc1, sh1, name="norm1" + tag)
    w = dict(get_weights("in", h))
    (z,) = _mm(h, w["w_in"], tb=True, name="in_proj" + tag, out_dtypes=(BF16,))
    p, yp, cq, ckv, kr, rq, rkv = _mixer_pre(z, cos_t, sin_t, small["w_pool"], small["pool_scale"],
                                              small["q_norm_g"], small["kv_norm_g"], name="mixer_pre" + tag)
    w.update(get_weights("mix", yp))
    (ya,) = _mm(yp, w["p_pool"], name="pool_out" + tag, out_dtypes=(BF16,))
    (q,) = _mm(cq, w["w_uq"], name="q_proj" + tag, out_dtypes=(BF16,),
               epilogue=lambda acc, cos, sin: (_per_head(_rope_head, acc, cos, sin),),
               extras=((cos_t, "table"), (sin_t, "table")))
    (k,) = _mm(ckv, w["w_uk"], name="k_proj" + tag, out_dtypes=(BF16,),
               epilogue=lambda acc, krv: (_per_head(lambda a, b: a + b, acc, krv),), extras=((kr, "table"),))
    (v,) = _mm(ckv, w["w_uv"], name="v_proj" + tag, out_dtypes=(BF16,))
    o, lse = _attn_fwd(q, k, v, name="attn_fwd" + tag)
    yb, merged = _mm(o, w["p_attn"], name="attn_out" + tag, out_dtypes=(BF16, BF16),
                     epilogue=lambda acc, ga, gb, yav: (acc, _sigmoid(ga) * yav + _sigmoid(gb) * acc),
                     extras=((z, ("tile", ZC_GA // D_MODEL)), (z, ("tile", ZC_GB // D_MODEL)), (ya, "tile")))
    mo, x1 = _mm(merged, w["w_out"], name="mix_out" + tag, out_dtypes=(BF16, F32),
                 epilogue=lambda acc, xr, g: (acc, xr + g * acc), extras=((x, "tile"), (g1, "row")))
    h2, r2 = _norm_mod(x1, small["ln2_g"], sc2, sh2, name="norm2" + tag)
    w.update(get_weights("ff1", merged))
    f, act = _mm(h2, w["w_ff1"], b_stack=True, name="ff1" + tag, out_dtypes=(BF16, BF16),
                 epilogue=lambda acc: (acc, jnp.square(jnp.maximum(acc, 0.0))))
    w.update(get_weights("ff2", act))
    m2, x2 = _mm(act, w["w_ff2"], name="ff2" + tag, out_dtypes=(BF16, F32),
                 epilogue=lambda acc, xr, g: (acc, xr + g * acc), extras=((x1, "tile"), (g2, "row")))
    saved = dict(x=x, h=h, r1=r1, z=z, p=p, yp=yp, cq=cq, ckv=ckv, rq=rq, rkv=rkv, ya=ya, q=q, k=k, v=v, o=o, lse=lse,
                 yb=yb, merged=merged, mo=mo, x1=x1, h2=h2, r2=r2, f=f, act=act, m2=m2)
    return x2, saved, w


def _merge_grads(dm, ga, gb, ya, yb):
    sa, sb = _sigmoid(ga), _sigmoid(gb)
    return dm * sa, dm * sb, dm * ya * (sa * (1.0 - sa)), dm * yb * (sb * (1.0 - sb))


def _layer_bwd(l, dx2, dm2, dg2, sv, mod, w, small, cos_t, sin_t, send_grads, gate_below):
    sh1, sc1, g1, sh2, sc2, g2 = mod
    tag = f"_l{l}"
    gw = {}
    (df,) = _mm(dm2, w["w_ff2"], tb=True, name="ff2_dx" + tag, out_dtypes=(BF16,),
                epilogue=lambda acc, f: (acc * (2.0 * jnp.maximum(f, 0.0)),), extras=((sv["f"], "tile"),))
    (g_ff2,) = _mm(sv["act"], dm2, ta=True, name="ff2_dw" + tag, out_dtypes=(BF16,))
    (g_ff1,) = _mm(sv["h2"], df, ta=True, out_stack=N_CHIPS, name="ff1_dw" + tag, out_dtypes=(BF16,))
    sc2 = sc2 + send_grads("ff2", dict(w_ff2=g_ff2)) + send_grads("ff1", dict(w_ff1=g_ff1))
    (dh2,) = _mm(df, w["w_ff1"], tb=True, b_stack=True, name="ff1_dx" + tag)
    dx1, dln2, dsc2, dsh2, dmo, dg1 = _norm_mod_bwd(dh2, sv["x1"], sv["r2"], small["ln2_g"], sc2, dx2,
                                                    gate=(sv["mo"], g1), name="norm2_bwd" + tag)
    dya, dyb, dga, dgb = _mm(dmo, w["w_out"], tb=True, name="mix_out_dx" + tag, out_dtypes=(BF16,) * 4, tm=512,
                             epilogue=_merge_grads,
                             extras=((sv["z"], ("tile", ZC_GA // D_MODEL)), (sv["z"], ("tile", ZC_GB // D_MODEL)),
                                     (sv["ya"], "tile"), (sv["yb"], "tile")))
    (gw["w_out"],) = _mm(sv["merged"], dmo, ta=True, name="mix_out_dw" + tag, out_dtypes=(BF16,))
    (gw["p_pool"],) = _mm(sv["yp"], dya, ta=True, name="pool_out_dw" + tag, out_dtypes=(BF16,))
    (dyp,) = _mm(dya, w["p_pool"], tb=True, name="pool_out_dx" + tag)
    du, g_w_pool, g_pool_scale = _pool_bwd(dyp, sv["p"], small["w_pool"], small["pool_scale"], name="pool_bwd" + tag)
    (gw["p_attn"],) = _mm(sv["o"], dyb, ta=True, name="attn_out_dw" + tag, out_dtypes=(BF16,))
    (do,) = _mm(dyb, w["p_attn"], tb=True, name="attn_out_dx" + tag, out_dtypes=(BF16,))
    dql, dkb, dv, dk_sums = _attn_bwd(sv["q"], sv["k"], sv["v"], do, sv["o"], sv["lse"], cos_t, sin_t,
                                      name="attn_bwd" + tag)
    dkr = _key_bwd(dk_sums, cos_t, sin_t, name="key_bwd" + tag)
    (gw["w_uq"],) = _mm(sv["cq"], dql, ta=True, name="q_proj_dw" + tag, out_dtypes=(BF16,))
    (gw["w_uk"],) = _mm(sv["ckv"], dkb, ta=True, name="k_proj_dw" + tag, out_dtypes=(BF16,))
    (gw["w_uv"],) = _mm(sv["ckv"], dv, ta=True, name="v_proj_dw" + tag, out_dtypes=(BF16,))
    (dcq,) = _mm(dql, w["w_uq"], tb=True, name="q_proj_dx" + tag)
    (dckv,) = _mm(dkb, w["w_uk"], tb=True, second=(dv, w["w_uv"]), name="kv_proj_dx" + tag)
    q_norm_g = small["q_norm_g"] + send_grads("mix", gw)
    dcq_raw, g_qn = _rms_bwd(dcq, sv["z"], ZC_CQ, sv["rq"], q_norm_g, name="q_norm_bwd" + tag)
    dckv_raw, g_kvn = _rms_bwd(dckv, sv["z"], ZC_CKV, sv["rkv"], small["kv_norm_g"], name="kv_norm_bwd" + tag)
    dz = dict(cq=dcq_raw, kr=dkr, u=du, ga=dga, gb=dgb, ckv=dckv_raw)
    g_in = {n: _mm(piece, sv["h"], ta=True, name=f"in_proj_dw_{n}" + tag, out_dtypes=(BF16,))[0]
            for n, piece in dz.items()}
    sc1 = sc1 + send_grads("in", g_in)
    dh = _mm_sum(list(dz.values()), w["w_in"], [Z_OFFSETS[n] for n in dz], name="in_proj_dx" + tag)
    dx, dln1, dsc1, dsh1, *below = _norm_mod_bwd(dh, sv["x"], sv["r1"], small["ln1_g"], sc1, dx1, gate=gate_below,
                                                 name="norm1_bwd" + tag)
    dmod = jnp.concatenate([dsh1, dsc1, dg1, dsh2, dsc2, dg2], axis=0)
    gsmall = dict(ln1_g=dln1, ln2_g=dln2, q_norm_g=g_qn, kv_norm_g=g_kvn, w_pool=g_w_pool, pool_scale=g_pool_scale)
    return dx, dmod, gsmall, below


SMALL_LOSS = 6
SMALL_SINGLES = 16
SMALL_POOL = 24
SMALL_POOL_ROWS = len(POOL_WINDOWS) * POOL_GROUP * POOL_GROUP // D_MODEL
SMALL_ROWS = SMALL_POOL + DEPTH * SMALL_POOL_ROWS


def _pack_small(parts, *, name):
    def body(*refs):
        out_ref = refs[-1]
        out_ref[...] = jnp.zeros_like(out_ref)
        for ref, (_, row) in zip(refs[:-1], parts):
            out_ref[row:row + ref.shape[0], :] = ref[...]

    return pl.pallas_call(body, name=name, out_shape=jax.ShapeDtypeStruct((SMALL_ROWS, D_MODEL), F32),
                          compiler_params=_params())(*[a for a, _ in parts])


def kernel(x, c, positions, ln1_g, ln2_g, w_ada, b_ada, w_in, q_norm_g, w_uq, kv_norm_g, w_uk, w_uv, w_pool, pool_scale, p_pool, p_attn, w_out, w_ff1, w_ff2, final_g, loss_target, m_ln1_g, m_ln2_g, m_w_ada, m_b_ada, m_w_in, m_q_norm_g, m_w_uq, m_kv_norm_g, m_w_uk, m_w_uv, m_w_pool, m_pool_scale, m_p_pool, m_p_attn, m_w_out, m_w_ff1, m_w_ff2, m_final_g, v_ln1_g, v_ln2_g, v_w_ada, v_b_ada, v_w_in, v_q_norm_g, v_w_uq, v_kv_norm_g, v_w_uk, v_w_uv, v_w_pool, v_pool_scale, v_p_pool, v_p_attn, v_w_out, v_w_ff1, v_w_ff2, v_final_g):
    weights = dict(ln1_g=ln1_g, ln2_g=ln2_g, w_ada=w_ada, b_ada=b_ada, w_in=w_in, q_norm_g=q_norm_g, w_uq=w_uq,
                   kv_norm_g=kv_norm_g, w_uk=w_uk, w_uv=w_uv, w_pool=w_pool, pool_scale=pool_scale, p_pool=p_pool,
                   p_attn=p_attn, w_out=w_out, w_ff1=w_ff1, w_ff2=w_ff2, final_g=final_g)
    moms = dict(ln1_g=m_ln1_g, ln2_g=m_ln2_g, w_ada=m_w_ada, b_ada=m_b_ada, w_in=m_w_in, q_norm_g=m_q_norm_g,
                w_uq=m_w_uq, kv_norm_g=m_kv_norm_g, w_uk=m_w_uk, w_uv=m_w_uv, w_pool=m_w_pool,
                pool_scale=m_pool_scale, p_pool=m_p_pool, p_attn=m_p_attn, w_out=m_w_out, w_ff1=m_w_ff1,
                w_ff2=m_w_ff2, final_g=m_final_g)
    vels = dict(ln1_g=v_ln1_g, ln2_g=v_ln2_g, w_ada=v_w_ada, b_ada=v_b_ada, w_in=v_w_in, q_norm_g=v_q_norm_g,
                w_uq=v_w_uq, kv_norm_g=v_kv_norm_g, w_uk=v_w_uk, w_uv=v_w_uv, w_pool=v_w_pool,
                pool_scale=v_pool_scale, p_pool=v_p_pool, p_attn=v_p_attn, w_out=v_w_out, w_ff1=v_w_ff1,
                w_ff2=v_w_ff2, final_g=v_final_g)
    order = list(weights)
    for table in (weights, moms, vels):
        table["w_in"] = jnp.swapaxes(table["w_in"], 1, 2)
    seq = x.shape[1]
    my_chip = 2 * lax.axis_index("x") + lax.axis_index("y")
    my_dev = 2 * my_chip + lax.axis_index("c")
    ada_cols = w_ada.shape[2]

    small = [dict(ln1_g=ln1_g[l:l + 1], ln2_g=ln2_g[l:l + 1], q_norm_g=q_norm_g[l:l + 1], kv_norm_g=kv_norm_g[l:l + 1],
                  w_pool=w_pool[l], pool_scale=pool_scale[l:l + 1]) for l in range(DEPTH)]

    b_mine = lax.dynamic_slice_in_dim(b_ada, my_chip * ada_cols, ada_cols, axis=1).reshape(1, DEPTH * ada_cols)
    c_act, mod_all = _ada_modulation(jnp.pad(c, ((0, 7), (0, 0))), w_ada, b_mine, name="ada_modulation")
    mod_all = mod_all.reshape(N_DEV, DEPTH, N_DEV, ada_cols)

    zero = mod_all[0, 0, 0, 0] * 0.0
    keys = [(l, group) for l in range(DEPTH) for group in GROUPS]
    exchanges, token = _exchange_start([_local_shard(weights, l, group, zero) for l, group in keys],
                                       name="weights_send", scatter=False)
    started = dict(zip(keys, exchanges))
    pin = token[0:1, 0:1]

    def gathered_weights(l, group, after):
        mine, land = _exchange_wait(started[l, group], after, name=f"weights_wait_l{l}_{group}", scatter=False)
        land = lax.dynamic_update_slice_in_dim(land, mine[None], my_chip, axis=0)
        return _unpack_weights(land, group)

    mods = []
    for l in range(DEPTH):
        row = jnp.concatenate([lax.dynamic_index_in_dim(mod_all[2 * j, l], my_dev, axis=0, keepdims=True)
                               for j in range(N_CHIPS)], axis=1) + pin
        mods.append([row[:, i * D_MODEL:(i + 1) * D_MODEL] for i in range(N_MOD)])

    inv_freq = ROPE_THETA ** (-jnp.arange(0, QK_ROPE, 2, dtype=F32) / QK_ROPE)
    freq_lanes = jnp.concatenate([jnp.zeros((QK_NOPE,), F32), inv_freq, inv_freq,
                                  jnp.zeros((HEAD_PAD - QK_DIM,), F32)]).reshape(1, LANES)
    cos_t, sin_t = _rope_tables(positions.reshape(seq, 1), freq_lanes, name="rope_tables")

    xs, saved, wl = x.reshape(seq, D_MODEL), [], []
    for l in range(DEPTH):
        xs, sv, w_l = _layer_fwd(l, xs, mods[l], functools.partial(gathered_weights, l), small[l], cos_t, sin_t)
        saved.append(sv)
        wl.append(w_l)
    dx, loss_part, g_final, dm2, dg2 = _final_loss(xs, final_g.reshape(1, D_MODEL), loss_target.reshape(seq, D_MODEL),
                                                   saved[-1]["m2"], mods[-1][5], name="final_loss")

    sent, pending = [], []
    send_after = {(0, "ff1"), (0, "mix"), (0, "in")}

    def send_grads(l, group, g):
        pending.append((l, group, _pack_grads(g, group)))
        if (l, group) not in send_after:
            return jnp.zeros((1, 1), F32)
        exchanges, token_g = _exchange_start([gpack for _, _, gpack in pending], name=f"grads_send_l{l}_{group}",
                                             scatter=True)
        sent.extend((item[0], item[1], exchange) for item, exchange in zip(pending, exchanges))
        pending.clear()
        return token_g[0:1, 0:1]

    dmod, gsmall = [None] * DEPTH, [None] * DEPTH
    for l in reversed(range(DEPTH)):
        gate_below = (saved[l - 1]["m2"], mods[l - 1][5]) if l > 0 else None
        dx, dmod[l], gsmall[l], below = _layer_bwd(l, dx, dm2, dg2, saved[l], mods[l], wl[l], small[l], cos_t, sin_t,
                                                   functools.partial(send_grads, l), gate_below)
        dm2, dg2 = below if below else (None, None)
    grads = dict(x=dx.reshape(1, seq, D_MODEL))

    big_parts, after = [], dmod[0]
    for l, group, started_g in sent:
        tg = f"_l{l}_{group}"
        gpack, land = _exchange_wait(started_g, after, name="grads_wait" + tg, scatter=True)
        own = lax.dynamic_index_in_dim(gpack, my_chip, axis=0, keepdims=True)
        land = lax.dynamic_update_slice_in_dim(land, own, my_chip, axis=0)
        big_parts.append(_sum_slots(land, N_CHIPS, name="grads_sum_chips" + tg))
        after = big_parts[-1]

    def lanes(a):
        flat = a.reshape(1, -1)
        return jnp.pad(flat, ((0, 0), (0, D_MODEL - flat.shape[1])))

    singles = [gsmall[0]["ln1_g"], gsmall[1]["ln1_g"], gsmall[0]["ln2_g"], gsmall[1]["ln2_g"], g_final,
               lanes(jnp.concatenate([gsmall[l]["pool_scale"] for l in range(DEPTH)], axis=1)),
               lanes(jnp.concatenate([gsmall[l]["q_norm_g"] for l in range(DEPTH)], axis=1)),
               lanes(jnp.concatenate([gsmall[l]["kv_norm_g"] for l in range(DEPTH)], axis=1))]
    parts = [(dmod[0], 0), (lanes(loss_part), SMALL_LOSS), (dmod[1], 8)]
    parts += [(a, SMALL_SINGLES + i) for i, a in enumerate(singles)]
    parts += [(gsmall[l]["w_pool"].reshape(-1, D_MODEL), SMALL_POOL + l * SMALL_POOL_ROWS) for l in range(DEPTH)]
    small_all, big_others = _all_gather_small(_pack_small(parts, name="small_grads_pack"), swaps=big_parts,
                                              name="small_grads_all_gather")
    small_all = small_all.reshape(N_DEV, SMALL_ROWS, D_MODEL)
    ssum = _sum_slots(small_all, N_DEV, name="small_grads_sum")
    loss = ssum[SMALL_LOSS, 0]
    grads["b_ada"] = jnp.stack([ssum[8 * l:8 * l + N_MOD] for l in range(DEPTH)]).reshape(DEPTH, N_MOD * D_MODEL)
    grads["ln1_g"] = ssum[SMALL_SINGLES:SMALL_SINGLES + 2]
    grads["ln2_g"] = ssum[SMALL_SINGLES + 2:SMALL_SINGLES + 4]
    grads["final_g"] = ssum[SMALL_SINGLES + 4]
    grads["pool_scale"] = ssum[SMALL_SINGLES + 5].reshape(DEPTH, POOL_DIM)
    grads["q_norm_g"] = ssum[SMALL_SINGLES + 6, :DEPTH * Q_LORA].reshape(DEPTH, Q_LORA)
    grads["kv_norm_g"] = ssum[SMALL_SINGLES + 7, :DEPTH * KV_LORA].reshape(DEPTH, KV_LORA)
    grads["w_pool"] = ssum[SMALL_POOL:SMALL_ROWS].reshape(w_pool.shape)

    mix_sum, halves = None, {group: [None] * DEPTH for group in ("in", "ff1", "ff2")}
    for (l, group, _), part, other in zip(sent, big_parts, big_others):
        if group == "mix":
            mix_sum = _add2_stacked(part, other, mix_sum, l, name=f"grads_sum_cores_l{l}_mix")
        else:
            halves[group][l] = (part, other)
    off = 0
    for name in MIX_NAMES:
        grads[name] = mix_sum[:, off:off + ROWS_OF[name]].reshape(weights[name].shape)
        off += ROWS_OF[name]

    c_act_t = jnp.pad(c_act.T, ((0, 0), (0, LANES - N_DEV)))
    d_mine = []
    for l in range(DEPTH):
        d_all = small_all[:, 8 * l:8 * l + N_MOD].reshape(N_DEV, N_MOD * D_MODEL)
        d_mine.append(lax.dynamic_slice_in_dim(d_all, my_chip * ada_cols, ada_cols, axis=1))
    d_cat = jnp.pad(jnp.concatenate(d_mine, axis=1), ((0, LANES - N_DEV), (0, 0)))
    (grads["w_ada"],) = _mm(c_act_t, d_cat, out_stack=DEPTH, name="ada_dw")

    def view(a):
        return a.reshape(1, -1) if a.ndim == 1 else a if a.ndim == 3 else a.reshape(-1, a.shape[-1])

    delta, new_m, new_v = {}, {}, {}
    for name in order:
        shape = weights[name].shape
        group = {"w_in": "in", "w_ff1": "ff1", "w_ff2": "ff2"}.get(name)
        if group:
            grads[name], d, nm, nv = _adamw_summed(weights[name], halves[group], moms[name], vels[name],
                                                   name="adamw_" + name)
        else:
            d, nm, nv = _adamw(view(weights[name]), view(grads[name]), view(moms[name]), view(vels[name]),
                               name="adamw_" + name)
        delta[name], new_m[name], new_v[name] = d.reshape(shape), nm.reshape(shape), nv.reshape(shape)
    for table in (grads, delta, new_m, new_v):
        table["w_in"] = jnp.swapaxes(table["w_in"], 1, 2)
    return (loss, grads["x"], *[grads[n] for n in order], *[delta[n] for n in order],
            *[new_m[n] for n in order], *[new_v[n] for n in order])
```
